```python
import math
import jax, jax.numpy as jnp
from jax import lax
import numpy as np

D_MODEL = 1024
BATCH = 32
SEQ = 2048
DEPTH = 1

CHUNK = 64
MIX_WIDTH = D_MODEL
RET_HEADS = 4
RET_VWIDTH = MIX_WIDTH // 2
RET_DV = RET_VWIDTH // RET_HEADS
RET_DK = RET_DV // 2
RET_QKWIDTH = RET_HEADS * RET_DK
POOL_WIDTH = MIX_WIDTH - RET_VWIDTH
POOL_WINDOWS = (2, 4, 8, 16)
POOL_GROUPS = len(POOL_WINDOWS)
POOL_GC = POOL_WIDTH // POOL_GROUPS
IN_WIDTH = 2 * RET_QKWIDTH + 2 * RET_VWIDTH + POOL_WIDTH
D_FF = 2816
ROPE_BASE = 10000.0
RMS_EPS = 1e-6
GN_EPS = 1e-5

kernel_name = "hybrid_retention_multiscale_pool_macaron"


def rms_norm(x, g):
    xf = x.astype(jnp.float32)
    y = xf * lax.rsqrt(jnp.mean(xf * xf, axis=-1, keepdims=True) + RMS_EPS)
    return (y * g.astype(jnp.float32)).astype(x.dtype)


def swiglu(x, w_gate, w_up, w_down):
    return (jax.nn.silu(x @ w_gate) * (x @ w_up)) @ w_down


def rotary(x, pos):
    d = x.shape[-1]
    half = d // 2
    freqs = ROPE_BASE ** (-jnp.arange(half, dtype=jnp.float32) * 2.0 / d)
    ang = pos.astype(jnp.float32)[:, None] * freqs[None, :]
    cos = jnp.cos(ang)[None, :, None, :].astype(x.dtype)
    sin = jnp.sin(ang)[None, :, None, :].astype(x.dtype)
    x1, x2 = x[..., :half], x[..., half:]
    return jnp.concatenate([x1 * cos - x2 * sin, x1 * sin + x2 * cos], axis=-1)


def retention_chunkwise(q, k, v):
    b, s, h, dk = q.shape
    dv = v.shape[-1]
    n = s // CHUNK
    dt = q.dtype
    gamma = 1.0 - 2.0 ** (-5.0 - jnp.arange(h, dtype=jnp.float32))
    log_g = jnp.log(gamma)
    idx = jnp.arange(CHUNK, dtype=jnp.float32)
    d_intra = jnp.exp(log_g[:, None, None] * jnp.abs(idx[:, None] - idx[None, :])).astype(dt)
    d_key = jnp.exp(log_g[:, None] * (CHUNK - 1.0 - idx)[None, :]).astype(dt)
    d_query = jnp.exp(log_g[:, None] * (idx + 1.0)[None, :]).astype(dt)
    d_chunk = jnp.exp(log_g * CHUNK).astype(dt)

    qc = (q * (dk ** -0.5)).reshape(b, n, CHUNK, h, dk)
    kc = k.reshape(b, n, CHUNK, h, dk)
    vc = v.reshape(b, n, CHUNK, h, dv)

    scores = jnp.einsum('bnihd,bnjhd->bnhij', qc, kc) * d_intra[None, None]
    intra = jnp.einsum('bnhij,bnjhe->bnihe', scores, vc)

    kv = jnp.einsum('bnjhd,hj,bnjhe->nbhde', kc, d_key, vc)

    def step(state, kv_n):
        return state * d_chunk[None, :, None, None] + kv_n, state

    init = jnp.zeros((b, h, dk, dv), dtype=kv.dtype)
    _, s_prev = lax.scan(step, init, kv)
    cross = jnp.einsum('bnihd,nbhde,hi->bnihe', qc, s_prev, d_query)
    return (intra + cross).reshape(b, s, h, dv)


def head_group_norm(o, gain):
    b, s, h, dv = o.shape
    of = o.astype(jnp.float32)
    mu = jnp.mean(of, axis=-1, keepdims=True)
    var = jnp.mean(jnp.square(of - mu), axis=-1, keepdims=True)
    y = ((of - mu) * lax.rsqrt(var + GN_EPS)).reshape(b, s, h * dv)
    return (y * gain.astype(jnp.float32)).astype(o.dtype)


def multiscale_pool(u, w_pool, scale):
    b, s, _ = u.shape
    ug = u.reshape(b, s, POOL_GROUPS, POOL_GC)
    cs = jnp.cumsum(ug.astype(jnp.float32), axis=1)
    cs = jnp.concatenate([jnp.zeros((b, 1, POOL_GROUPS, POOL_GC), jnp.float32), cs], axis=1)
    t = jnp.arange(s)
    win = jnp.array(POOL_WINDOWS, dtype=jnp.int32)
    lo = jnp.maximum(t[:, None] + 1 - win[None, :], 0)
    cnt = (t[:, None] + 1 - lo).astype(jnp.float32)
    cs_lo = cs[:, lo, jnp.arange(POOL_GROUPS)[None, :], :]
    mean = (cs[:, 1:] - cs_lo) / cnt[None, :, :, None]
    pooled = mean.astype(u.dtype) - ug
    y = jnp.einsum('bsgc,gcd->bsgd', pooled, w_pool).reshape(b, s, POOL_WIDTH)
    return y * scale


def hybrid_mixer(h, w_in, w_out, ret_gn_gain, pool_w, pool_scale):
    b, s, _ = h.shape
    p = h @ w_in
    o1 = RET_QKWIDTH
    o2 = o1 + RET_QKWIDTH
    o3 = o2 + RET_VWIDTH
    o4 = o3 + RET_VWIDTH
    q = p[..., :o1].reshape(b, s, RET_HEADS, RET_DK)
    k = p[..., o1:o2].reshape(b, s, RET_HEADS, RET_DK)
    v = p[..., o2:o3].reshape(b, s, RET_HEADS, RET_DV)
    g = p[..., o3:o4]
    u = p[..., o4:]
    pos = jnp.arange(s)
    q = rotary(q, pos)
    k = rotary(k, pos)
    ret = retention_chunkwise(q, k, v)
    ret = jax.nn.silu(g) * head_group_norm(ret, ret_gn_gain)
    pool = multiscale_pool(u, pool_w, pool_scale)
    return jnp.concatenate([ret, pool], axis=-1) @ w_out


def _fwd_setup_inputs(seed: int = 0) -> dict:
    key = jax.random.key(seed)
    ks = jax.random.split(key, 20)
    f32 = jnp.float32
    nrm = lambda k_, shape, fan: jax.random.normal(k_, shape, f32) * (fan ** -0.5)
    gain = lambda k_, shape: 1.0 + 0.02 * jax.random.normal(k_, shape, f32)
    L = DEPTH
    return {
        "x": jax.random.normal(ks[0], (BATCH, SEQ, D_MODEL), f32),
        "norm_ffn1": gain(ks[1], (L, D_MODEL)),
        "ffn1_gate": nrm(ks[2], (L, D_MODEL, D_FF), D_MODEL),
        "ffn1_up": nrm(ks[3], (L, D_MODEL, D_FF), D_MODEL),
        "ffn1_down": nrm(ks[4], (L, D_FF, D_MODEL), D_FF),
        "norm_mix": gain(ks[5], (L, D_MODEL)),
        "w_in": nrm(ks[6], (L, D_MODEL, IN_WIDTH), D_MODEL),
        "ret_gn_gain": gain(ks[7], (L, RET_VWIDTH)),
        "pool_w": nrm(ks[8], (L, POOL_GROUPS, POOL_GC, POOL_GC), POOL_GC),
        "pool_scale": gain(ks[9], (L, POOL_WIDTH)),
        "w_out": nrm(ks[10], (L, MIX_WIDTH, D_MODEL), MIX_WIDTH),
        "norm_ffn2": gain(ks[11], (L, D_MODEL)),
        "ffn2_gate": nrm(ks[12], (L, D_MODEL, D_FF), D_MODEL),
        "ffn2_up": nrm(ks[13], (L, D_MODEL, D_FF), D_MODEL),
        "ffn2_down": nrm(ks[14], (L, D_FF, D_MODEL), D_FF),
        "norm_final": gain(ks[15], (D_MODEL,)),
    }


def _fwd_reference(x, norm_ffn1, ffn1_gate, ffn1_up, ffn1_down, norm_mix, w_in,
              ret_gn_gain, pool_w, pool_scale, w_out, norm_ffn2, ffn2_gate,
              ffn2_up, ffn2_down, norm_final):
    for l in range(DEPTH):
        x = x + 0.5 * swiglu(rms_norm(x, norm_ffn1[l]), ffn1_gate[l], ffn1_up[l], ffn1_down[l])
        x = x + hybrid_mixer(rms_norm(x, norm_mix[l]), w_in[l], w_out[l],
                             ret_gn_gain[l], pool_w[l], pool_scale[l])
        x = x + 0.5 * swiglu(rms_norm(x, norm_ffn2[l]), ffn2_gate[l], ffn2_up[l], ffn2_down[l])
    return rms_norm(x, norm_final)


import jax as _jax
import jax.numpy as _jnp

TWIN_FORMAT = 'train_step'
FWD_PARAMS = ['x', 'norm_ffn1', 'ffn1_gate', 'ffn1_up', 'ffn1_down', 'norm_mix', 'w_in', 'ret_gn_gain', 'pool_w', 'pool_scale', 'w_out', 'norm_ffn2', 'ffn2_gate', 'ffn2_up', 'ffn2_down', 'norm_final']
TWIN_WEIGHTS = ['norm_ffn1', 'ffn1_gate', 'ffn1_up', 'ffn1_down', 'norm_mix', 'w_in', 'ret_gn_gain', 'pool_w', 'pool_scale', 'w_out', 'norm_ffn2', 'ffn2_gate', 'ffn2_up', 'ffn2_down', 'norm_final']
TWIN_DIFF_INPUT = 'x'
TWIN_INPUTS = ['x', 'norm_ffn1', 'ffn1_gate', 'ffn1_up', 'ffn1_down', 'norm_mix', 'w_in', 'ret_gn_gain', 'pool_w', 'pool_scale', 'w_out', 'norm_ffn2', 'ffn2_gate', 'ffn2_up', 'ffn2_down', 'norm_final', 'loss_target', 'm_norm_ffn1', 'm_ffn1_gate', 'm_ffn1_up', 'm_ffn1_down', 'm_norm_mix', 'm_w_in', 'm_ret_gn_gain', 'm_pool_w', 'm_pool_scale', 'm_w_out', 'm_norm_ffn2', 'm_ffn2_gate', 'm_ffn2_up', 'm_ffn2_down', 'm_norm_final', 'v_norm_ffn1', 'v_ffn1_gate', 'v_ffn1_up', 'v_ffn1_down', 'v_norm_mix', 'v_w_in', 'v_ret_gn_gain', 'v_pool_w', 'v_pool_scale', 'v_w_out', 'v_norm_ffn2', 'v_ffn2_gate', 'v_ffn2_up', 'v_ffn2_down', 'v_norm_final']
TWIN_OUTPUTS = ['loss', 'grad_x', 'grad_norm_ffn1', 'grad_ffn1_gate', 'grad_ffn1_up', 'grad_ffn1_down', 'grad_norm_mix', 'grad_w_in', 'grad_ret_gn_gain', 'grad_pool_w', 'grad_pool_scale', 'grad_w_out', 'grad_norm_ffn2', 'grad_ffn2_gate', 'grad_ffn2_up', 'grad_ffn2_down', 'grad_norm_final', 'delta_norm_ffn1', 'delta_ffn1_gate', 'delta_ffn1_up', 'delta_ffn1_down', 'delta_norm_mix', 'delta_w_in', 'delta_ret_gn_gain', 'delta_pool_w', 'delta_pool_scale', 'delta_w_out', 'delta_norm_ffn2', 'delta_ffn2_gate', 'delta_ffn2_up', 'delta_ffn2_down', 'delta_norm_final', 'new_m_norm_ffn1', 'new_m_ffn1_gate', 'new_m_ffn1_up', 'new_m_ffn1_down', 'new_m_norm_mix', 'new_m_w_in', 'new_m_ret_gn_gain', 'new_m_pool_w', 'new_m_pool_scale', 'new_m_w_out', 'new_m_norm_ffn2', 'new_m_ffn2_gate', 'new_m_ffn2_up', 'new_m_ffn2_down', 'new_m_norm_final', 'new_v_norm_ffn1', 'new_v_ffn1_gate', 'new_v_ffn1_up', 'new_v_ffn1_down', 'new_v_norm_mix', 'new_v_w_in', 'new_v_ret_gn_gain', 'new_v_pool_w', 'new_v_pool_scale', 'new_v_w_out', 'new_v_norm_ffn2', 'new_v_ffn2_gate', 'new_v_ffn2_up', 'new_v_ffn2_down', 'new_v_norm_final']
TWIN_LEAF_KINDS = {'loss': 'loss', 'grad_x': 'grad_x', 'grad_norm_ffn1': 'grad_w', 'grad_ffn1_gate': 'grad_w', 'grad_ffn1_up': 'grad_w', 'grad_ffn1_down': 'grad_w', 'grad_norm_mix': 'grad_w', 'grad_w_in': 'grad_w', 'grad_ret_gn_gain': 'grad_w', 'grad_pool_w': 'grad_w', 'grad_pool_scale': 'grad_w', 'grad_w_out': 'grad_w', 'grad_norm_ffn2': 'grad_w', 'grad_ffn2_gate': 'grad_w', 'grad_ffn2_up': 'grad_w', 'grad_ffn2_down': 'grad_w', 'grad_norm_final': 'grad_w', 'delta_norm_ffn1': 'delta_w', 'delta_ffn1_gate': 'delta_w', 'delta_ffn1_up': 'delta_w', 'delta_ffn1_down': 'delta_w', 'delta_norm_mix': 'delta_w', 'delta_w_in': 'delta_w', 'delta_ret_gn_gain': 'delta_w', 'delta_pool_w': 'delta_w', 'delta_pool_scale': 'delta_w', 'delta_w_out': 'delta_w', 'delta_norm_ffn2': 'delta_w', 'delta_ffn2_gate': 'delta_w', 'delta_ffn2_up': 'delta_w', 'delta_ffn2_down': 'delta_w', 'delta_norm_final': 'delta_w', 'new_m_norm_ffn1': 'new_m', 'new_m_ffn1_gate': 'new_m', 'new_m_ffn1_up': 'new_m', 'new_m_ffn1_down': 'new_m', 'new_m_norm_mix': 'new_m', 'new_m_w_in': 'new_m', 'new_m_ret_gn_gain': 'new_m', 'new_m_pool_w': 'new_m', 'new_m_pool_scale': 'new_m', 'new_m_w_out': 'new_m', 'new_m_norm_ffn2': 'new_m', 'new_m_ffn2_gate': 'new_m', 'new_m_ffn2_up': 'new_m', 'new_m_ffn2_down': 'new_m', 'new_m_norm_final': 'new_m', 'new_v_norm_ffn1': 'new_v', 'new_v_ffn1_gate': 'new_v', 'new_v_ffn1_up': 'new_v', 'new_v_ffn1_down': 'new_v', 'new_v_norm_mix': 'new_v', 'new_v_w_in': 'new_v', 'new_v_ret_gn_gain': 'new_v', 'new_v_pool_w': 'new_v', 'new_v_pool_scale': 'new_v', 'new_v_w_out': 'new_v', 'new_v_norm_ffn2': 'new_v', 'new_v_ffn2_gate': 'new_v', 'new_v_ffn2_up': 'new_v', 'new_v_ffn2_down': 'new_v', 'new_v_norm_final': 'new_v'}


def _forward(args):
    return _fwd_reference(*[args[k] for k in FWD_PARAMS])


def _output_shape():
    out = _jax.eval_shape(lambda: _forward(_fwd_setup_inputs(0)))
    return out.shape, out.dtype

N_MICROBATCH = 1
ADAM_LR = 0.001
ADAM_B1 = 0.9
ADAM_B2 = 0.999
ADAM_EPS = 1e-08
ADAM_WD = 0.01
ADAM_STEP = 10
PER_EXAMPLE_BATCH_AXIS = {'x': 0, 'loss_target': 0}
SHARED_INPUTS = []
_WEIGHT_DTYPES = {'norm_ffn1': _jnp.float32, 'ffn1_gate': _jnp.float32, 'ffn1_up': _jnp.float32, 'ffn1_down': _jnp.float32, 'norm_mix': _jnp.float32, 'w_in': _jnp.float32, 'ret_gn_gain': _jnp.float32, 'pool_w': _jnp.float32, 'pool_scale': _jnp.float32, 'w_out': _jnp.float32, 'norm_ffn2': _jnp.float32, 'ffn2_gate': _jnp.float32, 'ffn2_up': _jnp.float32, 'ffn2_down': _jnp.float32, 'norm_final': _jnp.float32}
MOMENT_SCALE = {'norm_ffn1': 1.388837e-01, 'ffn1_gate': 5.268181e-02, 'ffn1_up': 5.095401e-02, 'ffn1_down': 8.468882e-02, 'norm_mix': 2.200993e-01, 'w_in': 1.495332e-01, 'ret_gn_gain': 1.209163e-01, 'pool_w': 1.749874e-01, 'pool_scale': 2.197955e-01, 'w_out': 1.510851e-01, 'norm_ffn2': 8.542650e-02, 'ffn2_gate': 3.535541e-02, 'ffn2_up': 3.426810e-02, 'ffn2_down': 5.681174e-02, 'norm_final': 6.376574e+01}


def _to_microbatches(a, axis):
    t = _jnp.moveaxis(a, axis, 0)
    t = t.reshape((N_MICROBATCH, t.shape[0] // N_MICROBATCH) + t.shape[1:])
    return _jnp.moveaxis(t, 1, axis + 1)


def setup_inputs(seed: int = 0) -> dict:
    inp = _fwd_setup_inputs(seed)
    key = _jax.random.fold_in(_jax.random.key(seed), 7919)
    shape, _ = _output_shape()
    out = dict(inp)
    out["loss_target"] = _jax.random.normal(_jax.random.fold_in(key, 0), shape, _jnp.float32)
    for i, name in enumerate(TWIN_WEIGHTS):
        w = inp[name].astype(_jnp.float32)
        if MOMENT_SCALE is None:
            s = _jnp.sqrt(_jnp.mean(_jnp.square(w)) + 1e-30)
        else:
            s = MOMENT_SCALE[name]
        km, kv = _jax.random.split(_jax.random.fold_in(key, i + 1))
        out[name] = w
        out["m_" + name] = s * _jax.random.normal(km, w.shape, _jnp.float32)
        out["v_" + name] = (s * s) * _jax.random.uniform(kv, w.shape, _jnp.float32, 0.5, 1.5)
    if N_MICROBATCH > 1:
        for name, axis in PER_EXAMPLE_BATCH_AXIS.items():
            out[name] = _to_microbatches(out[name], axis)
    return {'x': out['x'], 'norm_ffn1': out['norm_ffn1'], 'ffn1_gate': out['ffn1_gate'], 'ffn1_up': out['ffn1_up'], 'ffn1_down': out['ffn1_down'], 'norm_mix': out['norm_mix'], 'w_in': out['w_in'], 'ret_gn_gain': out['ret_gn_gain'], 'pool_w': out['pool_w'], 'pool_scale': out['pool_scale'], 'w_out': out['w_out'], 'norm_ffn2': out['norm_ffn2'], 'ffn2_gate': out['ffn2_gate'], 'ffn2_up': out['ffn2_up'], 'ffn2_down': out['ffn2_down'], 'norm_final': out['norm_final'], 'loss_target': out['loss_target'], 'm_norm_ffn1': out['m_norm_ffn1'], 'm_ffn1_gate': out['m_ffn1_gate'], 'm_ffn1_up': out['m_ffn1_up'], 'm_ffn1_down': out['m_ffn1_down'], 'm_norm_mix': out['m_norm_mix'], 'm_w_in': out['m_w_in'], 'm_ret_gn_gain': out['m_ret_gn_gain'], 'm_pool_w': out['m_pool_w'], 'm_pool_scale': out['m_pool_scale'], 'm_w_out': out['m_w_out'], 'm_norm_ffn2': out['m_norm_ffn2'], 'm_ffn2_gate': out['m_ffn2_gate'], 'm_ffn2_up': out['m_ffn2_up'], 'm_ffn2_down': out['m_ffn2_down'], 'm_norm_final': out['m_norm_final'], 'v_norm_ffn1': out['v_norm_ffn1'], 'v_ffn1_gate': out['v_ffn1_gate'], 'v_ffn1_up': out['v_ffn1_up'], 'v_ffn1_down': out['v_ffn1_down'], 'v_norm_mix': out['v_norm_mix'], 'v_w_in': out['v_w_in'], 'v_ret_gn_gain': out['v_ret_gn_gain'], 'v_pool_w': out['v_pool_w'], 'v_pool_scale': out['v_pool_scale'], 'v_w_out': out['v_w_out'], 'v_norm_ffn2': out['v_norm_ffn2'], 'v_ffn2_gate': out['v_ffn2_gate'], 'v_ffn2_up': out['v_ffn2_up'], 'v_ffn2_down': out['v_ffn2_down'], 'v_norm_final': out['v_norm_final']}


def _loss(weights, diff, rest, loss_target):
    with _jax.named_scope("forward"):
        args = {**rest, TWIN_DIFF_INPUT: diff, **{k: w.astype(_WEIGHT_DTYPES[k]) for k, w in weights.items()}}
        y = _forward(args)
    with _jax.named_scope("loss_head"):
        err = _jnp.square(y.astype(_jnp.float32) - loss_target)
        return 0.5 * _jnp.sum(_jnp.mean(err, axis=-1)) if err.ndim else 0.5 * err


def _adamw(w, g, m, v):
    m = ADAM_B1 * m + (1.0 - ADAM_B1) * g
    v = ADAM_B2 * v + (1.0 - ADAM_B2) * _jnp.square(g)
    m_hat = m / (1.0 - ADAM_B1 ** ADAM_STEP)
    v_hat = v / (1.0 - ADAM_B2 ** ADAM_STEP)
    delta = -ADAM_LR * (m_hat / (_jnp.sqrt(v_hat) + ADAM_EPS) + ADAM_WD * w)
    return delta, m, v


def reference(x, norm_ffn1, ffn1_gate, ffn1_up, ffn1_down, norm_mix, w_in, ret_gn_gain, pool_w, pool_scale, w_out, norm_ffn2, ffn2_gate, ffn2_up, ffn2_down, norm_final, loss_target, m_norm_ffn1, m_ffn1_gate, m_ffn1_up, m_ffn1_down, m_norm_mix, m_w_in, m_ret_gn_gain, m_pool_w, m_pool_scale, m_w_out, m_norm_ffn2, m_ffn2_gate, m_ffn2_up, m_ffn2_down, m_norm_final, v_norm_ffn1, v_ffn1_gate, v_ffn1_up, v_ffn1_down, v_norm_mix, v_w_in, v_ret_gn_gain, v_pool_w, v_pool_scale, v_w_out, v_norm_ffn2, v_ffn2_gate, v_ffn2_up, v_ffn2_down, v_norm_final):
    given = dict(x=x, norm_ffn1=norm_ffn1, ffn1_gate=ffn1_gate, ffn1_up=ffn1_up, ffn1_down=ffn1_down, norm_mix=norm_mix, w_in=w_in, ret_gn_gain=ret_gn_gain, pool_w=pool_w, pool_scale=pool_scale, w_out=w_out, norm_ffn2=norm_ffn2, ffn2_gate=ffn2_gate, ffn2_up=ffn2_up, ffn2_down=ffn2_down, norm_final=norm_final, loss_target=loss_target, m_norm_ffn1=m_norm_ffn1, m_ffn1_gate=m_ffn1_gate, m_ffn1_up=m_ffn1_up, m_ffn1_down=m_ffn1_down, m_norm_mix=m_norm_mix, m_w_in=m_w_in, m_ret_gn_gain=m_ret_gn_gain, m_pool_w=m_pool_w, m_pool_scale=m_pool_scale, m_w_out=m_w_out, m_norm_ffn2=m_norm_ffn2, m_ffn2_gate=m_ffn2_gate, m_ffn2_up=m_ffn2_up, m_ffn2_down=m_ffn2_down, m_norm_final=m_norm_final, v_norm_ffn1=v_norm_ffn1, v_ffn1_gate=v_ffn1_gate, v_ffn1_up=v_ffn1_up, v_ffn1_down=v_ffn1_down, v_norm_mix=v_norm_mix, v_w_in=v_w_in, v_ret_gn_gain=v_ret_gn_gain, v_pool_w=v_pool_w, v_pool_scale=v_pool_scale, v_w_out=v_w_out, v_norm_ffn2=v_norm_ffn2, v_ffn2_gate=v_ffn2_gate, v_ffn2_up=v_ffn2_up, v_ffn2_down=v_ffn2_down, v_norm_final=v_norm_final)
    weights = {n: given[n] for n in TWIN_WEIGHTS}
    shared = {n: given[n] for n in SHARED_INPUTS}
    per_example = {n: given[n] for n in ['x']}
    grad_fn = _jax.value_and_grad(_loss, argnums=(0, 1))

    def one_microbatch(ex, loss_target):
        ex = dict(ex)
        diff = ex.pop(TWIN_DIFF_INPUT)
        return grad_fn(weights, diff, {**shared, **ex}, loss_target)

    if N_MICROBATCH == 1:
        loss, (grad_w, grad_x) = one_microbatch(per_example, given["loss_target"])
    else:
        def body(carry, xs):
            loss_sum, grad_sum = carry
            l_k, (gw_k, gx_k) = one_microbatch(xs[0], xs[1])
            with _jax.named_scope("update"):
                return (loss_sum + l_k, _jax.tree.map(_jnp.add, grad_sum, gw_k)), gx_k

        init = (_jnp.zeros((), _jnp.float32), _jax.tree.map(_jnp.zeros_like, weights))
        (loss, grad_w), grad_x = _jax.lax.scan(body, init, (per_example, given["loss_target"]))
    with _jax.named_scope("update"):
        delta_w, new_m, new_v = {}, {}, {}
        for n in TWIN_WEIGHTS:
            delta_w[n], new_m[n], new_v[n] = _adamw(weights[n], grad_w[n], given["m_" + n], given["v_" + n])
    return (loss, grad_x, *[grad_w[n] for n in TWIN_WEIGHTS], *[delta_w[n] for n in TWIN_WEIGHTS],
            *[new_m[n] for n in TWIN_WEIGHTS], *[new_v[n] for n in TWIN_WEIGHTS])
```

```python
import functools

import numpy as np
import jax
import jax.numpy as jnp
from jax import lax
from jax.experimental import pallas as pl
from jax.experimental.pallas import tpu as pltpu

F32, BF16 = jnp.float32, jnp.bfloat16
MESH_ID = pl.DeviceIdType.MESH
ANY = pl.BlockSpec(memory_space=pl.ANY)
VMEM_SPEC = pl.BlockSpec(memory_space=pltpu.VMEM)

N_DEV = 8
RMS_EPS = 1e-6
GN_EPS = 1e-5
HEADS, DK, DV = 4, 64, 128
QK_W, V_W, POOL_W = HEADS * DK, HEADS * DV, 512
WINDOWS = (2, 4, 8, 16)
GC = POOL_W // len(WINDOWS)
CHUNK = 64
BLK = 4 * CHUNK
HALO = 16
ROPE_BASE = 10000.0
LR, B1, B2, ADAM_EPS, WD, STEP = 0.001, 0.9, 0.999, 1e-08, 0.01, 10
LANE = 128
TM = 512
VMEM_LIMIT = 56 * 1024 * 1024


def _cparams(n_axes):
    return pltpu.CompilerParams(dimension_semantics=("arbitrary",) * n_axes, vmem_limit_bytes=VMEM_LIMIT)


def _dot(a, b):
    return jnp.dot(a, b, preferred_element_type=F32)


def _dot_nt(a, b):
    return lax.dot_general(a, b, (((1,), (1,)), ((), ())), preferred_element_type=F32)


def _dot_tn(a, b):
    return lax.dot_general(a, b, (((0,), (0,)), ((), ())), preferred_element_type=F32)


def _sigmoid(x):
    return 1.0 / (1.0 + jnp.exp(-x))


def _pad_to(n, m):
    return (n + m - 1) // m * m


def _retention_constants():
    gamma = (1.0 - 2.0 ** (-5.0 - np.arange(HEADS, dtype=np.float32))).astype(np.float32)
    log_g = np.log(gamma).astype(np.float32)
    i = np.arange(BLK)
    diff = (i[:, None] - i[None, :]).astype(np.float32)
    same = (i[:, None] // CHUNK) == (i[None, :] // CHUNK)
    earlier = (i[None, :] // CHUNK) < (i[:, None] // CHUNK)
    mask = np.zeros((HEADS, BLK, BLK), np.float32)
    for h in range(HEADS):
        dec_abs = np.exp(log_g[h] * np.abs(diff)).astype(np.float32)
        dec = np.exp(log_g[h] * diff * earlier).astype(np.float32)
        mask[h] = np.where(same, dec_abs, np.where(earlier, dec, 0.0))
    dq = np.zeros((BLK, V_W), np.float32)
    dk = np.zeros((BLK, QK_W), np.float32)
    gbd = np.zeros((QK_W, V_W), np.float32)
    for h in range(HEADS):
        dq[:, h * DV:(h + 1) * DV] = np.exp(log_g[h] * (i + 1.0)).astype(np.float32)[:, None]
        dk[:, h * DK:(h + 1) * DK] = np.exp(log_g[h] * (BLK - 1.0 - i)).astype(np.float32)[:, None]
        gbd[h * DK:(h + 1) * DK, h * DV:(h + 1) * DV] = np.exp(log_g[h] * np.float32(BLK))
    bd = (gbd > 0).astype(np.float32)
    return jnp.asarray(mask), jnp.asarray(dq), jnp.asarray(dk), jnp.asarray(gbd), jnp.asarray(bd)


def _rotary_tables(seq):
    half = DK // 2
    freqs = ROPE_BASE ** (-jnp.arange(half, dtype=F32) * 2.0 / DK)
    ang = jnp.arange(seq, dtype=F32)[:, None] * freqs[None, :]
    cos, sin = jnp.cos(ang), jnp.sin(ang)
    cos_t = jnp.tile(jnp.concatenate([cos, cos], axis=1), (1, HEADS))
    sin_t = jnp.tile(jnp.concatenate([-sin, sin], axis=1), (1, HEADS))
    return cos_t, sin_t


def _swap_halves(x):
    lane = lax.broadcasted_iota(jnp.int32, (1, QK_W), 1)
    first = (lane & (DK - 1)) < DK // 2
    return jnp.where(first, pltpu.roll(x, QK_W - DK // 2, 1), pltpu.roll(x, DK // 2, 1))


def _head_mask(h):
    lane = lax.broadcasted_iota(jnp.int32, (1, QK_W), 1)
    return (lane >= h * DK) & (lane < (h + 1) * DK)


def _ffn_fwd(x, n, cols, gq, wd, name):
    t, d = x.shape
    fp = cols.shape[1]
    tf = 2 * fp // N_DEV
    nj = fp // tf

    def body(x_ref, n_ref, wg_ref, wu_ref, wd_ref, xo_ref, h_ref, a_ref, b_ref, acc_ref):
        j = pl.program_id(1)

        @pl.when(j == 0)
        def _():
            xv = x_ref[...]
            r = lax.rsqrt(jnp.mean(xv * xv, axis=-1, keepdims=True) + RMS_EPS)
            h_ref[...] = (xv * r * n_ref[...]).astype(BF16)
            acc_ref[...] = jnp.zeros_like(acc_ref)

        h = h_ref[...]
        a = _dot(h, wg_ref[...])
        b = _dot(h, wu_ref[...])
        a_ref[...] = a.astype(BF16)
        b_ref[...] = b.astype(BF16)
        s = (a * _sigmoid(a)) * b
        acc_ref[...] += _dot(s.astype(BF16), wd_ref[...])

        @pl.when(j == nj - 1)
        def _():
            xo_ref[...] = x_ref[...] + 0.5 * acc_ref[...]

    return pl.pallas_call(
        body, name=name, grid=(t // TM, nj),
        in_specs=[pl.BlockSpec((TM, d), lambda i, j: (i, 0)), pl.BlockSpec((1, d), lambda i, j: (0, 0)),
                  pl.BlockSpec((d, tf), lambda i, j: (gq, j)), pl.BlockSpec((d, tf), lambda i, j: (gq + 1, j)),
                  pl.BlockSpec((tf, d), lambda i, j: (j, 0))],
        out_specs=[pl.BlockSpec((TM, d), lambda i, j: (i, 0)), pl.BlockSpec((TM, d), lambda i, j: (i, 0)),
                   pl.BlockSpec((TM, tf), lambda i, j: (i, j)), pl.BlockSpec((TM, tf), lambda i, j: (i, j))],
        out_shape=[jax.ShapeDtypeStruct((t, d), F32), jax.ShapeDtypeStruct((t, d), BF16),
                   jax.ShapeDtypeStruct((t, fp), BF16), jax.ShapeDtypeStruct((t, fp), BF16)],
        scratch_shapes=[pltpu.VMEM((TM, d), F32)],
        compiler_params=_cparams(2),
    )(x, n, cols, cols, wd)


def _ffn_bwd(dxo, x, n, a, b, cols, gq, wd, name):
    t, d = x.shape
    fp = cols.shape[1]
    tf = 2 * fp // N_DEV
    nj = fp // tf

    def body(dxo_ref, x_ref, n_ref, a_ref, b_ref, wg_ref, wu_ref, wd_ref,
             dx_ref, dn_ref, da_ref, db_ref, s_ref, dxob_ref, acc_ref):
        i, j = pl.program_id(0), pl.program_id(1)

        @pl.when((i == 0) & (j == 0))
        def _():
            dn_ref[...] = jnp.zeros_like(dn_ref)

        @pl.when(j == 0)
        def _():
            dxob_ref[...] = dxo_ref[...].astype(BF16)
            acc_ref[...] = jnp.zeros_like(acc_ref)

        av = a_ref[...].astype(F32)
        bv = b_ref[...].astype(F32)
        ds = 0.5 * _dot_nt(dxob_ref[...], wd_ref[...])
        sg = _sigmoid(av)
        sil = av * sg
        da = (ds * bv * (sg * (1.0 + av * (1.0 - sg)))).astype(BF16)
        db = (ds * sil).astype(BF16)
        da_ref[...] = da
        db_ref[...] = db
        s_ref[...] = (sil * bv).astype(BF16)
        acc_ref[...] += _dot_nt(da, wg_ref[...]) + _dot_nt(db, wu_ref[...])

        @pl.when(j == nj - 1)
        def _():
            xv = x_ref[...]
            r = lax.rsqrt(jnp.mean(xv * xv, axis=-1, keepdims=True) + RMS_EPS)
            xh = xv * r
            dh = acc_ref[...]
            dn_ref[...] += jnp.sum(dh * xh, axis=0, keepdims=True)
            dhn = dh * n_ref[...]
            dx_ref[...] = dxo_ref[...] + r * (dhn - xh * jnp.mean(dhn * xh, axis=-1, keepdims=True))

    return pl.pallas_call(
        body, name=name, grid=(t // TM, nj),
        in_specs=[pl.BlockSpec((TM, d), lambda i, j: (i, 0)), pl.BlockSpec((TM, d), lambda i, j: (i, 0)),
                  pl.BlockSpec((1, d), lambda i, j: (0, 0)),
                  pl.BlockSpec((TM, tf), lambda i, j: (i, j)), pl.BlockSpec((TM, tf), lambda i, j: (i, j)),
                  pl.BlockSpec((d, tf), lambda i, j: (gq, j)), pl.BlockSpec((d, tf), lambda i, j: (gq + 1, j)),
                  pl.BlockSpec((tf, d), lambda i, j: (j, 0))],
        out_specs=[pl.BlockSpec((TM, d), lambda i, j: (i, 0)), pl.BlockSpec((1, d), lambda i, j: (0, 0)),
                   pl.BlockSpec((TM, tf), lambda i, j: (i, j)), pl.BlockSpec((TM, tf), lambda i, j: (i, j)),
                   pl.BlockSpec((TM, tf), lambda i, j: (i, j)), pl.BlockSpec((TM, d), lambda i, j: (i, 0))],
        out_shape=[jax.ShapeDtypeStruct((t, d), F32), jax.ShapeDtypeStruct((1, d), F32),
                   jax.ShapeDtypeStruct((t, fp), BF16), jax.ShapeDtypeStruct((t, fp), BF16),
                   jax.ShapeDtypeStruct((t, fp), BF16), jax.ShapeDtypeStruct((t, d), BF16)],
        scratch_shapes=[pltpu.VMEM((TM, d), F32)],
        compiler_params=_cparams(2),
    )(dxo, x, n, a, b, cols, cols, wd)


def _wgrad(a, b, scale, tk, tn, name):
    t, k = a.shape
    n = b.shape[1]
    tt = TM
    nt = t // tt

    def body(a_ref, b_ref, o_ref, acc_ref):
        s = pl.program_id(2)

        @pl.when(s == 0)
        def _():
            acc_ref[...] = jnp.zeros_like(acc_ref)

        acc_ref[...] += _dot_tn(a_ref[...], b_ref[...])

        @pl.when(s == nt - 1)
        def _():
            o_ref[...] = (scale * acc_ref[...]).astype(BF16)

    return pl.pallas_call(
        body, name=name, grid=(k // tk, n // tn, nt),
        in_specs=[pl.BlockSpec((tt, tk), lambda p, q, s: (s, p)), pl.BlockSpec((tt, tn), lambda p, q, s: (s, q))],
        out_specs=pl.BlockSpec((tk, tn), lambda p, q, s: (p, q)),
        out_shape=jax.ShapeDtypeStruct((k, n), BF16),
        scratch_shapes=[pltpu.VMEM((tk, tn), F32)],
        compiler_params=_cparams(3),
    )(a, b)


def _mix_in(x, n, w_in, cos_t, sin_t, seq, name):
    t, d = x.shape
    per_seq = seq // TM

    def body(x_ref, n_ref, w_ref, c_ref, s_ref, h_ref, q_ref, k_ref, v_ref, g_ref, u_ref):
        xv = x_ref[...]
        r = lax.rsqrt(jnp.mean(xv * xv, axis=-1, keepdims=True) + RMS_EPS)
        h = (xv * r * n_ref[...]).astype(BF16)
        h_ref[...] = h
        p = _dot(h, w_ref[...])
        c, s = c_ref[...], s_ref[...]
        q = p[:, :QK_W]
        k = p[:, QK_W:2 * QK_W]
        q_ref[...] = ((q * c + _swap_halves(q) * s) * (DK ** -0.5)).astype(BF16)
        k_ref[...] = (k * c + _swap_halves(k) * s).astype(BF16)
        v_ref[...] = p[:, 2 * QK_W:2 * QK_W + V_W].astype(BF16)
        g_ref[...] = p[:, 2 * QK_W + V_W:2 * QK_W + 2 * V_W]
        u_ref[...] = p[:, 2 * QK_W + 2 * V_W:]

    tile = lambda w: pl.BlockSpec((TM, w), lambda i: (i, 0))
    return pl.pallas_call(
        body, name=name, grid=(t // TM,),
        in_specs=[tile(d), pl.BlockSpec((1, d), lambda i: (0, 0)), pl.BlockSpec(w_in.shape, lambda i: (0, 0)),
                  pl.BlockSpec((TM, QK_W), lambda i: (i % per_seq, 0)), pl.BlockSpec((TM, QK_W), lambda i: (i % per_seq, 0))],
        out_specs=[tile(d), tile(QK_W), tile(QK_W), tile(V_W), tile(V_W), tile(POOL_W)],
        out_shape=[jax.ShapeDtypeStruct((t, d), BF16), jax.ShapeDtypeStruct((t, QK_W), BF16),
                   jax.ShapeDtypeStruct((t, QK_W), BF16), jax.ShapeDtypeStruct((t, V_W), BF16),
                   jax.ShapeDtypeStruct((t, V_W), F32), jax.ShapeDtypeStruct((t, POOL_W), F32)],
        compiler_params=_cparams(1),
    )(x, n, w_in, cos_t, sin_t)


def _mix_in_bwd(dp, dx2, x1, n, w_in, name):
    t, d = x1.shape

    def body(dp_ref, dx2_ref, x_ref, n_ref, w_ref, dx_ref, dn_ref):
        @pl.when(pl.program_id(0) == 0)
        def _():
            dn_ref[...] = jnp.zeros_like(dn_ref)

        dh = _dot_nt(dp_ref[...], w_ref[...])
        xv = x_ref[...]
        r = lax.rsqrt(jnp.mean(xv * xv, axis=-1, keepdims=True) + RMS_EPS)
        xh = xv * r
        dn_ref[...] += jnp.sum(dh * xh, axis=0, keepdims=True)
        dhn = dh * n_ref[...]
        dx_ref[...] = dx2_ref[...] + r * (dhn - xh * jnp.mean(dhn * xh, axis=-1, keepdims=True))

    tile = lambda w: pl.BlockSpec((TM, w), lambda i: (i, 0))
    return pl.pallas_call(
        body, name=name, grid=(t // TM,),
        in_specs=[tile(dp.shape[1]), tile(d), tile(d), pl.BlockSpec((1, d), lambda i: (0, 0)),
                  pl.BlockSpec(w_in.shape, lambda i: (0, 0))],
        out_specs=[tile(d), pl.BlockSpec((1, d), lambda i: (0, 0))],
        out_shape=[jax.ShapeDtypeStruct((t, d), F32), jax.ShapeDtypeStruct((1, d), F32)],
        compiler_params=_cparams(1),
    )(dp, dx2, x1, n, w_in)


def _group_norm(o):
    parts, rstds = [], []
    for h in range(HEADS):
        oh = o[:, h * DV:(h + 1) * DV]
        dlt = oh - jnp.mean(oh, axis=-1, keepdims=True)
        rstd = lax.rsqrt(jnp.mean(dlt * dlt, axis=-1, keepdims=True) + GN_EPS)
        parts.append(dlt * rstd)
        rstds.append(rstd)
    return jnp.concatenate(parts, axis=1), rstds


def _mix_core_fwd(qs, k, v, g, u, x1, consts, gain, wp, scale, w_out, nseq, seq, name):
    t, d = x1.shape
    nblk = seq // BLK
    mask, dq, dk, gbd, bd = consts

    def body(q_ref, k_ref, v_ref, g_ref, u_ref, x1_ref, m_ref, dq_ref, dk_ref, gbd_ref, bd_ref, gain_ref, wp_ref,
             sc_ref, wo_ref, x2_ref, mix_ref, o_ref, pooled_ref, st_ref, state, halo):
        j = pl.program_id(1)

        @pl.when(j == 0)
        def _():
            state[...] = jnp.zeros_like(state)
            halo[...] = jnp.zeros_like(halo)

        qv, kv, vv = q_ref[...], k_ref[...], v_ref[...]
        st = state[...]
        st_ref[0] = st
        cross = _dot(qv, st.astype(BF16)) * dq_ref[...]
        outs = []
        for h in range(HEADS):
            qh = jnp.where(_head_mask(h), qv, jnp.zeros_like(qv))
            am = (_dot_nt(qh, kv) * m_ref[h]).astype(BF16)
            outs.append(_dot(am, vv[:, h * DV:(h + 1) * DV]))
        o = jnp.concatenate(outs, axis=1) + cross
        o_ref[...] = o
        kd = (kv.astype(F32) * dk_ref[...]).astype(BF16)
        state[...] = gbd_ref[...] * st + _dot_tn(kd, vv) * bd_ref[...]

        gv = g_ref[...]
        nrm, _ = _group_norm(o)
        ret = (gv * _sigmoid(gv)) * (nrm * gain_ref[...])

        uv = u_ref[...]
        c = jnp.concatenate([halo[...], uv], axis=0)
        halo[...] = uv[BLK - HALO:, :]
        pos = j * BLK + lax.broadcasted_iota(jnp.int32, (BLK, 1), 0)
        parts = []
        for gi, w in enumerate(WINDOWS):
            c = c + pltpu.roll(c, w // 2, 0)
            cnt = jnp.minimum(pos + 1, w).astype(F32)
            parts.append(c[HALO:, :GC] / cnt)
            if gi + 1 < len(WINDOWS):
                c = c[:, GC:]
        pooled = (jnp.concatenate(parts, axis=1) - uv).astype(BF16)
        pooled_ref[...] = pooled
        z = jnp.concatenate([_dot(pooled[:, gi * GC:(gi + 1) * GC], wp_ref[gi]) for gi in range(len(WINDOWS))], axis=1)
        mix = jnp.concatenate([ret, z * sc_ref[...]], axis=1).astype(BF16)
        mix_ref[...] = mix
        x2_ref[...] = x1_ref[...] + _dot(mix, wo_ref[...])

    blk = lambda w: pl.BlockSpec((BLK, w), lambda i, j: (i * nblk + j, 0))
    full = lambda a: pl.BlockSpec(a.shape, lambda i, j: (0,) * a.ndim)
    return pl.pallas_call(
        body, name=name, grid=(nseq, nblk),
        in_specs=[blk(QK_W), blk(QK_W), blk(V_W), blk(V_W), blk(POOL_W), blk(d),
                  full(mask), full(dq), full(dk), full(gbd), full(bd), full(gain), full(wp), full(scale), full(w_out)],
        out_specs=[blk(d), blk(d), blk(V_W), blk(POOL_W),
                   pl.BlockSpec((1, QK_W, V_W), lambda i, j: (i * nblk + j, 0, 0))],
        out_shape=[jax.ShapeDtypeStruct((t, d), F32), jax.ShapeDtypeStruct((t, d), BF16),
                   jax.ShapeDtypeStruct((t, V_W), F32), jax.ShapeDtypeStruct((t, POOL_W), BF16),
                   jax.ShapeDtypeStruct((nseq * nblk, QK_W, V_W), F32)],
        scratch_shapes=[pltpu.VMEM((QK_W, V_W), F32), pltpu.VMEM((HALO, POOL_W), F32)],
        compiler_params=_cparams(2),
    )(qs, k, v, g, u, x1, mask, dq, dk, gbd, bd, gain, wp, scale, w_out)


def _mix_core_bwd(dx2, qs, k, v, g, o, pooled, st, consts, gain, wp, scale, w_out, cos_t, sin_t, nseq, seq, name):
    t, d = dx2.shape
    nblk = seq // BLK
    mask, dq, dk, gbd, bd = consts
    n_win = len(WINDOWS)

    def body(dx2_ref, q_ref, k_ref, v_ref, g_ref, o_ref, pooled_ref, st_ref, m_ref, dq_ref, dk_ref, gbd_ref, bd_ref,
             gain_ref, wp_ref, sc_ref, wo_ref, c_ref, s_ref,
             dp_ref, dx2b_ref, dgain_ref, dscale_ref, dwp_ref, rstate, carry):
        i, j = pl.program_id(0), pl.program_id(1)

        @pl.when((i == 0) & (j == 0))
        def _():
            dgain_ref[...] = jnp.zeros_like(dgain_ref)
            dscale_ref[...] = jnp.zeros_like(dscale_ref)
            dwp_ref[...] = jnp.zeros_like(dwp_ref)

        @pl.when(j == 0)
        def _():
            rstate[...] = jnp.zeros_like(rstate)
            carry[...] = jnp.zeros_like(carry)

        dx2b = dx2_ref[...].astype(BF16)
        dx2b_ref[...] = dx2b
        dmix = _dot_nt(dx2b, wo_ref[...])
        dret, dpool = dmix[:, :V_W], dmix[:, V_W:]

        gv, ov, gain_v = g_ref[...], o_ref[...], gain_ref[...]
        sg = _sigmoid(gv)
        sil = gv * sg
        nrm, rstds = _group_norm(ov)
        dg = dret * (nrm * gain_v) * (sg * (1.0 + gv * (1.0 - sg)))
        dgn = dret * sil
        dgain_ref[...] += jnp.sum(dgn * nrm, axis=0, keepdims=True)
        dnrm = dgn * gain_v
        do_parts = []
        for h in range(HEADS):
            dn_h = dnrm[:, h * DV:(h + 1) * DV]
            n_h = nrm[:, h * DV:(h + 1) * DV]
            do_parts.append(rstds[h] * (dn_h - jnp.mean(dn_h, axis=-1, keepdims=True)
                                        - n_h * jnp.mean(dn_h * n_h, axis=-1, keepdims=True)))
        do = jnp.concatenate(do_parts, axis=1)
        dob = do.astype(BF16)

        qv, kv, vv = q_ref[...], k_ref[...], v_ref[...]
        stb = st_ref[0].astype(BF16)
        rs = rstate[...]
        rsb = rs.astype(BF16)
        dod = (do * dq_ref[...]).astype(BF16)
        dqs = _dot_nt(dod, stb)
        dst = _dot_tn(qv, dod) * bd_ref[...]
        dkf = dk_ref[...]
        kd = (kv.astype(F32) * dkf).astype(BF16)
        dks = _dot_nt(vv, rsb) * dkf
        dvs = _dot(kd, rsb)
        dv_parts = []
        for h in range(HEADS):
            hm = _head_mask(h)
            qh = jnp.where(hm, qv, jnp.zeros_like(qv))
            mh = m_ref[h]
            am = (_dot_nt(qh, kv) * mh).astype(BF16)
            dpm = (_dot_nt(dob[:, h * DV:(h + 1) * DV], vv[:, h * DV:(h + 1) * DV]) * mh).astype(BF16)
            dqs = dqs + jnp.where(hm, _dot(dpm, kv), 0.0)
            dks = dks + jnp.where(hm, _dot_tn(dpm, qv), 0.0)
            dv_parts.append(_dot_tn(am, dob[:, h * DV:(h + 1) * DV]))
        dvs = dvs + jnp.concatenate(dv_parts, axis=1)
        rstate[...] = dst + gbd_ref[...] * rs

        cv, sv = c_ref[...], s_ref[...]
        dqr = dqs * (DK ** -0.5)
        dq_pre = dqr * cv + _swap_halves(dqr * sv)
        dk_pre = dks * cv + _swap_halves(dks * sv)

        pv = pooled_ref[...]
        sc = sc_ref[...]
        dzb = (dpool * sc).astype(BF16)
        z_parts, dpo_parts = [], []
        for gi in range(n_win):
            p_g = pv[:, gi * GC:(gi + 1) * GC]
            dz_g = dzb[:, gi * GC:(gi + 1) * GC]
            z_parts.append(_dot(p_g, wp_ref[gi]))
            dwp_ref[gi] += _dot_tn(p_g, dz_g)
            dpo_parts.append(_dot_nt(dz_g, wp_ref[gi]))
        dscale_ref[...] += jnp.sum(dpool * jnp.concatenate(z_parts, axis=1), axis=0, keepdims=True)
        dpo = jnp.concatenate(dpo_parts, axis=1)
        pos = (nblk - 1 - j) * BLK + lax.broadcasted_iota(jnp.int32, (BLK, 1), 0)
        e = jnp.concatenate(
            [dpo[:, gi * GC:(gi + 1) * GC] / jnp.minimum(pos + 1, w).astype(F32) for gi, w in enumerate(WINDOWS)], axis=1)
        c = jnp.concatenate([e, carry[...]], axis=0)
        carry[...] = e[:HALO, :]
        rows = BLK + HALO
        lead = []
        for gi, w in enumerate(WINDOWS):
            c = c + pltpu.roll(c, rows - w // 2, 0)
            lead.append(c[:BLK, :GC])
            if gi + 1 < n_win:
                c = c[:, GC:]
        du = jnp.concatenate(lead, axis=1) - dpo

        dp_ref[:, 0:QK_W] = dq_pre.astype(BF16)
        dp_ref[:, QK_W:2 * QK_W] = dk_pre.astype(BF16)
        dp_ref[:, 2 * QK_W:2 * QK_W + V_W] = dvs.astype(BF16)
        dp_ref[:, 2 * QK_W + V_W:2 * QK_W + 2 * V_W] = dg.astype(BF16)
        dp_ref[:, 2 * QK_W + 2 * V_W:] = du.astype(BF16)

    rev = lambda i, j: i * nblk + (nblk - 1 - j)
    blk = lambda w: pl.BlockSpec((BLK, w), lambda i, j: (rev(i, j), 0))
    full = lambda a: pl.BlockSpec(a.shape, lambda i, j: (0,) * a.ndim)
    in_w = 2 * QK_W + 2 * V_W + POOL_W
    return pl.pallas_call(
        body, name=name, grid=(nseq, nblk),
        in_specs=[blk(d), blk(QK_W), blk(QK_W), blk(V_W), blk(V_W), blk(V_W), blk(POOL_W),
                  pl.BlockSpec((1, QK_W, V_W), lambda i, j: (rev(i, j), 0, 0)),
                  full(mask), full(dq), full(dk), full(gbd), full(bd), full(gain), full(wp), full(scale), full(w_out),
                  pl.BlockSpec((BLK, QK_W), lambda i, j: (nblk - 1 - j, 0)),
                  pl.BlockSpec((BLK, QK_W), lambda i, j: (nblk - 1 - j, 0))],
        out_specs=[blk(in_w), blk(d), pl.BlockSpec((1, V_W), lambda i, j: (0, 0)),
                   pl.BlockSpec((1, POOL_W), lambda i, j: (0, 0)), pl.BlockSpec((n_win, GC, GC), lambda i, j: (0, 0, 0))],
        out_shape=[jax.ShapeDtypeStruct((t, in_w), BF16), jax.ShapeDtypeStruct((t, d), BF16),
                   jax.ShapeDtypeStruct((1, V_W), F32), jax.ShapeDtypeStruct((1, POOL_W), F32),
                   jax.ShapeDtypeStruct((n_win, GC, GC), F32)],
        scratch_shapes=[pltpu.VMEM((QK_W, V_W), F32), pltpu.VMEM((HALO, POOL_W), F32)],
        compiler_params=_cparams(2),
    )(dx2, qs, k, v, g, o, pooled, st, mask, dq, dk, gbd, bd, gain, wp, scale, w_out, cos_t, sin_t)


def _loss_head(x3, nf, tgt, name):
    t, d = x3.shape

    def body(x_ref, n_ref, t_ref, dx_ref, dn_ref, loss_ref):
        @pl.when(pl.program_id(0) == 0)
        def _():
            dn_ref[...] = jnp.zeros_like(dn_ref)
            loss_ref[...] = jnp.zeros_like(loss_ref)

        xv = x_ref[...]
        nv = n_ref[...]
        r = lax.rsqrt(jnp.mean(xv * xv, axis=-1, keepdims=True) + RMS_EPS)
        xh = xv * r
        err = xh * nv - t_ref[...]
        row = jnp.mean(err * err, axis=-1, keepdims=True)
        loss_ref[...] += 0.5 * jnp.sum(row, axis=0, keepdims=True)
        dy = err * (1.0 / d)
        dn_ref[...] += jnp.sum(dy * xh, axis=0, keepdims=True)
        dxh = dy * nv
        dx_ref[...] = r * (dxh - xh * jnp.mean(dxh * xh, axis=-1, keepdims=True))

    tile = pl.BlockSpec((TM, d), lambda i: (i, 0))
    return pl.pallas_call(
        body, name=name, grid=(t // TM,),
        in_specs=[tile, pl.BlockSpec((1, d), lambda i: (0, 0)), tile],
        out_specs=[tile, pl.BlockSpec((1, d), lambda i: (0, 0)), pl.BlockSpec((1, 1), lambda i: (0, 0))],
        out_shape=[jax.ShapeDtypeStruct((t, d), F32), jax.ShapeDtypeStruct((1, d), F32), jax.ShapeDtypeStruct((1, 1), F32)],
        compiler_params=_cparams(1),
    )(x3, nf, tgt)


def _coords():
    return lax.axis_index("x"), lax.axis_index("y"), lax.axis_index("c")


def _window(ref, kind, idx, size):
    if kind == "col":
        return ref.at[:, pl.ds(pl.multiple_of(idx * size, LANE), size)]
    return ref.at[pl.ds(pl.multiple_of(idx * size, 8), size), :]


def _all_gather(parts, name):
    n = len(parts)
    kinds = [kd for _, kd in parts]
    sizes = [a.shape[1] if kd == "col" else a.shape[0] for a, kd in parts]

    def body(*refs):
        ins, outs = refs[:n], refs[n:2 * n]
        send_sems, recv_sems, local_sems = refs[2 * n:]
        x, y, c = _coords()
        me, sibling = (x, y, c), (x, y, 1 - c)
        chips = [(1 - x, y), (x, 1 - y), (1 - x, 1 - y)]

        def win(p, dev):
            return _window(outs[p], kinds[p], 4 * dev[0] + 2 * dev[1] + dev[2], sizes[p])

        def copy(p, k, block, to, src=None):
            return pltpu.make_async_remote_copy(
                src_ref=win(p, block) if src is None else src, dst_ref=win(p, block),
                send_sem=send_sems.at[p * 7 + k], recv_sem=recv_sems.at[p * 7 + k], device_id=to, device_id_type=MESH_ID)

        mine = [pltpu.make_async_copy(ins[p], win(p, me), local_sems.at[p]) for p in range(n)]
        for cp in mine:
            cp.start()
        first = []
        for p in range(n):
            first.append(copy(p, 0, me, sibling, src=ins[p]))
            first += [copy(p, 1 + q, me, (*chip, c), src=ins[p]) for q, chip in enumerate(chips)]
        for cp in first:
            cp.start()
        passed = []
        for q, chip in enumerate(chips):
            for p in range(n):
                copy(p, 1 + q, (*chip, c), me).wait_recv()
                fwd = copy(p, 4 + q, (*chip, c), sibling)
                fwd.start()
                passed.append(fwd)
        for p in range(n):
            copy(p, 0, sibling, me).wait_recv()
            for q, chip in enumerate(chips):
                copy(p, 4 + q, (*chip, 1 - c), me).wait_recv()
        for cp in first + passed:
            cp.wait_send()
        for cp in mine:
            cp.wait()

    out_shape = [jax.ShapeDtypeStruct((a.shape[0], N_DEV * a.shape[1]) if kd == "col" else (N_DEV * a.shape[0], a.shape[1]),
                                      a.dtype) for a, kd in parts]
    return pl.pallas_call(
        body, name=name, in_specs=[ANY] * n, out_specs=[ANY] * n, out_shape=out_shape,
        scratch_shapes=[pltpu.SemaphoreType.DMA((7 * n,)), pltpu.SemaphoreType.DMA((7 * n,)), pltpu.SemaphoreType.DMA((n,))],
    )(*[a for a, _ in parts])


def _shard_shape(a, kd):
    return (a.shape[0], a.shape[1] // N_DEV) if kd == "col" else (a.shape[0] // N_DEV, a.shape[1])


def _rs_pair(grads, name):
    n = len(grads)
    kinds = [kd for _, kd in grads]
    shapes = [_shard_shape(a, kd) for a, kd in grads]

    def body(*refs):
        ins, outs = refs[:n], refs[n:2 * n]
        send_sems, recv_sems = refs[2 * n:]
        x, y, c = _coords()
        copies = []
        for p in range(n):
            size = shapes[p][1] if kinds[p] == "col" else shapes[p][0]
            for s in range(4):
                src = _window(ins[p], kinds[p], 2 * s + (1 - c), size)
                copies.append(pltpu.make_async_remote_copy(
                    src_ref=src, dst_ref=outs[p].at[s], send_sem=send_sems.at[4 * p + s], recv_sem=recv_sems.at[4 * p + s],
                    device_id=(x, y, 1 - c), device_id_type=MESH_ID))
        for cp in copies:
            cp.start()
        for cp in copies:
            cp.wait_recv()
        for cp in copies:
            cp.wait_send()

    return pl.pallas_call(
        body, name=name, in_specs=[ANY] * n, out_specs=[ANY] * n,
        out_shape=[jax.ShapeDtypeStruct((4,) + shapes[p], BF16) for p in range(n)],
        scratch_shapes=[pltpu.SemaphoreType.DMA((4 * n,)), pltpu.SemaphoreType.DMA((4 * n,))],
    )(*[a for a, _ in grads])


def _rs_chips(sums, name):
    n = len(sums)

    def body(*refs):
        ins, outs = refs[:n], refs[n:2 * n]
        send_sems, recv_sems = refs[2 * n:]
        x, y, c = _coords()
        chips = [(1 - x, y), (x, 1 - y), (1 - x, 1 - y)]
        copies = []
        for p in range(n):
            for q, (cx, cy) in enumerate(chips):
                copies.append(pltpu.make_async_remote_copy(
                    src_ref=ins[p].at[2 * cx + cy], dst_ref=outs[p].at[q],
                    send_sem=send_sems.at[3 * p + q], recv_sem=recv_sems.at[3 * p + q],
                    device_id=(cx, cy, c), device_id_type=MESH_ID))
        for cp in copies:
            cp.start()
        for cp in copies:
            cp.wait_recv()
        for cp in copies:
            cp.wait_send()

    return pl.pallas_call(
        body, name=name, in_specs=[ANY] * n, out_specs=[ANY] * n,
        out_shape=[jax.ShapeDtypeStruct((3,) + a.shape[1:], BF16) for a in sums],
        scratch_shapes=[pltpu.SemaphoreType.DMA((3 * n,)), pltpu.SemaphoreType.DMA((3 * n,))],
    )(*sums)


def _pair_sum(grad, kd, recv, core, name):
    _, r, cw = recv.shape
    tr = min(r, TM)

    def body(core_ref, g_ref, r_ref, o_ref):
        del core_ref
        o_ref[0] = (g_ref[...].astype(F32) + r_ref[0].astype(F32)).astype(BF16)

    if kd == "col":
        g_spec = pl.BlockSpec((tr, cw), lambda s, i, core_ref: (i, 2 * s + core_ref[0]))
    else:
        g_spec = pl.BlockSpec((tr, cw), lambda s, i, core_ref: ((2 * s + core_ref[0]) * (r // tr) + i, 0))
    grid_spec = pltpu.PrefetchScalarGridSpec(
        num_scalar_prefetch=1, grid=(4, r // tr),
        in_specs=[g_spec, pl.BlockSpec((1, tr, cw), lambda s, i, core_ref: (s, i, 0))],
        out_specs=pl.BlockSpec((1, tr, cw), lambda s, i, core_ref: (s, i, 0)))
    return pl.pallas_call(
        body, name=name, grid_spec=grid_spec, out_shape=jax.ShapeDtypeStruct(recv.shape, BF16),
        compiler_params=_cparams(2),
    )(core, grad, recv)


def _adam_math(w, g, m, v):
    m2 = B1 * m + (1.0 - B1) * g
    v2 = B2 * v + (1.0 - B2) * (g * g)
    m_hat = m2 / (1.0 - B1 ** STEP)
    v_hat = v2 / (1.0 - B2 ** STEP)
    delta = -LR * (m_hat / (jnp.sqrt(v_hat) + ADAM_EPS) + WD * w)
    return delta, m2, v2


def _chip_sum_adam(psum, recv, chip, w, m, v, name):
    r, cw = w.shape
    pc = psum.shape[2]
    tr = min(r, TM)

    def body(chip_ref, p_ref, r_ref, w_ref, m_ref, v_ref, g_ref, d_ref, m2_ref, v2_ref):
        del chip_ref
        g = p_ref[0].astype(F32) + r_ref[0].astype(F32) + r_ref[1].astype(F32) + r_ref[2].astype(F32)
        g = g[:, :cw]
        delta, m2, v2 = _adam_math(w_ref[...], g, m_ref[...], v_ref[...])
        g_ref[...] = g
        d_ref[...] = delta
        m2_ref[...] = m2
        v2_ref[...] = v2

    loc = pl.BlockSpec((tr, cw), lambda i, chip_ref: (i, 0))
    grid_spec = pltpu.PrefetchScalarGridSpec(
        num_scalar_prefetch=1, grid=(r // tr,),
        in_specs=[pl.BlockSpec((1, tr, pc), lambda i, chip_ref: (chip_ref[0], i, 0)),
                  pl.BlockSpec((3, tr, pc), lambda i, chip_ref: (0, i, 0)), loc, loc, loc],
        out_specs=[loc, loc, loc, loc])
    return pl.pallas_call(
        body, name=name, grid_spec=grid_spec, out_shape=[jax.ShapeDtypeStruct((r, cw), F32)] * 4,
        compiler_params=_cparams(1),
    )(chip, psum, recv, w, m, v)


def _small_allreduce_adam(partials, params, moms, vels, name):
    n = len(partials)
    row0 = []
    rows = 0
    for a in partials:
        row0.append(rows)
        rows += _pad_to(a.shape[0], 8)
    width = max(a.shape[1] for a in partials)

    def body(*refs):
        g_in = refs[:n]
        w_in, m_in, v_in = refs[n:2 * n], refs[2 * n:3 * n], refs[3 * n:4 * n]
        outs = refs[4 * n:8 * n]
        slab, send_sems, recv_sems = refs[8 * n:]
        x, y, c = _coords()
        me = 4 * x + 2 * y + c
        slab[me] = jnp.zeros((rows, width), F32)
        for p in range(n):
            r, cw = partials[p].shape
            slab[me, row0[p]:row0[p] + r, 0:cw] = g_in[p][...]
        copies = []
        for q in range(1, N_DEV):
            peer = me ^ q
            copies.append(pltpu.make_async_remote_copy(
                src_ref=slab.at[me], dst_ref=slab.at[me], send_sem=send_sems.at[q - 1], recv_sem=recv_sems.at[q - 1],
                device_id=(peer >> 2, (peer >> 1) & 1, peer & 1), device_id_type=MESH_ID))
        for cp in copies:
            cp.start()
        for cp in copies:
            cp.wait_recv()
        for cp in copies:
            cp.wait_send()
        for p in range(n):
            r, cw = partials[p].shape
            g = slab[0, row0[p]:row0[p] + r, 0:cw]
            for dev in range(1, N_DEV):
                g = g + slab[dev, row0[p]:row0[p] + r, 0:cw]
            delta, m2, v2 = _adam_math(w_in[p][...], g, m_in[p][...], v_in[p][...])
            outs[4 * p][...] = g
            outs[4 * p + 1][...] = delta
            outs[4 * p + 2][...] = m2
            outs[4 * p + 3][...] = v2

    out_shape = []
    for a in partials:
        out_shape += [jax.ShapeDtypeStruct(a.shape, F32)] * 4
    return pl.pallas_call(
        body, name=name, in_specs=[VMEM_SPEC] * (4 * n), out_specs=[VMEM_SPEC] * (4 * n), out_shape=out_shape,
        scratch_shapes=[pltpu.VMEM((N_DEV, rows, width), F32), pltpu.SemaphoreType.DMA((N_DEV - 1,)),
                        pltpu.SemaphoreType.DMA((N_DEV - 1,))],
    )(*partials, *params, *moms, *vels)


def _local_step(xf, tgt, nseq, seq, cols_all, win_all, d1_all, d2_all, wout_all, small_w):
    d = xf.shape[1]
    n1, n2, gain, pool_w, pool_scale, n3, nf = small_w
    tf = 2 * cols_all.shape[1] // N_DEV
    consts = _retention_constants()
    cos_t, sin_t = _rotary_tables(seq)
    wp_b = pool_w.astype(BF16)

    x1, h1, a1, b1 = _ffn_fwd(xf, n1, cols_all, 0, d1_all, "ffn1_fwd")
    h2, qs, kr, vv, gg, uu = _mix_in(x1, n2, win_all, cos_t, sin_t, seq, "mix_in")
    x2, mix, oo, pooled, states = _mix_core_fwd(qs, kr, vv, gg, uu, x1, consts, gain, wp_b, pool_scale, wout_all,
                                                 nseq, seq, "mix_core_fwd")
    x3, h3, a3, b3 = _ffn_fwd(x2, n3, cols_all, 2, d2_all, "ffn2_fwd")
    dx3, dnf, loss_part = _loss_head(x3, nf, tgt, "loss_head")

    dx2, dn3, da3, db3, s3, dx3b = _ffn_bwd(dx3, x2, n3, a3, b3, cols_all, 2, d2_all, "ffn2_bwd")
    g_wg2 = _wgrad(h3, da3, 1.0, d, tf, "wgrad_gate2")
    g_wu2 = _wgrad(h3, db3, 1.0, d, tf, "wgrad_up2")
    g_wd2 = _wgrad(s3, dx3b, 0.5, tf, d, "wgrad_down2")
    dp, dx2b, dgain, dscale, dwp = _mix_core_bwd(dx2, qs, kr, vv, gg, oo, pooled, states, consts, gain, wp_b,
                                                 pool_scale, wout_all, cos_t, sin_t, nseq, seq, "mix_core_bwd")
    g_wout = _wgrad(mix, dx2b, 1.0, d, d, "wgrad_out")
    g_win = _wgrad(h2, dp, 1.0, d, d, "wgrad_in")
    dx1, dn2 = _mix_in_bwd(dp, dx2, x1, n2, win_all, "mix_in_bwd")
    dx0, dn1, da1, db1, s1, dx1b = _ffn_bwd(dx1, xf, n1, a1, b1, cols_all, 0, d1_all, "ffn1_bwd")
    g_wg1 = _wgrad(h1, da1, 1.0, d, tf, "wgrad_gate1")
    g_wu1 = _wgrad(h1, db1, 1.0, d, tf, "wgrad_up1")
    g_wd1 = _wgrad(s1, dx1b, 0.5, tf, d, "wgrad_down1")
    grads = [(g_wg1, "col"), (g_wu1, "col"), (g_wd1, "row"), (g_win, "col"), (g_wout, "row"),
             (g_wg2, "col"), (g_wu2, "col"), (g_wd2, "row")]
    return loss_part, dx0, grads, (dn1, dn2, dgain, dwp, dscale, dn3, dnf)


def kernel(x, norm_ffn1, ffn1_gate, ffn1_up, ffn1_down, norm_mix, w_in, ret_gn_gain, pool_w, pool_scale, w_out, norm_ffn2, ffn2_gate, ffn2_up, ffn2_down, norm_final, loss_target, m_norm_ffn1, m_ffn1_gate, m_ffn1_up, m_ffn1_down, m_norm_mix, m_w_in, m_ret_gn_gain, m_pool_w, m_pool_scale, m_w_out, m_norm_ffn2, m_ffn2_gate, m_ffn2_up, m_ffn2_down, m_norm_final, v_norm_ffn1, v_ffn1_gate, v_ffn1_up, v_ffn1_down, v_norm_mix, v_w_in, v_ret_gn_gain, v_pool_w, v_pool_scale, v_w_out, v_norm_ffn2, v_ffn2_gate, v_ffn2_up, v_ffn2_down, v_norm_final):
    nseq, seq, d = x.shape
    t = nseq * seq
    f_loc = ffn1_gate.shape[2]
    f_pad = _pad_to(f_loc, LANE)
    xf = x.reshape(t, d)
    tgt = loss_target.reshape(t, d)
    core = lax.axis_index("c").astype(jnp.int32).reshape(1)
    chip = (2 * lax.axis_index("x") + lax.axis_index("y")).astype(jnp.int32).reshape(1)

    colp = lambda w: jnp.pad(w[0].astype(BF16), ((0, 0), (0, f_pad - f_loc)))
    rowp = lambda w: jnp.pad(w[0].astype(BF16), ((0, f_pad - f_loc), (0, 0)))
    ffn_cols = jnp.concatenate([colp(ffn1_gate), colp(ffn1_up), colp(ffn2_gate), colp(ffn2_up)], axis=0)
    cols_all, win_all, d1_all, d2_all, wout_all = _all_gather(
        [(ffn_cols, "col"), (w_in[0].astype(BF16), "col"), (rowp(ffn1_down), "row"), (rowp(ffn2_down), "row"),
         (w_out[0].astype(BF16), "row")], "all_gather_weights")

    small_w = (norm_ffn1, norm_mix, ret_gn_gain, pool_w[0], pool_scale, norm_ffn2, norm_final.reshape(1, d))
    loss_part, dx0, grads, small_parts = _local_step(xf, tgt, nseq, seq, cols_all, win_all, d1_all, d2_all, wout_all, small_w)
    dn1, dn2, dgain, dwp, dscale, dn3, dnf = small_parts

    names = ["ffn1_gate", "ffn1_up", "ffn1_down", "w_in", "w_out", "ffn2_gate", "ffn2_up", "ffn2_down"]
    local = [(ffn1_gate, m_ffn1_gate, v_ffn1_gate), (ffn1_up, m_ffn1_up, v_ffn1_up), (ffn1_down, m_ffn1_down, v_ffn1_down),
             (w_in, m_w_in, v_w_in), (w_out, m_w_out, v_w_out), (ffn2_gate, m_ffn2_gate, v_ffn2_gate),
             (ffn2_up, m_ffn2_up, v_ffn2_up), (ffn2_down, m_ffn2_down, v_ffn2_down)]
    pair_recv = _rs_pair(grads, "rs_pair")
    pair_sums = [_pair_sum(g, kd, rcv, core, "pair_sum_" + nm) for (g, kd), rcv, nm in zip(grads, pair_recv, names)]
    chip_recv = _rs_chips(pair_sums, "rs_chips")
    big = {}
    for ps, rcv, (w, m, v), nm in zip(pair_sums, chip_recv, local, names):
        g, dlt, m2, v2 = _chip_sum_adam(ps, rcv, chip, w[0], m[0], v[0], "adam_" + nm)
        big[nm] = tuple(a[None] for a in (g, dlt, m2, v2))

    small_names = ["norm_ffn1", "norm_mix", "ret_gn_gain", "pool_w", "pool_scale", "norm_ffn2", "norm_final"]
    flat = lambda a: a.reshape(pool_w.size // d, d)
    partials = [dn1, dn2, dgain, flat(dwp), dscale, dn3, dnf]
    params = [norm_ffn1, norm_mix, ret_gn_gain, flat(pool_w), pool_scale, norm_ffn2, norm_final.reshape(1, d)]
    moms = [m_norm_ffn1, m_norm_mix, m_ret_gn_gain, flat(m_pool_w), m_pool_scale, m_norm_ffn2, m_norm_final.reshape(1, d)]
    vels = [v_norm_ffn1, v_norm_mix, v_ret_gn_gain, flat(v_pool_w), v_pool_scale, v_norm_ffn2, v_norm_final.reshape(1, d)]
    small_out = _small_allreduce_adam(partials, params, moms, vels, "small_allreduce_adam")
    shapes = [norm_ffn1.shape, norm_mix.shape, ret_gn_gain.shape, pool_w.shape, pool_scale.shape, norm_ffn2.shape,
              norm_final.shape]
    small = {nm: tuple(small_out[4 * p + q].reshape(shapes[p]) for q in range(4)) for p, nm in enumerate(small_names)}

    loss = lax.psum(loss_part[0, 0], ("x", "y", "c"))
    order = ["norm_ffn1", "ffn1_gate", "ffn1_up", "ffn1_down", "norm_mix", "w_in", "ret_gn_gain", "pool_w", "pool_scale",
             "w_out", "norm_ffn2", "ffn2_gate", "ffn2_up", "ffn2_down", "norm_final"]
    both = {**big, **small}
    outs = [loss, dx0.reshape(nseq, seq, d)]
    for q in range(4):
        outs += [both[nm][q] for nm in order]
    return tuple(outs)
```

```python
import functools

import numpy as np
import jax
import jax.numpy as jnp
from jax import lax
from jax.experimental import pallas as pl
from jax.experimental.pallas import tpu as pltpu

F32, BF16 = jnp.float32, jnp.bfloat16
MESH_ID = pl.DeviceIdType.MESH
ANY = pl.BlockSpec(memory_space=pl.ANY)
VMEM_SPEC = pl.BlockSpec(memory_space=pltpu.VMEM)

N_DEV = 8
RMS_EPS = 1e-6
GN_EPS = 1e-5
HEADS, DK, DV = 4, 64, 128
QK_W, V_W, POOL_W = HEADS * DK, HEADS * DV, 512
WINDOWS = (2, 4, 8, 16)
GC = POOL_W // len(WINDOWS)
CHUNK = 64
BLK = 4 * CHUNK
HALO = 16
ROPE_BASE = 10000.0
LR, B1, B2, ADAM_EPS, WD, STEP = 0.001, 0.9, 0.999, 1e-08, 0.01, 10
LANE = 128
TM = 512
WGRAD_TT = 2048
VMEM_LIMIT = 56 * 1024 * 1024


def _cparams(n_axes):
    return pltpu.CompilerParams(dimension_semantics=("arbitrary",) * n_axes, vmem_limit_bytes=VMEM_LIMIT)


def _dot(a, b):
    return jnp.dot(a, b, preferred_element_type=F32)


def _dot_nt(a, b):
    return lax.dot_general(a, b, (((1,), (1,)), ((), ())), preferred_element_type=F32)


def _dot_tn(a, b):
    return lax.dot_general(a, b, (((0,), (0,)), ((), ())), preferred_element_type=F32)


def _sigmoid(x):
    return 0.5 * jnp.tanh(0.5 * x) + 0.5


def _transpose(w, name):
    r, c = w.shape
    tb = 512

    def body(x_ref, o_ref):
        o_ref[...] = x_ref[...].T

    return pl.pallas_call(
        body, name=name, grid=(r // tb, c // tb),
        in_specs=[pl.BlockSpec((tb, tb), lambda i, j: (i, j))],
        out_specs=pl.BlockSpec((tb, tb), lambda i, j: (j, i)),
        out_shape=jax.ShapeDtypeStruct((c, r), w.dtype),
        compiler_params=_cparams(2),
    )(w)


def _pad_to(n, m):
    return (n + m - 1) // m * m


def _retention_constants():
    gamma = (1.0 - 2.0 ** (-5.0 - np.arange(HEADS, dtype=np.float32))).astype(np.float32)
    log_g = np.log(gamma).astype(np.float32)
    i = np.arange(BLK)
    diff = (i[:, None] - i[None, :]).astype(np.float32)
    same = (i[:, None] // CHUNK) == (i[None, :] // CHUNK)
    earlier = (i[None, :] // CHUNK) < (i[:, None] // CHUNK)
    mask = np.zeros((HEADS, BLK, BLK), np.float32)
    for h in range(HEADS):
        dec_abs = np.exp(log_g[h] * np.abs(diff)).astype(np.float32)
        dec = np.exp(log_g[h] * diff * earlier).astype(np.float32)
        mask[h] = np.where(same, dec_abs, np.where(earlier, dec, 0.0))
    dq = np.zeros((BLK, V_W), np.float32)
    dk = np.zeros((BLK, QK_W), np.float32)
    gbd = np.zeros((QK_W, V_W), np.float32)
    for h in range(HEADS):
        dq[:, h * DV:(h + 1) * DV] = np.exp(log_g[h] * (i + 1.0)).astype(np.float32)[:, None]
        dk[:, h * DK:(h + 1) * DK] = np.exp(log_g[h] * (BLK - 1.0 - i)).astype(np.float32)[:, None]
        gbd[h * DK:(h + 1) * DK, h * DV:(h + 1) * DV] = np.exp(log_g[h] * np.float32(BLK))
    bd = (gbd > 0).astype(np.float32)
    return jnp.asarray(mask), jnp.asarray(dq), jnp.asarray(dk), jnp.asarray(gbd), jnp.asarray(bd)


def _rotary_tables(seq):
    half = DK // 2
    freqs = ROPE_BASE ** (-jnp.arange(half, dtype=F32) * 2.0 / DK)
    ang = jnp.arange(seq, dtype=F32)[:, None] * freqs[None, :]
    cos, sin = jnp.cos(ang), jnp.sin(ang)
    cos_t = jnp.tile(jnp.concatenate([cos, cos], axis=1), (1, HEADS))
    sin_t = jnp.tile(jnp.concatenate([-sin, sin], axis=1), (1, HEADS))
    return cos_t, sin_t


def _swap_halves(x):
    lane = lax.broadcasted_iota(jnp.int32, (1, QK_W), 1)
    first = (lane & (DK - 1)) < DK // 2
    return jnp.where(first, pltpu.roll(x, QK_W - DK // 2, 1), pltpu.roll(x, DK // 2, 1))


def _head_mask(h):
    lane = lax.broadcasted_iota(jnp.int32, (1, QK_W), 1)
    return (lane >= h * DK) & (lane < (h + 1) * DK)


def _ffn_fwd(x, n, cols, gq, wd, name):
    t, d = x.shape
    fp = cols.shape[1]
    tf = 2 * fp // N_DEV
    nj = fp // tf

    def body(x_ref, n_ref, wg_ref, wu_ref, wd_ref, xo_ref, h_ref, b_ref, sil_ref, dsil_ref, s_ref, acc_ref):
        j = pl.program_id(1)

        @pl.when(j == 0)
        def _():
            xv = x_ref[...]
            r = lax.rsqrt(jnp.mean(xv * xv, axis=-1, keepdims=True) + RMS_EPS)
            h_ref[...] = (xv * r * n_ref[...]).astype(BF16)
            acc_ref[...] = jnp.zeros_like(acc_ref)

        h = h_ref[...]
        a = _dot(h, wg_ref[...])
        b = _dot(h, wu_ref[...])
        sg = _sigmoid(a)
        sil = a * sg
        s = (sil * b).astype(BF16)
        b_ref[...] = b.astype(BF16)
        sil_ref[...] = sil.astype(BF16)
        dsil_ref[...] = (sg + sil * (1.0 - sg)).astype(BF16)
        s_ref[...] = s
        acc_ref[...] += _dot(s, wd_ref[...])

        @pl.when(j == nj - 1)
        def _():
            xo_ref[...] = x_ref[...] + 0.5 * acc_ref[...]

    act = pl.BlockSpec((TM, tf), lambda i, j: (i, j))
    return pl.pallas_call(
        body, name=name, grid=(t // TM, nj),
        in_specs=[pl.BlockSpec((TM, d), lambda i, j: (i, 0)), pl.BlockSpec((1, d), lambda i, j: (0, 0)),
                  pl.BlockSpec((d, tf), lambda i, j: (gq, j)), pl.BlockSpec((d, tf), lambda i, j: (gq + 1, j)),
                  pl.BlockSpec((tf, d), lambda i, j: (j, 0))],
        out_specs=[pl.BlockSpec((TM, d), lambda i, j: (i, 0)), pl.BlockSpec((TM, d), lambda i, j: (i, 0)),
                   act, act, act, act],
        out_shape=[jax.ShapeDtypeStruct((t, d), F32), jax.ShapeDtypeStruct((t, d), BF16)]
        + [jax.ShapeDtypeStruct((t, fp), BF16)] * 4,
        scratch_shapes=[pltpu.VMEM((TM, d), F32)],
        compiler_params=_cparams(2),
    )(x, n, cols, cols, wd)


def _ffn_bwd_act(dxo, b, sil, dsil, wd_t, name):
    t, d = dxo.shape
    fp = wd_t.shape[1]
    tf = fp // 3
    nj = fp // tf

    def body(dxo_ref, b_ref, sil_ref, dsil_ref, wd_ref, da_ref, db_ref, dxob_ref):
        @pl.when(pl.program_id(1) == 0)
        def _():
            dxob_ref[...] = (0.5 * dxo_ref[...]).astype(BF16)

        ds = _dot(dxob_ref[...], wd_ref[...])
        da_ref[...] = (ds * b_ref[...].astype(F32) * dsil_ref[...].astype(F32)).astype(BF16)
        db_ref[...] = (ds * sil_ref[...].astype(F32)).astype(BF16)

    act = pl.BlockSpec((TM, tf), lambda i, j: (i, j))
    return pl.pallas_call(
        body, name=name, grid=(t // TM, nj),
        in_specs=[pl.BlockSpec((TM, d), lambda i, j: (i, 0)), act, act, act, pl.BlockSpec((d, tf), lambda i, j: (0, j))],
        out_specs=[act, act, pl.BlockSpec((TM, d), lambda i, j: (i, 0))],
        out_shape=[jax.ShapeDtypeStruct((t, fp), BF16), jax.ShapeDtypeStruct((t, fp), BF16),
                   jax.ShapeDtypeStruct((t, d), BF16)],
        compiler_params=_cparams(2),
    )(dxo, b, sil, dsil, wd_t)


def _ffn_bwd_in(da, db, dxo, x, n, cols_t, gq, name):
    t, d = x.shape
    fp = cols_t.shape[0]
    tf = 2 * fp // N_DEV
    nj = fp // tf

    def body(da_ref, db_ref, dxo_ref, x_ref, n_ref, wg_ref, wu_ref, dx_ref, dn_ref, acc_ref):
        i, j = pl.program_id(0), pl.program_id(1)

        @pl.when((i == 0) & (j == 0))
        def _():
            dn_ref[...] = jnp.zeros_like(dn_ref)

        @pl.when(j == 0)
        def _():
            acc_ref[...] = jnp.zeros_like(acc_ref)

        acc_ref[...] += _dot(da_ref[...], wg_ref[...]) + _dot(db_ref[...], wu_ref[...])

        @pl.when(j == nj - 1)
        def _():
            xv = x_ref[...]
            r = lax.rsqrt(jnp.mean(xv * xv, axis=-1, keepdims=True) + RMS_EPS)
            xh = xv * r
            dh = acc_ref[...]
            dn_ref[...] += jnp.sum(dh * xh, axis=0, keepdims=True)
            dhn = dh * n_ref[...]
            dx_ref[...] = dxo_ref[...] + r * (dhn - xh * jnp.mean(dhn * xh, axis=-1, keepdims=True))

    act = pl.BlockSpec((TM, tf), lambda i, j: (i, j))
    row = pl.BlockSpec((TM, d), lambda i, j: (i, 0))
    return pl.pallas_call(
        body, name=name, grid=(t // TM, nj),
        in_specs=[act, act, row, row, pl.BlockSpec((1, d), lambda i, j: (0, 0)),
                  pl.BlockSpec((tf, d), lambda i, j: (j, gq)), pl.BlockSpec((tf, d), lambda i, j: (j, gq + 1))],
        out_specs=[row, pl.BlockSpec((1, d), lambda i, j: (0, 0))],
        out_shape=[jax.ShapeDtypeStruct((t, d), F32), jax.ShapeDtypeStruct((1, d), F32)],
        scratch_shapes=[pltpu.VMEM((TM, d), F32)],
        compiler_params=_cparams(2),
    )(da, db, dxo, x, n, cols_t, cols_t)


def _wgrad(a, b, scale, tk, tn, name):
    t, k = a.shape
    n = b.shape[1]
    tt = min(t, WGRAD_TT)
    nt = t // tt

    def body(a_ref, b_ref, o_ref, acc_ref):
        s = pl.program_id(2)

        @pl.when(s == 0)
        def _():
            acc_ref[...] = jnp.zeros_like(acc_ref)

        acc_ref[...] += _dot_tn(a_ref[...], b_ref[...])

        @pl.when(s == nt - 1)
        def _():
            o_ref[...] = (scale * acc_ref[...]).astype(BF16)

    return pl.pallas_call(
        body, name=name, grid=(k // tk, n // tn, nt),
        in_specs=[pl.BlockSpec((tt, tk), lambda p, q, s: (s, p)), pl.BlockSpec((tt, tn), lambda p, q, s: (s, q))],
        out_specs=pl.BlockSpec((tk, tn), lambda p, q, s: (p, q)),
        out_shape=jax.ShapeDtypeStruct((k, n), BF16),
        scratch_shapes=[pltpu.VMEM((tk, tn), F32)],
        compiler_params=_cparams(3),
    )(a, b)


def _mix_in(x, n, w_in, cos_t, sin_t, seq, name):
    t, d = x.shape
    per_seq = seq // TM

    def body(x_ref, n_ref, w_ref, c_ref, s_ref, h_ref, q_ref, k_ref, v_ref, g_ref, u_ref):
        xv = x_ref[...]
        r = lax.rsqrt(jnp.mean(xv * xv, axis=-1, keepdims=True) + RMS_EPS)
        h = (xv * r * n_ref[...]).astype(BF16)
        h_ref[...] = h
        p = _dot(h, w_ref[...])
        c, s = c_ref[...], s_ref[...]
        q = p[:, :QK_W]
        k = p[:, QK_W:2 * QK_W]
        q_ref[...] = ((q * c + _swap_halves(q) * s) * (DK ** -0.5)).astype(BF16)
        k_ref[...] = (k * c + _swap_halves(k) * s).astype(BF16)
        v_ref[...] = p[:, 2 * QK_W:2 * QK_W + V_W].astype(BF16)
        g_ref[...] = p[:, 2 * QK_W + V_W:2 * QK_W + 2 * V_W]
        u_ref[...] = p[:, 2 * QK_W + 2 * V_W:]

    tile = lambda w: pl.BlockSpec((TM, w), lambda i: (i, 0))
    return pl.pallas_call(
        body, name=name, grid=(t // TM,),
        in_specs=[tile(d), pl.BlockSpec((1, d), lambda i: (0, 0)), pl.BlockSpec(w_in.shape, lambda i: (0, 0)),
                  pl.BlockSpec((TM, QK_W), lambda i: (i % per_seq, 0)), pl.BlockSpec((TM, QK_W), lambda i: (i % per_seq, 0))],
        out_specs=[tile(d), tile(QK_W), tile(QK_W), tile(V_W), tile(V_W), tile(POOL_W)],
        out_shape=[jax.ShapeDtypeStruct((t, d), BF16), jax.ShapeDtypeStruct((t, QK_W), BF16),
                   jax.ShapeDtypeStruct((t, QK_W), BF16), jax.ShapeDtypeStruct((t, V_W), BF16),
                   jax.ShapeDtypeStruct((t, V_W), F32), jax.ShapeDtypeStruct((t, POOL_W), F32)],
        compiler_params=_cparams(1),
    )(x, n, w_in, cos_t, sin_t)


def _mix_in_bwd(dp, dx2, x1, n, w_in_t, name):
    t, d = x1.shape

    def body(dp_ref, dx2_ref, x_ref, n_ref, w_ref, dx_ref, dn_ref):
        @pl.when(pl.program_id(0) == 0)
        def _():
            dn_ref[...] = jnp.zeros_like(dn_ref)

        dh = _dot(dp_ref[...], w_ref[...])
        xv = x_ref[...]
        r = lax.rsqrt(jnp.mean(xv * xv, axis=-1, keepdims=True) + RMS_EPS)
        xh = xv * r
        dn_ref[...] += jnp.sum(dh * xh, axis=0, keepdims=True)
        dhn = dh * n_ref[...]
        dx_ref[...] = dx2_ref[...] + r * (dhn - xh * jnp.mean(dhn * xh, axis=-1, keepdims=True))

    tile = lambda w: pl.BlockSpec((TM, w), lambda i: (i, 0))
    return pl.pallas_call(
        body, name=name, grid=(t // TM,),
        in_specs=[tile(dp.shape[1]), tile(d), tile(d), pl.BlockSpec((1, d), lambda i: (0, 0)),
                  pl.BlockSpec(w_in_t.shape, lambda i: (0, 0))],
        out_specs=[tile(d), pl.BlockSpec((1, d), lambda i: (0, 0))],
        out_shape=[jax.ShapeDtypeStruct((t, d), F32), jax.ShapeDtypeStruct((1, d), F32)],
        compiler_params=_cparams(1),
    )(dp, dx2, x1, n, w_in_t)


def _group_norm(o):
    parts, rstds = [], []
    for h in range(HEADS):
        oh = o[:, h * DV:(h + 1) * DV]
        dlt = oh - jnp.mean(oh, axis=-1, keepdims=True)
        rstd = lax.rsqrt(jnp.mean(dlt * dlt, axis=-1, keepdims=True) + GN_EPS)
        parts.append(dlt * rstd)
        rstds.append(rstd)
    return jnp.concatenate(parts, axis=1), rstds


def _mix_core_fwd(qs, k, v, g, u, x1, consts, gain, wp, scale, w_out, nseq, seq, name):
    t, d = x1.shape
    nblk = seq // BLK
    mask, dq, dk, gbd, bd = consts

    def body(q_ref, k_ref, v_ref, g_ref, u_ref, x1_ref, m_ref, dq_ref, dk_ref, gbd_ref, bd_ref, gain_ref, wp_ref,
             sc_ref, wo_ref, x2_ref, mix_ref, o_ref, pooled_ref, st_ref, state, halo):
        j = pl.program_id(1)

        @pl.when(j == 0)
        def _():
            state[...] = jnp.zeros_like(state)
            halo[...] = jnp.zeros_like(halo)

        qv, kv, vv = q_ref[...], k_ref[...], v_ref[...]
        st = state[...]
        st_ref[0] = st
        cross = _dot(qv, st.astype(BF16)) * dq_ref[...]
        outs = []
        for h in range(HEADS):
            qh = jnp.where(_head_mask(h), qv, jnp.zeros_like(qv))
            am = (_dot_nt(qh, kv) * m_ref[h]).astype(BF16)
            outs.append(_dot(am, vv[:, h * DV:(h + 1) * DV]))
        o = jnp.concatenate(outs, axis=1) + cross
        o_ref[...] = o
        kd = (kv.astype(F32) * dk_ref[...]).astype(BF16)
        state[...] = gbd_ref[...] * st + _dot_tn(kd, vv) * bd_ref[...]

        gv = g_ref[...]
        nrm, _ = _group_norm(o)
        ret = (gv * _sigmoid(gv)) * (nrm * gain_ref[...])

        uv = u_ref[...]
        c = jnp.concatenate([halo[...], uv], axis=0)
        halo[...] = uv[BLK - HALO:, :]
        pos = j * BLK + lax.broadcasted_iota(jnp.int32, (BLK, 1), 0)
        parts = []
        for gi, w in enumerate(WINDOWS):
            c = c + pltpu.roll(c, w // 2, 0)
            cnt = jnp.minimum(pos + 1, w).astype(F32)
            parts.append(c[HALO:, :GC] / cnt)
            if gi + 1 < len(WINDOWS):
                c = c[:, GC:]
        pooled = (jnp.concatenate(parts, axis=1) - uv).astype(BF16)
        pooled_ref[...] = pooled
        z = jnp.concatenate([_dot(pooled[:, gi * GC:(gi + 1) * GC], wp_ref[gi]) for gi in range(len(WINDOWS))], axis=1)
        mix = jnp.concatenate([ret, z * sc_ref[...]], axis=1).astype(BF16)
        mix_ref[...] = mix
        x2_ref[...] = x1_ref[...] + _dot(mix, wo_ref[...])

    blk = lambda w: pl.BlockSpec((BLK, w), lambda i, j: (i * nblk + j, 0))
    full = lambda a: pl.BlockSpec(a.shape, lambda i, j: (0,) * a.ndim)
    return pl.pallas_call(
        body, name=name, grid=(nseq, nblk),
        in_specs=[blk(QK_W), blk(QK_W), blk(V_W), blk(V_W), blk(POOL_W), blk(d),
                  full(mask), full(dq), full(dk), full(gbd), full(bd), full(gain), full(wp), full(scale), full(w_out)],
        out_specs=[blk(d), blk(d), blk(V_W), blk(POOL_W),
                   pl.BlockSpec((1, QK_W, V_W), lambda i, j: (i * nblk + j, 0, 0))],
        out_shape=[jax.ShapeDtypeStruct((t, d), F32), jax.ShapeDtypeStruct((t, d), BF16),
                   jax.ShapeDtypeStruct((t, V_W), F32), jax.ShapeDtypeStruct((t, POOL_W), BF16),
                   jax.ShapeDtypeStruct((nseq * nblk, QK_W, V_W), F32)],
        scratch_shapes=[pltpu.VMEM((QK_W, V_W), F32), pltpu.VMEM((HALO, POOL_W), F32)],
        compiler_params=_cparams(2),
    )(qs, k, v, g, u, x1, mask, dq, dk, gbd, bd, gain, wp, scale, w_out)


def _mix_core_bwd(dx2, qs, k, v, g, o, pooled, st, consts, gain, wp, scale, w_out, cos_t, sin_t, nseq, seq, name):
    t, d = dx2.shape
    nblk = seq // BLK
    mask, dq, dk, gbd, bd = consts
    n_win = len(WINDOWS)

    def body(dx2_ref, q_ref, k_ref, v_ref, g_ref, o_ref, pooled_ref, st_ref, m_ref, dq_ref, dk_ref, gbd_ref, bd_ref,
             gain_ref, wp_ref, sc_ref, wo_ref, c_ref, s_ref,
             dp_ref, dx2b_ref, dgain_ref, dscale_ref, dwp_ref, rstate, carry):
        i, j = pl.program_id(0), pl.program_id(1)

        @pl.when((i == 0) & (j == 0))
        def _():
            dgain_ref[...] = jnp.zeros_like(dgain_ref)
            dscale_ref[...] = jnp.zeros_like(dscale_ref)
            dwp_ref[...] = jnp.zeros_like(dwp_ref)

        @pl.when(j == 0)
        def _():
            rstate[...] = jnp.zeros_like(rstate)
            carry[...] = jnp.zeros_like(carry)

        dx2b = dx2_ref[...].astype(BF16)
        dx2b_ref[...] = dx2b
        dmix = _dot(dx2b, wo_ref[...])
        dret, dpool = dmix[:, :V_W], dmix[:, V_W:]

        gv, ov, gain_v = g_ref[...], o_ref[...], gain_ref[...]
        sg = _sigmoid(gv)
        sil = gv * sg
        nrm, rstds = _group_norm(ov)
        dg = dret * (nrm * gain_v) * (sg * (1.0 + gv * (1.0 - sg)))
        dgn = dret * sil
        dgain_ref[...] += jnp.sum(dgn * nrm, axis=0, keepdims=True)
        dnrm = dgn * gain_v
        do_parts = []
        for h in range(HEADS):
            dn_h = dnrm[:, h * DV:(h + 1) * DV]
            n_h = nrm[:, h * DV:(h + 1) * DV]
            do_parts.append(rstds[h] * (dn_h - jnp.mean(dn_h, axis=-1, keepdims=True)
                                        - n_h * jnp.mean(dn_h * n_h, axis=-1, keepdims=True)))
        do = jnp.concatenate(do_parts, axis=1)
        dob = do.astype(BF16)

        qv, kv, vv = q_ref[...], k_ref[...], v_ref[...]
        stb = st_ref[0].astype(BF16)
        rs = rstate[...]
        rsb = rs.astype(BF16)
        dod = (do * dq_ref[...]).astype(BF16)
        dqs = _dot_nt(dod, stb)
        dst = _dot_tn(qv, dod) * bd_ref[...]
        dkf = dk_ref[...]
        kd = (kv.astype(F32) * dkf).astype(BF16)
        dks = _dot_nt(vv, rsb) * dkf
        dvs = _dot(kd, rsb)
        dv_parts = []
        for h in range(HEADS):
            hm = _head_mask(h)
            qh = jnp.where(hm, qv, jnp.zeros_like(qv))
            mh = m_ref[h]
            am = (_dot_nt(qh, kv) * mh).astype(BF16)
            dpm = (_dot_nt(dob[:, h * DV:(h + 1) * DV], vv[:, h * DV:(h + 1) * DV]) * mh).astype(BF16)
            dqs = dqs + jnp.where(hm, _dot(dpm, kv), 0.0)
            dks = dks + jnp.where(hm, _dot_tn(dpm, qv), 0.0)
            dv_parts.append(_dot_tn(am, dob[:, h * DV:(h + 1) * DV]))
        dvs = dvs + jnp.concatenate(dv_parts, axis=1)
        rstate[...] = dst + gbd_ref[...] * rs

        cv, sv = c_ref[...], s_ref[...]
        dqr = dqs * (DK ** -0.5)
        dq_pre = dqr * cv + _swap_halves(dqr * sv)
        dk_pre = dks * cv + _swap_halves(dks * sv)

        pv = pooled_ref[...]
        sc = sc_ref[...]
        dzb = (dpool * sc).astype(BF16)
        z_parts, dpo_parts = [], []
        for gi in range(n_win):
            p_g = pv[:, gi * GC:(gi + 1) * GC]
            dz_g = dzb[:, gi * GC:(gi + 1) * GC]
            z_parts.append(_dot(p_g, wp_ref[gi]))
            dwp_ref[gi] += _dot_tn(p_g, dz_g)
            dpo_parts.append(_dot_nt(dz_g, wp_ref[gi]))
        dscale_ref[...] += jnp.sum(dpool * jnp.concatenate(z_parts, axis=1), axis=0, keepdims=True)
        dpo = jnp.concatenate(dpo_parts, axis=1)
        pos = (nblk - 1 - j) * BLK + lax.broadcasted_iota(jnp.int32, (BLK, 1), 0)
        e = jnp.concatenate(
            [dpo[:, gi * GC:(gi + 1) * GC] / jnp.minimum(pos + 1, w).astype(F32) for gi, w in enumerate(WINDOWS)], axis=1)
        c = jnp.concatenate([e, carry[...]], axis=0)
        carry[...] = e[:HALO, :]
        rows = BLK + HALO
        lead = []
        for gi, w in enumerate(WINDOWS):
            c = c + pltpu.roll(c, rows - w // 2, 0)
            lead.append(c[:BLK, :GC])
            if gi + 1 < n_win:
                c = c[:, GC:]
        du = jnp.concatenate(lead, axis=1) - dpo

        dp_ref[:, 0:QK_W] = dq_pre.astype(BF16)
        dp_ref[:, QK_W:2 * QK_W] = dk_pre.astype(BF16)
        dp_ref[:, 2 * QK_W:2 * QK_W + V_W] = dvs.astype(BF16)
        dp_ref[:, 2 * QK_W + V_W:2 * QK_W + 2 * V_W] = dg.astype(BF16)
        dp_ref[:, 2 * QK_W + 2 * V_W:] = du.astype(BF16)

    rev = lambda i, j: i * nblk + (nblk - 1 - j)
    blk = lambda w: pl.BlockSpec((BLK, w), lambda i, j: (rev(i, j), 0))
    full = lambda a: pl.BlockSpec(a.shape, lambda i, j: (0,) * a.ndim)
    in_w = 2 * QK_W + 2 * V_W + POOL_W
    return pl.pallas_call(
        body, name=name, grid=(nseq, nblk),
        in_specs=[blk(d), blk(QK_W), blk(QK_W), blk(V_W), blk(V_W), blk(V_W), blk(POOL_W),
                  pl.BlockSpec((1, QK_W, V_W), lambda i, j: (rev(i, j), 0, 0)),
                  full(mask), full(dq), full(dk), full(gbd), full(bd), full(gain), full(wp), full(scale), full(w_out),
                  pl.BlockSpec((BLK, QK_W), lambda i, j: (nblk - 1 - j, 0)),
                  pl.BlockSpec((BLK, QK_W), lambda i, j: (nblk - 1 - j, 0))],
        out_specs=[blk(in_w), blk(d), pl.BlockSpec((1, V_W), lambda i, j: (0, 0)),
                   pl.BlockSpec((1, POOL_W), lambda i, j: (0, 0)), pl.BlockSpec((n_win, GC, GC), lambda i, j: (0, 0, 0))],
        out_shape=[jax.ShapeDtypeStruct((t, in_w), BF16), jax.ShapeDtypeStruct((t, d), BF16),
                   jax.ShapeDtypeStruct((1, V_W), F32), jax.ShapeDtypeStruct((1, POOL_W), F32),
                   jax.ShapeDtypeStruct((n_win, GC, GC), F32)],
        scratch_shapes=[pltpu.VMEM((QK_W, V_W), F32), pltpu.VMEM((HALO, POOL_W), F32)],
        compiler_params=_cparams(2),
    )(dx2, qs, k, v, g, o, pooled, st, mask, dq, dk, gbd, bd, gain, wp, scale, w_out, cos_t, sin_t)


def _loss_head(x3, nf, tgt, name):
    t, d = x3.shape

    def body(x_ref, n_ref, t_ref, dx_ref, dn_ref, loss_ref):
        @pl.when(pl.program_id(0) == 0)
        def _():
            dn_ref[...] = jnp.zeros_like(dn_ref)
            loss_ref[...] = jnp.zeros_like(loss_ref)

        xv = x_ref[...]
        nv = n_ref[...]
        r = lax.rsqrt(jnp.mean(xv * xv, axis=-1, keepdims=True) + RMS_EPS)
        xh = xv * r
        err = xh * nv - t_ref[...]
        row = jnp.mean(err * err, axis=-1, keepdims=True)
        loss_ref[...] += 0.5 * jnp.sum(row, axis=0, keepdims=True)
        dy = err * (1.0 / d)
        dn_ref[...] += jnp.sum(dy * xh, axis=0, keepdims=True)
        dxh = dy * nv
        dx_ref[...] = r * (dxh - xh * jnp.mean(dxh * xh, axis=-1, keepdims=True))

    tile = pl.BlockSpec((TM, d), lambda i: (i, 0))
    return pl.pallas_call(
        body, name=name, grid=(t // TM,),
        in_specs=[tile, pl.BlockSpec((1, d), lambda i: (0, 0)), tile],
        out_specs=[tile, pl.BlockSpec((1, d), lambda i: (0, 0)), pl.BlockSpec((1, 1), lambda i: (0, 0))],
        out_shape=[jax.ShapeDtypeStruct((t, d), F32), jax.ShapeDtypeStruct((1, d), F32), jax.ShapeDtypeStruct((1, 1), F32)],
        compiler_params=_cparams(1),
    )(x3, nf, tgt)


def _coords():
    return lax.axis_index("x"), lax.axis_index("y"), lax.axis_index("c")


def _window(ref, kind, idx, size):
    if kind == "col":
        return ref.at[:, pl.ds(pl.multiple_of(idx * size, LANE), size)]
    return ref.at[pl.ds(pl.multiple_of(idx * size, 8), size), :]


def _all_gather(parts, name):
    n = len(parts)
    kinds = [kd for _, kd in parts]
    sizes = [a.shape[1] if kd == "col" else a.shape[0] for a, kd in parts]

    def body(*refs):
        ins, outs = refs[:n], refs[n:2 * n]
        send_sems, recv_sems, local_sems = refs[2 * n:]
        x, y, c = _coords()
        me, sibling = (x, y, c), (x, y, 1 - c)
        chips = [(1 - x, y), (x, 1 - y), (1 - x, 1 - y)]

        def win(p, dev):
            return _window(outs[p], kinds[p], 4 * dev[0] + 2 * dev[1] + dev[2], sizes[p])

        def copy(p, k, block, to, src=None):
            return pltpu.make_async_remote_copy(
                src_ref=win(p, block) if src is None else src, dst_ref=win(p, block),
                send_sem=send_sems.at[p * 7 + k], recv_sem=recv_sems.at[p * 7 + k], device_id=to, device_id_type=MESH_ID)

        mine = [pltpu.make_async_copy(ins[p], win(p, me), local_sems.at[p]) for p in range(n)]
        for cp in mine:
            cp.start()
        first = []
        for p in range(n):
            first.append(copy(p, 0, me, sibling, src=ins[p]))
            first += [copy(p, 1 + q, me, (*chip, c), src=ins[p]) for q, chip in enumerate(chips)]
        for cp in first:
            cp.start()
        passed = []
        for q, chip in enumerate(chips):
            for p in range(n):
                copy(p, 1 + q, (*chip, c), me).wait_recv()
                fwd = copy(p, 4 + q, (*chip, c), sibling)
                fwd.start()
                passed.append(fwd)
        for p in range(n):
            copy(p, 0, sibling, me).wait_recv()
            for q, chip in enumerate(chips):
                copy(p, 4 + q, (*chip, 1 - c), me).wait_recv()
        for cp in first + passed:
            cp.wait_send()
        for cp in mine:
            cp.wait()

    out_shape = [jax.ShapeDtypeStruct((a.shape[0], N_DEV * a.shape[1]) if kd == "col" else (N_DEV * a.shape[0], a.shape[1]),
                                      a.dtype) for a, kd in parts]
    return pl.pallas_call(
        body, name=name, in_specs=[ANY] * n, out_specs=[ANY] * n, out_shape=out_shape,
        scratch_shapes=[pltpu.SemaphoreType.DMA((7 * n,)), pltpu.SemaphoreType.DMA((7 * n,)), pltpu.SemaphoreType.DMA((n,))],
    )(*[a for a, _ in parts])


def _shard_shape(a, kd):
    return (a.shape[0], a.shape[1] // N_DEV) if kd == "col" else (a.shape[0] // N_DEV, a.shape[1])


def _rs_pair(grads, name):
    n = len(grads)
    kinds = [kd for _, kd in grads]
    shapes = [_shard_shape(a, kd) for a, kd in grads]

    def body(*refs):
        ins, outs = refs[:n], refs[n:2 * n]
        send_sems, recv_sems = refs[2 * n:]
        x, y, c = _coords()
        copies = []
        for p in range(n):
            size = shapes[p][1] if kinds[p] == "col" else shapes[p][0]
            for s in range(4):
                src = _window(ins[p], kinds[p], 2 * s + (1 - c), size)
                copies.append(pltpu.make_async_remote_copy(
                    src_ref=src, dst_ref=outs[p].at[s], send_sem=send_sems.at[4 * p + s], recv_sem=recv_sems.at[4 * p + s],
                    device_id=(x, y, 1 - c), device_id_type=MESH_ID))
        for cp in copies:
            cp.start()
        for cp in copies:
            cp.wait_recv()
        for cp in copies:
            cp.wait_send()

    return pl.pallas_call(
        body, name=name, in_specs=[ANY] * n, out_specs=[ANY] * n,
        out_shape=[jax.ShapeDtypeStruct((4,) + shapes[p], BF16) for p in range(n)],
        scratch_shapes=[pltpu.SemaphoreType.DMA((4 * n,)), pltpu.SemaphoreType.DMA((4 * n,))],
    )(*[a for a, _ in grads])


def _rs_chips(sums, name):
    n = len(sums)

    def body(*refs):
        ins, outs = refs[:n], refs[n:2 * n]
        send_sems, recv_sems = refs[2 * n:]
        x, y, c = _coords()
        chips = [(1 - x, y), (x, 1 - y), (1 - x, 1 - y)]
        copies = []
        for p in range(n):
            for q, (cx, cy) in enumerate(chips):
                copies.append(pltpu.make_async_remote_copy(
                    src_ref=ins[p].at[2 * cx + cy], dst_ref=outs[p].at[q],
                    send_sem=send_sems.at[3 * p + q], recv_sem=recv_sems.at[3 * p + q],
                    device_id=(cx, cy, c), device_id_type=MESH_ID))
        for cp in copies:
            cp.start()
        for cp in copies:
            cp.wait_recv()
        for cp in copies:
            cp.wait_send()

    return pl.pallas_call(
        body, name=name, in_specs=[ANY] * n, out_specs=[ANY] * n,
        out_shape=[jax.ShapeDtypeStruct((3,) + a.shape[1:], BF16) for a in sums],
        scratch_shapes=[pltpu.SemaphoreType.DMA((3 * n,)), pltpu.SemaphoreType.DMA((3 * n,))],
    )(*sums)


def _pair_sum(grad, kd, recv, core, name):
    _, r, cw = recv.shape
    tr = min(r, TM)

    def body(core_ref, g_ref, r_ref, o_ref):
        del core_ref
        o_ref[0] = (g_ref[...].astype(F32) + r_ref[0].astype(F32)).astype(BF16)

    if kd == "col":
        g_spec = pl.BlockSpec((tr, cw), lambda s, i, core_ref: (i, 2 * s + core_ref[0]))
    else:
        g_spec = pl.BlockSpec((tr, cw), lambda s, i, core_ref: ((2 * s + core_ref[0]) * (r // tr) + i, 0))
    grid_spec = pltpu.PrefetchScalarGridSpec(
        num_scalar_prefetch=1, grid=(4, r // tr),
        in_specs=[g_spec, pl.BlockSpec((1, tr, cw), lambda s, i, core_ref: (s, i, 0))],
        out_specs=pl.BlockSpec((1, tr, cw), lambda s, i, core_ref: (s, i, 0)))
    return pl.pallas_call(
        body, name=name, grid_spec=grid_spec, out_shape=jax.ShapeDtypeStruct(recv.shape, BF16),
        compiler_params=_cparams(2),
    )(core, grad, recv)


def _adam_math(w, g, m, v):
    m2 = B1 * m + (1.0 - B1) * g
    v2 = B2 * v + (1.0 - B2) * (g * g)
    m_hat = m2 / (1.0 - B1 ** STEP)
    v_hat = v2 / (1.0 - B2 ** STEP)
    delta = -LR * (m_hat / (jnp.sqrt(v_hat) + ADAM_EPS) + WD * w)
    return delta, m2, v2


def _chip_sum_adam(psum, recv, chip, w, m, v, name):
    r, cw = w.shape
    pc = psum.shape[2]
    tr = min(r, TM)

    def body(chip_ref, p_ref, r_ref, w_ref, m_ref, v_ref, g_ref, d_ref, m2_ref, v2_ref):
        del chip_ref
        g = p_ref[0].astype(F32) + r_ref[0].astype(F32) + r_ref[1].astype(F32) + r_ref[2].astype(F32)
        g = g[:, :cw]
        delta, m2, v2 = _adam_math(w_ref[...], g, m_ref[...], v_ref[...])
        g_ref[...] = g
        d_ref[...] = delta
        m2_ref[...] = m2
        v2_ref[...] = v2

    loc = pl.BlockSpec((tr, cw), lambda i, chip_ref: (i, 0))
    grid_spec = pltpu.PrefetchScalarGridSpec(
        num_scalar_prefetch=1, grid=(r // tr,),
        in_specs=[pl.BlockSpec((1, tr, pc), lambda i, chip_ref: (chip_ref[0], i, 0)),
                  pl.BlockSpec((3, tr, pc), lambda i, chip_ref: (0, i, 0)), loc, loc, loc],
        out_specs=[loc, loc, loc, loc])
    return pl.pallas_call(
        body, name=name, grid_spec=grid_spec, out_shape=[jax.ShapeDtypeStruct((r, cw), F32)] * 4,
        compiler_params=_cparams(1),
    )(chip, psum, recv, w, m, v)


def _small_allreduce_adam(partials, params, moms, vels, name):
    n = len(partials)
    row0 = []
    rows = 0
    for a in partials:
        row0.append(rows)
        rows += _pad_to(a.shape[0], 8)
    width = max(a.shape[1] for a in partials)

    def body(*refs):
        g_in = refs[:n]
        w_in, m_in, v_in = refs[n:2 * n], refs[2 * n:3 * n], refs[3 * n:4 * n]
        outs = refs[4 * n:8 * n]
        slab, send_sems, recv_sems = refs[8 * n:]
        x, y, c = _coords()
        me = 4 * x + 2 * y + c
        slab[me] = jnp.zeros((rows, width), F32)
        for p in range(n):
            r, cw = partials[p].shape
            slab[me, row0[p]:row0[p] + r, 0:cw] = g_in[p][...]
        copies = []
        for q in range(1, N_DEV):
            peer = me ^ q
            copies.append(pltpu.make_async_remote_copy(
                src_ref=slab.at[me], dst_ref=slab.at[me], send_sem=send_sems.at[q - 1], recv_sem=recv_sems.at[q - 1],
                device_id=(peer >> 2, (peer >> 1) & 1, peer & 1), device_id_type=MESH_ID))
        for cp in copies:
            cp.start()
        for cp in copies:
            cp.wait_recv()
        for cp in copies:
            cp.wait_send()
        for p in range(n):
            r, cw = partials[p].shape
            g = slab[0, row0[p]:row0[p] + r, 0:cw]
            for dev in range(1, N_DEV):
                g = g + slab[dev, row0[p]:row0[p] + r, 0:cw]
            delta, m2, v2 = _adam_math(w_in[p][...], g, m_in[p][...], v_in[p][...])
            outs[4 * p][...] = g
            outs[4 * p + 1][...] = delta
            outs[4 * p + 2][...] = m2
            outs[4 * p + 3][...] = v2

    out_shape = []
    for a in partials:
        out_shape += [jax.ShapeDtypeStruct(a.shape, F32)] * 4
    return pl.pallas_call(
        body, name=name, in_specs=[VMEM_SPEC] * (4 * n), out_specs=[VMEM_SPEC] * (4 * n), out_shape=out_shape,
        scratch_shapes=[pltpu.VMEM((N_DEV, rows, width), F32), pltpu.SemaphoreType.DMA((N_DEV - 1,)),
                        pltpu.SemaphoreType.DMA((N_DEV - 1,))],
    )(*partials, *params, *moms, *vels)


def _local_step(xf, tgt, nseq, seq, cols_all, win_all, d1_all, d2_all, wout_all, small_w):
    d = xf.shape[1]
    n1, n2, gain, pool_w, pool_scale, n3, nf = small_w
    tf = 2 * cols_all.shape[1] // N_DEV
    consts = _retention_constants()
    cos_t, sin_t = _rotary_tables(seq)
    wp_b = pool_w.astype(BF16)

    x1, h1, b1, sil1, dsil1, s1 = _ffn_fwd(xf, n1, cols_all, 0, d1_all, "ffn1_fwd")
    h2, qs, kr, vv, gg, uu = _mix_in(x1, n2, win_all, cos_t, sin_t, seq, "mix_in")
    x2, mix, oo, pooled, states = _mix_core_fwd(qs, kr, vv, gg, uu, x1, consts, gain, wp_b, pool_scale, wout_all,
                                                 nseq, seq, "mix_core_fwd")
    x3, h3, b3, sil3, dsil3, s3 = _ffn_fwd(x2, n3, cols_all, 2, d2_all, "ffn2_fwd")
    dx3, dnf, loss_part = _loss_head(x3, nf, tgt, "loss_head")

    cols_t = _transpose(cols_all, "transpose_cols")
    d1_t = _transpose(d1_all, "transpose_down1")
    d2_t = _transpose(d2_all, "transpose_down2")
    win_t = _transpose(win_all, "transpose_w_in")
    wout_t = _transpose(wout_all, "transpose_w_out")
    da3, db3, dx3b = _ffn_bwd_act(dx3, b3, sil3, dsil3, d2_t, "ffn2_bwd_act")
    dx2, dn3 = _ffn_bwd_in(da3, db3, dx3, x2, n3, cols_t, 2, "ffn2_bwd_in")
    g_wg2 = _wgrad(h3, da3, 1.0, d, tf, "wgrad_gate2")
    g_wu2 = _wgrad(h3, db3, 1.0, d, tf, "wgrad_up2")
    g_wd2 = _wgrad(s3, dx3b, 1.0, tf, d, "wgrad_down2")
    dp, dx2b, dgain, dscale, dwp = _mix_core_bwd(dx2, qs, kr, vv, gg, oo, pooled, states, consts, gain, wp_b,
                                                 pool_scale, wout_t, cos_t, sin_t, nseq, seq, "mix_core_bwd")
    g_wout = _wgrad(mix, dx2b, 1.0, d, d, "wgrad_out")
    g_win = _wgrad(h2, dp, 1.0, d, d, "wgrad_in")
    dx1, dn2 = _mix_in_bwd(dp, dx2, x1, n2, win_t, "mix_in_bwd")
    da1, db1, dx1b = _ffn_bwd_act(dx1, b1, sil1, dsil1, d1_t, "ffn1_bwd_act")
    dx0, dn1 = _ffn_bwd_in(da1, db1, dx1, xf, n1, cols_t, 0, "ffn1_bwd_in")
    g_wg1 = _wgrad(h1, da1, 1.0, d, tf, "wgrad_gate1")
    g_wu1 = _wgrad(h1, db1, 1.0, d, tf, "wgrad_up1")
    g_wd1 = _wgrad(s1, dx1b, 1.0, tf, d, "wgrad_down1")
    grads = [(g_wg1, "col"), (g_wu1, "col"), (g_wd1, "row"), (g_win, "col"), (g_wout, "row"),
             (g_wg2, "col"), (g_wu2, "col"), (g_wd2, "row")]
    return loss_part, dx0, grads, (dn1, dn2, dgain, dwp, dscale, dn3, dnf)


def kernel(x, norm_ffn1, ffn1_gate, ffn1_up, ffn1_down, norm_mix, w_in, ret_gn_gain, pool_w, pool_scale, w_out, norm_ffn2, ffn2_gate, ffn2_up, ffn2_down, norm_final, loss_target, m_norm_ffn1, m_ffn1_gate, m_ffn1_up, m_ffn1_down, m_norm_mix, m_w_in, m_ret_gn_gain, m_pool_w, m_pool_scale, m_w_out, m_norm_ffn2, m_ffn2_gate, m_ffn2_up, m_ffn2_down, m_norm_final, v_norm_ffn1, v_ffn1_gate, v_ffn1_up, v_ffn1_down, v_norm_mix, v_w_in, v_ret_gn_gain, v_pool_w, v_pool_scale, v_w_out, v_norm_ffn2, v_ffn2_gate, v_ffn2_up, v_ffn2_down, v_norm_final):
    nseq, seq, d = x.shape
    t = nseq * seq
    f_loc = ffn1_gate.shape[2]
    f_pad = _pad_to(f_loc, LANE)
    xf = x.reshape(t, d)
    tgt = loss_target.reshape(t, d)
    core = lax.axis_index("c").astype(jnp.int32).reshape(1)
    chip = (2 * lax.axis_index("x") + lax.axis_index("y")).astype(jnp.int32).reshape(1)

    colp = lambda w: jnp.pad(w[0].astype(BF16), ((0, 0), (0, f_pad - f_loc)))
    rowp = lambda w: jnp.pad(w[0].astype(BF16), ((0, f_pad - f_loc), (0, 0)))
    ffn_cols = jnp.concatenate([colp(ffn1_gate), colp(ffn1_up), colp(ffn2_gate), colp(ffn2_up)], axis=0)
    cols_all, win_all, d1_all, d2_all, wout_all = _all_gather(
        [(ffn_cols, "col"), (w_in[0].astype(BF16), "col"), (rowp(ffn1_down), "row"), (rowp(ffn2_down), "row"),
         (w_out[0].astype(BF16), "row")], "all_gather_weights")

    small_w = (norm_ffn1, norm_mix, ret_gn_gain, pool_w[0], pool_scale, norm_ffn2, norm_final.reshape(1, d))
    loss_part, dx0, grads, small_parts = _local_step(xf, tgt, nseq, seq, cols_all, win_all, d1_all, d2_all, wout_all, small_w)
    dn1, dn2, dgain, dwp, dscale, dn3, dnf = small_parts

    names = ["ffn1_gate", "ffn1_up", "ffn1_down", "w_in", "w_out", "ffn2_gate", "ffn2_up", "ffn2_down"]
    local = [(ffn1_gate, m_ffn1_gate, v_ffn1_gate), (ffn1_up, m_ffn1_up, v_ffn1_up), (ffn1_down, m_ffn1_down, v_ffn1_down),
             (w_in, m_w_in, v_w_in), (w_out, m_w_out, v_w_out), (ffn2_gate, m_ffn2_gate, v_ffn2_gate),
             (ffn2_up, m_ffn2_up, v_ffn2_up), (ffn2_down, m_ffn2_down, v_ffn2_down)]
    pair_recv = _rs_pair(grads, "rs_pair")
    pair_sums = [_pair_sum(g, kd, rcv, core, "pair_sum_" + nm) for (g, kd), rcv, nm in zip(grads, pair_recv, names)]
    chip_recv = _rs_chips(pair_sums, "rs_chips")
    big = {}
    for ps, rcv, (w, m, v), nm in zip(pair_sums, chip_recv, local, names):
        g, dlt, m2, v2 = _chip_sum_adam(ps, rcv, chip, w[0], m[0], v[0], "adam_" + nm)
        big[nm] = tuple(a[None] for a in (g, dlt, m2, v2))

    small_names = ["norm_ffn1", "norm_mix", "ret_gn_gain", "pool_w", "pool_scale", "norm_ffn2", "norm_final"]
    flat = lambda a: a.reshape(pool_w.size // d, d)
    partials = [dn1, dn2, dgain, flat(dwp), dscale, dn3, dnf]
    params = [norm_ffn1, norm_mix, ret_gn_gain, flat(pool_w), pool_scale, norm_ffn2, norm_final.reshape(1, d)]
    moms = [m_norm_ffn1, m_norm_mix, m_ret_gn_gain, flat(m_pool_w), m_pool_scale, m_norm_ffn2, m_norm_final.reshape(1, d)]
    vels = [v_norm_ffn1, v_norm_mix, v_ret_gn_gain, flat(v_pool_w), v_pool_scale, v_norm_ffn2, v_norm_final.reshape(1, d)]
    small_out = _small_allreduce_adam(partials, params, moms, vels, "small_allreduce_adam")
    shapes = [norm_ffn1.shape, norm_mix.shape, ret_gn_gain.shape, pool_w.shape, pool_scale.shape, norm_ffn2.shape,
              norm_final.shape]
    small = {nm: tuple(small_out[4 * p + q].reshape(shapes[p]) for q in range(4)) for p, nm in enumerate(small_names)}

    loss = lax.psum(loss_part[0, 0], ("x", "y", "c"))
    order = ["norm_ffn1", "ffn1_gate", "ffn1_up", "ffn1_down", "norm_mix", "w_in", "ret_gn_gain", "pool_w", "pool_scale",
             "w_out", "norm_ffn2", "ffn2_gate", "ffn2_up", "ffn2_down", "norm_final"]
    both = {**big, **small}
    outs = [loss, dx0.reshape(nseq, seq, d)]
    for q in range(4):
        outs += [both[nm][q] for nm in order]
    return tuple(outs)
```

```python
import functools

import numpy as np
import jax
import jax.numpy as jnp
from jax import lax
from jax.experimental import pallas as pl
from jax.experimental.pallas import tpu as pltpu

F32, BF16 = jnp.float32, jnp.bfloat16
MESH_ID = pl.DeviceIdType.MESH
ANY = pl.BlockSpec(memory_space=pl.ANY)
VMEM_SPEC = pl.BlockSpec(memory_space=pltpu.VMEM)

N_DEV = 8
RMS_EPS = 1e-6
GN_EPS = 1e-5
HEADS, DK, DV = 4, 64, 128
QK_W, V_W, POOL_W = HEADS * DK, HEADS * DV, 512
WINDOWS = (2, 4, 8, 16)
GC = POOL_W // len(WINDOWS)
CHUNK = 64
BLK = 4 * CHUNK
HALO = 16
ROPE_BASE = 10000.0
LR, B1, B2, ADAM_EPS, WD, STEP = 0.001, 0.9, 0.999, 1e-08, 0.01, 10
LANE = 128
TM = 512
FFN_TM = 1024
FFN_FWD_TF = 512
WGRAD_TT = 2048
VMEM_LIMIT = 56 * 1024 * 1024


def _cparams(n_axes):
    return pltpu.CompilerParams(dimension_semantics=("arbitrary",) * n_axes, vmem_limit_bytes=VMEM_LIMIT)


def _dot(a, b):
    return jnp.dot(a, b, preferred_element_type=F32)


def _dot_nt(a, b):
    return lax.dot_general(a, b, (((1,), (1,)), ((), ())), preferred_element_type=F32)


def _dot_tn(a, b):
    return lax.dot_general(a, b, (((0,), (0,)), ((), ())), preferred_element_type=F32)


def _sigmoid(x):
    return 0.5 * jnp.tanh(0.5 * x) + 0.5


def _transpose(w, name):
    r, c = w.shape
    tb = 512

    def body(x_ref, o_ref):
        o_ref[...] = x_ref[...].T

    return pl.pallas_call(
        body, name=name, grid=(r // tb, c // tb),
        in_specs=[pl.BlockSpec((tb, tb), lambda i, j: (i, j))],
        out_specs=pl.BlockSpec((tb, tb), lambda i, j: (j, i)),
        out_shape=jax.ShapeDtypeStruct((c, r), w.dtype),
        compiler_params=_cparams(2),
    )(w)


def _pad_to(n, m):
    return (n + m - 1) // m * m


def _retention_constants():
    gamma = (1.0 - 2.0 ** (-5.0 - np.arange(HEADS, dtype=np.float32))).astype(np.float32)
    log_g = np.log(gamma).astype(np.float32)
    i = np.arange(BLK)
    diff = (i[:, None] - i[None, :]).astype(np.float32)
    same = (i[:, None] // CHUNK) == (i[None, :] // CHUNK)
    earlier = (i[None, :] // CHUNK) < (i[:, None] // CHUNK)
    mask = np.zeros((HEADS, BLK, BLK), np.float32)
    for h in range(HEADS):
        dec_abs = np.exp(log_g[h] * np.abs(diff)).astype(np.float32)
        dec = np.exp(log_g[h] * diff * earlier).astype(np.float32)
        mask[h] = np.where(same, dec_abs, np.where(earlier, dec, 0.0))
    dq = np.zeros((BLK, V_W), np.float32)
    dk = np.zeros((BLK, QK_W), np.float32)
    gbd = np.zeros((QK_W, V_W), np.float32)
    for h in range(HEADS):
        dq[:, h * DV:(h + 1) * DV] = np.exp(log_g[h] * (i + 1.0)).astype(np.float32)[:, None]
        dk[:, h * DK:(h + 1) * DK] = np.exp(log_g[h] * (BLK - 1.0 - i)).astype(np.float32)[:, None]
        gbd[h * DK:(h + 1) * DK, h * DV:(h + 1) * DV] = np.exp(log_g[h] * np.float32(BLK))
    bd = (gbd > 0).astype(np.float32)
    return jnp.asarray(mask), jnp.asarray(dq), jnp.asarray(dk), jnp.asarray(gbd), jnp.asarray(bd)


def _rotary_tables(seq):
    half = DK // 2
    freqs = ROPE_BASE ** (-jnp.arange(half, dtype=F32) * 2.0 / DK)
    ang = jnp.arange(seq, dtype=F32)[:, None] * freqs[None, :]
    cos, sin = jnp.cos(ang), jnp.sin(ang)
    cos_t = jnp.tile(jnp.concatenate([cos, cos], axis=1), (1, HEADS))
    sin_t = jnp.tile(jnp.concatenate([-sin, sin], axis=1), (1, HEADS))
    return cos_t, sin_t


def _swap_halves(x):
    lane = lax.broadcasted_iota(jnp.int32, (1, QK_W), 1)
    first = (lane & (DK - 1)) < DK // 2
    return jnp.where(first, pltpu.roll(x, QK_W - DK // 2, 1), pltpu.roll(x, DK // 2, 1))


def _head_mask(h):
    lane = lax.broadcasted_iota(jnp.int32, (1, QK_W), 1)
    return (lane >= h * DK) & (lane < (h + 1) * DK)


def _ffn_fwd(x, n, cols, gq, wd, name):
    t, d = x.shape
    fp = cols.shape[1]
    tm = min(t, FFN_TM)
    tf = FFN_FWD_TF
    nj = fp // tf

    def body(x_ref, n_ref, wg_ref, wu_ref, wd_ref, xo_ref, h_ref, b_ref, sil_ref, dsil_ref, s_ref, acc_ref):
        j = pl.program_id(1)

        @pl.when(j == 0)
        def _():
            xv = x_ref[...]
            r = lax.rsqrt(jnp.mean(xv * xv, axis=-1, keepdims=True) + RMS_EPS)
            h_ref[...] = (xv * r * n_ref[...]).astype(BF16)
            acc_ref[...] = jnp.zeros_like(acc_ref)

        h = h_ref[...]
        a = _dot(h, wg_ref[...])
        b = _dot(h, wu_ref[...])
        sg = _sigmoid(a)
        sil = a * sg
        s = (sil * b).astype(BF16)
        b_ref[...] = b.astype(BF16)
        sil_ref[...] = sil.astype(BF16)
        dsil_ref[...] = (sg + sil * (1.0 - sg)).astype(BF16)
        s_ref[...] = s
        acc_ref[...] += _dot(s, wd_ref[...])

        @pl.when(j == nj - 1)
        def _():
            xo_ref[...] = x_ref[...] + 0.5 * acc_ref[...]

    act = pl.BlockSpec((tm, tf), lambda i, j: (i, j))
    return pl.pallas_call(
        body, name=name, grid=(t // tm, nj),
        in_specs=[pl.BlockSpec((tm, d), lambda i, j: (i, 0)), pl.BlockSpec((1, d), lambda i, j: (0, 0)),
                  pl.BlockSpec((d, tf), lambda i, j: (gq, j)), pl.BlockSpec((d, tf), lambda i, j: (gq + 1, j)),
                  pl.BlockSpec((tf, d), lambda i, j: (j, 0))],
        out_specs=[pl.BlockSpec((tm, d), lambda i, j: (i, 0)), pl.BlockSpec((tm, d), lambda i, j: (i, 0)),
                   act, act, act, act],
        out_shape=[jax.ShapeDtypeStruct((t, d), F32), jax.ShapeDtypeStruct((t, d), BF16)]
        + [jax.ShapeDtypeStruct((t, fp), BF16)] * 4,
        scratch_shapes=[pltpu.VMEM((tm, d), F32)],
        compiler_params=_cparams(2),
    )(x, n, cols, cols, wd)


def _ffn_bwd_act(dxo, b, sil, dsil, wd_t, name):
    t, d = dxo.shape
    fp = wd_t.shape[1]
    tm = min(t, FFN_TM)
    tf = fp // 3
    nj = fp // tf

    def body(dxo_ref, b_ref, sil_ref, dsil_ref, wd_ref, da_ref, db_ref, dxob_ref):
        @pl.when(pl.program_id(1) == 0)
        def _():
            dxob_ref[...] = (0.5 * dxo_ref[...]).astype(BF16)

        ds = _dot(dxob_ref[...], wd_ref[...])
        da_ref[...] = (ds * b_ref[...].astype(F32) * dsil_ref[...].astype(F32)).astype(BF16)
        db_ref[...] = (ds * sil_ref[...].astype(F32)).astype(BF16)

    act = pl.BlockSpec((tm, tf), lambda i, j: (i, j))
    return pl.pallas_call(
        body, name=name, grid=(t // tm, nj),
        in_specs=[pl.BlockSpec((tm, d), lambda i, j: (i, 0)), act, act, act, pl.BlockSpec((d, tf), lambda i, j: (0, j))],
        out_specs=[act, act, pl.BlockSpec((tm, d), lambda i, j: (i, 0))],
        out_shape=[jax.ShapeDtypeStruct((t, fp), BF16), jax.ShapeDtypeStruct((t, fp), BF16),
                   jax.ShapeDtypeStruct((t, d), BF16)],
        compiler_params=_cparams(2),
    )(dxo, b, sil, dsil, wd_t)


def _ffn_bwd_in(da, db, dxo, x, n, cols_t, gq, name):
    t, d = x.shape
    fp = cols_t.shape[0]
    tm = min(t, FFN_TM)
    tf = 2 * fp // N_DEV
    nj = fp // tf

    def body(da_ref, db_ref, dxo_ref, x_ref, n_ref, wg_ref, wu_ref, dx_ref, dn_ref, acc_ref):
        i, j = pl.program_id(0), pl.program_id(1)

        @pl.when((i == 0) & (j == 0))
        def _():
            dn_ref[...] = jnp.zeros_like(dn_ref)

        @pl.when(j == 0)
        def _():
            acc_ref[...] = jnp.zeros_like(acc_ref)

        acc_ref[...] += _dot(da_ref[...], wg_ref[...]) + _dot(db_ref[...], wu_ref[...])

        @pl.when(j == nj - 1)
        def _():
            xv = x_ref[...]
            r = lax.rsqrt(jnp.mean(xv * xv, axis=-1, keepdims=True) + RMS_EPS)
            xh = xv * r
            dh = acc_ref[...]
            dn_ref[...] += jnp.sum(dh * xh, axis=0, keepdims=True)
            dhn = dh * n_ref[...]
            dx_ref[...] = dxo_ref[...] + r * (dhn - xh * jnp.mean(dhn * xh, axis=-1, keepdims=True))

    act = pl.BlockSpec((tm, tf), lambda i, j: (i, j))
    row = pl.BlockSpec((tm, d), lambda i, j: (i, 0))
    return pl.pallas_call(
        body, name=name, grid=(t // tm, nj),
        in_specs=[act, act, row, row, pl.BlockSpec((1, d), lambda i, j: (0, 0)),
                  pl.BlockSpec((tf, d), lambda i, j: (j, gq)), pl.BlockSpec((tf, d), lambda i, j: (j, gq + 1))],
        out_specs=[row, pl.BlockSpec((1, d), lambda i, j: (0, 0))],
        out_shape=[jax.ShapeDtypeStruct((t, d), F32), jax.ShapeDtypeStruct((1, d), F32)],
        scratch_shapes=[pltpu.VMEM((tm, d), F32)],
        compiler_params=_cparams(2),
    )(da, db, dxo, x, n, cols_t, cols_t)


def _wgrad(a, b, scale, tk, tn, name):
    t, k = a.shape
    n = b.shape[1]
    tt = min(t, WGRAD_TT)
    nt = t // tt

    def body(a_ref, b_ref, o_ref, acc_ref):
        s = pl.program_id(2)

        @pl.when(s == 0)
        def _():
            acc_ref[...] = jnp.zeros_like(acc_ref)

        acc_ref[...] += _dot_tn(a_ref[...], b_ref[...])

        @pl.when(s == nt - 1)
        def _():
            o_ref[...] = (scale * acc_ref[...]).astype(BF16)

    return pl.pallas_call(
        body, name=name, grid=(k // tk, n // tn, nt),
        in_specs=[pl.BlockSpec((tt, tk), lambda p, q, s: (s, p)), pl.BlockSpec((tt, tn), lambda p, q, s: (s, q))],
        out_specs=pl.BlockSpec((tk, tn), lambda p, q, s: (p, q)),
        out_shape=jax.ShapeDtypeStruct((k, n), BF16),
        scratch_shapes=[pltpu.VMEM((tk, tn), F32)],
        compiler_params=_cparams(3),
    )(a, b)


def _mix_in(x, n, w_in, cos_t, sin_t, seq, name):
    t, d = x.shape
    per_seq = seq // TM

    def body(x_ref, n_ref, w_ref, c_ref, s_ref, h_ref, q_ref, k_ref, v_ref, g_ref, u_ref):
        xv = x_ref[...]
        r = lax.rsqrt(jnp.mean(xv * xv, axis=-1, keepdims=True) + RMS_EPS)
        h = (xv * r * n_ref[...]).astype(BF16)
        h_ref[...] = h
        p = _dot(h, w_ref[...])
        c, s = c_ref[...], s_ref[...]
        q = p[:, :QK_W]
        k = p[:, QK_W:2 * QK_W]
        q_ref[...] = ((q * c + _swap_halves(q) * s) * (DK ** -0.5)).astype(BF16)
        k_ref[...] = (k * c + _swap_halves(k) * s).astype(BF16)
        v_ref[...] = p[:, 2 * QK_W:2 * QK_W + V_W].astype(BF16)
        g_ref[...] = p[:, 2 * QK_W + V_W:2 * QK_W + 2 * V_W]
        u_ref[...] = p[:, 2 * QK_W + 2 * V_W:]

    tile = lambda w: pl.BlockSpec((TM, w), lambda i: (i, 0))
    return pl.pallas_call(
        body, name=name, grid=(t // TM,),
        in_specs=[tile(d), pl.BlockSpec((1, d), lambda i: (0, 0)), pl.BlockSpec(w_in.shape, lambda i: (0, 0)),
                  pl.BlockSpec((TM, QK_W), lambda i: (i % per_seq, 0)), pl.BlockSpec((TM, QK_W), lambda i: (i % per_seq, 0))],
        out_specs=[tile(d), tile(QK_W), tile(QK_W), tile(V_W), tile(V_W), tile(POOL_W)],
        out_shape=[jax.ShapeDtypeStruct((t, d), BF16), jax.ShapeDtypeStruct((t, QK_W), BF16),
                   jax.ShapeDtypeStruct((t, QK_W), BF16), jax.ShapeDtypeStruct((t, V_W), BF16),
                   jax.ShapeDtypeStruct((t, V_W), F32), jax.ShapeDtypeStruct((t, POOL_W), F32)],
        compiler_params=_cparams(1),
    )(x, n, w_in, cos_t, sin_t)


def _mix_in_bwd(dp, dx2, x1, n, w_in_t, name):
    t, d = x1.shape

    def body(dp_ref, dx2_ref, x_ref, n_ref, w_ref, dx_ref, dn_ref):
        @pl.when(pl.program_id(0) == 0)
        def _():
            dn_ref[...] = jnp.zeros_like(dn_ref)

        dh = _dot(dp_ref[...], w_ref[...])
        xv = x_ref[...]
        r = lax.rsqrt(jnp.mean(xv * xv, axis=-1, keepdims=True) + RMS_EPS)
        xh = xv * r
        dn_ref[...] += jnp.sum(dh * xh, axis=0, keepdims=True)
        dhn = dh * n_ref[...]
        dx_ref[...] = dx2_ref[...] + r * (dhn - xh * jnp.mean(dhn * xh, axis=-1, keepdims=True))

    tile = lambda w: pl.BlockSpec((TM, w), lambda i: (i, 0))
    return pl.pallas_call(
        body, name=name, grid=(t // TM,),
        in_specs=[tile(dp.shape[1]), tile(d), tile(d), pl.BlockSpec((1, d), lambda i: (0, 0)),
                  pl.BlockSpec(w_in_t.shape, lambda i: (0, 0))],
        out_specs=[tile(d), pl.BlockSpec((1, d), lambda i: (0, 0))],
        out_shape=[jax.ShapeDtypeStruct((t, d), F32), jax.ShapeDtypeStruct((1, d), F32)],
        compiler_params=_cparams(1),
    )(dp, dx2, x1, n, w_in_t)


def _group_norm(o):
    parts, rstds = [], []
    for h in range(HEADS):
        oh = o[:, h * DV:(h + 1) * DV]
        dlt = oh - jnp.mean(oh, axis=-1, keepdims=True)
        rstd = lax.rsqrt(jnp.mean(dlt * dlt, axis=-1, keepdims=True) + GN_EPS)
        parts.append(dlt * rstd)
        rstds.append(rstd)
    return jnp.concatenate(parts, axis=1), rstds


def _mix_core_fwd(qs, k, v, g, u, x1, consts, gain, wp, scale, w_out, nseq, seq, name):
    t, d = x1.shape
    nblk = seq // BLK
    mask, dq, dk, gbd, bd = consts

    def body(q_ref, k_ref, v_ref, g_ref, u_ref, x1_ref, m_ref, dq_ref, dk_ref, gbd_ref, bd_ref, gain_ref, wp_ref,
             sc_ref, wo_ref, x2_ref, mix_ref, o_ref, pooled_ref, st_ref, state, halo):
        j = pl.program_id(1)

        @pl.when(j == 0)
        def _():
            state[...] = jnp.zeros_like(state)
            halo[...] = jnp.zeros_like(halo)

        qv, kv, vv = q_ref[...], k_ref[...], v_ref[...]
        st = state[...]
        st_ref[0] = st
        cross = _dot(qv, st.astype(BF16)) * dq_ref[...]
        outs = []
        for h in range(HEADS):
            qh = jnp.where(_head_mask(h), qv, jnp.zeros_like(qv))
            am = (_dot_nt(qh, kv) * m_ref[h]).astype(BF16)
            outs.append(_dot(am, vv[:, h * DV:(h + 1) * DV]))
        o = jnp.concatenate(outs, axis=1) + cross
        o_ref[...] = o
        kd = (kv.astype(F32) * dk_ref[...]).astype(BF16)
        state[...] = gbd_ref[...] * st + _dot_tn(kd, vv) * bd_ref[...]

        gv = g_ref[...]
        nrm, _ = _group_norm(o)
        ret = (gv * _sigmoid(gv)) * (nrm * gain_ref[...])

        uv = u_ref[...]
        c = jnp.concatenate([halo[...], uv], axis=0)
        halo[...] = uv[BLK - HALO:, :]
        pos = j * BLK + lax.broadcasted_iota(jnp.int32, (BLK, 1), 0)
        parts = []
        for gi, w in enumerate(WINDOWS):
            c = c + pltpu.roll(c, w // 2, 0)
            cnt = jnp.minimum(pos + 1, w).astype(F32)
            parts.append(c[HALO:, :GC] / cnt)
            if gi + 1 < len(WINDOWS):
                c = c[:, GC:]
        pooled = (jnp.concatenate(parts, axis=1) - uv).astype(BF16)
        pooled_ref[...] = pooled
        z = jnp.concatenate([_dot(pooled[:, gi * GC:(gi + 1) * GC], wp_ref[gi]) for gi in range(len(WINDOWS))], axis=1)
        mix = jnp.concatenate([ret, z * sc_ref[...]], axis=1).astype(BF16)
        mix_ref[...] = mix
        x2_ref[...] = x1_ref[...] + _dot(mix, wo_ref[...])

    blk = lambda w: pl.BlockSpec((BLK, w), lambda i, j: (i * nblk + j, 0))
    full = lambda a: pl.BlockSpec(a.shape, lambda i, j: (0,) * a.ndim)
    return pl.pallas_call(
        body, name=name, grid=(nseq, nblk),
        in_specs=[blk(QK_W), blk(QK_W), blk(V_W), blk(V_W), blk(POOL_W), blk(d),
                  full(mask), full(dq), full(dk), full(gbd), full(bd), full(gain), full(wp), full(scale), full(w_out)],
        out_specs=[blk(d), blk(d), blk(V_W), blk(POOL_W),
                   pl.BlockSpec((1, QK_W, V_W), lambda i, j: (i * nblk + j, 0, 0))],
        out_shape=[jax.ShapeDtypeStruct((t, d), F32), jax.ShapeDtypeStruct((t, d), BF16),
                   jax.ShapeDtypeStruct((t, V_W), F32), jax.ShapeDtypeStruct((t, POOL_W), BF16),
                   jax.ShapeDtypeStruct((nseq * nblk, QK_W, V_W), F32)],
        scratch_shapes=[pltpu.VMEM((QK_W, V_W), F32), pltpu.VMEM((HALO, POOL_W), F32)],
        compiler_params=_cparams(2),
    )(qs, k, v, g, u, x1, mask, dq, dk, gbd, bd, gain, wp, scale, w_out)


def _mix_core_bwd(dx2, qs, k, v, g, o, pooled, st, consts, gain, wp, scale, w_out, cos_t, sin_t, nseq, seq, name):
    t, d = dx2.shape
    nblk = seq // BLK
    mask, dq, dk, gbd, bd = consts
    n_win = len(WINDOWS)

    def body(dx2_ref, q_ref, k_ref, v_ref, g_ref, o_ref, pooled_ref, st_ref, m_ref, dq_ref, dk_ref, gbd_ref, bd_ref,
             gain_ref, wp_ref, sc_ref, wo_ref, c_ref, s_ref,
             dp_ref, dx2b_ref, dgain_ref, dscale_ref, dwp_ref, rstate, carry):
        i, j = pl.program_id(0), pl.program_id(1)

        @pl.when((i == 0) & (j == 0))
        def _():
            dgain_ref[...] = jnp.zeros_like(dgain_ref)
            dscale_ref[...] = jnp.zeros_like(dscale_ref)
            dwp_ref[...] = jnp.zeros_like(dwp_ref)

        @pl.when(j == 0)
        def _():
            rstate[...] = jnp.zeros_like(rstate)
            carry[...] = jnp.zeros_like(carry)

        dx2b = dx2_ref[...].astype(BF16)
        dx2b_ref[...] = dx2b
        dmix = _dot(dx2b, wo_ref[...])
        dret, dpool = dmix[:, :V_W], dmix[:, V_W:]

        gv, ov, gain_v = g_ref[...], o_ref[...], gain_ref[...]
        sg = _sigmoid(gv)
        sil = gv * sg
        nrm, rstds = _group_norm(ov)
        dg = dret * (nrm * gain_v) * (sg * (1.0 + gv * (1.0 - sg)))
        dgn = dret * sil
        dgain_ref[...] += jnp.sum(dgn * nrm, axis=0, keepdims=True)
        dnrm = dgn * gain_v
        do_parts = []
        for h in range(HEADS):
            dn_h = dnrm[:, h * DV:(h + 1) * DV]
            n_h = nrm[:, h * DV:(h + 1) * DV]
            do_parts.append(rstds[h] * (dn_h - jnp.mean(dn_h, axis=-1, keepdims=True)
                                        - n_h * jnp.mean(dn_h * n_h, axis=-1, keepdims=True)))
        do = jnp.concatenate(do_parts, axis=1)
        dob = do.astype(BF16)

        qv, kv, vv = q_ref[...], k_ref[...], v_ref[...]
        stb = st_ref[0].astype(BF16)
        rs = rstate[...]
        rsb = rs.astype(BF16)
        dod = (do * dq_ref[...]).astype(BF16)
        dqs = _dot_nt(dod, stb)
        dst = _dot_tn(qv, dod) * bd_ref[...]
        dkf = dk_ref[...]
        kd = (kv.astype(F32) * dkf).astype(BF16)
        dks = _dot_nt(vv, rsb) * dkf
        dvs = _dot(kd, rsb)
        dv_parts = []
        for h in range(HEADS):
            hm = _head_mask(h)
            qh = jnp.where(hm, qv, jnp.zeros_like(qv))
            mh = m_ref[h]
            am = (_dot_nt(qh, kv) * mh).astype(BF16)
            dpm = (_dot_nt(dob[:, h * DV:(h + 1) * DV], vv[:, h * DV:(h + 1) * DV]) * mh).astype(BF16)
            dqs = dqs + jnp.where(hm, _dot(dpm, kv), 0.0)
            dks = dks + jnp.where(hm, _dot_tn(dpm, qv), 0.0)
            dv_parts.append(_dot_tn(am, dob[:, h * DV:(h + 1) * DV]))
        dvs = dvs + jnp.concatenate(dv_parts, axis=1)
        rstate[...] = dst + gbd_ref[...] * rs

        cv, sv = c_ref[...], s_ref[...]
        dqr = dqs * (DK ** -0.5)
        dq_pre = dqr * cv + _swap_halves(dqr * sv)
        dk_pre = dks * cv + _swap_halves(dks * sv)

        pv = pooled_ref[...]
        sc = sc_ref[...]
        dzb = (dpool * sc).astype(BF16)
        z_parts, dpo_parts = [], []
        for gi in range(n_win):
            p_g = pv[:, gi * GC:(gi + 1) * GC]
            dz_g = dzb[:, gi * GC:(gi + 1) * GC]
            z_parts.append(_dot(p_g, wp_ref[gi]))
            dwp_ref[gi] += _dot_tn(p_g, dz_g)
            dpo_parts.append(_dot_nt(dz_g, wp_ref[gi]))
        dscale_ref[...] += jnp.sum(dpool * jnp.concatenate(z_parts, axis=1), axis=0, keepdims=True)
        dpo = jnp.concatenate(dpo_parts, axis=1)
        pos = (nblk - 1 - j) * BLK + lax.broadcasted_iota(jnp.int32, (BLK, 1), 0)
        e = jnp.concatenate(
            [dpo[:, gi * GC:(gi + 1) * GC] / jnp.minimum(pos + 1, w).astype(F32) for gi, w in enumerate(WINDOWS)], axis=1)
        c = jnp.concatenate([e, carry[...]], axis=0)
        carry[...] = e[:HALO, :]
        rows = BLK + HALO
        lead = []
        for gi, w in enumerate(WINDOWS):
            c = c + pltpu.roll(c, rows - w // 2, 0)
            lead.append(c[:BLK, :GC])
            if gi + 1 < n_win:
                c = c[:, GC:]
        du = jnp.concatenate(lead, axis=1) - dpo

        dp_ref[:, 0:QK_W] = dq_pre.astype(BF16)
        dp_ref[:, QK_W:2 * QK_W] = dk_pre.astype(BF16)
        dp_ref[:, 2 * QK_W:2 * QK_W + V_W] = dvs.astype(BF16)
        dp_ref[:, 2 * QK_W + V_W:2 * QK_W + 2 * V_W] = dg.astype(BF16)
        dp_ref[:, 2 * QK_W + 2 * V_W:] = du.astype(BF16)

    rev = lambda i, j: i * nblk + (nblk - 1 - j)
    blk = lambda w: pl.BlockSpec((BLK, w), lambda i, j: (rev(i, j), 0))
    full = lambda a: pl.BlockSpec(a.shape, lambda i, j: (0,) * a.ndim)
    in_w = 2 * QK_W + 2 * V_W + POOL_W
    return pl.pallas_call(
        body, name=name, grid=(nseq, nblk),
        in_specs=[blk(d), blk(QK_W), blk(QK_W), blk(V_W), blk(V_W), blk(V_W), blk(POOL_W),
                  pl.BlockSpec((1, QK_W, V_W), lambda i, j: (rev(i, j), 0, 0)),
                  full(mask), full(dq), full(dk), full(gbd), full(bd), full(gain), full(wp), full(scale), full(w_out),
                  pl.BlockSpec((BLK, QK_W), lambda i, j: (nblk - 1 - j, 0)),
                  pl.BlockSpec((BLK, QK_W), lambda i, j: (nblk - 1 - j, 0))],
        out_specs=[blk(in_w), blk(d), pl.BlockSpec((1, V_W), lambda i, j: (0, 0)),
                   pl.BlockSpec((1, POOL_W), lambda i, j: (0, 0)), pl.BlockSpec((n_win, GC, GC), lambda i, j: (0, 0, 0))],
        out_shape=[jax.ShapeDtypeStruct((t, in_w), BF16), jax.ShapeDtypeStruct((t, d), BF16),
                   jax.ShapeDtypeStruct((1, V_W), F32), jax.ShapeDtypeStruct((1, POOL_W), F32),
                   jax.ShapeDtypeStruct((n_win, GC, GC), F32)],
        scratch_shapes=[pltpu.VMEM((QK_W, V_W), F32), pltpu.VMEM((HALO, POOL_W), F32)],
        compiler_params=_cparams(2),
    )(dx2, qs, k, v, g, o, pooled, st, mask, dq, dk, gbd, bd, gain, wp, scale, w_out, cos_t, sin_t)


def _loss_head(x3, nf, tgt, name):
    t, d = x3.shape

    def body(x_ref, n_ref, t_ref, dx_ref, dn_ref, loss_ref):
        @pl.when(pl.program_id(0) == 0)
        def _():
            dn_ref[...] = jnp.zeros_like(dn_ref)
            loss_ref[...] = jnp.zeros_like(loss_ref)

        xv = x_ref[...]
        nv = n_ref[...]
        r = lax.rsqrt(jnp.mean(xv * xv, axis=-1, keepdims=True) + RMS_EPS)
        xh = xv * r
        err = xh * nv - t_ref[...]
        row = jnp.mean(err * err, axis=-1, keepdims=True)
        loss_ref[...] += 0.5 * jnp.sum(row, axis=0, keepdims=True)
        dy = err * (1.0 / d)
        dn_ref[...] += jnp.sum(dy * xh, axis=0, keepdims=True)
        dxh = dy * nv
        dx_ref[...] = r * (dxh - xh * jnp.mean(dxh * xh, axis=-1, keepdims=True))

    tile = pl.BlockSpec((TM, d), lambda i: (i, 0))
    return pl.pallas_call(
        body, name=name, grid=(t // TM,),
        in_specs=[tile, pl.BlockSpec((1, d), lambda i: (0, 0)), tile],
        out_specs=[tile, pl.BlockSpec((1, d), lambda i: (0, 0)), pl.BlockSpec((1, 1), lambda i: (0, 0))],
        out_shape=[jax.ShapeDtypeStruct((t, d), F32), jax.ShapeDtypeStruct((1, d), F32), jax.ShapeDtypeStruct((1, 1), F32)],
        compiler_params=_cparams(1),
    )(x3, nf, tgt)


def _coords():
    return lax.axis_index("x"), lax.axis_index("y"), lax.axis_index("c")


def _window(ref, kind, idx, size):
    if kind == "col":
        return ref.at[:, pl.ds(pl.multiple_of(idx * size, LANE), size)]
    return ref.at[pl.ds(pl.multiple_of(idx * size, 8), size), :]


def _all_gather(parts, name):
    n = len(parts)
    kinds = [kd for _, kd in parts]
    sizes = [a.shape[1] if kd == "col" else a.shape[0] for a, kd in parts]

    def body(*refs):
        ins, outs = refs[:n], refs[n:2 * n]
        send_sems, recv_sems, local_sems = refs[2 * n:]
        x, y, c = _coords()
        me, sibling = (x, y, c), (x, y, 1 - c)
        chips = [(1 - x, y), (x, 1 - y), (1 - x, 1 - y)]

        def win(p, dev):
            return _window(outs[p], kinds[p], 4 * dev[0] + 2 * dev[1] + dev[2], sizes[p])

        def copy(p, k, block, to, src=None):
            return pltpu.make_async_remote_copy(
                src_ref=win(p, block) if src is None else src, dst_ref=win(p, block),
                send_sem=send_sems.at[p * 7 + k], recv_sem=recv_sems.at[p * 7 + k], device_id=to, device_id_type=MESH_ID)

        mine = [pltpu.make_async_copy(ins[p], win(p, me), local_sems.at[p]) for p in range(n)]
        for cp in mine:
            cp.start()
        first = []
        for p in range(n):
            first.append(copy(p, 0, me, sibling, src=ins[p]))
            first += [copy(p, 1 + q, me, (*chip, c), src=ins[p]) for q, chip in enumerate(chips)]
        for cp in first:
            cp.start()
        passed = []
        for q, chip in enumerate(chips):
            for p in range(n):
                copy(p, 1 + q, (*chip, c), me).wait_recv()
                fwd = copy(p, 4 + q, (*chip, c), sibling)
                fwd.start()
                passed.append(fwd)
        for p in range(n):
            copy(p, 0, sibling, me).wait_recv()
            for q, chip in enumerate(chips):
                copy(p, 4 + q, (*chip, 1 - c), me).wait_recv()
        for cp in first + passed:
            cp.wait_send()
        for cp in mine:
            cp.wait()

    out_shape = [jax.ShapeDtypeStruct((a.shape[0], N_DEV * a.shape[1]) if kd == "col" else (N_DEV * a.shape[0], a.shape[1]),
                                      a.dtype) for a, kd in parts]
    return pl.pallas_call(
        body, name=name, in_specs=[ANY] * n, out_specs=[ANY] * n, out_shape=out_shape,
        scratch_shapes=[pltpu.SemaphoreType.DMA((7 * n,)), pltpu.SemaphoreType.DMA((7 * n,)), pltpu.SemaphoreType.DMA((n,))],
    )(*[a for a, _ in parts])


def _shard_shape(a, kd):
    return (a.shape[0], a.shape[1] // N_DEV) if kd == "col" else (a.shape[0] // N_DEV, a.shape[1])


def _rs_pair(grads, name):
    n = len(grads)
    kinds = [kd for _, kd in grads]
    shapes = [_shard_shape(a, kd) for a, kd in grads]

    def body(*refs):
        ins, outs = refs[:n], refs[n:2 * n]
        send_sems, recv_sems = refs[2 * n:]
        x, y, c = _coords()
        copies = []
        for p in range(n):
            size = shapes[p][1] if kinds[p] == "col" else shapes[p][0]
            for s in range(4):
                src = _window(ins[p], kinds[p], 2 * s + (1 - c), size)
                copies.append(pltpu.make_async_remote_copy(
                    src_ref=src, dst_ref=outs[p].at[s], send_sem=send_sems.at[4 * p + s], recv_sem=recv_sems.at[4 * p + s],
                    device_id=(x, y, 1 - c), device_id_type=MESH_ID))
        for cp in copies:
            cp.start()
        for cp in copies:
            cp.wait_recv()
        for cp in copies:
            cp.wait_send()

    return pl.pallas_call(
        body, name=name, in_specs=[ANY] * n, out_specs=[ANY] * n,
        out_shape=[jax.ShapeDtypeStruct((4,) + shapes[p], BF16) for p in range(n)],
        scratch_shapes=[pltpu.SemaphoreType.DMA((4 * n,)), pltpu.SemaphoreType.DMA((4 * n,))],
    )(*[a for a, _ in grads])


def _rs_chips(sums, name):
    n = len(sums)

    def body(*refs):
        ins, outs = refs[:n], refs[n:2 * n]
        send_sems, recv_sems = refs[2 * n:]
        x, y, c = _coords()
        chips = [(1 - x, y), (x, 1 - y), (1 - x, 1 - y)]
        copies = []
        for p in range(n):
            for q, (cx, cy) in enumerate(chips):
                copies.append(pltpu.make_async_remote_copy(
                    src_ref=ins[p].at[2 * cx + cy], dst_ref=outs[p].at[q],
                    send_sem=send_sems.at[3 * p + q], recv_sem=recv_sems.at[3 * p + q],
                    device_id=(cx, cy, c), device_id_type=MESH_ID))
        for cp in copies:
            cp.start()
        for cp in copies:
            cp.wait_recv()
        for cp in copies:
            cp.wait_send()

    return pl.pallas_call(
        body, name=name, in_specs=[ANY] * n, out_specs=[ANY] * n,
        out_shape=[jax.ShapeDtypeStruct((3,) + a.shape[1:], BF16) for a in sums],
        scratch_shapes=[pltpu.SemaphoreType.DMA((3 * n,)), pltpu.SemaphoreType.DMA((3 * n,))],
    )(*sums)


def _pair_sum(grad, kd, recv, core, name):
    _, r, cw = recv.shape
    tr = min(r, TM)

    def body(core_ref, g_ref, r_ref, o_ref):
        del core_ref
        o_ref[0] = (g_ref[...].astype(F32) + r_ref[0].astype(F32)).astype(BF16)

    if kd == "col":
        g_spec = pl.BlockSpec((tr, cw), lambda s, i, core_ref: (i, 2 * s + core_ref[0]))
    else:
        g_spec = pl.BlockSpec((tr, cw), lambda s, i, core_ref: ((2 * s + core_ref[0]) * (r // tr) + i, 0))
    grid_spec = pltpu.PrefetchScalarGridSpec(
        num_scalar_prefetch=1, grid=(4, r // tr),
        in_specs=[g_spec, pl.BlockSpec((1, tr, cw), lambda s, i, core_ref: (s, i, 0))],
        out_specs=pl.BlockSpec((1, tr, cw), lambda s, i, core_ref: (s, i, 0)))
    return pl.pallas_call(
        body, name=name, grid_spec=grid_spec, out_shape=jax.ShapeDtypeStruct(recv.shape, BF16),
        compiler_params=_cparams(2),
    )(core, grad, recv)


def _adam_math(w, g, m, v):
    m2 = B1 * m + (1.0 - B1) * g
    v2 = B2 * v + (1.0 - B2) * (g * g)
    m_hat = m2 / (1.0 - B1 ** STEP)
    v_hat = v2 / (1.0 - B2 ** STEP)
    delta = -LR * (m_hat / (jnp.sqrt(v_hat) + ADAM_EPS) + WD * w)
    return delta, m2, v2


def _chip_sum_adam(psum, recv, chip, w, m, v, name):
    r, cw = w.shape
    pc = psum.shape[2]
    tr = min(r, TM)

    def body(chip_ref, p_ref, r_ref, w_ref, m_ref, v_ref, g_ref, d_ref, m2_ref, v2_ref):
        del chip_ref
        g = p_ref[0].astype(F32) + r_ref[0].astype(F32) + r_ref[1].astype(F32) + r_ref[2].astype(F32)
        g = g[:, :cw]
        delta, m2, v2 = _adam_math(w_ref[...], g, m_ref[...], v_ref[...])
        g_ref[...] = g
        d_ref[...] = delta
        m2_ref[...] = m2
        v2_ref[...] = v2

    loc = pl.BlockSpec((tr, cw), lambda i, chip_ref: (i, 0))
    grid_spec = pltpu.PrefetchScalarGridSpec(
        num_scalar_prefetch=1, grid=(r // tr,),
        in_specs=[pl.BlockSpec((1, tr, pc), lambda i, chip_ref: (chip_ref[0], i, 0)),
                  pl.BlockSpec((3, tr, pc), lambda i, chip_ref: (0, i, 0)), loc, loc, loc],
        out_specs=[loc, loc, loc, loc])
    return pl.pallas_call(
        body, name=name, grid_spec=grid_spec, out_shape=[jax.ShapeDtypeStruct((r, cw), F32)] * 4,
        compiler_params=_cparams(1),
    )(chip, psum, recv, w, m, v)


def _small_allreduce_adam(partials, params, moms, vels, name):
    n = len(partials)
    row0 = []
    rows = 0
    for a in partials:
        row0.append(rows)
        rows += _pad_to(a.shape[0], 8)
    width = max(a.shape[1] for a in partials)

    def body(*refs):
        g_in = refs[:n]
        w_in, m_in, v_in = refs[n:2 * n], refs[2 * n:3 * n], refs[3 * n:4 * n]
        outs = refs[4 * n:8 * n]
        slab, send_sems, recv_sems = refs[8 * n:]
        x, y, c = _coords()
        me = 4 * x + 2 * y + c
        slab[me] = jnp.zeros((rows, width), F32)
        for p in range(n):
            r, cw = partials[p].shape
            slab[me, row0[p]:row0[p] + r, 0:cw] = g_in[p][...]
        copies = []
        for q in range(1, N_DEV):
            peer = me ^ q
            copies.append(pltpu.make_async_remote_copy(
                src_ref=slab.at[me], dst_ref=slab.at[me], send_sem=send_sems.at[q - 1], recv_sem=recv_sems.at[q - 1],
                device_id=(peer >> 2, (peer >> 1) & 1, peer & 1), device_id_type=MESH_ID))
        for cp in copies:
            cp.start()
        for cp in copies:
            cp.wait_recv()
        for cp in copies:
            cp.wait_send()
        for p in range(n):
            r, cw = partials[p].shape
            g = slab[0, row0[p]:row0[p] + r, 0:cw]
            for dev in range(1, N_DEV):
                g = g + slab[dev, row0[p]:row0[p] + r, 0:cw]
            delta, m2, v2 = _adam_math(w_in[p][...], g, m_in[p][...], v_in[p][...])
            outs[4 * p][...] = g
            outs[4 * p + 1][...] = delta
            outs[4 * p + 2][...] = m2
            outs[4 * p + 3][...] = v2

    out_shape = []
    for a in partials:
        out_shape += [jax.ShapeDtypeStruct(a.shape, F32)] * 4
    return pl.pallas_call(
        body, name=name, in_specs=[VMEM_SPEC] * (4 * n), out_specs=[VMEM_SPEC] * (4 * n), out_shape=out_shape,
        scratch_shapes=[pltpu.VMEM((N_DEV, rows, width), F32), pltpu.SemaphoreType.DMA((N_DEV - 1,)),
                        pltpu.SemaphoreType.DMA((N_DEV - 1,))],
    )(*partials, *params, *moms, *vels)


def _local_step(xf, tgt, nseq, seq, cols_all, win_all, d1_all, d2_all, wout_all, small_w):
    d = xf.shape[1]
    n1, n2, gain, pool_w, pool_scale, n3, nf = small_w
    tf = 2 * cols_all.shape[1] // N_DEV
    consts = _retention_constants()
    cos_t, sin_t = _rotary_tables(seq)
    wp_b = pool_w.astype(BF16)

    x1, h1, b1, sil1, dsil1, s1 = _ffn_fwd(xf, n1, cols_all, 0, d1_all, "ffn1_fwd")
    h2, qs, kr, vv, gg, uu = _mix_in(x1, n2, win_all, cos_t, sin_t, seq, "mix_in")
    x2, mix, oo, pooled, states = _mix_core_fwd(qs, kr, vv, gg, uu, x1, consts, gain, wp_b, pool_scale, wout_all,
                                                 nseq, seq, "mix_core_fwd")
    x3, h3, b3, sil3, dsil3, s3 = _ffn_fwd(x2, n3, cols_all, 2, d2_all, "ffn2_fwd")
    dx3, dnf, loss_part = _loss_head(x3, nf, tgt, "loss_head")

    cols_t = _transpose(cols_all, "transpose_cols")
    d1_t = _transpose(d1_all, "transpose_down1")
    d2_t = _transpose(d2_all, "transpose_down2")
    win_t = _transpose(win_all, "transpose_w_in")
    wout_t = _transpose(wout_all, "transpose_w_out")
    da3, db3, dx3b = _ffn_bwd_act(dx3, b3, sil3, dsil3, d2_t, "ffn2_bwd_act")
    dx2, dn3 = _ffn_bwd_in(da3, db3, dx3, x2, n3, cols_t, 2, "ffn2_bwd_in")
    g_wg2 = _wgrad(h3, da3, 1.0, d, tf, "wgrad_gate2")
    g_wu2 = _wgrad(h3, db3, 1.0, d, tf, "wgrad_up2")
    g_wd2 = _wgrad(s3, dx3b, 1.0, tf, d, "wgrad_down2")
    dp, dx2b, dgain, dscale, dwp = _mix_core_bwd(dx2, qs, kr, vv, gg, oo, pooled, states, consts, gain, wp_b,
                                                 pool_scale, wout_t, cos_t, sin_t, nseq, seq, "mix_core_bwd")
    g_wout = _wgrad(mix, dx2b, 1.0, d, d, "wgrad_out")
    g_win = _wgrad(h2, dp, 1.0, d, d, "wgrad_in")
    dx1, dn2 = _mix_in_bwd(dp, dx2, x1, n2, win_t, "mix_in_bwd")
    da1, db1, dx1b = _ffn_bwd_act(dx1, b1, sil1, dsil1, d1_t, "ffn1_bwd_act")
    dx0, dn1 = _ffn_bwd_in(da1, db1, dx1, xf, n1, cols_t, 0, "ffn1_bwd_in")
    g_wg1 = _wgrad(h1, da1, 1.0, d, tf, "wgrad_gate1")
    g_wu1 = _wgrad(h1, db1, 1.0, d, tf, "wgrad_up1")
    g_wd1 = _wgrad(s1, dx1b, 1.0, tf, d, "wgrad_down1")
    grads = [(g_wg1, "col"), (g_wu1, "col"), (g_wd1, "row"), (g_win, "col"), (g_wout, "row"),
             (g_wg2, "col"), (g_wu2, "col"), (g_wd2, "row")]
    return loss_part, dx0, grads, (dn1, dn2, dgain, dwp, dscale, dn3, dnf)


def kernel(x, norm_ffn1, ffn1_gate, ffn1_up, ffn1_down, norm_mix, w_in, ret_gn_gain, pool_w, pool_scale, w_out, norm_ffn2, ffn2_gate, ffn2_up, ffn2_down, norm_final, loss_target, m_norm_ffn1, m_ffn1_gate, m_ffn1_up, m_ffn1_down, m_norm_mix, m_w_in, m_ret_gn_gain, m_pool_w, m_pool_scale, m_w_out, m_norm_ffn2, m_ffn2_gate, m_ffn2_up, m_ffn2_down, m_norm_final, v_norm_ffn1, v_ffn1_gate, v_ffn1_up, v_ffn1_down, v_norm_mix, v_w_in, v_ret_gn_gain, v_pool_w, v_pool_scale, v_w_out, v_norm_ffn2, v_ffn2_gate, v_ffn2_up, v_ffn2_down, v_norm_final):
    nseq, seq, d = x.shape
    t = nseq * seq
    f_loc = ffn1_gate.shape[2]
    f_pad = _pad_to(f_loc, LANE)
    xf = x.reshape(t, d)
    tgt = loss_target.reshape(t, d)
    core = lax.axis_index("c").astype(jnp.int32).reshape(1)
    chip = (2 * lax.axis_index("x") + lax.axis_index("y")).astype(jnp.int32).reshape(1)

    colp = lambda w: jnp.pad(w[0].astype(BF16), ((0, 0), (0, f_pad - f_loc)))
    rowp = lambda w: jnp.pad(w[0].astype(BF16), ((0, f_pad - f_loc), (0, 0)))
    ffn_cols = jnp.concatenate([colp(ffn1_gate), colp(ffn1_up), colp(ffn2_gate), colp(ffn2_up)], axis=0)
    cols_all, win_all, d1_all, d2_all, wout_all = _all_gather(
        [(ffn_cols, "col"), (w_in[0].astype(BF16), "col"), (rowp(ffn1_down), "row"), (rowp(ffn2_down), "row"),
         (w_out[0].astype(BF16), "row")], "all_gather_weights")

    small_w = (norm_ffn1, norm_mix, ret_gn_gain, pool_w[0], pool_scale, norm_ffn2, norm_final.reshape(1, d))
    loss_part, dx0, grads, small_parts = _local_step(xf, tgt, nseq, seq, cols_all, win_all, d1_all, d2_all, wout_all, small_w)
    dn1, dn2, dgain, dwp, dscale, dn3, dnf = small_parts

    names = ["ffn1_gate", "ffn1_up", "ffn1_down", "w_in", "w_out", "ffn2_gate", "ffn2_up", "ffn2_down"]
    local = [(ffn1_gate, m_ffn1_gate, v_ffn1_gate), (ffn1_up, m_ffn1_up, v_ffn1_up), (ffn1_down, m_ffn1_down, v_ffn1_down),
             (w_in, m_w_in, v_w_in), (w_out, m_w_out, v_w_out), (ffn2_gate, m_ffn2_gate, v_ffn2_gate),
             (ffn2_up, m_ffn2_up, v_ffn2_up), (ffn2_down, m_ffn2_down, v_ffn2_down)]
    pair_recv = _rs_pair(grads, "rs_pair")
    pair_sums = [_pair_sum(g, kd, rcv, core, "pair_sum_" + nm) for (g, kd), rcv, nm in zip(grads, pair_recv, names)]
    chip_recv = _rs_chips(pair_sums, "rs_chips")
    big = {}
    for ps, rcv, (w, m, v), nm in zip(pair_sums, chip_recv, local, names):
        g, dlt, m2, v2 = _chip_sum_adam(ps, rcv, chip, w[0], m[0], v[0], "adam_" + nm)
        big[nm] = tuple(a[None] for a in (g, dlt, m2, v2))

    small_names = ["norm_ffn1", "norm_mix", "ret_gn_gain", "pool_w", "pool_scale", "norm_ffn2", "norm_final"]
    flat = lambda a: a.reshape(pool_w.size // d, d)
    partials = [dn1, dn2, dgain, flat(dwp), dscale, dn3, dnf]
    params = [norm_ffn1, norm_mix, ret_gn_gain, flat(pool_w), pool_scale, norm_ffn2, norm_final.reshape(1, d)]
    moms = [m_norm_ffn1, m_norm_mix, m_ret_gn_gain, flat(m_pool_w), m_pool_scale, m_norm_ffn2, m_norm_final.reshape(1, d)]
    vels = [v_norm_ffn1, v_norm_mix, v_ret_gn_gain, flat(v_pool_w), v_pool_scale, v_norm_ffn2, v_norm_final.reshape(1, d)]
    small_out = _small_allreduce_adam(partials, params, moms, vels, "small_allreduce_adam")
    shapes = [norm_ffn1.shape, norm_mix.shape, ret_gn_gain.shape, pool_w.shape, pool_scale.shape, norm_ffn2.shape,
              norm_final.shape]
    small = {nm: tuple(small_out[4 * p + q].reshape(shapes[p]) for q in range(4)) for p, nm in enumerate(small_names)}

    loss = lax.psum(loss_part[0, 0], ("x", "y", "c"))
    order = ["norm_ffn1", "ffn1_gate", "ffn1_up", "ffn1_down", "norm_mix", "w_in", "ret_gn_gain", "pool_w", "pool_scale",
             "w_out", "norm_ffn2", "ffn2_gate", "ffn2_up", "ffn2_down", "norm_final"]
    both = {**big, **small}
    outs = [loss, dx0.reshape(nseq, seq, d)]
    for q in range(4):
        outs += [both[nm][q] for nm in order]
    return tuple(outs)
```

```python
import functools

import numpy as np
import jax
import jax.numpy as jnp
from jax import lax
from jax.experimental import pallas as pl
from jax.experimental.pallas import tpu as pltpu

F32, BF16 = jnp.float32, jnp.bfloat16
MESH_ID = pl.DeviceIdType.MESH
ANY = pl.BlockSpec(memory_space=pl.ANY)
VMEM_SPEC = pl.BlockSpec(memory_space=pltpu.VMEM)

N_DEV = 8
RMS_EPS = 1e-6
GN_EPS = 1e-5
HEADS, DK, DV = 4, 64, 128
QK_W, V_W, POOL_W = HEADS * DK, HEADS * DV, 512
WINDOWS = (2, 4, 8, 16)
GC = POOL_W // len(WINDOWS)
CHUNK = 64
BLK = 4 * CHUNK
HALO = 16
ROPE_BASE = 10000.0
LR, B1, B2, ADAM_EPS, WD, STEP = 0.001, 0.9, 0.999, 1e-08, 0.01, 10
LANE = 128
TM = 512
FFN_TM = 1024
FFN_FWD_TF = 512
WGRAD_TT = 2048
VMEM_LIMIT = 56 * 1024 * 1024


def _cparams(n_axes):
    return pltpu.CompilerParams(dimension_semantics=("arbitrary",) * n_axes, vmem_limit_bytes=VMEM_LIMIT)


class _Exchange:
    def __init__(self, inputs, out_shape, scratch, start, finish, mid=None):
        self.inputs, self.out_shape, self.scratch = list(inputs), list(out_shape), list(scratch)
        self.start, self.finish, self.mid = start, finish, mid


def _pallas(body, name, grid, in_specs, out_specs, out_shape, scratch_shapes, args, ride=None):
    n_axes = len(grid)
    if ride is None:
        return pl.pallas_call(body, name=name, grid=grid, in_specs=in_specs, out_specs=out_specs, out_shape=out_shape,
                              scratch_shapes=scratch_shapes, compiler_params=_cparams(n_axes))(*args)
    n_in, n_out, n_scr = len(in_specs), len(out_specs), len(scratch_shapes)
    r_in, r_out = len(ride.inputs), len(ride.out_shape)

    def hosted(*refs):
        ins, refs = refs[:n_in], refs[n_in:]
        r_ins, refs = refs[:r_in], refs[r_in:]
        outs, refs = refs[:n_out], refs[n_out:]
        r_outs, refs = refs[:r_out], refs[r_out:]
        scr, sems = refs[:n_scr], refs[n_scr:]
        ids = [pl.program_id(a) for a in range(n_axes)]
        first, last, inner0 = ids[0] == 0, ids[0] == grid[0] - 1, None
        for a in range(1, n_axes):
            first = first & (ids[a] == 0)
            last = last & (ids[a] == grid[a] - 1)
            inner0 = (ids[a] == 0) if inner0 is None else inner0 & (ids[a] == 0)

        @pl.when(first)
        def _():
            ride.start(r_ins, r_outs, sems)

        if ride.mid is not None:
            at_mid = ids[0] == grid[0] // 2
            if inner0 is not None:
                at_mid = at_mid & inner0

            @pl.when(at_mid)
            def _():
                ride.mid(r_ins, r_outs, sems)

        body(*ins, *outs, *scr)

        @pl.when(last)
        def _():
            ride.finish(r_ins, r_outs, sems)

    res = pl.pallas_call(
        hosted, name=name, grid=grid, in_specs=list(in_specs) + [ANY] * r_in, out_specs=list(out_specs) + [ANY] * r_out,
        out_shape=list(out_shape) + ride.out_shape, scratch_shapes=list(scratch_shapes) + ride.scratch,
        compiler_params=_cparams(n_axes))(*args, *ride.inputs)
    return res[:n_out], res[n_out:]


def _dot(a, b):
    return jnp.dot(a, b, preferred_element_type=F32)


def _dot_nt(a, b):
    return lax.dot_general(a, b, (((1,), (1,)), ((), ())), preferred_element_type=F32)


def _dot_tn(a, b):
    return lax.dot_general(a, b, (((0,), (0,)), ((), ())), preferred_element_type=F32)


def _sigmoid(x):
    return 0.5 * jnp.tanh(0.5 * x) + 0.5


def _transpose(w, name):
    r, c = w.shape
    tb = 512

    def body(x_ref, o_ref):
        o_ref[...] = x_ref[...].T

    return pl.pallas_call(
        body, name=name, grid=(r // tb, c // tb),
        in_specs=[pl.BlockSpec((tb, tb), lambda i, j: (i, j))],
        out_specs=pl.BlockSpec((tb, tb), lambda i, j: (j, i)),
        out_shape=jax.ShapeDtypeStruct((c, r), w.dtype),
        compiler_params=_cparams(2),
    )(w)


def _pad_to(n, m):
    return (n + m - 1) // m * m


def _retention_constants():
    gamma = (1.0 - 2.0 ** (-5.0 - np.arange(HEADS, dtype=np.float32))).astype(np.float32)
    log_g = np.log(gamma).astype(np.float32)
    i = np.arange(BLK)
    diff = (i[:, None] - i[None, :]).astype(np.float32)
    same = (i[:, None] // CHUNK) == (i[None, :] // CHUNK)
    earlier = (i[None, :] // CHUNK) < (i[:, None] // CHUNK)
    mask = np.zeros((HEADS, BLK, BLK), np.float32)
    for h in range(HEADS):
        dec_abs = np.exp(log_g[h] * np.abs(diff)).astype(np.float32)
        dec = np.exp(log_g[h] * diff * earlier).astype(np.float32)
        mask[h] = np.where(same, dec_abs, np.where(earlier, dec, 0.0))
    dq = np.zeros((BLK, V_W), np.float32)
    dk = np.zeros((BLK, QK_W), np.float32)
    gbd = np.zeros((QK_W, V_W), np.float32)
    for h in range(HEADS):
        dq[:, h * DV:(h + 1) * DV] = np.exp(log_g[h] * (i + 1.0)).astype(np.float32)[:, None]
        dk[:, h * DK:(h + 1) * DK] = np.exp(log_g[h] * (BLK - 1.0 - i)).astype(np.float32)[:, None]
        gbd[h * DK:(h + 1) * DK, h * DV:(h + 1) * DV] = np.exp(log_g[h] * np.float32(BLK))
    bd = (gbd > 0).astype(np.float32)
    return jnp.asarray(mask), jnp.asarray(dq), jnp.asarray(dk), jnp.asarray(gbd), jnp.asarray(bd)


def _rotary_tables(seq):
    half = DK // 2
    freqs = ROPE_BASE ** (-jnp.arange(half, dtype=F32) * 2.0 / DK)
    ang = jnp.arange(seq, dtype=F32)[:, None] * freqs[None, :]
    cos, sin = jnp.cos(ang), jnp.sin(ang)
    cos_t = jnp.tile(jnp.concatenate([cos, cos], axis=1), (1, HEADS))
    sin_t = jnp.tile(jnp.concatenate([-sin, sin], axis=1), (1, HEADS))
    return cos_t, sin_t


def _swap_halves(x):
    lane = lax.broadcasted_iota(jnp.int32, (1, QK_W), 1)
    first = (lane & (DK - 1)) < DK // 2
    return jnp.where(first, pltpu.roll(x, QK_W - DK // 2, 1), pltpu.roll(x, DK // 2, 1))


def _head_mask(h):
    lane = lax.broadcasted_iota(jnp.int32, (1, QK_W), 1)
    return (lane >= h * DK) & (lane < (h + 1) * DK)


def _ffn_fwd(x, n, cols, gq, wd, name, ride=None):
    t, d = x.shape
    fp = cols.shape[1]
    tm = min(t, FFN_TM)
    tf = FFN_FWD_TF
    nj = fp // tf

    def body(x_ref, n_ref, wg_ref, wu_ref, wd_ref, xo_ref, h_ref, b_ref, sil_ref, dsil_ref, s_ref, acc_ref):
        j = pl.program_id(1)

        @pl.when(j == 0)
        def _():
            xv = x_ref[...]
            r = lax.rsqrt(jnp.mean(xv * xv, axis=-1, keepdims=True) + RMS_EPS)
            h_ref[...] = (xv * r * n_ref[...]).astype(BF16)
            acc_ref[...] = jnp.zeros_like(acc_ref)

        h = h_ref[...]
        a = _dot(h, wg_ref[...])
        b = _dot(h, wu_ref[...])
        sg = _sigmoid(a)
        sil = a * sg
        s = (sil * b).astype(BF16)
        b_ref[...] = b.astype(BF16)
        sil_ref[...] = sil.astype(BF16)
        dsil_ref[...] = (sg + sil * (1.0 - sg)).astype(BF16)
        s_ref[...] = s
        acc_ref[...] += _dot(s, wd_ref[...])

        @pl.when(j == nj - 1)
        def _():
            xo_ref[...] = x_ref[...] + 0.5 * acc_ref[...]

    act = pl.BlockSpec((tm, tf), lambda i, j: (i, j))
    return _pallas(
        body, name, (t // tm, nj),
        [pl.BlockSpec((tm, d), lambda i, j: (i, 0)), pl.BlockSpec((1, d), lambda i, j: (0, 0)),
         pl.BlockSpec((d, tf), lambda i, j: (gq, j)), pl.BlockSpec((d, tf), lambda i, j: (gq + 1, j)),
         pl.BlockSpec((tf, d), lambda i, j: (j, 0))],
        [pl.BlockSpec((tm, d), lambda i, j: (i, 0)), pl.BlockSpec((tm, d), lambda i, j: (i, 0)), act, act, act, act],
        [jax.ShapeDtypeStruct((t, d), F32), jax.ShapeDtypeStruct((t, d), BF16)] + [jax.ShapeDtypeStruct((t, fp), BF16)] * 4,
        [pltpu.VMEM((tm, d), F32)], (x, n, cols, cols, wd), ride)


def _ffn_bwd_act(dxo, b, sil, dsil, wd_t, name, ride=None):
    t, d = dxo.shape
    fp = wd_t.shape[1]
    tm = min(t, FFN_TM)
    tf = fp // 3
    nj = fp // tf

    def body(dxo_ref, b_ref, sil_ref, dsil_ref, wd_ref, da_ref, db_ref, dxob_ref):
        @pl.when(pl.program_id(1) == 0)
        def _():
            dxob_ref[...] = (0.5 * dxo_ref[...]).astype(BF16)

        ds = _dot(dxob_ref[...], wd_ref[...])
        da_ref[...] = (ds * b_ref[...].astype(F32) * dsil_ref[...].astype(F32)).astype(BF16)
        db_ref[...] = (ds * sil_ref[...].astype(F32)).astype(BF16)

    act = pl.BlockSpec((tm, tf), lambda i, j: (i, j))
    return _pallas(
        body, name, (t // tm, nj),
        [pl.BlockSpec((tm, d), lambda i, j: (i, 0)), act, act, act, pl.BlockSpec((d, tf), lambda i, j: (0, j))],
        [act, act, pl.BlockSpec((tm, d), lambda i, j: (i, 0))],
        [jax.ShapeDtypeStruct((t, fp), BF16), jax.ShapeDtypeStruct((t, fp), BF16), jax.ShapeDtypeStruct((t, d), BF16)],
        [], (dxo, b, sil, dsil, wd_t), ride)


def _ffn_bwd_in(da, db, dxo, x, n, cols_t, gq, name, ride=None):
    t, d = x.shape
    fp = cols_t.shape[0]
    tm = min(t, FFN_TM)
    tf = 2 * fp // N_DEV
    nj = fp // tf

    def body(da_ref, db_ref, dxo_ref, x_ref, n_ref, wg_ref, wu_ref, dx_ref, dn_ref, acc_ref):
        i, j = pl.program_id(0), pl.program_id(1)

        @pl.when((i == 0) & (j == 0))
        def _():
            dn_ref[...] = jnp.zeros_like(dn_ref)

        @pl.when(j == 0)
        def _():
            acc_ref[...] = jnp.zeros_like(acc_ref)

        acc_ref[...] += _dot(da_ref[...], wg_ref[...]) + _dot(db_ref[...], wu_ref[...])

        @pl.when(j == nj - 1)
        def _():
            xv = x_ref[...]
            r = lax.rsqrt(jnp.mean(xv * xv, axis=-1, keepdims=True) + RMS_EPS)
            xh = xv * r
            dh = acc_ref[...]
            dn_ref[...] += jnp.sum(dh * xh, axis=0, keepdims=True)
            dhn = dh * n_ref[...]
            dx_ref[...] = dxo_ref[...] + r * (dhn - xh * jnp.mean(dhn * xh, axis=-1, keepdims=True))

    act = pl.BlockSpec((tm, tf), lambda i, j: (i, j))
    row = pl.BlockSpec((tm, d), lambda i, j: (i, 0))
    return _pallas(
        body, name, (t // tm, nj),
        [act, act, row, row, pl.BlockSpec((1, d), lambda i, j: (0, 0)),
         pl.BlockSpec((tf, d), lambda i, j: (j, gq)), pl.BlockSpec((tf, d), lambda i, j: (j, gq + 1))],
        [row, pl.BlockSpec((1, d), lambda i, j: (0, 0))],
        [jax.ShapeDtypeStruct((t, d), F32), jax.ShapeDtypeStruct((1, d), F32)],
        [pltpu.VMEM((tm, d), F32)], (da, db, dxo, x, n, cols_t, cols_t), ride)


def _wgrad(a, b, scale, tk, tn, name, ride=None):
    t, k = a.shape
    n = b.shape[1]
    tt = min(t, WGRAD_TT)
    nt = t // tt

    def body(a_ref, b_ref, o_ref, acc_ref):
        s = pl.program_id(2)

        @pl.when(s == 0)
        def _():
            acc_ref[...] = jnp.zeros_like(acc_ref)

        acc_ref[...] += _dot_tn(a_ref[...], b_ref[...])

        @pl.when(s == nt - 1)
        def _():
            o_ref[...] = (scale * acc_ref[...]).astype(BF16)

    res = _pallas(
        body, name, (k // tk, n // tn, nt),
        [pl.BlockSpec((tt, tk), lambda p, q, s: (s, p)), pl.BlockSpec((tt, tn), lambda p, q, s: (s, q))],
        [pl.BlockSpec((tk, tn), lambda p, q, s: (p, q))], [jax.ShapeDtypeStruct((k, n), BF16)],
        [pltpu.VMEM((tk, tn), F32)], (a, b), ride)
    return res[0] if ride is None else (res[0][0], res[1])


def _mix_in(x, n, w_in, cos_t, sin_t, seq, name):
    t, d = x.shape
    per_seq = seq // TM

    def body(x_ref, n_ref, w_ref, c_ref, s_ref, h_ref, q_ref, k_ref, v_ref, g_ref, u_ref):
        xv = x_ref[...]
        r = lax.rsqrt(jnp.mean(xv * xv, axis=-1, keepdims=True) + RMS_EPS)
        h = (xv * r * n_ref[...]).astype(BF16)
        h_ref[...] = h
        p = _dot(h, w_ref[...])
        c, s = c_ref[...], s_ref[...]
        q = p[:, :QK_W]
        k = p[:, QK_W:2 * QK_W]
        q_ref[...] = ((q * c + _swap_halves(q) * s) * (DK ** -0.5)).astype(BF16)
        k_ref[...] = (k * c + _swap_halves(k) * s).astype(BF16)
        v_ref[...] = p[:, 2 * QK_W:2 * QK_W + V_W].astype(BF16)
        g_ref[...] = p[:, 2 * QK_W + V_W:2 * QK_W + 2 * V_W]
        u_ref[...] = p[:, 2 * QK_W + 2 * V_W:]

    tile = lambda w: pl.BlockSpec((TM, w), lambda i: (i, 0))
    return pl.pallas_call(
        body, name=name, grid=(t // TM,),
        in_specs=[tile(d), pl.BlockSpec((1, d), lambda i: (0, 0)), pl.BlockSpec(w_in.shape, lambda i: (0, 0)),
                  pl.BlockSpec((TM, QK_W), lambda i: (i % per_seq, 0)), pl.BlockSpec((TM, QK_W), lambda i: (i % per_seq, 0))],
        out_specs=[tile(d), tile(QK_W), tile(QK_W), tile(V_W), tile(V_W), tile(POOL_W)],
        out_shape=[jax.ShapeDtypeStruct((t, d), BF16), jax.ShapeDtypeStruct((t, QK_W), BF16),
                   jax.ShapeDtypeStruct((t, QK_W), BF16), jax.ShapeDtypeStruct((t, V_W), BF16),
                   jax.ShapeDtypeStruct((t, V_W), F32), jax.ShapeDtypeStruct((t, POOL_W), F32)],
        compiler_params=_cparams(1),
    )(x, n, w_in, cos_t, sin_t)


def _mix_in_bwd(dp, dx2, x1, n, w_in_t, name, ride=None):
    t, d = x1.shape

    def body(dp_ref, dx2_ref, x_ref, n_ref, w_ref, dx_ref, dn_ref):
        @pl.when(pl.program_id(0) == 0)
        def _():
            dn_ref[...] = jnp.zeros_like(dn_ref)

        dh = _dot(dp_ref[...], w_ref[...])
        xv = x_ref[...]
        r = lax.rsqrt(jnp.mean(xv * xv, axis=-1, keepdims=True) + RMS_EPS)
        xh = xv * r
        dn_ref[...] += jnp.sum(dh * xh, axis=0, keepdims=True)
        dhn = dh * n_ref[...]
        dx_ref[...] = dx2_ref[...] + r * (dhn - xh * jnp.mean(dhn * xh, axis=-1, keepdims=True))

    tile = lambda w: pl.BlockSpec((TM, w), lambda i: (i, 0))
    return _pallas(
        body, name, (t // TM,),
        [tile(dp.shape[1]), tile(d), tile(d), pl.BlockSpec((1, d), lambda i: (0, 0)),
         pl.BlockSpec(w_in_t.shape, lambda i: (0, 0))],
        [tile(d), pl.BlockSpec((1, d), lambda i: (0, 0))],
        [jax.ShapeDtypeStruct((t, d), F32), jax.ShapeDtypeStruct((1, d), F32)],
        [], (dp, dx2, x1, n, w_in_t), ride)


def _group_norm(o):
    parts, rstds = [], []
    for h in range(HEADS):
        oh = o[:, h * DV:(h + 1) * DV]
        dlt = oh - jnp.mean(oh, axis=-1, keepdims=True)
        rstd = lax.rsqrt(jnp.mean(dlt * dlt, axis=-1, keepdims=True) + GN_EPS)
        parts.append(dlt * rstd)
        rstds.append(rstd)
    return jnp.concatenate(parts, axis=1), rstds


def _mix_core_fwd(qs, k, v, g, u, x1, consts, gain, wp, scale, w_out, nseq, seq, name):
    t, d = x1.shape
    nblk = seq // BLK
    mask, dq, dk, gbd, bd = consts

    def body(q_ref, k_ref, v_ref, g_ref, u_ref, x1_ref, m_ref, dq_ref, dk_ref, gbd_ref, bd_ref, gain_ref, wp_ref,
             sc_ref, wo_ref, x2_ref, mix_ref, o_ref, pooled_ref, st_ref, state, halo):
        j = pl.program_id(1)

        @pl.when(j == 0)
        def _():
            state[...] = jnp.zeros_like(state)
            halo[...] = jnp.zeros_like(halo)

        qv, kv, vv = q_ref[...], k_ref[...], v_ref[...]
        st = state[...]
        st_ref[0] = st
        cross = _dot(qv, st.astype(BF16)) * dq_ref[...]
        outs = []
        for h in range(HEADS):
            qh = jnp.where(_head_mask(h), qv, jnp.zeros_like(qv))
            am = (_dot_nt(qh, kv) * m_ref[h]).astype(BF16)
            outs.append(_dot(am, vv[:, h * DV:(h + 1) * DV]))
        o = jnp.concatenate(outs, axis=1) + cross
        o_ref[...] = o
        kd = (kv.astype(F32) * dk_ref[...]).astype(BF16)
        state[...] = gbd_ref[...] * st + _dot_tn(kd, vv) * bd_ref[...]

        gv = g_ref[...]
        nrm, _ = _group_norm(o)
        ret = (gv * _sigmoid(gv)) * (nrm * gain_ref[...])

        uv = u_ref[...]
        c = jnp.concatenate([halo[...], uv], axis=0)
        halo[...] = uv[BLK - HALO:, :]
        pos = j * BLK + lax.broadcasted_iota(jnp.int32, (BLK, 1), 0)
        parts = []
        for gi, w in enumerate(WINDOWS):
            c = c + pltpu.roll(c, w // 2, 0)
            cnt = jnp.minimum(pos + 1, w).astype(F32)
            parts.append(c[HALO:, :GC] / cnt)
            if gi + 1 < len(WINDOWS):
                c = c[:, GC:]
        pooled = (jnp.concatenate(parts, axis=1) - uv).astype(BF16)
        pooled_ref[...] = pooled
        z = jnp.concatenate([_dot(pooled[:, gi * GC:(gi + 1) * GC], wp_ref[gi]) for gi in range(len(WINDOWS))], axis=1)
        mix = jnp.concatenate([ret, z * sc_ref[...]], axis=1).astype(BF16)
        mix_ref[...] = mix
        x2_ref[...] = x1_ref[...] + _dot(mix, wo_ref[...])

    blk = lambda w: pl.BlockSpec((BLK, w), lambda i, j: (i * nblk + j, 0))
    full = lambda a: pl.BlockSpec(a.shape, lambda i, j: (0,) * a.ndim)
    return pl.pallas_call(
        body, name=name, grid=(nseq, nblk),
        in_specs=[blk(QK_W), blk(QK_W), blk(V_W), blk(V_W), blk(POOL_W), blk(d),
                  full(mask), full(dq), full(dk), full(gbd), full(bd), full(gain), full(wp), full(scale), full(w_out)],
        out_specs=[blk(d), blk(d), blk(V_W), blk(POOL_W),
                   pl.BlockSpec((1, QK_W, V_W), lambda i, j: (i * nblk + j, 0, 0))],
        out_shape=[jax.ShapeDtypeStruct((t, d), F32), jax.ShapeDtypeStruct((t, d), BF16),
                   jax.ShapeDtypeStruct((t, V_W), F32), jax.ShapeDtypeStruct((t, POOL_W), BF16),
                   jax.ShapeDtypeStruct((nseq * nblk, QK_W, V_W), F32)],
        scratch_shapes=[pltpu.VMEM((QK_W, V_W), F32), pltpu.VMEM((HALO, POOL_W), F32)],
        compiler_params=_cparams(2),
    )(qs, k, v, g, u, x1, mask, dq, dk, gbd, bd, gain, wp, scale, w_out)


def _mix_core_bwd(dx2, qs, k, v, g, o, pooled, st, consts, gain, wp, scale, w_out, cos_t, sin_t, nseq, seq, name,
                  ride=None):
    t, d = dx2.shape
    nblk = seq // BLK
    mask, dq, dk, gbd, bd = consts
    n_win = len(WINDOWS)

    def body(dx2_ref, q_ref, k_ref, v_ref, g_ref, o_ref, pooled_ref, st_ref, m_ref, dq_ref, dk_ref, gbd_ref, bd_ref,
             gain_ref, wp_ref, sc_ref, wo_ref, c_ref, s_ref,
             dp_ref, dx2b_ref, dgain_ref, dscale_ref, dwp_ref, rstate, carry):
        i, j = pl.program_id(0), pl.program_id(1)

        @pl.when((i == 0) & (j == 0))
        def _():
            dgain_ref[...] = jnp.zeros_like(dgain_ref)
            dscale_ref[...] = jnp.zeros_like(dscale_ref)
            dwp_ref[...] = jnp.zeros_like(dwp_ref)

        @pl.when(j == 0)
        def _():
            rstate[...] = jnp.zeros_like(rstate)
            carry[...] = jnp.zeros_like(carry)

        dx2b = dx2_ref[...].astype(BF16)
        dx2b_ref[...] = dx2b
        dmix = _dot(dx2b, wo_ref[...])
        dret, dpool = dmix[:, :V_W], dmix[:, V_W:]

        gv, ov, gain_v = g_ref[...], o_ref[...], gain_ref[...]
        sg = _sigmoid(gv)
        sil = gv * sg
        nrm, rstds = _group_norm(ov)
        dg = dret * (nrm * gain_v) * (sg * (1.0 + gv * (1.0 - sg)))
        dgn = dret * sil
        dgain_ref[...] += jnp.sum(dgn * nrm, axis=0, keepdims=True)
        dnrm = dgn * gain_v
        do_parts = []
        for h in range(HEADS):
            dn_h = dnrm[:, h * DV:(h + 1) * DV]
            n_h = nrm[:, h * DV:(h + 1) * DV]
            do_parts.append(rstds[h] * (dn_h - jnp.mean(dn_h, axis=-1, keepdims=True)
                                        - n_h * jnp.mean(dn_h * n_h, axis=-1, keepdims=True)))
        do = jnp.concatenate(do_parts, axis=1)
        dob = do.astype(BF16)

        qv, kv, vv = q_ref[...], k_ref[...], v_ref[...]
        stb = st_ref[0].astype(BF16)
        rs = rstate[...]
        rsb = rs.astype(BF16)
        dod = (do * dq_ref[...]).astype(BF16)
        dqs = _dot_nt(dod, stb)
        dst = _dot_tn(qv, dod) * bd_ref[...]
        dkf = dk_ref[...]
        kd = (kv.astype(F32) * dkf).astype(BF16)
        dks = _dot_nt(vv, rsb) * dkf
        dvs = _dot(kd, rsb)
        dv_parts = []
        for h in range(HEADS):
            hm = _head_mask(h)
            qh = jnp.where(hm, qv, jnp.zeros_like(qv))
            mh = m_ref[h]
            am = (_dot_nt(qh, kv) * mh).astype(BF16)
            dpm = (_dot_nt(dob[:, h * DV:(h + 1) * DV], vv[:, h * DV:(h + 1) * DV]) * mh).astype(BF16)
            dqs = dqs + jnp.where(hm, _dot(dpm, kv), 0.0)
            dks = dks + jnp.where(hm, _dot_tn(dpm, qv), 0.0)
            dv_parts.append(_dot_tn(am, dob[:, h * DV:(h + 1) * DV]))
        dvs = dvs + jnp.concatenate(dv_parts, axis=1)
        rstate[...] = dst + gbd_ref[...] * rs

        cv, sv = c_ref[...], s_ref[...]
        dqr = dqs * (DK ** -0.5)
        dq_pre = dqr * cv + _swap_halves(dqr * sv)
        dk_pre = dks * cv + _swap_halves(dks * sv)

        pv = pooled_ref[...]
        sc = sc_ref[...]
        dzb = (dpool * sc).astype(BF16)
        z_parts, dpo_parts = [], []
        for gi in range(n_win):
            p_g = pv[:, gi * GC:(gi + 1) * GC]
            dz_g = dzb[:, gi * GC:(gi + 1) * GC]
            z_parts.append(_dot(p_g, wp_ref[gi]))
            dwp_ref[gi] += _dot_tn(p_g, dz_g)
            dpo_parts.append(_dot_nt(dz_g, wp_ref[gi]))
        dscale_ref[...] += jnp.sum(dpool * jnp.concatenate(z_parts, axis=1), axis=0, keepdims=True)
        dpo = jnp.concatenate(dpo_parts, axis=1)
        pos = (nblk - 1 - j) * BLK + lax.broadcasted_iota(jnp.int32, (BLK, 1), 0)
        e = jnp.concatenate(
            [dpo[:, gi * GC:(gi + 1) * GC] / jnp.minimum(pos + 1, w).astype(F32) for gi, w in enumerate(WINDOWS)], axis=1)
        c = jnp.concatenate([e, carry[...]], axis=0)
        carry[...] = e[:HALO, :]
        rows = BLK + HALO
        lead = []
        for gi, w in enumerate(WINDOWS):
            c = c + pltpu.roll(c, rows - w // 2, 0)
            lead.append(c[:BLK, :GC])
            if gi + 1 < n_win:
                c = c[:, GC:]
        du = jnp.concatenate(lead, axis=1) - dpo

        dp_ref[:, 0:QK_W] = dq_pre.astype(BF16)
        dp_ref[:, QK_W:2 * QK_W] = dk_pre.astype(BF16)
        dp_ref[:, 2 * QK_W:2 * QK_W + V_W] = dvs.astype(BF16)
        dp_ref[:, 2 * QK_W + V_W:2 * QK_W + 2 * V_W] = dg.astype(BF16)
        dp_ref[:, 2 * QK_W + 2 * V_W:] = du.astype(BF16)

    rev = lambda i, j: i * nblk + (nblk - 1 - j)
    blk = lambda w: pl.BlockSpec((BLK, w), lambda i, j: (rev(i, j), 0))
    full = lambda a: pl.BlockSpec(a.shape, lambda i, j: (0,) * a.ndim)
    in_w = 2 * QK_W + 2 * V_W + POOL_W
    return _pallas(
        body, name, (nseq, nblk),
        [blk(d), blk(QK_W), blk(QK_W), blk(V_W), blk(V_W), blk(V_W), blk(POOL_W),
         pl.BlockSpec((1, QK_W, V_W), lambda i, j: (rev(i, j), 0, 0)),
         full(mask), full(dq), full(dk), full(gbd), full(bd), full(gain), full(wp), full(scale), full(w_out),
         pl.BlockSpec((BLK, QK_W), lambda i, j: (nblk - 1 - j, 0)),
         pl.BlockSpec((BLK, QK_W), lambda i, j: (nblk - 1 - j, 0))],
        [blk(in_w), blk(d), pl.BlockSpec((1, V_W), lambda i, j: (0, 0)),
         pl.BlockSpec((1, POOL_W), lambda i, j: (0, 0)), pl.BlockSpec((n_win, GC, GC), lambda i, j: (0, 0, 0))],
        [jax.ShapeDtypeStruct((t, in_w), BF16), jax.ShapeDtypeStruct((t, d), BF16),
         jax.ShapeDtypeStruct((1, V_W), F32), jax.ShapeDtypeStruct((1, POOL_W), F32),
         jax.ShapeDtypeStruct((n_win, GC, GC), F32)],
        [pltpu.VMEM((QK_W, V_W), F32), pltpu.VMEM((HALO, POOL_W), F32)],
        (dx2, qs, k, v, g, o, pooled, st, mask, dq, dk, gbd, bd, gain, wp, scale, w_out, cos_t, sin_t), ride)


def _loss_head(x3, nf, tgt, name):
    t, d = x3.shape

    def body(x_ref, n_ref, t_ref, dx_ref, dn_ref, loss_ref):
        @pl.when(pl.program_id(0) == 0)
        def _():
            dn_ref[...] = jnp.zeros_like(dn_ref)
            loss_ref[...] = jnp.zeros_like(loss_ref)

        xv = x_ref[...]
        nv = n_ref[...]
        r = lax.rsqrt(jnp.mean(xv * xv, axis=-1, keepdims=True) + RMS_EPS)
        xh = xv * r
        err = xh * nv - t_ref[...]
        row = jnp.mean(err * err, axis=-1, keepdims=True)
        loss_ref[...] += 0.5 * jnp.sum(row, axis=0, keepdims=True)
        dy = err * (1.0 / d)
        dn_ref[...] += jnp.sum(dy * xh, axis=0, keepdims=True)
        dxh = dy * nv
        dx_ref[...] = r * (dxh - xh * jnp.mean(dxh * xh, axis=-1, keepdims=True))

    tile = pl.BlockSpec((TM, d), lambda i: (i, 0))
    return pl.pallas_call(
        body, name=name, grid=(t // TM,),
        in_specs=[tile, pl.BlockSpec((1, d), lambda i: (0, 0)), tile],
        out_specs=[tile, pl.BlockSpec((1, d), lambda i: (0, 0)), pl.BlockSpec((1, 1), lambda i: (0, 0))],
        out_shape=[jax.ShapeDtypeStruct((t, d), F32), jax.ShapeDtypeStruct((1, d), F32), jax.ShapeDtypeStruct((1, 1), F32)],
        compiler_params=_cparams(1),
    )(x3, nf, tgt)


def _coords():
    return lax.axis_index("x"), lax.axis_index("y"), lax.axis_index("c")


def _window(ref, kind, idx, size):
    if kind == "col":
        return ref.at[:, pl.ds(pl.multiple_of(idx * size, LANE), size)]
    return ref.at[pl.ds(pl.multiple_of(idx * size, 8), size), :]


def _run_exchange(ex, name):
    n_in = len(ex.inputs)

    def body(*refs):
        ins, outs, sems = refs[:n_in], refs[n_in:n_in + len(ex.out_shape)], refs[n_in + len(ex.out_shape):]
        ex.start(ins, outs, sems)
        if ex.mid is not None:
            ex.mid(ins, outs, sems)
        ex.finish(ins, outs, sems)

    return pl.pallas_call(body, name=name, in_specs=[ANY] * n_in, out_specs=[ANY] * len(ex.out_shape),
                          out_shape=ex.out_shape, scratch_shapes=ex.scratch)(*ex.inputs)


def _join(exchanges):
    bounds = []
    i0 = o0 = s0 = 0
    for ex in exchanges:
        bounds.append((i0, o0, s0))
        i0, o0, s0 = i0 + len(ex.inputs), o0 + len(ex.out_shape), s0 + len(ex.scratch)

    def phase(which):
        def run(ins, outs, sems):
            for ex, (i, o, s) in zip(exchanges, bounds):
                fn = getattr(ex, which)
                if fn is not None:
                    fn(ins[i:i + len(ex.inputs)], outs[o:o + len(ex.out_shape)], sems[s:s + len(ex.scratch)])
        return run

    return _Exchange(sum((ex.inputs for ex in exchanges), []), sum((ex.out_shape for ex in exchanges), []),
                     sum((ex.scratch for ex in exchanges), []), phase("start"), phase("finish"),
                     phase("mid") if any(ex.mid is not None for ex in exchanges) else None)


def _gather_exchange(parts):
    n = len(parts)
    kinds = [kd for _, kd in parts]
    sizes = [a.shape[1] if kd == "col" else a.shape[0] for a, kd in parts]

    def plan(ins, outs, sems):
        send_sems, recv_sems, local_sems = sems
        x, y, c = _coords()
        me, sibling = (x, y, c), (x, y, 1 - c)
        chips = [(1 - x, y), (x, 1 - y), (1 - x, 1 - y)]

        def win(p, dev):
            return _window(outs[p], kinds[p], 4 * dev[0] + 2 * dev[1] + dev[2], sizes[p])

        def copy(p, k, block, to, src=None):
            return pltpu.make_async_remote_copy(
                src_ref=win(p, block) if src is None else src, dst_ref=win(p, block),
                send_sem=send_sems.at[p * 7 + k], recv_sem=recv_sems.at[p * 7 + k], device_id=to, device_id_type=MESH_ID)

        mine = [pltpu.make_async_copy(ins[p], win(p, me), local_sems.at[p]) for p in range(n)]
        first, arrived, passed, rest = [], [], [], []
        for p in range(n):
            first.append(copy(p, 0, me, sibling, src=ins[p]))
            first += [copy(p, 1 + q, me, (*chip, c), src=ins[p]) for q, chip in enumerate(chips)]
            rest.append(copy(p, 0, sibling, me))
            rest += [copy(p, 4 + q, (*chip, 1 - c), me) for q, chip in enumerate(chips)]
        for q, chip in enumerate(chips):
            for p in range(n):
                arrived.append(copy(p, 1 + q, (*chip, c), me))
                passed.append(copy(p, 4 + q, (*chip, c), sibling))
        return mine, first, arrived, passed, rest

    def start(ins, outs, sems):
        mine, first, _, _, _ = plan(ins, outs, sems)
        for cp in mine + first:
            cp.start()

    def mid(ins, outs, sems):
        _, _, arrived, passed, _ = plan(ins, outs, sems)
        for got, fwd in zip(arrived, passed):
            got.wait_recv()
            fwd.start()

    def finish(ins, outs, sems):
        mine, first, _, passed, rest = plan(ins, outs, sems)
        for cp in rest:
            cp.wait_recv()
        for cp in first + passed:
            cp.wait_send()
        for cp in mine:
            cp.wait()

    out_shape = [jax.ShapeDtypeStruct((a.shape[0], N_DEV * a.shape[1]) if kd == "col" else (N_DEV * a.shape[0], a.shape[1]),
                                      a.dtype) for a, kd in parts]
    scratch = [pltpu.SemaphoreType.DMA((7 * n,)), pltpu.SemaphoreType.DMA((7 * n,)), pltpu.SemaphoreType.DMA((n,))]
    return _Exchange([a for a, _ in parts], out_shape, scratch, start, finish, mid)


def _all_gather(parts, name):
    return _run_exchange(_gather_exchange(parts), name)


def _shard_shape(a, kd):
    return (a.shape[0], a.shape[1] // N_DEV) if kd == "col" else (a.shape[0] // N_DEV, a.shape[1])


def _symmetric_exchange(inputs, out_shape, n_copies, plan):
    def start(ins, outs, sems):
        for cp in plan(ins, outs, sems):
            cp.start()

    def finish(ins, outs, sems):
        copies = plan(ins, outs, sems)
        for cp in copies:
            cp.wait_recv()
        for cp in copies:
            cp.wait_send()

    scratch = [pltpu.SemaphoreType.DMA((n_copies,)), pltpu.SemaphoreType.DMA((n_copies,))]
    return _Exchange(inputs, out_shape, scratch, start, finish)


def _rs_pair_exchange(grads):
    n = len(grads)
    kinds = [kd for _, kd in grads]
    shapes = [_shard_shape(a, kd) for a, kd in grads]

    def plan(ins, outs, sems):
        send_sems, recv_sems = sems
        x, y, c = _coords()
        copies = []
        for p in range(n):
            size = shapes[p][1] if kinds[p] == "col" else shapes[p][0]
            for s in range(4):
                src = _window(ins[p], kinds[p], 2 * s + (1 - c), size)
                copies.append(pltpu.make_async_remote_copy(
                    src_ref=src, dst_ref=outs[p].at[s], send_sem=send_sems.at[4 * p + s], recv_sem=recv_sems.at[4 * p + s],
                    device_id=(x, y, 1 - c), device_id_type=MESH_ID))
        return copies

    return _symmetric_exchange([a for a, _ in grads], [jax.ShapeDtypeStruct((4,) + shapes[p], BF16) for p in range(n)],
                               4 * n, plan)


def _rs_chips_exchange(sums):
    n = len(sums)

    def plan(ins, outs, sems):
        send_sems, recv_sems = sems
        x, y, c = _coords()
        chips = [(1 - x, y), (x, 1 - y), (1 - x, 1 - y)]
        copies = []
        for p in range(n):
            for q, (cx, cy) in enumerate(chips):
                copies.append(pltpu.make_async_remote_copy(
                    src_ref=ins[p].at[2 * cx + cy], dst_ref=outs[p].at[q],
                    send_sem=send_sems.at[3 * p + q], recv_sem=recv_sems.at[3 * p + q],
                    device_id=(cx, cy, c), device_id_type=MESH_ID))
        return copies

    return _symmetric_exchange(list(sums), [jax.ShapeDtypeStruct((3,) + a.shape[1:], BF16) for a in sums], 3 * n, plan)


def _rs_pair(grads, name):
    return _run_exchange(_rs_pair_exchange(grads), name)


def _rs_chips(sums, name):
    return _run_exchange(_rs_chips_exchange(sums), name)


def _pair_sum(grad, kd, recv, core, name):
    _, r, cw = recv.shape
    tr = min(r, TM)

    def body(core_ref, g_ref, r_ref, o_ref):
        del core_ref
        o_ref[0] = (g_ref[...].astype(F32) + r_ref[0].astype(F32)).astype(BF16)

    if kd == "col":
        g_spec = pl.BlockSpec((tr, cw), lambda s, i, core_ref: (i, 2 * s + core_ref[0]))
    else:
        g_spec = pl.BlockSpec((tr, cw), lambda s, i, core_ref: ((2 * s + core_ref[0]) * (r // tr) + i, 0))
    grid_spec = pltpu.PrefetchScalarGridSpec(
        num_scalar_prefetch=1, grid=(4, r // tr),
        in_specs=[g_spec, pl.BlockSpec((1, tr, cw), lambda s, i, core_ref: (s, i, 0))],
        out_specs=pl.BlockSpec((1, tr, cw), lambda s, i, core_ref: (s, i, 0)))
    return pl.pallas_call(
        body, name=name, grid_spec=grid_spec, out_shape=jax.ShapeDtypeStruct(recv.shape, BF16),
        compiler_params=_cparams(2),
    )(core, grad, recv)


def _adam_math(w, g, m, v):
    m2 = B1 * m + (1.0 - B1) * g
    v2 = B2 * v + (1.0 - B2) * (g * g)
    m_hat = m2 / (1.0 - B1 ** STEP)
    v_hat = v2 / (1.0 - B2 ** STEP)
    delta = -LR * (m_hat / (jnp.sqrt(v_hat) + ADAM_EPS) + WD * w)
    return delta, m2, v2


def _chip_sum_adam(psum, recv, chip, w, m, v, name):
    r, cw = w.shape
    pc = psum.shape[2]
    tr = min(r, TM)

    def body(chip_ref, p_ref, r_ref, w_ref, m_ref, v_ref, g_ref, d_ref, m2_ref, v2_ref):
        del chip_ref
        g = p_ref[0].astype(F32) + r_ref[0].astype(F32) + r_ref[1].astype(F32) + r_ref[2].astype(F32)
        g = g[:, :cw]
        delta, m2, v2 = _adam_math(w_ref[...], g, m_ref[...], v_ref[...])
        g_ref[...] = g
        d_ref[...] = delta
        m2_ref[...] = m2
        v2_ref[...] = v2

    loc = pl.BlockSpec((tr, cw), lambda i, chip_ref: (i, 0))
    grid_spec = pltpu.PrefetchScalarGridSpec(
        num_scalar_prefetch=1, grid=(r // tr,),
        in_specs=[pl.BlockSpec((1, tr, pc), lambda i, chip_ref: (chip_ref[0], i, 0)),
                  pl.BlockSpec((3, tr, pc), lambda i, chip_ref: (0, i, 0)), loc, loc, loc],
        out_specs=[loc, loc, loc, loc])
    return pl.pallas_call(
        body, name=name, grid_spec=grid_spec, out_shape=[jax.ShapeDtypeStruct((r, cw), F32)] * 4,
        compiler_params=_cparams(1),
    )(chip, psum, recv, w, m, v)


def _small_allreduce_adam(partials, params, moms, vels, name):
    n = len(partials)
    row0 = []
    rows = 0
    for a in partials:
        row0.append(rows)
        rows += _pad_to(a.shape[0], 8)
    width = max(a.shape[1] for a in partials)

    def body(*refs):
        g_in = refs[:n]
        w_in, m_in, v_in = refs[n:2 * n], refs[2 * n:3 * n], refs[3 * n:4 * n]
        outs = refs[4 * n:8 * n]
        slab, send_sems, recv_sems = refs[8 * n:]
        x, y, c = _coords()
        me = 4 * x + 2 * y + c
        slab[me] = jnp.zeros((rows, width), F32)
        for p in range(n):
            r, cw = partials[p].shape
            slab[me, row0[p]:row0[p] + r, 0:cw] = g_in[p][...]
        copies = []
        for q in range(1, N_DEV):
            peer = me ^ q
            copies.append(pltpu.make_async_remote_copy(
                src_ref=slab.at[me], dst_ref=slab.at[me], send_sem=send_sems.at[q - 1], recv_sem=recv_sems.at[q - 1],
                device_id=(peer >> 2, (peer >> 1) & 1, peer & 1), device_id_type=MESH_ID))
        for cp in copies:
            cp.start()
        for cp in copies:
            cp.wait_recv()
        for cp in copies:
            cp.wait_send()
        for p in range(n):
            r, cw = partials[p].shape
            g = slab[0, row0[p]:row0[p] + r, 0:cw]
            for dev in range(1, N_DEV):
                g = g + slab[dev, row0[p]:row0[p] + r, 0:cw]
            delta, m2, v2 = _adam_math(w_in[p][...], g, m_in[p][...], v_in[p][...])
            outs[4 * p][...] = g
            outs[4 * p + 1][...] = delta
            outs[4 * p + 2][...] = m2
            outs[4 * p + 3][...] = v2

    out_shape = []
    for a in partials:
        out_shape += [jax.ShapeDtypeStruct(a.shape, F32)] * 4
    return pl.pallas_call(
        body, name=name, in_specs=[VMEM_SPEC] * (4 * n), out_specs=[VMEM_SPEC] * (4 * n), out_shape=out_shape,
        scratch_shapes=[pltpu.VMEM((N_DEV, rows, width), F32), pltpu.SemaphoreType.DMA((N_DEV - 1,)),
                        pltpu.SemaphoreType.DMA((N_DEV - 1,))],
    )(*partials, *params, *moms, *vels)


def _local_step(xf, tgt, nseq, seq, cols1_all, d1_all, later, small_w, core=None):
    d = xf.shape[1]
    dist = core is not None
    n1, n2, gain, pool_w, pool_scale, n3, nf = small_w
    tf = 2 * cols1_all.shape[1] // N_DEV
    consts = _retention_constants()
    cos_t, sin_t = _rotary_tables(seq)
    wp_b = pool_w.astype(BF16)

    def pair_sums(grads, recv, names):
        return [_pair_sum(g, kd, r, core, "pair_sum_" + nm) for (g, kd), r, nm in zip(grads, recv, names)]

    if dist:
        (x1, h1, b1, sil1, dsil1, s1), later = _ffn_fwd(xf, n1, cols1_all, 0, d1_all, "ffn1_fwd",
                                                       ride=_gather_exchange(later))
    else:
        x1, h1, b1, sil1, dsil1, s1 = _ffn_fwd(xf, n1, cols1_all, 0, d1_all, "ffn1_fwd")
    cols2_all, d2_all, win_all, wout_all = later
    h2, qs, kr, vv, gg, uu = _mix_in(x1, n2, win_all, cos_t, sin_t, seq, "mix_in")
    x2, mix, oo, pooled, states = _mix_core_fwd(qs, kr, vv, gg, uu, x1, consts, gain, wp_b, pool_scale, wout_all,
                                                 nseq, seq, "mix_core_fwd")
    x3, h3, b3, sil3, dsil3, s3 = _ffn_fwd(x2, n3, cols2_all, 0, d2_all, "ffn2_fwd")
    dx3, dnf, loss_part = _loss_head(x3, nf, tgt, "loss_head")

    cols1_t = _transpose(cols1_all, "transpose_cols1")
    cols2_t = _transpose(cols2_all, "transpose_cols2")
    d1_t = _transpose(d1_all, "transpose_down1")
    d2_t = _transpose(d2_all, "transpose_down2")
    win_t = _transpose(win_all, "transpose_w_in")
    wout_t = _transpose(wout_all, "transpose_w_out")
    out = {}

    da3, db3, dx3b = _ffn_bwd_act(dx3, b3, sil3, dsil3, d2_t, "ffn2_bwd_act")
    names2 = ["ffn2_gate", "ffn2_up", "ffn2_down"]
    grads2 = [(_wgrad(h3, da3, 1.0, d, tf, "wgrad_gate2"), "col"), (_wgrad(h3, db3, 1.0, d, tf, "wgrad_up2"), "col"),
              (_wgrad(s3, dx3b, 1.0, tf, d, "wgrad_down2"), "row")]
    if dist:
        (dx2, dn3), recv2 = _ffn_bwd_in(da3, db3, dx3, x2, n3, cols2_t, 0, "ffn2_bwd_in", ride=_rs_pair_exchange(grads2))
        sums2 = pair_sums(grads2, recv2, names2)
        (dp, dx2b, dgain, dscale, dwp), crecv2 = _mix_core_bwd(
            dx2, qs, kr, vv, gg, oo, pooled, states, consts, gain, wp_b, pool_scale, wout_t, cos_t, sin_t, nseq, seq,
            "mix_core_bwd", ride=_rs_chips_exchange(sums2))
        out.update({nm: (s, r) for nm, s, r in zip(names2, sums2, crecv2)})
    else:
        dx2, dn3 = _ffn_bwd_in(da3, db3, dx3, x2, n3, cols2_t, 0, "ffn2_bwd_in")
        dp, dx2b, dgain, dscale, dwp = _mix_core_bwd(dx2, qs, kr, vv, gg, oo, pooled, states, consts, gain, wp_b,
                                                     pool_scale, wout_t, cos_t, sin_t, nseq, seq, "mix_core_bwd")
        out.update(dict(zip(names2, grads2)))

    names_m = ["w_in", "w_out"]
    grads_m = [(_wgrad(h2, dp, 1.0, d, d, "wgrad_in"), "col"), (_wgrad(mix, dx2b, 1.0, d, d, "wgrad_out"), "row")]
    if dist:
        (dx1, dn2), recv_m = _mix_in_bwd(dp, dx2, x1, n2, win_t, "mix_in_bwd", ride=_rs_pair_exchange(grads_m))
        sums_m = pair_sums(grads_m, recv_m, names_m)
        (da1, db1, dx1b), crecv_m = _ffn_bwd_act(dx1, b1, sil1, dsil1, d1_t, "ffn1_bwd_act",
                                                 ride=_rs_chips_exchange(sums_m))
        out.update({nm: (s, r) for nm, s, r in zip(names_m, sums_m, crecv_m)})
    else:
        dx1, dn2 = _mix_in_bwd(dp, dx2, x1, n2, win_t, "mix_in_bwd")
        da1, db1, dx1b = _ffn_bwd_act(dx1, b1, sil1, dsil1, d1_t, "ffn1_bwd_act")
        out.update(dict(zip(names_m, grads_m)))

    dx0, dn1 = _ffn_bwd_in(da1, db1, dx1, xf, n1, cols1_t, 0, "ffn1_bwd_in")
    g_gate = (_wgrad(h1, da1, 1.0, d, tf, "wgrad_gate1"), "col")
    if dist:
        g_up, recv_g = _wgrad(h1, db1, 1.0, d, tf, "wgrad_up1", ride=_rs_pair_exchange([g_gate]))
        g_up = (g_up, "col")
        sum_g = pair_sums([g_gate], recv_g, ["ffn1_gate"])
        g_down, (crecv_g, recv_u) = _wgrad(s1, dx1b, 1.0, tf, d, "wgrad_down1",
                                           ride=_join([_rs_chips_exchange(sum_g), _rs_pair_exchange([g_up])]))
        g_down = (g_down, "row")
        sum_u = pair_sums([g_up], [recv_u], ["ffn1_up"])
        crecv_u, recv_d = _run_exchange(_join([_rs_chips_exchange(sum_u), _rs_pair_exchange([g_down])]), "rs_tail_up_down")
        sum_d = pair_sums([g_down], [recv_d], ["ffn1_down"])
        (crecv_d,) = _run_exchange(_rs_chips_exchange(sum_d), "rs_tail_down")
        out.update({"ffn1_gate": (sum_g[0], crecv_g), "ffn1_up": (sum_u[0], crecv_u), "ffn1_down": (sum_d[0], crecv_d)})
    else:
        out.update({"ffn1_gate": g_gate, "ffn1_up": (_wgrad(h1, db1, 1.0, d, tf, "wgrad_up1"), "col"),
                    "ffn1_down": (_wgrad(s1, dx1b, 1.0, tf, d, "wgrad_down1"), "row")})
    return loss_part, dx0, out, (dn1, dn2, dgain, dwp, dscale, dn3, dnf)


def kernel(x, norm_ffn1, ffn1_gate, ffn1_up, ffn1_down, norm_mix, w_in, ret_gn_gain, pool_w, pool_scale, w_out, norm_ffn2, ffn2_gate, ffn2_up, ffn2_down, norm_final, loss_target, m_norm_ffn1, m_ffn1_gate, m_ffn1_up, m_ffn1_down, m_norm_mix, m_w_in, m_ret_gn_gain, m_pool_w, m_pool_scale, m_w_out, m_norm_ffn2, m_ffn2_gate, m_ffn2_up, m_ffn2_down, m_norm_final, v_norm_ffn1, v_ffn1_gate, v_ffn1_up, v_ffn1_down, v_norm_mix, v_w_in, v_ret_gn_gain, v_pool_w, v_pool_scale, v_w_out, v_norm_ffn2, v_ffn2_gate, v_ffn2_up, v_ffn2_down, v_norm_final):
    nseq, seq, d = x.shape
    t = nseq * seq
    f_loc = ffn1_gate.shape[2]
    f_pad = _pad_to(f_loc, LANE)
    xf = x.reshape(t, d)
    tgt = loss_target.reshape(t, d)
    core = lax.axis_index("c").astype(jnp.int32).reshape(1)
    chip = (2 * lax.axis_index("x") + lax.axis_index("y")).astype(jnp.int32).reshape(1)

    colp = lambda w: jnp.pad(w[0].astype(BF16), ((0, 0), (0, f_pad - f_loc)))
    rowp = lambda w: jnp.pad(w[0].astype(BF16), ((0, f_pad - f_loc), (0, 0)))
    cols1 = jnp.concatenate([colp(ffn1_gate), colp(ffn1_up)], axis=0)
    cols2 = jnp.concatenate([colp(ffn2_gate), colp(ffn2_up)], axis=0)
    cols1_all, d1_all = _all_gather([(cols1, "col"), (rowp(ffn1_down), "row")], "all_gather_ffn1")
    later = [(cols2, "col"), (rowp(ffn2_down), "row"), (w_in[0].astype(BF16), "col"), (w_out[0].astype(BF16), "row")]

    small_w = (norm_ffn1, norm_mix, ret_gn_gain, pool_w[0], pool_scale, norm_ffn2, norm_final.reshape(1, d))
    loss_part, dx0, reduced, small_parts = _local_step(xf, tgt, nseq, seq, cols1_all, d1_all, later, small_w, core)
    dn1, dn2, dgain, dwp, dscale, dn3, dnf = small_parts

    local = {"ffn1_gate": (ffn1_gate, m_ffn1_gate, v_ffn1_gate), "ffn1_up": (ffn1_up, m_ffn1_up, v_ffn1_up),
             "ffn1_down": (ffn1_down, m_ffn1_down, v_ffn1_down), "w_in": (w_in, m_w_in, v_w_in),
             "w_out": (w_out, m_w_out, v_w_out), "ffn2_gate": (ffn2_gate, m_ffn2_gate, v_ffn2_gate),
             "ffn2_up": (ffn2_up, m_ffn2_up, v_ffn2_up), "ffn2_down": (ffn2_down, m_ffn2_down, v_ffn2_down)}
    big = {}
    for nm, (w, m, v) in local.items():
        ps, rcv = reduced[nm]
        g, dlt, m2, v2 = _chip_sum_adam(ps, rcv, chip, w[0], m[0], v[0], "adam_" + nm)
        big[nm] = tuple(a[None] for a in (g, dlt, m2, v2))

    small_names = ["norm_ffn1", "norm_mix", "ret_gn_gain", "pool_w", "pool_scale", "norm_ffn2", "norm_final"]
    flat = lambda a: a.reshape(pool_w.size // d, d)
    partials = [dn1, dn2, dgain, flat(dwp), dscale, dn3, dnf]
    params = [norm_ffn1, norm_mix, ret_gn_gain, flat(pool_w), pool_scale, norm_ffn2, norm_final.reshape(1, d)]
    moms = [m_norm_ffn1, m_norm_mix, m_ret_gn_gain, flat(m_pool_w), m_pool_scale, m_norm_ffn2, m_norm_final.reshape(1, d)]
    vels = [v_norm_ffn1, v_norm_mix, v_ret_gn_gain, flat(v_pool_w), v_pool_scale, v_norm_ffn2, v_norm_final.reshape(1, d)]
    small_out = _small_allreduce_adam(partials, params, moms, vels, "small_allreduce_adam")
    shapes = [norm_ffn1.shape, norm_mix.shape, ret_gn_gain.shape, pool_w.shape, pool_scale.shape, norm_ffn2.shape,
              norm_final.shape]
    small = {nm: tuple(small_out[4 * p + q].reshape(shapes[p]) for q in range(4)) for p, nm in enumerate(small_names)}

    loss = lax.psum(loss_part[0, 0], ("x", "y", "c"))
    order = ["norm_ffn1", "ffn1_gate", "ffn1_up", "ffn1_down", "norm_mix", "w_in", "ret_gn_gain", "pool_w", "pool_scale",
             "w_out", "norm_ffn2", "ffn2_gate", "ffn2_up", "ffn2_down", "norm_final"]
    both = {**big, **small}
    outs = [loss, dx0.reshape(nseq, seq, d)]
    for q in range(4):
        outs += [both[nm][q] for nm in order]
    return tuple(outs)
```

```python
import functools

import numpy as np
import jax
import jax.numpy as jnp
from jax import lax
from jax.experimental import pallas as pl
from jax.experimental.pallas import tpu as pltpu

F32, BF16 = jnp.float32, jnp.bfloat16
MESH_ID = pl.DeviceIdType.MESH
ANY = pl.BlockSpec(memory_space=pl.ANY)
VMEM_SPEC = pl.BlockSpec(memory_space=pltpu.VMEM)

N_DEV = 8
RMS_EPS = 1e-6
GN_EPS = 1e-5
HEADS, DK, DV = 4, 64, 128
QK_W, V_W, POOL_W = HEADS * DK, HEADS * DV, 512
WINDOWS = (2, 4, 8, 16)
GC = POOL_W // len(WINDOWS)
CHUNK = 64
BLK = 4 * CHUNK
HALO = 16
ROPE_BASE = 10000.0
LR, B1, B2, ADAM_EPS, WD, STEP = 0.001, 0.9, 0.999, 1e-08, 0.01, 10
LANE = 128
TM = 512
FFN_TM = 1024
FFN_FWD_TF = 512
WGRAD_TT = 2048
VMEM_LIMIT = 56 * 1024 * 1024


def _cparams(n_axes):
    return pltpu.CompilerParams(dimension_semantics=("arbitrary",) * n_axes, vmem_limit_bytes=VMEM_LIMIT)


class _Exchange:
    def __init__(self, inputs, out_shape, scratch, start, finish, mid=None):
        self.inputs, self.out_shape, self.scratch = list(inputs), list(out_shape), list(scratch)
        self.start, self.finish, self.mid = start, finish, mid


def _pallas(body, name, grid, in_specs, out_specs, out_shape, scratch_shapes, args, ride=None):
    n_axes = len(grid)
    if ride is None:
        return pl.pallas_call(body, name=name, grid=grid, in_specs=in_specs, out_specs=out_specs, out_shape=out_shape,
                              scratch_shapes=scratch_shapes, compiler_params=_cparams(n_axes))(*args)
    n_in, n_out, n_scr = len(in_specs), len(out_specs), len(scratch_shapes)
    r_in, r_out = len(ride.inputs), len(ride.out_shape)

    def hosted(*refs):
        ins, refs = refs[:n_in], refs[n_in:]
        r_ins, refs = refs[:r_in], refs[r_in:]
        outs, refs = refs[:n_out], refs[n_out:]
        r_outs, refs = refs[:r_out], refs[r_out:]
        scr, sems = refs[:n_scr], refs[n_scr:]
        ids = [pl.program_id(a) for a in range(n_axes)]
        first, last, inner0 = ids[0] == 0, ids[0] == grid[0] - 1, None
        for a in range(1, n_axes):
            first = first & (ids[a] == 0)
            last = last & (ids[a] == grid[a] - 1)
            inner0 = (ids[a] == 0) if inner0 is None else inner0 & (ids[a] == 0)

        @pl.when(first)
        def _():
            ride.start(r_ins, r_outs, sems)

        if ride.mid is not None:
            at_mid = ids[0] == (3 * grid[0]) // 4
            if inner0 is not None:
                at_mid = at_mid & inner0

            @pl.when(at_mid)
            def _():
                ride.mid(r_ins, r_outs, sems)

        body(*ins, *outs, *scr)

        @pl.when(last)
        def _():
            ride.finish(r_ins, r_outs, sems)

    res = pl.pallas_call(
        hosted, name=name, grid=grid, in_specs=list(in_specs) + [ANY] * r_in, out_specs=list(out_specs) + [ANY] * r_out,
        out_shape=list(out_shape) + ride.out_shape, scratch_shapes=list(scratch_shapes) + ride.scratch,
        compiler_params=_cparams(n_axes))(*args, *ride.inputs)
    return res[:n_out], res[n_out:]


def _dot(a, b):
    return jnp.dot(a, b, preferred_element_type=F32)


def _dot_nt(a, b):
    return lax.dot_general(a, b, (((1,), (1,)), ((), ())), preferred_element_type=F32)


def _dot_tn(a, b):
    return lax.dot_general(a, b, (((0,), (0,)), ((), ())), preferred_element_type=F32)


def _sigmoid(x):
    return 0.5 * jnp.tanh(0.5 * x) + 0.5


def _transpose(w, name):
    r, c = w.shape
    tb = 512

    def body(x_ref, o_ref):
        o_ref[...] = x_ref[...].T

    return pl.pallas_call(
        body, name=name, grid=(r // tb, c // tb),
        in_specs=[pl.BlockSpec((tb, tb), lambda i, j: (i, j))],
        out_specs=pl.BlockSpec((tb, tb), lambda i, j: (j, i)),
        out_shape=jax.ShapeDtypeStruct((c, r), w.dtype),
        compiler_params=_cparams(2),
    )(w)


def _pad_to(n, m):
    return (n + m - 1) // m * m


def _retention_constants():
    gamma = (1.0 - 2.0 ** (-5.0 - np.arange(HEADS, dtype=np.float32))).astype(np.float32)
    log_g = np.log(gamma).astype(np.float32)
    i = np.arange(BLK)
    diff = (i[:, None] - i[None, :]).astype(np.float32)
    same = (i[:, None] // CHUNK) == (i[None, :] // CHUNK)
    earlier = (i[None, :] // CHUNK) < (i[:, None] // CHUNK)
    mask = np.zeros((HEADS, BLK, BLK), np.float32)
    for h in range(HEADS):
        dec_abs = np.exp(log_g[h] * np.abs(diff)).astype(np.float32)
        dec = np.exp(log_g[h] * diff * earlier).astype(np.float32)
        mask[h] = np.where(same, dec_abs, np.where(earlier, dec, 0.0))
    dq = np.zeros((BLK, V_W), np.float32)
    dk = np.zeros((BLK, QK_W), np.float32)
    gbd = np.zeros((QK_W, V_W), np.float32)
    for h in range(HEADS):
        dq[:, h * DV:(h + 1) * DV] = np.exp(log_g[h] * (i + 1.0)).astype(np.float32)[:, None]
        dk[:, h * DK:(h + 1) * DK] = np.exp(log_g[h] * (BLK - 1.0 - i)).astype(np.float32)[:, None]
        gbd[h * DK:(h + 1) * DK, h * DV:(h + 1) * DV] = np.exp(log_g[h] * np.float32(BLK))
    bd = (gbd > 0).astype(np.float32)
    return jnp.asarray(mask), jnp.asarray(dq), jnp.asarray(dk), jnp.asarray(gbd), jnp.asarray(bd)


def _rotary_tables(seq):
    half = DK // 2
    freqs = ROPE_BASE ** (-jnp.arange(half, dtype=F32) * 2.0 / DK)
    ang = jnp.arange(seq, dtype=F32)[:, None] * freqs[None, :]
    cos, sin = jnp.cos(ang), jnp.sin(ang)
    cos_t = jnp.tile(jnp.concatenate([cos, cos], axis=1), (1, HEADS))
    sin_t = jnp.tile(jnp.concatenate([-sin, sin], axis=1), (1, HEADS))
    return cos_t, sin_t


def _swap_halves(x):
    lane = lax.broadcasted_iota(jnp.int32, (1, QK_W), 1)
    first = (lane & (DK - 1)) < DK // 2
    return jnp.where(first, pltpu.roll(x, QK_W - DK // 2, 1), pltpu.roll(x, DK // 2, 1))


def _head_mask(h):
    lane = lax.broadcasted_iota(jnp.int32, (1, QK_W), 1)
    return (lane >= h * DK) & (lane < (h + 1) * DK)


def _ffn_fwd(x, n, cols, gq, wd, name, ride=None):
    t, d = x.shape
    fp = cols.shape[1]
    tm = min(t, FFN_TM)
    tf = FFN_FWD_TF
    nj = fp // tf

    def body(x_ref, n_ref, wg_ref, wu_ref, wd_ref, xo_ref, h_ref, b_ref, sil_ref, dsil_ref, s_ref, acc_ref):
        j = pl.program_id(1)

        @pl.when(j == 0)
        def _():
            xv = x_ref[...]
            r = lax.rsqrt(jnp.mean(xv * xv, axis=-1, keepdims=True) + RMS_EPS)
            h_ref[...] = (xv * r * n_ref[...]).astype(BF16)
            acc_ref[...] = jnp.zeros_like(acc_ref)

        h = h_ref[...]
        a = _dot(h, wg_ref[...])
        b = _dot(h, wu_ref[...])
        sg = _sigmoid(a)
        sil = a * sg
        s = (sil * b).astype(BF16)
        b_ref[...] = b.astype(BF16)
        sil_ref[...] = sil.astype(BF16)
        dsil_ref[...] = (sg + sil * (1.0 - sg)).astype(BF16)
        s_ref[...] = s
        acc_ref[...] += _dot(s, wd_ref[...])

        @pl.when(j == nj - 1)
        def _():
            xo_ref[...] = x_ref[...] + 0.5 * acc_ref[...]

    act = pl.BlockSpec((tm, tf), lambda i, j: (i, j))
    return _pallas(
        body, name, (t // tm, nj),
        [pl.BlockSpec((tm, d), lambda i, j: (i, 0)), pl.BlockSpec((1, d), lambda i, j: (0, 0)),
         pl.BlockSpec((d, tf), lambda i, j: (gq, j)), pl.BlockSpec((d, tf), lambda i, j: (gq + 1, j)),
         pl.BlockSpec((tf, d), lambda i, j: (j, 0))],
        [pl.BlockSpec((tm, d), lambda i, j: (i, 0)), pl.BlockSpec((tm, d), lambda i, j: (i, 0)), act, act, act, act],
        [jax.ShapeDtypeStruct((t, d), F32), jax.ShapeDtypeStruct((t, d), BF16)] + [jax.ShapeDtypeStruct((t, fp), BF16)] * 4,
        [pltpu.VMEM((tm, d), F32)], (x, n, cols, cols, wd), ride)


def _ffn_bwd_act(dxo, b, sil, dsil, wd_t, name, ride=None):
    t, d = dxo.shape
    fp = wd_t.shape[1]
    tm = min(t, FFN_TM)
    tf = fp // 3
    nj = fp // tf

    def body(dxo_ref, b_ref, sil_ref, dsil_ref, wd_ref, da_ref, db_ref, dxob_ref):
        @pl.when(pl.program_id(1) == 0)
        def _():
            dxob_ref[...] = (0.5 * dxo_ref[...]).astype(BF16)

        ds = _dot(dxob_ref[...], wd_ref[...])
        da_ref[...] = (ds * b_ref[...].astype(F32) * dsil_ref[...].astype(F32)).astype(BF16)
        db_ref[...] = (ds * sil_ref[...].astype(F32)).astype(BF16)

    act = pl.BlockSpec((tm, tf), lambda i, j: (i, j))
    return _pallas(
        body, name, (t // tm, nj),
        [pl.BlockSpec((tm, d), lambda i, j: (i, 0)), act, act, act, pl.BlockSpec((d, tf), lambda i, j: (0, j))],
        [act, act, pl.BlockSpec((tm, d), lambda i, j: (i, 0))],
        [jax.ShapeDtypeStruct((t, fp), BF16), jax.ShapeDtypeStruct((t, fp), BF16), jax.ShapeDtypeStruct((t, d), BF16)],
        [], (dxo, b, sil, dsil, wd_t), ride)


def _ffn_bwd_in(da, db, dxo, x, n, cols_t, gq, name, ride=None):
    t, d = x.shape
    fp = cols_t.shape[0]
    tm = min(t, FFN_TM)
    tf = 2 * fp // N_DEV
    nj = fp // tf

    def body(da_ref, db_ref, dxo_ref, x_ref, n_ref, wg_ref, wu_ref, dx_ref, dn_ref, acc_ref):
        i, j = pl.program_id(0), pl.program_id(1)

        @pl.when((i == 0) & (j == 0))
        def _():
            dn_ref[...] = jnp.zeros_like(dn_ref)

        @pl.when(j == 0)
        def _():
            acc_ref[...] = jnp.zeros_like(acc_ref)

        acc_ref[...] += _dot(da_ref[...], wg_ref[...]) + _dot(db_ref[...], wu_ref[...])

        @pl.when(j == nj - 1)
        def _():
            xv = x_ref[...]
            r = lax.rsqrt(jnp.mean(xv * xv, axis=-1, keepdims=True) + RMS_EPS)
            xh = xv * r
            dh = acc_ref[...]
            dn_ref[...] += jnp.sum(dh * xh, axis=0, keepdims=True)
            dhn = dh * n_ref[...]
            dx_ref[...] = dxo_ref[...] + r * (dhn - xh * jnp.mean(dhn * xh, axis=-1, keepdims=True))

    act = pl.BlockSpec((tm, tf), lambda i, j: (i, j))
    row = pl.BlockSpec((tm, d), lambda i, j: (i, 0))
    return _pallas(
        body, name, (t // tm, nj),
        [act, act, row, row, pl.BlockSpec((1, d), lambda i, j: (0, 0)),
         pl.BlockSpec((tf, d), lambda i, j: (j, gq)), pl.BlockSpec((tf, d), lambda i, j: (j, gq + 1))],
        [row, pl.BlockSpec((1, d), lambda i, j: (0, 0))],
        [jax.ShapeDtypeStruct((t, d), F32), jax.ShapeDtypeStruct((1, d), F32)],
        [pltpu.VMEM((tm, d), F32)], (da, db, dxo, x, n, cols_t, cols_t), ride)


def _wgrad(a, b, scale, tk, tn, name, ride=None):
    t, k = a.shape
    n = b.shape[1]
    tt = min(t, WGRAD_TT)
    nt = t // tt

    def body(a_ref, b_ref, o_ref, acc_ref):
        s = pl.program_id(2)

        @pl.when(s == 0)
        def _():
            acc_ref[...] = jnp.zeros_like(acc_ref)

        acc_ref[...] += _dot_tn(a_ref[...], b_ref[...])

        @pl.when(s == nt - 1)
        def _():
            o_ref[...] = (scale * acc_ref[...]).astype(BF16)

    res = _pallas(
        body, name, (k // tk, n // tn, nt),
        [pl.BlockSpec((tt, tk), lambda p, q, s: (s, p)), pl.BlockSpec((tt, tn), lambda p, q, s: (s, q))],
        [pl.BlockSpec((tk, tn), lambda p, q, s: (p, q))], [jax.ShapeDtypeStruct((k, n), BF16)],
        [pltpu.VMEM((tk, tn), F32)], (a, b), ride)
    return res[0] if ride is None else (res[0][0], res[1])


def _mix_in(x, n, w_in, cos_t, sin_t, seq, name):
    t, d = x.shape
    per_seq = seq // TM

    def body(x_ref, n_ref, w_ref, c_ref, s_ref, h_ref, q_ref, k_ref, v_ref, g_ref, u_ref):
        xv = x_ref[...]
        r = lax.rsqrt(jnp.mean(xv * xv, axis=-1, keepdims=True) + RMS_EPS)
        h = (xv * r * n_ref[...]).astype(BF16)
        h_ref[...] = h
        p = _dot(h, w_ref[...])
        c, s = c_ref[...], s_ref[...]
        q = p[:, :QK_W]
        k = p[:, QK_W:2 * QK_W]
        q_ref[...] = ((q * c + _swap_halves(q) * s) * (DK ** -0.5)).astype(BF16)
        k_ref[...] = (k * c + _swap_halves(k) * s).astype(BF16)
        v_ref[...] = p[:, 2 * QK_W:2 * QK_W + V_W].astype(BF16)
        g_ref[...] = p[:, 2 * QK_W + V_W:2 * QK_W + 2 * V_W]
        u_ref[...] = p[:, 2 * QK_W + 2 * V_W:]

    tile = lambda w: pl.BlockSpec((TM, w), lambda i: (i, 0))
    return pl.pallas_call(
        body, name=name, grid=(t // TM,),
        in_specs=[tile(d), pl.BlockSpec((1, d), lambda i: (0, 0)), pl.BlockSpec(w_in.shape, lambda i: (0, 0)),
                  pl.BlockSpec((TM, QK_W), lambda i: (i % per_seq, 0)), pl.BlockSpec((TM, QK_W), lambda i: (i % per_seq, 0))],
        out_specs=[tile(d), tile(QK_W), tile(QK_W), tile(V_W), tile(V_W), tile(POOL_W)],
        out_shape=[jax.ShapeDtypeStruct((t, d), BF16), jax.ShapeDtypeStruct((t, QK_W), BF16),
                   jax.ShapeDtypeStruct((t, QK_W), BF16), jax.ShapeDtypeStruct((t, V_W), BF16),
                   jax.ShapeDtypeStruct((t, V_W), F32), jax.ShapeDtypeStruct((t, POOL_W), F32)],
        compiler_params=_cparams(1),
    )(x, n, w_in, cos_t, sin_t)


def _mix_in_bwd(dp, dx2, x1, n, w_in_t, name, ride=None):
    t, d = x1.shape

    def body(dp_ref, dx2_ref, x_ref, n_ref, w_ref, dx_ref, dn_ref):
        @pl.when(pl.program_id(0) == 0)
        def _():
            dn_ref[...] = jnp.zeros_like(dn_ref)

        dh = _dot(dp_ref[...], w_ref[...])
        xv = x_ref[...]
        r = lax.rsqrt(jnp.mean(xv * xv, axis=-1, keepdims=True) + RMS_EPS)
        xh = xv * r
        dn_ref[...] += jnp.sum(dh * xh, axis=0, keepdims=True)
        dhn = dh * n_ref[...]
        dx_ref[...] = dx2_ref[...] + r * (dhn - xh * jnp.mean(dhn * xh, axis=-1, keepdims=True))

    tile = lambda w: pl.BlockSpec((TM, w), lambda i: (i, 0))
    return _pallas(
        body, name, (t // TM,),
        [tile(dp.shape[1]), tile(d), tile(d), pl.BlockSpec((1, d), lambda i: (0, 0)),
         pl.BlockSpec(w_in_t.shape, lambda i: (0, 0))],
        [tile(d), pl.BlockSpec((1, d), lambda i: (0, 0))],
        [jax.ShapeDtypeStruct((t, d), F32), jax.ShapeDtypeStruct((1, d), F32)],
        [], (dp, dx2, x1, n, w_in_t), ride)


def _group_norm(o):
    parts, rstds = [], []
    for h in range(HEADS):
        oh = o[:, h * DV:(h + 1) * DV]
        dlt = oh - jnp.mean(oh, axis=-1, keepdims=True)
        rstd = lax.rsqrt(jnp.mean(dlt * dlt, axis=-1, keepdims=True) + GN_EPS)
        parts.append(dlt * rstd)
        rstds.append(rstd)
    return jnp.concatenate(parts, axis=1), rstds


def _mix_core_fwd(qs, k, v, g, u, x1, consts, gain, wp, scale, w_out, nseq, seq, name, ride=None):
    t, d = x1.shape
    nblk = seq // BLK
    mask, dq, dk, gbd, bd = consts

    def body(q_ref, k_ref, v_ref, g_ref, u_ref, x1_ref, m_ref, dq_ref, dk_ref, gbd_ref, bd_ref, gain_ref, wp_ref,
             sc_ref, wo_ref, x2_ref, mix_ref, o_ref, pooled_ref, st_ref, state, halo):
        j = pl.program_id(1)

        @pl.when(j == 0)
        def _():
            state[...] = jnp.zeros_like(state)
            halo[...] = jnp.zeros_like(halo)

        qv, kv, vv = q_ref[...], k_ref[...], v_ref[...]
        st = state[...]
        st_ref[0] = st
        cross = _dot(qv, st.astype(BF16)) * dq_ref[...]
        outs = []
        for h in range(HEADS):
            qh = jnp.where(_head_mask(h), qv, jnp.zeros_like(qv))
            am = (_dot_nt(qh, kv) * m_ref[h]).astype(BF16)
            outs.append(_dot(am, vv[:, h * DV:(h + 1) * DV]))
        o = jnp.concatenate(outs, axis=1) + cross
        o_ref[...] = o
        kd = (kv.astype(F32) * dk_ref[...]).astype(BF16)
        state[...] = gbd_ref[...] * st + _dot_tn(kd, vv) * bd_ref[...]

        gv = g_ref[...]
        nrm, _ = _group_norm(o)
        ret = (gv * _sigmoid(gv)) * (nrm * gain_ref[...])

        uv = u_ref[...]
        c = jnp.concatenate([halo[...], uv], axis=0)
        halo[...] = uv[BLK - HALO:, :]
        pos = j * BLK + lax.broadcasted_iota(jnp.int32, (BLK, 1), 0)
        parts = []
        for gi, w in enumerate(WINDOWS):
            c = c + pltpu.roll(c, w // 2, 0)
            cnt = jnp.minimum(pos + 1, w).astype(F32)
            parts.append(c[HALO:, :GC] / cnt)
            if gi + 1 < len(WINDOWS):
                c = c[:, GC:]
        pooled = (jnp.concatenate(parts, axis=1) - uv).astype(BF16)
        pooled_ref[...] = pooled
        z = jnp.concatenate([_dot(pooled[:, gi * GC:(gi + 1) * GC], wp_ref[gi]) for gi in range(len(WINDOWS))], axis=1)
        mix = jnp.concatenate([ret, z * sc_ref[...]], axis=1).astype(BF16)
        mix_ref[...] = mix
        x2_ref[...] = x1_ref[...] + _dot(mix, wo_ref[...])

    blk = lambda w: pl.BlockSpec((BLK, w), lambda i, j: (i * nblk + j, 0))
    full = lambda a: pl.BlockSpec(a.shape, lambda i, j: (0,) * a.ndim)
    return _pallas(
        body, name, (nseq, nblk),
        [blk(QK_W), blk(QK_W), blk(V_W), blk(V_W), blk(POOL_W), blk(d),
         full(mask), full(dq), full(dk), full(gbd), full(bd), full(gain), full(wp), full(scale), full(w_out)],
        [blk(d), blk(d), blk(V_W), blk(POOL_W), pl.BlockSpec((1, QK_W, V_W), lambda i, j: (i * nblk + j, 0, 0))],
        [jax.ShapeDtypeStruct((t, d), F32), jax.ShapeDtypeStruct((t, d), BF16),
         jax.ShapeDtypeStruct((t, V_W), F32), jax.ShapeDtypeStruct((t, POOL_W), BF16),
         jax.ShapeDtypeStruct((nseq * nblk, QK_W, V_W), F32)],
        [pltpu.VMEM((QK_W, V_W), F32), pltpu.VMEM((HALO, POOL_W), F32)],
        (qs, k, v, g, u, x1, mask, dq, dk, gbd, bd, gain, wp, scale, w_out), ride)


def _mix_core_bwd(dx2, qs, k, v, g, o, pooled, st, consts, gain, wp, scale, w_out, cos_t, sin_t, nseq, seq, name,
                  ride=None):
    t, d = dx2.shape
    nblk = seq // BLK
    mask, dq, dk, gbd, bd = consts
    n_win = len(WINDOWS)

    def body(dx2_ref, q_ref, k_ref, v_ref, g_ref, o_ref, pooled_ref, st_ref, m_ref, dq_ref, dk_ref, gbd_ref, bd_ref,
             gain_ref, wp_ref, sc_ref, wo_ref, c_ref, s_ref,
             dp_ref, dx2b_ref, dgain_ref, dscale_ref, dwp_ref, rstate, carry):
        i, j = pl.program_id(0), pl.program_id(1)

        @pl.when((i == 0) & (j == 0))
        def _():
            dgain_ref[...] = jnp.zeros_like(dgain_ref)
            dscale_ref[...] = jnp.zeros_like(dscale_ref)
            dwp_ref[...] = jnp.zeros_like(dwp_ref)

        @pl.when(j == 0)
        def _():
            rstate[...] = jnp.zeros_like(rstate)
            carry[...] = jnp.zeros_like(carry)

        dx2b = dx2_ref[...].astype(BF16)
        dx2b_ref[...] = dx2b
        dmix = _dot(dx2b, wo_ref[...])
        dret, dpool = dmix[:, :V_W], dmix[:, V_W:]

        gv, ov, gain_v = g_ref[...], o_ref[...], gain_ref[...]
        sg = _sigmoid(gv)
        sil = gv * sg
        nrm, rstds = _group_norm(ov)
        dg = dret * (nrm * gain_v) * (sg * (1.0 + gv * (1.0 - sg)))
        dgn = dret * sil
        dgain_ref[...] += jnp.sum(dgn * nrm, axis=0, keepdims=True)
        dnrm = dgn * gain_v
        do_parts = []
        for h in range(HEADS):
            dn_h = dnrm[:, h * DV:(h + 1) * DV]
            n_h = nrm[:, h * DV:(h + 1) * DV]
            do_parts.append(rstds[h] * (dn_h - jnp.mean(dn_h, axis=-1, keepdims=True)
                                        - n_h * jnp.mean(dn_h * n_h, axis=-1, keepdims=True)))
        do = jnp.concatenate(do_parts, axis=1)
        dob = do.astype(BF16)

        qv, kv, vv = q_ref[...], k_ref[...], v_ref[...]
        stb = st_ref[0].astype(BF16)
        rs = rstate[...]
        rsb = rs.astype(BF16)
        dod = (do * dq_ref[...]).astype(BF16)
        dqs = _dot_nt(dod, stb)
        dst = _dot_tn(qv, dod) * bd_ref[...]
        dkf = dk_ref[...]
        kd = (kv.astype(F32) * dkf).astype(BF16)
        dks = _dot_nt(vv, rsb) * dkf
        dvs = _dot(kd, rsb)
        dv_parts = []
        for h in range(HEADS):
            hm = _head_mask(h)
            qh = jnp.where(hm, qv, jnp.zeros_like(qv))
            mh = m_ref[h]
            am = (_dot_nt(qh, kv) * mh).astype(BF16)
            dpm = (_dot_nt(dob[:, h * DV:(h + 1) * DV], vv[:, h * DV:(h + 1) * DV]) * mh).astype(BF16)
            dqs = dqs + jnp.where(hm, _dot(dpm, kv), 0.0)
            dks = dks + jnp.where(hm, _dot_tn(dpm, qv), 0.0)
            dv_parts.append(_dot_tn(am, dob[:, h * DV:(h + 1) * DV]))
        dvs = dvs + jnp.concatenate(dv_parts, axis=1)
        rstate[...] = dst + gbd_ref[...] * rs

        cv, sv = c_ref[...], s_ref[...]
        dqr = dqs * (DK ** -0.5)
        dq_pre = dqr * cv + _swap_halves(dqr * sv)
        dk_pre = dks * cv + _swap_halves(dks * sv)

        pv = pooled_ref[...]
        sc = sc_ref[...]
        dzb = (dpool * sc).astype(BF16)
        z_parts, dpo_parts = [], []
        for gi in range(n_win):
            p_g = pv[:, gi * GC:(gi + 1) * GC]
            dz_g = dzb[:, gi * GC:(gi + 1) * GC]
            z_parts.append(_dot(p_g, wp_ref[gi]))
            dwp_ref[gi] += _dot_tn(p_g, dz_g)
            dpo_parts.append(_dot_nt(dz_g, wp_ref[gi]))
        dscale_ref[...] += jnp.sum(dpool * jnp.concatenate(z_parts, axis=1), axis=0, keepdims=True)
        dpo = jnp.concatenate(dpo_parts, axis=1)
        pos = (nblk - 1 - j) * BLK + lax.broadcasted_iota(jnp.int32, (BLK, 1), 0)
        e = jnp.concatenate(
            [dpo[:, gi * GC:(gi + 1) * GC] / jnp.minimum(pos + 1, w).astype(F32) for gi, w in enumerate(WINDOWS)], axis=1)
        c = jnp.concatenate([e, carry[...]], axis=0)
        carry[...] = e[:HALO, :]
        rows = BLK + HALO
        lead = []
        for gi, w in enumerate(WINDOWS):
            c = c + pltpu.roll(c, rows - w // 2, 0)
            lead.append(c[:BLK, :GC])
            if gi + 1 < n_win:
                c = c[:, GC:]
        du = jnp.concatenate(lead, axis=1) - dpo

        dp_ref[:, 0:QK_W] = dq_pre.astype(BF16)
        dp_ref[:, QK_W:2 * QK_W] = dk_pre.astype(BF16)
        dp_ref[:, 2 * QK_W:2 * QK_W + V_W] = dvs.astype(BF16)
        dp_ref[:, 2 * QK_W + V_W:2 * QK_W + 2 * V_W] = dg.astype(BF16)
        dp_ref[:, 2 * QK_W + 2 * V_W:] = du.astype(BF16)

    rev = lambda i, j: i * nblk + (nblk - 1 - j)
    blk = lambda w: pl.BlockSpec((BLK, w), lambda i, j: (rev(i, j), 0))
    full = lambda a: pl.BlockSpec(a.shape, lambda i, j: (0,) * a.ndim)
    in_w = 2 * QK_W + 2 * V_W + POOL_W
    return _pallas(
        body, name, (nseq, nblk),
        [blk(d), blk(QK_W), blk(QK_W), blk(V_W), blk(V_W), blk(V_W), blk(POOL_W),
         pl.BlockSpec((1, QK_W, V_W), lambda i, j: (rev(i, j), 0, 0)),
         full(mask), full(dq), full(dk), full(gbd), full(bd), full(gain), full(wp), full(scale), full(w_out),
         pl.BlockSpec((BLK, QK_W), lambda i, j: (nblk - 1 - j, 0)),
         pl.BlockSpec((BLK, QK_W), lambda i, j: (nblk - 1 - j, 0))],
        [blk(in_w), blk(d), pl.BlockSpec((1, V_W), lambda i, j: (0, 0)),
         pl.BlockSpec((1, POOL_W), lambda i, j: (0, 0)), pl.BlockSpec((n_win, GC, GC), lambda i, j: (0, 0, 0))],
        [jax.ShapeDtypeStruct((t, in_w), BF16), jax.ShapeDtypeStruct((t, d), BF16),
         jax.ShapeDtypeStruct((1, V_W), F32), jax.ShapeDtypeStruct((1, POOL_W), F32),
         jax.ShapeDtypeStruct((n_win, GC, GC), F32)],
        [pltpu.VMEM((QK_W, V_W), F32), pltpu.VMEM((HALO, POOL_W), F32)],
        (dx2, qs, k, v, g, o, pooled, st, mask, dq, dk, gbd, bd, gain, wp, scale, w_out, cos_t, sin_t), ride)


def _loss_head(x3, nf, tgt, name):
    t, d = x3.shape

    def body(x_ref, n_ref, t_ref, dx_ref, dn_ref, loss_ref):
        @pl.when(pl.program_id(0) == 0)
        def _():
            dn_ref[...] = jnp.zeros_like(dn_ref)
            loss_ref[...] = jnp.zeros_like(loss_ref)

        xv = x_ref[...]
        nv = n_ref[...]
        r = lax.rsqrt(jnp.mean(xv * xv, axis=-1, keepdims=True) + RMS_EPS)
        xh = xv * r
        err = xh * nv - t_ref[...]
        row = jnp.mean(err * err, axis=-1, keepdims=True)
        loss_ref[...] += 0.5 * jnp.sum(row, axis=0, keepdims=True)
        dy = err * (1.0 / d)
        dn_ref[...] += jnp.sum(dy * xh, axis=0, keepdims=True)
        dxh = dy * nv
        dx_ref[...] = r * (dxh - xh * jnp.mean(dxh * xh, axis=-1, keepdims=True))

    tile = pl.BlockSpec((TM, d), lambda i: (i, 0))
    return pl.pallas_call(
        body, name=name, grid=(t // TM,),
        in_specs=[tile, pl.BlockSpec((1, d), lambda i: (0, 0)), tile],
        out_specs=[tile, pl.BlockSpec((1, d), lambda i: (0, 0)), pl.BlockSpec((1, 1), lambda i: (0, 0))],
        out_shape=[jax.ShapeDtypeStruct((t, d), F32), jax.ShapeDtypeStruct((1, d), F32), jax.ShapeDtypeStruct((1, 1), F32)],
        compiler_params=_cparams(1),
    )(x3, nf, tgt)


def _coords():
    return lax.axis_index("x"), lax.axis_index("y"), lax.axis_index("c")


def _window(ref, kind, idx, size):
    if kind == "col":
        return ref.at[:, pl.ds(pl.multiple_of(idx * size, LANE), size)]
    return ref.at[pl.ds(pl.multiple_of(idx * size, 8), size), :]


def _run_exchange(ex, name):
    n_in = len(ex.inputs)

    def body(*refs):
        ins, outs, sems = refs[:n_in], refs[n_in:n_in + len(ex.out_shape)], refs[n_in + len(ex.out_shape):]
        ex.start(ins, outs, sems)
        if ex.mid is not None:
            ex.mid(ins, outs, sems)
        ex.finish(ins, outs, sems)

    return pl.pallas_call(body, name=name, in_specs=[ANY] * n_in, out_specs=[ANY] * len(ex.out_shape),
                          out_shape=ex.out_shape, scratch_shapes=ex.scratch)(*ex.inputs)


def _join(exchanges):
    bounds = []
    i0 = o0 = s0 = 0
    for ex in exchanges:
        bounds.append((i0, o0, s0))
        i0, o0, s0 = i0 + len(ex.inputs), o0 + len(ex.out_shape), s0 + len(ex.scratch)

    def phase(which):
        def run(ins, outs, sems):
            for ex, (i, o, s) in zip(exchanges, bounds):
                fn = getattr(ex, which)
                if fn is not None:
                    fn(ins[i:i + len(ex.inputs)], outs[o:o + len(ex.out_shape)], sems[s:s + len(ex.scratch)])
        return run

    return _Exchange(sum((ex.inputs for ex in exchanges), []), sum((ex.out_shape for ex in exchanges), []),
                     sum((ex.scratch for ex in exchanges), []), phase("start"), phase("finish"),
                     phase("mid") if any(ex.mid is not None for ex in exchanges) else None)


def _gather_exchange(parts):
    n = len(parts)
    kinds = [kd for _, kd in parts]
    sizes = [a.shape[1] if kd == "col" else a.shape[0] for a, kd in parts]

    def plan(ins, outs, sems):
        send_sems, recv_sems, local_sems = sems
        x, y, c = _coords()
        me, sibling = (x, y, c), (x, y, 1 - c)
        chips = [(1 - x, y), (x, 1 - y), (1 - x, 1 - y)]

        def win(p, dev):
            return _window(outs[p], kinds[p], 4 * dev[0] + 2 * dev[1] + dev[2], sizes[p])

        def copy(p, k, block, to, src=None):
            return pltpu.make_async_remote_copy(
                src_ref=win(p, block) if src is None else src, dst_ref=win(p, block),
                send_sem=send_sems.at[p * 7 + k], recv_sem=recv_sems.at[p * 7 + k], device_id=to, device_id_type=MESH_ID)

        mine = [pltpu.make_async_copy(ins[p], win(p, me), local_sems.at[p]) for p in range(n)]
        first, arrived, passed, rest = [], [], [], []
        for p in range(n):
            first.append(copy(p, 0, me, sibling, src=ins[p]))
            first += [copy(p, 1 + q, me, (*chip, c), src=ins[p]) for q, chip in enumerate(chips)]
            rest.append(copy(p, 0, sibling, me))
            rest += [copy(p, 4 + q, (*chip, 1 - c), me) for q, chip in enumerate(chips)]
        for q, chip in enumerate(chips):
            for p in range(n):
                arrived.append(copy(p, 1 + q, (*chip, c), me))
                passed.append(copy(p, 4 + q, (*chip, c), sibling))
        return mine, first, arrived, passed, rest

    def start(ins, outs, sems):
        mine, first, _, _, _ = plan(ins, outs, sems)
        for cp in mine + first:
            cp.start()

    def mid(ins, outs, sems):
        _, _, arrived, passed, _ = plan(ins, outs, sems)
        for got, fwd in zip(arrived, passed):
            got.wait_recv()
            fwd.start()

    def finish(ins, outs, sems):
        mine, first, _, passed, rest = plan(ins, outs, sems)
        for cp in rest:
            cp.wait_recv()
        for cp in first + passed:
            cp.wait_send()
        for cp in mine:
            cp.wait()

    out_shape = [jax.ShapeDtypeStruct((a.shape[0], N_DEV * a.shape[1]) if kd == "col" else (N_DEV * a.shape[0], a.shape[1]),
                                      a.dtype) for a, kd in parts]
    scratch = [pltpu.SemaphoreType.DMA((7 * n,)), pltpu.SemaphoreType.DMA((7 * n,)), pltpu.SemaphoreType.DMA((n,))]
    return _Exchange([a for a, _ in parts], out_shape, scratch, start, finish, mid)


def _all_gather(parts, name):
    return _run_exchange(_gather_exchange(parts), name)


def _shard_shape(a, kd):
    return (a.shape[0], a.shape[1] // N_DEV) if kd == "col" else (a.shape[0] // N_DEV, a.shape[1])


def _symmetric_exchange(inputs, out_shape, n_copies, plan):
    def start(ins, outs, sems):
        for cp in plan(ins, outs, sems):
            cp.start()

    def finish(ins, outs, sems):
        copies = plan(ins, outs, sems)
        for cp in copies:
            cp.wait_recv()
        for cp in copies:
            cp.wait_send()

    scratch = [pltpu.SemaphoreType.DMA((n_copies,)), pltpu.SemaphoreType.DMA((n_copies,))]
    return _Exchange(inputs, out_shape, scratch, start, finish)


def _rs_pair_exchange(grads):
    n = len(grads)
    kinds = [kd for _, kd in grads]
    shapes = [_shard_shape(a, kd) for a, kd in grads]

    def plan(ins, outs, sems):
        send_sems, recv_sems = sems
        x, y, c = _coords()
        copies = []
        for p in range(n):
            size = shapes[p][1] if kinds[p] == "col" else shapes[p][0]
            for s in range(4):
                src = _window(ins[p], kinds[p], 2 * s + (1 - c), size)
                copies.append(pltpu.make_async_remote_copy(
                    src_ref=src, dst_ref=outs[p].at[s], send_sem=send_sems.at[4 * p + s], recv_sem=recv_sems.at[4 * p + s],
                    device_id=(x, y, 1 - c), device_id_type=MESH_ID))
        return copies

    return _symmetric_exchange([a for a, _ in grads], [jax.ShapeDtypeStruct((4,) + shapes[p], BF16) for p in range(n)],
                               4 * n, plan)


def _rs_chips_exchange(sums):
    n = len(sums)

    def plan(ins, outs, sems):
        send_sems, recv_sems = sems
        x, y, c = _coords()
        chips = [(1 - x, y), (x, 1 - y), (1 - x, 1 - y)]
        copies = []
        for p in range(n):
            for q, (cx, cy) in enumerate(chips):
                copies.append(pltpu.make_async_remote_copy(
                    src_ref=ins[p].at[2 * cx + cy], dst_ref=outs[p].at[q],
                    send_sem=send_sems.at[3 * p + q], recv_sem=recv_sems.at[3 * p + q],
                    device_id=(cx, cy, c), device_id_type=MESH_ID))
        return copies

    return _symmetric_exchange(list(sums), [jax.ShapeDtypeStruct((3,) + a.shape[1:], BF16) for a in sums], 3 * n, plan)


def _rs_pair(grads, name):
    return _run_exchange(_rs_pair_exchange(grads), name)


def _rs_chips(sums, name):
    return _run_exchange(_rs_chips_exchange(sums), name)


def _pair_sum(grad, kd, recv, core, name):
    _, r, cw = recv.shape
    tr = min(r, TM)

    def body(core_ref, g_ref, r_ref, o_ref):
        del core_ref
        o_ref[0] = (g_ref[...].astype(F32) + r_ref[0].astype(F32)).astype(BF16)

    if kd == "col":
        g_spec = pl.BlockSpec((tr, cw), lambda s, i, core_ref: (i, 2 * s + core_ref[0]))
    else:
        g_spec = pl.BlockSpec((tr, cw), lambda s, i, core_ref: ((2 * s + core_ref[0]) * (r // tr) + i, 0))
    grid_spec = pltpu.PrefetchScalarGridSpec(
        num_scalar_prefetch=1, grid=(4, r // tr),
        in_specs=[g_spec, pl.BlockSpec((1, tr, cw), lambda s, i, core_ref: (s, i, 0))],
        out_specs=pl.BlockSpec((1, tr, cw), lambda s, i, core_ref: (s, i, 0)))
    return pl.pallas_call(
        body, name=name, grid_spec=grid_spec, out_shape=jax.ShapeDtypeStruct(recv.shape, BF16),
        compiler_params=_cparams(2),
    )(core, grad, recv)


def _adam_math(w, g, m, v):
    m2 = B1 * m + (1.0 - B1) * g
    v2 = B2 * v + (1.0 - B2) * (g * g)
    m_hat = m2 / (1.0 - B1 ** STEP)
    v_hat = v2 / (1.0 - B2 ** STEP)
    delta = -LR * (m_hat / (jnp.sqrt(v_hat) + ADAM_EPS) + WD * w)
    return delta, m2, v2


def _chip_sum_adam(psum, recv, chip, w, m, v, name):
    r, cw = w.shape
    pc = psum.shape[2]
    tr = min(r, TM)

    def body(chip_ref, p_ref, r_ref, w_ref, m_ref, v_ref, g_ref, d_ref, m2_ref, v2_ref):
        del chip_ref
        g = p_ref[0].astype(F32) + r_ref[0].astype(F32) + r_ref[1].astype(F32) + r_ref[2].astype(F32)
        g = g[:, :cw]
        delta, m2, v2 = _adam_math(w_ref[...], g, m_ref[...], v_ref[...])
        g_ref[...] = g
        d_ref[...] = delta
        m2_ref[...] = m2
        v2_ref[...] = v2

    loc = pl.BlockSpec((tr, cw), lambda i, chip_ref: (i, 0))
    grid_spec = pltpu.PrefetchScalarGridSpec(
        num_scalar_prefetch=1, grid=(r // tr,),
        in_specs=[pl.BlockSpec((1, tr, pc), lambda i, chip_ref: (chip_ref[0], i, 0)),
                  pl.BlockSpec((3, tr, pc), lambda i, chip_ref: (0, i, 0)), loc, loc, loc],
        out_specs=[loc, loc, loc, loc])
    return pl.pallas_call(
        body, name=name, grid_spec=grid_spec, out_shape=[jax.ShapeDtypeStruct((r, cw), F32)] * 4,
        compiler_params=_cparams(1),
    )(chip, psum, recv, w, m, v)


def _small_allreduce_adam(partials, params, moms, vels, name, ride=None):
    n = len(partials)
    row0 = []
    rows = 0
    for a in partials:
        if a.shape[0] >= 8:
            rows = _pad_to(rows, 8)
        row0.append(rows)
        rows += a.shape[0]
    rows = _pad_to(rows, 8)
    width = max(a.shape[1] for a in partials)
    r_in = 0 if ride is None else len(ride.inputs)
    r_out = 0 if ride is None else len(ride.out_shape)

    def body(*refs):
        g_in = refs[:n]
        w_in, m_in, v_in = refs[n:2 * n], refs[2 * n:3 * n], refs[3 * n:4 * n]
        refs = refs[4 * n:]
        r_ins, refs = refs[:r_in], refs[r_in:]
        outs, refs = refs[:4 * n], refs[4 * n:]
        r_outs, refs = refs[:r_out], refs[r_out:]
        pair, chips, send_sems, recv_sems = refs[:4]
        if ride is not None:
            ride.start(r_ins, r_outs, refs[4:])
        x, y, c = _coords()
        chip = 2 * x + y
        pair[c] = jnp.zeros((rows, width), F32)
        for p in range(n):
            r, cw = partials[p].shape
            pair[c, row0[p]:row0[p] + r, 0:cw] = g_in[p][...]
        swap = pltpu.make_async_remote_copy(src_ref=pair.at[c], dst_ref=pair.at[c], send_sem=send_sems.at[0],
                                            recv_sem=recv_sems.at[0], device_id=(x, y, 1 - c), device_id_type=MESH_ID)
        swap.start()
        swap.wait_recv()
        swap.wait_send()
        chips[chip] = pair[0] + pair[1]
        copies = [pltpu.make_async_remote_copy(
            src_ref=chips.at[chip], dst_ref=chips.at[chip], send_sem=send_sems.at[1 + q], recv_sem=recv_sems.at[1 + q],
            device_id=(cx, cy, c), device_id_type=MESH_ID) for q, (cx, cy) in enumerate([(1 - x, y), (x, 1 - y), (1 - x, 1 - y)])]
        for cp in copies:
            cp.start()
        for cp in copies:
            cp.wait_recv()
        for cp in copies:
            cp.wait_send()
        for p in range(n):
            r, cw = partials[p].shape
            g = chips[0, row0[p]:row0[p] + r, 0:cw]
            for q in range(1, 4):
                g = g + chips[q, row0[p]:row0[p] + r, 0:cw]
            delta, m2, v2 = _adam_math(w_in[p][...], g, m_in[p][...], v_in[p][...])
            outs[4 * p][...] = g
            outs[4 * p + 1][...] = delta
            outs[4 * p + 2][...] = m2
            outs[4 * p + 3][...] = v2
        if ride is not None:
            ride.finish(r_ins, r_outs, refs[4:])

    out_shape = []
    for a in partials:
        out_shape += [jax.ShapeDtypeStruct(a.shape, F32)] * 4
    res = pl.pallas_call(
        body, name=name, in_specs=[VMEM_SPEC] * (4 * n) + [ANY] * r_in, out_specs=[VMEM_SPEC] * (4 * n) + [ANY] * r_out,
        out_shape=out_shape + ([] if ride is None else ride.out_shape),
        scratch_shapes=[pltpu.VMEM((2, rows, width), F32), pltpu.VMEM((4, rows, width), F32),
                        pltpu.SemaphoreType.DMA((4,)), pltpu.SemaphoreType.DMA((4,))] + ([] if ride is None else ride.scratch),
    )(*partials, *params, *moms, *vels, *([] if ride is None else ride.inputs))
    return res if ride is None else (res[:4 * n], res[4 * n:])


def _local_step(xf, tgt, nseq, seq, cols1_all, d1_all, later, small_w, core=None, trans=None, small_step=None):
    d = xf.shape[1]
    dist = core is not None
    n1, n2, gain, pool_w, pool_scale, n3, nf = small_w
    tf = 2 * cols1_all.shape[1] // N_DEV
    consts = _retention_constants()
    cos_t, sin_t = _rotary_tables(seq)
    wp_b = pool_w.astype(BF16)

    def pair_sums(grads, recv, names):
        return [_pair_sum(g, kd, r, core, "pair_sum_" + nm) for (g, kd), r, nm in zip(grads, recv, names)]

    if dist:
        (x1, h1, b1, sil1, dsil1, s1), later = _ffn_fwd(xf, n1, cols1_all, 0, d1_all, "ffn1_fwd",
                                                       ride=_gather_exchange(later))
    else:
        x1, h1, b1, sil1, dsil1, s1 = _ffn_fwd(xf, n1, cols1_all, 0, d1_all, "ffn1_fwd")
    cols2_all, d2_all, win_all, wout_all = later
    h2, qs, kr, vv, gg, uu = _mix_in(x1, n2, win_all, cos_t, sin_t, seq, "mix_in")
    fwd_mix = (qs, kr, vv, gg, uu, x1, consts, gain, wp_b, pool_scale, wout_all, nseq, seq, "mix_core_fwd")
    if dist:
        (x2, mix, oo, pooled, states), (win_t, wout_t) = _mix_core_fwd(*fwd_mix, ride=_gather_exchange(trans["mix"]))
        (x3, h3, b3, sil3, dsil3, s3), (cols2_t, d2_t) = _ffn_fwd(x2, n3, cols2_all, 0, d2_all, "ffn2_fwd",
                                                                 ride=_gather_exchange(trans["ffn2"]))
    else:
        x2, mix, oo, pooled, states = _mix_core_fwd(*fwd_mix)
        x3, h3, b3, sil3, dsil3, s3 = _ffn_fwd(x2, n3, cols2_all, 0, d2_all, "ffn2_fwd")
        cols1_t = _transpose(cols1_all, "transpose_cols1")
        cols2_t = _transpose(cols2_all, "transpose_cols2")
        d1_t = _transpose(d1_all, "transpose_down1")
        d2_t = _transpose(d2_all, "transpose_down2")
        win_t = _transpose(win_all, "transpose_w_in")
        wout_t = _transpose(wout_all, "transpose_w_out")
    dx3, dnf, loss_part = _loss_head(x3, nf, tgt, "loss_head")
    out = {}

    if dist:
        (da3, db3, dx3b), (cols1_t, d1_t) = _ffn_bwd_act(dx3, b3, sil3, dsil3, d2_t, "ffn2_bwd_act",
                                                         ride=_gather_exchange(trans["ffn1"]))
    else:
        da3, db3, dx3b = _ffn_bwd_act(dx3, b3, sil3, dsil3, d2_t, "ffn2_bwd_act")
    names2 = ["ffn2_gate", "ffn2_up", "ffn2_down"]
    grads2 = [(_wgrad(h3, da3, 1.0, d, tf, "wgrad_gate2"), "col"), (_wgrad(h3, db3, 1.0, d, tf, "wgrad_up2"), "col"),
              (_wgrad(s3, dx3b, 1.0, tf, d, "wgrad_down2"), "row")]
    if dist:
        (dx2, dn3), recv2 = _ffn_bwd_in(da3, db3, dx3, x2, n3, cols2_t, 0, "ffn2_bwd_in", ride=_rs_pair_exchange(grads2))
        sums2 = pair_sums(grads2, recv2, names2)
        (dp, dx2b, dgain, dscale, dwp), crecv2 = _mix_core_bwd(
            dx2, qs, kr, vv, gg, oo, pooled, states, consts, gain, wp_b, pool_scale, wout_t, cos_t, sin_t, nseq, seq,
            "mix_core_bwd", ride=_rs_chips_exchange(sums2))
        out.update({nm: (s, r) for nm, s, r in zip(names2, sums2, crecv2)})
    else:
        dx2, dn3 = _ffn_bwd_in(da3, db3, dx3, x2, n3, cols2_t, 0, "ffn2_bwd_in")
        dp, dx2b, dgain, dscale, dwp = _mix_core_bwd(dx2, qs, kr, vv, gg, oo, pooled, states, consts, gain, wp_b,
                                                     pool_scale, wout_t, cos_t, sin_t, nseq, seq, "mix_core_bwd")
        out.update(dict(zip(names2, grads2)))

    names_m = ["w_in", "w_out"]
    grads_m = [(_wgrad(h2, dp, 1.0, d, d, "wgrad_in"), "col"), (_wgrad(mix, dx2b, 1.0, d, d, "wgrad_out"), "row")]
    if dist:
        (dx1, dn2), recv_m = _mix_in_bwd(dp, dx2, x1, n2, win_t, "mix_in_bwd", ride=_rs_pair_exchange(grads_m))
        sums_m = pair_sums(grads_m, recv_m, names_m)
        (da1, db1, dx1b), crecv_m = _ffn_bwd_act(dx1, b1, sil1, dsil1, d1_t, "ffn1_bwd_act",
                                                 ride=_rs_chips_exchange(sums_m))
        out.update({nm: (s, r) for nm, s, r in zip(names_m, sums_m, crecv_m)})
    else:
        dx1, dn2 = _mix_in_bwd(dp, dx2, x1, n2, win_t, "mix_in_bwd")
        da1, db1, dx1b = _ffn_bwd_act(dx1, b1, sil1, dsil1, d1_t, "ffn1_bwd_act")
        out.update(dict(zip(names_m, grads_m)))

    dx0, dn1 = _ffn_bwd_in(da1, db1, dx1, xf, n1, cols1_t, 0, "ffn1_bwd_in")
    small_parts = (dn1, dn2, dgain, dwp, dscale, dn3, dnf)
    g_down = (_wgrad(s1, dx1b, 1.0, tf, d, "wgrad_down1"), "row")
    if dist:
        g_gate, recv_d = _wgrad(h1, da1, 1.0, d, tf, "wgrad_gate1", ride=_rs_pair_exchange([g_down]))
        g_gate = (g_gate, "col")
        sum_d = pair_sums([g_down], recv_d, ["ffn1_down"])
        g_up, (crecv_d, recv_g) = _wgrad(h1, db1, 1.0, d, tf, "wgrad_up1",
                                         ride=_join([_rs_chips_exchange(sum_d), _rs_pair_exchange([g_gate])]))
        g_up = (g_up, "col")
        sum_g = pair_sums([g_gate], [recv_g], ["ffn1_gate"])
        small_out, (crecv_g, recv_u) = small_step(small_parts, _join([_rs_chips_exchange(sum_g), _rs_pair_exchange([g_up])]))
        sum_u = pair_sums([g_up], [recv_u], ["ffn1_up"])
        (crecv_u,) = _run_exchange(_rs_chips_exchange(sum_u), "rs_tail")
        out.update({"ffn1_gate": (sum_g[0], crecv_g), "ffn1_up": (sum_u[0], crecv_u), "ffn1_down": (sum_d[0], crecv_d)})
        return loss_part, dx0, out, small_out
    out.update({"ffn1_gate": (_wgrad(h1, da1, 1.0, d, tf, "wgrad_gate1"), "col"),
                "ffn1_up": (_wgrad(h1, db1, 1.0, d, tf, "wgrad_up1"), "col"), "ffn1_down": g_down})
    return loss_part, dx0, out, small_parts


def kernel(x, norm_ffn1, ffn1_gate, ffn1_up, ffn1_down, norm_mix, w_in, ret_gn_gain, pool_w, pool_scale, w_out, norm_ffn2, ffn2_gate, ffn2_up, ffn2_down, norm_final, loss_target, m_norm_ffn1, m_ffn1_gate, m_ffn1_up, m_ffn1_down, m_norm_mix, m_w_in, m_ret_gn_gain, m_pool_w, m_pool_scale, m_w_out, m_norm_ffn2, m_ffn2_gate, m_ffn2_up, m_ffn2_down, m_norm_final, v_norm_ffn1, v_ffn1_gate, v_ffn1_up, v_ffn1_down, v_norm_mix, v_w_in, v_ret_gn_gain, v_pool_w, v_pool_scale, v_w_out, v_norm_ffn2, v_ffn2_gate, v_ffn2_up, v_ffn2_down, v_norm_final):
    nseq, seq, d = x.shape
    t = nseq * seq
    f_loc = ffn1_gate.shape[2]
    f_pad = _pad_to(f_loc, LANE)
    xf = x.reshape(t, d)
    tgt = loss_target.reshape(t, d)
    core = lax.axis_index("c").astype(jnp.int32).reshape(1)
    chip = (2 * lax.axis_index("x") + lax.axis_index("y")).astype(jnp.int32).reshape(1)

    colp = lambda w: jnp.pad(w[0].astype(BF16), ((0, 0), (0, f_pad - f_loc)))
    rowp = lambda w: jnp.pad(w[0].astype(BF16), ((0, f_pad - f_loc), (0, 0)))
    cols1 = jnp.concatenate([colp(ffn1_gate), colp(ffn1_up)], axis=0)
    cols2 = jnp.concatenate([colp(ffn2_gate), colp(ffn2_up)], axis=0)
    cols1_all, d1_all = _all_gather([(cols1, "col"), (rowp(ffn1_down), "row")], "all_gather_ffn1")
    d1_loc, d2_loc, win_loc, wout_loc = rowp(ffn1_down), rowp(ffn2_down), w_in[0].astype(BF16), w_out[0].astype(BF16)
    later = [(cols2, "col"), (d2_loc, "row"), (win_loc, "col"), (wout_loc, "row")]
    trans = {"mix": [(win_loc.T, "row"), (wout_loc.T, "col")], "ffn2": [(cols2.T, "row"), (d2_loc.T, "col")],
             "ffn1": [(cols1.T, "row"), (d1_loc.T, "col")]}

    flat = lambda a: a.reshape(pool_w.size // d, d)
    params = [norm_ffn1, norm_mix, ret_gn_gain, flat(pool_w), pool_scale, norm_ffn2, norm_final.reshape(1, d)]
    moms = [m_norm_ffn1, m_norm_mix, m_ret_gn_gain, flat(m_pool_w), m_pool_scale, m_norm_ffn2, m_norm_final.reshape(1, d)]
    vels = [v_norm_ffn1, v_norm_mix, v_ret_gn_gain, flat(v_pool_w), v_pool_scale, v_norm_ffn2, v_norm_final.reshape(1, d)]

    def small_step(parts, ride):
        dn1, dn2, dgain, dwp, dscale, dn3, dnf = parts
        return _small_allreduce_adam([dn1, dn2, dgain, flat(dwp), dscale, dn3, dnf], params, moms, vels,
                                     "small_allreduce_adam", ride)

    small_w = (norm_ffn1, norm_mix, ret_gn_gain, pool_w[0], pool_scale, norm_ffn2, norm_final.reshape(1, d))
    loss_part, dx0, reduced, small_out = _local_step(xf, tgt, nseq, seq, cols1_all, d1_all, later, small_w, core, trans,
                                                     small_step)

    local = {"ffn1_gate": (ffn1_gate, m_ffn1_gate, v_ffn1_gate), "ffn1_up": (ffn1_up, m_ffn1_up, v_ffn1_up),
             "ffn1_down": (ffn1_down, m_ffn1_down, v_ffn1_down), "w_in": (w_in, m_w_in, v_w_in),
             "w_out": (w_out, m_w_out, v_w_out), "ffn2_gate": (ffn2_gate, m_ffn2_gate, v_ffn2_gate),
             "ffn2_up": (ffn2_up, m_ffn2_up, v_ffn2_up), "ffn2_down": (ffn2_down, m_ffn2_down, v_ffn2_down)}
    big = {}
    for nm, (w, m, v) in local.items():
        ps, rcv = reduced[nm]
        g, dlt, m2, v2 = _chip_sum_adam(ps, rcv, chip, w[0], m[0], v[0], "adam_" + nm)
        big[nm] = tuple(a[None] for a in (g, dlt, m2, v2))

    small_names = ["norm_ffn1", "norm_mix", "ret_gn_gain", "pool_w", "pool_scale", "norm_ffn2", "norm_final"]
    shapes = [norm_ffn1.shape, norm_mix.shape, ret_gn_gain.shape, pool_w.shape, pool_scale.shape, norm_ffn2.shape,
              norm_final.shape]
    small = {nm: tuple(small_out[4 * p + q].reshape(shapes[p]) for q in range(4)) for p, nm in enumerate(small_names)}

    loss = lax.psum(loss_part[0, 0], ("x", "y", "c"))
    order = ["norm_ffn1", "ffn1_gate", "ffn1_up", "ffn1_down", "norm_mix", "w_in", "ret_gn_gain", "pool_w", "pool_scale",
             "w_out", "norm_ffn2", "ffn2_gate", "ffn2_up", "ffn2_down", "norm_final"]
    both = {**big, **small}
    outs = [loss, dx0.reshape(nseq, seq, d)]
    for q in range(4):
        outs += [both[nm][q] for nm in order]
    return tuple(outs)
```

```python
import functools

import numpy as np
import jax
import jax.numpy as jnp
from jax import lax
from jax.experimental import pallas as pl
from jax.experimental.pallas import tpu as pltpu

F32, BF16 = jnp.float32, jnp.bfloat16
MESH_ID = pl.DeviceIdType.MESH
ANY = pl.BlockSpec(memory_space=pl.ANY)
VMEM_SPEC = pl.BlockSpec(memory_space=pltpu.VMEM)

N_DEV = 8
RMS_EPS = 1e-6
GN_EPS = 1e-5
HEADS, DK, DV = 4, 64, 128
QK_W, V_W, POOL_W = HEADS * DK, HEADS * DV, 512
WINDOWS = (2, 4, 8, 16)
GC = POOL_W // len(WINDOWS)
CHUNK = 64
BLK = 4 * CHUNK
HALO = 16
ROPE_BASE = 10000.0
LR, B1, B2, ADAM_EPS, WD, STEP = 0.001, 0.9, 0.999, 1e-08, 0.01, 10
LANE = 128
TM = 512
FFN_TM = 1024
FFN_FWD_TF = 512
WGRAD_TT = 2048
VMEM_LIMIT = 56 * 1024 * 1024


def _cparams(n_axes):
    return pltpu.CompilerParams(dimension_semantics=("arbitrary",) * n_axes, vmem_limit_bytes=VMEM_LIMIT)


class _Exchange:
    def __init__(self, inputs, out_shape, scratch, start, finish, mid=None):
        self.inputs, self.out_shape, self.scratch = list(inputs), list(out_shape), list(scratch)
        self.start, self.finish, self.mid = start, finish, mid


def _pallas(body, name, grid, in_specs, out_specs, out_shape, scratch_shapes, args, ride=None):
    n_axes = len(grid)
    if ride is None:
        return pl.pallas_call(body, name=name, grid=grid, in_specs=in_specs, out_specs=out_specs, out_shape=out_shape,
                              scratch_shapes=scratch_shapes, compiler_params=_cparams(n_axes))(*args)
    n_in, n_out, n_scr = len(in_specs), len(out_specs), len(scratch_shapes)
    r_in, r_out = len(ride.inputs), len(ride.out_shape)

    def hosted(*refs):
        ins, refs = refs[:n_in], refs[n_in:]
        r_ins, refs = refs[:r_in], refs[r_in:]
        outs, refs = refs[:n_out], refs[n_out:]
        r_outs, refs = refs[:r_out], refs[r_out:]
        scr, sems = refs[:n_scr], refs[n_scr:]
        ids = [pl.program_id(a) for a in range(n_axes)]
        first, last, inner0 = ids[0] == 0, ids[0] == grid[0] - 1, None
        for a in range(1, n_axes):
            first = first & (ids[a] == 0)
            last = last & (ids[a] == grid[a] - 1)
            inner0 = (ids[a] == 0) if inner0 is None else inner0 & (ids[a] == 0)

        @pl.when(first)
        def _():
            ride.start(r_ins, r_outs, sems)

        if ride.mid is not None:
            at_mid = ids[0] == grid[0] - 1
            if inner0 is not None:
                at_mid = at_mid & inner0

            @pl.when(at_mid)
            def _():
                ride.mid(r_ins, r_outs, sems)

        body(*ins, *outs, *scr)

        @pl.when(last)
        def _():
            ride.finish(r_ins, r_outs, sems)

    res = pl.pallas_call(
        hosted, name=name, grid=grid, in_specs=list(in_specs) + [ANY] * r_in, out_specs=list(out_specs) + [ANY] * r_out,
        out_shape=list(out_shape) + ride.out_shape, scratch_shapes=list(scratch_shapes) + ride.scratch,
        compiler_params=_cparams(n_axes))(*args, *ride.inputs)
    return res[:n_out], res[n_out:]


def _dot(a, b):
    return jnp.dot(a, b, preferred_element_type=F32)


def _dot_nt(a, b):
    return lax.dot_general(a, b, (((1,), (1,)), ((), ())), preferred_element_type=F32)


def _dot_tn(a, b):
    return lax.dot_general(a, b, (((0,), (0,)), ((), ())), preferred_element_type=F32)


def _sigmoid(x):
    return 0.5 * jnp.tanh(0.5 * x) + 0.5


def _transpose(w, name):
    r, c = w.shape
    tb = 512

    def body(x_ref, o_ref):
        o_ref[...] = x_ref[...].T

    return pl.pallas_call(
        body, name=name, grid=(r // tb, c // tb),
        in_specs=[pl.BlockSpec((tb, tb), lambda i, j: (i, j))],
        out_specs=pl.BlockSpec((tb, tb), lambda i, j: (j, i)),
        out_shape=jax.ShapeDtypeStruct((c, r), w.dtype),
        compiler_params=_cparams(2),
    )(w)


def _pad_to(n, m):
    return (n + m - 1) // m * m


def _retention_constants():
    gamma = (1.0 - 2.0 ** (-5.0 - np.arange(HEADS, dtype=np.float32))).astype(np.float32)
    log_g = np.log(gamma).astype(np.float32)
    i = np.arange(BLK)
    diff = (i[:, None] - i[None, :]).astype(np.float32)
    same = (i[:, None] // CHUNK) == (i[None, :] // CHUNK)
    earlier = (i[None, :] // CHUNK) < (i[:, None] // CHUNK)
    mask = np.zeros((HEADS, BLK, BLK), np.float32)
    for h in range(HEADS):
        dec_abs = np.exp(log_g[h] * np.abs(diff)).astype(np.float32)
        dec = np.exp(log_g[h] * diff * earlier).astype(np.float32)
        mask[h] = np.where(same, dec_abs, np.where(earlier, dec, 0.0))
    dq = np.zeros((BLK, V_W), np.float32)
    dk = np.zeros((BLK, QK_W), np.float32)
    gbd = np.zeros((QK_W, V_W), np.float32)
    for h in range(HEADS):
        dq[:, h * DV:(h + 1) * DV] = np.exp(log_g[h] * (i + 1.0)).astype(np.float32)[:, None]
        dk[:, h * DK:(h + 1) * DK] = np.exp(log_g[h] * (BLK - 1.0 - i)).astype(np.float32)[:, None]
        gbd[h * DK:(h + 1) * DK, h * DV:(h + 1) * DV] = np.exp(log_g[h] * np.float32(BLK))
    bd = (gbd > 0).astype(np.float32)
    return jnp.asarray(mask), jnp.asarray(dq), jnp.asarray(dk), jnp.asarray(gbd), jnp.asarray(bd)


def _rotary_tables(seq):
    half = DK // 2
    freqs = ROPE_BASE ** (-jnp.arange(half, dtype=F32) * 2.0 / DK)
    ang = jnp.arange(seq, dtype=F32)[:, None] * freqs[None, :]
    cos, sin = jnp.cos(ang), jnp.sin(ang)
    cos_t = jnp.tile(jnp.concatenate([cos, cos], axis=1), (1, HEADS))
    sin_t = jnp.tile(jnp.concatenate([-sin, sin], axis=1), (1, HEADS))
    return cos_t, sin_t


def _swap_halves(x):
    lane = lax.broadcasted_iota(jnp.int32, (1, QK_W), 1)
    first = (lane & (DK - 1)) < DK // 2
    return jnp.where(first, pltpu.roll(x, QK_W - DK // 2, 1), pltpu.roll(x, DK // 2, 1))


def _head_mask(h):
    lane = lax.broadcasted_iota(jnp.int32, (1, QK_W), 1)
    return (lane >= h * DK) & (lane < (h + 1) * DK)


def _ffn_fwd(x, n, cols, gq, wd, name, ride=None):
    t, d = x.shape
    fp = cols.shape[1]
    tm = min(t, FFN_TM)
    tf = FFN_FWD_TF
    nj = fp // tf

    def body(x_ref, n_ref, wg_ref, wu_ref, wd_ref, xo_ref, h_ref, b_ref, sil_ref, dsil_ref, s_ref, acc_ref):
        j = pl.program_id(1)

        @pl.when(j == 0)
        def _():
            xv = x_ref[...]
            r = lax.rsqrt(jnp.mean(xv * xv, axis=-1, keepdims=True) + RMS_EPS)
            h_ref[...] = (xv * r * n_ref[...]).astype(BF16)
            acc_ref[...] = jnp.zeros_like(acc_ref)

        h = h_ref[...]
        a = _dot(h, wg_ref[...])
        b = _dot(h, wu_ref[...])
        sg = _sigmoid(a)
        sil = a * sg
        s = (sil * b).astype(BF16)
        b_ref[...] = b.astype(BF16)
        sil_ref[...] = sil.astype(BF16)
        dsil_ref[...] = (sg + sil * (1.0 - sg)).astype(BF16)
        s_ref[...] = s
        acc_ref[...] += _dot(s, wd_ref[...])

        @pl.when(j == nj - 1)
        def _():
            xo_ref[...] = x_ref[...] + 0.5 * acc_ref[...]

    act = pl.BlockSpec((tm, tf), lambda i, j: (i, j))
    return _pallas(
        body, name, (t // tm, nj),
        [pl.BlockSpec((tm, d), lambda i, j: (i, 0)), pl.BlockSpec((1, d), lambda i, j: (0, 0)),
         pl.BlockSpec((d, tf), lambda i, j: (gq, j)), pl.BlockSpec((d, tf), lambda i, j: (gq + 1, j)),
         pl.BlockSpec((tf, d), lambda i, j: (j, 0))],
        [pl.BlockSpec((tm, d), lambda i, j: (i, 0)), pl.BlockSpec((tm, d), lambda i, j: (i, 0)), act, act, act, act],
        [jax.ShapeDtypeStruct((t, d), F32), jax.ShapeDtypeStruct((t, d), BF16)] + [jax.ShapeDtypeStruct((t, fp), BF16)] * 4,
        [pltpu.VMEM((tm, d), F32)], (x, n, cols, cols, wd), ride)


def _ffn_bwd_act(dxo, b, sil, dsil, wd_t, name, ride=None):
    t, d = dxo.shape
    fp = wd_t.shape[1]
    tm = min(t, FFN_TM)
    tf = fp // 3
    nj = fp // tf

    def body(dxo_ref, b_ref, sil_ref, dsil_ref, wd_ref, da_ref, db_ref, dxob_ref):
        @pl.when(pl.program_id(1) == 0)
        def _():
            dxob_ref[...] = (0.5 * dxo_ref[...]).astype(BF16)

        ds = _dot(dxob_ref[...], wd_ref[...])
        da_ref[...] = (ds * b_ref[...].astype(F32) * dsil_ref[...].astype(F32)).astype(BF16)
        db_ref[...] = (ds * sil_ref[...].astype(F32)).astype(BF16)

    act = pl.BlockSpec((tm, tf), lambda i, j: (i, j))
    return _pallas(
        body, name, (t // tm, nj),
        [pl.BlockSpec((tm, d), lambda i, j: (i, 0)), act, act, act, pl.BlockSpec((d, tf), lambda i, j: (0, j))],
        [act, act, pl.BlockSpec((tm, d), lambda i, j: (i, 0))],
        [jax.ShapeDtypeStruct((t, fp), BF16), jax.ShapeDtypeStruct((t, fp), BF16), jax.ShapeDtypeStruct((t, d), BF16)],
        [], (dxo, b, sil, dsil, wd_t), ride)


def _ffn_bwd_in(da, db, dxo, x, n, cols_t, gq, name, ride=None):
    t, d = x.shape
    fp = cols_t.shape[0]
    tm = min(t, FFN_TM)
    tf = 2 * fp // N_DEV
    nj = fp // tf

    def body(da_ref, db_ref, dxo_ref, x_ref, n_ref, wg_ref, wu_ref, dx_ref, dn_ref, acc_ref):
        i, j = pl.program_id(0), pl.program_id(1)

        @pl.when((i == 0) & (j == 0))
        def _():
            dn_ref[...] = jnp.zeros_like(dn_ref)

        @pl.when(j == 0)
        def _():
            acc_ref[...] = jnp.zeros_like(acc_ref)

        acc_ref[...] += _dot(da_ref[...], wg_ref[...]) + _dot(db_ref[...], wu_ref[...])

        @pl.when(j == nj - 1)
        def _():
            xv = x_ref[...]
            r = lax.rsqrt(jnp.mean(xv * xv, axis=-1, keepdims=True) + RMS_EPS)
            xh = xv * r
            dh = acc_ref[...]
            dn_ref[...] += jnp.sum(dh * xh, axis=0, keepdims=True)
            dhn = dh * n_ref[...]
            dx_ref[...] = dxo_ref[...] + r * (dhn - xh * jnp.mean(dhn * xh, axis=-1, keepdims=True))

    act = pl.BlockSpec((tm, tf), lambda i, j: (i, j))
    row = pl.BlockSpec((tm, d), lambda i, j: (i, 0))
    return _pallas(
        body, name, (t // tm, nj),
        [act, act, row, row, pl.BlockSpec((1, d), lambda i, j: (0, 0)),
         pl.BlockSpec((tf, d), lambda i, j: (j, gq)), pl.BlockSpec((tf, d), lambda i, j: (j, gq + 1))],
        [row, pl.BlockSpec((1, d), lambda i, j: (0, 0))],
        [jax.ShapeDtypeStruct((t, d), F32), jax.ShapeDtypeStruct((1, d), F32)],
        [pltpu.VMEM((tm, d), F32)], (da, db, dxo, x, n, cols_t, cols_t), ride)


def _wgrad(a, b, scale, tk, tn, name, ride=None):
    t, k = a.shape
    n = b.shape[1]
    tt = min(t, WGRAD_TT)
    nt = t // tt

    def body(a_ref, b_ref, o_ref, acc_ref):
        s = pl.program_id(2)

        @pl.when(s == 0)
        def _():
            acc_ref[...] = jnp.zeros_like(acc_ref)

        acc_ref[...] += _dot_tn(a_ref[...], b_ref[...])

        @pl.when(s == nt - 1)
        def _():
            o_ref[...] = (scale * acc_ref[...]).astype(BF16)

    res = _pallas(
        body, name, (k // tk, n // tn, nt),
        [pl.BlockSpec((tt, tk), lambda p, q, s: (s, p)), pl.BlockSpec((tt, tn), lambda p, q, s: (s, q))],
        [pl.BlockSpec((tk, tn), lambda p, q, s: (p, q))], [jax.ShapeDtypeStruct((k, n), BF16)],
        [pltpu.VMEM((tk, tn), F32)], (a, b), ride)
    return res[0] if ride is None else (res[0][0], res[1])


def _mix_in(x, n, w_in, cos_t, sin_t, seq, name):
    t, d = x.shape
    per_seq = seq // TM

    def body(x_ref, n_ref, w_ref, c_ref, s_ref, h_ref, q_ref, k_ref, v_ref, g_ref, u_ref):
        xv = x_ref[...]
        r = lax.rsqrt(jnp.mean(xv * xv, axis=-1, keepdims=True) + RMS_EPS)
        h = (xv * r * n_ref[...]).astype(BF16)
        h_ref[...] = h
        p = _dot(h, w_ref[...])
        c, s = c_ref[...], s_ref[...]
        q = p[:, :QK_W]
        k = p[:, QK_W:2 * QK_W]
        q_ref[...] = ((q * c + _swap_halves(q) * s) * (DK ** -0.5)).astype(BF16)
        k_ref[...] = (k * c + _swap_halves(k) * s).astype(BF16)
        v_ref[...] = p[:, 2 * QK_W:2 * QK_W + V_W].astype(BF16)
        g_ref[...] = p[:, 2 * QK_W + V_W:2 * QK_W + 2 * V_W]
        u_ref[...] = p[:, 2 * QK_W + 2 * V_W:]

    tile = lambda w: pl.BlockSpec((TM, w), lambda i: (i, 0))
    return pl.pallas_call(
        body, name=name, grid=(t // TM,),
        in_specs=[tile(d), pl.BlockSpec((1, d), lambda i: (0, 0)), pl.BlockSpec(w_in.shape, lambda i: (0, 0)),
                  pl.BlockSpec((TM, QK_W), lambda i: (i % per_seq, 0)), pl.BlockSpec((TM, QK_W), lambda i: (i % per_seq, 0))],
        out_specs=[tile(d), tile(QK_W), tile(QK_W), tile(V_W), tile(V_W), tile(POOL_W)],
        out_shape=[jax.ShapeDtypeStruct((t, d), BF16), jax.ShapeDtypeStruct((t, QK_W), BF16),
                   jax.ShapeDtypeStruct((t, QK_W), BF16), jax.ShapeDtypeStruct((t, V_W), BF16),
                   jax.ShapeDtypeStruct((t, V_W), F32), jax.ShapeDtypeStruct((t, POOL_W), F32)],
        compiler_params=_cparams(1),
    )(x, n, w_in, cos_t, sin_t)


def _mix_in_bwd(dp, dx2, x1, n, w_in_t, name, ride=None):
    t, d = x1.shape

    def body(dp_ref, dx2_ref, x_ref, n_ref, w_ref, dx_ref, dn_ref):
        @pl.when(pl.program_id(0) == 0)
        def _():
            dn_ref[...] = jnp.zeros_like(dn_ref)

        dh = _dot(dp_ref[...], w_ref[...])
        xv = x_ref[...]
        r = lax.rsqrt(jnp.mean(xv * xv, axis=-1, keepdims=True) + RMS_EPS)
        xh = xv * r
        dn_ref[...] += jnp.sum(dh * xh, axis=0, keepdims=True)
        dhn = dh * n_ref[...]
        dx_ref[...] = dx2_ref[...] + r * (dhn - xh * jnp.mean(dhn * xh, axis=-1, keepdims=True))

    tile = lambda w: pl.BlockSpec((TM, w), lambda i: (i, 0))
    return _pallas(
        body, name, (t // TM,),
        [tile(dp.shape[1]), tile(d), tile(d), pl.BlockSpec((1, d), lambda i: (0, 0)),
         pl.BlockSpec(w_in_t.shape, lambda i: (0, 0))],
        [tile(d), pl.BlockSpec((1, d), lambda i: (0, 0))],
        [jax.ShapeDtypeStruct((t, d), F32), jax.ShapeDtypeStruct((1, d), F32)],
        [], (dp, dx2, x1, n, w_in_t), ride)


def _group_norm(o):
    parts, rstds = [], []
    for h in range(HEADS):
        oh = o[:, h * DV:(h + 1) * DV]
        dlt = oh - jnp.mean(oh, axis=-1, keepdims=True)
        rstd = lax.rsqrt(jnp.mean(dlt * dlt, axis=-1, keepdims=True) + GN_EPS)
        parts.append(dlt * rstd)
        rstds.append(rstd)
    return jnp.concatenate(parts, axis=1), rstds


def _mix_core_fwd(qs, k, v, g, u, x1, consts, gain, wp, scale, w_out, nseq, seq, name, ride=None):
    t, d = x1.shape
    nblk = seq // BLK
    mask, dq, dk, gbd, bd = consts

    def body(q_ref, k_ref, v_ref, g_ref, u_ref, x1_ref, m_ref, dq_ref, dk_ref, gbd_ref, bd_ref, gain_ref, wp_ref,
             sc_ref, wo_ref, x2_ref, mix_ref, o_ref, pooled_ref, st_ref, state, halo):
        j = pl.program_id(1)

        @pl.when(j == 0)
        def _():
            state[...] = jnp.zeros_like(state)
            halo[...] = jnp.zeros_like(halo)

        qv, kv, vv = q_ref[...], k_ref[...], v_ref[...]
        st = state[...]
        st_ref[0] = st
        cross = _dot(qv, st.astype(BF16)) * dq_ref[...]
        outs = []
        for h in range(HEADS):
            qh = jnp.where(_head_mask(h), qv, jnp.zeros_like(qv))
            am = (_dot_nt(qh, kv) * m_ref[h]).astype(BF16)
            outs.append(_dot(am, vv[:, h * DV:(h + 1) * DV]))
        o = jnp.concatenate(outs, axis=1) + cross
        o_ref[...] = o
        kd = (kv.astype(F32) * dk_ref[...]).astype(BF16)
        state[...] = gbd_ref[...] * st + _dot_tn(kd, vv) * bd_ref[...]

        gv = g_ref[...]
        nrm, _ = _group_norm(o)
        ret = (gv * _sigmoid(gv)) * (nrm * gain_ref[...])

        uv = u_ref[...]
        c = jnp.concatenate([halo[...], uv], axis=0)
        halo[...] = uv[BLK - HALO:, :]
        pos = j * BLK + lax.broadcasted_iota(jnp.int32, (BLK, 1), 0)
        parts = []
        for gi, w in enumerate(WINDOWS):
            c = c + pltpu.roll(c, w // 2, 0)
            cnt = jnp.minimum(pos + 1, w).astype(F32)
            parts.append(c[HALO:, :GC] / cnt)
            if gi + 1 < len(WINDOWS):
                c = c[:, GC:]
        pooled = (jnp.concatenate(parts, axis=1) - uv).astype(BF16)
        pooled_ref[...] = pooled
        z = jnp.concatenate([_dot(pooled[:, gi * GC:(gi + 1) * GC], wp_ref[gi]) for gi in range(len(WINDOWS))], axis=1)
        mix = jnp.concatenate([ret, z * sc_ref[...]], axis=1).astype(BF16)
        mix_ref[...] = mix
        x2_ref[...] = x1_ref[...] + _dot(mix, wo_ref[...])

    blk = lambda w: pl.BlockSpec((BLK, w), lambda i, j: (i * nblk + j, 0))
    full = lambda a: pl.BlockSpec(a.shape, lambda i, j: (0,) * a.ndim)
    return _pallas(
        body, name, (nseq, nblk),
        [blk(QK_W), blk(QK_W), blk(V_W), blk(V_W), blk(POOL_W), blk(d),
         full(mask), full(dq), full(dk), full(gbd), full(bd), full(gain), full(wp), full(scale), full(w_out)],
        [blk(d), blk(d), blk(V_W), blk(POOL_W), pl.BlockSpec((1, QK_W, V_W), lambda i, j: (i * nblk + j, 0, 0))],
        [jax.ShapeDtypeStruct((t, d), F32), jax.ShapeDtypeStruct((t, d), BF16),
         jax.ShapeDtypeStruct((t, V_W), F32), jax.ShapeDtypeStruct((t, POOL_W), BF16),
         jax.ShapeDtypeStruct((nseq * nblk, QK_W, V_W), F32)],
        [pltpu.VMEM((QK_W, V_W), F32), pltpu.VMEM((HALO, POOL_W), F32)],
        (qs, k, v, g, u, x1, mask, dq, dk, gbd, bd, gain, wp, scale, w_out), ride)


def _mix_core_bwd(dx2, qs, k, v, g, o, pooled, st, consts, gain, wp, scale, w_out, cos_t, sin_t, nseq, seq, name,
                  ride=None):
    t, d = dx2.shape
    nblk = seq // BLK
    mask, dq, dk, gbd, bd = consts
    n_win = len(WINDOWS)

    def body(dx2_ref, q_ref, k_ref, v_ref, g_ref, o_ref, pooled_ref, st_ref, m_ref, dq_ref, dk_ref, gbd_ref, bd_ref,
             gain_ref, wp_ref, sc_ref, wo_ref, c_ref, s_ref,
             dp_ref, dx2b_ref, dgain_ref, dscale_ref, dwp_ref, rstate, carry):
        i, j = pl.program_id(0), pl.program_id(1)

        @pl.when((i == 0) & (j == 0))
        def _():
            dgain_ref[...] = jnp.zeros_like(dgain_ref)
            dscale_ref[...] = jnp.zeros_like(dscale_ref)
            dwp_ref[...] = jnp.zeros_like(dwp_ref)

        @pl.when(j == 0)
        def _():
            rstate[...] = jnp.zeros_like(rstate)
            carry[...] = jnp.zeros_like(carry)

        dx2b = dx2_ref[...].astype(BF16)
        dx2b_ref[...] = dx2b
        dmix = _dot(dx2b, wo_ref[...])
        dret, dpool = dmix[:, :V_W], dmix[:, V_W:]

        gv, ov, gain_v = g_ref[...], o_ref[...], gain_ref[...]
        sg = _sigmoid(gv)
        sil = gv * sg
        nrm, rstds = _group_norm(ov)
        dg = dret * (nrm * gain_v) * (sg * (1.0 + gv * (1.0 - sg)))
        dgn = dret * sil
        dgain_ref[...] += jnp.sum(dgn * nrm, axis=0, keepdims=True)
        dnrm = dgn * gain_v
        do_parts = []
        for h in range(HEADS):
            dn_h = dnrm[:, h * DV:(h + 1) * DV]
            n_h = nrm[:, h * DV:(h + 1) * DV]
            do_parts.append(rstds[h] * (dn_h - jnp.mean(dn_h, axis=-1, keepdims=True)
                                        - n_h * jnp.mean(dn_h * n_h, axis=-1, keepdims=True)))
        do = jnp.concatenate(do_parts, axis=1)
        dob = do.astype(BF16)

        qv, kv, vv = q_ref[...], k_ref[...], v_ref[...]
        stb = st_ref[0].astype(BF16)
        rs = rstate[...]
        rsb = rs.astype(BF16)
        dod = (do * dq_ref[...]).astype(BF16)
        dqs = _dot_nt(dod, stb)
        dst = _dot_tn(qv, dod) * bd_ref[...]
        dkf = dk_ref[...]
        kd = (kv.astype(F32) * dkf).astype(BF16)
        dks = _dot_nt(vv, rsb) * dkf
        dvs = _dot(kd, rsb)
        dv_parts = []
        for h in range(HEADS):
            hm = _head_mask(h)
            qh = jnp.where(hm, qv, jnp.zeros_like(qv))
            mh = m_ref[h]
            am = (_dot_nt(qh, kv) * mh).astype(BF16)
            dpm = (_dot_nt(dob[:, h * DV:(h + 1) * DV], vv[:, h * DV:(h + 1) * DV]) * mh).astype(BF16)
            dqs = dqs + jnp.where(hm, _dot(dpm, kv), 0.0)
            dks = dks + jnp.where(hm, _dot_tn(dpm, qv), 0.0)
            dv_parts.append(_dot_tn(am, dob[:, h * DV:(h + 1) * DV]))
        dvs = dvs + jnp.concatenate(dv_parts, axis=1)
        rstate[...] = dst + gbd_ref[...] * rs

        cv, sv = c_ref[...], s_ref[...]
        dqr = dqs * (DK ** -0.5)
        dq_pre = dqr * cv + _swap_halves(dqr * sv)
        dk_pre = dks * cv + _swap_halves(dks * sv)

        pv = pooled_ref[...]
        sc = sc_ref[...]
        dzb = (dpool * sc).astype(BF16)
        z_parts, dpo_parts = [], []
        for gi in range(n_win):
            p_g = pv[:, gi * GC:(gi + 1) * GC]
            dz_g = dzb[:, gi * GC:(gi + 1) * GC]
            z_parts.append(_dot(p_g, wp_ref[gi]))
            dwp_ref[gi] += _dot_tn(p_g, dz_g)
            dpo_parts.append(_dot_nt(dz_g, wp_ref[gi]))
        dscale_ref[...] += jnp.sum(dpool * jnp.concatenate(z_parts, axis=1), axis=0, keepdims=True)
        dpo = jnp.concatenate(dpo_parts, axis=1)
        pos = (nblk - 1 - j) * BLK + lax.broadcasted_iota(jnp.int32, (BLK, 1), 0)
        e = jnp.concatenate(
            [dpo[:, gi * GC:(gi + 1) * GC] / jnp.minimum(pos + 1, w).astype(F32) for gi, w in enumerate(WINDOWS)], axis=1)
        c = jnp.concatenate([e, carry[...]], axis=0)
        carry[...] = e[:HALO, :]
        rows = BLK + HALO
        lead = []
        for gi, w in enumerate(WINDOWS):
            c = c + pltpu.roll(c, rows - w // 2, 0)
            lead.append(c[:BLK, :GC])
            if gi + 1 < n_win:
                c = c[:, GC:]
        du = jnp.concatenate(lead, axis=1) - dpo

        dp_ref[:, 0:QK_W] = dq_pre.astype(BF16)
        dp_ref[:, QK_W:2 * QK_W] = dk_pre.astype(BF16)
        dp_ref[:, 2 * QK_W:2 * QK_W + V_W] = dvs.astype(BF16)
        dp_ref[:, 2 * QK_W + V_W:2 * QK_W + 2 * V_W] = dg.astype(BF16)
        dp_ref[:, 2 * QK_W + 2 * V_W:] = du.astype(BF16)

    rev = lambda i, j: i * nblk + (nblk - 1 - j)
    blk = lambda w: pl.BlockSpec((BLK, w), lambda i, j: (rev(i, j), 0))
    full = lambda a: pl.BlockSpec(a.shape, lambda i, j: (0,) * a.ndim)
    in_w = 2 * QK_W + 2 * V_W + POOL_W
    return _pallas(
        body, name, (nseq, nblk),
        [blk(d), blk(QK_W), blk(QK_W), blk(V_W), blk(V_W), blk(V_W), blk(POOL_W),
         pl.BlockSpec((1, QK_W, V_W), lambda i, j: (rev(i, j), 0, 0)),
         full(mask), full(dq), full(dk), full(gbd), full(bd), full(gain), full(wp), full(scale), full(w_out),
         pl.BlockSpec((BLK, QK_W), lambda i, j: (nblk - 1 - j, 0)),
         pl.BlockSpec((BLK, QK_W), lambda i, j: (nblk - 1 - j, 0))],
        [blk(in_w), blk(d), pl.BlockSpec((1, V_W), lambda i, j: (0, 0)),
         pl.BlockSpec((1, POOL_W), lambda i, j: (0, 0)), pl.BlockSpec((n_win, GC, GC), lambda i, j: (0, 0, 0))],
        [jax.ShapeDtypeStruct((t, in_w), BF16), jax.ShapeDtypeStruct((t, d), BF16),
         jax.ShapeDtypeStruct((1, V_W), F32), jax.ShapeDtypeStruct((1, POOL_W), F32),
         jax.ShapeDtypeStruct((n_win, GC, GC), F32)],
        [pltpu.VMEM((QK_W, V_W), F32), pltpu.VMEM((HALO, POOL_W), F32)],
        (dx2, qs, k, v, g, o, pooled, st, mask, dq, dk, gbd, bd, gain, wp, scale, w_out, cos_t, sin_t), ride)


def _loss_head(x3, nf, tgt, name):
    t, d = x3.shape

    def body(x_ref, n_ref, t_ref, dx_ref, dn_ref, loss_ref):
        @pl.when(pl.program_id(0) == 0)
        def _():
            dn_ref[...] = jnp.zeros_like(dn_ref)
            loss_ref[...] = jnp.zeros_like(loss_ref)

        xv = x_ref[...]
        nv = n_ref[...]
        r = lax.rsqrt(jnp.mean(xv * xv, axis=-1, keepdims=True) + RMS_EPS)
        xh = xv * r
        err = xh * nv - t_ref[...]
        row = jnp.mean(err * err, axis=-1, keepdims=True)
        loss_ref[...] += 0.5 * jnp.sum(row, axis=0, keepdims=True)
        dy = err * (1.0 / d)
        dn_ref[...] += jnp.sum(dy * xh, axis=0, keepdims=True)
        dxh = dy * nv
        dx_ref[...] = r * (dxh - xh * jnp.mean(dxh * xh, axis=-1, keepdims=True))

    tile = pl.BlockSpec((TM, d), lambda i: (i, 0))
    return pl.pallas_call(
        body, name=name, grid=(t // TM,),
        in_specs=[tile, pl.BlockSpec((1, d), lambda i: (0, 0)), tile],
        out_specs=[tile, pl.BlockSpec((1, d), lambda i: (0, 0)), pl.BlockSpec((1, 1), lambda i: (0, 0))],
        out_shape=[jax.ShapeDtypeStruct((t, d), F32), jax.ShapeDtypeStruct((1, d), F32), jax.ShapeDtypeStruct((1, 1), F32)],
        compiler_params=_cparams(1),
    )(x3, nf, tgt)


def _coords():
    return lax.axis_index("x"), lax.axis_index("y"), lax.axis_index("c")


def _window(ref, kind, idx, size):
    if kind == "col":
        return ref.at[:, pl.ds(pl.multiple_of(idx * size, LANE), size)]
    return ref.at[pl.ds(pl.multiple_of(idx * size, 8), size), :]


def _run_exchange(ex, name):
    n_in = len(ex.inputs)

    def body(*refs):
        ins, outs, sems = refs[:n_in], refs[n_in:n_in + len(ex.out_shape)], refs[n_in + len(ex.out_shape):]
        ex.start(ins, outs, sems)
        if ex.mid is not None:
            ex.mid(ins, outs, sems)
        ex.finish(ins, outs, sems)

    return pl.pallas_call(body, name=name, in_specs=[ANY] * n_in, out_specs=[ANY] * len(ex.out_shape),
                          out_shape=ex.out_shape, scratch_shapes=ex.scratch)(*ex.inputs)


def _join(exchanges):
    bounds = []
    i0 = o0 = s0 = 0
    for ex in exchanges:
        bounds.append((i0, o0, s0))
        i0, o0, s0 = i0 + len(ex.inputs), o0 + len(ex.out_shape), s0 + len(ex.scratch)

    def phase(which):
        def run(ins, outs, sems):
            for ex, (i, o, s) in zip(exchanges, bounds):
                fn = getattr(ex, which)
                if fn is not None:
                    fn(ins[i:i + len(ex.inputs)], outs[o:o + len(ex.out_shape)], sems[s:s + len(ex.scratch)])
        return run

    return _Exchange(sum((ex.inputs for ex in exchanges), []), sum((ex.out_shape for ex in exchanges), []),
                     sum((ex.scratch for ex in exchanges), []), phase("start"), phase("finish"),
                     phase("mid") if any(ex.mid is not None for ex in exchanges) else None)


def _gather_exchange(parts):
    n = len(parts)
    kinds = [kd for _, kd in parts]
    sizes = [a.shape[1] if kd == "col" else a.shape[0] for a, kd in parts]

    def plan(ins, outs, sems):
        send_sems, recv_sems, local_sems = sems
        x, y, c = _coords()
        me, sibling = (x, y, c), (x, y, 1 - c)
        chips = [(1 - x, y), (x, 1 - y), (1 - x, 1 - y)]

        def win(p, dev):
            return _window(outs[p], kinds[p], 4 * dev[0] + 2 * dev[1] + dev[2], sizes[p])

        def copy(p, k, block, to, src=None):
            return pltpu.make_async_remote_copy(
                src_ref=win(p, block) if src is None else src, dst_ref=win(p, block),
                send_sem=send_sems.at[p * 7 + k], recv_sem=recv_sems.at[p * 7 + k], device_id=to, device_id_type=MESH_ID)

        mine = [pltpu.make_async_copy(ins[p], win(p, me), local_sems.at[p]) for p in range(n)]
        first, arrived, passed, rest = [], [], [], []
        for p in range(n):
            first.append(copy(p, 0, me, sibling, src=ins[p]))
            first += [copy(p, 1 + q, me, (*chip, c), src=ins[p]) for q, chip in enumerate(chips)]
            rest.append(copy(p, 0, sibling, me))
            rest += [copy(p, 4 + q, (*chip, 1 - c), me) for q, chip in enumerate(chips)]
        for q, chip in enumerate(chips):
            for p in range(n):
                arrived.append(copy(p, 1 + q, (*chip, c), me))
                passed.append(copy(p, 4 + q, (*chip, c), sibling))
        return mine, first, arrived, passed, rest

    def start(ins, outs, sems):
        mine, first, _, _, _ = plan(ins, outs, sems)
        for cp in mine + first:
            cp.start()

    def mid(ins, outs, sems):
        _, _, arrived, passed, _ = plan(ins, outs, sems)
        for got, fwd in zip(arrived, passed):
            got.wait_recv()
            fwd.start()

    def finish(ins, outs, sems):
        mine, first, _, passed, rest = plan(ins, outs, sems)
        for cp in rest:
            cp.wait_recv()
        for cp in first + passed:
            cp.wait_send()
        for cp in mine:
            cp.wait()

    out_shape = [jax.ShapeDtypeStruct((a.shape[0], N_DEV * a.shape[1]) if kd == "col" else (N_DEV * a.shape[0], a.shape[1]),
                                      a.dtype) for a, kd in parts]
    scratch = [pltpu.SemaphoreType.DMA((7 * n,)), pltpu.SemaphoreType.DMA((7 * n,)), pltpu.SemaphoreType.DMA((n,))]
    return _Exchange([a for a, _ in parts], out_shape, scratch, start, finish, mid)


def _all_gather(parts, name):
    return _run_exchange(_gather_exchange(parts), name)


def _shard_shape(a, kd):
    return (a.shape[0], a.shape[1] // N_DEV) if kd == "col" else (a.shape[0] // N_DEV, a.shape[1])


def _symmetric_exchange(inputs, out_shape, n_copies, plan):
    def start(ins, outs, sems):
        for cp in plan(ins, outs, sems):
            cp.start()

    def finish(ins, outs, sems):
        copies = plan(ins, outs, sems)
        for cp in copies:
            cp.wait_recv()
        for cp in copies:
            cp.wait_send()

    scratch = [pltpu.SemaphoreType.DMA((n_copies,)), pltpu.SemaphoreType.DMA((n_copies,))]
    return _Exchange(inputs, out_shape, scratch, start, finish)


def _rs_pair_exchange(grads):
    n = len(grads)
    kinds = [kd for _, kd in grads]
    shapes = [_shard_shape(a, kd) for a, kd in grads]

    def plan(ins, outs, sems):
        send_sems, recv_sems = sems
        x, y, c = _coords()
        copies = []
        for p in range(n):
            size = shapes[p][1] if kinds[p] == "col" else shapes[p][0]
            for s in range(4):
                src = _window(ins[p], kinds[p], 2 * s + (1 - c), size)
                copies.append(pltpu.make_async_remote_copy(
                    src_ref=src, dst_ref=outs[p].at[s], send_sem=send_sems.at[4 * p + s], recv_sem=recv_sems.at[4 * p + s],
                    device_id=(x, y, 1 - c), device_id_type=MESH_ID))
        return copies

    return _symmetric_exchange([a for a, _ in grads], [jax.ShapeDtypeStruct((4,) + shapes[p], BF16) for p in range(n)],
                               4 * n, plan)


def _rs_chips_exchange(sums):
    n = len(sums)

    def plan(ins, outs, sems):
        send_sems, recv_sems = sems
        x, y, c = _coords()
        chips = [(1 - x, y), (x, 1 - y), (1 - x, 1 - y)]
        copies = []
        for p in range(n):
            for q, (cx, cy) in enumerate(chips):
                copies.append(pltpu.make_async_remote_copy(
                    src_ref=ins[p].at[2 * cx + cy], dst_ref=outs[p].at[q],
                    send_sem=send_sems.at[3 * p + q], recv_sem=recv_sems.at[3 * p + q],
                    device_id=(cx, cy, c), device_id_type=MESH_ID))
        return copies

    return _symmetric_exchange(list(sums), [jax.ShapeDtypeStruct((3,) + a.shape[1:], BF16) for a in sums], 3 * n, plan)


def _rs_pair(grads, name):
    return _run_exchange(_rs_pair_exchange(grads), name)


def _rs_chips(sums, name):
    return _run_exchange(_rs_chips_exchange(sums), name)


def _pair_sum(grad, kd, recv, core, name):
    _, r, cw = recv.shape
    tr = min(r, TM)

    def body(core_ref, g_ref, r_ref, o_ref):
        del core_ref
        o_ref[0] = (g_ref[...].astype(F32) + r_ref[0].astype(F32)).astype(BF16)

    if kd == "col":
        g_spec = pl.BlockSpec((tr, cw), lambda s, i, core_ref: (i, 2 * s + core_ref[0]))
    else:
        g_spec = pl.BlockSpec((tr, cw), lambda s, i, core_ref: ((2 * s + core_ref[0]) * (r // tr) + i, 0))
    grid_spec = pltpu.PrefetchScalarGridSpec(
        num_scalar_prefetch=1, grid=(4, r // tr),
        in_specs=[g_spec, pl.BlockSpec((1, tr, cw), lambda s, i, core_ref: (s, i, 0))],
        out_specs=pl.BlockSpec((1, tr, cw), lambda s, i, core_ref: (s, i, 0)))
    return pl.pallas_call(
        body, name=name, grid_spec=grid_spec, out_shape=jax.ShapeDtypeStruct(recv.shape, BF16),
        compiler_params=_cparams(2),
    )(core, grad, recv)


def _adam_math(w, g, m, v):
    m2 = B1 * m + (1.0 - B1) * g
    v2 = B2 * v + (1.0 - B2) * (g * g)
    m_hat = m2 / (1.0 - B1 ** STEP)
    v_hat = v2 / (1.0 - B2 ** STEP)
    delta = -LR * (m_hat / (jnp.sqrt(v_hat) + ADAM_EPS) + WD * w)
    return delta, m2, v2


def _chip_sum_adam(psum, recv, chip, w, m, v, name):
    r, cw = w.shape
    pc = psum.shape[2]
    tr = min(r, TM)

    def body(chip_ref, p_ref, r_ref, w_ref, m_ref, v_ref, g_ref, d_ref, m2_ref, v2_ref):
        del chip_ref
        g = p_ref[0].astype(F32) + r_ref[0].astype(F32) + r_ref[1].astype(F32) + r_ref[2].astype(F32)
        g = g[:, :cw]
        delta, m2, v2 = _adam_math(w_ref[...], g, m_ref[...], v_ref[...])
        g_ref[...] = g
        d_ref[...] = delta
        m2_ref[...] = m2
        v2_ref[...] = v2

    loc = pl.BlockSpec((tr, cw), lambda i, chip_ref: (i, 0))
    grid_spec = pltpu.PrefetchScalarGridSpec(
        num_scalar_prefetch=1, grid=(r // tr,),
        in_specs=[pl.BlockSpec((1, tr, pc), lambda i, chip_ref: (chip_ref[0], i, 0)),
                  pl.BlockSpec((3, tr, pc), lambda i, chip_ref: (0, i, 0)), loc, loc, loc],
        out_specs=[loc, loc, loc, loc])
    return pl.pallas_call(
        body, name=name, grid_spec=grid_spec, out_shape=[jax.ShapeDtypeStruct((r, cw), F32)] * 4,
        compiler_params=_cparams(1),
    )(chip, psum, recv, w, m, v)


def _small_allreduce_adam(partials, params, moms, vels, name, ride=None):
    n = len(partials)
    row0 = []
    rows = 0
    for a in partials:
        if a.shape[0] >= 8:
            rows = _pad_to(rows, 8)
        row0.append(rows)
        rows += a.shape[0]
    rows = _pad_to(rows, 8)
    width = max(a.shape[1] for a in partials)
    r_in = 0 if ride is None else len(ride.inputs)
    r_out = 0 if ride is None else len(ride.out_shape)

    def body(*refs):
        g_in = refs[:n]
        w_in, m_in, v_in = refs[n:2 * n], refs[2 * n:3 * n], refs[3 * n:4 * n]
        refs = refs[4 * n:]
        r_ins, refs = refs[:r_in], refs[r_in:]
        outs, refs = refs[:4 * n], refs[4 * n:]
        r_outs, refs = refs[:r_out], refs[r_out:]
        pair, chips, send_sems, recv_sems = refs[:4]
        if ride is not None:
            ride.start(r_ins, r_outs, refs[4:])
        x, y, c = _coords()
        chip = 2 * x + y
        pair[c] = jnp.zeros((rows, width), F32)
        for p in range(n):
            r, cw = partials[p].shape
            pair[c, row0[p]:row0[p] + r, 0:cw] = g_in[p][...]
        swap = pltpu.make_async_remote_copy(src_ref=pair.at[c], dst_ref=pair.at[c], send_sem=send_sems.at[0],
                                            recv_sem=recv_sems.at[0], device_id=(x, y, 1 - c), device_id_type=MESH_ID)
        swap.start()
        swap.wait_recv()
        swap.wait_send()
        chips[chip] = pair[0] + pair[1]
        copies = [pltpu.make_async_remote_copy(
            src_ref=chips.at[chip], dst_ref=chips.at[chip], send_sem=send_sems.at[1 + q], recv_sem=recv_sems.at[1 + q],
            device_id=(cx, cy, c), device_id_type=MESH_ID) for q, (cx, cy) in enumerate([(1 - x, y), (x, 1 - y), (1 - x, 1 - y)])]
        for cp in copies:
            cp.start()
        for cp in copies:
            cp.wait_recv()
        for cp in copies:
            cp.wait_send()
        for p in range(n):
            r, cw = partials[p].shape
            g = chips[0, row0[p]:row0[p] + r, 0:cw]
            for q in range(1, 4):
                g = g + chips[q, row0[p]:row0[p] + r, 0:cw]
            delta, m2, v2 = _adam_math(w_in[p][...], g, m_in[p][...], v_in[p][...])
            outs[4 * p][...] = g
            outs[4 * p + 1][...] = delta
            outs[4 * p + 2][...] = m2
            outs[4 * p + 3][...] = v2
        if ride is not None:
            ride.finish(r_ins, r_outs, refs[4:])

    out_shape = []
    for a in partials:
        out_shape += [jax.ShapeDtypeStruct(a.shape, F32)] * 4
    res = pl.pallas_call(
        body, name=name, in_specs=[VMEM_SPEC] * (4 * n) + [ANY] * r_in, out_specs=[VMEM_SPEC] * (4 * n) + [ANY] * r_out,
        out_shape=out_shape + ([] if ride is None else ride.out_shape),
        scratch_shapes=[pltpu.VMEM((2, rows, width), F32), pltpu.VMEM((4, rows, width), F32),
                        pltpu.SemaphoreType.DMA((4,)), pltpu.SemaphoreType.DMA((4,))] + ([] if ride is None else ride.scratch),
    )(*partials, *params, *moms, *vels, *([] if ride is None else ride.inputs))
    return res if ride is None else (res[:4 * n], res[4 * n:])


def _local_step(xf, tgt, nseq, seq, cols1_all, d1_all, later, small_w, core=None, trans=None, small_step=None):
    d = xf.shape[1]
    dist = core is not None
    n1, n2, gain, pool_w, pool_scale, n3, nf = small_w
    tf = 2 * cols1_all.shape[1] // N_DEV
    consts = _retention_constants()
    cos_t, sin_t = _rotary_tables(seq)
    wp_b = pool_w.astype(BF16)

    def pair_sums(grads, recv, names):
        return [_pair_sum(g, kd, r, core, "pair_sum_" + nm) for (g, kd), r, nm in zip(grads, recv, names)]

    if dist:
        (x1, h1, b1, sil1, dsil1, s1), later = _ffn_fwd(xf, n1, cols1_all, 0, d1_all, "ffn1_fwd",
                                                       ride=_gather_exchange(later))
    else:
        x1, h1, b1, sil1, dsil1, s1 = _ffn_fwd(xf, n1, cols1_all, 0, d1_all, "ffn1_fwd")
    cols2_all, d2_all, win_all, wout_all = later
    h2, qs, kr, vv, gg, uu = _mix_in(x1, n2, win_all, cos_t, sin_t, seq, "mix_in")
    fwd_mix = (qs, kr, vv, gg, uu, x1, consts, gain, wp_b, pool_scale, wout_all, nseq, seq, "mix_core_fwd")
    if dist:
        (x2, mix, oo, pooled, states), (win_t, wout_t) = _mix_core_fwd(*fwd_mix, ride=_gather_exchange(trans["mix"]))
        (x3, h3, b3, sil3, dsil3, s3), (cols2_t, d2_t) = _ffn_fwd(x2, n3, cols2_all, 0, d2_all, "ffn2_fwd",
                                                                 ride=_gather_exchange(trans["ffn2"]))
    else:
        x2, mix, oo, pooled, states = _mix_core_fwd(*fwd_mix)
        x3, h3, b3, sil3, dsil3, s3 = _ffn_fwd(x2, n3, cols2_all, 0, d2_all, "ffn2_fwd")
        cols1_t = _transpose(cols1_all, "transpose_cols1")
        cols2_t = _transpose(cols2_all, "transpose_cols2")
        d1_t = _transpose(d1_all, "transpose_down1")
        d2_t = _transpose(d2_all, "transpose_down2")
        win_t = _transpose(win_all, "transpose_w_in")
        wout_t = _transpose(wout_all, "transpose_w_out")
    dx3, dnf, loss_part = _loss_head(x3, nf, tgt, "loss_head")
    out = {}

    if dist:
        (da3, db3, dx3b), (cols1_t, d1_t) = _ffn_bwd_act(dx3, b3, sil3, dsil3, d2_t, "ffn2_bwd_act",
                                                         ride=_gather_exchange(trans["ffn1"]))
    else:
        da3, db3, dx3b = _ffn_bwd_act(dx3, b3, sil3, dsil3, d2_t, "ffn2_bwd_act")
    names2 = ["ffn2_gate", "ffn2_up", "ffn2_down"]
    grads2 = [(_wgrad(da3, h3, 1.0, tf, d, "wgrad_gate2"), "row"), (_wgrad(db3, h3, 1.0, tf, d, "wgrad_up2"), "row"),
              (_wgrad(s3, dx3b, 1.0, tf, d, "wgrad_down2"), "row")]
    if dist:
        (dx2, dn3), recv2 = _ffn_bwd_in(da3, db3, dx3, x2, n3, cols2_t, 0, "ffn2_bwd_in", ride=_rs_pair_exchange(grads2))
        sums2 = pair_sums(grads2, recv2, names2)
        (dp, dx2b, dgain, dscale, dwp), crecv2 = _mix_core_bwd(
            dx2, qs, kr, vv, gg, oo, pooled, states, consts, gain, wp_b, pool_scale, wout_t, cos_t, sin_t, nseq, seq,
            "mix_core_bwd", ride=_rs_chips_exchange(sums2))
        out.update({nm: (s, r) for nm, s, r in zip(names2, sums2, crecv2)})
    else:
        dx2, dn3 = _ffn_bwd_in(da3, db3, dx3, x2, n3, cols2_t, 0, "ffn2_bwd_in")
        dp, dx2b, dgain, dscale, dwp = _mix_core_bwd(dx2, qs, kr, vv, gg, oo, pooled, states, consts, gain, wp_b,
                                                     pool_scale, wout_t, cos_t, sin_t, nseq, seq, "mix_core_bwd")
        out.update(dict(zip(names2, grads2)))

    names_m = ["w_in", "w_out"]
    grads_m = [(_wgrad(h2, dp, 1.0, d, d, "wgrad_in"), "col"), (_wgrad(mix, dx2b, 1.0, d, d, "wgrad_out"), "row")]
    if dist:
        (dx1, dn2), recv_m = _mix_in_bwd(dp, dx2, x1, n2, win_t, "mix_in_bwd", ride=_rs_pair_exchange(grads_m))
        sums_m = pair_sums(grads_m, recv_m, names_m)
        (da1, db1, dx1b), crecv_m = _ffn_bwd_act(dx1, b1, sil1, dsil1, d1_t, "ffn1_bwd_act",
                                                 ride=_rs_chips_exchange(sums_m))
        out.update({nm: (s, r) for nm, s, r in zip(names_m, sums_m, crecv_m)})
    else:
        dx1, dn2 = _mix_in_bwd(dp, dx2, x1, n2, win_t, "mix_in_bwd")
        da1, db1, dx1b = _ffn_bwd_act(dx1, b1, sil1, dsil1, d1_t, "ffn1_bwd_act")
        out.update(dict(zip(names_m, grads_m)))

    dx0, dn1 = _ffn_bwd_in(da1, db1, dx1, xf, n1, cols1_t, 0, "ffn1_bwd_in")
    small_parts = (dn1, dn2, dgain, dwp, dscale, dn3, dnf)
    g_down = (_wgrad(s1, dx1b, 1.0, tf, d, "wgrad_down1"), "row")
    if dist:
        g_gate, recv_d = _wgrad(da1, h1, 1.0, tf, d, "wgrad_gate1", ride=_rs_pair_exchange([g_down]))
        g_gate = (g_gate, "row")
        sum_d = pair_sums([g_down], recv_d, ["ffn1_down"])
        g_up, (crecv_d, recv_g) = _wgrad(db1, h1, 1.0, tf, d, "wgrad_up1",
                                         ride=_join([_rs_chips_exchange(sum_d), _rs_pair_exchange([g_gate])]))
        g_up = (g_up, "row")
        sum_g = pair_sums([g_gate], [recv_g], ["ffn1_gate"])
        small_out, (crecv_g, recv_u) = small_step(small_parts, _join([_rs_chips_exchange(sum_g), _rs_pair_exchange([g_up])]))
        sum_u = pair_sums([g_up], [recv_u], ["ffn1_up"])
        (crecv_u,) = _run_exchange(_rs_chips_exchange(sum_u), "rs_tail")
        out.update({"ffn1_gate": (sum_g[0], crecv_g), "ffn1_up": (sum_u[0], crecv_u), "ffn1_down": (sum_d[0], crecv_d)})
        return loss_part, dx0, out, small_out
    out.update({"ffn1_gate": (_wgrad(da1, h1, 1.0, tf, d, "wgrad_gate1"), "row"),
                "ffn1_up": (_wgrad(db1, h1, 1.0, tf, d, "wgrad_up1"), "row"), "ffn1_down": g_down})
    return loss_part, dx0, out, small_parts


def kernel(x, norm_ffn1, ffn1_gate, ffn1_up, ffn1_down, norm_mix, w_in, ret_gn_gain, pool_w, pool_scale, w_out, norm_ffn2, ffn2_gate, ffn2_up, ffn2_down, norm_final, loss_target, m_norm_ffn1, m_ffn1_gate, m_ffn1_up, m_ffn1_down, m_norm_mix, m_w_in, m_ret_gn_gain, m_pool_w, m_pool_scale, m_w_out, m_norm_ffn2, m_ffn2_gate, m_ffn2_up, m_ffn2_down, m_norm_final, v_norm_ffn1, v_ffn1_gate, v_ffn1_up, v_ffn1_down, v_norm_mix, v_w_in, v_ret_gn_gain, v_pool_w, v_pool_scale, v_w_out, v_norm_ffn2, v_ffn2_gate, v_ffn2_up, v_ffn2_down, v_norm_final):
    nseq, seq, d = x.shape
    t = nseq * seq
    f_loc = ffn1_gate.shape[2]
    f_pad = _pad_to(f_loc, LANE)
    xf = x.reshape(t, d)
    tgt = loss_target.reshape(t, d)
    core = lax.axis_index("c").astype(jnp.int32).reshape(1)
    chip = (2 * lax.axis_index("x") + lax.axis_index("y")).astype(jnp.int32).reshape(1)

    colp = lambda w: jnp.pad(w[0].astype(BF16), ((0, 0), (0, f_pad - f_loc)))
    rowp = lambda w: jnp.pad(w[0].astype(BF16), ((0, f_pad - f_loc), (0, 0)))
    cols1 = jnp.concatenate([colp(ffn1_gate), colp(ffn1_up)], axis=0)
    cols2 = jnp.concatenate([colp(ffn2_gate), colp(ffn2_up)], axis=0)
    cols1_all, d1_all = _all_gather([(cols1, "col"), (rowp(ffn1_down), "row")], "all_gather_ffn1")
    d1_loc, d2_loc, win_loc, wout_loc = rowp(ffn1_down), rowp(ffn2_down), w_in[0].astype(BF16), w_out[0].astype(BF16)
    later = [(cols2, "col"), (d2_loc, "row"), (win_loc, "col"), (wout_loc, "row")]
    trans = {"mix": [(win_loc.T, "row"), (wout_loc.T, "col")], "ffn2": [(cols2.T, "row"), (d2_loc.T, "col")],
             "ffn1": [(cols1.T, "row"), (d1_loc.T, "col")]}

    flat = lambda a: a.reshape(pool_w.size // d, d)
    params = [norm_ffn1, norm_mix, ret_gn_gain, flat(pool_w), pool_scale, norm_ffn2, norm_final.reshape(1, d)]
    moms = [m_norm_ffn1, m_norm_mix, m_ret_gn_gain, flat(m_pool_w), m_pool_scale, m_norm_ffn2, m_norm_final.reshape(1, d)]
    vels = [v_norm_ffn1, v_norm_mix, v_ret_gn_gain, flat(v_pool_w), v_pool_scale, v_norm_ffn2, v_norm_final.reshape(1, d)]

    def small_step(parts, ride):
        dn1, dn2, dgain, dwp, dscale, dn3, dnf = parts
        return _small_allreduce_adam([dn1, dn2, dgain, flat(dwp), dscale, dn3, dnf], params, moms, vels,
                                     "small_allreduce_adam", ride)

    small_w = (norm_ffn1, norm_mix, ret_gn_gain, pool_w[0], pool_scale, norm_ffn2, norm_final.reshape(1, d))
    loss_part, dx0, reduced, small_out = _local_step(xf, tgt, nseq, seq, cols1_all, d1_all, later, small_w, core, trans,
                                                     small_step)

    local = {"ffn1_gate": (ffn1_gate, m_ffn1_gate, v_ffn1_gate), "ffn1_up": (ffn1_up, m_ffn1_up, v_ffn1_up),
             "ffn1_down": (ffn1_down, m_ffn1_down, v_ffn1_down), "w_in": (w_in, m_w_in, v_w_in),
             "w_out": (w_out, m_w_out, v_w_out), "ffn2_gate": (ffn2_gate, m_ffn2_gate, v_ffn2_gate),
             "ffn2_up": (ffn2_up, m_ffn2_up, v_ffn2_up), "ffn2_down": (ffn2_down, m_ffn2_down, v_ffn2_down)}
    big = {}
    for nm, (w, m, v) in local.items():
        ps, rcv = reduced[nm]
        flip = nm.endswith("gate") or nm.endswith("up")
        view = (lambda a: a[0].T) if flip else (lambda a: a[0])
        res = _chip_sum_adam(ps, rcv, chip, view(w), view(m), view(v), "adam_" + nm)
        big[nm] = tuple((a.T if flip else a)[None] for a in res)

    small_names = ["norm_ffn1", "norm_mix", "ret_gn_gain", "pool_w", "pool_scale", "norm_ffn2", "norm_final"]
    shapes = [norm_ffn1.shape, norm_mix.shape, ret_gn_gain.shape, pool_w.shape, pool_scale.shape, norm_ffn2.shape,
              norm_final.shape]
    small = {nm: tuple(small_out[4 * p + q].reshape(shapes[p]) for q in range(4)) for p, nm in enumerate(small_names)}

    loss = lax.psum(loss_part[0, 0], ("x", "y", "c"))
    order = ["norm_ffn1", "ffn1_gate", "ffn1_up", "ffn1_down", "norm_mix", "w_in", "ret_gn_gain", "pool_w", "pool_scale",
             "w_out", "norm_ffn2", "ffn2_gate", "ffn2_up", "ffn2_down", "norm_final"]
    both = {**big, **small}
    outs = [loss, dx0.reshape(nseq, seq, d)]
    for q in range(4):
        outs += [both[nm][q] for nm in order]
    return tuple(outs)
```

```python
import functools

import numpy as np
import jax
import jax.numpy as jnp
from jax import lax
from jax.experimental import pallas as pl
from jax.experimental.pallas import tpu as pltpu

F32, BF16 = jnp.float32, jnp.bfloat16
MESH_ID = pl.DeviceIdType.MESH
ANY = pl.BlockSpec(memory_space=pl.ANY)
VMEM_SPEC = pl.BlockSpec(memory_space=pltpu.VMEM)

N_DEV = 8
RMS_EPS = 1e-6
GN_EPS = 1e-5
HEADS, DK, DV = 4, 64, 128
QK_W, V_W, POOL_W = HEADS * DK, HEADS * DV, 512
WINDOWS = (2, 4, 8, 16)
GC = POOL_W // len(WINDOWS)
CHUNK = 64
BLK = 4 * CHUNK
HALO = 16
ROPE_BASE = 10000.0
LR, B1, B2, ADAM_EPS, WD, STEP = 0.001, 0.9, 0.999, 1e-08, 0.01, 10
LANE = 128
TM = 512
FFN_TM = 1024
FFN_FWD_TF = 512
WGRAD_TT = 4096
VMEM_LIMIT = 56 * 1024 * 1024


def _cparams(n_axes):
    return pltpu.CompilerParams(dimension_semantics=("arbitrary",) * n_axes, vmem_limit_bytes=VMEM_LIMIT)


class _Exchange:
    def __init__(self, inputs, out_shape, scratch, start, finish, mid=None):
        self.inputs, self.out_shape, self.scratch = list(inputs), list(out_shape), list(scratch)
        self.start, self.finish, self.mid = start, finish, mid


def _pallas(body, name, grid, in_specs, out_specs, out_shape, scratch_shapes, args, ride=None):
    n_axes = len(grid)
    if ride is None:
        return pl.pallas_call(body, name=name, grid=grid, in_specs=in_specs, out_specs=out_specs, out_shape=out_shape,
                              scratch_shapes=scratch_shapes, compiler_params=_cparams(n_axes))(*args)
    n_in, n_out, n_scr = len(in_specs), len(out_specs), len(scratch_shapes)
    r_in, r_out = len(ride.inputs), len(ride.out_shape)

    def hosted(*refs):
        ins, refs = refs[:n_in], refs[n_in:]
        r_ins, refs = refs[:r_in], refs[r_in:]
        outs, refs = refs[:n_out], refs[n_out:]
        r_outs, refs = refs[:r_out], refs[r_out:]
        scr, sems = refs[:n_scr], refs[n_scr:]
        ids = [pl.program_id(a) for a in range(n_axes)]
        first, last, inner0 = ids[0] == 0, ids[0] == grid[0] - 1, None
        for a in range(1, n_axes):
            first = first & (ids[a] == 0)
            last = last & (ids[a] == grid[a] - 1)
            inner0 = (ids[a] == 0) if inner0 is None else inner0 & (ids[a] == 0)

        @pl.when(first)
        def _():
            ride.start(r_ins, r_outs, sems)

        if ride.mid is not None:
            at_mid = ids[0] == grid[0] - 1
            if inner0 is not None:
                at_mid = at_mid & inner0

            @pl.when(at_mid)
            def _():
                ride.mid(r_ins, r_outs, sems)

        body(*ins, *outs, *scr)

        @pl.when(last)
        def _():
            ride.finish(r_ins, r_outs, sems)

    res = pl.pallas_call(
        hosted, name=name, grid=grid, in_specs=list(in_specs) + [ANY] * r_in, out_specs=list(out_specs) + [ANY] * r_out,
        out_shape=list(out_shape) + ride.out_shape, scratch_shapes=list(scratch_shapes) + ride.scratch,
        compiler_params=_cparams(n_axes))(*args, *ride.inputs)
    return res[:n_out], res[n_out:]


def _dot(a, b):
    return jnp.dot(a, b, preferred_element_type=F32)


def _dot_nt(a, b):
    return lax.dot_general(a, b, (((1,), (1,)), ((), ())), preferred_element_type=F32)


def _dot_tn(a, b):
    return lax.dot_general(a, b, (((0,), (0,)), ((), ())), preferred_element_type=F32)


def _sigmoid(x):
    return 0.5 * jnp.tanh(0.5 * x) + 0.5


def _transpose(w, name):
    r, c = w.shape
    tb = 512

    def body(x_ref, o_ref):
        o_ref[...] = x_ref[...].T

    return pl.pallas_call(
        body, name=name, grid=(r // tb, c // tb),
        in_specs=[pl.BlockSpec((tb, tb), lambda i, j: (i, j))],
        out_specs=pl.BlockSpec((tb, tb), lambda i, j: (j, i)),
        out_shape=jax.ShapeDtypeStruct((c, r), w.dtype),
        compiler_params=_cparams(2),
    )(w)


def _pad_to(n, m):
    return (n + m - 1) // m * m


def _retention_constants():
    gamma = (1.0 - 2.0 ** (-5.0 - np.arange(HEADS, dtype=np.float32))).astype(np.float32)
    log_g = np.log(gamma).astype(np.float32)
    i = np.arange(BLK)
    diff = (i[:, None] - i[None, :]).astype(np.float32)
    same = (i[:, None] // CHUNK) == (i[None, :] // CHUNK)
    earlier = (i[None, :] // CHUNK) < (i[:, None] // CHUNK)
    mask = np.zeros((HEADS, BLK, BLK), np.float32)
    for h in range(HEADS):
        dec_abs = np.exp(log_g[h] * np.abs(diff)).astype(np.float32)
        dec = np.exp(log_g[h] * diff * earlier).astype(np.float32)
        mask[h] = np.where(same, dec_abs, np.where(earlier, dec, 0.0))
    dq = np.zeros((BLK, V_W), np.float32)
    dk = np.zeros((BLK, QK_W), np.float32)
    gbd = np.zeros((QK_W, V_W), np.float32)
    for h in range(HEADS):
        dq[:, h * DV:(h + 1) * DV] = np.exp(log_g[h] * (i + 1.0)).astype(np.float32)[:, None]
        dk[:, h * DK:(h + 1) * DK] = np.exp(log_g[h] * (BLK - 1.0 - i)).astype(np.float32)[:, None]
        gbd[h * DK:(h + 1) * DK, h * DV:(h + 1) * DV] = np.exp(log_g[h] * np.float32(BLK))
    bd = (gbd > 0).astype(np.float32)
    return jnp.asarray(mask), jnp.asarray(dq), jnp.asarray(dk), jnp.asarray(gbd), jnp.asarray(bd)


def _rotary_tables(seq):
    half = DK // 2
    freqs = ROPE_BASE ** (-jnp.arange(half, dtype=F32) * 2.0 / DK)
    ang = jnp.arange(seq, dtype=F32)[:, None] * freqs[None, :]
    cos, sin = jnp.cos(ang), jnp.sin(ang)
    cos_t = jnp.tile(jnp.concatenate([cos, cos], axis=1), (1, HEADS))
    sin_t = jnp.tile(jnp.concatenate([-sin, sin], axis=1), (1, HEADS))
    return cos_t, sin_t


def _swap_halves(x):
    lane = lax.broadcasted_iota(jnp.int32, (1, QK_W), 1)
    first = (lane & (DK - 1)) < DK // 2
    return jnp.where(first, pltpu.roll(x, QK_W - DK // 2, 1), pltpu.roll(x, DK // 2, 1))


def _head_mask(h):
    lane = lax.broadcasted_iota(jnp.int32, (1, QK_W), 1)
    return (lane >= h * DK) & (lane < (h + 1) * DK)


def _ffn_fwd(x, n, cols, gq, wd, name, ride=None):
    t, d = x.shape
    fp = cols.shape[1]
    tm = min(t, FFN_TM)
    tf = FFN_FWD_TF
    nj = fp // tf

    def body(x_ref, n_ref, wg_ref, wu_ref, wd_ref, xo_ref, h_ref, b_ref, sil_ref, dsil_ref, s_ref, acc_ref):
        j = pl.program_id(1)

        @pl.when(j == 0)
        def _():
            xv = x_ref[...]
            r = lax.rsqrt(jnp.mean(xv * xv, axis=-1, keepdims=True) + RMS_EPS)
            h_ref[...] = (xv * r * n_ref[...]).astype(BF16)
            acc_ref[...] = jnp.zeros_like(acc_ref)

        h = h_ref[...]
        a = _dot(h, wg_ref[...])
        b = _dot(h, wu_ref[...])
        sg = _sigmoid(a)
        sil = a * sg
        s = (sil * b).astype(BF16)
        b_ref[...] = b.astype(BF16)
        sil_ref[...] = sil.astype(BF16)
        dsil_ref[...] = (sg + sil * (1.0 - sg)).astype(BF16)
        s_ref[...] = s
        acc_ref[...] += _dot(s, wd_ref[...])

        @pl.when(j == nj - 1)
        def _():
            xo_ref[...] = x_ref[...] + 0.5 * acc_ref[...]

    act = pl.BlockSpec((tm, tf), lambda i, j: (i, j))
    return _pallas(
        body, name, (t // tm, nj),
        [pl.BlockSpec((tm, d), lambda i, j: (i, 0)), pl.BlockSpec((1, d), lambda i, j: (0, 0)),
         pl.BlockSpec((d, tf), lambda i, j: (gq, j)), pl.BlockSpec((d, tf), lambda i, j: (gq + 1, j)),
         pl.BlockSpec((tf, d), lambda i, j: (j, 0))],
        [pl.BlockSpec((tm, d), lambda i, j: (i, 0)), pl.BlockSpec((tm, d), lambda i, j: (i, 0)), act, act, act, act],
        [jax.ShapeDtypeStruct((t, d), F32), jax.ShapeDtypeStruct((t, d), BF16)] + [jax.ShapeDtypeStruct((t, fp), BF16)] * 4,
        [pltpu.VMEM((tm, d), F32)], (x, n, cols, cols, wd), ride)


def _ffn_bwd_act(dxob, b, sil, dsil, s, wd_t, name, ride=None):
    t, d = dxob.shape
    fp = wd_t.shape[1]
    tm = min(t, FFN_TM)
    tf = 2 * fp // N_DEV
    ni = t // tm

    def body(dx_ref, b_ref, sil_ref, dsil_ref, s_ref, wd_ref, da_ref, db_ref, gd_ref, acc_ref):
        i = pl.program_id(1)

        @pl.when(i == 0)
        def _():
            acc_ref[...] = jnp.zeros_like(acc_ref)

        dxv = dx_ref[...]
        ds = _dot(dxv, wd_ref[...])
        da_ref[...] = (ds * b_ref[...].astype(F32) * dsil_ref[...].astype(F32)).astype(BF16)
        db_ref[...] = (ds * sil_ref[...].astype(F32)).astype(BF16)
        acc_ref[...] += _dot_tn(s_ref[...], dxv)

        @pl.when(i == ni - 1)
        def _():
            gd_ref[...] = acc_ref[...].astype(BF16)

    act = pl.BlockSpec((tm, tf), lambda c, i: (i, c))
    return _pallas(
        body, name, (fp // tf, ni),
        [pl.BlockSpec((tm, d), lambda c, i: (i, 0)), act, act, act, act, pl.BlockSpec((d, tf), lambda c, i: (0, c))],
        [act, act, pl.BlockSpec((tf, d), lambda c, i: (c, 0))],
        [jax.ShapeDtypeStruct((t, fp), BF16), jax.ShapeDtypeStruct((t, fp), BF16), jax.ShapeDtypeStruct((fp, d), BF16)],
        [pltpu.VMEM((tf, d), F32)], (dxob, b, sil, dsil, s, wd_t), ride)


def _ffn_bwd_in(da, db, dxo, x, n, cols_t, gq, name, ride=None):
    t, d = x.shape
    fp = cols_t.shape[0]
    tm = min(t, FFN_TM)
    tf = 2 * fp // N_DEV
    nj = fp // tf

    def body(da_ref, db_ref, dxo_ref, x_ref, n_ref, wg_ref, wu_ref, dx_ref, dn_ref, acc_ref):
        i, j = pl.program_id(0), pl.program_id(1)

        @pl.when((i == 0) & (j == 0))
        def _():
            dn_ref[...] = jnp.zeros_like(dn_ref)

        @pl.when(j == 0)
        def _():
            acc_ref[...] = jnp.zeros_like(acc_ref)

        acc_ref[...] += _dot(da_ref[...], wg_ref[...]) + _dot(db_ref[...], wu_ref[...])

        @pl.when(j == nj - 1)
        def _():
            xv = x_ref[...]
            r = lax.rsqrt(jnp.mean(xv * xv, axis=-1, keepdims=True) + RMS_EPS)
            xh = xv * r
            dh = acc_ref[...]
            dn_ref[...] += jnp.sum(dh * xh, axis=0, keepdims=True)
            dhn = dh * n_ref[...]
            dx_ref[...] = dxo_ref[...] + r * (dhn - xh * jnp.mean(dhn * xh, axis=-1, keepdims=True))

    act = pl.BlockSpec((tm, tf), lambda i, j: (i, j))
    row = pl.BlockSpec((tm, d), lambda i, j: (i, 0))
    return _pallas(
        body, name, (t // tm, nj),
        [act, act, row, row, pl.BlockSpec((1, d), lambda i, j: (0, 0)),
         pl.BlockSpec((tf, d), lambda i, j: (j, gq)), pl.BlockSpec((tf, d), lambda i, j: (j, gq + 1))],
        [row, pl.BlockSpec((1, d), lambda i, j: (0, 0))],
        [jax.ShapeDtypeStruct((t, d), F32), jax.ShapeDtypeStruct((1, d), F32)],
        [pltpu.VMEM((tm, d), F32)], (da, db, dxo, x, n, cols_t, cols_t), ride)


def _wgrad(a, b, scale, tk, tn, name, ride=None):
    t, k = a.shape
    n = b.shape[1]
    tt = min(t, WGRAD_TT)
    nt = t // tt

    def body(a_ref, b_ref, o_ref, acc_ref):
        s = pl.program_id(2)

        @pl.when(s == 0)
        def _():
            acc_ref[...] = jnp.zeros_like(acc_ref)

        acc_ref[...] += _dot_tn(a_ref[...], b_ref[...])

        @pl.when(s == nt - 1)
        def _():
            o_ref[...] = (scale * acc_ref[...]).astype(BF16)

    res = _pallas(
        body, name, (k // tk, n // tn, nt),
        [pl.BlockSpec((tt, tk), lambda p, q, s: (s, p)), pl.BlockSpec((tt, tn), lambda p, q, s: (s, q))],
        [pl.BlockSpec((tk, tn), lambda p, q, s: (p, q))], [jax.ShapeDtypeStruct((k, n), BF16)],
        [pltpu.VMEM((tk, tn), F32)], (a, b), ride)
    return res[0] if ride is None else (res[0][0], res[1])


def _mix_in(x, n, w_in, cos_t, sin_t, seq, name):
    t, d = x.shape
    per_seq = seq // TM

    def body(x_ref, n_ref, w_ref, c_ref, s_ref, h_ref, q_ref, k_ref, v_ref, g_ref, u_ref):
        xv = x_ref[...]
        r = lax.rsqrt(jnp.mean(xv * xv, axis=-1, keepdims=True) + RMS_EPS)
        h = (xv * r * n_ref[...]).astype(BF16)
        h_ref[...] = h
        p = _dot(h, w_ref[...])
        c, s = c_ref[...], s_ref[...]
        q = p[:, :QK_W]
        k = p[:, QK_W:2 * QK_W]
        q_ref[...] = ((q * c + _swap_halves(q) * s) * (DK ** -0.5)).astype(BF16)
        k_ref[...] = (k * c + _swap_halves(k) * s).astype(BF16)
        v_ref[...] = p[:, 2 * QK_W:2 * QK_W + V_W].astype(BF16)
        g_ref[...] = p[:, 2 * QK_W + V_W:2 * QK_W + 2 * V_W]
        u_ref[...] = p[:, 2 * QK_W + 2 * V_W:]

    tile = lambda w: pl.BlockSpec((TM, w), lambda i: (i, 0))
    return pl.pallas_call(
        body, name=name, grid=(t // TM,),
        in_specs=[tile(d), pl.BlockSpec((1, d), lambda i: (0, 0)), pl.BlockSpec(w_in.shape, lambda i: (0, 0)),
                  pl.BlockSpec((TM, QK_W), lambda i: (i % per_seq, 0)), pl.BlockSpec((TM, QK_W), lambda i: (i % per_seq, 0))],
        out_specs=[tile(d), tile(QK_W), tile(QK_W), tile(V_W), tile(V_W), tile(POOL_W)],
        out_shape=[jax.ShapeDtypeStruct((t, d), BF16), jax.ShapeDtypeStruct((t, QK_W), BF16),
                   jax.ShapeDtypeStruct((t, QK_W), BF16), jax.ShapeDtypeStruct((t, V_W), BF16),
                   jax.ShapeDtypeStruct((t, V_W), F32), jax.ShapeDtypeStruct((t, POOL_W), F32)],
        compiler_params=_cparams(1),
    )(x, n, w_in, cos_t, sin_t)


def _mix_in_bwd(dp, dx2, x1, n, w_in_t, name, ride=None):
    t, d = x1.shape

    def body(dp_ref, dx2_ref, x_ref, n_ref, w_ref, dx_ref, dn_ref, dxb_ref):
        @pl.when(pl.program_id(0) == 0)
        def _():
            dn_ref[...] = jnp.zeros_like(dn_ref)

        dh = _dot(dp_ref[...], w_ref[...])
        xv = x_ref[...]
        r = lax.rsqrt(jnp.mean(xv * xv, axis=-1, keepdims=True) + RMS_EPS)
        xh = xv * r
        dn_ref[...] += jnp.sum(dh * xh, axis=0, keepdims=True)
        dhn = dh * n_ref[...]
        dx = dx2_ref[...] + r * (dhn - xh * jnp.mean(dhn * xh, axis=-1, keepdims=True))
        dx_ref[...] = dx
        dxb_ref[...] = (0.5 * dx).astype(BF16)

    tile = lambda w: pl.BlockSpec((TM, w), lambda i: (i, 0))
    return _pallas(
        body, name, (t // TM,),
        [tile(dp.shape[1]), tile(d), tile(d), pl.BlockSpec((1, d), lambda i: (0, 0)),
         pl.BlockSpec(w_in_t.shape, lambda i: (0, 0))],
        [tile(d), pl.BlockSpec((1, d), lambda i: (0, 0)), tile(d)],
        [jax.ShapeDtypeStruct((t, d), F32), jax.ShapeDtypeStruct((1, d), F32), jax.ShapeDtypeStruct((t, d), BF16)],
        [], (dp, dx2, x1, n, w_in_t), ride)


def _group_norm(o):
    parts, rstds = [], []
    for h in range(HEADS):
        oh = o[:, h * DV:(h + 1) * DV]
        dlt = oh - jnp.mean(oh, axis=-1, keepdims=True)
        rstd = lax.rsqrt(jnp.mean(dlt * dlt, axis=-1, keepdims=True) + GN_EPS)
        parts.append(dlt * rstd)
        rstds.append(rstd)
    return jnp.concatenate(parts, axis=1), rstds


def _mix_core_fwd(qs, k, v, g, u, x1, consts, gain, wp, scale, w_out, nseq, seq, name, ride=None):
    t, d = x1.shape
    nblk = seq // BLK
    mask, dq, dk, gbd, bd = consts

    def body(q_ref, k_ref, v_ref, g_ref, u_ref, x1_ref, m_ref, dq_ref, dk_ref, gbd_ref, bd_ref, gain_ref, wp_ref,
             sc_ref, wo_ref, x2_ref, mix_ref, o_ref, pooled_ref, st_ref, state, halo):
        j = pl.program_id(1)

        @pl.when(j == 0)
        def _():
            state[...] = jnp.zeros_like(state)
            halo[...] = jnp.zeros_like(halo)

        qv, kv, vv = q_ref[...], k_ref[...], v_ref[...]
        st = state[...]
        st_ref[0] = st
        cross = _dot(qv, st.astype(BF16)) * dq_ref[...]
        outs = []
        for h in range(HEADS):
            qh = jnp.where(_head_mask(h), qv, jnp.zeros_like(qv))
            am = (_dot_nt(qh, kv) * m_ref[h]).astype(BF16)
            outs.append(_dot(am, vv[:, h * DV:(h + 1) * DV]))
        o = jnp.concatenate(outs, axis=1) + cross
        o_ref[...] = o
        kd = (kv.astype(F32) * dk_ref[...]).astype(BF16)
        state[...] = gbd_ref[...] * st + _dot_tn(kd, vv) * bd_ref[...]

        gv = g_ref[...]
        nrm, _ = _group_norm(o)
        ret = (gv * _sigmoid(gv)) * (nrm * gain_ref[...])

        uv = u_ref[...]
        c = jnp.concatenate([halo[...], uv], axis=0)
        halo[...] = uv[BLK - HALO:, :]
        pos = j * BLK + lax.broadcasted_iota(jnp.int32, (BLK, 1), 0)
        parts = []
        for gi, w in enumerate(WINDOWS):
            c = c + pltpu.roll(c, w // 2, 0)
            cnt = jnp.minimum(pos + 1, w).astype(F32)
            parts.append(c[HALO:, :GC] / cnt)
            if gi + 1 < len(WINDOWS):
                c = c[:, GC:]
        pooled = (jnp.concatenate(parts, axis=1) - uv).astype(BF16)
        pooled_ref[...] = pooled
        z = jnp.concatenate([_dot(pooled[:, gi * GC:(gi + 1) * GC], wp_ref[gi]) for gi in range(len(WINDOWS))], axis=1)
        mix = jnp.concatenate([ret, z * sc_ref[...]], axis=1).astype(BF16)
        mix_ref[...] = mix
        x2_ref[...] = x1_ref[...] + _dot(mix, wo_ref[...])

    blk = lambda w: pl.BlockSpec((BLK, w), lambda i, j: (i * nblk + j, 0))
    full = lambda a: pl.BlockSpec(a.shape, lambda i, j: (0,) * a.ndim)
    return _pallas(
        body, name, (nseq, nblk),
        [blk(QK_W), blk(QK_W), blk(V_W), blk(V_W), blk(POOL_W), blk(d),
         full(mask), full(dq), full(dk), full(gbd), full(bd), full(gain), full(wp), full(scale), full(w_out)],
        [blk(d), blk(d), blk(V_W), blk(POOL_W), pl.BlockSpec((1, QK_W, V_W), lambda i, j: (i * nblk + j, 0, 0))],
        [jax.ShapeDtypeStruct((t, d), F32), jax.ShapeDtypeStruct((t, d), BF16),
         jax.ShapeDtypeStruct((t, V_W), F32), jax.ShapeDtypeStruct((t, POOL_W), BF16),
         jax.ShapeDtypeStruct((nseq * nblk, QK_W, V_W), F32)],
        [pltpu.VMEM((QK_W, V_W), F32), pltpu.VMEM((HALO, POOL_W), F32)],
        (qs, k, v, g, u, x1, mask, dq, dk, gbd, bd, gain, wp, scale, w_out), ride)


def _mix_core_bwd(dx2, qs, k, v, g, o, pooled, st, consts, gain, wp, scale, w_out, cos_t, sin_t, nseq, seq, name,
                  ride=None):
    t, d = dx2.shape
    nblk = seq // BLK
    mask, dq, dk, gbd, bd = consts
    n_win = len(WINDOWS)

    def body(dx2_ref, q_ref, k_ref, v_ref, g_ref, o_ref, pooled_ref, st_ref, m_ref, dq_ref, dk_ref, gbd_ref, bd_ref,
             gain_ref, wp_ref, sc_ref, wo_ref, c_ref, s_ref,
             dp_ref, dx2b_ref, dgain_ref, dscale_ref, dwp_ref, rstate, carry):
        i, j = pl.program_id(0), pl.program_id(1)

        @pl.when((i == 0) & (j == 0))
        def _():
            dgain_ref[...] = jnp.zeros_like(dgain_ref)
            dscale_ref[...] = jnp.zeros_like(dscale_ref)
            dwp_ref[...] = jnp.zeros_like(dwp_ref)

        @pl.when(j == 0)
        def _():
            rstate[...] = jnp.zeros_like(rstate)
            carry[...] = jnp.zeros_like(carry)

        dx2b = dx2_ref[...].astype(BF16)
        dx2b_ref[...] = dx2b
        dmix = _dot(dx2b, wo_ref[...])
        dret, dpool = dmix[:, :V_W], dmix[:, V_W:]

        gv, ov, gain_v = g_ref[...], o_ref[...], gain_ref[...]
        sg = _sigmoid(gv)
        sil = gv * sg
        nrm, rstds = _group_norm(ov)
        dg = dret * (nrm * gain_v) * (sg * (1.0 + gv * (1.0 - sg)))
        dgn = dret * sil
        dgain_ref[...] += jnp.sum(dgn * nrm, axis=0, keepdims=True)
        dnrm = dgn * gain_v
        do_parts = []
        for h in range(HEADS):
            dn_h = dnrm[:, h * DV:(h + 1) * DV]
            n_h = nrm[:, h * DV:(h + 1) * DV]
            do_parts.append(rstds[h] * (dn_h - jnp.mean(dn_h, axis=-1, keepdims=True)
                                        - n_h * jnp.mean(dn_h * n_h, axis=-1, keepdims=True)))
        do = jnp.concatenate(do_parts, axis=1)
        dob = do.astype(BF16)

        qv, kv, vv = q_ref[...], k_ref[...], v_ref[...]
        stb = st_ref[0].astype(BF16)
        rs = rstate[...]
        rsb = rs.astype(BF16)
        dod = (do * dq_ref[...]).astype(BF16)
        dqs = _dot_nt(dod, stb)
        dst = _dot_tn(qv, dod) * bd_ref[...]
        dkf = dk_ref[...]
        kd = (kv.astype(F32) * dkf).astype(BF16)
        dks = _dot_nt(vv, rsb) * dkf
        dvs = _dot(kd, rsb)
        dv_parts = []
        for h in range(HEADS):
            hm = _head_mask(h)
            qh = jnp.where(hm, qv, jnp.zeros_like(qv))
            mh = m_ref[h]
            am = (_dot_nt(qh, kv) * mh).astype(BF16)
            dpm = (_dot_nt(dob[:, h * DV:(h + 1) * DV], vv[:, h * DV:(h + 1) * DV]) * mh).astype(BF16)
            dqs = dqs + jnp.where(hm, _dot(dpm, kv), 0.0)
            dks = dks + jnp.where(hm, _dot_tn(dpm, qv), 0.0)
            dv_parts.append(_dot_tn(am, dob[:, h * DV:(h + 1) * DV]))
        dvs = dvs + jnp.concatenate(dv_parts, axis=1)
        rstate[...] = dst + gbd_ref[...] * rs

        cv, sv = c_ref[...], s_ref[...]
        dqr = dqs * (DK ** -0.5)
        dq_pre = dqr * cv + _swap_halves(dqr * sv)
        dk_pre = dks * cv + _swap_halves(dks * sv)

        pv = pooled_ref[...]
        sc = sc_ref[...]
        dzb = (dpool * sc).astype(BF16)
        z_parts, dpo_parts = [], []
        for gi in range(n_win):
            p_g = pv[:, gi * GC:(gi + 1) * GC]
            dz_g = dzb[:, gi * GC:(gi + 1) * GC]
            z_parts.append(_dot(p_g, wp_ref[gi]))
            dwp_ref[gi] += _dot_tn(p_g, dz_g)
            dpo_parts.append(_dot_nt(dz_g, wp_ref[gi]))
        dscale_ref[...] += jnp.sum(dpool * jnp.concatenate(z_parts, axis=1), axis=0, keepdims=True)
        dpo = jnp.concatenate(dpo_parts, axis=1)
        pos = (nblk - 1 - j) * BLK + lax.broadcasted_iota(jnp.int32, (BLK, 1), 0)
        e = jnp.concatenate(
            [dpo[:, gi * GC:(gi + 1) * GC] / jnp.minimum(pos + 1, w).astype(F32) for gi, w in enumerate(WINDOWS)], axis=1)
        c = jnp.concatenate([e, carry[...]], axis=0)
        carry[...] = e[:HALO, :]
        rows = BLK + HALO
        lead = []
        for gi, w in enumerate(WINDOWS):
            c = c + pltpu.roll(c, rows - w // 2, 0)
            lead.append(c[:BLK, :GC])
            if gi + 1 < n_win:
                c = c[:, GC:]
        du = jnp.concatenate(lead, axis=1) - dpo

        dp_ref[:, 0:QK_W] = dq_pre.astype(BF16)
        dp_ref[:, QK_W:2 * QK_W] = dk_pre.astype(BF16)
        dp_ref[:, 2 * QK_W:2 * QK_W + V_W] = dvs.astype(BF16)
        dp_ref[:, 2 * QK_W + V_W:2 * QK_W + 2 * V_W] = dg.astype(BF16)
        dp_ref[:, 2 * QK_W + 2 * V_W:] = du.astype(BF16)

    rev = lambda i, j: i * nblk + (nblk - 1 - j)
    blk = lambda w: pl.BlockSpec((BLK, w), lambda i, j: (rev(i, j), 0))
    full = lambda a: pl.BlockSpec(a.shape, lambda i, j: (0,) * a.ndim)
    in_w = 2 * QK_W + 2 * V_W + POOL_W
    return _pallas(
        body, name, (nseq, nblk),
        [blk(d), blk(QK_W), blk(QK_W), blk(V_W), blk(V_W), blk(V_W), blk(POOL_W),
         pl.BlockSpec((1, QK_W, V_W), lambda i, j: (rev(i, j), 0, 0)),
         full(mask), full(dq), full(dk), full(gbd), full(bd), full(gain), full(wp), full(scale), full(w_out),
         pl.BlockSpec((BLK, QK_W), lambda i, j: (nblk - 1 - j, 0)),
         pl.BlockSpec((BLK, QK_W), lambda i, j: (nblk - 1 - j, 0))],
        [blk(in_w), blk(d), pl.BlockSpec((1, V_W), lambda i, j: (0, 0)),
         pl.BlockSpec((1, POOL_W), lambda i, j: (0, 0)), pl.BlockSpec((n_win, GC, GC), lambda i, j: (0, 0, 0))],
        [jax.ShapeDtypeStruct((t, in_w), BF16), jax.ShapeDtypeStruct((t, d), BF16),
         jax.ShapeDtypeStruct((1, V_W), F32), jax.ShapeDtypeStruct((1, POOL_W), F32),
         jax.ShapeDtypeStruct((n_win, GC, GC), F32)],
        [pltpu.VMEM((QK_W, V_W), F32), pltpu.VMEM((HALO, POOL_W), F32)],
        (dx2, qs, k, v, g, o, pooled, st, mask, dq, dk, gbd, bd, gain, wp, scale, w_out, cos_t, sin_t), ride)


def _loss_head(x3, nf, tgt, name):
    t, d = x3.shape

    def body(x_ref, n_ref, t_ref, dx_ref, dn_ref, loss_ref, dxb_ref):
        @pl.when(pl.program_id(0) == 0)
        def _():
            dn_ref[...] = jnp.zeros_like(dn_ref)
            loss_ref[...] = jnp.zeros_like(loss_ref)

        xv = x_ref[...]
        nv = n_ref[...]
        r = lax.rsqrt(jnp.mean(xv * xv, axis=-1, keepdims=True) + RMS_EPS)
        xh = xv * r
        err = xh * nv - t_ref[...]
        row = jnp.mean(err * err, axis=-1, keepdims=True)
        loss_ref[...] += 0.5 * jnp.sum(row, axis=0, keepdims=True)
        dy = err * (1.0 / d)
        dn_ref[...] += jnp.sum(dy * xh, axis=0, keepdims=True)
        dxh = dy * nv
        dx = r * (dxh - xh * jnp.mean(dxh * xh, axis=-1, keepdims=True))
        dx_ref[...] = dx
        dxb_ref[...] = (0.5 * dx).astype(BF16)

    tile = pl.BlockSpec((TM, d), lambda i: (i, 0))
    return pl.pallas_call(
        body, name=name, grid=(t // TM,),
        in_specs=[tile, pl.BlockSpec((1, d), lambda i: (0, 0)), tile],
        out_specs=[tile, pl.BlockSpec((1, d), lambda i: (0, 0)), pl.BlockSpec((1, 1), lambda i: (0, 0)), tile],
        out_shape=[jax.ShapeDtypeStruct((t, d), F32), jax.ShapeDtypeStruct((1, d), F32), jax.ShapeDtypeStruct((1, 1), F32),
                   jax.ShapeDtypeStruct((t, d), BF16)],
        compiler_params=_cparams(1),
    )(x3, nf, tgt)


def _coords():
    return lax.axis_index("x"), lax.axis_index("y"), lax.axis_index("c")


def _window(ref, kind, idx, size):
    if kind == "col":
        return ref.at[:, pl.ds(pl.multiple_of(idx * size, LANE), size)]
    return ref.at[pl.ds(pl.multiple_of(idx * size, 8), size), :]


def _run_exchange(ex, name):
    n_in = len(ex.inputs)

    def body(*refs):
        ins, outs, sems = refs[:n_in], refs[n_in:n_in + len(ex.out_shape)], refs[n_in + len(ex.out_shape):]
        ex.start(ins, outs, sems)
        if ex.mid is not None:
            ex.mid(ins, outs, sems)
        ex.finish(ins, outs, sems)

    return pl.pallas_call(body, name=name, in_specs=[ANY] * n_in, out_specs=[ANY] * len(ex.out_shape),
                          out_shape=ex.out_shape, scratch_shapes=ex.scratch)(*ex.inputs)


def _join(exchanges):
    bounds = []
    i0 = o0 = s0 = 0
    for ex in exchanges:
        bounds.append((i0, o0, s0))
        i0, o0, s0 = i0 + len(ex.inputs), o0 + len(ex.out_shape), s0 + len(ex.scratch)

    def phase(which):
        def run(ins, outs, sems):
            for ex, (i, o, s) in zip(exchanges, bounds):
                fn = getattr(ex, which)
                if fn is not None:
                    fn(ins[i:i + len(ex.inputs)], outs[o:o + len(ex.out_shape)], sems[s:s + len(ex.scratch)])
        return run

    return _Exchange(sum((ex.inputs for ex in exchanges), []), sum((ex.out_shape for ex in exchanges), []),
                     sum((ex.scratch for ex in exchanges), []), phase("start"), phase("finish"),
                     phase("mid") if any(ex.mid is not None for ex in exchanges) else None)


def _gather_exchange(parts):
    n = len(parts)
    kinds = [kd for _, kd in parts]
    sizes = [a.shape[1] if kd == "col" else a.shape[0] for a, kd in parts]

    def plan(ins, outs, sems):
        send_sems, recv_sems, local_sems = sems
        x, y, c = _coords()
        me, sibling = (x, y, c), (x, y, 1 - c)
        chips = [(1 - x, y), (x, 1 - y), (1 - x, 1 - y)]

        def win(p, dev):
            return _window(outs[p], kinds[p], 4 * dev[0] + 2 * dev[1] + dev[2], sizes[p])

        def copy(p, k, block, to, src=None):
            return pltpu.make_async_remote_copy(
                src_ref=win(p, block) if src is None else src, dst_ref=win(p, block),
                send_sem=send_sems.at[p * 7 + k], recv_sem=recv_sems.at[p * 7 + k], device_id=to, device_id_type=MESH_ID)

        mine = [pltpu.make_async_copy(ins[p], win(p, me), local_sems.at[p]) for p in range(n)]
        first, arrived, passed, rest = [], [], [], []
        for p in range(n):
            first.append(copy(p, 0, me, sibling, src=ins[p]))
            first += [copy(p, 1 + q, me, (*chip, c), src=ins[p]) for q, chip in enumerate(chips)]
            rest.append(copy(p, 0, sibling, me))
            rest += [copy(p, 4 + q, (*chip, 1 - c), me) for q, chip in enumerate(chips)]
        for q, chip in enumerate(chips):
            for p in range(n):
                arrived.append(copy(p, 1 + q, (*chip, c), me))
                passed.append(copy(p, 4 + q, (*chip, c), sibling))
        return mine, first, arrived, passed, rest

    def start(ins, outs, sems):
        mine, first, _, _, _ = plan(ins, outs, sems)
        for cp in mine + first:
            cp.start()

    def mid(ins, outs, sems):
        _, _, arrived, passed, _ = plan(ins, outs, sems)
        for got, fwd in zip(arrived, passed):
            got.wait_recv()
            fwd.start()

    def finish(ins, outs, sems):
        mine, first, _, passed, rest = plan(ins, outs, sems)
        for cp in rest:
            cp.wait_recv()
        for cp in first + passed:
            cp.wait_send()
        for cp in mine:
            cp.wait()

    out_shape = [jax.ShapeDtypeStruct((a.shape[0], N_DEV * a.shape[1]) if kd == "col" else (N_DEV * a.shape[0], a.shape[1]),
                                      a.dtype) for a, kd in parts]
    scratch = [pltpu.SemaphoreType.DMA((7 * n,)), pltpu.SemaphoreType.DMA((7 * n,)), pltpu.SemaphoreType.DMA((n,))]
    return _Exchange([a for a, _ in parts], out_shape, scratch, start, finish, mid)


def _all_gather(parts, name):
    return _run_exchange(_gather_exchange(parts), name)


def _shard_shape(a, kd):
    return (a.shape[0], a.shape[1] // N_DEV) if kd == "col" else (a.shape[0] // N_DEV, a.shape[1])


def _symmetric_exchange(inputs, out_shape, n_copies, plan):
    def start(ins, outs, sems):
        for cp in plan(ins, outs, sems):
            cp.start()

    def finish(ins, outs, sems):
        copies = plan(ins, outs, sems)
        for cp in copies:
            cp.wait_recv()
        for cp in copies:
            cp.wait_send()

    scratch = [pltpu.SemaphoreType.DMA((n_copies,)), pltpu.SemaphoreType.DMA((n_copies,))]
    return _Exchange(inputs, out_shape, scratch, start, finish)


def _rs_pair_exchange(grads):
    n = len(grads)
    kinds = [kd for _, kd in grads]
    shapes = [_shard_shape(a, kd) for a, kd in grads]

    def plan(ins, outs, sems):
        send_sems, recv_sems = sems
        x, y, c = _coords()
        copies = []
        for p in range(n):
            size = shapes[p][1] if kinds[p] == "col" else shapes[p][0]
            for s in range(4):
                src = _window(ins[p], kinds[p], 2 * s + (1 - c), size)
                copies.append(pltpu.make_async_remote_copy(
                    src_ref=src, dst_ref=outs[p].at[s], send_sem=send_sems.at[4 * p + s], recv_sem=recv_sems.at[4 * p + s],
                    device_id=(x, y, 1 - c), device_id_type=MESH_ID))
        return copies

    return _symmetric_exchange([a for a, _ in grads], [jax.ShapeDtypeStruct((4,) + shapes[p], BF16) for p in range(n)],
                               4 * n, plan)


def _rs_chips_exchange(sums):
    n = len(sums)

    def plan(ins, outs, sems):
        send_sems, recv_sems = sems
        x, y, c = _coords()
        chips = [(1 - x, y), (x, 1 - y), (1 - x, 1 - y)]
        copies = []
        for p in range(n):
            for q, (cx, cy) in enumerate(chips):
                copies.append(pltpu.make_async_remote_copy(
                    src_ref=ins[p].at[2 * cx + cy], dst_ref=outs[p].at[q],
                    send_sem=send_sems.at[3 * p + q], recv_sem=recv_sems.at[3 * p + q],
                    device_id=(cx, cy, c), device_id_type=MESH_ID))
        return copies

    return _symmetric_exchange(list(sums), [jax.ShapeDtypeStruct((3,) + a.shape[1:], BF16) for a in sums], 3 * n, plan)


def _rs_pair(grads, name):
    return _run_exchange(_rs_pair_exchange(grads), name)


def _rs_chips(sums, name):
    return _run_exchange(_rs_chips_exchange(sums), name)


def _pair_sum(grad, kd, recv, core, name):
    _, r, cw = recv.shape
    tr = min(r, TM)

    def body(core_ref, g_ref, r_ref, o_ref):
        del core_ref
        o_ref[0] = (g_ref[...].astype(F32) + r_ref[0].astype(F32)).astype(BF16)

    if kd == "col":
        g_spec = pl.BlockSpec((tr, cw), lambda s, i, core_ref: (i, 2 * s + core_ref[0]))
    else:
        g_spec = pl.BlockSpec((tr, cw), lambda s, i, core_ref: ((2 * s + core_ref[0]) * (r // tr) + i, 0))
    grid_spec = pltpu.PrefetchScalarGridSpec(
        num_scalar_prefetch=1, grid=(4, r // tr),
        in_specs=[g_spec, pl.BlockSpec((1, tr, cw), lambda s, i, core_ref: (s, i, 0))],
        out_specs=pl.BlockSpec((1, tr, cw), lambda s, i, core_ref: (s, i, 0)))
    return pl.pallas_call(
        body, name=name, grid_spec=grid_spec, out_shape=jax.ShapeDtypeStruct(recv.shape, BF16),
        compiler_params=_cparams(2),
    )(core, grad, recv)


def _adam_math(w, g, m, v):
    m2 = B1 * m + (1.0 - B1) * g
    v2 = B2 * v + (1.0 - B2) * (g * g)
    m_hat = m2 / (1.0 - B1 ** STEP)
    v_hat = v2 / (1.0 - B2 ** STEP)
    delta = -LR * (m_hat / (jnp.sqrt(v_hat) + ADAM_EPS) + WD * w)
    return delta, m2, v2


def _chip_sum_adam(psum, recv, chip, w, m, v, name):
    r, cw = w.shape
    pc = psum.shape[2]
    tr = min(r, TM)

    def body(chip_ref, p_ref, r_ref, w_ref, m_ref, v_ref, g_ref, d_ref, m2_ref, v2_ref):
        del chip_ref
        g = p_ref[0].astype(F32) + r_ref[0].astype(F32) + r_ref[1].astype(F32) + r_ref[2].astype(F32)
        g = g[:, :cw]
        delta, m2, v2 = _adam_math(w_ref[...], g, m_ref[...], v_ref[...])
        g_ref[...] = g
        d_ref[...] = delta
        m2_ref[...] = m2
        v2_ref[...] = v2

    loc = pl.BlockSpec((tr, cw), lambda i, chip_ref: (i, 0))
    grid_spec = pltpu.PrefetchScalarGridSpec(
        num_scalar_prefetch=1, grid=(r // tr,),
        in_specs=[pl.BlockSpec((1, tr, pc), lambda i, chip_ref: (chip_ref[0], i, 0)),
                  pl.BlockSpec((3, tr, pc), lambda i, chip_ref: (0, i, 0)), loc, loc, loc],
        out_specs=[loc, loc, loc, loc])
    return pl.pallas_call(
        body, name=name, grid_spec=grid_spec, out_shape=[jax.ShapeDtypeStruct((r, cw), F32)] * 4,
        compiler_params=_cparams(1),
    )(chip, psum, recv, w, m, v)


def _small_allreduce_adam(partials, params, moms, vels, name, ride=None):
    n = len(partials)
    row0 = []
    rows = 0
    for a in partials:
        if a.shape[0] >= 8:
            rows = _pad_to(rows, 8)
        row0.append(rows)
        rows += a.shape[0]
    rows = _pad_to(rows, 8)
    width = max(a.shape[1] for a in partials)
    r_in = 0 if ride is None else len(ride.inputs)
    r_out = 0 if ride is None else len(ride.out_shape)

    def body(*refs):
        g_in = refs[:n]
        w_in, m_in, v_in = refs[n:2 * n], refs[2 * n:3 * n], refs[3 * n:4 * n]
        refs = refs[4 * n:]
        r_ins, refs = refs[:r_in], refs[r_in:]
        outs, refs = refs[:4 * n], refs[4 * n:]
        r_outs, refs = refs[:r_out], refs[r_out:]
        pair, chips, send_sems, recv_sems = refs[:4]
        if ride is not None:
            ride.start(r_ins, r_outs, refs[4:])
        x, y, c = _coords()
        chip = 2 * x + y
        pair[c] = jnp.zeros((rows, width), F32)
        for p in range(n):
            r, cw = partials[p].shape
            pair[c, row0[p]:row0[p] + r, 0:cw] = g_in[p][...]
        swap = pltpu.make_async_remote_copy(src_ref=pair.at[c], dst_ref=pair.at[c], send_sem=send_sems.at[0],
                                            recv_sem=recv_sems.at[0], device_id=(x, y, 1 - c), device_id_type=MESH_ID)
        swap.start()
        swap.wait_recv()
        swap.wait_send()
        chips[chip] = pair[0] + pair[1]
        copies = [pltpu.make_async_remote_copy(
            src_ref=chips.at[chip], dst_ref=chips.at[chip], send_sem=send_sems.at[1 + q], recv_sem=recv_sems.at[1 + q],
            device_id=(cx, cy, c), device_id_type=MESH_ID) for q, (cx, cy) in enumerate([(1 - x, y), (x, 1 - y), (1 - x, 1 - y)])]
        for cp in copies:
            cp.start()
        for cp in copies:
            cp.wait_recv()
        for cp in copies:
            cp.wait_send()
        for p in range(n):
            r, cw = partials[p].shape
            g = chips[0, row0[p]:row0[p] + r, 0:cw]
            for q in range(1, 4):
                g = g + chips[q, row0[p]:row0[p] + r, 0:cw]
            delta, m2, v2 = _adam_math(w_in[p][...], g, m_in[p][...], v_in[p][...])
            outs[4 * p][...] = g
            outs[4 * p + 1][...] = delta
            outs[4 * p + 2][...] = m2
            outs[4 * p + 3][...] = v2
        if ride is not None:
            ride.finish(r_ins, r_outs, refs[4:])

    out_shape = []
    for a in partials:
        out_shape += [jax.ShapeDtypeStruct(a.shape, F32)] * 4
    res = pl.pallas_call(
        body, name=name, in_specs=[VMEM_SPEC] * (4 * n) + [ANY] * r_in, out_specs=[VMEM_SPEC] * (4 * n) + [ANY] * r_out,
        out_shape=out_shape + ([] if ride is None else ride.out_shape),
        scratch_shapes=[pltpu.VMEM((2, rows, width), F32), pltpu.VMEM((4, rows, width), F32),
                        pltpu.SemaphoreType.DMA((4,)), pltpu.SemaphoreType.DMA((4,))] + ([] if ride is None else ride.scratch),
    )(*partials, *params, *moms, *vels, *([] if ride is None else ride.inputs))
    return res if ride is None else (res[:4 * n], res[4 * n:])


def _local_step(xf, tgt, nseq, seq, cols1_all, d1_all, later, small_w, core=None, trans=None, small_step=None):
    d = xf.shape[1]
    dist = core is not None
    n1, n2, gain, pool_w, pool_scale, n3, nf = small_w
    tf = 2 * cols1_all.shape[1] // N_DEV
    consts = _retention_constants()
    cos_t, sin_t = _rotary_tables(seq)
    wp_b = pool_w.astype(BF16)

    def pair_sums(grads, recv, names):
        return [_pair_sum(g, kd, r, core, "pair_sum_" + nm) for (g, kd), r, nm in zip(grads, recv, names)]

    if dist:
        (x1, h1, b1, sil1, dsil1, s1), later = _ffn_fwd(xf, n1, cols1_all, 0, d1_all, "ffn1_fwd",
                                                       ride=_gather_exchange(later))
    else:
        x1, h1, b1, sil1, dsil1, s1 = _ffn_fwd(xf, n1, cols1_all, 0, d1_all, "ffn1_fwd")
    cols2_all, d2_all, win_all, wout_all = later
    h2, qs, kr, vv, gg, uu = _mix_in(x1, n2, win_all, cos_t, sin_t, seq, "mix_in")
    fwd_mix = (qs, kr, vv, gg, uu, x1, consts, gain, wp_b, pool_scale, wout_all, nseq, seq, "mix_core_fwd")
    if dist:
        (x2, mix, oo, pooled, states), (win_t, wout_t) = _mix_core_fwd(*fwd_mix, ride=_gather_exchange(trans["mix"]))
        (x3, h3, b3, sil3, dsil3, s3), (cols2_t, d2_t) = _ffn_fwd(x2, n3, cols2_all, 0, d2_all, "ffn2_fwd",
                                                                 ride=_gather_exchange(trans["ffn2"]))
    else:
        x2, mix, oo, pooled, states = _mix_core_fwd(*fwd_mix)
        x3, h3, b3, sil3, dsil3, s3 = _ffn_fwd(x2, n3, cols2_all, 0, d2_all, "ffn2_fwd")
        cols1_t = _transpose(cols1_all, "transpose_cols1")
        cols2_t = _transpose(cols2_all, "transpose_cols2")
        d1_t = _transpose(d1_all, "transpose_down1")
        d2_t = _transpose(d2_all, "transpose_down2")
        win_t = _transpose(win_all, "transpose_w_in")
        wout_t = _transpose(wout_all, "transpose_w_out")
    dx3, dnf, loss_part, dx3b = _loss_head(x3, nf, tgt, "loss_head")
    out = {}

    if dist:
        (da3, db3, g_wd2), (cols1_t, d1_t) = _ffn_bwd_act(dx3b, b3, sil3, dsil3, s3, d2_t, "ffn2_bwd_act",
                                                          ride=_gather_exchange(trans["ffn1"]))
    else:
        da3, db3, g_wd2 = _ffn_bwd_act(dx3b, b3, sil3, dsil3, s3, d2_t, "ffn2_bwd_act")
    names2 = ["ffn2_gate", "ffn2_up", "ffn2_down"]
    grads2 = [(_wgrad(da3, h3, 1.0, tf, d, "wgrad_gate2"), "row"), (_wgrad(db3, h3, 1.0, tf, d, "wgrad_up2"), "row"),
              (g_wd2, "row")]
    if dist:
        (dx2, dn3), recv2 = _ffn_bwd_in(da3, db3, dx3, x2, n3, cols2_t, 0, "ffn2_bwd_in", ride=_rs_pair_exchange(grads2))
        sums2 = pair_sums(grads2, recv2, names2)
        (dp, dx2b, dgain, dscale, dwp), crecv2 = _mix_core_bwd(
            dx2, qs, kr, vv, gg, oo, pooled, states, consts, gain, wp_b, pool_scale, wout_t, cos_t, sin_t, nseq, seq,
            "mix_core_bwd", ride=_rs_chips_exchange(sums2))
        out.update({nm: (s, r) for nm, s, r in zip(names2, sums2, crecv2)})
    else:
        dx2, dn3 = _ffn_bwd_in(da3, db3, dx3, x2, n3, cols2_t, 0, "ffn2_bwd_in")
        dp, dx2b, dgain, dscale, dwp = _mix_core_bwd(dx2, qs, kr, vv, gg, oo, pooled, states, consts, gain, wp_b,
                                                     pool_scale, wout_t, cos_t, sin_t, nseq, seq, "mix_core_bwd")
        out.update(dict(zip(names2, grads2)))

    names_m = ["w_in", "w_out"]
    grads_m = [(_wgrad(h2, dp, 1.0, d, d, "wgrad_in"), "col"), (_wgrad(mix, dx2b, 1.0, d, d, "wgrad_out"), "row")]
    if dist:
        (dx1, dn2, dx1b), recv_m = _mix_in_bwd(dp, dx2, x1, n2, win_t, "mix_in_bwd", ride=_rs_pair_exchange(grads_m))
        sums_m = pair_sums(grads_m, recv_m, names_m)
        (da1, db1, g_wd1), crecv_m = _ffn_bwd_act(dx1b, b1, sil1, dsil1, s1, d1_t, "ffn1_bwd_act",
                                                  ride=_rs_chips_exchange(sums_m))
        out.update({nm: (s, r) for nm, s, r in zip(names_m, sums_m, crecv_m)})
    else:
        dx1, dn2, dx1b = _mix_in_bwd(dp, dx2, x1, n2, win_t, "mix_in_bwd")
        da1, db1, g_wd1 = _ffn_bwd_act(dx1b, b1, sil1, dsil1, s1, d1_t, "ffn1_bwd_act")
        out.update(dict(zip(names_m, grads_m)))

    dx0, dn1 = _ffn_bwd_in(da1, db1, dx1, xf, n1, cols1_t, 0, "ffn1_bwd_in")
    small_parts = (dn1, dn2, dgain, dwp, dscale, dn3, dnf)
    g_down = (g_wd1, "row")
    if dist:
        g_gate, recv_d = _wgrad(da1, h1, 1.0, tf, d, "wgrad_gate1", ride=_rs_pair_exchange([g_down]))
        g_gate = (g_gate, "row")
        sum_d = pair_sums([g_down], recv_d, ["ffn1_down"])
        g_up, (crecv_d, recv_g) = _wgrad(db1, h1, 1.0, tf, d, "wgrad_up1",
                                         ride=_join([_rs_chips_exchange(sum_d), _rs_pair_exchange([g_gate])]))
        g_up = (g_up, "row")
        sum_g = pair_sums([g_gate], [recv_g], ["ffn1_gate"])
        small_out, (crecv_g, recv_u) = small_step(small_parts, _join([_rs_chips_exchange(sum_g), _rs_pair_exchange([g_up])]))
        sum_u = pair_sums([g_up], [recv_u], ["ffn1_up"])
        (crecv_u,) = _run_exchange(_rs_chips_exchange(sum_u), "rs_tail")
        out.update({"ffn1_gate": (sum_g[0], crecv_g), "ffn1_up": (sum_u[0], crecv_u), "ffn1_down": (sum_d[0], crecv_d)})
        return loss_part, dx0, out, small_out
    out.update({"ffn1_gate": (_wgrad(da1, h1, 1.0, tf, d, "wgrad_gate1"), "row"),
                "ffn1_up": (_wgrad(db1, h1, 1.0, tf, d, "wgrad_up1"), "row"), "ffn1_down": g_down})
    return loss_part, dx0, out, small_parts


def kernel(x, norm_ffn1, ffn1_gate, ffn1_up, ffn1_down, norm_mix, w_in, ret_gn_gain, pool_w, pool_scale, w_out, norm_ffn2, ffn2_gate, ffn2_up, ffn2_down, norm_final, loss_target, m_norm_ffn1, m_ffn1_gate, m_ffn1_up, m_ffn1_down, m_norm_mix, m_w_in, m_ret_gn_gain, m_pool_w, m_pool_scale, m_w_out, m_norm_ffn2, m_ffn2_gate, m_ffn2_up, m_ffn2_down, m_norm_final, v_norm_ffn1, v_ffn1_gate, v_ffn1_up, v_ffn1_down, v_norm_mix, v_w_in, v_ret_gn_gain, v_pool_w, v_pool_scale, v_w_out, v_norm_ffn2, v_ffn2_gate, v_ffn2_up, v_ffn2_down, v_norm_final):
    nseq, seq, d = x.shape
    t = nseq * seq
    f_loc = ffn1_gate.shape[2]
    f_pad = _pad_to(f_loc, LANE)
    xf = x.reshape(t, d)
    tgt = loss_target.reshape(t, d)
    core = lax.axis_index("c").astype(jnp.int32).reshape(1)
    chip = (2 * lax.axis_index("x") + lax.axis_index("y")).astype(jnp.int32).reshape(1)

    colp = lambda w: jnp.pad(w[0].astype(BF16), ((0, 0), (0, f_pad - f_loc)))
    rowp = lambda w: jnp.pad(w[0].astype(BF16), ((0, f_pad - f_loc), (0, 0)))
    cols1 = jnp.concatenate([colp(ffn1_gate), colp(ffn1_up)], axis=0)
    cols2 = jnp.concatenate([colp(ffn2_gate), colp(ffn2_up)], axis=0)
    cols1_all, d1_all = _all_gather([(cols1, "col"), (rowp(ffn1_down), "row")], "all_gather_ffn1")
    d1_loc, d2_loc, win_loc, wout_loc = rowp(ffn1_down), rowp(ffn2_down), w_in[0].astype(BF16), w_out[0].astype(BF16)
    later = [(cols2, "col"), (d2_loc, "row"), (win_loc, "col"), (wout_loc, "row")]
    trans = {"mix": [(win_loc.T, "row"), (wout_loc.T, "col")], "ffn2": [(cols2.T, "row"), (d2_loc.T, "col")],
             "ffn1": [(cols1.T, "row"), (d1_loc.T, "col")]}

    flat = lambda a: a.reshape(pool_w.size // d, d)
    params = [norm_ffn1, norm_mix, ret_gn_gain, flat(pool_w), pool_scale, norm_ffn2, norm_final.reshape(1, d)]
    moms = [m_norm_ffn1, m_norm_mix, m_ret_gn_gain, flat(m_pool_w), m_pool_scale, m_norm_ffn2, m_norm_final.reshape(1, d)]
    vels = [v_norm_ffn1, v_norm_mix, v_ret_gn_gain, flat(v_pool_w), v_pool_scale, v_norm_ffn2, v_norm_final.reshape(1, d)]

    def small_step(parts, ride):
        dn1, dn2, dgain, dwp, dscale, dn3, dnf = parts
        return _small_allreduce_adam([dn1, dn2, dgain, flat(dwp), dscale, dn3, dnf], params, moms, vels,
                                     "small_allreduce_adam", ride)

    small_w = (norm_ffn1, norm_mix, ret_gn_gain, pool_w[0], pool_scale, norm_ffn2, norm_final.reshape(1, d))
    loss_part, dx0, reduced, small_out = _local_step(xf, tgt, nseq, seq, cols1_all, d1_all, later, small_w, core, trans,
                                                     small_step)

    local = {"ffn1_gate": (ffn1_gate, m_ffn1_gate, v_ffn1_gate), "ffn1_up": (ffn1_up, m_ffn1_up, v_ffn1_up),
             "ffn1_down": (ffn1_down, m_ffn1_down, v_ffn1_down), "w_in": (w_in, m_w_in, v_w_in),
             "w_out": (w_out, m_w_out, v_w_out), "ffn2_gate": (ffn2_gate, m_ffn2_gate, v_ffn2_gate),
             "ffn2_up": (ffn2_up, m_ffn2_up, v_ffn2_up), "ffn2_down": (ffn2_down, m_ffn2_down, v_ffn2_down)}
    big = {}
    for nm, (w, m, v) in local.items():
        ps, rcv = reduced[nm]
        flip = nm.endswith("gate") or nm.endswith("up")
        view = (lambda a: a[0].T) if flip else (lambda a: a[0])
        res = _chip_sum_adam(ps, rcv, chip, view(w), view(m), view(v), "adam_" + nm)
        big[nm] = tuple((a.T if flip else a)[None] for a in res)

    small_names = ["norm_ffn1", "norm_mix", "ret_gn_gain", "pool_w", "pool_scale", "norm_ffn2", "norm_final"]
    shapes = [norm_ffn1.shape, norm_mix.shape, ret_gn_gain.shape, pool_w.shape, pool_scale.shape, norm_ffn2.shape,
              norm_final.shape]
    small = {nm: tuple(small_out[4 * p + q].reshape(shapes[p]) for q in range(4)) for p, nm in enumerate(small_names)}

    loss = lax.psum(loss_part[0, 0], ("x", "y", "c"))
    order = ["norm_ffn1", "ffn1_gate", "ffn1_up", "ffn1_down", "norm_mix", "w_in", "ret_gn_gain", "pool_w", "pool_scale",
             "w_out", "norm_ffn2", "ffn2_gate", "ffn2_up", "ffn2_down", "norm_final"]
    both = {**big, **small}
    outs = [loss, dx0.reshape(nseq, seq, d)]
    for q in range(4):
        outs += [both[nm][q] for nm in order]
    return tuple(outs)
```

```python
import functools

import numpy as np
import jax
import jax.numpy as jnp
from jax import lax
from jax.experimental import pallas as pl
from jax.experimental.pallas import tpu as pltpu

F32, BF16 = jnp.float32, jnp.bfloat16
MESH_ID = pl.DeviceIdType.MESH
ANY = pl.BlockSpec(memory_space=pl.ANY)
VMEM_SPEC = pl.BlockSpec(memory_space=pltpu.VMEM)

N_DEV = 8
RMS_EPS = 1e-6
GN_EPS = 1e-5
HEADS, DK, DV = 4, 64, 128
QK_W, V_W, POOL_W = HEADS * DK, HEADS * DV, 512
WINDOWS = (2, 4, 8, 16)
GC = POOL_W // len(WINDOWS)
CHUNK = 64
BLK = 4 * CHUNK
HALO = 16
ROPE_BASE = 10000.0
LR, B1, B2, ADAM_EPS, WD, STEP = 0.001, 0.9, 0.999, 1e-08, 0.01, 10
LANE = 128
TM = 512
FFN_TM = 1024
FFN_FWD_TF = 512
WGRAD_TT = 4096
VMEM_LIMIT = 56 * 1024 * 1024


def _cparams(n_axes):
    return pltpu.CompilerParams(dimension_semantics=("arbitrary",) * n_axes, vmem_limit_bytes=VMEM_LIMIT)


class _Exchange:
    def __init__(self, inputs, out_shape, scratch, start, finish, mid=None):
        self.inputs, self.out_shape, self.scratch = list(inputs), list(out_shape), list(scratch)
        self.start, self.finish, self.mid = start, finish, mid


def _pallas(body, name, grid, in_specs, out_specs, out_shape, scratch_shapes, args, ride=None):
    n_axes = len(grid)
    if ride is None:
        return pl.pallas_call(body, name=name, grid=grid, in_specs=in_specs, out_specs=out_specs, out_shape=out_shape,
                              scratch_shapes=scratch_shapes, compiler_params=_cparams(n_axes))(*args)
    n_in, n_out, n_scr = len(in_specs), len(out_specs), len(scratch_shapes)
    r_in, r_out = len(ride.inputs), len(ride.out_shape)

    def hosted(*refs):
        ins, refs = refs[:n_in], refs[n_in:]
        r_ins, refs = refs[:r_in], refs[r_in:]
        outs, refs = refs[:n_out], refs[n_out:]
        r_outs, refs = refs[:r_out], refs[r_out:]
        scr, sems = refs[:n_scr], refs[n_scr:]
        ids = [pl.program_id(a) for a in range(n_axes)]
        first, last, inner0 = ids[0] == 0, ids[0] == grid[0] - 1, None
        for a in range(1, n_axes):
            first = first & (ids[a] == 0)
            last = last & (ids[a] == grid[a] - 1)
            inner0 = (ids[a] == 0) if inner0 is None else inner0 & (ids[a] == 0)

        @pl.when(first)
        def _():
            ride.start(r_ins, r_outs, sems)

        if ride.mid is not None:
            at_mid = ids[0] == grid[0] - 1
            if inner0 is not None:
                at_mid = at_mid & inner0

            @pl.when(at_mid)
            def _():
                ride.mid(r_ins, r_outs, sems)

        body(*ins, *outs, *scr)

        @pl.when(last)
        def _():
            ride.finish(r_ins, r_outs, sems)

    res = pl.pallas_call(
        hosted, name=name, grid=grid, in_specs=list(in_specs) + [ANY] * r_in, out_specs=list(out_specs) + [ANY] * r_out,
        out_shape=list(out_shape) + ride.out_shape, scratch_shapes=list(scratch_shapes) + ride.scratch,
        compiler_params=_cparams(n_axes))(*args, *ride.inputs)
    return res[:n_out], res[n_out:]


def _dot(a, b):
    return jnp.dot(a, b, preferred_element_type=F32)


def _dot_nt(a, b):
    return lax.dot_general(a, b, (((1,), (1,)), ((), ())), preferred_element_type=F32)


def _dot_tn(a, b):
    return lax.dot_general(a, b, (((0,), (0,)), ((), ())), preferred_element_type=F32)


def _sigmoid(x):
    return 0.5 * jnp.tanh(0.5 * x) + 0.5


def _transpose(w, name):
    r, c = w.shape
    tb = 512

    def body(x_ref, o_ref):
        o_ref[...] = x_ref[...].T

    return pl.pallas_call(
        body, name=name, grid=(r // tb, c // tb),
        in_specs=[pl.BlockSpec((tb, tb), lambda i, j: (i, j))],
        out_specs=pl.BlockSpec((tb, tb), lambda i, j: (j, i)),
        out_shape=jax.ShapeDtypeStruct((c, r), w.dtype),
        compiler_params=_cparams(2),
    )(w)


def _pad_to(n, m):
    return (n + m - 1) // m * m


def _retention_constants():
    gamma = (1.0 - 2.0 ** (-5.0 - np.arange(HEADS, dtype=np.float32))).astype(np.float32)
    log_g = np.log(gamma).astype(np.float32)
    i = np.arange(BLK)
    diff = (i[:, None] - i[None, :]).astype(np.float32)
    same = (i[:, None] // CHUNK) == (i[None, :] // CHUNK)
    earlier = (i[None, :] // CHUNK) < (i[:, None] // CHUNK)
    mask = np.zeros((HEADS, BLK, BLK), np.float32)
    for h in range(HEADS):
        dec_abs = np.exp(log_g[h] * np.abs(diff)).astype(np.float32)
        dec = np.exp(log_g[h] * diff * earlier).astype(np.float32)
        mask[h] = np.where(same, dec_abs, np.where(earlier, dec, 0.0))
    dq = np.zeros((BLK, V_W), np.float32)
    dk = np.zeros((BLK, QK_W), np.float32)
    gbd = np.zeros((QK_W, V_W), np.float32)
    for h in range(HEADS):
        dq[:, h * DV:(h + 1) * DV] = np.exp(log_g[h] * (i + 1.0)).astype(np.float32)[:, None]
        dk[:, h * DK:(h + 1) * DK] = np.exp(log_g[h] * (BLK - 1.0 - i)).astype(np.float32)[:, None]
        gbd[h * DK:(h + 1) * DK, h * DV:(h + 1) * DV] = np.exp(log_g[h] * np.float32(BLK))
    bd = (gbd > 0).astype(np.float32)
    return jnp.asarray(mask), jnp.asarray(dq), jnp.asarray(dk), jnp.asarray(gbd), jnp.asarray(bd)


def _rotary_tables(seq):
    half = DK // 2
    freqs = ROPE_BASE ** (-jnp.arange(half, dtype=F32) * 2.0 / DK)
    ang = jnp.arange(seq, dtype=F32)[:, None] * freqs[None, :]
    cos, sin = jnp.cos(ang), jnp.sin(ang)
    cos_t = jnp.tile(jnp.concatenate([cos, cos], axis=1), (1, HEADS))
    sin_t = jnp.tile(jnp.concatenate([-sin, sin], axis=1), (1, HEADS))
    return cos_t, sin_t


def _swap_halves(x):
    lane = lax.broadcasted_iota(jnp.int32, (1, QK_W), 1)
    first = (lane & (DK - 1)) < DK // 2
    return jnp.where(first, pltpu.roll(x, QK_W - DK // 2, 1), pltpu.roll(x, DK // 2, 1))


def _head_mask(h):
    lane = lax.broadcasted_iota(jnp.int32, (1, QK_W), 1)
    return (lane >= h * DK) & (lane < (h + 1) * DK)


def _ffn_fwd(x, n, cols, gq, wd, name, ride=None):
    t, d = x.shape
    fp = cols.shape[1]
    tm = min(t, FFN_TM)
    tf = FFN_FWD_TF
    nj = fp // tf

    def body(x_ref, n_ref, wg_ref, wu_ref, wd_ref, xo_ref, h_ref, b_ref, sil_ref, dsil_ref, acc_ref):
        j = pl.program_id(1)

        @pl.when(j == 0)
        def _():
            xv = x_ref[...]
            r = lax.rsqrt(jnp.mean(xv * xv, axis=-1, keepdims=True) + RMS_EPS)
            h_ref[...] = (xv * r * n_ref[...]).astype(BF16)
            acc_ref[...] = jnp.zeros_like(acc_ref)

        h = h_ref[...]
        a = _dot(h, wg_ref[...])
        b = _dot(h, wu_ref[...])
        sg = _sigmoid(a)
        sil = a * sg
        b_ref[...] = b.astype(BF16)
        sil_ref[...] = sil.astype(BF16)
        dsil_ref[...] = (sg + sil * (1.0 - sg)).astype(BF16)
        acc_ref[...] += _dot((sil * b).astype(BF16), wd_ref[...])

        @pl.when(j == nj - 1)
        def _():
            xo_ref[...] = x_ref[...] + 0.5 * acc_ref[...]

    act = pl.BlockSpec((tm, tf), lambda i, j: (i, j))
    return _pallas(
        body, name, (t // tm, nj),
        [pl.BlockSpec((tm, d), lambda i, j: (i, 0)), pl.BlockSpec((1, d), lambda i, j: (0, 0)),
         pl.BlockSpec((d, tf), lambda i, j: (gq, j)), pl.BlockSpec((d, tf), lambda i, j: (gq + 1, j)),
         pl.BlockSpec((tf, d), lambda i, j: (j, 0))],
        [pl.BlockSpec((tm, d), lambda i, j: (i, 0)), pl.BlockSpec((tm, d), lambda i, j: (i, 0)), act, act, act],
        [jax.ShapeDtypeStruct((t, d), F32), jax.ShapeDtypeStruct((t, d), BF16)] + [jax.ShapeDtypeStruct((t, fp), BF16)] * 3,
        [pltpu.VMEM((tm, d), F32)], (x, n, cols, cols, wd), ride)


def _ffn_bwd_act(dxob, b, sil, dsil, wd_t, name, ride=None):
    t, d = dxob.shape
    fp = wd_t.shape[1]
    tm = min(t, FFN_TM)
    tf = 2 * fp // N_DEV
    ni = t // tm

    def body(dx_ref, b_ref, sil_ref, dsil_ref, wd_ref, da_ref, db_ref, gd_ref, acc_ref):
        i = pl.program_id(1)

        @pl.when(i == 0)
        def _():
            acc_ref[...] = jnp.zeros_like(acc_ref)

        dxv = dx_ref[...]
        bv, sv = b_ref[...].astype(F32), sil_ref[...].astype(F32)
        ds = _dot(dxv, wd_ref[...])
        da_ref[...] = (ds * bv * dsil_ref[...].astype(F32)).astype(BF16)
        db_ref[...] = (ds * sv).astype(BF16)
        acc_ref[...] += _dot_tn((sv * bv).astype(BF16), dxv)

        @pl.when(i == ni - 1)
        def _():
            gd_ref[...] = acc_ref[...].astype(BF16)

    act = pl.BlockSpec((tm, tf), lambda c, i: (i, c))
    return _pallas(
        body, name, (fp // tf, ni),
        [pl.BlockSpec((tm, d), lambda c, i: (i, 0)), act, act, act, pl.BlockSpec((d, tf), lambda c, i: (0, c))],
        [act, act, pl.BlockSpec((tf, d), lambda c, i: (c, 0))],
        [jax.ShapeDtypeStruct((t, fp), BF16), jax.ShapeDtypeStruct((t, fp), BF16), jax.ShapeDtypeStruct((fp, d), BF16)],
        [pltpu.VMEM((tf, d), F32)], (dxob, b, sil, dsil, wd_t), ride)


def _ffn_bwd_in(da, db, dxo, x, n, cols_t, gq, name, ride=None):
    t, d = x.shape
    fp = cols_t.shape[0]
    tm = min(t, FFN_TM)
    tf = 2 * fp // N_DEV
    nj = fp // tf

    def body(da_ref, db_ref, dxo_ref, x_ref, n_ref, wg_ref, wu_ref, dx_ref, dn_ref, acc_ref):
        i, j = pl.program_id(0), pl.program_id(1)

        @pl.when((i == 0) & (j == 0))
        def _():
            dn_ref[...] = jnp.zeros_like(dn_ref)

        @pl.when(j == 0)
        def _():
            acc_ref[...] = jnp.zeros_like(acc_ref)

        acc_ref[...] += _dot(da_ref[...], wg_ref[...]) + _dot(db_ref[...], wu_ref[...])

        @pl.when(j == nj - 1)
        def _():
            xv = x_ref[...]
            r = lax.rsqrt(jnp.mean(xv * xv, axis=-1, keepdims=True) + RMS_EPS)
            xh = xv * r
            dh = acc_ref[...]
            dn_ref[...] += jnp.sum(dh * xh, axis=0, keepdims=True)
            dhn = dh * n_ref[...]
            dx_ref[...] = dxo_ref[...] + r * (dhn - xh * jnp.mean(dhn * xh, axis=-1, keepdims=True))

    act = pl.BlockSpec((tm, tf), lambda i, j: (i, j))
    row = pl.BlockSpec((tm, d), lambda i, j: (i, 0))
    return _pallas(
        body, name, (t // tm, nj),
        [act, act, row, row, pl.BlockSpec((1, d), lambda i, j: (0, 0)),
         pl.BlockSpec((tf, d), lambda i, j: (j, gq)), pl.BlockSpec((tf, d), lambda i, j: (j, gq + 1))],
        [row, pl.BlockSpec((1, d), lambda i, j: (0, 0))],
        [jax.ShapeDtypeStruct((t, d), F32), jax.ShapeDtypeStruct((1, d), F32)],
        [pltpu.VMEM((tm, d), F32)], (da, db, dxo, x, n, cols_t, cols_t), ride)


def _wgrad(a, b, scale, tk, tn, name, ride=None):
    t, k = a.shape
    n = b.shape[1]
    tt = min(t, WGRAD_TT)
    nt = t // tt

    def body(a_ref, b_ref, o_ref, acc_ref):
        s = pl.program_id(2)

        @pl.when(s == 0)
        def _():
            acc_ref[...] = jnp.zeros_like(acc_ref)

        acc_ref[...] += _dot_tn(a_ref[...], b_ref[...])

        @pl.when(s == nt - 1)
        def _():
            o_ref[...] = (scale * acc_ref[...]).astype(BF16)

    res = _pallas(
        body, name, (k // tk, n // tn, nt),
        [pl.BlockSpec((tt, tk), lambda p, q, s: (s, p)), pl.BlockSpec((tt, tn), lambda p, q, s: (s, q))],
        [pl.BlockSpec((tk, tn), lambda p, q, s: (p, q))], [jax.ShapeDtypeStruct((k, n), BF16)],
        [pltpu.VMEM((tk, tn), F32)], (a, b), ride)
    return res[0] if ride is None else (res[0][0], res[1])


def _mix_in(x, n, w_in, cos_t, sin_t, seq, name):
    t, d = x.shape
    per_seq = seq // TM

    def body(x_ref, n_ref, w_ref, c_ref, s_ref, h_ref, q_ref, k_ref, v_ref, g_ref, u_ref):
        xv = x_ref[...]
        r = lax.rsqrt(jnp.mean(xv * xv, axis=-1, keepdims=True) + RMS_EPS)
        h = (xv * r * n_ref[...]).astype(BF16)
        h_ref[...] = h
        p = _dot(h, w_ref[...])
        c, s = c_ref[...], s_ref[...]
        q = p[:, :QK_W]
        k = p[:, QK_W:2 * QK_W]
        q_ref[...] = ((q * c + _swap_halves(q) * s) * (DK ** -0.5)).astype(BF16)
        k_ref[...] = (k * c + _swap_halves(k) * s).astype(BF16)
        v_ref[...] = p[:, 2 * QK_W:2 * QK_W + V_W].astype(BF16)
        g_ref[...] = p[:, 2 * QK_W + V_W:2 * QK_W + 2 * V_W]
        u_ref[...] = p[:, 2 * QK_W + 2 * V_W:]

    tile = lambda w: pl.BlockSpec((TM, w), lambda i: (i, 0))
    return pl.pallas_call(
        body, name=name, grid=(t // TM,),
        in_specs=[tile(d), pl.BlockSpec((1, d), lambda i: (0, 0)), pl.BlockSpec(w_in.shape, lambda i: (0, 0)),
                  pl.BlockSpec((TM, QK_W), lambda i: (i % per_seq, 0)), pl.BlockSpec((TM, QK_W), lambda i: (i % per_seq, 0))],
        out_specs=[tile(d), tile(QK_W), tile(QK_W), tile(V_W), tile(V_W), tile(POOL_W)],
        out_shape=[jax.ShapeDtypeStruct((t, d), BF16), jax.ShapeDtypeStruct((t, QK_W), BF16),
                   jax.ShapeDtypeStruct((t, QK_W), BF16), jax.ShapeDtypeStruct((t, V_W), BF16),
                   jax.ShapeDtypeStruct((t, V_W), F32), jax.ShapeDtypeStruct((t, POOL_W), F32)],
        compiler_params=_cparams(1),
    )(x, n, w_in, cos_t, sin_t)


def _mix_in_bwd(dp, dx2, x1, n, w_in_t, name, ride=None):
    t, d = x1.shape

    def body(dp_ref, dx2_ref, x_ref, n_ref, w_ref, dx_ref, dn_ref, dxb_ref):
        @pl.when(pl.program_id(0) == 0)
        def _():
            dn_ref[...] = jnp.zeros_like(dn_ref)

        dh = _dot(dp_ref[...], w_ref[...])
        xv = x_ref[...]
        r = lax.rsqrt(jnp.mean(xv * xv, axis=-1, keepdims=True) + RMS_EPS)
        xh = xv * r
        dn_ref[...] += jnp.sum(dh * xh, axis=0, keepdims=True)
        dhn = dh * n_ref[...]
        dx = dx2_ref[...] + r * (dhn - xh * jnp.mean(dhn * xh, axis=-1, keepdims=True))
        dx_ref[...] = dx
        dxb_ref[...] = (0.5 * dx).astype(BF16)

    tile = lambda w: pl.BlockSpec((TM, w), lambda i: (i, 0))
    return _pallas(
        body, name, (t // TM,),
        [tile(dp.shape[1]), tile(d), tile(d), pl.BlockSpec((1, d), lambda i: (0, 0)),
         pl.BlockSpec(w_in_t.shape, lambda i: (0, 0))],
        [tile(d), pl.BlockSpec((1, d), lambda i: (0, 0)), tile(d)],
        [jax.ShapeDtypeStruct((t, d), F32), jax.ShapeDtypeStruct((1, d), F32), jax.ShapeDtypeStruct((t, d), BF16)],
        [], (dp, dx2, x1, n, w_in_t), ride)


def _group_norm(o):
    parts, rstds = [], []
    for h in range(HEADS):
        oh = o[:, h * DV:(h + 1) * DV]
        dlt = oh - jnp.mean(oh, axis=-1, keepdims=True)
        rstd = lax.rsqrt(jnp.mean(dlt * dlt, axis=-1, keepdims=True) + GN_EPS)
        parts.append(dlt * rstd)
        rstds.append(rstd)
    return jnp.concatenate(parts, axis=1), rstds


def _mix_core_fwd(qs, k, v, g, u, x1, consts, gain, wp, scale, w_out, nseq, seq, name, ride=None):
    t, d = x1.shape
    nblk = seq // BLK
    mask, dq, dk, gbd, bd = consts

    def body(q_ref, k_ref, v_ref, g_ref, u_ref, x1_ref, m_ref, dq_ref, dk_ref, gbd_ref, bd_ref, gain_ref, wp_ref,
             sc_ref, wo_ref, x2_ref, mix_ref, o_ref, pooled_ref, st_ref, state, halo):
        j = pl.program_id(1)

        @pl.when(j == 0)
        def _():
            state[...] = jnp.zeros_like(state)
            halo[...] = jnp.zeros_like(halo)

        qv, kv, vv = q_ref[...], k_ref[...], v_ref[...]
        st = state[...]
        st_ref[0] = st
        cross = _dot(qv, st.astype(BF16)) * dq_ref[...]
        outs = []
        for h in range(HEADS):
            qh = jnp.where(_head_mask(h), qv, jnp.zeros_like(qv))
            am = (_dot_nt(qh, kv) * m_ref[h]).astype(BF16)
            outs.append(_dot(am, vv[:, h * DV:(h + 1) * DV]))
        o = jnp.concatenate(outs, axis=1) + cross
        o_ref[...] = o
        kd = (kv.astype(F32) * dk_ref[...]).astype(BF16)
        state[...] = gbd_ref[...] * st + _dot_tn(kd, vv) * bd_ref[...]

        gv = g_ref[...]
        nrm, _ = _group_norm(o)
        ret = (gv * _sigmoid(gv)) * (nrm * gain_ref[...])

        uv = u_ref[...]
        c = jnp.concatenate([halo[...], uv], axis=0)
        halo[...] = uv[BLK - HALO:, :]
        pos = j * BLK + lax.broadcasted_iota(jnp.int32, (BLK, 1), 0)
        parts = []
        for gi, w in enumerate(WINDOWS):
            c = c + pltpu.roll(c, w // 2, 0)
            cnt = jnp.minimum(pos + 1, w).astype(F32)
            parts.append(c[HALO:, :GC] / cnt)
            if gi + 1 < len(WINDOWS):
                c = c[:, GC:]
        pooled = (jnp.concatenate(parts, axis=1) - uv).astype(BF16)
        pooled_ref[...] = pooled
        z = jnp.concatenate([_dot(pooled[:, gi * GC:(gi + 1) * GC], wp_ref[gi]) for gi in range(len(WINDOWS))], axis=1)
        mix = jnp.concatenate([ret, z * sc_ref[...]], axis=1).astype(BF16)
        mix_ref[...] = mix
        x2_ref[...] = x1_ref[...] + _dot(mix, wo_ref[...])

    blk = lambda w: pl.BlockSpec((BLK, w), lambda i, j: (i * nblk + j, 0))
    full = lambda a: pl.BlockSpec(a.shape, lambda i, j: (0,) * a.ndim)
    return _pallas(
        body, name, (nseq, nblk),
        [blk(QK_W), blk(QK_W), blk(V_W), blk(V_W), blk(POOL_W), blk(d),
         full(mask), full(dq), full(dk), full(gbd), full(bd), full(gain), full(wp), full(scale), full(w_out)],
        [blk(d), blk(d), blk(V_W), blk(POOL_W), pl.BlockSpec((1, QK_W, V_W), lambda i, j: (i * nblk + j, 0, 0))],
        [jax.ShapeDtypeStruct((t, d), F32), jax.ShapeDtypeStruct((t, d), BF16),
         jax.ShapeDtypeStruct((t, V_W), F32), jax.ShapeDtypeStruct((t, POOL_W), BF16),
         jax.ShapeDtypeStruct((nseq * nblk, QK_W, V_W), F32)],
        [pltpu.VMEM((QK_W, V_W), F32), pltpu.VMEM((HALO, POOL_W), F32)],
        (qs, k, v, g, u, x1, mask, dq, dk, gbd, bd, gain, wp, scale, w_out), ride)


def _mix_core_bwd(dx2, qs, k, v, g, o, pooled, st, consts, gain, wp, scale, w_out, cos_t, sin_t, nseq, seq, name,
                  ride=None):
    t, d = dx2.shape
    nblk = seq // BLK
    mask, dq, dk, gbd, bd = consts
    n_win = len(WINDOWS)

    def body(dx2_ref, q_ref, k_ref, v_ref, g_ref, o_ref, pooled_ref, st_ref, m_ref, dq_ref, dk_ref, gbd_ref, bd_ref,
             gain_ref, wp_ref, sc_ref, wo_ref, c_ref, s_ref,
             dp_ref, dx2b_ref, dgain_ref, dscale_ref, dwp_ref, rstate, carry):
        i, j = pl.program_id(0), pl.program_id(1)

        @pl.when((i == 0) & (j == 0))
        def _():
            dgain_ref[...] = jnp.zeros_like(dgain_ref)
            dscale_ref[...] = jnp.zeros_like(dscale_ref)
            dwp_ref[...] = jnp.zeros_like(dwp_ref)

        @pl.when(j == 0)
        def _():
            rstate[...] = jnp.zeros_like(rstate)
            carry[...] = jnp.zeros_like(carry)

        dx2b = dx2_ref[...].astype(BF16)
        dx2b_ref[...] = dx2b
        dmix = _dot(dx2b, wo_ref[...])
        dret, dpool = dmix[:, :V_W], dmix[:, V_W:]

        gv, ov, gain_v = g_ref[...], o_ref[...], gain_ref[...]
        sg = _sigmoid(gv)
        sil = gv * sg
        nrm, rstds = _group_norm(ov)
        dg = dret * (nrm * gain_v) * (sg * (1.0 + gv * (1.0 - sg)))
        dgn = dret * sil
        dgain_ref[...] += jnp.sum(dgn * nrm, axis=0, keepdims=True)
        dnrm = dgn * gain_v
        do_parts = []
        for h in range(HEADS):
            dn_h = dnrm[:, h * DV:(h + 1) * DV]
            n_h = nrm[:, h * DV:(h + 1) * DV]
            do_parts.append(rstds[h] * (dn_h - jnp.mean(dn_h, axis=-1, keepdims=True)
                                        - n_h * jnp.mean(dn_h * n_h, axis=-1, keepdims=True)))
        do = jnp.concatenate(do_parts, axis=1)
        dob = do.astype(BF16)

        qv, kv, vv = q_ref[...], k_ref[...], v_ref[...]
        stb = st_ref[0].astype(BF16)
        rs = rstate[...]
        rsb = rs.astype(BF16)
        dod = (do * dq_ref[...]).astype(BF16)
        dqs = _dot_nt(dod, stb)
        dst = _dot_tn(qv, dod) * bd_ref[...]
        dkf = dk_ref[...]
        kd = (kv.astype(F32) * dkf).astype(BF16)
        dks = _dot_nt(vv, rsb) * dkf
        dvs = _dot(kd, rsb)
        dv_parts = []
        for h in range(HEADS):
            hm = _head_mask(h)
            qh = jnp.where(hm, qv, jnp.zeros_like(qv))
            mh = m_ref[h]
            am = (_dot_nt(qh, kv) * mh).astype(BF16)
            dpm = (_dot_nt(dob[:, h * DV:(h + 1) * DV], vv[:, h * DV:(h + 1) * DV]) * mh).astype(BF16)
            dqs = dqs + jnp.where(hm, _dot(dpm, kv), 0.0)
            dks = dks + jnp.where(hm, _dot_tn(dpm, qv), 0.0)
            dv_parts.append(_dot_tn(am, dob[:, h * DV:(h + 1) * DV]))
        dvs = dvs + jnp.concatenate(dv_parts, axis=1)
        rstate[...] = dst + gbd_ref[...] * rs

        cv, sv = c_ref[...], s_ref[...]
        dqr = dqs * (DK ** -0.5)
        dq_pre = dqr * cv + _swap_halves(dqr * sv)
        dk_pre = dks * cv + _swap_halves(dks * sv)

        pv = pooled_ref[...]
        sc = sc_ref[...]
        dzb = (dpool * sc).astype(BF16)
        z_parts, dpo_parts = [], []
        for gi in range(n_win):
            p_g = pv[:, gi * GC:(gi + 1) * GC]
            dz_g = dzb[:, gi * GC:(gi + 1) * GC]
            z_parts.append(_dot(p_g, wp_ref[gi]))
            dwp_ref[gi] += _dot_tn(p_g, dz_g)
            dpo_parts.append(_dot_nt(dz_g, wp_ref[gi]))
        dscale_ref[...] += jnp.sum(dpool * jnp.concatenate(z_parts, axis=1), axis=0, keepdims=True)
        dpo = jnp.concatenate(dpo_parts, axis=1)
        pos = (nblk - 1 - j) * BLK + lax.broadcasted_iota(jnp.int32, (BLK, 1), 0)
        e = jnp.concatenate(
            [dpo[:, gi * GC:(gi + 1) * GC] / jnp.minimum(pos + 1, w).astype(F32) for gi, w in enumerate(WINDOWS)], axis=1)
        c = jnp.concatenate([e, carry[...]], axis=0)
        carry[...] = e[:HALO, :]
        rows = BLK + HALO
        lead = []
        for gi, w in enumerate(WINDOWS):
            c = c + pltpu.roll(c, rows - w // 2, 0)
            lead.append(c[:BLK, :GC])
            if gi + 1 < n_win:
                c = c[:, GC:]
        du = jnp.concatenate(lead, axis=1) - dpo

        dp_ref[:, 0:QK_W] = dq_pre.astype(BF16)
        dp_ref[:, QK_W:2 * QK_W] = dk_pre.astype(BF16)
        dp_ref[:, 2 * QK_W:2 * QK_W + V_W] = dvs.astype(BF16)
        dp_ref[:, 2 * QK_W + V_W:2 * QK_W + 2 * V_W] = dg.astype(BF16)
        dp_ref[:, 2 * QK_W + 2 * V_W:] = du.astype(BF16)

    rev = lambda i, j: i * nblk + (nblk - 1 - j)
    blk = lambda w: pl.BlockSpec((BLK, w), lambda i, j: (rev(i, j), 0))
    full = lambda a: pl.BlockSpec(a.shape, lambda i, j: (0,) * a.ndim)
    in_w = 2 * QK_W + 2 * V_W + POOL_W
    return _pallas(
        body, name, (nseq, nblk),
        [blk(d), blk(QK_W), blk(QK_W), blk(V_W), blk(V_W), blk(V_W), blk(POOL_W),
         pl.BlockSpec((1, QK_W, V_W), lambda i, j: (rev(i, j), 0, 0)),
         full(mask), full(dq), full(dk), full(gbd), full(bd), full(gain), full(wp), full(scale), full(w_out),
         pl.BlockSpec((BLK, QK_W), lambda i, j: (nblk - 1 - j, 0)),
         pl.BlockSpec((BLK, QK_W), lambda i, j: (nblk - 1 - j, 0))],
        [blk(in_w), blk(d), pl.BlockSpec((1, V_W), lambda i, j: (0, 0)),
         pl.BlockSpec((1, POOL_W), lambda i, j: (0, 0)), pl.BlockSpec((n_win, GC, GC), lambda i, j: (0, 0, 0))],
        [jax.ShapeDtypeStruct((t, in_w), BF16), jax.ShapeDtypeStruct((t, d), BF16),
         jax.ShapeDtypeStruct((1, V_W), F32), jax.ShapeDtypeStruct((1, POOL_W), F32),
         jax.ShapeDtypeStruct((n_win, GC, GC), F32)],
        [pltpu.VMEM((QK_W, V_W), F32), pltpu.VMEM((HALO, POOL_W), F32)],
        (dx2, qs, k, v, g, o, pooled, st, mask, dq, dk, gbd, bd, gain, wp, scale, w_out, cos_t, sin_t), ride)


def _loss_head(x3, nf, tgt, name):
    t, d = x3.shape

    def body(x_ref, n_ref, t_ref, dx_ref, dn_ref, loss_ref, dxb_ref):
        @pl.when(pl.program_id(0) == 0)
        def _():
            dn_ref[...] = jnp.zeros_like(dn_ref)
            loss_ref[...] = jnp.zeros_like(loss_ref)

        xv = x_ref[...]
        nv = n_ref[...]
        r = lax.rsqrt(jnp.mean(xv * xv, axis=-1, keepdims=True) + RMS_EPS)
        xh = xv * r
        err = xh * nv - t_ref[...]
        row = jnp.mean(err * err, axis=-1, keepdims=True)
        loss_ref[...] += 0.5 * jnp.sum(row, axis=0, keepdims=True)
        dy = err * (1.0 / d)
        dn_ref[...] += jnp.sum(dy * xh, axis=0, keepdims=True)
        dxh = dy * nv
        dx = r * (dxh - xh * jnp.mean(dxh * xh, axis=-1, keepdims=True))
        dx_ref[...] = dx
        dxb_ref[...] = (0.5 * dx).astype(BF16)

    tile = pl.BlockSpec((TM, d), lambda i: (i, 0))
    return pl.pallas_call(
        body, name=name, grid=(t // TM,),
        in_specs=[tile, pl.BlockSpec((1, d), lambda i: (0, 0)), tile],
        out_specs=[tile, pl.BlockSpec((1, d), lambda i: (0, 0)), pl.BlockSpec((1, 1), lambda i: (0, 0)), tile],
        out_shape=[jax.ShapeDtypeStruct((t, d), F32), jax.ShapeDtypeStruct((1, d), F32), jax.ShapeDtypeStruct((1, 1), F32),
                   jax.ShapeDtypeStruct((t, d), BF16)],
        compiler_params=_cparams(1),
    )(x3, nf, tgt)


def _coords():
    return lax.axis_index("x"), lax.axis_index("y"), lax.axis_index("c")


def _window(ref, kind, idx, size):
    if kind == "col":
        return ref.at[:, pl.ds(pl.multiple_of(idx * size, LANE), size)]
    return ref.at[pl.ds(pl.multiple_of(idx * size, 8), size), :]


def _run_exchange(ex, name):
    n_in = len(ex.inputs)

    def body(*refs):
        ins, outs, sems = refs[:n_in], refs[n_in:n_in + len(ex.out_shape)], refs[n_in + len(ex.out_shape):]
        ex.start(ins, outs, sems)
        if ex.mid is not None:
            ex.mid(ins, outs, sems)
        ex.finish(ins, outs, sems)

    return pl.pallas_call(body, name=name, in_specs=[ANY] * n_in, out_specs=[ANY] * len(ex.out_shape),
                          out_shape=ex.out_shape, scratch_shapes=ex.scratch)(*ex.inputs)


def _join(exchanges):
    bounds = []
    i0 = o0 = s0 = 0
    for ex in exchanges:
        bounds.append((i0, o0, s0))
        i0, o0, s0 = i0 + len(ex.inputs), o0 + len(ex.out_shape), s0 + len(ex.scratch)

    def phase(which):
        def run(ins, outs, sems):
            for ex, (i, o, s) in zip(exchanges, bounds):
                fn = getattr(ex, which)
                if fn is not None:
                    fn(ins[i:i + len(ex.inputs)], outs[o:o + len(ex.out_shape)], sems[s:s + len(ex.scratch)])
        return run

    return _Exchange(sum((ex.inputs for ex in exchanges), []), sum((ex.out_shape for ex in exchanges), []),
                     sum((ex.scratch for ex in exchanges), []), phase("start"), phase("finish"),
                     phase("mid") if any(ex.mid is not None for ex in exchanges) else None)


def _gather_exchange(parts):
    n = len(parts)
    kinds = [kd for _, kd in parts]
    sizes = [a.shape[1] if kd == "col" else a.shape[0] for a, kd in parts]

    def plan(ins, outs, sems):
        send_sems, recv_sems, local_sems = sems
        x, y, c = _coords()
        me, sibling = (x, y, c), (x, y, 1 - c)
        chips = [(1 - x, y), (x, 1 - y), (1 - x, 1 - y)]

        def win(p, dev):
            return _window(outs[p], kinds[p], 4 * dev[0] + 2 * dev[1] + dev[2], sizes[p])

        def copy(p, k, block, to, src=None):
            return pltpu.make_async_remote_copy(
                src_ref=win(p, block) if src is None else src, dst_ref=win(p, block),
                send_sem=send_sems.at[p * 7 + k], recv_sem=recv_sems.at[p * 7 + k], device_id=to, device_id_type=MESH_ID)

        mine = [pltpu.make_async_copy(ins[p], win(p, me), local_sems.at[p]) for p in range(n)]
        first, arrived, passed, rest = [], [], [], []
        for p in range(n):
            first.append(copy(p, 0, me, sibling, src=ins[p]))
            first += [copy(p, 1 + q, me, (*chip, c), src=ins[p]) for q, chip in enumerate(chips)]
            rest.append(copy(p, 0, sibling, me))
            rest += [copy(p, 4 + q, (*chip, 1 - c), me) for q, chip in enumerate(chips)]
        for q, chip in enumerate(chips):
            for p in range(n):
                arrived.append(copy(p, 1 + q, (*chip, c), me))
                passed.append(copy(p, 4 + q, (*chip, c), sibling))
        return mine, first, arrived, passed, rest

    def start(ins, outs, sems):
        mine, first, _, _, _ = plan(ins, outs, sems)
        for cp in mine + first:
            cp.start()

    def mid(ins, outs, sems):
        _, _, arrived, passed, _ = plan(ins, outs, sems)
        for got, fwd in zip(arrived, passed):
            got.wait_recv()
            fwd.start()

    def finish(ins, outs, sems):
        mine, first, _, passed, rest = plan(ins, outs, sems)
        for cp in rest:
            cp.wait_recv()
        for cp in first + passed:
            cp.wait_send()
        for cp in mine:
            cp.wait()

    out_shape = [jax.ShapeDtypeStruct((a.shape[0], N_DEV * a.shape[1]) if kd == "col" else (N_DEV * a.shape[0], a.shape[1]),
                                      a.dtype) for a, kd in parts]
    scratch = [pltpu.SemaphoreType.DMA((7 * n,)), pltpu.SemaphoreType.DMA((7 * n,)), pltpu.SemaphoreType.DMA((n,))]
    return _Exchange([a for a, _ in parts], out_shape, scratch, start, finish, mid)


def _all_gather(parts, name):
    return _run_exchange(_gather_exchange(parts), name)


def _shard_shape(a, kd):
    return (a.shape[0], a.shape[1] // N_DEV) if kd == "col" else (a.shape[0] // N_DEV, a.shape[1])


def _symmetric_exchange(inputs, out_shape, n_copies, plan):
    def start(ins, outs, sems):
        for cp in plan(ins, outs, sems):
            cp.start()

    def finish(ins, outs, sems):
        copies = plan(ins, outs, sems)
        for cp in copies:
            cp.wait_recv()
        for cp in copies:
            cp.wait_send()

    scratch = [pltpu.SemaphoreType.DMA((n_copies,)), pltpu.SemaphoreType.DMA((n_copies,))]
    return _Exchange(inputs, out_shape, scratch, start, finish)


def _rs_pair_exchange(grads):
    n = len(grads)
    kinds = [kd for _, kd in grads]
    shapes = [_shard_shape(a, kd) for a, kd in grads]

    def plan(ins, outs, sems):
        send_sems, recv_sems = sems
        x, y, c = _coords()
        copies = []
        for p in range(n):
            size = shapes[p][1] if kinds[p] == "col" else shapes[p][0]
            for s in range(4):
                src = _window(ins[p], kinds[p], 2 * s + (1 - c), size)
                copies.append(pltpu.make_async_remote_copy(
                    src_ref=src, dst_ref=outs[p].at[s], send_sem=send_sems.at[4 * p + s], recv_sem=recv_sems.at[4 * p + s],
                    device_id=(x, y, 1 - c), device_id_type=MESH_ID))
        return copies

    return _symmetric_exchange([a for a, _ in grads], [jax.ShapeDtypeStruct((4,) + shapes[p], BF16) for p in range(n)],
                               4 * n, plan)


def _rs_chips_exchange(sums):
    n = len(sums)

    def plan(ins, outs, sems):
        send_sems, recv_sems = sems
        x, y, c = _coords()
        chips = [(1 - x, y), (x, 1 - y), (1 - x, 1 - y)]
        copies = []
        for p in range(n):
            for q, (cx, cy) in enumerate(chips):
                copies.append(pltpu.make_async_remote_copy(
                    src_ref=ins[p].at[2 * cx + cy], dst_ref=outs[p].at[q],
                    send_sem=send_sems.at[3 * p + q], recv_sem=recv_sems.at[3 * p + q],
                    device_id=(cx, cy, c), device_id_type=MESH_ID))
        return copies

    return _symmetric_exchange(list(sums), [jax.ShapeDtypeStruct((3,) + a.shape[1:], BF16) for a in sums], 3 * n, plan)


def _rs_pair(grads, name):
    return _run_exchange(_rs_pair_exchange(grads), name)


def _rs_chips(sums, name):
    return _run_exchange(_rs_chips_exchange(sums), name)


def _pair_sum(grad, kd, recv, core, name):
    _, r, cw = recv.shape
    tr = min(r, TM)

    def body(core_ref, g_ref, r_ref, o_ref):
        del core_ref
        o_ref[0] = (g_ref[...].astype(F32) + r_ref[0].astype(F32)).astype(BF16)

    if kd == "col":
        g_spec = pl.BlockSpec((tr, cw), lambda s, i, core_ref: (i, 2 * s + core_ref[0]))
    else:
        g_spec = pl.BlockSpec((tr, cw), lambda s, i, core_ref: ((2 * s + core_ref[0]) * (r // tr) + i, 0))
    grid_spec = pltpu.PrefetchScalarGridSpec(
        num_scalar_prefetch=1, grid=(4, r // tr),
        in_specs=[g_spec, pl.BlockSpec((1, tr, cw), lambda s, i, core_ref: (s, i, 0))],
        out_specs=pl.BlockSpec((1, tr, cw), lambda s, i, core_ref: (s, i, 0)))
    return pl.pallas_call(
        body, name=name, grid_spec=grid_spec, out_shape=jax.ShapeDtypeStruct(recv.shape, BF16),
        compiler_params=_cparams(2),
    )(core, grad, recv)


def _adam_math(w, g, m, v):
    m2 = B1 * m + (1.0 - B1) * g
    v2 = B2 * v + (1.0 - B2) * (g * g)
    m_hat = m2 / (1.0 - B1 ** STEP)
    v_hat = v2 / (1.0 - B2 ** STEP)
    delta = -LR * (m_hat / (jnp.sqrt(v_hat) + ADAM_EPS) + WD * w)
    return delta, m2, v2


def _chip_sum_adam(psum, recv, chip, w, m, v, name):
    r, cw = w.shape
    pc = psum.shape[2]
    tr = min(r, TM)

    def body(chip_ref, p_ref, r_ref, w_ref, m_ref, v_ref, g_ref, d_ref, m2_ref, v2_ref):
        del chip_ref
        g = p_ref[0].astype(F32) + r_ref[0].astype(F32) + r_ref[1].astype(F32) + r_ref[2].astype(F32)
        g = g[:, :cw]
        delta, m2, v2 = _adam_math(w_ref[...], g, m_ref[...], v_ref[...])
        g_ref[...] = g
        d_ref[...] = delta
        m2_ref[...] = m2
        v2_ref[...] = v2

    loc = pl.BlockSpec((tr, cw), lambda i, chip_ref: (i, 0))
    grid_spec = pltpu.PrefetchScalarGridSpec(
        num_scalar_prefetch=1, grid=(r // tr,),
        in_specs=[pl.BlockSpec((1, tr, pc), lambda i, chip_ref: (chip_ref[0], i, 0)),
                  pl.BlockSpec((3, tr, pc), lambda i, chip_ref: (0, i, 0)), loc, loc, loc],
        out_specs=[loc, loc, loc, loc])
    return pl.pallas_call(
        body, name=name, grid_spec=grid_spec, out_shape=[jax.ShapeDtypeStruct((r, cw), F32)] * 4,
        compiler_params=_cparams(1),
    )(chip, psum, recv, w, m, v)


def _small_allreduce_adam(partials, params, moms, vels, name, ride=None):
    n = len(partials)
    row0 = []
    rows = 0
    for a in partials:
        if a.shape[0] >= 8:
            rows = _pad_to(rows, 8)
        row0.append(rows)
        rows += a.shape[0]
    rows = _pad_to(rows, 8)
    width = max(a.shape[1] for a in partials)
    r_in = 0 if ride is None else len(ride.inputs)
    r_out = 0 if ride is None else len(ride.out_shape)

    def body(*refs):
        g_in = refs[:n]
        w_in, m_in, v_in = refs[n:2 * n], refs[2 * n:3 * n], refs[3 * n:4 * n]
        refs = refs[4 * n:]
        r_ins, refs = refs[:r_in], refs[r_in:]
        outs, refs = refs[:4 * n], refs[4 * n:]
        r_outs, refs = refs[:r_out], refs[r_out:]
        pair, chips, send_sems, recv_sems = refs[:4]
        if ride is not None:
            ride.start(r_ins, r_outs, refs[4:])
        x, y, c = _coords()
        chip = 2 * x + y
        pair[c] = jnp.zeros((rows, width), F32)
        for p in range(n):
            r, cw = partials[p].shape
            pair[c, row0[p]:row0[p] + r, 0:cw] = g_in[p][...]
        swap = pltpu.make_async_remote_copy(src_ref=pair.at[c], dst_ref=pair.at[c], send_sem=send_sems.at[0],
                                            recv_sem=recv_sems.at[0], device_id=(x, y, 1 - c), device_id_type=MESH_ID)
        swap.start()
        swap.wait_recv()
        swap.wait_send()
        chips[chip] = pair[0] + pair[1]
        copies = [pltpu.make_async_remote_copy(
            src_ref=chips.at[chip], dst_ref=chips.at[chip], send_sem=send_sems.at[1 + q], recv_sem=recv_sems.at[1 + q],
            device_id=(cx, cy, c), device_id_type=MESH_ID) for q, (cx, cy) in enumerate([(1 - x, y), (x, 1 - y), (1 - x, 1 - y)])]
        for cp in copies:
            cp.start()
        for cp in copies:
            cp.wait_recv()
        for cp in copies:
            cp.wait_send()
        for p in range(n):
            r, cw = partials[p].shape
            g = chips[0, row0[p]:row0[p] + r, 0:cw]
            for q in range(1, 4):
                g = g + chips[q, row0[p]:row0[p] + r, 0:cw]
            delta, m2, v2 = _adam_math(w_in[p][...], g, m_in[p][...], v_in[p][...])
            outs[4 * p][...] = g
            outs[4 * p + 1][...] = delta
            outs[4 * p + 2][...] = m2
            outs[4 * p + 3][...] = v2
        if ride is not None:
            ride.finish(r_ins, r_outs, refs[4:])

    out_shape = []
    for a in partials:
        out_shape += [jax.ShapeDtypeStruct(a.shape, F32)] * 4
    res = pl.pallas_call(
        body, name=name, in_specs=[VMEM_SPEC] * (4 * n) + [ANY] * r_in, out_specs=[VMEM_SPEC] * (4 * n) + [ANY] * r_out,
        out_shape=out_shape + ([] if ride is None else ride.out_shape),
        scratch_shapes=[pltpu.VMEM((2, rows, width), F32), pltpu.VMEM((4, rows, width), F32),
                        pltpu.SemaphoreType.DMA((4,)), pltpu.SemaphoreType.DMA((4,))] + ([] if ride is None else ride.scratch),
    )(*partials, *params, *moms, *vels, *([] if ride is None else ride.inputs))
    return res if ride is None else (res[:4 * n], res[4 * n:])


def _local_step(xf, tgt, nseq, seq, cols1_all, d1_all, later, small_w, core=None, trans=None, small_step=None):
    d = xf.shape[1]
    dist = core is not None
    n1, n2, gain, pool_w, pool_scale, n3, nf = small_w
    tf = 2 * cols1_all.shape[1] // N_DEV
    consts = _retention_constants()
    cos_t, sin_t = _rotary_tables(seq)
    wp_b = pool_w.astype(BF16)

    def pair_sums(grads, recv, names):
        return [_pair_sum(g, kd, r, core, "pair_sum_" + nm) for (g, kd), r, nm in zip(grads, recv, names)]

    if dist:
        (x1, h1, b1, sil1, dsil1), later = _ffn_fwd(xf, n1, cols1_all, 0, d1_all, "ffn1_fwd",
                                                   ride=_gather_exchange(later))
    else:
        x1, h1, b1, sil1, dsil1 = _ffn_fwd(xf, n1, cols1_all, 0, d1_all, "ffn1_fwd")
    cols2_all, d2_all, win_all, wout_all = later
    h2, qs, kr, vv, gg, uu = _mix_in(x1, n2, win_all, cos_t, sin_t, seq, "mix_in")
    fwd_mix = (qs, kr, vv, gg, uu, x1, consts, gain, wp_b, pool_scale, wout_all, nseq, seq, "mix_core_fwd")
    if dist:
        (x2, mix, oo, pooled, states), (win_t, wout_t) = _mix_core_fwd(*fwd_mix, ride=_gather_exchange(trans["mix"]))
        (x3, h3, b3, sil3, dsil3), (cols2_t, d2_t) = _ffn_fwd(x2, n3, cols2_all, 0, d2_all, "ffn2_fwd",
                                                             ride=_gather_exchange(trans["ffn2"]))
    else:
        x2, mix, oo, pooled, states = _mix_core_fwd(*fwd_mix)
        x3, h3, b3, sil3, dsil3 = _ffn_fwd(x2, n3, cols2_all, 0, d2_all, "ffn2_fwd")
        cols1_t = _transpose(cols1_all, "transpose_cols1")
        cols2_t = _transpose(cols2_all, "transpose_cols2")
        d1_t = _transpose(d1_all, "transpose_down1")
        d2_t = _transpose(d2_all, "transpose_down2")
        win_t = _transpose(win_all, "transpose_w_in")
        wout_t = _transpose(wout_all, "transpose_w_out")
    dx3, dnf, loss_part, dx3b = _loss_head(x3, nf, tgt, "loss_head")
    out = {}

    if dist:
        (da3, db3, g_wd2), (cols1_t, d1_t) = _ffn_bwd_act(dx3b, b3, sil3, dsil3, d2_t, "ffn2_bwd_act",
                                                          ride=_gather_exchange(trans["ffn1"]))
    else:
        da3, db3, g_wd2 = _ffn_bwd_act(dx3b, b3, sil3, dsil3, d2_t, "ffn2_bwd_act")
    names2 = ["ffn2_gate", "ffn2_up", "ffn2_down"]
    grads2 = [(_wgrad(da3, h3, 1.0, tf, d, "wgrad_gate2"), "row"), (_wgrad(db3, h3, 1.0, tf, d, "wgrad_up2"), "row"),
              (g_wd2, "row")]
    if dist:
        (dx2, dn3), recv2 = _ffn_bwd_in(da3, db3, dx3, x2, n3, cols2_t, 0, "ffn2_bwd_in", ride=_rs_pair_exchange(grads2))
        sums2 = pair_sums(grads2, recv2, names2)
        (dp, dx2b, dgain, dscale, dwp), crecv2 = _mix_core_bwd(
            dx2, qs, kr, vv, gg, oo, pooled, states, consts, gain, wp_b, pool_scale, wout_t, cos_t, sin_t, nseq, seq,
            "mix_core_bwd", ride=_rs_chips_exchange(sums2))
        out.update({nm: (s, r) for nm, s, r in zip(names2, sums2, crecv2)})
    else:
        dx2, dn3 = _ffn_bwd_in(da3, db3, dx3, x2, n3, cols2_t, 0, "ffn2_bwd_in")
        dp, dx2b, dgain, dscale, dwp = _mix_core_bwd(dx2, qs, kr, vv, gg, oo, pooled, states, consts, gain, wp_b,
                                                     pool_scale, wout_t, cos_t, sin_t, nseq, seq, "mix_core_bwd")
        out.update(dict(zip(names2, grads2)))

    names_m = ["w_in", "w_out"]
    grads_m = [(_wgrad(h2, dp, 1.0, d, d, "wgrad_in"), "col"), (_wgrad(mix, dx2b, 1.0, d, d, "wgrad_out"), "row")]
    if dist:
        (dx1, dn2, dx1b), recv_m = _mix_in_bwd(dp, dx2, x1, n2, win_t, "mix_in_bwd", ride=_rs_pair_exchange(grads_m))
        sums_m = pair_sums(grads_m, recv_m, names_m)
        (da1, db1, g_wd1), crecv_m = _ffn_bwd_act(dx1b, b1, sil1, dsil1, d1_t, "ffn1_bwd_act",
                                                  ride=_rs_chips_exchange(sums_m))
        out.update({nm: (s, r) for nm, s, r in zip(names_m, sums_m, crecv_m)})
    else:
        dx1, dn2, dx1b = _mix_in_bwd(dp, dx2, x1, n2, win_t, "mix_in_bwd")
        da1, db1, g_wd1 = _ffn_bwd_act(dx1b, b1, sil1, dsil1, d1_t, "ffn1_bwd_act")
        out.update(dict(zip(names_m, grads_m)))

    dx0, dn1 = _ffn_bwd_in(da1, db1, dx1, xf, n1, cols1_t, 0, "ffn1_bwd_in")
    small_parts = (dn1, dn2, dgain, dwp, dscale, dn3, dnf)
    g_down = (g_wd1, "row")
    if dist:
        g_gate, recv_d = _wgrad(da1, h1, 1.0, tf, d, "wgrad_gate1", ride=_rs_pair_exchange([g_down]))
        g_gate = (g_gate, "row")
        sum_d = pair_sums([g_down], recv_d, ["ffn1_down"])
        g_up, (crecv_d, recv_g) = _wgrad(db1, h1, 1.0, tf, d, "wgrad_up1",
                                         ride=_join([_rs_chips_exchange(sum_d), _rs_pair_exchange([g_gate])]))
        g_up = (g_up, "row")
        sum_g = pair_sums([g_gate], [recv_g], ["ffn1_gate"])
        small_out, (crecv_g, recv_u) = small_step(small_parts, _join([_rs_chips_exchange(sum_g), _rs_pair_exchange([g_up])]))
        sum_u = pair_sums([g_up], [recv_u], ["ffn1_up"])
        (crecv_u,) = _run_exchange(_rs_chips_exchange(sum_u), "rs_tail")
        out.update({"ffn1_gate": (sum_g[0], crecv_g), "ffn1_up": (sum_u[0], crecv_u), "ffn1_down": (sum_d[0], crecv_d)})
        return loss_part, dx0, out, small_out
    out.update({"ffn1_gate": (_wgrad(da1, h1, 1.0, tf, d, "wgrad_gate1"), "row"),
                "ffn1_up": (_wgrad(db1, h1, 1.0, tf, d, "wgrad_up1"), "row"), "ffn1_down": g_down})
    return loss_part, dx0, out, small_parts


def kernel(x, norm_ffn1, ffn1_gate, ffn1_up, ffn1_down, norm_mix, w_in, ret_gn_gain, pool_w, pool_scale, w_out, norm_ffn2, ffn2_gate, ffn2_up, ffn2_down, norm_final, loss_target, m_norm_ffn1, m_ffn1_gate, m_ffn1_up, m_ffn1_down, m_norm_mix, m_w_in, m_ret_gn_gain, m_pool_w, m_pool_scale, m_w_out, m_norm_ffn2, m_ffn2_gate, m_ffn2_up, m_ffn2_down, m_norm_final, v_norm_ffn1, v_ffn1_gate, v_ffn1_up, v_ffn1_down, v_norm_mix, v_w_in, v_ret_gn_gain, v_pool_w, v_pool_scale, v_w_out, v_norm_ffn2, v_ffn2_gate, v_ffn2_up, v_ffn2_down, v_norm_final):
    nseq, seq, d = x.shape
    t = nseq * seq
    f_loc = ffn1_gate.shape[2]
    f_pad = _pad_to(f_loc, LANE)
    xf = x.reshape(t, d)
    tgt = loss_target.reshape(t, d)
    core = lax.axis_index("c").astype(jnp.int32).reshape(1)
    chip = (2 * lax.axis_index("x") + lax.axis_index("y")).astype(jnp.int32).reshape(1)

    colp = lambda w: jnp.pad(w[0].astype(BF16), ((0, 0), (0, f_pad - f_loc)))
    rowp = lambda w: jnp.pad(w[0].astype(BF16), ((0, f_pad - f_loc), (0, 0)))
    cols1 = jnp.concatenate([colp(ffn1_gate), colp(ffn1_up)], axis=0)
    cols2 = jnp.concatenate([colp(ffn2_gate), colp(ffn2_up)], axis=0)
    cols1_all, d1_all = _all_gather([(cols1, "col"), (rowp(ffn1_down), "row")], "all_gather_ffn1")
    d1_loc, d2_loc, win_loc, wout_loc = rowp(ffn1_down), rowp(ffn2_down), w_in[0].astype(BF16), w_out[0].astype(BF16)
    later = [(cols2, "col"), (d2_loc, "row"), (win_loc, "col"), (wout_loc, "row")]
    trans = {"mix": [(win_loc.T, "row"), (wout_loc.T, "col")], "ffn2": [(cols2.T, "row"), (d2_loc.T, "col")],
             "ffn1": [(cols1.T, "row"), (d1_loc.T, "col")]}

    flat = lambda a: a.reshape(pool_w.size // d, d)
    params = [norm_ffn1, norm_mix, ret_gn_gain, flat(pool_w), pool_scale, norm_ffn2, norm_final.reshape(1, d)]
    moms = [m_norm_ffn1, m_norm_mix, m_ret_gn_gain, flat(m_pool_w), m_pool_scale, m_norm_ffn2, m_norm_final.reshape(1, d)]
    vels = [v_norm_ffn1, v_norm_mix, v_ret_gn_gain, flat(v_pool_w), v_pool_scale, v_norm_ffn2, v_norm_final.reshape(1, d)]

    def small_step(parts, ride):
        dn1, dn2, dgain, dwp, dscale, dn3, dnf = parts
        return _small_allreduce_adam([dn1, dn2, dgain, flat(dwp), dscale, dn3, dnf], params, moms, vels,
                                     "small_allreduce_adam", ride)

    small_w = (norm_ffn1, norm_mix, ret_gn_gain, pool_w[0], pool_scale, norm_ffn2, norm_final.reshape(1, d))
    loss_part, dx0, reduced, small_out = _local_step(xf, tgt, nseq, seq, cols1_all, d1_all, later, small_w, core, trans,
                                                     small_step)

    local = {"ffn1_gate": (ffn1_gate, m_ffn1_gate, v_ffn1_gate), "ffn1_up": (ffn1_up, m_ffn1_up, v_ffn1_up),
             "ffn1_down": (ffn1_down, m_ffn1_down, v_ffn1_down), "w_in": (w_in, m_w_in, v_w_in),
             "w_out": (w_out, m_w_out, v_w_out), "ffn2_gate": (ffn2_gate, m_ffn2_gate, v_ffn2_gate),
             "ffn2_up": (ffn2_up, m_ffn2_up, v_ffn2_up), "ffn2_down": (ffn2_down, m_ffn2_down, v_ffn2_down)}
    big = {}
    for nm, (w, m, v) in local.items():
        ps, rcv = reduced[nm]
        flip = nm.endswith("gate") or nm.endswith("up")
        view = (lambda a: a[0].T) if flip else (lambda a: a[0])
        res = _chip_sum_adam(ps, rcv, chip, view(w), view(m), view(v), "adam_" + nm)
        big[nm] = tuple((a.T if flip else a)[None] for a in res)

    small_names = ["norm_ffn1", "norm_mix", "ret_gn_gain", "pool_w", "pool_scale", "norm_ffn2", "norm_final"]
    shapes = [norm_ffn1.shape, norm_mix.shape, ret_gn_gain.shape, pool_w.shape, pool_scale.shape, norm_ffn2.shape,
              norm_final.shape]
    small = {nm: tuple(small_out[4 * p + q].reshape(shapes[p]) for q in range(4)) for p, nm in enumerate(small_names)}

    loss = lax.psum(loss_part[0, 0], ("x", "y", "c"))
    order = ["norm_ffn1", "ffn1_gate", "ffn1_up", "ffn1_down", "norm_mix", "w_in", "ret_gn_gain", "pool_w", "pool_scale",
             "w_out", "norm_ffn2", "ffn2_gate", "ffn2_up", "ffn2_down", "norm_final"]
    both = {**big, **small}
    outs = [loss, dx0.reshape(nseq, seq, d)]
    for q in range(4):
        outs += [both[nm][q] for nm in order]
    return tuple(outs)
```

```python
import functools

import numpy as np
import jax
import jax.numpy as jnp
from jax import lax
from jax.experimental import pallas as pl
from jax.experimental.pallas import tpu as pltpu

F32, BF16 = jnp.float32, jnp.bfloat16
MESH_ID = pl.DeviceIdType.MESH
ANY = pl.BlockSpec(memory_space=pl.ANY)
VMEM_SPEC = pl.BlockSpec(memory_space=pltpu.VMEM)

N_DEV = 8
RMS_EPS = 1e-6
GN_EPS = 1e-5
HEADS, DK, DV = 4, 64, 128
QK_W, V_W, POOL_W = HEADS * DK, HEADS * DV, 512
WINDOWS = (2, 4, 8, 16)
GC = POOL_W // len(WINDOWS)
CHUNK = 64
BLK = 4 * CHUNK
HALO = 16
ROPE_BASE = 10000.0
LR, B1, B2, ADAM_EPS, WD, STEP = 0.001, 0.9, 0.999, 1e-08, 0.01, 10
LANE = 128
TM = 512
FFN_TM = 1024
WGRAD_TT = 4096
VMEM_LIMIT = 56 * 1024 * 1024


def _cparams(n_axes):
    return pltpu.CompilerParams(dimension_semantics=("arbitrary",) * n_axes, vmem_limit_bytes=VMEM_LIMIT)


class _Exchange:
    def __init__(self, inputs, out_shape, scratch, start, finish, mid=None):
        self.inputs, self.out_shape, self.scratch = list(inputs), list(out_shape), list(scratch)
        self.start, self.finish, self.mid = start, finish, mid


def _pallas(body, name, grid, in_specs, out_specs, out_shape, scratch_shapes, args, ride=None):
    n_axes = len(grid)
    if ride is None:
        return pl.pallas_call(body, name=name, grid=grid, in_specs=in_specs, out_specs=out_specs, out_shape=out_shape,
                              scratch_shapes=scratch_shapes, compiler_params=_cparams(n_axes))(*args)
    n_in, n_out, n_scr = len(in_specs), len(out_specs), len(scratch_shapes)
    r_in, r_out = len(ride.inputs), len(ride.out_shape)

    def hosted(*refs):
        ins, refs = refs[:n_in], refs[n_in:]
        r_ins, refs = refs[:r_in], refs[r_in:]
        outs, refs = refs[:n_out], refs[n_out:]
        r_outs, refs = refs[:r_out], refs[r_out:]
        scr, sems = refs[:n_scr], refs[n_scr:]
        ids = [pl.program_id(a) for a in range(n_axes)]
        first, last, inner0 = ids[0] == 0, ids[0] == grid[0] - 1, None
        for a in range(1, n_axes):
            first = first & (ids[a] == 0)
            last = last & (ids[a] == grid[a] - 1)
            inner0 = (ids[a] == 0) if inner0 is None else inner0 & (ids[a] == 0)

        @pl.when(first)
        def _():
            ride.start(r_ins, r_outs, sems)

        if ride.mid is not None:
            at_mid = ids[0] == grid[0] - 1
            if inner0 is not None:
                at_mid = at_mid & inner0

            @pl.when(at_mid)
            def _():
                ride.mid(r_ins, r_outs, sems)

        body(*ins, *outs, *scr)

        @pl.when(last)
        def _():
            ride.finish(r_ins, r_outs, sems)

    res = pl.pallas_call(
        hosted, name=name, grid=grid, in_specs=list(in_specs) + [ANY] * r_in, out_specs=list(out_specs) + [ANY] * r_out,
        out_shape=list(out_shape) + ride.out_shape, scratch_shapes=list(scratch_shapes) + ride.scratch,
        compiler_params=_cparams(n_axes))(*args, *ride.inputs)
    return res[:n_out], res[n_out:]


def _dot(a, b):
    return jnp.dot(a, b, preferred_element_type=F32)


def _dot_nt(a, b):
    return lax.dot_general(a, b, (((1,), (1,)), ((), ())), preferred_element_type=F32)


def _dot_tn(a, b):
    return lax.dot_general(a, b, (((0,), (0,)), ((), ())), preferred_element_type=F32)


def _sigmoid(x):
    return 0.5 * jnp.tanh(0.5 * x) + 0.5


def _transpose(w, name):
    r, c = w.shape
    tb = 512

    def body(x_ref, o_ref):
        o_ref[...] = x_ref[...].T

    return pl.pallas_call(
        body, name=name, grid=(r // tb, c // tb),
        in_specs=[pl.BlockSpec((tb, tb), lambda i, j: (i, j))],
        out_specs=pl.BlockSpec((tb, tb), lambda i, j: (j, i)),
        out_shape=jax.ShapeDtypeStruct((c, r), w.dtype),
        compiler_params=_cparams(2),
    )(w)


def _pad_to(n, m):
    return (n + m - 1) // m * m


def _retention_constants():
    gamma = (1.0 - 2.0 ** (-5.0 - np.arange(HEADS, dtype=np.float32))).astype(np.float32)
    log_g = np.log(gamma).astype(np.float32)
    i = np.arange(BLK)
    diff = (i[:, None] - i[None, :]).astype(np.float32)
    same = (i[:, None] // CHUNK) == (i[None, :] // CHUNK)
    earlier = (i[None, :] // CHUNK) < (i[:, None] // CHUNK)
    mask = np.zeros((HEADS, BLK, BLK), np.float32)
    for h in range(HEADS):
        dec_abs = np.exp(log_g[h] * np.abs(diff)).astype(np.float32)
        dec = np.exp(log_g[h] * diff * earlier).astype(np.float32)
        mask[h] = np.where(same, dec_abs, np.where(earlier, dec, 0.0))
    dq = np.zeros((BLK, V_W), np.float32)
    dk = np.zeros((BLK, QK_W), np.float32)
    gbd = np.zeros((QK_W, V_W), np.float32)
    for h in range(HEADS):
        dq[:, h * DV:(h + 1) * DV] = np.exp(log_g[h] * (i + 1.0)).astype(np.float32)[:, None]
        dk[:, h * DK:(h + 1) * DK] = np.exp(log_g[h] * (BLK - 1.0 - i)).astype(np.float32)[:, None]
        gbd[h * DK:(h + 1) * DK, h * DV:(h + 1) * DV] = np.exp(log_g[h] * np.float32(BLK))
    bd = (gbd > 0).astype(np.float32)
    return jnp.asarray(mask), jnp.asarray(dq), jnp.asarray(dk), jnp.asarray(gbd), jnp.asarray(bd)


def _rotary_tables(seq):
    half = DK // 2
    freqs = ROPE_BASE ** (-jnp.arange(half, dtype=F32) * 2.0 / DK)
    ang = jnp.arange(seq, dtype=F32)[:, None] * freqs[None, :]
    cos, sin = jnp.cos(ang), jnp.sin(ang)
    cos_t = jnp.tile(jnp.concatenate([cos, cos], axis=1), (1, HEADS))
    sin_t = jnp.tile(jnp.concatenate([-sin, sin], axis=1), (1, HEADS))
    return cos_t, sin_t


def _swap_halves(x):
    lane = lax.broadcasted_iota(jnp.int32, (1, QK_W), 1)
    first = (lane & (DK - 1)) < DK // 2
    return jnp.where(first, pltpu.roll(x, QK_W - DK // 2, 1), pltpu.roll(x, DK // 2, 1))


def _head_mask(h):
    lane = lax.broadcasted_iota(jnp.int32, (1, QK_W), 1)
    return (lane >= h * DK) & (lane < (h + 1) * DK)


def _ffn_act(x, n, gate, up, name, ride=None):
    t, d = x.shape
    (wg, gq), (wu, uq) = gate, up
    fp = wg.shape[1]
    tm = min(t, FFN_TM)
    tf = 2 * fp // N_DEV
    nj = fp // tf

    def body(x_ref, n_ref, wg_ref, wu_ref, h_ref, b_ref, sil_ref, dsil_ref, s_ref):
        @pl.when(pl.program_id(1) == 0)
        def _():
            xv = x_ref[...]
            r = lax.rsqrt(jnp.mean(xv * xv, axis=-1, keepdims=True) + RMS_EPS)
            h_ref[...] = (xv * r * n_ref[...]).astype(BF16)

        h = h_ref[...]
        a = _dot(h, wg_ref[...])
        b = _dot(h, wu_ref[...])
        sg = _sigmoid(a)
        sil = a * sg
        b_ref[...] = b.astype(BF16)
        sil_ref[...] = sil.astype(BF16)
        dsil_ref[...] = (sg + sil * (1.0 - sg)).astype(BF16)
        s_ref[...] = (sil * b).astype(BF16)

    act = pl.BlockSpec((tm, tf), lambda i, j: (i, j))
    return _pallas(
        body, name, (t // tm, nj),
        [pl.BlockSpec((tm, d), lambda i, j: (i, 0)), pl.BlockSpec((1, d), lambda i, j: (0, 0)),
         pl.BlockSpec((d, tf), lambda i, j: (gq, j)), pl.BlockSpec((d, tf), lambda i, j: (uq, j))],
        [pl.BlockSpec((tm, d), lambda i, j: (i, 0)), act, act, act, act],
        [jax.ShapeDtypeStruct((t, d), BF16)] + [jax.ShapeDtypeStruct((t, fp), BF16)] * 4,
        [], (x, n, wg, wu), ride)


def _ffn_down(s, x, wd, name, ride=None):
    t, d = x.shape
    fp = wd.shape[0]
    tm = min(t, FFN_TM)

    def body(s_ref, x_ref, wd_ref, xo_ref):
        xo_ref[...] = x_ref[...] + 0.5 * _dot(s_ref[...], wd_ref[...])

    row = pl.BlockSpec((tm, d), lambda i: (i, 0))
    res = _pallas(body, name, (t // tm,), [pl.BlockSpec((tm, fp), lambda i: (i, 0)), row, pl.BlockSpec((fp, d), lambda i: (0, 0))],
                  [row], [jax.ShapeDtypeStruct((t, d), F32)], [], (s, x, wd), ride)
    return res[0] if ride is None else (res[0][0], res[1])


def _ffn_bwd_act(dxob, b, sil, dsil, wd_t, name, ride=None):
    t, d = dxob.shape
    fp = wd_t.shape[1]
    tm = min(t, FFN_TM)
    tf = 2 * fp // N_DEV
    ni = t // tm

    def body(dx_ref, b_ref, sil_ref, dsil_ref, wd_ref, da_ref, db_ref, gd_ref, acc_ref):
        i = pl.program_id(1)

        @pl.when(i == 0)
        def _():
            acc_ref[...] = jnp.zeros_like(acc_ref)

        dxv = dx_ref[...]
        bv, sv = b_ref[...].astype(F32), sil_ref[...].astype(F32)
        ds = _dot(dxv, wd_ref[...])
        da_ref[...] = (ds * bv * dsil_ref[...].astype(F32)).astype(BF16)
        db_ref[...] = (ds * sv).astype(BF16)
        acc_ref[...] += _dot_tn((sv * bv).astype(BF16), dxv)

        @pl.when(i == ni - 1)
        def _():
            gd_ref[...] = acc_ref[...].astype(BF16)

    act = pl.BlockSpec((tm, tf), lambda c, i: (i, c))
    return _pallas(
        body, name, (fp // tf, ni),
        [pl.BlockSpec((tm, d), lambda c, i: (i, 0)), act, act, act, pl.BlockSpec((d, tf), lambda c, i: (0, c))],
        [act, act, pl.BlockSpec((tf, d), lambda c, i: (c, 0))],
        [jax.ShapeDtypeStruct((t, fp), BF16), jax.ShapeDtypeStruct((t, fp), BF16), jax.ShapeDtypeStruct((fp, d), BF16)],
        [pltpu.VMEM((tf, d), F32)], (dxob, b, sil, dsil, wd_t), ride)


def _ffn_bwd_in(da, db, dxo, x, n, cols_t, gq, name, ride=None):
    t, d = x.shape
    fp = cols_t.shape[0]
    tm = min(t, FFN_TM)
    tf = 2 * fp // N_DEV
    nj = fp // tf

    def body(da_ref, db_ref, dxo_ref, x_ref, n_ref, wg_ref, wu_ref, dx_ref, dn_ref, acc_ref):
        i, j = pl.program_id(0), pl.program_id(1)

        @pl.when((i == 0) & (j == 0))
        def _():
            dn_ref[...] = jnp.zeros_like(dn_ref)

        @pl.when(j == 0)
        def _():
            acc_ref[...] = jnp.zeros_like(acc_ref)

        acc_ref[...] += _dot(da_ref[...], wg_ref[...]) + _dot(db_ref[...], wu_ref[...])

        @pl.when(j == nj - 1)
        def _():
            xv = x_ref[...]
            r = lax.rsqrt(jnp.mean(xv * xv, axis=-1, keepdims=True) + RMS_EPS)
            xh = xv * r
            dh = acc_ref[...]
            dn_ref[...] += jnp.sum(dh * xh, axis=0, keepdims=True)
            dhn = dh * n_ref[...]
            dx_ref[...] = dxo_ref[...] + r * (dhn - xh * jnp.mean(dhn * xh, axis=-1, keepdims=True))

    act = pl.BlockSpec((tm, tf), lambda i, j: (i, j))
    row = pl.BlockSpec((tm, d), lambda i, j: (i, 0))
    return _pallas(
        body, name, (t // tm, nj),
        [act, act, row, row, pl.BlockSpec((1, d), lambda i, j: (0, 0)),
         pl.BlockSpec((tf, d), lambda i, j: (j, gq)), pl.BlockSpec((tf, d), lambda i, j: (j, gq + 1))],
        [row, pl.BlockSpec((1, d), lambda i, j: (0, 0))],
        [jax.ShapeDtypeStruct((t, d), F32), jax.ShapeDtypeStruct((1, d), F32)],
        [pltpu.VMEM((tm, d), F32)], (da, db, dxo, x, n, cols_t, cols_t), ride)


def _wgrad(a, b, scale, tk, tn, name, ride=None):
    t, k = a.shape
    n = b.shape[1]
    tt = min(t, WGRAD_TT)
    nt = t // tt

    def body(a_ref, b_ref, o_ref, acc_ref):
        s = pl.program_id(2)

        @pl.when(s == 0)
        def _():
            acc_ref[...] = jnp.zeros_like(acc_ref)

        acc_ref[...] += _dot_tn(a_ref[...], b_ref[...])

        @pl.when(s == nt - 1)
        def _():
            o_ref[...] = (scale * acc_ref[...]).astype(BF16)

    res = _pallas(
        body, name, (k // tk, n // tn, nt),
        [pl.BlockSpec((tt, tk), lambda p, q, s: (s, p)), pl.BlockSpec((tt, tn), lambda p, q, s: (s, q))],
        [pl.BlockSpec((tk, tn), lambda p, q, s: (p, q))], [jax.ShapeDtypeStruct((k, n), BF16)],
        [pltpu.VMEM((tk, tn), F32)], (a, b), ride)
    return res[0] if ride is None else (res[0][0], res[1])


def _mix_in(x, n, w_in, cos_t, sin_t, seq, name):
    t, d = x.shape
    per_seq = seq // TM

    def body(x_ref, n_ref, w_ref, c_ref, s_ref, h_ref, q_ref, k_ref, v_ref, g_ref, u_ref):
        xv = x_ref[...]
        r = lax.rsqrt(jnp.mean(xv * xv, axis=-1, keepdims=True) + RMS_EPS)
        h = (xv * r * n_ref[...]).astype(BF16)
        h_ref[...] = h
        p = _dot(h, w_ref[...])
        c, s = c_ref[...], s_ref[...]
        q = p[:, :QK_W]
        k = p[:, QK_W:2 * QK_W]
        q_ref[...] = ((q * c + _swap_halves(q) * s) * (DK ** -0.5)).astype(BF16)
        k_ref[...] = (k * c + _swap_halves(k) * s).astype(BF16)
        v_ref[...] = p[:, 2 * QK_W:2 * QK_W + V_W].astype(BF16)
        g_ref[...] = p[:, 2 * QK_W + V_W:2 * QK_W + 2 * V_W]
        u_ref[...] = p[:, 2 * QK_W + 2 * V_W:]

    tile = lambda w: pl.BlockSpec((TM, w), lambda i: (i, 0))
    return pl.pallas_call(
        body, name=name, grid=(t // TM,),
        in_specs=[tile(d), pl.BlockSpec((1, d), lambda i: (0, 0)), pl.BlockSpec(w_in.shape, lambda i: (0, 0)),
                  pl.BlockSpec((TM, QK_W), lambda i: (i % per_seq, 0)), pl.BlockSpec((TM, QK_W), lambda i: (i % per_seq, 0))],
        out_specs=[tile(d), tile(QK_W), tile(QK_W), tile(V_W), tile(V_W), tile(POOL_W)],
        out_shape=[jax.ShapeDtypeStruct((t, d), BF16), jax.ShapeDtypeStruct((t, QK_W), BF16),
                   jax.ShapeDtypeStruct((t, QK_W), BF16), jax.ShapeDtypeStruct((t, V_W), BF16),
                   jax.ShapeDtypeStruct((t, V_W), F32), jax.ShapeDtypeStruct((t, POOL_W), F32)],
        compiler_params=_cparams(1),
    )(x, n, w_in, cos_t, sin_t)


def _mix_in_bwd(dp, dx2, x1, n, w_in_t, name, ride=None):
    t, d = x1.shape

    def body(dp_ref, dx2_ref, x_ref, n_ref, w_ref, dx_ref, dn_ref, dxb_ref):
        @pl.when(pl.program_id(0) == 0)
        def _():
            dn_ref[...] = jnp.zeros_like(dn_ref)

        dh = _dot(dp_ref[...], w_ref[...])
        xv = x_ref[...]
        r = lax.rsqrt(jnp.mean(xv * xv, axis=-1, keepdims=True) + RMS_EPS)
        xh = xv * r
        dn_ref[...] += jnp.sum(dh * xh, axis=0, keepdims=True)
        dhn = dh * n_ref[...]
        dx = dx2_ref[...] + r * (dhn - xh * jnp.mean(dhn * xh, axis=-1, keepdims=True))
        dx_ref[...] = dx
        dxb_ref[...] = (0.5 * dx).astype(BF16)

    tile = lambda w: pl.BlockSpec((TM, w), lambda i: (i, 0))
    return _pallas(
        body, name, (t // TM,),
        [tile(dp.shape[1]), tile(d), tile(d), pl.BlockSpec((1, d), lambda i: (0, 0)),
         pl.BlockSpec(w_in_t.shape, lambda i: (0, 0))],
        [tile(d), pl.BlockSpec((1, d), lambda i: (0, 0)), tile(d)],
        [jax.ShapeDtypeStruct((t, d), F32), jax.ShapeDtypeStruct((1, d), F32), jax.ShapeDtypeStruct((t, d), BF16)],
        [], (dp, dx2, x1, n, w_in_t), ride)


def _group_norm(o):
    parts, rstds = [], []
    for h in range(HEADS):
        oh = o[:, h * DV:(h + 1) * DV]
        dlt = oh - jnp.mean(oh, axis=-1, keepdims=True)
        rstd = lax.rsqrt(jnp.mean(dlt * dlt, axis=-1, keepdims=True) + GN_EPS)
        parts.append(dlt * rstd)
        rstds.append(rstd)
    return jnp.concatenate(parts, axis=1), rstds


def _mix_core_fwd(qs, k, v, g, u, x1, consts, gain, wp, scale, w_out, nseq, seq, name, ride=None):
    t, d = x1.shape
    nblk = seq // BLK
    mask, dq, dk, gbd, bd = consts

    def body(q_ref, k_ref, v_ref, g_ref, u_ref, x1_ref, m_ref, dq_ref, dk_ref, gbd_ref, bd_ref, gain_ref, wp_ref,
             sc_ref, wo_ref, x2_ref, mix_ref, o_ref, pooled_ref, st_ref, state, halo):
        j = pl.program_id(1)

        @pl.when(j == 0)
        def _():
            state[...] = jnp.zeros_like(state)
            halo[...] = jnp.zeros_like(halo)

        qv, kv, vv = q_ref[...], k_ref[...], v_ref[...]
        st = state[...]
        st_ref[0] = st
        cross = _dot(qv, st.astype(BF16)) * dq_ref[...]
        outs = []
        for h in range(HEADS):
            qh = jnp.where(_head_mask(h), qv, jnp.zeros_like(qv))
            am = (_dot_nt(qh, kv) * m_ref[h]).astype(BF16)
            outs.append(_dot(am, vv[:, h * DV:(h + 1) * DV]))
        o = jnp.concatenate(outs, axis=1) + cross
        o_ref[...] = o
        kd = (kv.astype(F32) * dk_ref[...]).astype(BF16)
        state[...] = gbd_ref[...] * st + _dot_tn(kd, vv) * bd_ref[...]

        gv = g_ref[...]
        nrm, _ = _group_norm(o)
        ret = (gv * _sigmoid(gv)) * (nrm * gain_ref[...])

        uv = u_ref[...]
        c = jnp.concatenate([halo[...], uv], axis=0)
        halo[...] = uv[BLK - HALO:, :]
        pos = j * BLK + lax.broadcasted_iota(jnp.int32, (BLK, 1), 0)
        parts = []
        for gi, w in enumerate(WINDOWS):
            c = c + pltpu.roll(c, w // 2, 0)
            cnt = jnp.minimum(pos + 1, w).astype(F32)
            parts.append(c[HALO:, :GC] / cnt)
            if gi + 1 < len(WINDOWS):
                c = c[:, GC:]
        pooled = (jnp.concatenate(parts, axis=1) - uv).astype(BF16)
        pooled_ref[...] = pooled
        z = jnp.concatenate([_dot(pooled[:, gi * GC:(gi + 1) * GC], wp_ref[gi]) for gi in range(len(WINDOWS))], axis=1)
        mix = jnp.concatenate([ret, z * sc_ref[...]], axis=1).astype(BF16)
        mix_ref[...] = mix
        x2_ref[...] = x1_ref[...] + _dot(mix, wo_ref[...])

    blk = lambda w: pl.BlockSpec((BLK, w), lambda i, j: (i * nblk + j, 0))
    full = lambda a: pl.BlockSpec(a.shape, lambda i, j: (0,) * a.ndim)
    return _pallas(
        body, name, (nseq, nblk),
        [blk(QK_W), blk(QK_W), blk(V_W), blk(V_W), blk(POOL_W), blk(d),
         full(mask), full(dq), full(dk), full(gbd), full(bd), full(gain), full(wp), full(scale), full(w_out)],
        [blk(d), blk(d), blk(V_W), blk(POOL_W), pl.BlockSpec((1, QK_W, V_W), lambda i, j: (i * nblk + j, 0, 0))],
        [jax.ShapeDtypeStruct((t, d), F32), jax.ShapeDtypeStruct((t, d), BF16),
         jax.ShapeDtypeStruct((t, V_W), F32), jax.ShapeDtypeStruct((t, POOL_W), BF16),
         jax.ShapeDtypeStruct((nseq * nblk, QK_W, V_W), F32)],
        [pltpu.VMEM((QK_W, V_W), F32), pltpu.VMEM((HALO, POOL_W), F32)],
        (qs, k, v, g, u, x1, mask, dq, dk, gbd, bd, gain, wp, scale, w_out), ride)


def _mix_core_bwd(dx2, qs, k, v, g, o, pooled, st, consts, gain, wp, scale, w_out, cos_t, sin_t, nseq, seq, name,
                  ride=None):
    t, d = dx2.shape
    nblk = seq // BLK
    mask, dq, dk, gbd, bd = consts
    n_win = len(WINDOWS)

    def body(dx2_ref, q_ref, k_ref, v_ref, g_ref, o_ref, pooled_ref, st_ref, m_ref, dq_ref, dk_ref, gbd_ref, bd_ref,
             gain_ref, wp_ref, sc_ref, wo_ref, c_ref, s_ref,
             dp_ref, dx2b_ref, dgain_ref, dscale_ref, dwp_ref, rstate, carry):
        i, j = pl.program_id(0), pl.program_id(1)

        @pl.when((i == 0) & (j == 0))
        def _():
            dgain_ref[...] = jnp.zeros_like(dgain_ref)
            dscale_ref[...] = jnp.zeros_like(dscale_ref)
            dwp_ref[...] = jnp.zeros_like(dwp_ref)

        @pl.when(j == 0)
        def _():
            rstate[...] = jnp.zeros_like(rstate)
            carry[...] = jnp.zeros_like(carry)

        dx2b = dx2_ref[...].astype(BF16)
        dx2b_ref[...] = dx2b
        dmix = _dot(dx2b, wo_ref[...])
        dret, dpool = dmix[:, :V_W], dmix[:, V_W:]

        gv, ov, gain_v = g_ref[...], o_ref[...], gain_ref[...]
        sg = _sigmoid(gv)
        sil = gv * sg
        nrm, rstds = _group_norm(ov)
        dg = dret * (nrm * gain_v) * (sg * (1.0 + gv * (1.0 - sg)))
        dgn = dret * sil
        dgain_ref[...] += jnp.sum(dgn * nrm, axis=0, keepdims=True)
        dnrm = dgn * gain_v
        do_parts = []
        for h in range(HEADS):
            dn_h = dnrm[:, h * DV:(h + 1) * DV]
            n_h = nrm[:, h * DV:(h + 1) * DV]
            do_parts.append(rstds[h] * (dn_h - jnp.mean(dn_h, axis=-1, keepdims=True)
                                        - n_h * jnp.mean(dn_h * n_h, axis=-1, keepdims=True)))
        do = jnp.concatenate(do_parts, axis=1)
        dob = do.astype(BF16)

        qv, kv, vv = q_ref[...], k_ref[...], v_ref[...]
        stb = st_ref[0].astype(BF16)
        rs = rstate[...]
        rsb = rs.astype(BF16)
        dod = (do * dq_ref[...]).astype(BF16)
        dqs = _dot_nt(dod, stb)
        dst = _dot_tn(qv, dod) * bd_ref[...]
        dkf = dk_ref[...]
        kd = (kv.astype(F32) * dkf).astype(BF16)
        dks = _dot_nt(vv, rsb) * dkf
        dvs = _dot(kd, rsb)
        dv_parts = []
        for h in range(HEADS):
            hm = _head_mask(h)
            qh = jnp.where(hm, qv, jnp.zeros_like(qv))
            mh = m_ref[h]
            am = (_dot_nt(qh, kv) * mh).astype(BF16)
            dpm = (_dot_nt(dob[:, h * DV:(h + 1) * DV], vv[:, h * DV:(h + 1) * DV]) * mh).astype(BF16)
            dqs = dqs + jnp.where(hm, _dot(dpm, kv), 0.0)
            dks = dks + jnp.where(hm, _dot_tn(dpm, qv), 0.0)
            dv_parts.append(_dot_tn(am, dob[:, h * DV:(h + 1) * DV]))
        dvs = dvs + jnp.concatenate(dv_parts, axis=1)
        rstate[...] = dst + gbd_ref[...] * rs

        cv, sv = c_ref[...], s_ref[...]
        dqr = dqs * (DK ** -0.5)
        dq_pre = dqr * cv + _swap_halves(dqr * sv)
        dk_pre = dks * cv + _swap_halves(dks * sv)

        pv = pooled_ref[...]
        sc = sc_ref[...]
        dzb = (dpool * sc).astype(BF16)
        z_parts, dpo_parts = [], []
        for gi in range(n_win):
            p_g = pv[:, gi * GC:(gi + 1) * GC]
            dz_g = dzb[:, gi * GC:(gi + 1) * GC]
            z_parts.append(_dot(p_g, wp_ref[gi]))
            dwp_ref[gi] += _dot_tn(p_g, dz_g)
            dpo_parts.append(_dot_nt(dz_g, wp_ref[gi]))
        dscale_ref[...] += jnp.sum(dpool * jnp.concatenate(z_parts, axis=1), axis=0, keepdims=True)
        dpo = jnp.concatenate(dpo_parts, axis=1)
        pos = (nblk - 1 - j) * BLK + lax.broadcasted_iota(jnp.int32, (BLK, 1), 0)
        e = jnp.concatenate(
            [dpo[:, gi * GC:(gi + 1) * GC] / jnp.minimum(pos + 1, w).astype(F32) for gi, w in enumerate(WINDOWS)], axis=1)
        c = jnp.concatenate([e, carry[...]], axis=0)
        carry[...] = e[:HALO, :]
        rows = BLK + HALO
        lead = []
        for gi, w in enumerate(WINDOWS):
            c = c + pltpu.roll(c, rows - w // 2, 0)
            lead.append(c[:BLK, :GC])
            if gi + 1 < n_win:
                c = c[:, GC:]
        du = jnp.concatenate(lead, axis=1) - dpo

        dp_ref[:, 0:QK_W] = dq_pre.astype(BF16)
        dp_ref[:, QK_W:2 * QK_W] = dk_pre.astype(BF16)
        dp_ref[:, 2 * QK_W:2 * QK_W + V_W] = dvs.astype(BF16)
        dp_ref[:, 2 * QK_W + V_W:2 * QK_W + 2 * V_W] = dg.astype(BF16)
        dp_ref[:, 2 * QK_W + 2 * V_W:] = du.astype(BF16)

    rev = lambda i, j: i * nblk + (nblk - 1 - j)
    blk = lambda w: pl.BlockSpec((BLK, w), lambda i, j: (rev(i, j), 0))
    full = lambda a: pl.BlockSpec(a.shape, lambda i, j: (0,) * a.ndim)
    in_w = 2 * QK_W + 2 * V_W + POOL_W
    return _pallas(
        body, name, (nseq, nblk),
        [blk(d), blk(QK_W), blk(QK_W), blk(V_W), blk(V_W), blk(V_W), blk(POOL_W),
         pl.BlockSpec((1, QK_W, V_W), lambda i, j: (rev(i, j), 0, 0)),
         full(mask), full(dq), full(dk), full(gbd), full(bd), full(gain), full(wp), full(scale), full(w_out),
         pl.BlockSpec((BLK, QK_W), lambda i, j: (nblk - 1 - j, 0)),
         pl.BlockSpec((BLK, QK_W), lambda i, j: (nblk - 1 - j, 0))],
        [blk(in_w), blk(d), pl.BlockSpec((1, V_W), lambda i, j: (0, 0)),
         pl.BlockSpec((1, POOL_W), lambda i, j: (0, 0)), pl.BlockSpec((n_win, GC, GC), lambda i, j: (0, 0, 0))],
        [jax.ShapeDtypeStruct((t, in_w), BF16), jax.ShapeDtypeStruct((t, d), BF16),
         jax.ShapeDtypeStruct((1, V_W), F32), jax.ShapeDtypeStruct((1, POOL_W), F32),
         jax.ShapeDtypeStruct((n_win, GC, GC), F32)],
        [pltpu.VMEM((QK_W, V_W), F32), pltpu.VMEM((HALO, POOL_W), F32)],
        (dx2, qs, k, v, g, o, pooled, st, mask, dq, dk, gbd, bd, gain, wp, scale, w_out, cos_t, sin_t), ride)


def _loss_head(x3, nf, tgt, name):
    t, d = x3.shape

    def body(x_ref, n_ref, t_ref, dx_ref, dn_ref, loss_ref, dxb_ref):
        @pl.when(pl.program_id(0) == 0)
        def _():
            dn_ref[...] = jnp.zeros_like(dn_ref)
            loss_ref[...] = jnp.zeros_like(loss_ref)

        xv = x_ref[...]
        nv = n_ref[...]
        r = lax.rsqrt(jnp.mean(xv * xv, axis=-1, keepdims=True) + RMS_EPS)
        xh = xv * r
        err = xh * nv - t_ref[...]
        row = jnp.mean(err * err, axis=-1, keepdims=True)
        loss_ref[...] += 0.5 * jnp.sum(row, axis=0, keepdims=True)
        dy = err * (1.0 / d)
        dn_ref[...] += jnp.sum(dy * xh, axis=0, keepdims=True)
        dxh = dy * nv
        dx = r * (dxh - xh * jnp.mean(dxh * xh, axis=-1, keepdims=True))
        dx_ref[...] = dx
        dxb_ref[...] = (0.5 * dx).astype(BF16)

    tile = pl.BlockSpec((TM, d), lambda i: (i, 0))
    return pl.pallas_call(
        body, name=name, grid=(t // TM,),
        in_specs=[tile, pl.BlockSpec((1, d), lambda i: (0, 0)), tile],
        out_specs=[tile, pl.BlockSpec((1, d), lambda i: (0, 0)), pl.BlockSpec((1, 1), lambda i: (0, 0)), tile],
        out_shape=[jax.ShapeDtypeStruct((t, d), F32), jax.ShapeDtypeStruct((1, d), F32), jax.ShapeDtypeStruct((1, 1), F32),
                   jax.ShapeDtypeStruct((t, d), BF16)],
        compiler_params=_cparams(1),
    )(x3, nf, tgt)


def _coords():
    return lax.axis_index("x"), lax.axis_index("y"), lax.axis_index("c")


def _window(ref, kind, idx, size):
    if kind == "col":
        return ref.at[:, pl.ds(pl.multiple_of(idx * size, LANE), size)]
    return ref.at[pl.ds(pl.multiple_of(idx * size, 8), size), :]


def _run_exchange(ex, name):
    n_in = len(ex.inputs)

    def body(*refs):
        ins, outs, sems = refs[:n_in], refs[n_in:n_in + len(ex.out_shape)], refs[n_in + len(ex.out_shape):]
        ex.start(ins, outs, sems)
        if ex.mid is not None:
            ex.mid(ins, outs, sems)
        ex.finish(ins, outs, sems)

    return pl.pallas_call(body, name=name, in_specs=[ANY] * n_in, out_specs=[ANY] * len(ex.out_shape),
                          out_shape=ex.out_shape, scratch_shapes=ex.scratch)(*ex.inputs)


def _join(exchanges):
    bounds = []
    i0 = o0 = s0 = 0
    for ex in exchanges:
        bounds.append((i0, o0, s0))
        i0, o0, s0 = i0 + len(ex.inputs), o0 + len(ex.out_shape), s0 + len(ex.scratch)

    def phase(which):
        def run(ins, outs, sems):
            for ex, (i, o, s) in zip(exchanges, bounds):
                fn = getattr(ex, which)
                if fn is not None:
                    fn(ins[i:i + len(ex.inputs)], outs[o:o + len(ex.out_shape)], sems[s:s + len(ex.scratch)])
        return run

    return _Exchange(sum((ex.inputs for ex in exchanges), []), sum((ex.out_shape for ex in exchanges), []),
                     sum((ex.scratch for ex in exchanges), []), phase("start"), phase("finish"),
                     phase("mid") if any(ex.mid is not None for ex in exchanges) else None)


def _gather_exchange(parts):
    n = len(parts)
    kinds = [kd for _, kd in parts]
    sizes = [a.shape[1] if kd == "col" else a.shape[0] for a, kd in parts]

    def plan(ins, outs, sems):
        send_sems, recv_sems, local_sems = sems
        x, y, c = _coords()
        me, sibling = (x, y, c), (x, y, 1 - c)
        chips = [(1 - x, y), (x, 1 - y), (1 - x, 1 - y)]

        def win(p, dev):
            return _window(outs[p], kinds[p], 4 * dev[0] + 2 * dev[1] + dev[2], sizes[p])

        def copy(p, k, block, to, src=None):
            return pltpu.make_async_remote_copy(
                src_ref=win(p, block) if src is None else src, dst_ref=win(p, block),
                send_sem=send_sems.at[p * 7 + k], recv_sem=recv_sems.at[p * 7 + k], device_id=to, device_id_type=MESH_ID)

        mine = [pltpu.make_async_copy(ins[p], win(p, me), local_sems.at[p]) for p in range(n)]
        first, arrived, passed, rest = [], [], [], []
        for p in range(n):
            first.append(copy(p, 0, me, sibling, src=ins[p]))
            first += [copy(p, 1 + q, me, (*chip, c), src=ins[p]) for q, chip in enumerate(chips)]
            rest.append(copy(p, 0, sibling, me))
            rest += [copy(p, 4 + q, (*chip, 1 - c), me) for q, chip in enumerate(chips)]
        for q, chip in enumerate(chips):
            for p in range(n):
                arrived.append(copy(p, 1 + q, (*chip, c), me))
                passed.append(copy(p, 4 + q, (*chip, c), sibling))
        return mine, first, arrived, passed, rest

    def start(ins, outs, sems):
        mine, first, _, _, _ = plan(ins, outs, sems)
        for cp in mine + first:
            cp.start()

    def mid(ins, outs, sems):
        _, _, arrived, passed, _ = plan(ins, outs, sems)
        for got, fwd in zip(arrived, passed):
            got.wait_recv()
            fwd.start()

    def finish(ins, outs, sems):
        mine, first, _, passed, rest = plan(ins, outs, sems)
        for cp in rest:
            cp.wait_recv()
        for cp in first + passed:
            cp.wait_send()
        for cp in mine:
            cp.wait()

    out_shape = [jax.ShapeDtypeStruct((a.shape[0], N_DEV * a.shape[1]) if kd == "col" else (N_DEV * a.shape[0], a.shape[1]),
                                      a.dtype) for a, kd in parts]
    scratch = [pltpu.SemaphoreType.DMA((7 * n,)), pltpu.SemaphoreType.DMA((7 * n,)), pltpu.SemaphoreType.DMA((n,))]
    return _Exchange([a for a, _ in parts], out_shape, scratch, start, finish, mid)


def _all_gather(parts, name):
    return _run_exchange(_gather_exchange(parts), name)


def _shard_shape(a, kd):
    return (a.shape[0], a.shape[1] // N_DEV) if kd == "col" else (a.shape[0] // N_DEV, a.shape[1])


def _symmetric_exchange(inputs, out_shape, n_copies, plan):
    def start(ins, outs, sems):
        for cp in plan(ins, outs, sems):
            cp.start()

    def finish(ins, outs, sems):
        copies = plan(ins, outs, sems)
        for cp in copies:
            cp.wait_recv()
        for cp in copies:
            cp.wait_send()

    scratch = [pltpu.SemaphoreType.DMA((n_copies,)), pltpu.SemaphoreType.DMA((n_copies,))]
    return _Exchange(inputs, out_shape, scratch, start, finish)


def _rs_pair_exchange(grads):
    n = len(grads)
    kinds = [kd for _, kd in grads]
    shapes = [_shard_shape(a, kd) for a, kd in grads]

    def plan(ins, outs, sems):
        send_sems, recv_sems = sems
        x, y, c = _coords()
        copies = []
        for p in range(n):
            size = shapes[p][1] if kinds[p] == "col" else shapes[p][0]
            for s in range(4):
                src = _window(ins[p], kinds[p], 2 * s + (1 - c), size)
                copies.append(pltpu.make_async_remote_copy(
                    src_ref=src, dst_ref=outs[p].at[s], send_sem=send_sems.at[4 * p + s], recv_sem=recv_sems.at[4 * p + s],
                    device_id=(x, y, 1 - c), device_id_type=MESH_ID))
        return copies

    return _symmetric_exchange([a for a, _ in grads], [jax.ShapeDtypeStruct((4,) + shapes[p], BF16) for p in range(n)],
                               4 * n, plan)


def _rs_chips_exchange(sums):
    n = len(sums)

    def plan(ins, outs, sems):
        send_sems, recv_sems = sems
        x, y, c = _coords()
        chips = [(1 - x, y), (x, 1 - y), (1 - x, 1 - y)]
        copies = []
        for p in range(n):
            for q, (cx, cy) in enumerate(chips):
                copies.append(pltpu.make_async_remote_copy(
                    src_ref=ins[p].at[2 * cx + cy], dst_ref=outs[p].at[q],
                    send_sem=send_sems.at[3 * p + q], recv_sem=recv_sems.at[3 * p + q],
                    device_id=(cx, cy, c), device_id_type=MESH_ID))
        return copies

    return _symmetric_exchange(list(sums), [jax.ShapeDtypeStruct((3,) + a.shape[1:], BF16) for a in sums], 3 * n, plan)


def _rs_pair(grads, name):
    return _run_exchange(_rs_pair_exchange(grads), name)


def _rs_chips(sums, name):
    return _run_exchange(_rs_chips_exchange(sums), name)


def _pair_sum(grad, kd, recv, core, name):
    _, r, cw = recv.shape
    tr = min(r, TM)

    def body(core_ref, g_ref, r_ref, o_ref):
        del core_ref
        o_ref[0] = (g_ref[...].astype(F32) + r_ref[0].astype(F32)).astype(BF16)

    if kd == "col":
        g_spec = pl.BlockSpec((tr, cw), lambda s, i, core_ref: (i, 2 * s + core_ref[0]))
    else:
        g_spec = pl.BlockSpec((tr, cw), lambda s, i, core_ref: ((2 * s + core_ref[0]) * (r // tr) + i, 0))
    grid_spec = pltpu.PrefetchScalarGridSpec(
        num_scalar_prefetch=1, grid=(4, r // tr),
        in_specs=[g_spec, pl.BlockSpec((1, tr, cw), lambda s, i, core_ref: (s, i, 0))],
        out_specs=pl.BlockSpec((1, tr, cw), lambda s, i, core_ref: (s, i, 0)))
    return pl.pallas_call(
        body, name=name, grid_spec=grid_spec, out_shape=jax.ShapeDtypeStruct(recv.shape, BF16),
        compiler_params=_cparams(2),
    )(core, grad, recv)


def _adam_math(w, g, m, v):
    m2 = B1 * m + (1.0 - B1) * g
    v2 = B2 * v + (1.0 - B2) * (g * g)
    m_hat = m2 / (1.0 - B1 ** STEP)
    v_hat = v2 / (1.0 - B2 ** STEP)
    delta = -LR * (m_hat / (jnp.sqrt(v_hat) + ADAM_EPS) + WD * w)
    return delta, m2, v2


def _chip_sum_adam(psum, recv, chip, w, m, v, name):
    r, cw = w.shape
    pc = psum.shape[2]
    tr = min(r, TM)

    def body(chip_ref, p_ref, r_ref, w_ref, m_ref, v_ref, g_ref, d_ref, m2_ref, v2_ref):
        del chip_ref
        g = p_ref[0].astype(F32) + r_ref[0].astype(F32) + r_ref[1].astype(F32) + r_ref[2].astype(F32)
        g = g[:, :cw]
        delta, m2, v2 = _adam_math(w_ref[...], g, m_ref[...], v_ref[...])
        g_ref[...] = g
        d_ref[...] = delta
        m2_ref[...] = m2
        v2_ref[...] = v2

    loc = pl.BlockSpec((tr, cw), lambda i, chip_ref: (i, 0))
    grid_spec = pltpu.PrefetchScalarGridSpec(
        num_scalar_prefetch=1, grid=(r // tr,),
        in_specs=[pl.BlockSpec((1, tr, pc), lambda i, chip_ref: (chip_ref[0], i, 0)),
                  pl.BlockSpec((3, tr, pc), lambda i, chip_ref: (0, i, 0)), loc, loc, loc],
        out_specs=[loc, loc, loc, loc])
    return pl.pallas_call(
        body, name=name, grid_spec=grid_spec, out_shape=[jax.ShapeDtypeStruct((r, cw), F32)] * 4,
        compiler_params=_cparams(1),
    )(chip, psum, recv, w, m, v)


def _small_allreduce_adam(partials, params, moms, vels, name, ride=None):
    n = len(partials)
    row0 = []
    rows = 0
    for a in partials:
        if a.shape[0] >= 8:
            rows = _pad_to(rows, 8)
        row0.append(rows)
        rows += a.shape[0]
    rows = _pad_to(rows, 8)
    width = max(a.shape[1] for a in partials)
    r_in = 0 if ride is None else len(ride.inputs)
    r_out = 0 if ride is None else len(ride.out_shape)

    def body(*refs):
        g_in = refs[:n]
        w_in, m_in, v_in = refs[n:2 * n], refs[2 * n:3 * n], refs[3 * n:4 * n]
        refs = refs[4 * n:]
        r_ins, refs = refs[:r_in], refs[r_in:]
        outs, refs = refs[:4 * n], refs[4 * n:]
        r_outs, refs = refs[:r_out], refs[r_out:]
        pair, chips, send_sems, recv_sems = refs[:4]
        if ride is not None:
            ride.start(r_ins, r_outs, refs[4:])
        x, y, c = _coords()
        chip = 2 * x + y
        pair[c] = jnp.zeros((rows, width), F32)
        for p in range(n):
            r, cw = partials[p].shape
            pair[c, row0[p]:row0[p] + r, 0:cw] = g_in[p][...]
        swap = pltpu.make_async_remote_copy(src_ref=pair.at[c], dst_ref=pair.at[c], send_sem=send_sems.at[0],
                                            recv_sem=recv_sems.at[0], device_id=(x, y, 1 - c), device_id_type=MESH_ID)
        swap.start()
        swap.wait_recv()
        swap.wait_send()
        chips[chip] = pair[0] + pair[1]
        copies = [pltpu.make_async_remote_copy(
            src_ref=chips.at[chip], dst_ref=chips.at[chip], send_sem=send_sems.at[1 + q], recv_sem=recv_sems.at[1 + q],
            device_id=(cx, cy, c), device_id_type=MESH_ID) for q, (cx, cy) in enumerate([(1 - x, y), (x, 1 - y), (1 - x, 1 - y)])]
        for cp in copies:
            cp.start()
        for cp in copies:
            cp.wait_recv()
        for cp in copies:
            cp.wait_send()
        for p in range(n):
            r, cw = partials[p].shape
            g = chips[0, row0[p]:row0[p] + r, 0:cw]
            for q in range(1, 4):
                g = g + chips[q, row0[p]:row0[p] + r, 0:cw]
            delta, m2, v2 = _adam_math(w_in[p][...], g, m_in[p][...], v_in[p][...])
            outs[4 * p][...] = g
            outs[4 * p + 1][...] = delta
            outs[4 * p + 2][...] = m2
            outs[4 * p + 3][...] = v2
        if ride is not None:
            ride.finish(r_ins, r_outs, refs[4:])

    out_shape = []
    for a in partials:
        out_shape += [jax.ShapeDtypeStruct(a.shape, F32)] * 4
    res = pl.pallas_call(
        body, name=name, in_specs=[VMEM_SPEC] * (4 * n) + [ANY] * r_in, out_specs=[VMEM_SPEC] * (4 * n) + [ANY] * r_out,
        out_shape=out_shape + ([] if ride is None else ride.out_shape),
        scratch_shapes=[pltpu.VMEM((2, rows, width), F32), pltpu.VMEM((4, rows, width), F32),
                        pltpu.SemaphoreType.DMA((4,)), pltpu.SemaphoreType.DMA((4,))] + ([] if ride is None else ride.scratch),
    )(*partials, *params, *moms, *vels, *([] if ride is None else ride.inputs))
    return res if ride is None else (res[:4 * n], res[4 * n:])


def _local_step(xf, tgt, nseq, seq, cols1_all, later, small_w, core=None, small_step=None):
    d = xf.shape[1]
    dist = core is not None
    n1, n2, gain, pool_w, pool_scale, n3, nf = small_w
    tf = 2 * cols1_all.shape[1] // N_DEV
    consts = _retention_constants()
    cos_t, sin_t = _rotary_tables(seq)
    wp_b = pool_w.astype(BF16)

    def pair_sums(grads, recv, names):
        return [_pair_sum(g, kd, r, core, "pair_sum_" + nm) for (g, kd), r, nm in zip(grads, recv, names)]

    def riding(host):
        return _gather_exchange(later[host])

    act1 = (xf, n1, (cols1_all, 0), (cols1_all, 1), "ffn1_act")
    if dist:
        (h1, b1, sil1, dsil1, s1), (d1_all, win_all, wout_all, gate2_all) = _ffn_act(*act1, ride=riding("ffn1_act"))
        x1, (up2_all,) = _ffn_down(s1, xf, d1_all, "ffn1_down", ride=riding("ffn1_down"))
    else:
        d1_all, win_all, wout_all, gate2_all, up2_all, d2_all = (later[k] for k in ("down1", "w_in", "w_out", "gate2",
                                                                                   "up2", "down2"))
        h1, b1, sil1, dsil1, s1 = _ffn_act(*act1)
        x1 = _ffn_down(s1, xf, d1_all, "ffn1_down")
    h2, qs, kr, vv, gg, uu = _mix_in(x1, n2, win_all, cos_t, sin_t, seq, "mix_in")
    fwd_mix = (qs, kr, vv, gg, uu, x1, consts, gain, wp_b, pool_scale, wout_all, nseq, seq, "mix_core_fwd")
    if dist:
        (x2, mix, oo, pooled, states), (d2_all,) = _mix_core_fwd(*fwd_mix, ride=riding("mix_core_fwd"))
        act2 = (x2, n3, (gate2_all, 0), (up2_all, 0), "ffn2_act")
        (h3, b3, sil3, dsil3, s3), (cols2_t, d2_t) = _ffn_act(*act2, ride=riding("ffn2_act"))
        x3, (win_t, wout_t) = _ffn_down(s3, x2, d2_all, "ffn2_down", ride=riding("ffn2_down"))
    else:
        x2, mix, oo, pooled, states = _mix_core_fwd(*fwd_mix)
        h3, b3, sil3, dsil3, s3 = _ffn_act(x2, n3, (gate2_all, 0), (up2_all, 0), "ffn2_act")
        x3 = _ffn_down(s3, x2, d2_all, "ffn2_down")
        cols1_t = _transpose(cols1_all, "transpose_cols1")
        cols2_t = jnp.concatenate([_transpose(gate2_all, "transpose_gate2"), _transpose(up2_all, "transpose_up2")], axis=1)
        d1_t = _transpose(d1_all, "transpose_down1")
        d2_t = _transpose(d2_all, "transpose_down2")
        win_t = _transpose(win_all, "transpose_w_in")
        wout_t = _transpose(wout_all, "transpose_w_out")
    dx3, dnf, loss_part, dx3b = _loss_head(x3, nf, tgt, "loss_head")
    out = {}

    if dist:
        (da3, db3, g_wd2), (cols1_t, d1_t) = _ffn_bwd_act(dx3b, b3, sil3, dsil3, d2_t, "ffn2_bwd_act",
                                                          ride=riding("ffn2_bwd_act"))
    else:
        da3, db3, g_wd2 = _ffn_bwd_act(dx3b, b3, sil3, dsil3, d2_t, "ffn2_bwd_act")
    names2 = ["ffn2_gate", "ffn2_up", "ffn2_down"]
    grads2 = [(_wgrad(da3, h3, 1.0, tf, d, "wgrad_gate2"), "row"), (_wgrad(db3, h3, 1.0, tf, d, "wgrad_up2"), "row"),
              (g_wd2, "row")]
    if dist:
        (dx2, dn3), recv2 = _ffn_bwd_in(da3, db3, dx3, x2, n3, cols2_t, 0, "ffn2_bwd_in", ride=_rs_pair_exchange(grads2))
        sums2 = pair_sums(grads2, recv2, names2)
        (dp, dx2b, dgain, dscale, dwp), crecv2 = _mix_core_bwd(
            dx2, qs, kr, vv, gg, oo, pooled, states, consts, gain, wp_b, pool_scale, wout_t, cos_t, sin_t, nseq, seq,
            "mix_core_bwd", ride=_rs_chips_exchange(sums2))
        out.update({nm: (s, r) for nm, s, r in zip(names2, sums2, crecv2)})
    else:
        dx2, dn3 = _ffn_bwd_in(da3, db3, dx3, x2, n3, cols2_t, 0, "ffn2_bwd_in")
        dp, dx2b, dgain, dscale, dwp = _mix_core_bwd(dx2, qs, kr, vv, gg, oo, pooled, states, consts, gain, wp_b,
                                                     pool_scale, wout_t, cos_t, sin_t, nseq, seq, "mix_core_bwd")
        out.update(dict(zip(names2, grads2)))

    names_m = ["w_in", "w_out"]
    grads_m = [(_wgrad(h2, dp, 1.0, d, d, "wgrad_in"), "col"), (_wgrad(mix, dx2b, 1.0, d, d, "wgrad_out"), "row")]
    if dist:
        (dx1, dn2, dx1b), recv_m = _mix_in_bwd(dp, dx2, x1, n2, win_t, "mix_in_bwd", ride=_rs_pair_exchange(grads_m))
        sums_m = pair_sums(grads_m, recv_m, names_m)
        (da1, db1, g_wd1), crecv_m = _ffn_bwd_act(dx1b, b1, sil1, dsil1, d1_t, "ffn1_bwd_act",
                                                  ride=_rs_chips_exchange(sums_m))
        out.update({nm: (s, r) for nm, s, r in zip(names_m, sums_m, crecv_m)})
    else:
        dx1, dn2, dx1b = _mix_in_bwd(dp, dx2, x1, n2, win_t, "mix_in_bwd")
        da1, db1, g_wd1 = _ffn_bwd_act(dx1b, b1, sil1, dsil1, d1_t, "ffn1_bwd_act")
        out.update(dict(zip(names_m, grads_m)))

    dx0, dn1 = _ffn_bwd_in(da1, db1, dx1, xf, n1, cols1_t, 0, "ffn1_bwd_in")
    small_parts = (dn1, dn2, dgain, dwp, dscale, dn3, dnf)
    g_down = (g_wd1, "row")
    if dist:
        g_gate, recv_d = _wgrad(da1, h1, 1.0, tf, d, "wgrad_gate1", ride=_rs_pair_exchange([g_down]))
        g_gate = (g_gate, "row")
        sum_d = pair_sums([g_down], recv_d, ["ffn1_down"])
        g_up, (crecv_d, recv_g) = _wgrad(db1, h1, 1.0, tf, d, "wgrad_up1",
                                         ride=_join([_rs_chips_exchange(sum_d), _rs_pair_exchange([g_gate])]))
        g_up = (g_up, "row")
        sum_g = pair_sums([g_gate], [recv_g], ["ffn1_gate"])
        small_out, (crecv_g, recv_u) = small_step(small_parts, _join([_rs_chips_exchange(sum_g), _rs_pair_exchange([g_up])]))
        sum_u = pair_sums([g_up], [recv_u], ["ffn1_up"])
        (crecv_u,) = _run_exchange(_rs_chips_exchange(sum_u), "rs_tail")
        out.update({"ffn1_gate": (sum_g[0], crecv_g), "ffn1_up": (sum_u[0], crecv_u), "ffn1_down": (sum_d[0], crecv_d)})
        return loss_part, dx0, out, small_out
    out.update({"ffn1_gate": (_wgrad(da1, h1, 1.0, tf, d, "wgrad_gate1"), "row"),
                "ffn1_up": (_wgrad(db1, h1, 1.0, tf, d, "wgrad_up1"), "row"), "ffn1_down": g_down})
    return loss_part, dx0, out, small_parts


def kernel(x, norm_ffn1, ffn1_gate, ffn1_up, ffn1_down, norm_mix, w_in, ret_gn_gain, pool_w, pool_scale, w_out, norm_ffn2, ffn2_gate, ffn2_up, ffn2_down, norm_final, loss_target, m_norm_ffn1, m_ffn1_gate, m_ffn1_up, m_ffn1_down, m_norm_mix, m_w_in, m_ret_gn_gain, m_pool_w, m_pool_scale, m_w_out, m_norm_ffn2, m_ffn2_gate, m_ffn2_up, m_ffn2_down, m_norm_final, v_norm_ffn1, v_ffn1_gate, v_ffn1_up, v_ffn1_down, v_norm_mix, v_w_in, v_ret_gn_gain, v_pool_w, v_pool_scale, v_w_out, v_norm_ffn2, v_ffn2_gate, v_ffn2_up, v_ffn2_down, v_norm_final):
    nseq, seq, d = x.shape
    t = nseq * seq
    f_loc = ffn1_gate.shape[2]
    f_pad = _pad_to(f_loc, LANE)
    xf = x.reshape(t, d)
    tgt = loss_target.reshape(t, d)
    core = lax.axis_index("c").astype(jnp.int32).reshape(1)
    chip = (2 * lax.axis_index("x") + lax.axis_index("y")).astype(jnp.int32).reshape(1)

    colp = lambda w: jnp.pad(w[0].astype(BF16), ((0, 0), (0, f_pad - f_loc)))
    rowp = lambda w: jnp.pad(w[0].astype(BF16), ((0, f_pad - f_loc), (0, 0)))
    gate2, up2 = colp(ffn2_gate), colp(ffn2_up)
    cols1 = jnp.concatenate([colp(ffn1_gate), colp(ffn1_up)], axis=0)
    cols2_t = jnp.concatenate([gate2.T, up2.T], axis=1)
    (cols1_all,) = _all_gather([(cols1, "col")], "all_gather_ffn1")
    d1_loc, d2_loc, win_loc, wout_loc = rowp(ffn1_down), rowp(ffn2_down), w_in[0].astype(BF16), w_out[0].astype(BF16)
    later = {"ffn1_act": [(d1_loc, "row"), (win_loc, "col"), (wout_loc, "row"), (gate2, "col")],
             "ffn1_down": [(up2, "col")], "mix_core_fwd": [(d2_loc, "row")],
             "ffn2_act": [(cols2_t, "row"), (d2_loc.T, "col")], "ffn2_down": [(win_loc.T, "row"), (wout_loc.T, "col")],
             "ffn2_bwd_act": [(cols1.T, "row"), (d1_loc.T, "col")]}

    flat = lambda a: a.reshape(pool_w.size // d, d)
    params = [norm_ffn1, norm_mix, ret_gn_gain, flat(pool_w), pool_scale, norm_ffn2, norm_final.reshape(1, d)]
    moms = [m_norm_ffn1, m_norm_mix, m_ret_gn_gain, flat(m_pool_w), m_pool_scale, m_norm_ffn2, m_norm_final.reshape(1, d)]
    vels = [v_norm_ffn1, v_norm_mix, v_ret_gn_gain, flat(v_pool_w), v_pool_scale, v_norm_ffn2, v_norm_final.reshape(1, d)]

    def small_step(parts, ride):
        dn1, dn2, dgain, dwp, dscale, dn3, dnf = parts
        return _small_allreduce_adam([dn1, dn2, dgain, flat(dwp), dscale, dn3, dnf], params, moms, vels,
                                     "small_allreduce_adam", ride)

    small_w = (norm_ffn1, norm_mix, ret_gn_gain, pool_w[0], pool_scale, norm_ffn2, norm_final.reshape(1, d))
    loss_part, dx0, reduced, small_out = _local_step(xf, tgt, nseq, seq, cols1_all, later, small_w, core, small_step)

    local = {"ffn1_gate": (ffn1_gate, m_ffn1_gate, v_ffn1_gate), "ffn1_up": (ffn1_up, m_ffn1_up, v_ffn1_up),
             "ffn1_down": (ffn1_down, m_ffn1_down, v_ffn1_down), "w_in": (w_in, m_w_in, v_w_in),
             "w_out": (w_out, m_w_out, v_w_out), "ffn2_gate": (ffn2_gate, m_ffn2_gate, v_ffn2_gate),
             "ffn2_up": (ffn2_up, m_ffn2_up, v_ffn2_up), "ffn2_down": (ffn2_down, m_ffn2_down, v_ffn2_down)}
    big = {}
    for nm, (w, m, v) in local.items():
        ps, rcv = reduced[nm]
        flip = nm.endswith("gate") or nm.endswith("up")
        view = (lambda a: a[0].T) if flip else (lambda a: a[0])
        res = _chip_sum_adam(ps, rcv, chip, view(w), view(m), view(v), "adam_" + nm)
        big[nm] = tuple((a.T if flip else a)[None] for a in res)

    small_names = ["norm_ffn1", "norm_mix", "ret_gn_gain", "pool_w", "pool_scale", "norm_ffn2", "norm_final"]
    shapes = [norm_ffn1.shape, norm_mix.shape, ret_gn_gain.shape, pool_w.shape, pool_scale.shape, norm_ffn2.shape,
              norm_final.shape]
    small = {nm: tuple(small_out[4 * p + q].reshape(shapes[p]) for q in range(4)) for p, nm in enumerate(small_names)}

    loss = lax.psum(loss_part[0, 0], ("x", "y", "c"))
    order = ["norm_ffn1", "ffn1_gate", "ffn1_up", "ffn1_down", "norm_mix", "w_in", "ret_gn_gain", "pool_w", "pool_scale",
             "w_out", "norm_ffn2", "ffn2_gate", "ffn2_up", "ffn2_down", "norm_final"]
    both = {**big, **small}
    outs = [loss, dx0.reshape(nseq, seq, d)]
    for q in range(4):
        outs += [both[nm][q] for nm in order]
    return tuple(outs)
```

```python
import functools

import numpy as np
import jax
import jax.numpy as jnp
from jax import lax
from jax.experimental import pallas as pl
from jax.experimental.pallas import tpu as pltpu

F32, BF16 = jnp.float32, jnp.bfloat16
MESH_ID = pl.DeviceIdType.MESH
ANY = pl.BlockSpec(memory_space=pl.ANY)
VMEM_SPEC = pl.BlockSpec(memory_space=pltpu.VMEM)

N_DEV = 8
RMS_EPS = 1e-6
GN_EPS = 1e-5
HEADS, DK, DV = 4, 64, 128
QK_W, V_W, POOL_W = HEADS * DK, HEADS * DV, 512
WINDOWS = (2, 4, 8, 16)
GC = POOL_W // len(WINDOWS)
CHUNK = 64
BLK = 4 * CHUNK
HALO = 16
ROPE_BASE = 10000.0
LR, B1, B2, ADAM_EPS, WD, STEP = 0.001, 0.9, 0.999, 1e-08, 0.01, 10
LANE = 128
TM = 512
FFN_TM = 1024
FFN_FWD_TF = 512
WGRAD_TT = 4096
VMEM_LIMIT = 56 * 1024 * 1024


def _cparams(n_axes):
    return pltpu.CompilerParams(dimension_semantics=("arbitrary",) * n_axes, vmem_limit_bytes=VMEM_LIMIT)


class _Exchange:
    def __init__(self, inputs, out_shape, scratch, start, finish, mid=None):
        self.inputs, self.out_shape, self.scratch = list(inputs), list(out_shape), list(scratch)
        self.start, self.finish, self.mid = start, finish, mid


def _pallas(body, name, grid, in_specs, out_specs, out_shape, scratch_shapes, args, ride=None):
    n_axes = len(grid)
    if ride is None:
        return pl.pallas_call(body, name=name, grid=grid, in_specs=in_specs, out_specs=out_specs, out_shape=out_shape,
                              scratch_shapes=scratch_shapes, compiler_params=_cparams(n_axes))(*args)
    n_in, n_out, n_scr = len(in_specs), len(out_specs), len(scratch_shapes)
    r_in, r_out = len(ride.inputs), len(ride.out_shape)

    def hosted(*refs):
        ins, refs = refs[:n_in], refs[n_in:]
        r_ins, refs = refs[:r_in], refs[r_in:]
        outs, refs = refs[:n_out], refs[n_out:]
        r_outs, refs = refs[:r_out], refs[r_out:]
        scr, sems = refs[:n_scr], refs[n_scr:]
        ids = [pl.program_id(a) for a in range(n_axes)]
        first, last, inner0 = ids[0] == 0, ids[0] == grid[0] - 1, None
        for a in range(1, n_axes):
            first = first & (ids[a] == 0)
            last = last & (ids[a] == grid[a] - 1)
            inner0 = (ids[a] == 0) if inner0 is None else inner0 & (ids[a] == 0)

        @pl.when(first)
        def _():
            ride.start(r_ins, r_outs, sems)

        if ride.mid is not None:
            at_mid = ids[0] == grid[0] - 1
            if inner0 is not None:
                at_mid = at_mid & inner0

            @pl.when(at_mid)
            def _():
                ride.mid(r_ins, r_outs, sems)

        body(*ins, *outs, *scr)

        @pl.when(last)
        def _():
            ride.finish(r_ins, r_outs, sems)

    res = pl.pallas_call(
        hosted, name=name, grid=grid, in_specs=list(in_specs) + [ANY] * r_in, out_specs=list(out_specs) + [ANY] * r_out,
        out_shape=list(out_shape) + ride.out_shape, scratch_shapes=list(scratch_shapes) + ride.scratch,
        compiler_params=_cparams(n_axes))(*args, *ride.inputs)
    return res[:n_out], res[n_out:]


def _dot(a, b):
    return jnp.dot(a, b, preferred_element_type=F32)


def _dot_nt(a, b):
    return lax.dot_general(a, b, (((1,), (1,)), ((), ())), preferred_element_type=F32)


def _dot_tn(a, b):
    return lax.dot_general(a, b, (((0,), (0,)), ((), ())), preferred_element_type=F32)


def _sigmoid(x):
    return 0.5 * jnp.tanh(0.5 * x) + 0.5


def _transpose(w, name):
    r, c = w.shape
    tb = 512

    def body(x_ref, o_ref):
        o_ref[...] = x_ref[...].T

    return pl.pallas_call(
        body, name=name, grid=(r // tb, c // tb),
        in_specs=[pl.BlockSpec((tb, tb), lambda i, j: (i, j))],
        out_specs=pl.BlockSpec((tb, tb), lambda i, j: (j, i)),
        out_shape=jax.ShapeDtypeStruct((c, r), w.dtype),
        compiler_params=_cparams(2),
    )(w)


def _pad_to(n, m):
    return (n + m - 1) // m * m


def _retention_constants():
    gamma = (1.0 - 2.0 ** (-5.0 - np.arange(HEADS, dtype=np.float32))).astype(np.float32)
    log_g = np.log(gamma).astype(np.float32)
    i = np.arange(BLK)
    diff = (i[:, None] - i[None, :]).astype(np.float32)
    same = (i[:, None] // CHUNK) == (i[None, :] // CHUNK)
    earlier = (i[None, :] // CHUNK) < (i[:, None] // CHUNK)
    mask = np.zeros((HEADS, BLK, BLK), np.float32)
    for h in range(HEADS):
        dec_abs = np.exp(log_g[h] * np.abs(diff)).astype(np.float32)
        dec = np.exp(log_g[h] * diff * earlier).astype(np.float32)
        mask[h] = np.where(same, dec_abs, np.where(earlier, dec, 0.0))
    dq = np.zeros((BLK, V_W), np.float32)
    dk = np.zeros((BLK, QK_W), np.float32)
    gbd = np.zeros((QK_W, V_W), np.float32)
    for h in range(HEADS):
        dq[:, h * DV:(h + 1) * DV] = np.exp(log_g[h] * (i + 1.0)).astype(np.float32)[:, None]
        dk[:, h * DK:(h + 1) * DK] = np.exp(log_g[h] * (BLK - 1.0 - i)).astype(np.float32)[:, None]
        gbd[h * DK:(h + 1) * DK, h * DV:(h + 1) * DV] = np.exp(log_g[h] * np.float32(BLK))
    bd = (gbd > 0).astype(np.float32)
    return jnp.asarray(mask), jnp.asarray(dq), jnp.asarray(dk), jnp.asarray(gbd), jnp.asarray(bd)


def _rotary_tables(seq):
    half = DK // 2
    freqs = ROPE_BASE ** (-jnp.arange(half, dtype=F32) * 2.0 / DK)
    ang = jnp.arange(seq, dtype=F32)[:, None] * freqs[None, :]
    cos, sin = jnp.cos(ang), jnp.sin(ang)
    cos_t = jnp.tile(jnp.concatenate([cos, cos], axis=1), (1, HEADS))
    sin_t = jnp.tile(jnp.concatenate([-sin, sin], axis=1), (1, HEADS))
    return cos_t, sin_t


def _swap_halves(x):
    lane = lax.broadcasted_iota(jnp.int32, (1, QK_W), 1)
    first = (lane & (DK - 1)) < DK // 2
    return jnp.where(first, pltpu.roll(x, QK_W - DK // 2, 1), pltpu.roll(x, DK // 2, 1))


def _head_mask(h):
    lane = lax.broadcasted_iota(jnp.int32, (1, QK_W), 1)
    return (lane >= h * DK) & (lane < (h + 1) * DK)


def _ffn_fwd(x, n, gate, up, wd, name, ride=None):
    t, d = x.shape
    (wg, gq), (wu, uq) = gate, up
    fp = wg.shape[1]
    tm = min(t, FFN_TM)
    tf = FFN_FWD_TF
    nj = fp // tf

    def body(x_ref, n_ref, wg_ref, wu_ref, wd_ref, xo_ref, h_ref, b_ref, sil_ref, dsil_ref, acc_ref):
        j = pl.program_id(1)

        @pl.when(j == 0)
        def _():
            xv = x_ref[...]
            r = lax.rsqrt(jnp.mean(xv * xv, axis=-1, keepdims=True) + RMS_EPS)
            h_ref[...] = (xv * r * n_ref[...]).astype(BF16)
            acc_ref[...] = jnp.zeros_like(acc_ref)

        h = h_ref[...]
        a = _dot(h, wg_ref[...])
        b = _dot(h, wu_ref[...])
        sg = _sigmoid(a)
        sil = a * sg
        b_ref[...] = b.astype(BF16)
        sil_ref[...] = sil.astype(BF16)
        dsil_ref[...] = (sg + sil * (1.0 - sg)).astype(BF16)
        acc_ref[...] += _dot((sil * b).astype(BF16), wd_ref[...])

        @pl.when(j == nj - 1)
        def _():
            xo_ref[...] = x_ref[...] + 0.5 * acc_ref[...]

    act = pl.BlockSpec((tm, tf), lambda i, j: (i, j))
    return _pallas(
        body, name, (t // tm, nj),
        [pl.BlockSpec((tm, d), lambda i, j: (i, 0)), pl.BlockSpec((1, d), lambda i, j: (0, 0)),
         pl.BlockSpec((d, tf), lambda i, j: (gq, j)), pl.BlockSpec((d, tf), lambda i, j: (uq, j)),
         pl.BlockSpec((tf, d), lambda i, j: (j, 0))],
        [pl.BlockSpec((tm, d), lambda i, j: (i, 0)), pl.BlockSpec((tm, d), lambda i, j: (i, 0)), act, act, act],
        [jax.ShapeDtypeStruct((t, d), F32), jax.ShapeDtypeStruct((t, d), BF16)] + [jax.ShapeDtypeStruct((t, fp), BF16)] * 3,
        [pltpu.VMEM((tm, d), F32)], (x, n, wg, wu, wd), ride)


def _ffn_act(x, n, gate, up, name, ride=None):
    t, d = x.shape
    (wg, gq), (wu, uq) = gate, up
    fp = wg.shape[1]
    tm = min(t, FFN_TM)
    tf = 2 * fp // N_DEV
    nj = fp // tf

    def body(x_ref, n_ref, wg_ref, wu_ref, h_ref, b_ref, sil_ref, dsil_ref, s_ref):
        @pl.when(pl.program_id(1) == 0)
        def _():
            xv = x_ref[...]
            r = lax.rsqrt(jnp.mean(xv * xv, axis=-1, keepdims=True) + RMS_EPS)
            h_ref[...] = (xv * r * n_ref[...]).astype(BF16)

        h = h_ref[...]
        a = _dot(h, wg_ref[...])
        b = _dot(h, wu_ref[...])
        sg = _sigmoid(a)
        sil = a * sg
        b_ref[...] = b.astype(BF16)
        sil_ref[...] = sil.astype(BF16)
        dsil_ref[...] = (sg + sil * (1.0 - sg)).astype(BF16)
        s_ref[...] = (sil * b).astype(BF16)

    act = pl.BlockSpec((tm, tf), lambda i, j: (i, j))
    return _pallas(
        body, name, (t // tm, nj),
        [pl.BlockSpec((tm, d), lambda i, j: (i, 0)), pl.BlockSpec((1, d), lambda i, j: (0, 0)),
         pl.BlockSpec((d, tf), lambda i, j: (gq, j)), pl.BlockSpec((d, tf), lambda i, j: (uq, j))],
        [pl.BlockSpec((tm, d), lambda i, j: (i, 0)), act, act, act, act],
        [jax.ShapeDtypeStruct((t, d), BF16)] + [jax.ShapeDtypeStruct((t, fp), BF16)] * 4,
        [], (x, n, wg, wu), ride)


def _ffn_down(s, x, wd, name, ride=None):
    t, d = x.shape
    fp = wd.shape[0]
    tm = min(t, FFN_TM)

    def body(s_ref, x_ref, wd_ref, xo_ref):
        xo_ref[...] = x_ref[...] + 0.5 * _dot(s_ref[...], wd_ref[...])

    row = pl.BlockSpec((tm, d), lambda i: (i, 0))
    res = _pallas(body, name, (t // tm,), [pl.BlockSpec((tm, fp), lambda i: (i, 0)), row, pl.BlockSpec((fp, d), lambda i: (0, 0))],
                  [row], [jax.ShapeDtypeStruct((t, d), F32)], [], (s, x, wd), ride)
    return res[0] if ride is None else (res[0][0], res[1])


def _ffn_bwd_act(dxob, b, sil, dsil, wd_t, name, ride=None):
    t, d = dxob.shape
    fp = wd_t.shape[1]
    tm = min(t, FFN_TM)
    tf = 2 * fp // N_DEV
    ni = t // tm

    def body(dx_ref, b_ref, sil_ref, dsil_ref, wd_ref, da_ref, db_ref, gd_ref, acc_ref):
        i = pl.program_id(1)

        @pl.when(i == 0)
        def _():
            acc_ref[...] = jnp.zeros_like(acc_ref)

        dxv = dx_ref[...]
        bv, sv = b_ref[...].astype(F32), sil_ref[...].astype(F32)
        ds = _dot(dxv, wd_ref[...])
        da_ref[...] = (ds * bv * dsil_ref[...].astype(F32)).astype(BF16)
        db_ref[...] = (ds * sv).astype(BF16)
        acc_ref[...] += _dot_tn((sv * bv).astype(BF16), dxv)

        @pl.when(i == ni - 1)
        def _():
            gd_ref[...] = acc_ref[...].astype(BF16)

    act = pl.BlockSpec((tm, tf), lambda c, i: (i, c))
    return _pallas(
        body, name, (fp // tf, ni),
        [pl.BlockSpec((tm, d), lambda c, i: (i, 0)), act, act, act, pl.BlockSpec((d, tf), lambda c, i: (0, c))],
        [act, act, pl.BlockSpec((tf, d), lambda c, i: (c, 0))],
        [jax.ShapeDtypeStruct((t, fp), BF16), jax.ShapeDtypeStruct((t, fp), BF16), jax.ShapeDtypeStruct((fp, d), BF16)],
        [pltpu.VMEM((tf, d), F32)], (dxob, b, sil, dsil, wd_t), ride)


def _ffn_bwd_in(da, db, dxo, x, n, cols_t, gq, name, ride=None):
    t, d = x.shape
    fp = cols_t.shape[0]
    tm = min(t, FFN_TM)
    tf = 2 * fp // N_DEV
    nj = fp // tf

    def body(da_ref, db_ref, dxo_ref, x_ref, n_ref, wg_ref, wu_ref, dx_ref, dn_ref, acc_ref):
        i, j = pl.program_id(0), pl.program_id(1)

        @pl.when((i == 0) & (j == 0))
        def _():
            dn_ref[...] = jnp.zeros_like(dn_ref)

        @pl.when(j == 0)
        def _():
            acc_ref[...] = jnp.zeros_like(acc_ref)

        acc_ref[...] += _dot(da_ref[...], wg_ref[...]) + _dot(db_ref[...], wu_ref[...])

        @pl.when(j == nj - 1)
        def _():
            xv = x_ref[...]
            r = lax.rsqrt(jnp.mean(xv * xv, axis=-1, keepdims=True) + RMS_EPS)
            xh = xv * r
            dh = acc_ref[...]
            dn_ref[...] += jnp.sum(dh * xh, axis=0, keepdims=True)
            dhn = dh * n_ref[...]
            dx_ref[...] = dxo_ref[...] + r * (dhn - xh * jnp.mean(dhn * xh, axis=-1, keepdims=True))

    act = pl.BlockSpec((tm, tf), lambda i, j: (i, j))
    row = pl.BlockSpec((tm, d), lambda i, j: (i, 0))
    return _pallas(
        body, name, (t // tm, nj),
        [act, act, row, row, pl.BlockSpec((1, d), lambda i, j: (0, 0)),
         pl.BlockSpec((tf, d), lambda i, j: (j, gq)), pl.BlockSpec((tf, d), lambda i, j: (j, gq + 1))],
        [row, pl.BlockSpec((1, d), lambda i, j: (0, 0))],
        [jax.ShapeDtypeStruct((t, d), F32), jax.ShapeDtypeStruct((1, d), F32)],
        [pltpu.VMEM((tm, d), F32)], (da, db, dxo, x, n, cols_t, cols_t), ride)


def _wgrad(a, b, scale, tk, tn, name, ride=None):
    t, k = a.shape
    n = b.shape[1]
    tt = min(t, WGRAD_TT)
    nt = t // tt

    def body(a_ref, b_ref, o_ref, acc_ref):
        s = pl.program_id(2)

        @pl.when(s == 0)
        def _():
            acc_ref[...] = jnp.zeros_like(acc_ref)

        acc_ref[...] += _dot_tn(a_ref[...], b_ref[...])

        @pl.when(s == nt - 1)
        def _():
            o_ref[...] = (scale * acc_ref[...]).astype(BF16)

    res = _pallas(
        body, name, (k // tk, n // tn, nt),
        [pl.BlockSpec((tt, tk), lambda p, q, s: (s, p)), pl.BlockSpec((tt, tn), lambda p, q, s: (s, q))],
        [pl.BlockSpec((tk, tn), lambda p, q, s: (p, q))], [jax.ShapeDtypeStruct((k, n), BF16)],
        [pltpu.VMEM((tk, tn), F32)], (a, b), ride)
    return res[0] if ride is None else (res[0][0], res[1])


def _mix_in(x, n, w_in, cos_t, sin_t, seq, name):
    t, d = x.shape
    per_seq = seq // TM

    def body(x_ref, n_ref, w_ref, c_ref, s_ref, h_ref, q_ref, k_ref, v_ref, g_ref, u_ref):
        xv = x_ref[...]
        r = lax.rsqrt(jnp.mean(xv * xv, axis=-1, keepdims=True) + RMS_EPS)
        h = (xv * r * n_ref[...]).astype(BF16)
        h_ref[...] = h
        p = _dot(h, w_ref[...])
        c, s = c_ref[...], s_ref[...]
        q = p[:, :QK_W]
        k = p[:, QK_W:2 * QK_W]
        q_ref[...] = ((q * c + _swap_halves(q) * s) * (DK ** -0.5)).astype(BF16)
        k_ref[...] = (k * c + _swap_halves(k) * s).astype(BF16)
        v_ref[...] = p[:, 2 * QK_W:2 * QK_W + V_W].astype(BF16)
        g_ref[...] = p[:, 2 * QK_W + V_W:2 * QK_W + 2 * V_W]
        u_ref[...] = p[:, 2 * QK_W + 2 * V_W:]

    tile = lambda w: pl.BlockSpec((TM, w), lambda i: (i, 0))
    return pl.pallas_call(
        body, name=name, grid=(t // TM,),
        in_specs=[tile(d), pl.BlockSpec((1, d), lambda i: (0, 0)), pl.BlockSpec(w_in.shape, lambda i: (0, 0)),
                  pl.BlockSpec((TM, QK_W), lambda i: (i % per_seq, 0)), pl.BlockSpec((TM, QK_W), lambda i: (i % per_seq, 0))],
        out_specs=[tile(d), tile(QK_W), tile(QK_W), tile(V_W), tile(V_W), tile(POOL_W)],
        out_shape=[jax.ShapeDtypeStruct((t, d), BF16), jax.ShapeDtypeStruct((t, QK_W), BF16),
                   jax.ShapeDtypeStruct((t, QK_W), BF16), jax.ShapeDtypeStruct((t, V_W), BF16),
                   jax.ShapeDtypeStruct((t, V_W), F32), jax.ShapeDtypeStruct((t, POOL_W), F32)],
        compiler_params=_cparams(1),
    )(x, n, w_in, cos_t, sin_t)


def _mix_in_bwd(dp, dx2, x1, n, w_in_t, name, ride=None):
    t, d = x1.shape

    def body(dp_ref, dx2_ref, x_ref, n_ref, w_ref, dx_ref, dn_ref, dxb_ref):
        @pl.when(pl.program_id(0) == 0)
        def _():
            dn_ref[...] = jnp.zeros_like(dn_ref)

        dh = _dot(dp_ref[...], w_ref[...])
        xv = x_ref[...]
        r = lax.rsqrt(jnp.mean(xv * xv, axis=-1, keepdims=True) + RMS_EPS)
        xh = xv * r
        dn_ref[...] += jnp.sum(dh * xh, axis=0, keepdims=True)
        dhn = dh * n_ref[...]
        dx = dx2_ref[...] + r * (dhn - xh * jnp.mean(dhn * xh, axis=-1, keepdims=True))
        dx_ref[...] = dx
        dxb_ref[...] = (0.5 * dx).astype(BF16)

    tile = lambda w: pl.BlockSpec((TM, w), lambda i: (i, 0))
    return _pallas(
        body, name, (t // TM,),
        [tile(dp.shape[1]), tile(d), tile(d), pl.BlockSpec((1, d), lambda i: (0, 0)),
         pl.BlockSpec(w_in_t.shape, lambda i: (0, 0))],
        [tile(d), pl.BlockSpec((1, d), lambda i: (0, 0)), tile(d)],
        [jax.ShapeDtypeStruct((t, d), F32), jax.ShapeDtypeStruct((1, d), F32), jax.ShapeDtypeStruct((t, d), BF16)],
        [], (dp, dx2, x1, n, w_in_t), ride)


def _group_norm(o):
    parts, rstds = [], []
    for h in range(HEADS):
        oh = o[:, h * DV:(h + 1) * DV]
        dlt = oh - jnp.mean(oh, axis=-1, keepdims=True)
        rstd = lax.rsqrt(jnp.mean(dlt * dlt, axis=-1, keepdims=True) + GN_EPS)
        parts.append(dlt * rstd)
        rstds.append(rstd)
    return jnp.concatenate(parts, axis=1), rstds


def _mix_core_fwd(qs, k, v, g, u, x1, consts, gain, wp, scale, w_out, nseq, seq, name, ride=None):
    t, d = x1.shape
    nblk = seq // BLK
    mask, dq, dk, gbd, bd = consts

    def body(q_ref, k_ref, v_ref, g_ref, u_ref, x1_ref, m_ref, dq_ref, dk_ref, gbd_ref, bd_ref, gain_ref, wp_ref,
             sc_ref, wo_ref, x2_ref, mix_ref, o_ref, pooled_ref, st_ref, state, halo):
        j = pl.program_id(1)

        @pl.when(j == 0)
        def _():
            state[...] = jnp.zeros_like(state)
            halo[...] = jnp.zeros_like(halo)

        qv, kv, vv = q_ref[...], k_ref[...], v_ref[...]
        st = state[...]
        st_ref[0] = st
        cross = _dot(qv, st.astype(BF16)) * dq_ref[...]
        outs = []
        for h in range(HEADS):
            qh = jnp.where(_head_mask(h), qv, jnp.zeros_like(qv))
            am = (_dot_nt(qh, kv) * m_ref[h]).astype(BF16)
            outs.append(_dot(am, vv[:, h * DV:(h + 1) * DV]))
        o = jnp.concatenate(outs, axis=1) + cross
        o_ref[...] = o
        kd = (kv.astype(F32) * dk_ref[...]).astype(BF16)
        state[...] = gbd_ref[...] * st + _dot_tn(kd, vv) * bd_ref[...]

        gv = g_ref[...]
        nrm, _ = _group_norm(o)
        ret = (gv * _sigmoid(gv)) * (nrm * gain_ref[...])

        uv = u_ref[...]
        c = jnp.concatenate([halo[...], uv], axis=0)
        halo[...] = uv[BLK - HALO:, :]
        pos = j * BLK + lax.broadcasted_iota(jnp.int32, (BLK, 1), 0)
        parts = []
        for gi, w in enumerate(WINDOWS):
            c = c + pltpu.roll(c, w // 2, 0)
            cnt = jnp.minimum(pos + 1, w).astype(F32)
            parts.append(c[HALO:, :GC] / cnt)
            if gi + 1 < len(WINDOWS):
                c = c[:, GC:]
        pooled = (jnp.concatenate(parts, axis=1) - uv).astype(BF16)
        pooled_ref[...] = pooled
        z = jnp.concatenate([_dot(pooled[:, gi * GC:(gi + 1) * GC], wp_ref[gi]) for gi in range(len(WINDOWS))], axis=1)
        mix = jnp.concatenate([ret, z * sc_ref[...]], axis=1).astype(BF16)
        mix_ref[...] = mix
        x2_ref[...] = x1_ref[...] + _dot(mix, wo_ref[...])

    blk = lambda w: pl.BlockSpec((BLK, w), lambda i, j: (i * nblk + j, 0))
    full = lambda a: pl.BlockSpec(a.shape, lambda i, j: (0,) * a.ndim)
    return _pallas(
        body, name, (nseq, nblk),
        [blk(QK_W), blk(QK_W), blk(V_W), blk(V_W), blk(POOL_W), blk(d),
         full(mask), full(dq), full(dk), full(gbd), full(bd), full(gain), full(wp), full(scale), full(w_out)],
        [blk(d), blk(d), blk(V_W), blk(POOL_W), pl.BlockSpec((1, QK_W, V_W), lambda i, j: (i * nblk + j, 0, 0))],
        [jax.ShapeDtypeStruct((t, d), F32), jax.ShapeDtypeStruct((t, d), BF16),
         jax.ShapeDtypeStruct((t, V_W), F32), jax.ShapeDtypeStruct((t, POOL_W), BF16),
         jax.ShapeDtypeStruct((nseq * nblk, QK_W, V_W), F32)],
        [pltpu.VMEM((QK_W, V_W), F32), pltpu.VMEM((HALO, POOL_W), F32)],
        (qs, k, v, g, u, x1, mask, dq, dk, gbd, bd, gain, wp, scale, w_out), ride)


def _mix_core_bwd(dx2, qs, k, v, g, o, pooled, st, consts, gain, wp, scale, w_out, cos_t, sin_t, nseq, seq, name,
                  ride=None):
    t, d = dx2.shape
    nblk = seq // BLK
    mask, dq, dk, gbd, bd = consts
    n_win = len(WINDOWS)

    def body(dx2_ref, q_ref, k_ref, v_ref, g_ref, o_ref, pooled_ref, st_ref, m_ref, dq_ref, dk_ref, gbd_ref, bd_ref,
             gain_ref, wp_ref, sc_ref, wo_ref, c_ref, s_ref,
             dp_ref, dx2b_ref, dgain_ref, dscale_ref, dwp_ref, rstate, carry):
        i, j = pl.program_id(0), pl.program_id(1)

        @pl.when((i == 0) & (j == 0))
        def _():
            dgain_ref[...] = jnp.zeros_like(dgain_ref)
            dscale_ref[...] = jnp.zeros_like(dscale_ref)
            dwp_ref[...] = jnp.zeros_like(dwp_ref)

        @pl.when(j == 0)
        def _():
            rstate[...] = jnp.zeros_like(rstate)
            carry[...] = jnp.zeros_like(carry)

        dx2b = dx2_ref[...].astype(BF16)
        dx2b_ref[...] = dx2b
        dmix = _dot(dx2b, wo_ref[...])
        dret, dpool = dmix[:, :V_W], dmix[:, V_W:]

        gv, ov, gain_v = g_ref[...], o_ref[...], gain_ref[...]
        sg = _sigmoid(gv)
        sil = gv * sg
        nrm, rstds = _group_norm(ov)
        dg = dret * (nrm * gain_v) * (sg * (1.0 + gv * (1.0 - sg)))
        dgn = dret * sil
        dgain_ref[...] += jnp.sum(dgn * nrm, axis=0, keepdims=True)
        dnrm = dgn * gain_v
        do_parts = []
        for h in range(HEADS):
            dn_h = dnrm[:, h * DV:(h + 1) * DV]
            n_h = nrm[:, h * DV:(h + 1) * DV]
            do_parts.append(rstds[h] * (dn_h - jnp.mean(dn_h, axis=-1, keepdims=True)
                                        - n_h * jnp.mean(dn_h * n_h, axis=-1, keepdims=True)))
        do = jnp.concatenate(do_parts, axis=1)
        dob = do.astype(BF16)

        qv, kv, vv = q_ref[...], k_ref[...], v_ref[...]
        stb = st_ref[0].astype(BF16)
        rs = rstate[...]
        rsb = rs.astype(BF16)
        dod = (do * dq_ref[...]).astype(BF16)
        dqs = _dot_nt(dod, stb)
        dst = _dot_tn(qv, dod) * bd_ref[...]
        dkf = dk_ref[...]
        kd = (kv.astype(F32) * dkf).astype(BF16)
        dks = _dot_nt(vv, rsb) * dkf
        dvs = _dot(kd, rsb)
        dv_parts = []
        for h in range(HEADS):
            hm = _head_mask(h)
            qh = jnp.where(hm, qv, jnp.zeros_like(qv))
            mh = m_ref[h]
            am = (_dot_nt(qh, kv) * mh).astype(BF16)
            dpm = (_dot_nt(dob[:, h * DV:(h + 1) * DV], vv[:, h * DV:(h + 1) * DV]) * mh).astype(BF16)
            dqs = dqs + jnp.where(hm, _dot(dpm, kv), 0.0)
            dks = dks + jnp.where(hm, _dot_tn(dpm, qv), 0.0)
            dv_parts.append(_dot_tn(am, dob[:, h * DV:(h + 1) * DV]))
        dvs = dvs + jnp.concatenate(dv_parts, axis=1)
        rstate[...] = dst + gbd_ref[...] * rs

        cv, sv = c_ref[...], s_ref[...]
        dqr = dqs * (DK ** -0.5)
        dq_pre = dqr * cv + _swap_halves(dqr * sv)
        dk_pre = dks * cv + _swap_halves(dks * sv)

        pv = pooled_ref[...]
        sc = sc_ref[...]
        dzb = (dpool * sc).astype(BF16)
        z_parts, dpo_parts = [], []
        for gi in range(n_win):
            p_g = pv[:, gi * GC:(gi + 1) * GC]
            dz_g = dzb[:, gi * GC:(gi + 1) * GC]
            z_parts.append(_dot(p_g, wp_ref[gi]))
            dwp_ref[gi] += _dot_tn(p_g, dz_g)
            dpo_parts.append(_dot_nt(dz_g, wp_ref[gi]))
        dscale_ref[...] += jnp.sum(dpool * jnp.concatenate(z_parts, axis=1), axis=0, keepdims=True)
        dpo = jnp.concatenate(dpo_parts, axis=1)
        pos = (nblk - 1 - j) * BLK + lax.broadcasted_iota(jnp.int32, (BLK, 1), 0)
        e = jnp.concatenate(
            [dpo[:, gi * GC:(gi + 1) * GC] / jnp.minimum(pos + 1, w).astype(F32) for gi, w in enumerate(WINDOWS)], axis=1)
        c = jnp.concatenate([e, carry[...]], axis=0)
        carry[...] = e[:HALO, :]
        rows = BLK + HALO
        lead = []
        for gi, w in enumerate(WINDOWS):
            c = c + pltpu.roll(c, rows - w // 2, 0)
            lead.append(c[:BLK, :GC])
            if gi + 1 < n_win:
                c = c[:, GC:]
        du = jnp.concatenate(lead, axis=1) - dpo

        dp_ref[:, 0:QK_W] = dq_pre.astype(BF16)
        dp_ref[:, QK_W:2 * QK_W] = dk_pre.astype(BF16)
        dp_ref[:, 2 * QK_W:2 * QK_W + V_W] = dvs.astype(BF16)
        dp_ref[:, 2 * QK_W + V_W:2 * QK_W + 2 * V_W] = dg.astype(BF16)
        dp_ref[:, 2 * QK_W + 2 * V_W:] = du.astype(BF16)

    rev = lambda i, j: i * nblk + (nblk - 1 - j)
    blk = lambda w: pl.BlockSpec((BLK, w), lambda i, j: (rev(i, j), 0))
    full = lambda a: pl.BlockSpec(a.shape, lambda i, j: (0,) * a.ndim)
    in_w = 2 * QK_W + 2 * V_W + POOL_W
    return _pallas(
        body, name, (nseq, nblk),
        [blk(d), blk(QK_W), blk(QK_W), blk(V_W), blk(V_W), blk(V_W), blk(POOL_W),
         pl.BlockSpec((1, QK_W, V_W), lambda i, j: (rev(i, j), 0, 0)),
         full(mask), full(dq), full(dk), full(gbd), full(bd), full(gain), full(wp), full(scale), full(w_out),
         pl.BlockSpec((BLK, QK_W), lambda i, j: (nblk - 1 - j, 0)),
         pl.BlockSpec((BLK, QK_W), lambda i, j: (nblk - 1 - j, 0))],
        [blk(in_w), blk(d), pl.BlockSpec((1, V_W), lambda i, j: (0, 0)),
         pl.BlockSpec((1, POOL_W), lambda i, j: (0, 0)), pl.BlockSpec((n_win, GC, GC), lambda i, j: (0, 0, 0))],
        [jax.ShapeDtypeStruct((t, in_w), BF16), jax.ShapeDtypeStruct((t, d), BF16),
         jax.ShapeDtypeStruct((1, V_W), F32), jax.ShapeDtypeStruct((1, POOL_W), F32),
         jax.ShapeDtypeStruct((n_win, GC, GC), F32)],
        [pltpu.VMEM((QK_W, V_W), F32), pltpu.VMEM((HALO, POOL_W), F32)],
        (dx2, qs, k, v, g, o, pooled, st, mask, dq, dk, gbd, bd, gain, wp, scale, w_out, cos_t, sin_t), ride)


def _loss_head(x3, nf, tgt, name):
    t, d = x3.shape

    def body(x_ref, n_ref, t_ref, dx_ref, dn_ref, loss_ref, dxb_ref):
        @pl.when(pl.program_id(0) == 0)
        def _():
            dn_ref[...] = jnp.zeros_like(dn_ref)
            loss_ref[...] = jnp.zeros_like(loss_ref)

        xv = x_ref[...]
        nv = n_ref[...]
        r = lax.rsqrt(jnp.mean(xv * xv, axis=-1, keepdims=True) + RMS_EPS)
        xh = xv * r
        err = xh * nv - t_ref[...]
        row = jnp.mean(err * err, axis=-1, keepdims=True)
        loss_ref[...] += 0.5 * jnp.sum(row, axis=0, keepdims=True)
        dy = err * (1.0 / d)
        dn_ref[...] += jnp.sum(dy * xh, axis=0, keepdims=True)
        dxh = dy * nv
        dx = r * (dxh - xh * jnp.mean(dxh * xh, axis=-1, keepdims=True))
        dx_ref[...] = dx
        dxb_ref[...] = (0.5 * dx).astype(BF16)

    tile = pl.BlockSpec((TM, d), lambda i: (i, 0))
    return pl.pallas_call(
        body, name=name, grid=(t // TM,),
        in_specs=[tile, pl.BlockSpec((1, d), lambda i: (0, 0)), tile],
        out_specs=[tile, pl.BlockSpec((1, d), lambda i: (0, 0)), pl.BlockSpec((1, 1), lambda i: (0, 0)), tile],
        out_shape=[jax.ShapeDtypeStruct((t, d), F32), jax.ShapeDtypeStruct((1, d), F32), jax.ShapeDtypeStruct((1, 1), F32),
                   jax.ShapeDtypeStruct((t, d), BF16)],
        compiler_params=_cparams(1),
    )(x3, nf, tgt)


def _coords():
    return lax.axis_index("x"), lax.axis_index("y"), lax.axis_index("c")


def _window(ref, kind, idx, size):
    if kind == "col":
        return ref.at[:, pl.ds(pl.multiple_of(idx * size, LANE), size)]
    return ref.at[pl.ds(pl.multiple_of(idx * size, 8), size), :]


def _run_exchange(ex, name):
    n_in = len(ex.inputs)

    def body(*refs):
        ins, outs, sems = refs[:n_in], refs[n_in:n_in + len(ex.out_shape)], refs[n_in + len(ex.out_shape):]
        ex.start(ins, outs, sems)
        if ex.mid is not None:
            ex.mid(ins, outs, sems)
        ex.finish(ins, outs, sems)

    return pl.pallas_call(body, name=name, in_specs=[ANY] * n_in, out_specs=[ANY] * len(ex.out_shape),
                          out_shape=ex.out_shape, scratch_shapes=ex.scratch)(*ex.inputs)


def _join(exchanges):
    bounds = []
    i0 = o0 = s0 = 0
    for ex in exchanges:
        bounds.append((i0, o0, s0))
        i0, o0, s0 = i0 + len(ex.inputs), o0 + len(ex.out_shape), s0 + len(ex.scratch)

    def phase(which):
        def run(ins, outs, sems):
            for ex, (i, o, s) in zip(exchanges, bounds):
                fn = getattr(ex, which)
                if fn is not None:
                    fn(ins[i:i + len(ex.inputs)], outs[o:o + len(ex.out_shape)], sems[s:s + len(ex.scratch)])
        return run

    return _Exchange(sum((ex.inputs for ex in exchanges), []), sum((ex.out_shape for ex in exchanges), []),
                     sum((ex.scratch for ex in exchanges), []), phase("start"), phase("finish"),
                     phase("mid") if any(ex.mid is not None for ex in exchanges) else None)


def _gather_exchange(parts):
    n = len(parts)
    kinds = [kd for _, kd in parts]
    sizes = [a.shape[1] if kd == "col" else a.shape[0] for a, kd in parts]

    def plan(ins, outs, sems):
        send_sems, recv_sems, local_sems = sems
        x, y, c = _coords()
        me, sibling = (x, y, c), (x, y, 1 - c)
        chips = [(1 - x, y), (x, 1 - y), (1 - x, 1 - y)]

        def win(p, dev):
            return _window(outs[p], kinds[p], 4 * dev[0] + 2 * dev[1] + dev[2], sizes[p])

        def copy(p, k, block, to, src=None):
            return pltpu.make_async_remote_copy(
                src_ref=win(p, block) if src is None else src, dst_ref=win(p, block),
                send_sem=send_sems.at[p * 7 + k], recv_sem=recv_sems.at[p * 7 + k], device_id=to, device_id_type=MESH_ID)

        mine = [pltpu.make_async_copy(ins[p], win(p, me), local_sems.at[p]) for p in range(n)]
        first, arrived, passed, rest = [], [], [], []
        for p in range(n):
            first.append(copy(p, 0, me, sibling, src=ins[p]))
            first += [copy(p, 1 + q, me, (*chip, c), src=ins[p]) for q, chip in enumerate(chips)]
            rest.append(copy(p, 0, sibling, me))
            rest += [copy(p, 4 + q, (*chip, 1 - c), me) for q, chip in enumerate(chips)]
        for q, chip in enumerate(chips):
            for p in range(n):
                arrived.append(copy(p, 1 + q, (*chip, c), me))
                passed.append(copy(p, 4 + q, (*chip, c), sibling))
        return mine, first, arrived, passed, rest

    def start(ins, outs, sems):
        mine, first, _, _, _ = plan(ins, outs, sems)
        for cp in mine + first:
            cp.start()

    def mid(ins, outs, sems):
        _, _, arrived, passed, _ = plan(ins, outs, sems)
        for got, fwd in zip(arrived, passed):
            got.wait_recv()
            fwd.start()

    def finish(ins, outs, sems):
        mine, first, _, passed, rest = plan(ins, outs, sems)
        for cp in rest:
            cp.wait_recv()
        for cp in first + passed:
            cp.wait_send()
        for cp in mine:
            cp.wait()

    out_shape = [jax.ShapeDtypeStruct((a.shape[0], N_DEV * a.shape[1]) if kd == "col" else (N_DEV * a.shape[0], a.shape[1]),
                                      a.dtype) for a, kd in parts]
    scratch = [pltpu.SemaphoreType.DMA((7 * n,)), pltpu.SemaphoreType.DMA((7 * n,)), pltpu.SemaphoreType.DMA((n,))]
    return _Exchange([a for a, _ in parts], out_shape, scratch, start, finish, mid)


def _all_gather(parts, name):
    return _run_exchange(_gather_exchange(parts), name)


def _shard_shape(a, kd):
    return (a.shape[0], a.shape[1] // N_DEV) if kd == "col" else (a.shape[0] // N_DEV, a.shape[1])


def _symmetric_exchange(inputs, out_shape, n_copies, plan):
    def start(ins, outs, sems):
        for cp in plan(ins, outs, sems):
            cp.start()

    def finish(ins, outs, sems):
        copies = plan(ins, outs, sems)
        for cp in copies:
            cp.wait_recv()
        for cp in copies:
            cp.wait_send()

    scratch = [pltpu.SemaphoreType.DMA((n_copies,)), pltpu.SemaphoreType.DMA((n_copies,))]
    return _Exchange(inputs, out_shape, scratch, start, finish)


def _rs_pair_exchange(grads):
    n = len(grads)
    kinds = [kd for _, kd in grads]
    shapes = [_shard_shape(a, kd) for a, kd in grads]

    def plan(ins, outs, sems):
        send_sems, recv_sems = sems
        x, y, c = _coords()
        copies = []
        for p in range(n):
            size = shapes[p][1] if kinds[p] == "col" else shapes[p][0]
            for s in range(4):
                src = _window(ins[p], kinds[p], 2 * s + (1 - c), size)
                copies.append(pltpu.make_async_remote_copy(
                    src_ref=src, dst_ref=outs[p].at[s], send_sem=send_sems.at[4 * p + s], recv_sem=recv_sems.at[4 * p + s],
                    device_id=(x, y, 1 - c), device_id_type=MESH_ID))
        return copies

    return _symmetric_exchange([a for a, _ in grads], [jax.ShapeDtypeStruct((4,) + shapes[p], BF16) for p in range(n)],
                               4 * n, plan)


def _rs_chips_exchange(sums):
    n = len(sums)

    def plan(ins, outs, sems):
        send_sems, recv_sems = sems
        x, y, c = _coords()
        chips = [(1 - x, y), (x, 1 - y), (1 - x, 1 - y)]
        copies = []
        for p in range(n):
            for q, (cx, cy) in enumerate(chips):
                copies.append(pltpu.make_async_remote_copy(
                    src_ref=ins[p].at[2 * cx + cy], dst_ref=outs[p].at[q],
                    send_sem=send_sems.at[3 * p + q], recv_sem=recv_sems.at[3 * p + q],
                    device_id=(cx, cy, c), device_id_type=MESH_ID))
        return copies

    return _symmetric_exchange(list(sums), [jax.ShapeDtypeStruct((3,) + a.shape[1:], BF16) for a in sums], 3 * n, plan)


def _rs_pair(grads, name):
    return _run_exchange(_rs_pair_exchange(grads), name)


def _rs_chips(sums, name):
    return _run_exchange(_rs_chips_exchange(sums), name)


def _pair_sum(grad, kd, recv, core, name):
    _, r, cw = recv.shape
    tr = min(r, TM)

    def body(core_ref, g_ref, r_ref, o_ref):
        del core_ref
        o_ref[0] = (g_ref[...].astype(F32) + r_ref[0].astype(F32)).astype(BF16)

    if kd == "col":
        g_spec = pl.BlockSpec((tr, cw), lambda s, i, core_ref: (i, 2 * s + core_ref[0]))
    else:
        g_spec = pl.BlockSpec((tr, cw), lambda s, i, core_ref: ((2 * s + core_ref[0]) * (r // tr) + i, 0))
    grid_spec = pltpu.PrefetchScalarGridSpec(
        num_scalar_prefetch=1, grid=(4, r // tr),
        in_specs=[g_spec, pl.BlockSpec((1, tr, cw), lambda s, i, core_ref: (s, i, 0))],
        out_specs=pl.BlockSpec((1, tr, cw), lambda s, i, core_ref: (s, i, 0)))
    return pl.pallas_call(
        body, name=name, grid_spec=grid_spec, out_shape=jax.ShapeDtypeStruct(recv.shape, BF16),
        compiler_params=_cparams(2),
    )(core, grad, recv)


def _adam_math(w, g, m, v):
    m2 = B1 * m + (1.0 - B1) * g
    v2 = B2 * v + (1.0 - B2) * (g * g)
    m_hat = m2 / (1.0 - B1 ** STEP)
    v_hat = v2 / (1.0 - B2 ** STEP)
    delta = -LR * (m_hat / (jnp.sqrt(v_hat) + ADAM_EPS) + WD * w)
    return delta, m2, v2


def _chip_sum_adam(psum, recv, chip, w, m, v, name):
    r, cw = w.shape
    pc = psum.shape[2]
    tr = min(r, TM)

    def body(chip_ref, p_ref, r_ref, w_ref, m_ref, v_ref, g_ref, d_ref, m2_ref, v2_ref):
        del chip_ref
        g = p_ref[0].astype(F32) + r_ref[0].astype(F32) + r_ref[1].astype(F32) + r_ref[2].astype(F32)
        g = g[:, :cw]
        delta, m2, v2 = _adam_math(w_ref[...], g, m_ref[...], v_ref[...])
        g_ref[...] = g
        d_ref[...] = delta
        m2_ref[...] = m2
        v2_ref[...] = v2

    loc = pl.BlockSpec((tr, cw), lambda i, chip_ref: (i, 0))
    grid_spec = pltpu.PrefetchScalarGridSpec(
        num_scalar_prefetch=1, grid=(r // tr,),
        in_specs=[pl.BlockSpec((1, tr, pc), lambda i, chip_ref: (chip_ref[0], i, 0)),
                  pl.BlockSpec((3, tr, pc), lambda i, chip_ref: (0, i, 0)), loc, loc, loc],
        out_specs=[loc, loc, loc, loc])
    return pl.pallas_call(
        body, name=name, grid_spec=grid_spec, out_shape=[jax.ShapeDtypeStruct((r, cw), F32)] * 4,
        compiler_params=_cparams(1),
    )(chip, psum, recv, w, m, v)


def _small_allreduce_adam(partials, params, moms, vels, plain, name, ride=None):
    n, n_plain = len(partials), len(plain)
    summed = list(partials) + list(plain)
    row0 = []
    rows = 0
    for a in summed:
        if a.shape[0] >= 8:
            rows = _pad_to(rows, 8)
        row0.append(rows)
        rows += a.shape[0]
    rows = _pad_to(rows, 8)
    width = max(a.shape[1] for a in summed)
    r_in = 0 if ride is None else len(ride.inputs)
    r_out = 0 if ride is None else len(ride.out_shape)
    n_out = 4 * n + n_plain

    def body(*refs):
        w_in, m_in, v_in = refs[0:n], refs[n:2 * n], refs[2 * n:3 * n]
        g_in, refs = refs[3 * n:4 * n + n_plain], refs[4 * n + n_plain:]
        r_ins, refs = refs[:r_in], refs[r_in:]
        outs, refs = refs[:n_out], refs[n_out:]
        r_outs, refs = refs[:r_out], refs[r_out:]
        pair, chips, send_sems, recv_sems = refs[:4]
        if ride is not None:
            ride.start(r_ins, r_outs, refs[4:])
        x, y, c = _coords()
        chip = 2 * x + y
        pair[c] = jnp.zeros((rows, width), F32)
        for p, a in enumerate(summed):
            r, cw = a.shape
            pair[c, row0[p]:row0[p] + r, 0:cw] = g_in[p][...]
        swap = pltpu.make_async_remote_copy(src_ref=pair.at[c], dst_ref=pair.at[c], send_sem=send_sems.at[0],
                                            recv_sem=recv_sems.at[0], device_id=(x, y, 1 - c), device_id_type=MESH_ID)
        swap.start()
        swap.wait_recv()
        swap.wait_send()
        chips[chip] = pair[0] + pair[1]
        copies = [pltpu.make_async_remote_copy(
            src_ref=chips.at[chip], dst_ref=chips.at[chip], send_sem=send_sems.at[1 + q], recv_sem=recv_sems.at[1 + q],
            device_id=(cx, cy, c), device_id_type=MESH_ID) for q, (cx, cy) in enumerate([(1 - x, y), (x, 1 - y), (1 - x, 1 - y)])]
        for cp in copies:
            cp.start()
        for cp in copies:
            cp.wait_recv()
        for cp in copies:
            cp.wait_send()
        for p, a in enumerate(summed):
            r, cw = a.shape
            g = chips[0, row0[p]:row0[p] + r, 0:cw]
            for q in range(1, 4):
                g = g + chips[q, row0[p]:row0[p] + r, 0:cw]
            if p >= n:
                outs[4 * n + p - n][...] = g
                continue
            delta, m2, v2 = _adam_math(w_in[p][...], g, m_in[p][...], v_in[p][...])
            outs[4 * p][...] = g
            outs[4 * p + 1][...] = delta
            outs[4 * p + 2][...] = m2
            outs[4 * p + 3][...] = v2
        if ride is not None:
            ride.finish(r_ins, r_outs, refs[4:])

    out_shape = []
    for a in partials:
        out_shape += [jax.ShapeDtypeStruct(a.shape, F32)] * 4
    out_shape += [jax.ShapeDtypeStruct(a.shape, F32) for a in plain]
    res = pl.pallas_call(
        body, name=name, in_specs=[VMEM_SPEC] * (4 * n + n_plain) + [ANY] * r_in,
        out_specs=[VMEM_SPEC] * n_out + [ANY] * r_out, out_shape=out_shape + ([] if ride is None else ride.out_shape),
        scratch_shapes=[pltpu.VMEM((2, rows, width), F32), pltpu.VMEM((4, rows, width), F32),
                        pltpu.SemaphoreType.DMA((4,)), pltpu.SemaphoreType.DMA((4,))] + ([] if ride is None else ride.scratch),
    )(*params, *moms, *vels, *partials, *plain, *([] if ride is None else ride.inputs))
    return res if ride is None else (res[:n_out], res[n_out:])


def _local_step(xf, tgt, nseq, seq, cols1_all, later, small_w, core=None, small_step=None):
    d = xf.shape[1]
    dist = core is not None
    n1, n2, gain, pool_w, pool_scale, n3, nf = small_w
    tf = 2 * cols1_all.shape[1] // N_DEV
    consts = _retention_constants()
    cos_t, sin_t = _rotary_tables(seq)
    wp_b = pool_w.astype(BF16)

    def pair_sums(grads, recv, names):
        return [_pair_sum(g, kd, r, core, "pair_sum_" + nm) for (g, kd), r, nm in zip(grads, recv, names)]

    def riding(host):
        return _gather_exchange(later[host])

    act1 = (xf, n1, (cols1_all, 0), (cols1_all, 1), "ffn1_act")
    if dist:
        (h1, b1, sil1, dsil1, s1), (d1_all, win_all, wout_all, gate2_all) = _ffn_act(*act1, ride=riding("ffn1_act"))
        x1, (up2_all,) = _ffn_down(s1, xf, d1_all, "ffn1_down", ride=riding("ffn1_down"))
    else:
        d1_all, win_all, wout_all, gate2_all, up2_all, d2_all = (later[k] for k in ("down1", "w_in", "w_out", "gate2",
                                                                                   "up2", "down2"))
        h1, b1, sil1, dsil1, s1 = _ffn_act(*act1)
        x1 = _ffn_down(s1, xf, d1_all, "ffn1_down")
    h2, qs, kr, vv, gg, uu = _mix_in(x1, n2, win_all, cos_t, sin_t, seq, "mix_in")
    fwd_mix = (qs, kr, vv, gg, uu, x1, consts, gain, wp_b, pool_scale, wout_all, nseq, seq, "mix_core_fwd")
    if dist:
        (x2, mix, oo, pooled, states), (d2_all,) = _mix_core_fwd(*fwd_mix, ride=riding("mix_core_fwd"))
        (x3, h3, b3, sil3, dsil3), (cols2_t, d2_t, win_t, wout_t) = _ffn_fwd(
            x2, n3, (gate2_all, 0), (up2_all, 0), d2_all, "ffn2_fwd", ride=riding("ffn2_fwd"))
    else:
        x2, mix, oo, pooled, states = _mix_core_fwd(*fwd_mix)
        x3, h3, b3, sil3, dsil3 = _ffn_fwd(x2, n3, (gate2_all, 0), (up2_all, 0), d2_all, "ffn2_fwd")
        cols1_t = _transpose(cols1_all, "transpose_cols1")
        cols2_t = jnp.concatenate([_transpose(gate2_all, "transpose_gate2"), _transpose(up2_all, "transpose_up2")], axis=1)
        d1_t = _transpose(d1_all, "transpose_down1")
        d2_t = _transpose(d2_all, "transpose_down2")
        win_t = _transpose(win_all, "transpose_w_in")
        wout_t = _transpose(wout_all, "transpose_w_out")
    dx3, dnf, loss_part, dx3b = _loss_head(x3, nf, tgt, "loss_head")
    out = {}

    if dist:
        (da3, db3, g_wd2), (cols1_t, d1_t) = _ffn_bwd_act(dx3b, b3, sil3, dsil3, d2_t, "ffn2_bwd_act",
                                                          ride=riding("ffn2_bwd_act"))
    else:
        da3, db3, g_wd2 = _ffn_bwd_act(dx3b, b3, sil3, dsil3, d2_t, "ffn2_bwd_act")
    names2 = ["ffn2_gate", "ffn2_up", "ffn2_down"]
    grads2 = [(_wgrad(da3, h3, 1.0, tf, d, "wgrad_gate2"), "row"), (_wgrad(db3, h3, 1.0, tf, d, "wgrad_up2"), "row"),
              (g_wd2, "row")]
    if dist:
        (dx2, dn3), recv2 = _ffn_bwd_in(da3, db3, dx3, x2, n3, cols2_t, 0, "ffn2_bwd_in", ride=_rs_pair_exchange(grads2))
        sums2 = pair_sums(grads2, recv2, names2)
        (dp, dx2b, dgain, dscale, dwp), crecv2 = _mix_core_bwd(
            dx2, qs, kr, vv, gg, oo, pooled, states, consts, gain, wp_b, pool_scale, wout_t, cos_t, sin_t, nseq, seq,
            "mix_core_bwd", ride=_rs_chips_exchange(sums2))
        out.update({nm: (s, r) for nm, s, r in zip(names2, sums2, crecv2)})
    else:
        dx2, dn3 = _ffn_bwd_in(da3, db3, dx3, x2, n3, cols2_t, 0, "ffn2_bwd_in")
        dp, dx2b, dgain, dscale, dwp = _mix_core_bwd(dx2, qs, kr, vv, gg, oo, pooled, states, consts, gain, wp_b,
                                                     pool_scale, wout_t, cos_t, sin_t, nseq, seq, "mix_core_bwd")
        out.update(dict(zip(names2, grads2)))

    names_m = ["w_in", "w_out"]
    grads_m = [(_wgrad(h2, dp, 1.0, d, d, "wgrad_in"), "col"), (_wgrad(mix, dx2b, 1.0, d, d, "wgrad_out"), "row")]
    if dist:
        (dx1, dn2, dx1b), recv_m = _mix_in_bwd(dp, dx2, x1, n2, win_t, "mix_in_bwd", ride=_rs_pair_exchange(grads_m))
        sums_m = pair_sums(grads_m, recv_m, names_m)
        (da1, db1, g_wd1), crecv_m = _ffn_bwd_act(dx1b, b1, sil1, dsil1, d1_t, "ffn1_bwd_act",
                                                  ride=_rs_chips_exchange(sums_m))
        out.update({nm: (s, r) for nm, s, r in zip(names_m, sums_m, crecv_m)})
    else:
        dx1, dn2, dx1b = _mix_in_bwd(dp, dx2, x1, n2, win_t, "mix_in_bwd")
        da1, db1, g_wd1 = _ffn_bwd_act(dx1b, b1, sil1, dsil1, d1_t, "ffn1_bwd_act")
        out.update(dict(zip(names_m, grads_m)))

    dx0, dn1 = _ffn_bwd_in(da1, db1, dx1, xf, n1, cols1_t, 0, "ffn1_bwd_in")
    small_parts = (dn1, dn2, dgain, dwp, dscale, dn3, dnf)
    g_down = (g_wd1, "row")
    if dist:
        g_gate, recv_d = _wgrad(da1, h1, 1.0, tf, d, "wgrad_gate1", ride=_rs_pair_exchange([g_down]))
        g_gate = (g_gate, "row")
        sum_d = pair_sums([g_down], recv_d, ["ffn1_down"])
        g_up, (crecv_d, recv_g) = _wgrad(db1, h1, 1.0, tf, d, "wgrad_up1",
                                         ride=_join([_rs_chips_exchange(sum_d), _rs_pair_exchange([g_gate])]))
        g_up = (g_up, "row")
        sum_g = pair_sums([g_gate], [recv_g], ["ffn1_gate"])
        small_out, (crecv_g, recv_u) = small_step(small_parts, loss_part,
                                                  _join([_rs_chips_exchange(sum_g), _rs_pair_exchange([g_up])]))
        sum_u = pair_sums([g_up], [recv_u], ["ffn1_up"])
        (crecv_u,) = _run_exchange(_rs_chips_exchange(sum_u), "rs_tail")
        out.update({"ffn1_gate": (sum_g[0], crecv_g), "ffn1_up": (sum_u[0], crecv_u), "ffn1_down": (sum_d[0], crecv_d)})
        return small_out[-1], dx0, out, small_out[:-1]
    out.update({"ffn1_gate": (_wgrad(da1, h1, 1.0, tf, d, "wgrad_gate1"), "row"),
                "ffn1_up": (_wgrad(db1, h1, 1.0, tf, d, "wgrad_up1"), "row"), "ffn1_down": g_down})
    return loss_part, dx0, out, small_parts


def kernel(x, norm_ffn1, ffn1_gate, ffn1_up, ffn1_down, norm_mix, w_in, ret_gn_gain, pool_w, pool_scale, w_out, norm_ffn2, ffn2_gate, ffn2_up, ffn2_down, norm_final, loss_target, m_norm_ffn1, m_ffn1_gate, m_ffn1_up, m_ffn1_down, m_norm_mix, m_w_in, m_ret_gn_gain, m_pool_w, m_pool_scale, m_w_out, m_norm_ffn2, m_ffn2_gate, m_ffn2_up, m_ffn2_down, m_norm_final, v_norm_ffn1, v_ffn1_gate, v_ffn1_up, v_ffn1_down, v_norm_mix, v_w_in, v_ret_gn_gain, v_pool_w, v_pool_scale, v_w_out, v_norm_ffn2, v_ffn2_gate, v_ffn2_up, v_ffn2_down, v_norm_final):
    nseq, seq, d = x.shape
    t = nseq * seq
    f_loc = ffn1_gate.shape[2]
    f_pad = _pad_to(f_loc, LANE)
    xf = x.reshape(t, d)
    tgt = loss_target.reshape(t, d)
    core = lax.axis_index("c").astype(jnp.int32).reshape(1)
    chip = (2 * lax.axis_index("x") + lax.axis_index("y")).astype(jnp.int32).reshape(1)

    colp = lambda w: jnp.pad(w[0].astype(BF16), ((0, 0), (0, f_pad - f_loc)))
    rowp = lambda w: jnp.pad(w[0].astype(BF16), ((0, f_pad - f_loc), (0, 0)))
    gate2, up2 = colp(ffn2_gate), colp(ffn2_up)
    cols1 = jnp.concatenate([colp(ffn1_gate), colp(ffn1_up)], axis=0)
    cols2_t = jnp.concatenate([gate2.T, up2.T], axis=1)
    (cols1_all,) = _all_gather([(cols1, "col")], "all_gather_ffn1")
    d1_loc, d2_loc, win_loc, wout_loc = rowp(ffn1_down), rowp(ffn2_down), w_in[0].astype(BF16), w_out[0].astype(BF16)
    later = {"ffn1_act": [(d1_loc, "row"), (win_loc, "col"), (wout_loc, "row"), (gate2, "col")],
             "ffn1_down": [(up2, "col")], "mix_core_fwd": [(d2_loc, "row")],
             "ffn2_fwd": [(cols2_t, "row"), (d2_loc.T, "col"), (win_loc.T, "row"), (wout_loc.T, "col")],
             "ffn2_bwd_act": [(cols1.T, "row"), (d1_loc.T, "col")]}

    flat = lambda a: a.reshape(pool_w.size // d, d)
    params = [norm_ffn1, norm_mix, ret_gn_gain, flat(pool_w), pool_scale, norm_ffn2, norm_final.reshape(1, d)]
    moms = [m_norm_ffn1, m_norm_mix, m_ret_gn_gain, flat(m_pool_w), m_pool_scale, m_norm_ffn2, m_norm_final.reshape(1, d)]
    vels = [v_norm_ffn1, v_norm_mix, v_ret_gn_gain, flat(v_pool_w), v_pool_scale, v_norm_ffn2, v_norm_final.reshape(1, d)]

    def small_step(parts, loss_part, ride):
        dn1, dn2, dgain, dwp, dscale, dn3, dnf = parts
        return _small_allreduce_adam([dn1, dn2, dgain, flat(dwp), dscale, dn3, dnf], params, moms, vels, [loss_part],
                                     "small_allreduce_adam", ride)

    small_w = (norm_ffn1, norm_mix, ret_gn_gain, pool_w[0], pool_scale, norm_ffn2, norm_final.reshape(1, d))
    loss_sum, dx0, reduced, small_out = _local_step(xf, tgt, nseq, seq, cols1_all, later, small_w, core, small_step)

    local = {"ffn1_gate": (ffn1_gate, m_ffn1_gate, v_ffn1_gate), "ffn1_up": (ffn1_up, m_ffn1_up, v_ffn1_up),
             "ffn1_down": (ffn1_down, m_ffn1_down, v_ffn1_down), "w_in": (w_in, m_w_in, v_w_in),
             "w_out": (w_out, m_w_out, v_w_out), "ffn2_gate": (ffn2_gate, m_ffn2_gate, v_ffn2_gate),
             "ffn2_up": (ffn2_up, m_ffn2_up, v_ffn2_up), "ffn2_down": (ffn2_down, m_ffn2_down, v_ffn2_down)}
    big = {}
    for nm, (w, m, v) in local.items():
        ps, rcv = reduced[nm]
        flip = nm.endswith("gate") or nm.endswith("up")
        view = (lambda a: a[0].T) if flip else (lambda a: a[0])
        res = _chip_sum_adam(ps, rcv, chip, view(w), view(m), view(v), "adam_" + nm)
        big[nm] = tuple((a.T if flip else a)[None] for a in res)

    small_names = ["norm_ffn1", "norm_mix", "ret_gn_gain", "pool_w", "pool_scale", "norm_ffn2", "norm_final"]
    shapes = [norm_ffn1.shape, norm_mix.shape, ret_gn_gain.shape, pool_w.shape, pool_scale.shape, norm_ffn2.shape,
              norm_final.shape]
    small = {nm: tuple(small_out[4 * p + q].reshape(shapes[p]) for q in range(4)) for p, nm in enumerate(small_names)}

    loss = loss_sum[0, 0]
    order = ["norm_ffn1", "ffn1_gate", "ffn1_up", "ffn1_down", "norm_mix", "w_in", "ret_gn_gain", "pool_w", "pool_scale",
             "w_out", "norm_ffn2", "ffn2_gate", "ffn2_up", "ffn2_down", "norm_final"]
    both = {**big, **small}
    outs = [loss, dx0.reshape(nseq, seq, d)]
    for q in range(4):
        outs += [both[nm][q] for nm in order]
    return tuple(outs)
```

```python
import functools

import numpy as np
import jax
import jax.numpy as jnp
from jax import lax
from jax.experimental import pallas as pl
from jax.experimental.pallas import tpu as pltpu

F32, BF16 = jnp.float32, jnp.bfloat16
MESH_ID = pl.DeviceIdType.MESH
ANY = pl.BlockSpec(memory_space=pl.ANY)
VMEM_SPEC = pl.BlockSpec(memory_space=pltpu.VMEM)

N_DEV = 8
RMS_EPS = 1e-6
GN_EPS = 1e-5
HEADS, DK, DV = 4, 64, 128
QK_W, V_W, POOL_W = HEADS * DK, HEADS * DV, 512
WINDOWS = (2, 4, 8, 16)
GC = POOL_W // len(WINDOWS)
CHUNK = 64
BLK = 4 * CHUNK
HALO = 16
ROPE_BASE = 10000.0
LR, B1, B2, ADAM_EPS, WD, STEP = 0.001, 0.9, 0.999, 1e-08, 0.01, 10
LANE = 128
TM = 512
FFN_TM = 1024
FFN_FWD_TF = 512
WGRAD_TT = 4096
VMEM_LIMIT = 56 * 1024 * 1024


def _cparams(n_axes):
    return pltpu.CompilerParams(dimension_semantics=("arbitrary",) * n_axes, vmem_limit_bytes=VMEM_LIMIT)


class _Exchange:
    def __init__(self, inputs, out_shape, scratch, start, finish, mid=None):
        self.inputs, self.out_shape, self.scratch = list(inputs), list(out_shape), list(scratch)
        self.start, self.finish, self.mid = start, finish, mid


def _pallas(body, name, grid, in_specs, out_specs, out_shape, scratch_shapes, args, ride=None):
    n_axes = len(grid)
    if ride is None:
        return pl.pallas_call(body, name=name, grid=grid, in_specs=in_specs, out_specs=out_specs, out_shape=out_shape,
                              scratch_shapes=scratch_shapes, compiler_params=_cparams(n_axes))(*args)
    n_in, n_out, n_scr = len(in_specs), len(out_specs), len(scratch_shapes)
    r_in, r_out = len(ride.inputs), len(ride.out_shape)

    def hosted(*refs):
        ins, refs = refs[:n_in], refs[n_in:]
        r_ins, refs = refs[:r_in], refs[r_in:]
        outs, refs = refs[:n_out], refs[n_out:]
        r_outs, refs = refs[:r_out], refs[r_out:]
        scr, sems = refs[:n_scr], refs[n_scr:]
        ids = [pl.program_id(a) for a in range(n_axes)]
        first, last, inner0 = ids[0] == 0, ids[0] == grid[0] - 1, None
        for a in range(1, n_axes):
            first = first & (ids[a] == 0)
            last = last & (ids[a] == grid[a] - 1)
            inner0 = (ids[a] == 0) if inner0 is None else inner0 & (ids[a] == 0)

        @pl.when(first)
        def _():
            ride.start(r_ins, r_outs, sems)

        if ride.mid is not None:
            at_mid = ids[0] == grid[0] - 1
            if inner0 is not None:
                at_mid = at_mid & inner0

            @pl.when(at_mid)
            def _():
                ride.mid(r_ins, r_outs, sems)

        body(*ins, *outs, *scr)

        @pl.when(last)
        def _():
            ride.finish(r_ins, r_outs, sems)

    res = pl.pallas_call(
        hosted, name=name, grid=grid, in_specs=list(in_specs) + [ANY] * r_in, out_specs=list(out_specs) + [ANY] * r_out,
        out_shape=list(out_shape) + ride.out_shape, scratch_shapes=list(scratch_shapes) + ride.scratch,
        compiler_params=_cparams(n_axes))(*args, *ride.inputs)
    return res[:n_out], res[n_out:]


def _dot(a, b):
    return jnp.dot(a, b, preferred_element_type=F32)


def _dot_nt(a, b):
    return lax.dot_general(a, b, (((1,), (1,)), ((), ())), preferred_element_type=F32)


def _dot_tn(a, b):
    return lax.dot_general(a, b, (((0,), (0,)), ((), ())), preferred_element_type=F32)


def _sigmoid(x):
    return 0.5 * jnp.tanh(0.5 * x) + 0.5


def _transpose(w, name):
    r, c = w.shape
    tb = 512

    def body(x_ref, o_ref):
        o_ref[...] = x_ref[...].T

    return pl.pallas_call(
        body, name=name, grid=(r // tb, c // tb),
        in_specs=[pl.BlockSpec((tb, tb), lambda i, j: (i, j))],
        out_specs=pl.BlockSpec((tb, tb), lambda i, j: (j, i)),
        out_shape=jax.ShapeDtypeStruct((c, r), w.dtype),
        compiler_params=_cparams(2),
    )(w)


def _pad_to(n, m):
    return (n + m - 1) // m * m


def _retention_constants():
    gamma = (1.0 - 2.0 ** (-5.0 - np.arange(HEADS, dtype=np.float32))).astype(np.float32)
    log_g = np.log(gamma).astype(np.float32)
    i = np.arange(BLK)
    diff = (i[:, None] - i[None, :]).astype(np.float32)
    same = (i[:, None] // CHUNK) == (i[None, :] // CHUNK)
    earlier = (i[None, :] // CHUNK) < (i[:, None] // CHUNK)
    mask = np.zeros((HEADS, BLK, BLK), np.float32)
    for h in range(HEADS):
        dec_abs = np.exp(log_g[h] * np.abs(diff)).astype(np.float32)
        dec = np.exp(log_g[h] * diff * earlier).astype(np.float32)
        mask[h] = np.where(same, dec_abs, np.where(earlier, dec, 0.0))
    dq = np.zeros((BLK, V_W), np.float32)
    dk = np.zeros((BLK, QK_W), np.float32)
    gbd = np.zeros((QK_W, V_W), np.float32)
    for h in range(HEADS):
        dq[:, h * DV:(h + 1) * DV] = np.exp(log_g[h] * (i + 1.0)).astype(np.float32)[:, None]
        dk[:, h * DK:(h + 1) * DK] = np.exp(log_g[h] * (BLK - 1.0 - i)).astype(np.float32)[:, None]
        gbd[h * DK:(h + 1) * DK, h * DV:(h + 1) * DV] = np.exp(log_g[h] * np.float32(BLK))
    bd = (gbd > 0).astype(np.float32)
    return jnp.asarray(mask), jnp.asarray(dq), jnp.asarray(dk), jnp.asarray(gbd), jnp.asarray(bd)


def _rotary_tables(seq):
    half = DK // 2
    freqs = ROPE_BASE ** (-jnp.arange(half, dtype=F32) * 2.0 / DK)
    ang = jnp.arange(seq, dtype=F32)[:, None] * freqs[None, :]
    cos, sin = jnp.cos(ang), jnp.sin(ang)
    cos_t = jnp.tile(jnp.concatenate([cos, cos], axis=1), (1, HEADS))
    sin_t = jnp.tile(jnp.concatenate([-sin, sin], axis=1), (1, HEADS))
    return cos_t, sin_t


def _swap_halves(x):
    lane = lax.broadcasted_iota(jnp.int32, (1, QK_W), 1)
    first = (lane & (DK - 1)) < DK // 2
    return jnp.where(first, pltpu.roll(x, QK_W - DK // 2, 1), pltpu.roll(x, DK // 2, 1))


def _head_mask(h):
    lane = lax.broadcasted_iota(jnp.int32, (1, QK_W), 1)
    return (lane >= h * DK) & (lane < (h + 1) * DK)


def _ffn_fwd(x, n, gate, up, wd, name, ride=None):
    t, d = x.shape
    (wg, gq), (wu, uq) = gate, up
    fp = wg.shape[1]
    tm = min(t, FFN_TM)
    tf = FFN_FWD_TF
    nj = fp // tf

    def body(x_ref, n_ref, wg_ref, wu_ref, wd_ref, xo_ref, h_ref, b_ref, sil_ref, dsil_ref, acc_ref):
        j = pl.program_id(1)

        @pl.when(j == 0)
        def _():
            xv = x_ref[...]
            r = lax.rsqrt(jnp.mean(xv * xv, axis=-1, keepdims=True) + RMS_EPS)
            h_ref[...] = (xv * r * n_ref[...]).astype(BF16)
            acc_ref[...] = jnp.zeros_like(acc_ref)

        h = h_ref[...]
        a = _dot(h, wg_ref[...])
        b = _dot(h, wu_ref[...])
        sg = _sigmoid(a)
        sil = a * sg
        b_ref[...] = b.astype(BF16)
        sil_ref[...] = sil.astype(BF16)
        dsil_ref[...] = (sg + sil * (1.0 - sg)).astype(BF16)
        acc_ref[...] += _dot((sil * b).astype(BF16), wd_ref[...])

        @pl.when(j == nj - 1)
        def _():
            xo_ref[...] = x_ref[...] + 0.5 * acc_ref[...]

    act = pl.BlockSpec((tm, tf), lambda i, j: (i, j))
    return _pallas(
        body, name, (t // tm, nj),
        [pl.BlockSpec((tm, d), lambda i, j: (i, 0)), pl.BlockSpec((1, d), lambda i, j: (0, 0)),
         pl.BlockSpec((d, tf), lambda i, j: (gq, j)), pl.BlockSpec((d, tf), lambda i, j: (uq, j)),
         pl.BlockSpec((tf, d), lambda i, j: (j, 0))],
        [pl.BlockSpec((tm, d), lambda i, j: (i, 0)), pl.BlockSpec((tm, d), lambda i, j: (i, 0)), act, act, act],
        [jax.ShapeDtypeStruct((t, d), F32), jax.ShapeDtypeStruct((t, d), BF16)] + [jax.ShapeDtypeStruct((t, fp), BF16)] * 3,
        [pltpu.VMEM((tm, d), F32)], (x, n, wg, wu, wd), ride)


def _ffn_act(x, n, gate, up, name, ride=None):
    t, d = x.shape
    (wg, gq), (wu, uq) = gate, up
    fp = wg.shape[1]
    tm = min(t, FFN_TM)
    tf = 2 * fp // N_DEV
    nj = fp // tf

    def body(x_ref, n_ref, wg_ref, wu_ref, h_ref, b_ref, sil_ref, dsil_ref, s_ref):
        @pl.when(pl.program_id(1) == 0)
        def _():
            xv = x_ref[...]
            r = lax.rsqrt(jnp.mean(xv * xv, axis=-1, keepdims=True) + RMS_EPS)
            h_ref[...] = (xv * r * n_ref[...]).astype(BF16)

        h = h_ref[...]
        a = _dot(h, wg_ref[...])
        b = _dot(h, wu_ref[...])
        sg = _sigmoid(a)
        sil = a * sg
        b_ref[...] = b.astype(BF16)
        sil_ref[...] = sil.astype(BF16)
        dsil_ref[...] = (sg + sil * (1.0 - sg)).astype(BF16)
        s_ref[...] = (sil * b).astype(BF16)

    act = pl.BlockSpec((tm, tf), lambda i, j: (i, j))
    return _pallas(
        body, name, (t // tm, nj),
        [pl.BlockSpec((tm, d), lambda i, j: (i, 0)), pl.BlockSpec((1, d), lambda i, j: (0, 0)),
         pl.BlockSpec((d, tf), lambda i, j: (gq, j)), pl.BlockSpec((d, tf), lambda i, j: (uq, j))],
        [pl.BlockSpec((tm, d), lambda i, j: (i, 0)), act, act, act, act],
        [jax.ShapeDtypeStruct((t, d), BF16)] + [jax.ShapeDtypeStruct((t, fp), BF16)] * 4,
        [], (x, n, wg, wu), ride)


def _ffn_down(s, x, wd, name, ride=None):
    t, d = x.shape
    fp = wd.shape[0]
    tm = min(t, FFN_TM)

    def body(s_ref, x_ref, wd_ref, xo_ref):
        xo_ref[...] = x_ref[...] + 0.5 * _dot(s_ref[...], wd_ref[...])

    row = pl.BlockSpec((tm, d), lambda i: (i, 0))
    res = _pallas(body, name, (t // tm,), [pl.BlockSpec((tm, fp), lambda i: (i, 0)), row, pl.BlockSpec((fp, d), lambda i: (0, 0))],
                  [row], [jax.ShapeDtypeStruct((t, d), F32)], [], (s, x, wd), ride)
    return res[0] if ride is None else (res[0][0], res[1])


def _ffn_bwd_act(dxob, b, sil, dsil, wd_t, name, ride=None):
    t, d = dxob.shape
    fp = wd_t.shape[1]
    tm = min(t, FFN_TM)
    tf = 2 * fp // N_DEV
    ni = t // tm

    def body(dx_ref, b_ref, sil_ref, dsil_ref, wd_ref, da_ref, db_ref, gd_ref, acc_ref):
        i = pl.program_id(1)

        @pl.when(i == 0)
        def _():
            acc_ref[...] = jnp.zeros_like(acc_ref)

        dxv = dx_ref[...]
        bv, sv = b_ref[...].astype(F32), sil_ref[...].astype(F32)
        ds = _dot(dxv, wd_ref[...])
        da_ref[...] = (ds * bv * dsil_ref[...].astype(F32)).astype(BF16)
        db_ref[...] = (ds * sv).astype(BF16)
        acc_ref[...] += _dot_tn((sv * bv).astype(BF16), dxv)

        @pl.when(i == ni - 1)
        def _():
            gd_ref[...] = acc_ref[...].astype(BF16)

    act = pl.BlockSpec((tm, tf), lambda c, i: (i, c))
    return _pallas(
        body, name, (fp // tf, ni),
        [pl.BlockSpec((tm, d), lambda c, i: (i, 0)), act, act, act, pl.BlockSpec((d, tf), lambda c, i: (0, c))],
        [act, act, pl.BlockSpec((tf, d), lambda c, i: (c, 0))],
        [jax.ShapeDtypeStruct((t, fp), BF16), jax.ShapeDtypeStruct((t, fp), BF16), jax.ShapeDtypeStruct((fp, d), BF16)],
        [pltpu.VMEM((tf, d), F32)], (dxob, b, sil, dsil, wd_t), ride)


def _ffn_bwd_in(da, db, dxo, x, n, cols_t, gq, name, ride=None):
    t, d = x.shape
    fp = cols_t.shape[0]
    tm = min(t, FFN_TM)
    tf = 2 * fp // N_DEV
    nj = fp // tf

    def body(da_ref, db_ref, dxo_ref, x_ref, n_ref, wg_ref, wu_ref, dx_ref, dn_ref, acc_ref):
        i, j = pl.program_id(0), pl.program_id(1)

        @pl.when((i == 0) & (j == 0))
        def _():
            dn_ref[...] = jnp.zeros_like(dn_ref)

        @pl.when(j == 0)
        def _():
            acc_ref[...] = jnp.zeros_like(acc_ref)

        acc_ref[...] += _dot(da_ref[...], wg_ref[...]) + _dot(db_ref[...], wu_ref[...])

        @pl.when(j == nj - 1)
        def _():
            xv = x_ref[...]
            r = lax.rsqrt(jnp.mean(xv * xv, axis=-1, keepdims=True) + RMS_EPS)
            xh = xv * r
            dh = acc_ref[...]
            dn_ref[...] += jnp.sum(dh * xh, axis=0, keepdims=True)
            dhn = dh * n_ref[...]
            dx_ref[...] = dxo_ref[...] + r * (dhn - xh * jnp.mean(dhn * xh, axis=-1, keepdims=True))

    act = pl.BlockSpec((tm, tf), lambda i, j: (i, j))
    row = pl.BlockSpec((tm, d), lambda i, j: (i, 0))
    return _pallas(
        body, name, (t // tm, nj),
        [act, act, row, row, pl.BlockSpec((1, d), lambda i, j: (0, 0)),
         pl.BlockSpec((tf, d), lambda i, j: (j, gq)), pl.BlockSpec((tf, d), lambda i, j: (j, gq + 1))],
        [row, pl.BlockSpec((1, d), lambda i, j: (0, 0))],
        [jax.ShapeDtypeStruct((t, d), F32), jax.ShapeDtypeStruct((1, d), F32)],
        [pltpu.VMEM((tm, d), F32)], (da, db, dxo, x, n, cols_t, cols_t), ride)


def _wgrad(a, b, scale, tk, tn, name, ride=None, b_cols=None):
    t, k = a.shape
    q0, nq = (0, b.shape[1] // tn) if b_cols is None else b_cols
    n = nq * tn
    tt = min(t, WGRAD_TT)
    nt = t // tt

    def body(a_ref, b_ref, o_ref, acc_ref):
        s = pl.program_id(2)

        @pl.when(s == 0)
        def _():
            acc_ref[...] = jnp.zeros_like(acc_ref)

        acc_ref[...] += _dot_tn(a_ref[...], b_ref[...])

        @pl.when(s == nt - 1)
        def _():
            o_ref[...] = (scale * acc_ref[...]).astype(BF16)

    res = _pallas(
        body, name, (k // tk, n // tn, nt),
        [pl.BlockSpec((tt, tk), lambda p, q, s: (s, p)), pl.BlockSpec((tt, tn), lambda p, q, s: (s, q + q0))],
        [pl.BlockSpec((tk, tn), lambda p, q, s: (p, q))], [jax.ShapeDtypeStruct((k, n), BF16)],
        [pltpu.VMEM((tk, tn), F32)], (a, b), ride)
    return res[0] if ride is None else (res[0][0], res[1])


def _mix_in(x, n, w_in, cos_t, sin_t, seq, name):
    t, d = x.shape
    per_seq = seq // TM

    def body(x_ref, n_ref, w_ref, c_ref, s_ref, h_ref, q_ref, k_ref, v_ref, g_ref, u_ref):
        xv = x_ref[...]
        r = lax.rsqrt(jnp.mean(xv * xv, axis=-1, keepdims=True) + RMS_EPS)
        h = (xv * r * n_ref[...]).astype(BF16)
        h_ref[...] = h
        p = _dot(h, w_ref[...])
        c, s = c_ref[...], s_ref[...]
        q = p[:, :QK_W]
        k = p[:, QK_W:2 * QK_W]
        q_ref[...] = ((q * c + _swap_halves(q) * s) * (DK ** -0.5)).astype(BF16)
        k_ref[...] = (k * c + _swap_halves(k) * s).astype(BF16)
        v_ref[...] = p[:, 2 * QK_W:2 * QK_W + V_W].astype(BF16)
        g_ref[...] = p[:, 2 * QK_W + V_W:2 * QK_W + 2 * V_W]
        u_ref[...] = p[:, 2 * QK_W + 2 * V_W:]

    tile = lambda w: pl.BlockSpec((TM, w), lambda i: (i, 0))
    return pl.pallas_call(
        body, name=name, grid=(t // TM,),
        in_specs=[tile(d), pl.BlockSpec((1, d), lambda i: (0, 0)), pl.BlockSpec(w_in.shape, lambda i: (0, 0)),
                  pl.BlockSpec((TM, QK_W), lambda i: (i % per_seq, 0)), pl.BlockSpec((TM, QK_W), lambda i: (i % per_seq, 0))],
        out_specs=[tile(d), tile(QK_W), tile(QK_W), tile(V_W), tile(V_W), tile(POOL_W)],
        out_shape=[jax.ShapeDtypeStruct((t, d), BF16), jax.ShapeDtypeStruct((t, QK_W), BF16),
                   jax.ShapeDtypeStruct((t, QK_W), BF16), jax.ShapeDtypeStruct((t, V_W), BF16),
                   jax.ShapeDtypeStruct((t, V_W), F32), jax.ShapeDtypeStruct((t, POOL_W), F32)],
        compiler_params=_cparams(1),
    )(x, n, w_in, cos_t, sin_t)


def _mix_in_bwd(dp, dx2, x1, n, w_in_t, name, ride=None):
    t, d = x1.shape

    def body(dp_ref, dx2_ref, x_ref, n_ref, w_ref, dx_ref, dn_ref, dxb_ref):
        @pl.when(pl.program_id(0) == 0)
        def _():
            dn_ref[...] = jnp.zeros_like(dn_ref)

        dh = _dot(dp_ref[...], w_ref[...])
        xv = x_ref[...]
        r = lax.rsqrt(jnp.mean(xv * xv, axis=-1, keepdims=True) + RMS_EPS)
        xh = xv * r
        dn_ref[...] += jnp.sum(dh * xh, axis=0, keepdims=True)
        dhn = dh * n_ref[...]
        dx = dx2_ref[...] + r * (dhn - xh * jnp.mean(dhn * xh, axis=-1, keepdims=True))
        dx_ref[...] = dx
        dxb_ref[...] = (0.5 * dx).astype(BF16)

    tile = lambda w: pl.BlockSpec((TM, w), lambda i: (i, 0))
    return _pallas(
        body, name, (t // TM,),
        [tile(dp.shape[1]), tile(d), tile(d), pl.BlockSpec((1, d), lambda i: (0, 0)),
         pl.BlockSpec(w_in_t.shape, lambda i: (0, 0))],
        [tile(d), pl.BlockSpec((1, d), lambda i: (0, 0)), tile(d)],
        [jax.ShapeDtypeStruct((t, d), F32), jax.ShapeDtypeStruct((1, d), F32), jax.ShapeDtypeStruct((t, d), BF16)],
        [], (dp, dx2, x1, n, w_in_t), ride)


def _group_norm(o):
    parts, rstds = [], []
    for h in range(HEADS):
        oh = o[:, h * DV:(h + 1) * DV]
        dlt = oh - jnp.mean(oh, axis=-1, keepdims=True)
        rstd = lax.rsqrt(jnp.mean(dlt * dlt, axis=-1, keepdims=True) + GN_EPS)
        parts.append(dlt * rstd)
        rstds.append(rstd)
    return jnp.concatenate(parts, axis=1), rstds


def _mix_core_fwd(qs, k, v, g, u, x1, consts, gain, wp, scale, w_out, nseq, seq, name, ride=None):
    t, d = x1.shape
    nblk = seq // BLK
    mask, dq, dk, gbd, bd = consts

    def body(q_ref, k_ref, v_ref, g_ref, u_ref, x1_ref, m_ref, dq_ref, dk_ref, gbd_ref, bd_ref, gain_ref, wp_ref,
             sc_ref, wo_ref, x2_ref, mix_ref, o_ref, pooled_ref, st_ref, state, halo):
        j = pl.program_id(1)

        @pl.when(j == 0)
        def _():
            state[...] = jnp.zeros_like(state)
            halo[...] = jnp.zeros_like(halo)

        qv, kv, vv = q_ref[...], k_ref[...], v_ref[...]
        st = state[...]
        st_ref[0] = st
        cross = _dot(qv, st.astype(BF16)) * dq_ref[...]
        outs = []
        for h in range(HEADS):
            qh = jnp.where(_head_mask(h), qv, jnp.zeros_like(qv))
            am = (_dot_nt(qh, kv) * m_ref[h]).astype(BF16)
            outs.append(_dot(am, vv[:, h * DV:(h + 1) * DV]))
        o = jnp.concatenate(outs, axis=1) + cross
        o_ref[...] = o
        kd = (kv.astype(F32) * dk_ref[...]).astype(BF16)
        state[...] = gbd_ref[...] * st + _dot_tn(kd, vv) * bd_ref[...]

        gv = g_ref[...]
        nrm, _ = _group_norm(o)
        ret = (gv * _sigmoid(gv)) * (nrm * gain_ref[...])

        uv = u_ref[...]
        c = jnp.concatenate([halo[...], uv], axis=0)
        halo[...] = uv[BLK - HALO:, :]
        pos = j * BLK + lax.broadcasted_iota(jnp.int32, (BLK, 1), 0)
        parts = []
        for gi, w in enumerate(WINDOWS):
            c = c + pltpu.roll(c, w // 2, 0)
            cnt = jnp.minimum(pos + 1, w).astype(F32)
            parts.append(c[HALO:, :GC] / cnt)
            if gi + 1 < len(WINDOWS):
                c = c[:, GC:]
        pooled = (jnp.concatenate(parts, axis=1) - uv).astype(BF16)
        pooled_ref[...] = pooled
        z = jnp.concatenate([_dot(pooled[:, gi * GC:(gi + 1) * GC], wp_ref[gi]) for gi in range(len(WINDOWS))], axis=1)
        mix = jnp.concatenate([ret, z * sc_ref[...]], axis=1).astype(BF16)
        mix_ref[...] = mix
        x2_ref[...] = x1_ref[...] + _dot(mix, wo_ref[...])

    blk = lambda w: pl.BlockSpec((BLK, w), lambda i, j: (i * nblk + j, 0))
    full = lambda a: pl.BlockSpec(a.shape, lambda i, j: (0,) * a.ndim)
    return _pallas(
        body, name, (nseq, nblk),
        [blk(QK_W), blk(QK_W), blk(V_W), blk(V_W), blk(POOL_W), blk(d),
         full(mask), full(dq), full(dk), full(gbd), full(bd), full(gain), full(wp), full(scale), full(w_out)],
        [blk(d), blk(d), blk(V_W), blk(POOL_W), pl.BlockSpec((1, QK_W, V_W), lambda i, j: (i * nblk + j, 0, 0))],
        [jax.ShapeDtypeStruct((t, d), F32), jax.ShapeDtypeStruct((t, d), BF16),
         jax.ShapeDtypeStruct((t, V_W), F32), jax.ShapeDtypeStruct((t, POOL_W), BF16),
         jax.ShapeDtypeStruct((nseq * nblk, QK_W, V_W), F32)],
        [pltpu.VMEM((QK_W, V_W), F32), pltpu.VMEM((HALO, POOL_W), F32)],
        (qs, k, v, g, u, x1, mask, dq, dk, gbd, bd, gain, wp, scale, w_out), ride)


def _mix_core_bwd(dx2, qs, k, v, g, o, pooled, st, consts, gain, wp, scale, w_out, cos_t, sin_t, nseq, seq, name,
                  ride=None):
    t, d = dx2.shape
    nblk = seq // BLK
    mask, dq, dk, gbd, bd = consts
    n_win = len(WINDOWS)

    def body(dx2_ref, q_ref, k_ref, v_ref, g_ref, o_ref, pooled_ref, st_ref, m_ref, dq_ref, dk_ref, gbd_ref, bd_ref,
             gain_ref, wp_ref, sc_ref, wo_ref, c_ref, s_ref,
             dp_ref, dx2b_ref, dgain_ref, dscale_ref, dwp_ref, rstate, carry):
        i, j = pl.program_id(0), pl.program_id(1)

        @pl.when((i == 0) & (j == 0))
        def _():
            dgain_ref[...] = jnp.zeros_like(dgain_ref)
            dscale_ref[...] = jnp.zeros_like(dscale_ref)
            dwp_ref[...] = jnp.zeros_like(dwp_ref)

        @pl.when(j == 0)
        def _():
            rstate[...] = jnp.zeros_like(rstate)
            carry[...] = jnp.zeros_like(carry)

        dx2b = dx2_ref[...].astype(BF16)
        dx2b_ref[...] = dx2b
        dmix = _dot(dx2b, wo_ref[...])
        dret, dpool = dmix[:, :V_W], dmix[:, V_W:]

        gv, ov, gain_v = g_ref[...], o_ref[...], gain_ref[...]
        sg = _sigmoid(gv)
        sil = gv * sg
        nrm, rstds = _group_norm(ov)
        dg = dret * (nrm * gain_v) * (sg * (1.0 + gv * (1.0 - sg)))
        dgn = dret * sil
        dgain_ref[...] += jnp.sum(dgn * nrm, axis=0, keepdims=True)
        dnrm = dgn * gain_v
        do_parts = []
        for h in range(HEADS):
            dn_h = dnrm[:, h * DV:(h + 1) * DV]
            n_h = nrm[:, h * DV:(h + 1) * DV]
            do_parts.append(rstds[h] * (dn_h - jnp.mean(dn_h, axis=-1, keepdims=True)
                                        - n_h * jnp.mean(dn_h * n_h, axis=-1, keepdims=True)))
        do = jnp.concatenate(do_parts, axis=1)
        dob = do.astype(BF16)

        qv, kv, vv = q_ref[...], k_ref[...], v_ref[...]
        stb = st_ref[0].astype(BF16)
        rs = rstate[...]
        rsb = rs.astype(BF16)
        dod = (do * dq_ref[...]).astype(BF16)
        dqs = _dot_nt(dod, stb)
        dst = _dot_tn(qv, dod) * bd_ref[...]
        dkf = dk_ref[...]
        kd = (kv.astype(F32) * dkf).astype(BF16)
        dks = _dot_nt(vv, rsb) * dkf
        dvs = _dot(kd, rsb)
        dv_parts = []
        for h in range(HEADS):
            hm = _head_mask(h)
            qh = jnp.where(hm, qv, jnp.zeros_like(qv))
            mh = m_ref[h]
            am = (_dot_nt(qh, kv) * mh).astype(BF16)
            dpm = (_dot_nt(dob[:, h * DV:(h + 1) * DV], vv[:, h * DV:(h + 1) * DV]) * mh).astype(BF16)
            dqs = dqs + jnp.where(hm, _dot(dpm, kv), 0.0)
            dks = dks + jnp.where(hm, _dot_tn(dpm, qv), 0.0)
            dv_parts.append(_dot_tn(am, dob[:, h * DV:(h + 1) * DV]))
        dvs = dvs + jnp.concatenate(dv_parts, axis=1)
        rstate[...] = dst + gbd_ref[...] * rs

        cv, sv = c_ref[...], s_ref[...]
        dqr = dqs * (DK ** -0.5)
        dq_pre = dqr * cv + _swap_halves(dqr * sv)
        dk_pre = dks * cv + _swap_halves(dks * sv)

        pv = pooled_ref[...]
        sc = sc_ref[...]
        dzb = (dpool * sc).astype(BF16)
        z_parts, dpo_parts = [], []
        for gi in range(n_win):
            p_g = pv[:, gi * GC:(gi + 1) * GC]
            dz_g = dzb[:, gi * GC:(gi + 1) * GC]
            z_parts.append(_dot(p_g, wp_ref[gi]))
            dwp_ref[gi] += _dot_tn(p_g, dz_g)
            dpo_parts.append(_dot_nt(dz_g, wp_ref[gi]))
        dscale_ref[...] += jnp.sum(dpool * jnp.concatenate(z_parts, axis=1), axis=0, keepdims=True)
        dpo = jnp.concatenate(dpo_parts, axis=1)
        pos = (nblk - 1 - j) * BLK + lax.broadcasted_iota(jnp.int32, (BLK, 1), 0)
        e = jnp.concatenate(
            [dpo[:, gi * GC:(gi + 1) * GC] / jnp.minimum(pos + 1, w).astype(F32) for gi, w in enumerate(WINDOWS)], axis=1)
        c = jnp.concatenate([e, carry[...]], axis=0)
        carry[...] = e[:HALO, :]
        rows = BLK + HALO
        lead = []
        for gi, w in enumerate(WINDOWS):
            c = c + pltpu.roll(c, rows - w // 2, 0)
            lead.append(c[:BLK, :GC])
            if gi + 1 < n_win:
                c = c[:, GC:]
        du = jnp.concatenate(lead, axis=1) - dpo

        dp_ref[:, 0:QK_W] = dq_pre.astype(BF16)
        dp_ref[:, QK_W:2 * QK_W] = dk_pre.astype(BF16)
        dp_ref[:, 2 * QK_W:2 * QK_W + V_W] = dvs.astype(BF16)
        dp_ref[:, 2 * QK_W + V_W:2 * QK_W + 2 * V_W] = dg.astype(BF16)
        dp_ref[:, 2 * QK_W + 2 * V_W:] = du.astype(BF16)

    rev = lambda i, j: i * nblk + (nblk - 1 - j)
    blk = lambda w: pl.BlockSpec((BLK, w), lambda i, j: (rev(i, j), 0))
    full = lambda a: pl.BlockSpec(a.shape, lambda i, j: (0,) * a.ndim)
    in_w = 2 * QK_W + 2 * V_W + POOL_W
    return _pallas(
        body, name, (nseq, nblk),
        [blk(d), blk(QK_W), blk(QK_W), blk(V_W), blk(V_W), blk(V_W), blk(POOL_W),
         pl.BlockSpec((1, QK_W, V_W), lambda i, j: (rev(i, j), 0, 0)),
         full(mask), full(dq), full(dk), full(gbd), full(bd), full(gain), full(wp), full(scale), full(w_out),
         pl.BlockSpec((BLK, QK_W), lambda i, j: (nblk - 1 - j, 0)),
         pl.BlockSpec((BLK, QK_W), lambda i, j: (nblk - 1 - j, 0))],
        [blk(in_w), blk(d), pl.BlockSpec((1, V_W), lambda i, j: (0, 0)),
         pl.BlockSpec((1, POOL_W), lambda i, j: (0, 0)), pl.BlockSpec((n_win, GC, GC), lambda i, j: (0, 0, 0))],
        [jax.ShapeDtypeStruct((t, in_w), BF16), jax.ShapeDtypeStruct((t, d), BF16),
         jax.ShapeDtypeStruct((1, V_W), F32), jax.ShapeDtypeStruct((1, POOL_W), F32),
         jax.ShapeDtypeStruct((n_win, GC, GC), F32)],
        [pltpu.VMEM((QK_W, V_W), F32), pltpu.VMEM((HALO, POOL_W), F32)],
        (dx2, qs, k, v, g, o, pooled, st, mask, dq, dk, gbd, bd, gain, wp, scale, w_out, cos_t, sin_t), ride)


def _loss_head(x3, nf, tgt, name):
    t, d = x3.shape

    def body(x_ref, n_ref, t_ref, dx_ref, dn_ref, loss_ref, dxb_ref):
        @pl.when(pl.program_id(0) == 0)
        def _():
            dn_ref[...] = jnp.zeros_like(dn_ref)
            loss_ref[...] = jnp.zeros_like(loss_ref)

        xv = x_ref[...]
        nv = n_ref[...]
        r = lax.rsqrt(jnp.mean(xv * xv, axis=-1, keepdims=True) + RMS_EPS)
        xh = xv * r
        err = xh * nv - t_ref[...]
        row = jnp.mean(err * err, axis=-1, keepdims=True)
        loss_ref[...] += 0.5 * jnp.sum(row, axis=0, keepdims=True)
        dy = err * (1.0 / d)
        dn_ref[...] += jnp.sum(dy * xh, axis=0, keepdims=True)
        dxh = dy * nv
        dx = r * (dxh - xh * jnp.mean(dxh * xh, axis=-1, keepdims=True))
        dx_ref[...] = dx
        dxb_ref[...] = (0.5 * dx).astype(BF16)

    tile = pl.BlockSpec((TM, d), lambda i: (i, 0))
    return pl.pallas_call(
        body, name=name, grid=(t // TM,),
        in_specs=[tile, pl.BlockSpec((1, d), lambda i: (0, 0)), tile],
        out_specs=[tile, pl.BlockSpec((1, d), lambda i: (0, 0)), pl.BlockSpec((1, 1), lambda i: (0, 0)), tile],
        out_shape=[jax.ShapeDtypeStruct((t, d), F32), jax.ShapeDtypeStruct((1, d), F32), jax.ShapeDtypeStruct((1, 1), F32),
                   jax.ShapeDtypeStruct((t, d), BF16)],
        compiler_params=_cparams(1),
    )(x3, nf, tgt)


def _coords():
    return lax.axis_index("x"), lax.axis_index("y"), lax.axis_index("c")


def _window(ref, kind, idx, size):
    if kind == "col":
        return ref.at[:, pl.ds(pl.multiple_of(idx * size, LANE), size)]
    return ref.at[pl.ds(pl.multiple_of(idx * size, 8), size), :]


def _run_exchange(ex, name):
    n_in = len(ex.inputs)

    def body(*refs):
        ins, outs, sems = refs[:n_in], refs[n_in:n_in + len(ex.out_shape)], refs[n_in + len(ex.out_shape):]
        ex.start(ins, outs, sems)
        if ex.mid is not None:
            ex.mid(ins, outs, sems)
        ex.finish(ins, outs, sems)

    return pl.pallas_call(body, name=name, in_specs=[ANY] * n_in, out_specs=[ANY] * len(ex.out_shape),
                          out_shape=ex.out_shape, scratch_shapes=ex.scratch)(*ex.inputs)


def _join(exchanges):
    bounds = []
    i0 = o0 = s0 = 0
    for ex in exchanges:
        bounds.append((i0, o0, s0))
        i0, o0, s0 = i0 + len(ex.inputs), o0 + len(ex.out_shape), s0 + len(ex.scratch)

    def phase(which):
        def run(ins, outs, sems):
            for ex, (i, o, s) in zip(exchanges, bounds):
                fn = getattr(ex, which)
                if fn is not None:
                    fn(ins[i:i + len(ex.inputs)], outs[o:o + len(ex.out_shape)], sems[s:s + len(ex.scratch)])
        return run

    return _Exchange(sum((ex.inputs for ex in exchanges), []), sum((ex.out_shape for ex in exchanges), []),
                     sum((ex.scratch for ex in exchanges), []), phase("start"), phase("finish"),
                     phase("mid") if any(ex.mid is not None for ex in exchanges) else None)


def _gather_exchange(parts):
    n = len(parts)
    kinds = [kd for _, kd in parts]
    sizes = [a.shape[1] if kd == "col" else a.shape[0] for a, kd in parts]

    def plan(ins, outs, sems):
        send_sems, recv_sems, local_sems = sems
        x, y, c = _coords()
        me, sibling = (x, y, c), (x, y, 1 - c)
        chips = [(1 - x, y), (x, 1 - y), (1 - x, 1 - y)]

        def win(p, dev):
            return _window(outs[p], kinds[p], 4 * dev[0] + 2 * dev[1] + dev[2], sizes[p])

        def copy(p, k, block, to, src=None):
            return pltpu.make_async_remote_copy(
                src_ref=win(p, block) if src is None else src, dst_ref=win(p, block),
                send_sem=send_sems.at[p * 7 + k], recv_sem=recv_sems.at[p * 7 + k], device_id=to, device_id_type=MESH_ID)

        mine = [pltpu.make_async_copy(ins[p], win(p, me), local_sems.at[p]) for p in range(n)]
        first, arrived, passed, rest = [], [], [], []
        for p in range(n):
            first.append(copy(p, 0, me, sibling, src=ins[p]))
            first += [copy(p, 1 + q, me, (*chip, c), src=ins[p]) for q, chip in enumerate(chips)]
            rest.append(copy(p, 0, sibling, me))
            rest += [copy(p, 4 + q, (*chip, 1 - c), me) for q, chip in enumerate(chips)]
        for q, chip in enumerate(chips):
            for p in range(n):
                arrived.append(copy(p, 1 + q, (*chip, c), me))
                passed.append(copy(p, 4 + q, (*chip, c), sibling))
        return mine, first, arrived, passed, rest

    def start(ins, outs, sems):
        mine, first, _, _, _ = plan(ins, outs, sems)
        for cp in mine + first:
            cp.start()

    def mid(ins, outs, sems):
        _, _, arrived, passed, _ = plan(ins, outs, sems)
        for got, fwd in zip(arrived, passed):
            got.wait_recv()
            fwd.start()

    def finish(ins, outs, sems):
        mine, first, _, passed, rest = plan(ins, outs, sems)
        for cp in rest:
            cp.wait_recv()
        for cp in first + passed:
            cp.wait_send()
        for cp in mine:
            cp.wait()

    out_shape = [jax.ShapeDtypeStruct((a.shape[0], N_DEV * a.shape[1]) if kd == "col" else (N_DEV * a.shape[0], a.shape[1]),
                                      a.dtype) for a, kd in parts]
    scratch = [pltpu.SemaphoreType.DMA((7 * n,)), pltpu.SemaphoreType.DMA((7 * n,)), pltpu.SemaphoreType.DMA((n,))]
    return _Exchange([a for a, _ in parts], out_shape, scratch, start, finish, mid)


def _all_gather(parts, name):
    return _run_exchange(_gather_exchange(parts), name)


def _shard_shape(a, kd):
    return (a.shape[0], a.shape[1] // N_DEV) if kd == "col" else (a.shape[0] // N_DEV, a.shape[1])


def _symmetric_exchange(inputs, out_shape, n_copies, plan):
    def start(ins, outs, sems):
        for cp in plan(ins, outs, sems):
            cp.start()

    def finish(ins, outs, sems):
        copies = plan(ins, outs, sems)
        for cp in copies:
            cp.wait_recv()
        for cp in copies:
            cp.wait_send()

    scratch = [pltpu.SemaphoreType.DMA((n_copies,)), pltpu.SemaphoreType.DMA((n_copies,))]
    return _Exchange(inputs, out_shape, scratch, start, finish)


def _rs_pair_exchange(grads):
    n = len(grads)
    kinds = [kd for _, kd in grads]
    shapes = [_shard_shape(a, kd) for a, kd in grads]

    def plan(ins, outs, sems):
        send_sems, recv_sems = sems
        x, y, c = _coords()
        copies = []
        for p in range(n):
            size = shapes[p][1] if kinds[p] == "col" else shapes[p][0]
            for s in range(4):
                src = _window(ins[p], kinds[p], 2 * s + (1 - c), size)
                copies.append(pltpu.make_async_remote_copy(
                    src_ref=src, dst_ref=outs[p].at[s], send_sem=send_sems.at[4 * p + s], recv_sem=recv_sems.at[4 * p + s],
                    device_id=(x, y, 1 - c), device_id_type=MESH_ID))
        return copies

    return _symmetric_exchange([a for a, _ in grads], [jax.ShapeDtypeStruct((4,) + shapes[p], BF16) for p in range(n)],
                               4 * n, plan)


def _rs_chips_exchange(sums):
    n = len(sums)

    def plan(ins, outs, sems):
        send_sems, recv_sems = sems
        x, y, c = _coords()
        chips = [(1 - x, y), (x, 1 - y), (1 - x, 1 - y)]
        copies = []
        for p in range(n):
            for q, (cx, cy) in enumerate(chips):
                copies.append(pltpu.make_async_remote_copy(
                    src_ref=ins[p].at[2 * cx + cy], dst_ref=outs[p].at[q],
                    send_sem=send_sems.at[3 * p + q], recv_sem=recv_sems.at[3 * p + q],
                    device_id=(cx, cy, c), device_id_type=MESH_ID))
        return copies

    return _symmetric_exchange(list(sums), [jax.ShapeDtypeStruct((3,) + a.shape[1:], BF16) for a in sums], 3 * n, plan)


def _rs_pair(grads, name):
    return _run_exchange(_rs_pair_exchange(grads), name)


def _rs_chips(sums, name):
    return _run_exchange(_rs_chips_exchange(sums), name)


def _pair_sum(grad, kd, recv, core, name):
    _, r, cw = recv.shape
    tr = min(r, TM)

    def body(core_ref, g_ref, r_ref, o_ref):
        del core_ref
        o_ref[0] = (g_ref[...].astype(F32) + r_ref[0].astype(F32)).astype(BF16)

    if kd == "col":
        g_spec = pl.BlockSpec((tr, cw), lambda s, i, core_ref: (i, 2 * s + core_ref[0]))
    else:
        g_spec = pl.BlockSpec((tr, cw), lambda s, i, core_ref: ((2 * s + core_ref[0]) * (r // tr) + i, 0))
    grid_spec = pltpu.PrefetchScalarGridSpec(
        num_scalar_prefetch=1, grid=(4, r // tr),
        in_specs=[g_spec, pl.BlockSpec((1, tr, cw), lambda s, i, core_ref: (s, i, 0))],
        out_specs=pl.BlockSpec((1, tr, cw), lambda s, i, core_ref: (s, i, 0)))
    return pl.pallas_call(
        body, name=name, grid_spec=grid_spec, out_shape=jax.ShapeDtypeStruct(recv.shape, BF16),
        compiler_params=_cparams(2),
    )(core, grad, recv)


def _adam_math(w, g, m, v):
    m2 = B1 * m + (1.0 - B1) * g
    v2 = B2 * v + (1.0 - B2) * (g * g)
    m_hat = m2 / (1.0 - B1 ** STEP)
    v_hat = v2 / (1.0 - B2 ** STEP)
    delta = -LR * (m_hat / (jnp.sqrt(v_hat) + ADAM_EPS) + WD * w)
    return delta, m2, v2


def _chip_sum_adam(items, chip, tr, name, ride=None):
    r = items[0][1].shape[0]
    steps = r // tr
    n_parts = [len(parts) for parts, _, _, _ in items]
    r_in = 0 if ride is None else len(ride.inputs)
    r_out = 0 if ride is None else len(ride.out_shape)
    n_in = sum(2 * k + 3 for k in n_parts)
    n_out = 4 * len(items)

    def body(chip_ref, *refs):
        del chip_ref
        ins, refs = refs[:n_in], refs[n_in:]
        r_ins, refs = refs[:r_in], refs[r_in:]
        outs, refs = refs[:n_out], refs[n_out:]
        r_outs, sems = refs[:r_out], refs[r_out:]
        i = pl.program_id(0)
        if ride is not None:
            @pl.when(i == 0)
            def _():
                ride.start(r_ins, r_outs, sems)

        pos = 0
        for q, (k, (_, w, _, _)) in enumerate(zip(n_parts, items)):
            cols = []
            for _ in range(k):
                p_ref, c_ref = ins[pos], ins[pos + 1]
                pos += 2
                cols.append(p_ref[0].astype(F32) + c_ref[0].astype(F32) + c_ref[1].astype(F32) + c_ref[2].astype(F32))
            g = (cols[0] if k == 1 else jnp.concatenate(cols, axis=1))[:, :w.shape[1]]
            w_ref, m_ref, v_ref = ins[pos:pos + 3]
            pos += 3
            delta, m2, v2 = _adam_math(w_ref[...], g, m_ref[...], v_ref[...])
            outs[4 * q][...] = g
            outs[4 * q + 1][...] = delta
            outs[4 * q + 2][...] = m2
            outs[4 * q + 3][...] = v2

        if ride is not None:
            @pl.when(i == steps - 1)
            def _():
                ride.finish(r_ins, r_outs, sems)

    in_specs, args, out_specs, out_shape = [], [], [], []
    for parts, w, m, v in items:
        for psum, recv in parts:
            pc = psum.shape[2]
            in_specs += [pl.BlockSpec((1, tr, pc), lambda i, chip_ref: (chip_ref[0], i, 0)),
                         pl.BlockSpec((3, tr, pc), lambda i, chip_ref: (0, i, 0))]
            args += [psum, recv]
        loc = pl.BlockSpec((tr, w.shape[1]), lambda i, chip_ref: (i, 0))
        in_specs += [loc] * 3
        args += [w, m, v]
        out_specs += [loc] * 4
        out_shape += [jax.ShapeDtypeStruct(w.shape, F32)] * 4
    grid_spec = pltpu.PrefetchScalarGridSpec(
        num_scalar_prefetch=1, grid=(steps,), in_specs=in_specs + [ANY] * r_in, out_specs=out_specs + [ANY] * r_out,
        scratch_shapes=[] if ride is None else ride.scratch)
    res = pl.pallas_call(
        body, name=name, grid_spec=grid_spec, out_shape=out_shape + ([] if ride is None else ride.out_shape),
        compiler_params=_cparams(1),
    )(chip, *args, *([] if ride is None else ride.inputs))
    return res if ride is None else (res[:n_out], res[n_out:])


def _small_allreduce_adam(partials, params, moms, vels, plain, name, ride=None):
    n, n_plain = len(partials), len(plain)
    summed = list(partials) + list(plain)
    row0 = []
    rows = 0
    for a in summed:
        if a.shape[0] >= 8:
            rows = _pad_to(rows, 8)
        row0.append(rows)
        rows += a.shape[0]
    rows = _pad_to(rows, 8)
    width = max(a.shape[1] for a in summed)
    r_in = 0 if ride is None else len(ride.inputs)
    r_out = 0 if ride is None else len(ride.out_shape)
    n_out = 4 * n + n_plain

    def body(*refs):
        w_in, m_in, v_in = refs[0:n], refs[n:2 * n], refs[2 * n:3 * n]
        g_in, refs = refs[3 * n:4 * n + n_plain], refs[4 * n + n_plain:]
        r_ins, refs = refs[:r_in], refs[r_in:]
        outs, refs = refs[:n_out], refs[n_out:]
        r_outs, refs = refs[:r_out], refs[r_out:]
        pair, chips, send_sems, recv_sems = refs[:4]
        if ride is not None:
            ride.start(r_ins, r_outs, refs[4:])
        x, y, c = _coords()
        chip = 2 * x + y
        pair[c] = jnp.zeros((rows, width), F32)
        for p, a in enumerate(summed):
            r, cw = a.shape
            pair[c, row0[p]:row0[p] + r, 0:cw] = g_in[p][...]
        swap = pltpu.make_async_remote_copy(src_ref=pair.at[c], dst_ref=pair.at[c], send_sem=send_sems.at[0],
                                            recv_sem=recv_sems.at[0], device_id=(x, y, 1 - c), device_id_type=MESH_ID)
        swap.start()
        swap.wait_recv()
        swap.wait_send()
        chips[chip] = pair[0] + pair[1]
        copies = [pltpu.make_async_remote_copy(
            src_ref=chips.at[chip], dst_ref=chips.at[chip], send_sem=send_sems.at[1 + q], recv_sem=recv_sems.at[1 + q],
            device_id=(cx, cy, c), device_id_type=MESH_ID) for q, (cx, cy) in enumerate([(1 - x, y), (x, 1 - y), (1 - x, 1 - y)])]
        for cp in copies:
            cp.start()
        for cp in copies:
            cp.wait_recv()
        for cp in copies:
            cp.wait_send()
        for p, a in enumerate(summed):
            r, cw = a.shape
            g = chips[0, row0[p]:row0[p] + r, 0:cw]
            for q in range(1, 4):
                g = g + chips[q, row0[p]:row0[p] + r, 0:cw]
            if p >= n:
                outs[4 * n + p - n][...] = g
                continue
            delta, m2, v2 = _adam_math(w_in[p][...], g, m_in[p][...], v_in[p][...])
            outs[4 * p][...] = g
            outs[4 * p + 1][...] = delta
            outs[4 * p + 2][...] = m2
            outs[4 * p + 3][...] = v2
        if ride is not None:
            ride.finish(r_ins, r_outs, refs[4:])

    out_shape = []
    for a in partials:
        out_shape += [jax.ShapeDtypeStruct(a.shape, F32)] * 4
    out_shape += [jax.ShapeDtypeStruct(a.shape, F32) for a in plain]
    res = pl.pallas_call(
        body, name=name, in_specs=[VMEM_SPEC] * (4 * n + n_plain) + [ANY] * r_in,
        out_specs=[VMEM_SPEC] * n_out + [ANY] * r_out, out_shape=out_shape + ([] if ride is None else ride.out_shape),
        scratch_shapes=[pltpu.VMEM((2, rows, width), F32), pltpu.VMEM((4, rows, width), F32),
                        pltpu.SemaphoreType.DMA((4,)), pltpu.SemaphoreType.DMA((4,))] + ([] if ride is None else ride.scratch),
    )(*params, *moms, *vels, *partials, *plain, *([] if ride is None else ride.inputs))
    return res if ride is None else (res[:n_out], res[n_out:])


def _local_step(xf, tgt, nseq, seq, cols1_all, later, small_w, core=None, small_step=None):
    d = xf.shape[1]
    dist = core is not None
    n1, n2, gain, pool_w, pool_scale, n3, nf = small_w
    tf = 2 * cols1_all.shape[1] // N_DEV
    consts = _retention_constants()
    cos_t, sin_t = _rotary_tables(seq)
    wp_b = pool_w.astype(BF16)

    def pair_sums(grads, recv, names):
        return [_pair_sum(g, kd, r, core, "pair_sum_" + nm) for (g, kd), r, nm in zip(grads, recv, names)]

    def riding(host):
        return _gather_exchange(later[host])

    act1 = (xf, n1, (cols1_all, 0), (cols1_all, 1), "ffn1_act")
    if dist:
        (h1, b1, sil1, dsil1, s1), (d1_all, win_all, wout_all, gate2_all) = _ffn_act(*act1, ride=riding("ffn1_act"))
        x1, (up2_all,) = _ffn_down(s1, xf, d1_all, "ffn1_down", ride=riding("ffn1_down"))
    else:
        d1_all, win_all, wout_all, gate2_all, up2_all, d2_all = (later[k] for k in ("down1", "w_in", "w_out", "gate2",
                                                                                   "up2", "down2"))
        h1, b1, sil1, dsil1, s1 = _ffn_act(*act1)
        x1 = _ffn_down(s1, xf, d1_all, "ffn1_down")
    h2, qs, kr, vv, gg, uu = _mix_in(x1, n2, win_all, cos_t, sin_t, seq, "mix_in")
    fwd_mix = (qs, kr, vv, gg, uu, x1, consts, gain, wp_b, pool_scale, wout_all, nseq, seq, "mix_core_fwd")
    if dist:
        (x2, mix, oo, pooled, states), (d2_all,) = _mix_core_fwd(*fwd_mix, ride=riding("mix_core_fwd"))
        (x3, h3, b3, sil3, dsil3), (cols2_t, d2_t, win_t, wout_t) = _ffn_fwd(
            x2, n3, (gate2_all, 0), (up2_all, 0), d2_all, "ffn2_fwd", ride=riding("ffn2_fwd"))
    else:
        x2, mix, oo, pooled, states = _mix_core_fwd(*fwd_mix)
        x3, h3, b3, sil3, dsil3 = _ffn_fwd(x2, n3, (gate2_all, 0), (up2_all, 0), d2_all, "ffn2_fwd")
        cols1_t = _transpose(cols1_all, "transpose_cols1")
        cols2_t = jnp.concatenate([_transpose(gate2_all, "transpose_gate2"), _transpose(up2_all, "transpose_up2")], axis=1)
        d1_t = _transpose(d1_all, "transpose_down1")
        d2_t = _transpose(d2_all, "transpose_down2")
        win_t = _transpose(win_all, "transpose_w_in")
        wout_t = _transpose(wout_all, "transpose_w_out")
    dx3, dnf, loss_part, dx3b = _loss_head(x3, nf, tgt, "loss_head")
    out = {}

    if dist:
        (da3, db3, g_wd2), (cols1_t, d1_t) = _ffn_bwd_act(dx3b, b3, sil3, dsil3, d2_t, "ffn2_bwd_act",
                                                          ride=riding("ffn2_bwd_act"))
    else:
        da3, db3, g_wd2 = _ffn_bwd_act(dx3b, b3, sil3, dsil3, d2_t, "ffn2_bwd_act")
    names2 = ["ffn2_gate", "ffn2_up", "ffn2_down"]
    grads2 = [(_wgrad(da3, h3, 1.0, tf, d, "wgrad_gate2"), "row"), (_wgrad(db3, h3, 1.0, tf, d, "wgrad_up2"), "row"),
              (g_wd2, "row")]
    if dist:
        (dx2, dn3), recv2 = _ffn_bwd_in(da3, db3, dx3, x2, n3, cols2_t, 0, "ffn2_bwd_in", ride=_rs_pair_exchange(grads2))
        sums2 = pair_sums(grads2, recv2, names2)
        (dp, dx2b, dgain, dscale, dwp), crecv2 = _mix_core_bwd(
            dx2, qs, kr, vv, gg, oo, pooled, states, consts, gain, wp_b, pool_scale, wout_t, cos_t, sin_t, nseq, seq,
            "mix_core_bwd", ride=_rs_chips_exchange(sums2))
        out.update({nm: [(s, r)] for nm, s, r in zip(names2, sums2, crecv2)})
    else:
        dx2, dn3 = _ffn_bwd_in(da3, db3, dx3, x2, n3, cols2_t, 0, "ffn2_bwd_in")
        dp, dx2b, dgain, dscale, dwp = _mix_core_bwd(dx2, qs, kr, vv, gg, oo, pooled, states, consts, gain, wp_b,
                                                     pool_scale, wout_t, cos_t, sin_t, nseq, seq, "mix_core_bwd")
        out.update(dict(zip(names2, grads2)))

    names_m = ["w_in", "w_out"]
    grads_m = [(_wgrad(h2, dp, 1.0, d, d, "wgrad_in"), "col"), (_wgrad(mix, dx2b, 1.0, d, d, "wgrad_out"), "row")]
    if dist:
        (dx1, dn2, dx1b), recv_m = _mix_in_bwd(dp, dx2, x1, n2, win_t, "mix_in_bwd", ride=_rs_pair_exchange(grads_m))
        sums_m = pair_sums(grads_m, recv_m, names_m)
        (da1, db1, g_wd1), crecv_m = _ffn_bwd_act(dx1b, b1, sil1, dsil1, d1_t, "ffn1_bwd_act",
                                                  ride=_rs_chips_exchange(sums_m))
        out.update({nm: [(s, r)] for nm, s, r in zip(names_m, sums_m, crecv_m)})
    else:
        dx1, dn2, dx1b = _mix_in_bwd(dp, dx2, x1, n2, win_t, "mix_in_bwd")
        da1, db1, g_wd1 = _ffn_bwd_act(dx1b, b1, sil1, dsil1, d1_t, "ffn1_bwd_act")
        out.update(dict(zip(names_m, grads_m)))

    dx0, dn1 = _ffn_bwd_in(da1, db1, dx1, xf, n1, cols1_t, 0, "ffn1_bwd_in")
    small_parts = (dn1, dn2, dgain, dwp, dscale, dn3, dnf)
    g_down = (g_wd1, "row")
    if dist:
        half = d // 2
        both = lambda first, second: _join([_rs_chips_exchange(first), _rs_pair_exchange([second])])
        g_gate, recv_d = _wgrad(da1, h1, 1.0, tf, d, "wgrad_gate1", ride=_rs_pair_exchange([g_down]))
        g_gate = (g_gate, "row")
        sum_d = pair_sums([g_down], recv_d, ["ffn1_down"])
        g_lo, (crecv_d, recv_g) = _wgrad(db1, h1, 1.0, tf, half, "wgrad_up1_lo", ride=both(sum_d, g_gate), b_cols=(0, 1))
        g_lo = (g_lo, "row")
        sum_g = pair_sums([g_gate], [recv_g], ["ffn1_gate"])
        g_hi, (crecv_g, recv_lo) = _wgrad(db1, h1, 1.0, tf, half, "wgrad_up1_hi", ride=both(sum_g, g_lo), b_cols=(1, 1))
        g_hi = (g_hi, "row")
        sum_lo = pair_sums([g_lo], [recv_lo], ["ffn1_up_lo"])
        small_out, (crecv_lo, recv_hi) = small_step(small_parts, loss_part, both(sum_lo, g_hi))
        sum_hi = pair_sums([g_hi], [recv_hi], ["ffn1_up_hi"])
        out.update({"ffn1_gate": [(sum_g[0], crecv_g)], "ffn1_down": [(sum_d[0], crecv_d)],
                    "ffn1_up": [(sum_lo[0], crecv_lo), (sum_hi[0], None)]})
        return small_out[-1], dx0, out, small_out[:-1], _rs_chips_exchange(sum_hi)
    out.update({"ffn1_gate": (_wgrad(da1, h1, 1.0, tf, d, "wgrad_gate1"), "row"),
                "ffn1_up": (_wgrad(db1, h1, 1.0, tf, d, "wgrad_up1"), "row"), "ffn1_down": g_down})
    return loss_part, dx0, out, small_parts


def kernel(x, norm_ffn1, ffn1_gate, ffn1_up, ffn1_down, norm_mix, w_in, ret_gn_gain, pool_w, pool_scale, w_out, norm_ffn2, ffn2_gate, ffn2_up, ffn2_down, norm_final, loss_target, m_norm_ffn1, m_ffn1_gate, m_ffn1_up, m_ffn1_down, m_norm_mix, m_w_in, m_ret_gn_gain, m_pool_w, m_pool_scale, m_w_out, m_norm_ffn2, m_ffn2_gate, m_ffn2_up, m_ffn2_down, m_norm_final, v_norm_ffn1, v_ffn1_gate, v_ffn1_up, v_ffn1_down, v_norm_mix, v_w_in, v_ret_gn_gain, v_pool_w, v_pool_scale, v_w_out, v_norm_ffn2, v_ffn2_gate, v_ffn2_up, v_ffn2_down, v_norm_final):
    nseq, seq, d = x.shape
    t = nseq * seq
    f_loc = ffn1_gate.shape[2]
    f_pad = _pad_to(f_loc, LANE)
    xf = x.reshape(t, d)
    tgt = loss_target.reshape(t, d)
    core = lax.axis_index("c").astype(jnp.int32).reshape(1)
    chip = (2 * lax.axis_index("x") + lax.axis_index("y")).astype(jnp.int32).reshape(1)

    colp = lambda w: jnp.pad(w[0].astype(BF16), ((0, 0), (0, f_pad - f_loc)))
    rowp = lambda w: jnp.pad(w[0].astype(BF16), ((0, f_pad - f_loc), (0, 0)))
    gate2, up2 = colp(ffn2_gate), colp(ffn2_up)
    cols1 = jnp.concatenate([colp(ffn1_gate), colp(ffn1_up)], axis=0)
    cols2_t = jnp.concatenate([gate2.T, up2.T], axis=1)
    (cols1_all,) = _all_gather([(cols1, "col")], "all_gather_ffn1")
    d1_loc, d2_loc, win_loc, wout_loc = rowp(ffn1_down), rowp(ffn2_down), w_in[0].astype(BF16), w_out[0].astype(BF16)
    later = {"ffn1_act": [(d1_loc, "row"), (win_loc, "col"), (wout_loc, "row"), (gate2, "col")],
             "ffn1_down": [(up2, "col")], "mix_core_fwd": [(d2_loc, "row")],
             "ffn2_fwd": [(cols2_t, "row"), (d2_loc.T, "col"), (win_loc.T, "row"), (wout_loc.T, "col")],
             "ffn2_bwd_act": [(cols1.T, "row"), (d1_loc.T, "col")]}

    flat = lambda a: a.reshape(pool_w.size // d, d)
    params = [norm_ffn1, norm_mix, ret_gn_gain, flat(pool_w), pool_scale, norm_ffn2, norm_final.reshape(1, d)]
    moms = [m_norm_ffn1, m_norm_mix, m_ret_gn_gain, flat(m_pool_w), m_pool_scale, m_norm_ffn2, m_norm_final.reshape(1, d)]
    vels = [v_norm_ffn1, v_norm_mix, v_ret_gn_gain, flat(v_pool_w), v_pool_scale, v_norm_ffn2, v_norm_final.reshape(1, d)]

    def small_step(parts, loss_part, ride):
        dn1, dn2, dgain, dwp, dscale, dn3, dnf = parts
        return _small_allreduce_adam([dn1, dn2, dgain, flat(dwp), dscale, dn3, dnf], params, moms, vels, [loss_part],
                                     "small_allreduce_adam", ride)

    small_w = (norm_ffn1, norm_mix, ret_gn_gain, pool_w[0], pool_scale, norm_ffn2, norm_final.reshape(1, d))
    loss_sum, dx0, reduced, small_out, pending = _local_step(xf, tgt, nseq, seq, cols1_all, later, small_w, core,
                                                             small_step)

    local = {"ffn1_gate": (ffn1_gate, m_ffn1_gate, v_ffn1_gate), "ffn1_up": (ffn1_up, m_ffn1_up, v_ffn1_up),
             "ffn1_down": (ffn1_down, m_ffn1_down, v_ffn1_down), "w_in": (w_in, m_w_in, v_w_in),
             "w_out": (w_out, m_w_out, v_w_out), "ffn2_gate": (ffn2_gate, m_ffn2_gate, v_ffn2_gate),
             "ffn2_up": (ffn2_up, m_ffn2_up, v_ffn2_up), "ffn2_down": (ffn2_down, m_ffn2_down, v_ffn2_down)}
    flip = lambda nm: nm.endswith("gate") or nm.endswith("up")

    def item(nm):
        view = (lambda a: a[0].T) if flip(nm) else (lambda a: a[0])
        w, m, v = local[nm]
        return reduced[nm], view(w), view(m), view(v)

    big = {}

    def keep(names, res):
        for q, nm in enumerate(names):
            big[nm] = tuple((a.T if flip(nm) else a)[None] for a in res[4 * q:4 * q + 4])

    second = ["ffn2_gate", "ffn2_up", "ffn2_down"]
    res, (last_recv,) = _chip_sum_adam([item(nm) for nm in second], chip, item(second[0])[1].shape[0] // 2, "adam_ffn2",
                                       ride=pending)
    keep(second, res)
    reduced["ffn1_up"][-1] = (reduced["ffn1_up"][-1][0], last_recv)
    for nm in ["ffn1_gate", "ffn1_up", "ffn1_down", "w_in", "w_out"]:
        rows = item(nm)[1].shape[0]
        keep([nm], _chip_sum_adam([item(nm)], chip, min(rows, TM), "adam_" + nm))

    small_names = ["norm_ffn1", "norm_mix", "ret_gn_gain", "pool_w", "pool_scale", "norm_ffn2", "norm_final"]
    shapes = [norm_ffn1.shape, norm_mix.shape, ret_gn_gain.shape, pool_w.shape, pool_scale.shape, norm_ffn2.shape,
              norm_final.shape]
    small = {nm: tuple(small_out[4 * p + q].reshape(shapes[p]) for q in range(4)) for p, nm in enumerate(small_names)}

    loss = loss_sum[0, 0]
    order = ["norm_ffn1", "ffn1_gate", "ffn1_up", "ffn1_down", "norm_mix", "w_in", "ret_gn_gain", "pool_w", "pool_scale",
             "w_out", "norm_ffn2", "ffn2_gate", "ffn2_up", "ffn2_down", "norm_final"]
    both = {**big, **small}
    outs = [loss, dx0.reshape(nseq, seq, d)]
    for q in range(4):
        outs += [both[nm][q] for nm in order]
    return tuple(outs)
```

```python
import functools

import numpy as np
import jax
import jax.numpy as jnp
from jax import lax
from jax.experimental import pallas as pl
from jax.experimental.pallas import tpu as pltpu

F32, BF16 = jnp.float32, jnp.bfloat16
MESH_ID = pl.DeviceIdType.MESH
ANY = pl.BlockSpec(memory_space=pl.ANY)
VMEM_SPEC = pl.BlockSpec(memory_space=pltpu.VMEM)

N_DEV = 8
RMS_EPS = 1e-6
GN_EPS = 1e-5
HEADS, DK, DV = 4, 64, 128
QK_W, V_W, POOL_W = HEADS * DK, HEADS * DV, 512
WINDOWS = (2, 4, 8, 16)
GC = POOL_W // len(WINDOWS)
CHUNK = 64
BLK = 4 * CHUNK
HALO = 16
ROPE_BASE = 10000.0
LR, B1, B2, ADAM_EPS, WD, STEP = 0.001, 0.9, 0.999, 1e-08, 0.01, 10
LANE = 128
TM = 512
FFN_TM = 1024
FFN_FWD_TF = 512
WGRAD_TT = 4096
VMEM_LIMIT = 56 * 1024 * 1024


def _cparams(n_axes):
    return pltpu.CompilerParams(dimension_semantics=("arbitrary",) * n_axes, vmem_limit_bytes=VMEM_LIMIT)


class _Exchange:
    def __init__(self, inputs, out_shape, scratch, start, finish, mid=None):
        self.inputs, self.out_shape, self.scratch = list(inputs), list(out_shape), list(scratch)
        self.start, self.finish, self.mid = start, finish, mid


def _pallas(body, name, grid, in_specs, out_specs, out_shape, scratch_shapes, args, ride=None):
    n_axes = len(grid)
    if ride is None:
        return pl.pallas_call(body, name=name, grid=grid, in_specs=in_specs, out_specs=out_specs, out_shape=out_shape,
                              scratch_shapes=scratch_shapes, compiler_params=_cparams(n_axes))(*args)
    n_in, n_out, n_scr = len(in_specs), len(out_specs), len(scratch_shapes)
    r_in, r_out = len(ride.inputs), len(ride.out_shape)

    def hosted(*refs):
        ins, refs = refs[:n_in], refs[n_in:]
        r_ins, refs = refs[:r_in], refs[r_in:]
        outs, refs = refs[:n_out], refs[n_out:]
        r_outs, refs = refs[:r_out], refs[r_out:]
        scr, sems = refs[:n_scr], refs[n_scr:]
        ids = [pl.program_id(a) for a in range(n_axes)]
        first, last, inner0 = ids[0] == 0, ids[0] == grid[0] - 1, None
        for a in range(1, n_axes):
            first = first & (ids[a] == 0)
            last = last & (ids[a] == grid[a] - 1)
            inner0 = (ids[a] == 0) if inner0 is None else inner0 & (ids[a] == 0)

        @pl.when(first)
        def _():
            ride.start(r_ins, r_outs, sems)

        if ride.mid is not None:
            at_mid = ids[0] == grid[0] - 1
            if inner0 is not None:
                at_mid = at_mid & inner0

            @pl.when(at_mid)
            def _():
                ride.mid(r_ins, r_outs, sems)

        body(*ins, *outs, *scr)

        @pl.when(last)
        def _():
            ride.finish(r_ins, r_outs, sems)

    res = pl.pallas_call(
        hosted, name=name, grid=grid, in_specs=list(in_specs) + [ANY] * r_in, out_specs=list(out_specs) + [ANY] * r_out,
        out_shape=list(out_shape) + ride.out_shape, scratch_shapes=list(scratch_shapes) + ride.scratch,
        compiler_params=_cparams(n_axes))(*args, *ride.inputs)
    return res[:n_out], res[n_out:]


def _dot(a, b):
    return jnp.dot(a, b, preferred_element_type=F32)


def _dot_nt(a, b):
    return lax.dot_general(a, b, (((1,), (1,)), ((), ())), preferred_element_type=F32)


def _dot_tn(a, b):
    return lax.dot_general(a, b, (((0,), (0,)), ((), ())), preferred_element_type=F32)


def _sigmoid(x):
    return 0.5 * jnp.tanh(0.5 * x) + 0.5


def _transpose(w, name):
    r, c = w.shape
    tb = 512

    def body(x_ref, o_ref):
        o_ref[...] = x_ref[...].T

    return pl.pallas_call(
        body, name=name, grid=(r // tb, c // tb),
        in_specs=[pl.BlockSpec((tb, tb), lambda i, j: (i, j))],
        out_specs=pl.BlockSpec((tb, tb), lambda i, j: (j, i)),
        out_shape=jax.ShapeDtypeStruct((c, r), w.dtype),
        compiler_params=_cparams(2),
    )(w)


def _pad_to(n, m):
    return (n + m - 1) // m * m


def _retention_constants():
    gamma = (1.0 - 2.0 ** (-5.0 - np.arange(HEADS, dtype=np.float32))).astype(np.float32)
    log_g = np.log(gamma).astype(np.float32)
    i = np.arange(BLK)
    diff = (i[:, None] - i[None, :]).astype(np.float32)
    same = (i[:, None] // CHUNK) == (i[None, :] // CHUNK)
    earlier = (i[None, :] // CHUNK) < (i[:, None] // CHUNK)
    mask = np.zeros((HEADS, BLK, BLK), np.float32)
    for h in range(HEADS):
        dec_abs = np.exp(log_g[h] * np.abs(diff)).astype(np.float32)
        dec = np.exp(log_g[h] * diff * earlier).astype(np.float32)
        mask[h] = np.where(same, dec_abs, np.where(earlier, dec, 0.0))
    dq = np.zeros((BLK, V_W), np.float32)
    dk = np.zeros((BLK, QK_W), np.float32)
    gbd = np.zeros((QK_W, V_W), np.float32)
    for h in range(HEADS):
        dq[:, h * DV:(h + 1) * DV] = np.exp(log_g[h] * (i + 1.0)).astype(np.float32)[:, None]
        dk[:, h * DK:(h + 1) * DK] = np.exp(log_g[h] * (BLK - 1.0 - i)).astype(np.float32)[:, None]
        gbd[h * DK:(h + 1) * DK, h * DV:(h + 1) * DV] = np.exp(log_g[h] * np.float32(BLK))
    bd = (gbd > 0).astype(np.float32)
    return jnp.asarray(mask), jnp.asarray(dq), jnp.asarray(dk), jnp.asarray(gbd), jnp.asarray(bd)


def _rotary_tables(seq):
    half = DK // 2
    freqs = ROPE_BASE ** (-jnp.arange(half, dtype=F32) * 2.0 / DK)
    ang = jnp.arange(seq, dtype=F32)[:, None] * freqs[None, :]
    cos, sin = jnp.cos(ang), jnp.sin(ang)
    cos_t = jnp.tile(jnp.concatenate([cos, cos], axis=1), (1, HEADS))
    sin_t = jnp.tile(jnp.concatenate([-sin, sin], axis=1), (1, HEADS))
    return cos_t, sin_t


def _swap_halves(x):
    lane = lax.broadcasted_iota(jnp.int32, (1, QK_W), 1)
    first = (lane & (DK - 1)) < DK // 2
    return jnp.where(first, pltpu.roll(x, QK_W - DK // 2, 1), pltpu.roll(x, DK // 2, 1))


def _head_mask(h):
    lane = lax.broadcasted_iota(jnp.int32, (1, QK_W), 1)
    return (lane >= h * DK) & (lane < (h + 1) * DK)


def _ffn_fwd(x, n, gate, up, wd, name, ride=None):
    t, d = x.shape
    (wg, gq), (wu, uq) = gate, up
    fp = wg.shape[1]
    tm = min(t, FFN_TM)
    tf = FFN_FWD_TF
    nj = fp // tf

    def body(x_ref, n_ref, wg_ref, wu_ref, wd_ref, xo_ref, h_ref, b_ref, sil_ref, dsil_ref, acc_ref):
        j = pl.program_id(1)

        @pl.when(j == 0)
        def _():
            xv = x_ref[...]
            r = lax.rsqrt(jnp.mean(xv * xv, axis=-1, keepdims=True) + RMS_EPS)
            h_ref[...] = (xv * r * n_ref[...]).astype(BF16)
            acc_ref[...] = jnp.zeros_like(acc_ref)

        h = h_ref[...]
        a = _dot(h, wg_ref[...])
        b = _dot(h, wu_ref[...])
        sg = _sigmoid(a)
        sil = a * sg
        b_ref[...] = b.astype(BF16)
        sil_ref[...] = sil.astype(BF16)
        dsil_ref[...] = (sg + sil * (1.0 - sg)).astype(BF16)
        acc_ref[...] += _dot((sil * b).astype(BF16), wd_ref[...])

        @pl.when(j == nj - 1)
        def _():
            xo_ref[...] = x_ref[...] + 0.5 * acc_ref[...]

    act = pl.BlockSpec((tm, tf), lambda i, j: (i, j))
    return _pallas(
        body, name, (t // tm, nj),
        [pl.BlockSpec((tm, d), lambda i, j: (i, 0)), pl.BlockSpec((1, d), lambda i, j: (0, 0)),
         pl.BlockSpec((d, tf), lambda i, j: (gq, j)), pl.BlockSpec((d, tf), lambda i, j: (uq, j)),
         pl.BlockSpec((tf, d), lambda i, j: (j, 0))],
        [pl.BlockSpec((tm, d), lambda i, j: (i, 0)), pl.BlockSpec((tm, d), lambda i, j: (i, 0)), act, act, act],
        [jax.ShapeDtypeStruct((t, d), F32), jax.ShapeDtypeStruct((t, d), BF16)] + [jax.ShapeDtypeStruct((t, fp), BF16)] * 3,
        [pltpu.VMEM((tm, d), F32)], (x, n, wg, wu, wd), ride)


def _ffn_act(x, n, gate, up, name, ride=None):
    t, d = x.shape
    (wg, gq), (wu, uq) = gate, up
    fp = wg.shape[1]
    tm = min(t, FFN_TM)
    tf = 2 * fp // N_DEV
    nj = fp // tf

    def body(x_ref, n_ref, wg_ref, wu_ref, h_ref, b_ref, sil_ref, dsil_ref, s_ref):
        @pl.when(pl.program_id(1) == 0)
        def _():
            xv = x_ref[...]
            r = lax.rsqrt(jnp.mean(xv * xv, axis=-1, keepdims=True) + RMS_EPS)
            h_ref[...] = (xv * r * n_ref[...]).astype(BF16)

        h = h_ref[...]
        a = _dot(h, wg_ref[...])
        b = _dot(h, wu_ref[...])
        sg = _sigmoid(a)
        sil = a * sg
        b_ref[...] = b.astype(BF16)
        sil_ref[...] = sil.astype(BF16)
        dsil_ref[...] = (sg + sil * (1.0 - sg)).astype(BF16)
        s_ref[...] = (sil * b).astype(BF16)

    act = pl.BlockSpec((tm, tf), lambda i, j: (i, j))
    return _pallas(
        body, name, (t // tm, nj),
        [pl.BlockSpec((tm, d), lambda i, j: (i, 0)), pl.BlockSpec((1, d), lambda i, j: (0, 0)),
         pl.BlockSpec((d, tf), lambda i, j: (gq, j)), pl.BlockSpec((d, tf), lambda i, j: (uq, j))],
        [pl.BlockSpec((tm, d), lambda i, j: (i, 0)), act, act, act, act],
        [jax.ShapeDtypeStruct((t, d), BF16)] + [jax.ShapeDtypeStruct((t, fp), BF16)] * 4,
        [], (x, n, wg, wu), ride)


def _ffn_down(s, x, wd, name, ride=None):
    t, d = x.shape
    fp = wd.shape[0]
    tm = min(t, FFN_TM)

    def body(s_ref, x_ref, wd_ref, xo_ref):
        xo_ref[...] = x_ref[...] + 0.5 * _dot(s_ref[...], wd_ref[...])

    row = pl.BlockSpec((tm, d), lambda i: (i, 0))
    res = _pallas(body, name, (t // tm,), [pl.BlockSpec((tm, fp), lambda i: (i, 0)), row, pl.BlockSpec((fp, d), lambda i: (0, 0))],
                  [row], [jax.ShapeDtypeStruct((t, d), F32)], [], (s, x, wd), ride)
    return res[0] if ride is None else (res[0][0], res[1])


def _ffn_bwd_act(dxob, b, sil, dsil, wd_t, name, ride=None):
    t, d = dxob.shape
    fp = wd_t.shape[1]
    tm = min(t, FFN_TM)
    tf = 2 * fp // N_DEV
    ni = t // tm

    def body(dx_ref, b_ref, sil_ref, dsil_ref, wd_ref, da_ref, db_ref, gd_ref, acc_ref):
        i = pl.program_id(1)

        @pl.when(i == 0)
        def _():
            acc_ref[...] = jnp.zeros_like(acc_ref)

        dxv = dx_ref[...]
        bv, sv = b_ref[...].astype(F32), sil_ref[...].astype(F32)
        ds = _dot(dxv, wd_ref[...])
        da_ref[...] = (ds * bv * dsil_ref[...].astype(F32)).astype(BF16)
        db_ref[...] = (ds * sv).astype(BF16)
        acc_ref[...] += _dot_tn((sv * bv).astype(BF16), dxv)

        @pl.when(i == ni - 1)
        def _():
            gd_ref[...] = acc_ref[...].astype(BF16)

    act = pl.BlockSpec((tm, tf), lambda c, i: (i, c))
    return _pallas(
        body, name, (fp // tf, ni),
        [pl.BlockSpec((tm, d), lambda c, i: (i, 0)), act, act, act, pl.BlockSpec((d, tf), lambda c, i: (0, c))],
        [act, act, pl.BlockSpec((tf, d), lambda c, i: (c, 0))],
        [jax.ShapeDtypeStruct((t, fp), BF16), jax.ShapeDtypeStruct((t, fp), BF16), jax.ShapeDtypeStruct((fp, d), BF16)],
        [pltpu.VMEM((tf, d), F32)], (dxob, b, sil, dsil, wd_t), ride)


def _ffn_bwd_in(da, db, dxo, x, n, cols_t, gq, name, ride=None):
    t, d = x.shape
    fp = cols_t.shape[0]
    tm = min(t, FFN_TM)
    tf = 2 * fp // N_DEV
    nj = fp // tf

    def body(da_ref, db_ref, dxo_ref, x_ref, n_ref, wg_ref, wu_ref, dx_ref, dn_ref, acc_ref):
        i, j = pl.program_id(0), pl.program_id(1)

        @pl.when((i == 0) & (j == 0))
        def _():
            dn_ref[...] = jnp.zeros_like(dn_ref)

        @pl.when(j == 0)
        def _():
            acc_ref[...] = jnp.zeros_like(acc_ref)

        acc_ref[...] += _dot(da_ref[...], wg_ref[...]) + _dot(db_ref[...], wu_ref[...])

        @pl.when(j == nj - 1)
        def _():
            xv = x_ref[...]
            r = lax.rsqrt(jnp.mean(xv * xv, axis=-1, keepdims=True) + RMS_EPS)
            xh = xv * r
            dh = acc_ref[...]
            dn_ref[...] += jnp.sum(dh * xh, axis=0, keepdims=True)
            dhn = dh * n_ref[...]
            dx_ref[...] = dxo_ref[...] + r * (dhn - xh * jnp.mean(dhn * xh, axis=-1, keepdims=True))

    act = pl.BlockSpec((tm, tf), lambda i, j: (i, j))
    row = pl.BlockSpec((tm, d), lambda i, j: (i, 0))
    return _pallas(
        body, name, (t // tm, nj),
        [act, act, row, row, pl.BlockSpec((1, d), lambda i, j: (0, 0)),
         pl.BlockSpec((tf, d), lambda i, j: (j, gq)), pl.BlockSpec((tf, d), lambda i, j: (j, gq + 1))],
        [row, pl.BlockSpec((1, d), lambda i, j: (0, 0))],
        [jax.ShapeDtypeStruct((t, d), F32), jax.ShapeDtypeStruct((1, d), F32)],
        [pltpu.VMEM((tm, d), F32)], (da, db, dxo, x, n, cols_t, cols_t), ride)


def _wgrad(a, b, scale, tk, tn, name, ride=None, b_cols=None):
    t, k = a.shape
    q0, nq = (0, b.shape[1] // tn) if b_cols is None else b_cols
    n = nq * tn
    tt = min(t, WGRAD_TT)
    nt = t // tt

    def body(a_ref, b_ref, o_ref, acc_ref):
        s = pl.program_id(2)

        @pl.when(s == 0)
        def _():
            acc_ref[...] = jnp.zeros_like(acc_ref)

        acc_ref[...] += _dot_tn(a_ref[...], b_ref[...])

        @pl.when(s == nt - 1)
        def _():
            o_ref[...] = (scale * acc_ref[...]).astype(BF16)

    res = _pallas(
        body, name, (k // tk, n // tn, nt),
        [pl.BlockSpec((tt, tk), lambda p, q, s: (s, p)), pl.BlockSpec((tt, tn), lambda p, q, s: (s, q + q0))],
        [pl.BlockSpec((tk, tn), lambda p, q, s: (p, q))], [jax.ShapeDtypeStruct((k, n), BF16)],
        [pltpu.VMEM((tk, tn), F32)], (a, b), ride)
    return res[0] if ride is None else (res[0][0], res[1])


def _mix_in(x, n, w_in, cos_t, sin_t, seq, name):
    t, d = x.shape
    per_seq = seq // TM

    def body(x_ref, n_ref, w_ref, c_ref, s_ref, h_ref, q_ref, k_ref, v_ref, g_ref, u_ref):
        xv = x_ref[...]
        r = lax.rsqrt(jnp.mean(xv * xv, axis=-1, keepdims=True) + RMS_EPS)
        h = (xv * r * n_ref[...]).astype(BF16)
        h_ref[...] = h
        p = _dot(h, w_ref[...])
        c, s = c_ref[...], s_ref[...]
        q = p[:, :QK_W]
        k = p[:, QK_W:2 * QK_W]
        q_ref[...] = ((q * c + _swap_halves(q) * s) * (DK ** -0.5)).astype(BF16)
        k_ref[...] = (k * c + _swap_halves(k) * s).astype(BF16)
        v_ref[...] = p[:, 2 * QK_W:2 * QK_W + V_W].astype(BF16)
        g_ref[...] = p[:, 2 * QK_W + V_W:2 * QK_W + 2 * V_W]
        u_ref[...] = p[:, 2 * QK_W + 2 * V_W:]

    tile = lambda w: pl.BlockSpec((TM, w), lambda i: (i, 0))
    return pl.pallas_call(
        body, name=name, grid=(t // TM,),
        in_specs=[tile(d), pl.BlockSpec((1, d), lambda i: (0, 0)), pl.BlockSpec(w_in.shape, lambda i: (0, 0)),
                  pl.BlockSpec((TM, QK_W), lambda i: (i % per_seq, 0)), pl.BlockSpec((TM, QK_W), lambda i: (i % per_seq, 0))],
        out_specs=[tile(d), tile(QK_W), tile(QK_W), tile(V_W), tile(V_W), tile(POOL_W)],
        out_shape=[jax.ShapeDtypeStruct((t, d), BF16), jax.ShapeDtypeStruct((t, QK_W), BF16),
                   jax.ShapeDtypeStruct((t, QK_W), BF16), jax.ShapeDtypeStruct((t, V_W), BF16),
                   jax.ShapeDtypeStruct((t, V_W), F32), jax.ShapeDtypeStruct((t, POOL_W), F32)],
        compiler_params=_cparams(1),
    )(x, n, w_in, cos_t, sin_t)


def _mix_in_bwd(dp, dx2, x1, n, w_in_t, name, ride=None):
    t, d = x1.shape

    def body(dp_ref, dx2_ref, x_ref, n_ref, w_ref, dx_ref, dn_ref, dxb_ref):
        @pl.when(pl.program_id(0) == 0)
        def _():
            dn_ref[...] = jnp.zeros_like(dn_ref)

        dh = _dot(dp_ref[...], w_ref[...])
        xv = x_ref[...]
        r = lax.rsqrt(jnp.mean(xv * xv, axis=-1, keepdims=True) + RMS_EPS)
        xh = xv * r
        dn_ref[...] += jnp.sum(dh * xh, axis=0, keepdims=True)
        dhn = dh * n_ref[...]
        dx = dx2_ref[...] + r * (dhn - xh * jnp.mean(dhn * xh, axis=-1, keepdims=True))
        dx_ref[...] = dx
        dxb_ref[...] = (0.5 * dx).astype(BF16)

    tile = lambda w: pl.BlockSpec((TM, w), lambda i: (i, 0))
    return _pallas(
        body, name, (t // TM,),
        [tile(dp.shape[1]), tile(d), tile(d), pl.BlockSpec((1, d), lambda i: (0, 0)),
         pl.BlockSpec(w_in_t.shape, lambda i: (0, 0))],
        [tile(d), pl.BlockSpec((1, d), lambda i: (0, 0)), tile(d)],
        [jax.ShapeDtypeStruct((t, d), F32), jax.ShapeDtypeStruct((1, d), F32), jax.ShapeDtypeStruct((t, d), BF16)],
        [], (dp, dx2, x1, n, w_in_t), ride)


def _group_norm(o):
    parts, rstds = [], []
    for h in range(HEADS):
        oh = o[:, h * DV:(h + 1) * DV]
        dlt = oh - jnp.mean(oh, axis=-1, keepdims=True)
        rstd = lax.rsqrt(jnp.mean(dlt * dlt, axis=-1, keepdims=True) + GN_EPS)
        parts.append(dlt * rstd)
        rstds.append(rstd)
    return jnp.concatenate(parts, axis=1), rstds


def _mix_core_fwd(qs, k, v, g, u, x1, consts, gain, wp, scale, w_out, nseq, seq, name, ride=None):
    t, d = x1.shape
    nblk = seq // BLK
    mask, dq, dk, gbd, bd = consts

    def body(q_ref, k_ref, v_ref, g_ref, u_ref, x1_ref, m_ref, dq_ref, dk_ref, gbd_ref, bd_ref, gain_ref, wp_ref,
             sc_ref, wo_ref, x2_ref, mix_ref, o_ref, pooled_ref, st_ref, state, halo):
        j = pl.program_id(1)

        @pl.when(j == 0)
        def _():
            state[...] = jnp.zeros_like(state)
            halo[...] = jnp.zeros_like(halo)

        qv, kv, vv = q_ref[...], k_ref[...], v_ref[...]
        st = state[...]
        st_ref[0] = st
        cross = _dot(qv, st.astype(BF16)) * dq_ref[...]
        outs = []
        for h in range(HEADS):
            qh = jnp.where(_head_mask(h), qv, jnp.zeros_like(qv))
            am = (_dot_nt(qh, kv) * m_ref[h]).astype(BF16)
            outs.append(_dot(am, vv[:, h * DV:(h + 1) * DV]))
        o = jnp.concatenate(outs, axis=1) + cross
        o_ref[...] = o
        kd = (kv.astype(F32) * dk_ref[...]).astype(BF16)
        state[...] = gbd_ref[...] * st + _dot_tn(kd, vv) * bd_ref[...]

        gv = g_ref[...]
        nrm, _ = _group_norm(o)
        ret = (gv * _sigmoid(gv)) * (nrm * gain_ref[...])

        uv = u_ref[...]
        c = jnp.concatenate([halo[...], uv], axis=0)
        halo[...] = uv[BLK - HALO:, :]
        pos = j * BLK + lax.broadcasted_iota(jnp.int32, (BLK, 1), 0)
        parts = []
        for gi, w in enumerate(WINDOWS):
            c = c + pltpu.roll(c, w // 2, 0)
            cnt = jnp.minimum(pos + 1, w).astype(F32)
            parts.append(c[HALO:, :GC] / cnt)
            if gi + 1 < len(WINDOWS):
                c = c[:, GC:]
        pooled = (jnp.concatenate(parts, axis=1) - uv).astype(BF16)
        pooled_ref[...] = pooled
        z = jnp.concatenate([_dot(pooled[:, gi * GC:(gi + 1) * GC], wp_ref[gi]) for gi in range(len(WINDOWS))], axis=1)
        mix = jnp.concatenate([ret, z * sc_ref[...]], axis=1).astype(BF16)
        mix_ref[...] = mix
        x2_ref[...] = x1_ref[...] + _dot(mix, wo_ref[...])

    blk = lambda w: pl.BlockSpec((BLK, w), lambda i, j: (i * nblk + j, 0))
    full = lambda a: pl.BlockSpec(a.shape, lambda i, j: (0,) * a.ndim)
    return _pallas(
        body, name, (nseq, nblk),
        [blk(QK_W), blk(QK_W), blk(V_W), blk(V_W), blk(POOL_W), blk(d),
         full(mask), full(dq), full(dk), full(gbd), full(bd), full(gain), full(wp), full(scale), full(w_out)],
        [blk(d), blk(d), blk(V_W), blk(POOL_W), pl.BlockSpec((1, QK_W, V_W), lambda i, j: (i * nblk + j, 0, 0))],
        [jax.ShapeDtypeStruct((t, d), F32), jax.ShapeDtypeStruct((t, d), BF16),
         jax.ShapeDtypeStruct((t, V_W), F32), jax.ShapeDtypeStruct((t, POOL_W), BF16),
         jax.ShapeDtypeStruct((nseq * nblk, QK_W, V_W), F32)],
        [pltpu.VMEM((QK_W, V_W), F32), pltpu.VMEM((HALO, POOL_W), F32)],
        (qs, k, v, g, u, x1, mask, dq, dk, gbd, bd, gain, wp, scale, w_out), ride)


def _mix_core_bwd(dx2, qs, k, v, g, o, pooled, st, consts, gain, wp, scale, w_out, cos_t, sin_t, nseq, seq, name,
                  ride=None):
    t, d = dx2.shape
    nblk = seq // BLK
    mask, dq, dk, gbd, bd = consts
    n_win = len(WINDOWS)

    def body(dx2_ref, q_ref, k_ref, v_ref, g_ref, o_ref, pooled_ref, st_ref, m_ref, dq_ref, dk_ref, gbd_ref, bd_ref,
             gain_ref, wp_ref, sc_ref, wo_ref, c_ref, s_ref,
             dp_ref, dx2b_ref, dgain_ref, dscale_ref, dwp_ref, rstate, carry):
        i, j = pl.program_id(0), pl.program_id(1)

        @pl.when((i == 0) & (j == 0))
        def _():
            dgain_ref[...] = jnp.zeros_like(dgain_ref)
            dscale_ref[...] = jnp.zeros_like(dscale_ref)
            dwp_ref[...] = jnp.zeros_like(dwp_ref)

        @pl.when(j == 0)
        def _():
            rstate[...] = jnp.zeros_like(rstate)
            carry[...] = jnp.zeros_like(carry)

        dx2b = dx2_ref[...].astype(BF16)
        dx2b_ref[...] = dx2b
        dmix = _dot(dx2b, wo_ref[...])
        dret, dpool = dmix[:, :V_W], dmix[:, V_W:]

        gv, ov, gain_v = g_ref[...], o_ref[...], gain_ref[...]
        sg = _sigmoid(gv)
        sil = gv * sg
        nrm, rstds = _group_norm(ov)
        dg = dret * (nrm * gain_v) * (sg * (1.0 + gv * (1.0 - sg)))
        dgn = dret * sil
        dgain_ref[...] += jnp.sum(dgn * nrm, axis=0, keepdims=True)
        dnrm = dgn * gain_v
        do_parts = []
        for h in range(HEADS):
            dn_h = dnrm[:, h * DV:(h + 1) * DV]
            n_h = nrm[:, h * DV:(h + 1) * DV]
            do_parts.append(rstds[h] * (dn_h - jnp.mean(dn_h, axis=-1, keepdims=True)
                                        - n_h * jnp.mean(dn_h * n_h, axis=-1, keepdims=True)))
        do = jnp.concatenate(do_parts, axis=1)
        dob = do.astype(BF16)

        qv, kv, vv = q_ref[...], k_ref[...], v_ref[...]
        stb = st_ref[0].astype(BF16)
        rs = rstate[...]
        rsb = rs.astype(BF16)
        dod = (do * dq_ref[...]).astype(BF16)
        dqs = _dot_nt(dod, stb)
        dst = _dot_tn(qv, dod) * bd_ref[...]
        dkf = dk_ref[...]
        kd = (kv.astype(F32) * dkf).astype(BF16)
        dks = _dot_nt(vv, rsb) * dkf
        dvs = _dot(kd, rsb)
        dv_parts = []
        for h in range(HEADS):
            hm = _head_mask(h)
            qh = jnp.where(hm, qv, jnp.zeros_like(qv))
            mh = m_ref[h]
            am = (_dot_nt(qh, kv) * mh).astype(BF16)
            dpm = (_dot_nt(dob[:, h * DV:(h + 1) * DV], vv[:, h * DV:(h + 1) * DV]) * mh).astype(BF16)
            dqs = dqs + jnp.where(hm, _dot(dpm, kv), 0.0)
            dks = dks + jnp.where(hm, _dot_tn(dpm, qv), 0.0)
            dv_parts.append(_dot_tn(am, dob[:, h * DV:(h + 1) * DV]))
        dvs = dvs + jnp.concatenate(dv_parts, axis=1)
        rstate[...] = dst + gbd_ref[...] * rs

        cv, sv = c_ref[...], s_ref[...]
        dqr = dqs * (DK ** -0.5)
        dq_pre = dqr * cv + _swap_halves(dqr * sv)
        dk_pre = dks * cv + _swap_halves(dks * sv)

        pv = pooled_ref[...]
        sc = sc_ref[...]
        dzb = (dpool * sc).astype(BF16)
        z_parts, dpo_parts = [], []
        for gi in range(n_win):
            p_g = pv[:, gi * GC:(gi + 1) * GC]
            dz_g = dzb[:, gi * GC:(gi + 1) * GC]
            z_parts.append(_dot(p_g, wp_ref[gi]))
            dwp_ref[gi] += _dot_tn(p_g, dz_g)
            dpo_parts.append(_dot_nt(dz_g, wp_ref[gi]))
        dscale_ref[...] += jnp.sum(dpool * jnp.concatenate(z_parts, axis=1), axis=0, keepdims=True)
        dpo = jnp.concatenate(dpo_parts, axis=1)
        pos = (nblk - 1 - j) * BLK + lax.broadcasted_iota(jnp.int32, (BLK, 1), 0)
        e = jnp.concatenate(
            [dpo[:, gi * GC:(gi + 1) * GC] / jnp.minimum(pos + 1, w).astype(F32) for gi, w in enumerate(WINDOWS)], axis=1)
        c = jnp.concatenate([e, carry[...]], axis=0)
        carry[...] = e[:HALO, :]
        rows = BLK + HALO
        lead = []
        for gi, w in enumerate(WINDOWS):
            c = c + pltpu.roll(c, rows - w // 2, 0)
            lead.append(c[:BLK, :GC])
            if gi + 1 < n_win:
                c = c[:, GC:]
        du = jnp.concatenate(lead, axis=1) - dpo

        dp_ref[:, 0:QK_W] = dq_pre.astype(BF16)
        dp_ref[:, QK_W:2 * QK_W] = dk_pre.astype(BF16)
        dp_ref[:, 2 * QK_W:2 * QK_W + V_W] = dvs.astype(BF16)
        dp_ref[:, 2 * QK_W + V_W:2 * QK_W + 2 * V_W] = dg.astype(BF16)
        dp_ref[:, 2 * QK_W + 2 * V_W:] = du.astype(BF16)

    rev = lambda i, j: i * nblk + (nblk - 1 - j)
    blk = lambda w: pl.BlockSpec((BLK, w), lambda i, j: (rev(i, j), 0))
    full = lambda a: pl.BlockSpec(a.shape, lambda i, j: (0,) * a.ndim)
    in_w = 2 * QK_W + 2 * V_W + POOL_W
    return _pallas(
        body, name, (nseq, nblk),
        [blk(d), blk(QK_W), blk(QK_W), blk(V_W), blk(V_W), blk(V_W), blk(POOL_W),
         pl.BlockSpec((1, QK_W, V_W), lambda i, j: (rev(i, j), 0, 0)),
         full(mask), full(dq), full(dk), full(gbd), full(bd), full(gain), full(wp), full(scale), full(w_out),
         pl.BlockSpec((BLK, QK_W), lambda i, j: (nblk - 1 - j, 0)),
         pl.BlockSpec((BLK, QK_W), lambda i, j: (nblk - 1 - j, 0))],
        [blk(in_w), blk(d), pl.BlockSpec((1, V_W), lambda i, j: (0, 0)),
         pl.BlockSpec((1, POOL_W), lambda i, j: (0, 0)), pl.BlockSpec((n_win, GC, GC), lambda i, j: (0, 0, 0))],
        [jax.ShapeDtypeStruct((t, in_w), BF16), jax.ShapeDtypeStruct((t, d), BF16),
         jax.ShapeDtypeStruct((1, V_W), F32), jax.ShapeDtypeStruct((1, POOL_W), F32),
         jax.ShapeDtypeStruct((n_win, GC, GC), F32)],
        [pltpu.VMEM((QK_W, V_W), F32), pltpu.VMEM((HALO, POOL_W), F32)],
        (dx2, qs, k, v, g, o, pooled, st, mask, dq, dk, gbd, bd, gain, wp, scale, w_out, cos_t, sin_t), ride)


def _loss_head(x3, nf, tgt, name):
    t, d = x3.shape

    def body(x_ref, n_ref, t_ref, dx_ref, dn_ref, loss_ref, dxb_ref):
        @pl.when(pl.program_id(0) == 0)
        def _():
            dn_ref[...] = jnp.zeros_like(dn_ref)
            loss_ref[...] = jnp.zeros_like(loss_ref)

        xv = x_ref[...]
        nv = n_ref[...]
        r = lax.rsqrt(jnp.mean(xv * xv, axis=-1, keepdims=True) + RMS_EPS)
        xh = xv * r
        err = xh * nv - t_ref[...]
        row = jnp.mean(err * err, axis=-1, keepdims=True)
        loss_ref[...] += 0.5 * jnp.sum(row, axis=0, keepdims=True)
        dy = err * (1.0 / d)
        dn_ref[...] += jnp.sum(dy * xh, axis=0, keepdims=True)
        dxh = dy * nv
        dx = r * (dxh - xh * jnp.mean(dxh * xh, axis=-1, keepdims=True))
        dx_ref[...] = dx
        dxb_ref[...] = (0.5 * dx).astype(BF16)

    tile = pl.BlockSpec((TM, d), lambda i: (i, 0))
    return pl.pallas_call(
        body, name=name, grid=(t // TM,),
        in_specs=[tile, pl.BlockSpec((1, d), lambda i: (0, 0)), tile],
        out_specs=[tile, pl.BlockSpec((1, d), lambda i: (0, 0)), pl.BlockSpec((1, 1), lambda i: (0, 0)), tile],
        out_shape=[jax.ShapeDtypeStruct((t, d), F32), jax.ShapeDtypeStruct((1, d), F32), jax.ShapeDtypeStruct((1, 1), F32),
                   jax.ShapeDtypeStruct((t, d), BF16)],
        compiler_params=_cparams(1),
    )(x3, nf, tgt)


def _coords():
    return lax.axis_index("x"), lax.axis_index("y"), lax.axis_index("c")


def _window(ref, kind, idx, size):
    if kind == "col":
        return ref.at[:, pl.ds(pl.multiple_of(idx * size, LANE), size)]
    return ref.at[pl.ds(pl.multiple_of(idx * size, 8), size), :]


def _run_exchange(ex, name):
    n_in = len(ex.inputs)

    def body(*refs):
        ins, outs, sems = refs[:n_in], refs[n_in:n_in + len(ex.out_shape)], refs[n_in + len(ex.out_shape):]
        ex.start(ins, outs, sems)
        if ex.mid is not None:
            ex.mid(ins, outs, sems)
        ex.finish(ins, outs, sems)

    return pl.pallas_call(body, name=name, in_specs=[ANY] * n_in, out_specs=[ANY] * len(ex.out_shape),
                          out_shape=ex.out_shape, scratch_shapes=ex.scratch)(*ex.inputs)


def _join(exchanges):
    bounds = []
    i0 = o0 = s0 = 0
    for ex in exchanges:
        bounds.append((i0, o0, s0))
        i0, o0, s0 = i0 + len(ex.inputs), o0 + len(ex.out_shape), s0 + len(ex.scratch)

    def phase(which):
        def run(ins, outs, sems):
            for ex, (i, o, s) in zip(exchanges, bounds):
                fn = getattr(ex, which)
                if fn is not None:
                    fn(ins[i:i + len(ex.inputs)], outs[o:o + len(ex.out_shape)], sems[s:s + len(ex.scratch)])
        return run

    return _Exchange(sum((ex.inputs for ex in exchanges), []), sum((ex.out_shape for ex in exchanges), []),
                     sum((ex.scratch for ex in exchanges), []), phase("start"), phase("finish"),
                     phase("mid") if any(ex.mid is not None for ex in exchanges) else None)


def _gather_exchange(parts):
    n = len(parts)
    kinds = [kd for _, kd in parts]
    sizes = [a.shape[1] if kd == "col" else a.shape[0] for a, kd in parts]

    def plan(ins, outs, sems):
        send_sems, recv_sems, local_sems = sems
        x, y, c = _coords()
        me, sibling = (x, y, c), (x, y, 1 - c)
        chips = [(1 - x, y), (x, 1 - y), (1 - x, 1 - y)]

        def win(p, dev):
            return _window(outs[p], kinds[p], 4 * dev[0] + 2 * dev[1] + dev[2], sizes[p])

        def copy(p, k, block, to, src=None):
            return pltpu.make_async_remote_copy(
                src_ref=win(p, block) if src is None else src, dst_ref=win(p, block),
                send_sem=send_sems.at[p * 7 + k], recv_sem=recv_sems.at[p * 7 + k], device_id=to, device_id_type=MESH_ID)

        mine = [pltpu.make_async_copy(ins[p], win(p, me), local_sems.at[p]) for p in range(n)]
        first, arrived, passed, rest = [], [], [], []
        for p in range(n):
            first.append(copy(p, 0, me, sibling, src=ins[p]))
            first += [copy(p, 1 + q, me, (*chip, c), src=ins[p]) for q, chip in enumerate(chips)]
            rest.append(copy(p, 0, sibling, me))
            rest += [copy(p, 4 + q, (*chip, 1 - c), me) for q, chip in enumerate(chips)]
        for q, chip in enumerate(chips):
            for p in range(n):
                arrived.append(copy(p, 1 + q, (*chip, c), me))
                passed.append(copy(p, 4 + q, (*chip, c), sibling))
        return mine, first, arrived, passed, rest

    def start(ins, outs, sems):
        mine, first, _, _, _ = plan(ins, outs, sems)
        for cp in mine + first:
            cp.start()

    def mid(ins, outs, sems):
        _, _, arrived, passed, _ = plan(ins, outs, sems)
        for got, fwd in zip(arrived, passed):
            got.wait_recv()
            fwd.start()

    def finish(ins, outs, sems):
        mine, first, _, passed, rest = plan(ins, outs, sems)
        for cp in rest:
            cp.wait_recv()
        for cp in first + passed:
            cp.wait_send()
        for cp in mine:
            cp.wait()

    out_shape = [jax.ShapeDtypeStruct((a.shape[0], N_DEV * a.shape[1]) if kd == "col" else (N_DEV * a.shape[0], a.shape[1]),
                                      a.dtype) for a, kd in parts]
    scratch = [pltpu.SemaphoreType.DMA((7 * n,)), pltpu.SemaphoreType.DMA((7 * n,)), pltpu.SemaphoreType.DMA((n,))]
    return _Exchange([a for a, _ in parts], out_shape, scratch, start, finish, mid)


def _all_gather(parts, name):
    return _run_exchange(_gather_exchange(parts), name)


def _shard_shape(a, kd):
    return (a.shape[0], a.shape[1] // N_DEV) if kd == "col" else (a.shape[0] // N_DEV, a.shape[1])


def _symmetric_exchange(inputs, out_shape, n_copies, plan):
    def start(ins, outs, sems):
        for cp in plan(ins, outs, sems):
            cp.start()

    def finish(ins, outs, sems):
        copies = plan(ins, outs, sems)
        for cp in copies:
            cp.wait_recv()
        for cp in copies:
            cp.wait_send()

    scratch = [pltpu.SemaphoreType.DMA((n_copies,)), pltpu.SemaphoreType.DMA((n_copies,))]
    return _Exchange(inputs, out_shape, scratch, start, finish)


def _rs_pair_exchange(grads):
    n = len(grads)
    kinds = [kd for _, kd in grads]
    shapes = [_shard_shape(a, kd) for a, kd in grads]

    def plan(ins, outs, sems):
        send_sems, recv_sems = sems
        x, y, c = _coords()
        copies = []
        for p in range(n):
            size = shapes[p][1] if kinds[p] == "col" else shapes[p][0]
            for s in range(4):
                src = _window(ins[p], kinds[p], 2 * s + (1 - c), size)
                copies.append(pltpu.make_async_remote_copy(
                    src_ref=src, dst_ref=outs[p].at[s], send_sem=send_sems.at[4 * p + s], recv_sem=recv_sems.at[4 * p + s],
                    device_id=(x, y, 1 - c), device_id_type=MESH_ID))
        return copies

    return _symmetric_exchange([a for a, _ in grads], [jax.ShapeDtypeStruct((4,) + shapes[p], BF16) for p in range(n)],
                               4 * n, plan)


def _rs_chips_exchange(sums):
    n = len(sums)

    def plan(ins, outs, sems):
        send_sems, recv_sems = sems
        x, y, c = _coords()
        chips = [(1 - x, y), (x, 1 - y), (1 - x, 1 - y)]
        copies = []
        for p in range(n):
            for q, (cx, cy) in enumerate(chips):
                copies.append(pltpu.make_async_remote_copy(
                    src_ref=ins[p].at[2 * cx + cy], dst_ref=outs[p].at[q],
                    send_sem=send_sems.at[3 * p + q], recv_sem=recv_sems.at[3 * p + q],
                    device_id=(cx, cy, c), device_id_type=MESH_ID))
        return copies

    return _symmetric_exchange(list(sums), [jax.ShapeDtypeStruct((3,) + a.shape[1:], BF16) for a in sums], 3 * n, plan)


def _rs_pair(grads, name):
    return _run_exchange(_rs_pair_exchange(grads), name)


def _rs_chips(sums, name):
    return _run_exchange(_rs_chips_exchange(sums), name)


def _pair_sum(grad, kd, recv, core, name):
    _, r, cw = recv.shape
    tr = min(r, TM)

    def body(core_ref, g_ref, r_ref, o_ref):
        del core_ref
        o_ref[0] = (g_ref[...].astype(F32) + r_ref[0].astype(F32)).astype(BF16)

    if kd == "col":
        g_spec = pl.BlockSpec((tr, cw), lambda s, i, core_ref: (i, 2 * s + core_ref[0]))
    else:
        g_spec = pl.BlockSpec((tr, cw), lambda s, i, core_ref: ((2 * s + core_ref[0]) * (r // tr) + i, 0))
    grid_spec = pltpu.PrefetchScalarGridSpec(
        num_scalar_prefetch=1, grid=(4, r // tr),
        in_specs=[g_spec, pl.BlockSpec((1, tr, cw), lambda s, i, core_ref: (s, i, 0))],
        out_specs=pl.BlockSpec((1, tr, cw), lambda s, i, core_ref: (s, i, 0)))
    return pl.pallas_call(
        body, name=name, grid_spec=grid_spec, out_shape=jax.ShapeDtypeStruct(recv.shape, BF16),
        compiler_params=_cparams(2),
    )(core, grad, recv)


def _adam_math(w, g, m, v):
    m2 = B1 * m + (1.0 - B1) * g
    v2 = B2 * v + (1.0 - B2) * (g * g)
    m_hat = m2 / (1.0 - B1 ** STEP)
    v_hat = v2 / (1.0 - B2 ** STEP)
    delta = -LR * (m_hat / (jnp.sqrt(v_hat) + ADAM_EPS) + WD * w)
    return delta, m2, v2


def _chip_sum_adam(items, chip, tr, name, ride=None):
    r = items[0][1].shape[0]
    steps = r // tr
    n_parts = [len(parts) for parts, _, _, _ in items]
    r_in = 0 if ride is None else len(ride.inputs)
    r_out = 0 if ride is None else len(ride.out_shape)
    n_in = sum(2 * k + 3 for k in n_parts)
    n_out = 4 * len(items)

    def body(chip_ref, *refs):
        del chip_ref
        ins, refs = refs[:n_in], refs[n_in:]
        r_ins, refs = refs[:r_in], refs[r_in:]
        outs, refs = refs[:n_out], refs[n_out:]
        r_outs, sems = refs[:r_out], refs[r_out:]
        i = pl.program_id(0)
        if ride is not None:
            @pl.when(i == 0)
            def _():
                ride.start(r_ins, r_outs, sems)

        pos = 0
        for q, (k, (_, w, _, _)) in enumerate(zip(n_parts, items)):
            cols = []
            for _ in range(k):
                p_ref, c_ref = ins[pos], ins[pos + 1]
                pos += 2
                cols.append(p_ref[0].astype(F32) + c_ref[0].astype(F32) + c_ref[1].astype(F32) + c_ref[2].astype(F32))
            g = (cols[0] if k == 1 else jnp.concatenate(cols, axis=1))[:, :w.shape[1]]
            w_ref, m_ref, v_ref = ins[pos:pos + 3]
            pos += 3
            delta, m2, v2 = _adam_math(w_ref[...], g, m_ref[...], v_ref[...])
            outs[4 * q][...] = g
            outs[4 * q + 1][...] = delta
            outs[4 * q + 2][...] = m2
            outs[4 * q + 3][...] = v2

        if ride is not None:
            @pl.when(i == steps - 1)
            def _():
                ride.finish(r_ins, r_outs, sems)

    in_specs, args, out_specs, out_shape = [], [], [], []
    for parts, w, m, v in items:
        for psum, recv in parts:
            pc = psum.shape[2]
            in_specs += [pl.BlockSpec((1, tr, pc), lambda i, chip_ref: (chip_ref[0], i, 0)),
                         pl.BlockSpec((3, tr, pc), lambda i, chip_ref: (0, i, 0))]
            args += [psum, recv]
        loc = pl.BlockSpec((tr, w.shape[1]), lambda i, chip_ref: (i, 0))
        in_specs += [loc] * 3
        args += [w, m, v]
        out_specs += [loc] * 4
        out_shape += [jax.ShapeDtypeStruct(w.shape, F32)] * 4
    grid_spec = pltpu.PrefetchScalarGridSpec(
        num_scalar_prefetch=1, grid=(steps,), in_specs=in_specs + [ANY] * r_in, out_specs=out_specs + [ANY] * r_out,
        scratch_shapes=[] if ride is None else ride.scratch)
    res = pl.pallas_call(
        body, name=name, grid_spec=grid_spec, out_shape=out_shape + ([] if ride is None else ride.out_shape),
        compiler_params=_cparams(1),
    )(chip, *args, *([] if ride is None else ride.inputs))
    return res if ride is None else (res[:n_out], res[n_out:])


def _small_allreduce_adam(partials, params, moms, vels, plain, name, ride=None):
    n, n_plain = len(partials), len(plain)
    summed = list(partials) + list(plain)
    row0 = []
    rows = 0
    for a in summed:
        if a.shape[0] >= 8:
            rows = _pad_to(rows, 8)
        row0.append(rows)
        rows += a.shape[0]
    rows = _pad_to(rows, 8)
    width = max(a.shape[1] for a in summed)
    r_in = 0 if ride is None else len(ride.inputs)
    r_out = 0 if ride is None else len(ride.out_shape)
    n_out = 4 * n + n_plain

    def body(*refs):
        w_in, m_in, v_in = refs[0:n], refs[n:2 * n], refs[2 * n:3 * n]
        g_in, refs = refs[3 * n:4 * n + n_plain], refs[4 * n + n_plain:]
        r_ins, refs = refs[:r_in], refs[r_in:]
        outs, refs = refs[:n_out], refs[n_out:]
        r_outs, refs = refs[:r_out], refs[r_out:]
        pair, chips, send_sems, recv_sems = refs[:4]
        if ride is not None:
            ride.start(r_ins, r_outs, refs[4:])
        x, y, c = _coords()
        chip = 2 * x + y
        pair[c] = jnp.zeros((rows, width), F32)
        for p, a in enumerate(summed):
            r, cw = a.shape
            pair[c, row0[p]:row0[p] + r, 0:cw] = g_in[p][...]
        swap = pltpu.make_async_remote_copy(src_ref=pair.at[c], dst_ref=pair.at[c], send_sem=send_sems.at[0],
                                            recv_sem=recv_sems.at[0], device_id=(x, y, 1 - c), device_id_type=MESH_ID)
        swap.start()
        swap.wait_recv()
        swap.wait_send()
        chips[chip] = pair[0] + pair[1]
        copies = [pltpu.make_async_remote_copy(
            src_ref=chips.at[chip], dst_ref=chips.at[chip], send_sem=send_sems.at[1 + q], recv_sem=recv_sems.at[1 + q],
            device_id=(cx, cy, c), device_id_type=MESH_ID) for q, (cx, cy) in enumerate([(1 - x, y), (x, 1 - y), (1 - x, 1 - y)])]
        for cp in copies:
            cp.start()
        for cp in copies:
            cp.wait_recv()
        for cp in copies:
            cp.wait_send()
        for p, a in enumerate(summed):
            r, cw = a.shape
            g = chips[0, row0[p]:row0[p] + r, 0:cw]
            for q in range(1, 4):
                g = g + chips[q, row0[p]:row0[p] + r, 0:cw]
            if p >= n:
                outs[4 * n + p - n][...] = g
                continue
            delta, m2, v2 = _adam_math(w_in[p][...], g, m_in[p][...], v_in[p][...])
            outs[4 * p][...] = g
            outs[4 * p + 1][...] = delta
            outs[4 * p + 2][...] = m2
            outs[4 * p + 3][...] = v2
        if ride is not None:
            ride.finish(r_ins, r_outs, refs[4:])

    out_shape = []
    for a in partials:
        out_shape += [jax.ShapeDtypeStruct(a.shape, F32)] * 4
    out_shape += [jax.ShapeDtypeStruct(a.shape, F32) for a in plain]
    res = pl.pallas_call(
        body, name=name, in_specs=[VMEM_SPEC] * (4 * n + n_plain) + [ANY] * r_in,
        out_specs=[VMEM_SPEC] * n_out + [ANY] * r_out, out_shape=out_shape + ([] if ride is None else ride.out_shape),
        scratch_shapes=[pltpu.VMEM((2, rows, width), F32), pltpu.VMEM((4, rows, width), F32),
                        pltpu.SemaphoreType.DMA((4,)), pltpu.SemaphoreType.DMA((4,))] + ([] if ride is None else ride.scratch),
    )(*params, *moms, *vels, *partials, *plain, *([] if ride is None else ride.inputs))
    return res if ride is None else (res[:n_out], res[n_out:])


def _local_step(xf, tgt, nseq, seq, cols1_all, later, small_w, core=None, small_step=None):
    d = xf.shape[1]
    dist = core is not None
    n1, n2, gain, pool_w, pool_scale, n3, nf = small_w
    tf = 2 * cols1_all.shape[1] // N_DEV
    consts = _retention_constants()
    cos_t, sin_t = _rotary_tables(seq)
    wp_b = pool_w.astype(BF16)

    def pair_sums(grads, recv, names):
        return [_pair_sum(g, kd, r, core, "pair_sum_" + nm) for (g, kd), r, nm in zip(grads, recv, names)]

    def riding(host):
        return _gather_exchange(later[host])

    act1 = (xf, n1, (cols1_all, 0), (cols1_all, 1), "ffn1_act")
    if dist:
        (h1, b1, sil1, dsil1, s1), (d1_all, win_all, wout_all, gate2_all, up2_all) = _ffn_act(*act1,
                                                                                             ride=riding("ffn1_act"))
        x1, (d2_all,) = _ffn_down(s1, xf, d1_all, "ffn1_down", ride=riding("ffn1_down"))
    else:
        d1_all, win_all, wout_all, gate2_all, up2_all, d2_all = (later[k] for k in ("down1", "w_in", "w_out", "gate2",
                                                                                   "up2", "down2"))
        h1, b1, sil1, dsil1, s1 = _ffn_act(*act1)
        x1 = _ffn_down(s1, xf, d1_all, "ffn1_down")
    h2, qs, kr, vv, gg, uu = _mix_in(x1, n2, win_all, cos_t, sin_t, seq, "mix_in")
    fwd_mix = (qs, kr, vv, gg, uu, x1, consts, gain, wp_b, pool_scale, wout_all, nseq, seq, "mix_core_fwd")
    x2, mix, oo, pooled, states = _mix_core_fwd(*fwd_mix)
    if dist:
        (x3, h3, b3, sil3, dsil3), (cols2_t, d2_t, win_t, wout_t) = _ffn_fwd(
            x2, n3, (gate2_all, 0), (up2_all, 0), d2_all, "ffn2_fwd", ride=riding("ffn2_fwd"))
    else:
        x3, h3, b3, sil3, dsil3 = _ffn_fwd(x2, n3, (gate2_all, 0), (up2_all, 0), d2_all, "ffn2_fwd")
        cols1_t = _transpose(cols1_all, "transpose_cols1")
        cols2_t = jnp.concatenate([_transpose(gate2_all, "transpose_gate2"), _transpose(up2_all, "transpose_up2")], axis=1)
        d1_t = _transpose(d1_all, "transpose_down1")
        d2_t = _transpose(d2_all, "transpose_down2")
        win_t = _transpose(win_all, "transpose_w_in")
        wout_t = _transpose(wout_all, "transpose_w_out")
    dx3, dnf, loss_part, dx3b = _loss_head(x3, nf, tgt, "loss_head")
    out = {}

    if dist:
        (da3, db3, g_wd2), (cols1_t, d1_t) = _ffn_bwd_act(dx3b, b3, sil3, dsil3, d2_t, "ffn2_bwd_act",
                                                          ride=riding("ffn2_bwd_act"))
    else:
        da3, db3, g_wd2 = _ffn_bwd_act(dx3b, b3, sil3, dsil3, d2_t, "ffn2_bwd_act")
    names2 = ["ffn2_gate", "ffn2_up", "ffn2_down"]
    grads2 = [(_wgrad(da3, h3, 1.0, tf, d, "wgrad_gate2"), "row"), (_wgrad(db3, h3, 1.0, tf, d, "wgrad_up2"), "row"),
              (g_wd2, "row")]
    if dist:
        (dx2, dn3), recv2 = _ffn_bwd_in(da3, db3, dx3, x2, n3, cols2_t, 0, "ffn2_bwd_in", ride=_rs_pair_exchange(grads2))
        sums2 = pair_sums(grads2, recv2, names2)
        (dp, dx2b, dgain, dscale, dwp), crecv2 = _mix_core_bwd(
            dx2, qs, kr, vv, gg, oo, pooled, states, consts, gain, wp_b, pool_scale, wout_t, cos_t, sin_t, nseq, seq,
            "mix_core_bwd", ride=_rs_chips_exchange(sums2))
        out.update({nm: [(s, r)] for nm, s, r in zip(names2, sums2, crecv2)})
    else:
        dx2, dn3 = _ffn_bwd_in(da3, db3, dx3, x2, n3, cols2_t, 0, "ffn2_bwd_in")
        dp, dx2b, dgain, dscale, dwp = _mix_core_bwd(dx2, qs, kr, vv, gg, oo, pooled, states, consts, gain, wp_b,
                                                     pool_scale, wout_t, cos_t, sin_t, nseq, seq, "mix_core_bwd")
        out.update(dict(zip(names2, grads2)))

    names_m = ["w_in", "w_out"]
    grads_m = [(_wgrad(h2, dp, 1.0, d, d, "wgrad_in"), "col"), (_wgrad(mix, dx2b, 1.0, d, d, "wgrad_out"), "row")]
    if dist:
        (dx1, dn2, dx1b), recv_m = _mix_in_bwd(dp, dx2, x1, n2, win_t, "mix_in_bwd", ride=_rs_pair_exchange(grads_m))
        sums_m = pair_sums(grads_m, recv_m, names_m)
        (da1, db1, g_wd1), crecv_m = _ffn_bwd_act(dx1b, b1, sil1, dsil1, d1_t, "ffn1_bwd_act",
                                                  ride=_rs_chips_exchange(sums_m))
        out.update({nm: [(s, r)] for nm, s, r in zip(names_m, sums_m, crecv_m)})
    else:
        dx1, dn2, dx1b = _mix_in_bwd(dp, dx2, x1, n2, win_t, "mix_in_bwd")
        da1, db1, g_wd1 = _ffn_bwd_act(dx1b, b1, sil1, dsil1, d1_t, "ffn1_bwd_act")
        out.update(dict(zip(names_m, grads_m)))

    dx0, dn1 = _ffn_bwd_in(da1, db1, dx1, xf, n1, cols1_t, 0, "ffn1_bwd_in")
    small_parts = (dn1, dn2, dgain, dwp, dscale, dn3, dnf)
    g_down = (g_wd1, "row")
    if dist:
        half = d // 2
        both = lambda first, second: _join([_rs_chips_exchange(first), _rs_pair_exchange([second])])
        g_gate, recv_d = _wgrad(da1, h1, 1.0, tf, d, "wgrad_gate1", ride=_rs_pair_exchange([g_down]))
        g_gate = (g_gate, "row")
        sum_d = pair_sums([g_down], recv_d, ["ffn1_down"])
        g_lo, (crecv_d, recv_g) = _wgrad(db1, h1, 1.0, tf, half, "wgrad_up1_lo", ride=both(sum_d, g_gate), b_cols=(0, 1))
        g_lo = (g_lo, "row")
        sum_g = pair_sums([g_gate], [recv_g], ["ffn1_gate"])
        g_hi, (crecv_g, recv_lo) = _wgrad(db1, h1, 1.0, tf, half, "wgrad_up1_hi", ride=both(sum_g, g_lo), b_cols=(1, 1))
        g_hi = (g_hi, "row")
        sum_lo = pair_sums([g_lo], [recv_lo], ["ffn1_up_lo"])
        small_out, (crecv_lo, recv_hi) = small_step(small_parts, loss_part, both(sum_lo, g_hi))
        sum_hi = pair_sums([g_hi], [recv_hi], ["ffn1_up_hi"])
        out.update({"ffn1_gate": [(sum_g[0], crecv_g)], "ffn1_down": [(sum_d[0], crecv_d)],
                    "ffn1_up": [(sum_lo[0], crecv_lo), (sum_hi[0], None)]})
        return small_out[-1], dx0, out, small_out[:-1], _rs_chips_exchange(sum_hi)
    out.update({"ffn1_gate": (_wgrad(da1, h1, 1.0, tf, d, "wgrad_gate1"), "row"),
                "ffn1_up": (_wgrad(db1, h1, 1.0, tf, d, "wgrad_up1"), "row"), "ffn1_down": g_down})
    return loss_part, dx0, out, small_parts


def kernel(x, norm_ffn1, ffn1_gate, ffn1_up, ffn1_down, norm_mix, w_in, ret_gn_gain, pool_w, pool_scale, w_out, norm_ffn2, ffn2_gate, ffn2_up, ffn2_down, norm_final, loss_target, m_norm_ffn1, m_ffn1_gate, m_ffn1_up, m_ffn1_down, m_norm_mix, m_w_in, m_ret_gn_gain, m_pool_w, m_pool_scale, m_w_out, m_norm_ffn2, m_ffn2_gate, m_ffn2_up, m_ffn2_down, m_norm_final, v_norm_ffn1, v_ffn1_gate, v_ffn1_up, v_ffn1_down, v_norm_mix, v_w_in, v_ret_gn_gain, v_pool_w, v_pool_scale, v_w_out, v_norm_ffn2, v_ffn2_gate, v_ffn2_up, v_ffn2_down, v_norm_final):
    nseq, seq, d = x.shape
    t = nseq * seq
    f_loc = ffn1_gate.shape[2]
    f_pad = _pad_to(f_loc, LANE)
    xf = x.reshape(t, d)
    tgt = loss_target.reshape(t, d)
    core = lax.axis_index("c").astype(jnp.int32).reshape(1)
    chip = (2 * lax.axis_index("x") + lax.axis_index("y")).astype(jnp.int32).reshape(1)

    colp = lambda w: jnp.pad(w[0].astype(BF16), ((0, 0), (0, f_pad - f_loc)))
    rowp = lambda w: jnp.pad(w[0].astype(BF16), ((0, f_pad - f_loc), (0, 0)))
    gate2, up2 = colp(ffn2_gate), colp(ffn2_up)
    cols1 = jnp.concatenate([colp(ffn1_gate), colp(ffn1_up)], axis=0)
    cols2_t = jnp.concatenate([gate2.T, up2.T], axis=1)
    (cols1_all,) = _all_gather([(cols1, "col")], "all_gather_ffn1")
    d1_loc, d2_loc, win_loc, wout_loc = rowp(ffn1_down), rowp(ffn2_down), w_in[0].astype(BF16), w_out[0].astype(BF16)
    later = {"ffn1_act": [(d1_loc, "row"), (win_loc, "col"), (wout_loc, "row"), (gate2, "col"), (up2, "col")],
             "ffn1_down": [(d2_loc, "row")],
             "ffn2_fwd": [(cols2_t, "row"), (d2_loc.T, "col"), (win_loc.T, "row"), (wout_loc.T, "col")],
             "ffn2_bwd_act": [(cols1.T, "row"), (d1_loc.T, "col")]}

    flat = lambda a: a.reshape(pool_w.size // d, d)
    params = [norm_ffn1, norm_mix, ret_gn_gain, flat(pool_w), pool_scale, norm_ffn2, norm_final.reshape(1, d)]
    moms = [m_norm_ffn1, m_norm_mix, m_ret_gn_gain, flat(m_pool_w), m_pool_scale, m_norm_ffn2, m_norm_final.reshape(1, d)]
    vels = [v_norm_ffn1, v_norm_mix, v_ret_gn_gain, flat(v_pool_w), v_pool_scale, v_norm_ffn2, v_norm_final.reshape(1, d)]

    def small_step(parts, loss_part, ride):
        dn1, dn2, dgain, dwp, dscale, dn3, dnf = parts
        return _small_allreduce_adam([dn1, dn2, dgain, flat(dwp), dscale, dn3, dnf], params, moms, vels, [loss_part],
                                     "small_allreduce_adam", ride)

    small_w = (norm_ffn1, norm_mix, ret_gn_gain, pool_w[0], pool_scale, norm_ffn2, norm_final.reshape(1, d))
    loss_sum, dx0, reduced, small_out, pending = _local_step(xf, tgt, nseq, seq, cols1_all, later, small_w, core,
                                                             small_step)

    local = {"ffn1_gate": (ffn1_gate, m_ffn1_gate, v_ffn1_gate), "ffn1_up": (ffn1_up, m_ffn1_up, v_ffn1_up),
             "ffn1_down": (ffn1_down, m_ffn1_down, v_ffn1_down), "w_in": (w_in, m_w_in, v_w_in),
             "w_out": (w_out, m_w_out, v_w_out), "ffn2_gate": (ffn2_gate, m_ffn2_gate, v_ffn2_gate),
             "ffn2_up": (ffn2_up, m_ffn2_up, v_ffn2_up), "ffn2_down": (ffn2_down, m_ffn2_down, v_ffn2_down)}
    flip = lambda nm: nm.endswith("gate") or nm.endswith("up")

    def item(nm):
        view = (lambda a: a[0].T) if flip(nm) else (lambda a: a[0])
        w, m, v = local[nm]
        return reduced[nm], view(w), view(m), view(v)

    big = {}

    def keep(names, res):
        for q, nm in enumerate(names):
            big[nm] = tuple((a.T if flip(nm) else a)[None] for a in res[4 * q:4 * q + 4])

    second = ["ffn2_gate", "ffn2_up", "ffn2_down"]
    res, (last_recv,) = _chip_sum_adam([item(nm) for nm in second], chip, item(second[0])[1].shape[0] // 2, "adam_ffn2",
                                       ride=pending)
    keep(second, res)
    reduced["ffn1_up"][-1] = (reduced["ffn1_up"][-1][0], last_recv)
    for nm in ["ffn1_gate", "ffn1_up", "ffn1_down", "w_in", "w_out"]:
        rows = item(nm)[1].shape[0]
        keep([nm], _chip_sum_adam([item(nm)], chip, min(rows, TM), "adam_" + nm))

    small_names = ["norm_ffn1", "norm_mix", "ret_gn_gain", "pool_w", "pool_scale", "norm_ffn2", "norm_final"]
    shapes = [norm_ffn1.shape, norm_mix.shape, ret_gn_gain.shape, pool_w.shape, pool_scale.shape, norm_ffn2.shape,
              norm_final.shape]
    small = {nm: tuple(small_out[4 * p + q].reshape(shapes[p]) for q in range(4)) for p, nm in enumerate(small_names)}

    loss = loss_sum[0, 0]
    order = ["norm_ffn1", "ffn1_gate", "ffn1_up", "ffn1_down", "norm_mix", "w_in", "ret_gn_gain", "pool_w", "pool_scale",
             "w_out", "norm_ffn2", "ffn2_gate", "ffn2_up", "ffn2_down", "norm_final"]
    both = {**big, **small}
    outs = [loss, dx0.reshape(nseq, seq, d)]
    for q in range(4):
        outs += [both[nm][q] for nm in order]
    return tuple(outs)
```

```python
import numpy as np
import jax
import jax.numpy as jnp
from jax import lax
from jax.experimental import pallas as pl
from jax.experimental.pallas import tpu as pltpu

F32, BF16 = jnp.float32, jnp.bfloat16
MESH_ID = pl.DeviceIdType.MESH
ANY = pl.BlockSpec(memory_space=pl.ANY)
VMEM_SPEC = pl.BlockSpec(memory_space=pltpu.VMEM)

N_DEV = 8
RMS_EPS = 1e-6
GN_EPS = 1e-5
HEADS, DK, DV = 4, 64, 128
QK_W, V_W, POOL_W = HEADS * DK, HEADS * DV, 512
WINDOWS = (2, 4, 8, 16)
GC = POOL_W // len(WINDOWS)
CHUNK = 64
BLK = 4 * CHUNK
HALO = 16
ROPE_BASE = 10000.0
LR, B1, B2, ADAM_EPS, WD, STEP = 0.001, 0.9, 0.999, 1e-08, 0.01, 10
LANE = 128
TM = 512
FFN_TM = 1024
FFN_FWD_TF = 512
WGRAD_TT = 4096
VMEM_LIMIT = 56 * 1024 * 1024


def _cparams(n_axes):
    return pltpu.CompilerParams(dimension_semantics=("arbitrary",) * n_axes, vmem_limit_bytes=VMEM_LIMIT)


class _Exchange:
    def __init__(self, inputs, out_shape, scratch, start, finish, mid=None):
        self.inputs, self.out_shape, self.scratch = list(inputs), list(out_shape), list(scratch)
        self.start, self.finish, self.mid = start, finish, mid


def _pallas(body, name, grid, in_specs, out_specs, out_shape, scratch_shapes, args, ride=None):
    n_axes = len(grid)
    if ride is None:
        return pl.pallas_call(body, name=name, grid=grid, in_specs=in_specs, out_specs=out_specs, out_shape=out_shape,
                              scratch_shapes=scratch_shapes, compiler_params=_cparams(n_axes))(*args)
    n_in, n_out, n_scr = len(in_specs), len(out_specs), len(scratch_shapes)
    r_in, r_out = len(ride.inputs), len(ride.out_shape)

    def hosted(*refs):
        ins, refs = refs[:n_in], refs[n_in:]
        r_ins, refs = refs[:r_in], refs[r_in:]
        outs, refs = refs[:n_out], refs[n_out:]
        r_outs, refs = refs[:r_out], refs[r_out:]
        scr, sems = refs[:n_scr], refs[n_scr:]
        ids = [pl.program_id(a) for a in range(n_axes)]
        first, last, inner0 = ids[0] == 0, ids[0] == grid[0] - 1, None
        for a in range(1, n_axes):
            first = first & (ids[a] == 0)
            last = last & (ids[a] == grid[a] - 1)
            inner0 = (ids[a] == 0) if inner0 is None else inner0 & (ids[a] == 0)

        @pl.when(first)
        def _():
            ride.start(r_ins, r_outs, sems)

        if ride.mid is not None:
            at_mid = ids[0] == grid[0] - 1
            if inner0 is not None:
                at_mid = at_mid & inner0

            @pl.when(at_mid)
            def _():
                ride.mid(r_ins, r_outs, sems)

        body(*ins, *outs, *scr)

        @pl.when(last)
        def _():
            ride.finish(r_ins, r_outs, sems)

    res = pl.pallas_call(
        hosted, name=name, grid=grid, in_specs=list(in_specs) + [ANY] * r_in, out_specs=list(out_specs) + [ANY] * r_out,
        out_shape=list(out_shape) + ride.out_shape, scratch_shapes=list(scratch_shapes) + ride.scratch,
        compiler_params=_cparams(n_axes))(*args, *ride.inputs)
    return res[:n_out], res[n_out:]


def _dot(a, b):
    return jnp.dot(a, b, preferred_element_type=F32)


def _dot_nt(a, b):
    return lax.dot_general(a, b, (((1,), (1,)), ((), ())), preferred_element_type=F32)


def _dot_tn(a, b):
    return lax.dot_general(a, b, (((0,), (0,)), ((), ())), preferred_element_type=F32)


def _sigmoid(x):
    return 0.5 * jnp.tanh(0.5 * x) + 0.5


def _pad_to(n, m):
    return (n + m - 1) // m * m


def _retention_constants():
    gamma = (1.0 - 2.0 ** (-5.0 - np.arange(HEADS, dtype=np.float32))).astype(np.float32)
    log_g = np.log(gamma).astype(np.float32)
    i = np.arange(BLK)
    diff = (i[:, None] - i[None, :]).astype(np.float32)
    same = (i[:, None] // CHUNK) == (i[None, :] // CHUNK)
    earlier = (i[None, :] // CHUNK) < (i[:, None] // CHUNK)
    mask = np.zeros((HEADS, BLK, BLK), np.float32)
    for h in range(HEADS):
        dec_abs = np.exp(log_g[h] * np.abs(diff)).astype(np.float32)
        dec = np.exp(log_g[h] * diff * earlier).astype(np.float32)
        mask[h] = np.where(same, dec_abs, np.where(earlier, dec, 0.0))
    dq = np.zeros((BLK, V_W), np.float32)
    dk = np.zeros((BLK, QK_W), np.float32)
    gbd = np.zeros((QK_W, V_W), np.float32)
    for h in range(HEADS):
        dq[:, h * DV:(h + 1) * DV] = np.exp(log_g[h] * (i + 1.0)).astype(np.float32)[:, None]
        dk[:, h * DK:(h + 1) * DK] = np.exp(log_g[h] * (BLK - 1.0 - i)).astype(np.float32)[:, None]
        gbd[h * DK:(h + 1) * DK, h * DV:(h + 1) * DV] = np.exp(log_g[h] * np.float32(BLK))
    bd = (gbd > 0).astype(np.float32)
    return jnp.asarray(mask), jnp.asarray(dq), jnp.asarray(dk), jnp.asarray(gbd), jnp.asarray(bd)


def _rotary_tables(seq):
    half = DK // 2
    freqs = ROPE_BASE ** (-jnp.arange(half, dtype=F32) * 2.0 / DK)
    ang = jnp.arange(seq, dtype=F32)[:, None] * freqs[None, :]
    cos, sin = jnp.cos(ang), jnp.sin(ang)
    cos_t = jnp.tile(jnp.concatenate([cos, cos], axis=1), (1, HEADS))
    sin_t = jnp.tile(jnp.concatenate([-sin, sin], axis=1), (1, HEADS))
    return cos_t, sin_t


def _swap_halves(x):
    lane = lax.broadcasted_iota(jnp.int32, (1, QK_W), 1)
    first = (lane & (DK - 1)) < DK // 2
    return jnp.where(first, pltpu.roll(x, QK_W - DK // 2, 1), pltpu.roll(x, DK // 2, 1))


def _head_mask(h):
    lane = lax.broadcasted_iota(jnp.int32, (1, QK_W), 1)
    return (lane >= h * DK) & (lane < (h + 1) * DK)


def _ffn_fwd(x, n, gate, up, wd, name, ride=None):
    t, d = x.shape
    (wg, gq), (wu, uq) = gate, up
    fp = wg.shape[1]
    tm = min(t, FFN_TM)
    tf = FFN_FWD_TF
    nj = fp // tf

    def body(x_ref, n_ref, wg_ref, wu_ref, wd_ref, xo_ref, h_ref, b_ref, sil_ref, dsil_ref, acc_ref):
        j = pl.program_id(1)

        @pl.when(j == 0)
        def _():
            xv = x_ref[...]
            r = lax.rsqrt(jnp.mean(xv * xv, axis=-1, keepdims=True) + RMS_EPS)
            h_ref[...] = (xv * r * n_ref[...]).astype(BF16)
            acc_ref[...] = jnp.zeros_like(acc_ref)

        h = h_ref[...]
        a = _dot(h, wg_ref[...])
        b = _dot(h, wu_ref[...])
        sg = _sigmoid(a)
        sil = a * sg
        b_ref[...] = b.astype(BF16)
        sil_ref[...] = sil.astype(BF16)
        dsil_ref[...] = (sg + sil * (1.0 - sg)).astype(BF16)
        acc_ref[...] += _dot((sil * b).astype(BF16), wd_ref[...])

        @pl.when(j == nj - 1)
        def _():
            xo_ref[...] = x_ref[...] + 0.5 * acc_ref[...]

    act = pl.BlockSpec((tm, tf), lambda i, j: (i, j))
    return _pallas(
        body, name, (t // tm, nj),
        [pl.BlockSpec((tm, d), lambda i, j: (i, 0)), pl.BlockSpec((1, d), lambda i, j: (0, 0)),
         pl.BlockSpec((d, tf), lambda i, j: (gq, j)), pl.BlockSpec((d, tf), lambda i, j: (uq, j)),
         pl.BlockSpec((tf, d), lambda i, j: (j, 0))],
        [pl.BlockSpec((tm, d), lambda i, j: (i, 0)), pl.BlockSpec((tm, d), lambda i, j: (i, 0)), act, act, act],
        [jax.ShapeDtypeStruct((t, d), F32), jax.ShapeDtypeStruct((t, d), BF16)] + [jax.ShapeDtypeStruct((t, fp), BF16)] * 3,
        [pltpu.VMEM((tm, d), F32)], (x, n, wg, wu, wd), ride)


def _ffn_act(x, n, gate, up, name, ride=None):
    t, d = x.shape
    (wg, gq), (wu, uq) = gate, up
    fp = wg.shape[1]
    tm = min(t, FFN_TM)
    tf = 2 * fp // N_DEV
    nj = fp // tf

    def body(x_ref, n_ref, wg_ref, wu_ref, h_ref, b_ref, sil_ref, dsil_ref, s_ref):
        @pl.when(pl.program_id(1) == 0)
        def _():
            xv = x_ref[...]
            r = lax.rsqrt(jnp.mean(xv * xv, axis=-1, keepdims=True) + RMS_EPS)
            h_ref[...] = (xv * r * n_ref[...]).astype(BF16)

        h = h_ref[...]
        a = _dot(h, wg_ref[...])
        b = _dot(h, wu_ref[...])
        sg = _sigmoid(a)
        sil = a * sg
        b_ref[...] = b.astype(BF16)
        sil_ref[...] = sil.astype(BF16)
        dsil_ref[...] = (sg + sil * (1.0 - sg)).astype(BF16)
        s_ref[...] = (sil * b).astype(BF16)

    act = pl.BlockSpec((tm, tf), lambda i, j: (i, j))
    return _pallas(
        body, name, (t // tm, nj),
        [pl.BlockSpec((tm, d), lambda i, j: (i, 0)), pl.BlockSpec((1, d), lambda i, j: (0, 0)),
         pl.BlockSpec((d, tf), lambda i, j: (gq, j)), pl.BlockSpec((d, tf), lambda i, j: (uq, j))],
        [pl.BlockSpec((tm, d), lambda i, j: (i, 0)), act, act, act, act],
        [jax.ShapeDtypeStruct((t, d), BF16)] + [jax.ShapeDtypeStruct((t, fp), BF16)] * 4,
        [], (x, n, wg, wu), ride)


def _ffn_down(s, x, wd, name, ride=None):
    t, d = x.shape
    fp = wd.shape[0]
    tm = min(t, FFN_TM)

    def body(s_ref, x_ref, wd_ref, xo_ref):
        xo_ref[...] = x_ref[...] + 0.5 * _dot(s_ref[...], wd_ref[...])

    row = pl.BlockSpec((tm, d), lambda i: (i, 0))
    res = _pallas(body, name, (t // tm,), [pl.BlockSpec((tm, fp), lambda i: (i, 0)), row, pl.BlockSpec((fp, d), lambda i: (0, 0))],
                  [row], [jax.ShapeDtypeStruct((t, d), F32)], [], (s, x, wd), ride)
    return res[0] if ride is None else (res[0][0], res[1])


def _ffn_bwd_act(dxob, b, sil, dsil, wd_t, name, ride=None):
    t, d = dxob.shape
    fp = wd_t.shape[1]
    tm = min(t, FFN_TM)
    tf = 2 * fp // N_DEV
    ni = t // tm

    def body(dx_ref, b_ref, sil_ref, dsil_ref, wd_ref, da_ref, db_ref, gd_ref, acc_ref):
        i = pl.program_id(1)

        @pl.when(i == 0)
        def _():
            acc_ref[...] = jnp.zeros_like(acc_ref)

        dxv = dx_ref[...]
        bv, sv = b_ref[...].astype(F32), sil_ref[...].astype(F32)
        ds = _dot(dxv, wd_ref[...])
        da_ref[...] = (ds * bv * dsil_ref[...].astype(F32)).astype(BF16)
        db_ref[...] = (ds * sv).astype(BF16)
        acc_ref[...] += _dot_tn((sv * bv).astype(BF16), dxv)

        @pl.when(i == ni - 1)
        def _():
            gd_ref[...] = acc_ref[...].astype(BF16)

    act = pl.BlockSpec((tm, tf), lambda c, i: (i, c))
    return _pallas(
        body, name, (fp // tf, ni),
        [pl.BlockSpec((tm, d), lambda c, i: (i, 0)), act, act, act, pl.BlockSpec((d, tf), lambda c, i: (0, c))],
        [act, act, pl.BlockSpec((tf, d), lambda c, i: (c, 0))],
        [jax.ShapeDtypeStruct((t, fp), BF16), jax.ShapeDtypeStruct((t, fp), BF16), jax.ShapeDtypeStruct((fp, d), BF16)],
        [pltpu.VMEM((tf, d), F32)], (dxob, b, sil, dsil, wd_t), ride)


def _ffn_bwd_in(da, db, dxo, x, n, cols_t, gq, name, ride=None):
    t, d = x.shape
    fp = cols_t.shape[0]
    tm = min(t, FFN_TM)
    tf = 2 * fp // N_DEV
    nj = fp // tf

    def body(da_ref, db_ref, dxo_ref, x_ref, n_ref, wg_ref, wu_ref, dx_ref, dn_ref, acc_ref):
        i, j = pl.program_id(0), pl.program_id(1)

        @pl.when((i == 0) & (j == 0))
        def _():
            dn_ref[...] = jnp.zeros_like(dn_ref)

        @pl.when(j == 0)
        def _():
            acc_ref[...] = jnp.zeros_like(acc_ref)

        acc_ref[...] += _dot(da_ref[...], wg_ref[...]) + _dot(db_ref[...], wu_ref[...])

        @pl.when(j == nj - 1)
        def _():
            xv = x_ref[...]
            r = lax.rsqrt(jnp.mean(xv * xv, axis=-1, keepdims=True) + RMS_EPS)
            xh = xv * r
            dh = acc_ref[...]
            dn_ref[...] += jnp.sum(dh * xh, axis=0, keepdims=True)
            dhn = dh * n_ref[...]
            dx_ref[...] = dxo_ref[...] + r * (dhn - xh * jnp.mean(dhn * xh, axis=-1, keepdims=True))

    act = pl.BlockSpec((tm, tf), lambda i, j: (i, j))
    row = pl.BlockSpec((tm, d), lambda i, j: (i, 0))
    return _pallas(
        body, name, (t // tm, nj),
        [act, act, row, row, pl.BlockSpec((1, d), lambda i, j: (0, 0)),
         pl.BlockSpec((tf, d), lambda i, j: (j, gq)), pl.BlockSpec((tf, d), lambda i, j: (j, gq + 1))],
        [row, pl.BlockSpec((1, d), lambda i, j: (0, 0))],
        [jax.ShapeDtypeStruct((t, d), F32), jax.ShapeDtypeStruct((1, d), F32)],
        [pltpu.VMEM((tm, d), F32)], (da, db, dxo, x, n, cols_t, cols_t), ride)


def _wgrad(a, b, scale, tk, tn, name, ride=None, b_cols=None):
    t, k = a.shape
    q0, nq = (0, b.shape[1] // tn) if b_cols is None else b_cols
    n = nq * tn
    tt = min(t, WGRAD_TT)
    nt = t // tt

    def body(a_ref, b_ref, o_ref, acc_ref):
        s = pl.program_id(2)

        @pl.when(s == 0)
        def _():
            acc_ref[...] = jnp.zeros_like(acc_ref)

        acc_ref[...] += _dot_tn(a_ref[...], b_ref[...])

        @pl.when(s == nt - 1)
        def _():
            o_ref[...] = (scale * acc_ref[...]).astype(BF16)

    res = _pallas(
        body, name, (k // tk, n // tn, nt),
        [pl.BlockSpec((tt, tk), lambda p, q, s: (s, p)), pl.BlockSpec((tt, tn), lambda p, q, s: (s, q + q0))],
        [pl.BlockSpec((tk, tn), lambda p, q, s: (p, q))], [jax.ShapeDtypeStruct((k, n), BF16)],
        [pltpu.VMEM((tk, tn), F32)], (a, b), ride)
    return res[0] if ride is None else (res[0][0], res[1])


def _mix_in(x, n, w_in, cos_t, sin_t, seq, name):
    t, d = x.shape
    per_seq = seq // TM

    def body(x_ref, n_ref, w_ref, c_ref, s_ref, h_ref, q_ref, k_ref, v_ref, g_ref, u_ref):
        xv = x_ref[...]
        r = lax.rsqrt(jnp.mean(xv * xv, axis=-1, keepdims=True) + RMS_EPS)
        h = (xv * r * n_ref[...]).astype(BF16)
        h_ref[...] = h
        p = _dot(h, w_ref[...])
        c, s = c_ref[...], s_ref[...]
        q = p[:, :QK_W]
        k = p[:, QK_W:2 * QK_W]
        q_ref[...] = ((q * c + _swap_halves(q) * s) * (DK ** -0.5)).astype(BF16)
        k_ref[...] = (k * c + _swap_halves(k) * s).astype(BF16)
        v_ref[...] = p[:, 2 * QK_W:2 * QK_W + V_W].astype(BF16)
        g_ref[...] = p[:, 2 * QK_W + V_W:2 * QK_W + 2 * V_W]
        u_ref[...] = p[:, 2 * QK_W + 2 * V_W:]

    tile = lambda w: pl.BlockSpec((TM, w), lambda i: (i, 0))
    return pl.pallas_call(
        body, name=name, grid=(t // TM,),
        in_specs=[tile(d), pl.BlockSpec((1, d), lambda i: (0, 0)), pl.BlockSpec(w_in.shape, lambda i: (0, 0)),
                  pl.BlockSpec((TM, QK_W), lambda i: (i % per_seq, 0)), pl.BlockSpec((TM, QK_W), lambda i: (i % per_seq, 0))],
        out_specs=[tile(d), tile(QK_W), tile(QK_W), tile(V_W), tile(V_W), tile(POOL_W)],
        out_shape=[jax.ShapeDtypeStruct((t, d), BF16), jax.ShapeDtypeStruct((t, QK_W), BF16),
                   jax.ShapeDtypeStruct((t, QK_W), BF16), jax.ShapeDtypeStruct((t, V_W), BF16),
                   jax.ShapeDtypeStruct((t, V_W), F32), jax.ShapeDtypeStruct((t, POOL_W), F32)],
        compiler_params=_cparams(1),
    )(x, n, w_in, cos_t, sin_t)


def _mix_in_bwd(dp, dx2, x1, n, w_in_t, name, ride=None):
    t, d = x1.shape

    def body(dp_ref, dx2_ref, x_ref, n_ref, w_ref, dx_ref, dn_ref, dxb_ref):
        @pl.when(pl.program_id(0) == 0)
        def _():
            dn_ref[...] = jnp.zeros_like(dn_ref)

        dh = _dot(dp_ref[...], w_ref[...])
        xv = x_ref[...]
        r = lax.rsqrt(jnp.mean(xv * xv, axis=-1, keepdims=True) + RMS_EPS)
        xh = xv * r
        dn_ref[...] += jnp.sum(dh * xh, axis=0, keepdims=True)
        dhn = dh * n_ref[...]
        dx = dx2_ref[...] + r * (dhn - xh * jnp.mean(dhn * xh, axis=-1, keepdims=True))
        dx_ref[...] = dx
        dxb_ref[...] = (0.5 * dx).astype(BF16)

    tile = lambda w: pl.BlockSpec((TM, w), lambda i: (i, 0))
    return _pallas(
        body, name, (t // TM,),
        [tile(dp.shape[1]), tile(d), tile(d), pl.BlockSpec((1, d), lambda i: (0, 0)),
         pl.BlockSpec(w_in_t.shape, lambda i: (0, 0))],
        [tile(d), pl.BlockSpec((1, d), lambda i: (0, 0)), tile(d)],
        [jax.ShapeDtypeStruct((t, d), F32), jax.ShapeDtypeStruct((1, d), F32), jax.ShapeDtypeStruct((t, d), BF16)],
        [], (dp, dx2, x1, n, w_in_t), ride)


def _group_norm(o):
    parts, rstds = [], []
    for h in range(HEADS):
        oh = o[:, h * DV:(h + 1) * DV]
        dlt = oh - jnp.mean(oh, axis=-1, keepdims=True)
        rstd = lax.rsqrt(jnp.mean(dlt * dlt, axis=-1, keepdims=True) + GN_EPS)
        parts.append(dlt * rstd)
        rstds.append(rstd)
    return jnp.concatenate(parts, axis=1), rstds


def _mix_core_fwd(qs, k, v, g, u, x1, consts, gain, wp, scale, w_out, nseq, seq, name):
    t, d = x1.shape
    nblk = seq // BLK
    mask, dq, dk, gbd, bd = consts

    def body(q_ref, k_ref, v_ref, g_ref, u_ref, x1_ref, m_ref, dq_ref, dk_ref, gbd_ref, bd_ref, gain_ref, wp_ref,
             sc_ref, wo_ref, x2_ref, mix_ref, o_ref, pooled_ref, st_ref, state, halo):
        j = pl.program_id(1)

        @pl.when(j == 0)
        def _():
            state[...] = jnp.zeros_like(state)
            halo[...] = jnp.zeros_like(halo)

        qv, kv, vv = q_ref[...], k_ref[...], v_ref[...]
        st = state[...]
        st_ref[0] = st
        cross = _dot(qv, st.astype(BF16)) * dq_ref[...]
        outs = []
        for h in range(HEADS):
            qh = jnp.where(_head_mask(h), qv, jnp.zeros_like(qv))
            am = (_dot_nt(qh, kv) * m_ref[h]).astype(BF16)
            outs.append(_dot(am, vv[:, h * DV:(h + 1) * DV]))
        o = jnp.concatenate(outs, axis=1) + cross
        o_ref[...] = o
        kd = (kv.astype(F32) * dk_ref[...]).astype(BF16)
        state[...] = gbd_ref[...] * st + _dot_tn(kd, vv) * bd_ref[...]

        gv = g_ref[...]
        nrm, _ = _group_norm(o)
        ret = (gv * _sigmoid(gv)) * (nrm * gain_ref[...])

        uv = u_ref[...]
        c = jnp.concatenate([halo[...], uv], axis=0)
        halo[...] = uv[BLK - HALO:, :]
        pos = j * BLK + lax.broadcasted_iota(jnp.int32, (BLK, 1), 0)
        parts = []
        for gi, w in enumerate(WINDOWS):
            c = c + pltpu.roll(c, w // 2, 0)
            cnt = jnp.minimum(pos + 1, w).astype(F32)
            parts.append(c[HALO:, :GC] / cnt)
            if gi + 1 < len(WINDOWS):
                c = c[:, GC:]
        pooled = (jnp.concatenate(parts, axis=1) - uv).astype(BF16)
        pooled_ref[...] = pooled
        z = jnp.concatenate([_dot(pooled[:, gi * GC:(gi + 1) * GC], wp_ref[gi]) for gi in range(len(WINDOWS))], axis=1)
        mix = jnp.concatenate([ret, z * sc_ref[...]], axis=1).astype(BF16)
        mix_ref[...] = mix
        x2_ref[...] = x1_ref[...] + _dot(mix, wo_ref[...])

    blk = lambda w: pl.BlockSpec((BLK, w), lambda i, j: (i * nblk + j, 0))
    full = lambda a: pl.BlockSpec(a.shape, lambda i, j: (0,) * a.ndim)
    return _pallas(
        body, name, (nseq, nblk),
        [blk(QK_W), blk(QK_W), blk(V_W), blk(V_W), blk(POOL_W), blk(d),
         full(mask), full(dq), full(dk), full(gbd), full(bd), full(gain), full(wp), full(scale), full(w_out)],
        [blk(d), blk(d), blk(V_W), blk(POOL_W), pl.BlockSpec((1, QK_W, V_W), lambda i, j: (i * nblk + j, 0, 0))],
        [jax.ShapeDtypeStruct((t, d), F32), jax.ShapeDtypeStruct((t, d), BF16),
         jax.ShapeDtypeStruct((t, V_W), F32), jax.ShapeDtypeStruct((t, POOL_W), BF16),
         jax.ShapeDtypeStruct((nseq * nblk, QK_W, V_W), F32)],
        [pltpu.VMEM((QK_W, V_W), F32), pltpu.VMEM((HALO, POOL_W), F32)],
        (qs, k, v, g, u, x1, mask, dq, dk, gbd, bd, gain, wp, scale, w_out))


def _mix_core_bwd(dx2, qs, k, v, g, o, pooled, st, consts, gain, wp, scale, w_out, cos_t, sin_t, nseq, seq, name,
                  ride=None):
    t, d = dx2.shape
    nblk = seq // BLK
    mask, dq, dk, gbd, bd = consts
    n_win = len(WINDOWS)

    def body(dx2_ref, q_ref, k_ref, v_ref, g_ref, o_ref, pooled_ref, st_ref, m_ref, dq_ref, dk_ref, gbd_ref, bd_ref,
             gain_ref, wp_ref, sc_ref, wo_ref, c_ref, s_ref,
             dp_ref, dx2b_ref, dgain_ref, dscale_ref, dwp_ref, rstate, carry):
        i, j = pl.program_id(0), pl.program_id(1)

        @pl.when((i == 0) & (j == 0))
        def _():
            dgain_ref[...] = jnp.zeros_like(dgain_ref)
            dscale_ref[...] = jnp.zeros_like(dscale_ref)
            dwp_ref[...] = jnp.zeros_like(dwp_ref)

        @pl.when(j == 0)
        def _():
            rstate[...] = jnp.zeros_like(rstate)
            carry[...] = jnp.zeros_like(carry)

        dx2b = dx2_ref[...].astype(BF16)
        dx2b_ref[...] = dx2b
        dmix = _dot(dx2b, wo_ref[...])
        dret, dpool = dmix[:, :V_W], dmix[:, V_W:]

        gv, ov, gain_v = g_ref[...], o_ref[...], gain_ref[...]
        sg = _sigmoid(gv)
        sil = gv * sg
        nrm, rstds = _group_norm(ov)
        dg = dret * (nrm * gain_v) * (sg * (1.0 + gv * (1.0 - sg)))
        dgn = dret * sil
        dgain_ref[...] += jnp.sum(dgn * nrm, axis=0, keepdims=True)
        dnrm = dgn * gain_v
        do_parts = []
        for h in range(HEADS):
            dn_h = dnrm[:, h * DV:(h + 1) * DV]
            n_h = nrm[:, h * DV:(h + 1) * DV]
            do_parts.append(rstds[h] * (dn_h - jnp.mean(dn_h, axis=-1, keepdims=True)
                                        - n_h * jnp.mean(dn_h * n_h, axis=-1, keepdims=True)))
        do = jnp.concatenate(do_parts, axis=1)
        dob = do.astype(BF16)

        qv, kv, vv = q_ref[...], k_ref[...], v_ref[...]
        stb = st_ref[0].astype(BF16)
        rs = rstate[...]
        rsb = rs.astype(BF16)
        dod = (do * dq_ref[...]).astype(BF16)
        dqs = _dot_nt(dod, stb)
        dst = _dot_tn(qv, dod) * bd_ref[...]
        dkf = dk_ref[...]
        kd = (kv.astype(F32) * dkf).astype(BF16)
        dks = _dot_nt(vv, rsb) * dkf
        dvs = _dot(kd, rsb)
        dv_parts = []
        for h in range(HEADS):
            hm = _head_mask(h)
            qh = jnp.where(hm, qv, jnp.zeros_like(qv))
            mh = m_ref[h]
            am = (_dot_nt(qh, kv) * mh).astype(BF16)
            dpm = (_dot_nt(dob[:, h * DV:(h + 1) * DV], vv[:, h * DV:(h + 1) * DV]) * mh).astype(BF16)
            dqs = dqs + jnp.where(hm, _dot(dpm, kv), 0.0)
            dks = dks + jnp.where(hm, _dot_tn(dpm, qv), 0.0)
            dv_parts.append(_dot_tn(am, dob[:, h * DV:(h + 1) * DV]))
        dvs = dvs + jnp.concatenate(dv_parts, axis=1)
        rstate[...] = dst + gbd_ref[...] * rs

        cv, sv = c_ref[...], s_ref[...]
        dqr = dqs * (DK ** -0.5)
        dq_pre = dqr * cv + _swap_halves(dqr * sv)
        dk_pre = dks * cv + _swap_halves(dks * sv)

        pv = pooled_ref[...]
        sc = sc_ref[...]
        dzb = (dpool * sc).astype(BF16)
        z_parts, dpo_parts = [], []
        for gi in range(n_win):
            p_g = pv[:, gi * GC:(gi + 1) * GC]
            dz_g = dzb[:, gi * GC:(gi + 1) * GC]
            z_parts.append(_dot(p_g, wp_ref[gi]))
            dwp_ref[gi] += _dot_tn(p_g, dz_g)
            dpo_parts.append(_dot_nt(dz_g, wp_ref[gi]))
        dscale_ref[...] += jnp.sum(dpool * jnp.concatenate(z_parts, axis=1), axis=0, keepdims=True)
        dpo = jnp.concatenate(dpo_parts, axis=1)
        pos = (nblk - 1 - j) * BLK + lax.broadcasted_iota(jnp.int32, (BLK, 1), 0)
        e = jnp.concatenate(
            [dpo[:, gi * GC:(gi + 1) * GC] / jnp.minimum(pos + 1, w).astype(F32) for gi, w in enumerate(WINDOWS)], axis=1)
        c = jnp.concatenate([e, carry[...]], axis=0)
        carry[...] = e[:HALO, :]
        rows = BLK + HALO
        lead = []
        for gi, w in enumerate(WINDOWS):
            c = c + pltpu.roll(c, rows - w // 2, 0)
            lead.append(c[:BLK, :GC])
            if gi + 1 < n_win:
                c = c[:, GC:]
        du = jnp.concatenate(lead, axis=1) - dpo

        dp_ref[:, 0:QK_W] = dq_pre.astype(BF16)
        dp_ref[:, QK_W:2 * QK_W] = dk_pre.astype(BF16)
        dp_ref[:, 2 * QK_W:2 * QK_W + V_W] = dvs.astype(BF16)
        dp_ref[:, 2 * QK_W + V_W:2 * QK_W + 2 * V_W] = dg.astype(BF16)
        dp_ref[:, 2 * QK_W + 2 * V_W:] = du.astype(BF16)

    rev = lambda i, j: i * nblk + (nblk - 1 - j)
    blk = lambda w: pl.BlockSpec((BLK, w), lambda i, j: (rev(i, j), 0))
    full = lambda a: pl.BlockSpec(a.shape, lambda i, j: (0,) * a.ndim)
    in_w = 2 * QK_W + 2 * V_W + POOL_W
    return _pallas(
        body, name, (nseq, nblk),
        [blk(d), blk(QK_W), blk(QK_W), blk(V_W), blk(V_W), blk(V_W), blk(POOL_W),
         pl.BlockSpec((1, QK_W, V_W), lambda i, j: (rev(i, j), 0, 0)),
         full(mask), full(dq), full(dk), full(gbd), full(bd), full(gain), full(wp), full(scale), full(w_out),
         pl.BlockSpec((BLK, QK_W), lambda i, j: (nblk - 1 - j, 0)),
         pl.BlockSpec((BLK, QK_W), lambda i, j: (nblk - 1 - j, 0))],
        [blk(in_w), blk(d), pl.BlockSpec((1, V_W), lambda i, j: (0, 0)),
         pl.BlockSpec((1, POOL_W), lambda i, j: (0, 0)), pl.BlockSpec((n_win, GC, GC), lambda i, j: (0, 0, 0))],
        [jax.ShapeDtypeStruct((t, in_w), BF16), jax.ShapeDtypeStruct((t, d), BF16),
         jax.ShapeDtypeStruct((1, V_W), F32), jax.ShapeDtypeStruct((1, POOL_W), F32),
         jax.ShapeDtypeStruct((n_win, GC, GC), F32)],
        [pltpu.VMEM((QK_W, V_W), F32), pltpu.VMEM((HALO, POOL_W), F32)],
        (dx2, qs, k, v, g, o, pooled, st, mask, dq, dk, gbd, bd, gain, wp, scale, w_out, cos_t, sin_t), ride)


def _loss_head(x3, nf, tgt, name):
    t, d = x3.shape

    def body(x_ref, n_ref, t_ref, dx_ref, dn_ref, loss_ref, dxb_ref):
        @pl.when(pl.program_id(0) == 0)
        def _():
            dn_ref[...] = jnp.zeros_like(dn_ref)
            loss_ref[...] = jnp.zeros_like(loss_ref)

        xv = x_ref[...]
        nv = n_ref[...]
        r = lax.rsqrt(jnp.mean(xv * xv, axis=-1, keepdims=True) + RMS_EPS)
        xh = xv * r
        err = xh * nv - t_ref[...]
        row = jnp.mean(err * err, axis=-1, keepdims=True)
        loss_ref[...] += 0.5 * jnp.sum(row, axis=0, keepdims=True)
        dy = err * (1.0 / d)
        dn_ref[...] += jnp.sum(dy * xh, axis=0, keepdims=True)
        dxh = dy * nv
        dx = r * (dxh - xh * jnp.mean(dxh * xh, axis=-1, keepdims=True))
        dx_ref[...] = dx
        dxb_ref[...] = (0.5 * dx).astype(BF16)

    tile = pl.BlockSpec((TM, d), lambda i: (i, 0))
    return pl.pallas_call(
        body, name=name, grid=(t // TM,),
        in_specs=[tile, pl.BlockSpec((1, d), lambda i: (0, 0)), tile],
        out_specs=[tile, pl.BlockSpec((1, d), lambda i: (0, 0)), pl.BlockSpec((1, 1), lambda i: (0, 0)), tile],
        out_shape=[jax.ShapeDtypeStruct((t, d), F32), jax.ShapeDtypeStruct((1, d), F32), jax.ShapeDtypeStruct((1, 1), F32),
                   jax.ShapeDtypeStruct((t, d), BF16)],
        compiler_params=_cparams(1),
    )(x3, nf, tgt)


def _coords():
    return lax.axis_index("x"), lax.axis_index("y"), lax.axis_index("c")


def _window(ref, kind, idx, size):
    if kind == "col":
        return ref.at[:, pl.ds(pl.multiple_of(idx * size, LANE), size)]
    return ref.at[pl.ds(pl.multiple_of(idx * size, 8), size), :]


def _run_exchange(ex, name):
    n_in = len(ex.inputs)

    def body(*refs):
        ins, outs, sems = refs[:n_in], refs[n_in:n_in + len(ex.out_shape)], refs[n_in + len(ex.out_shape):]
        ex.start(ins, outs, sems)
        if ex.mid is not None:
            ex.mid(ins, outs, sems)
        ex.finish(ins, outs, sems)

    return pl.pallas_call(body, name=name, in_specs=[ANY] * n_in, out_specs=[ANY] * len(ex.out_shape),
                          out_shape=ex.out_shape, scratch_shapes=ex.scratch)(*ex.inputs)


def _join(exchanges):
    bounds = []
    i0 = o0 = s0 = 0
    for ex in exchanges:
        bounds.append((i0, o0, s0))
        i0, o0, s0 = i0 + len(ex.inputs), o0 + len(ex.out_shape), s0 + len(ex.scratch)

    def phase(which):
        def run(ins, outs, sems):
            for ex, (i, o, s) in zip(exchanges, bounds):
                fn = getattr(ex, which)
                if fn is not None:
                    fn(ins[i:i + len(ex.inputs)], outs[o:o + len(ex.out_shape)], sems[s:s + len(ex.scratch)])
        return run

    return _Exchange(sum((ex.inputs for ex in exchanges), []), sum((ex.out_shape for ex in exchanges), []),
                     sum((ex.scratch for ex in exchanges), []), phase("start"), phase("finish"),
                     phase("mid") if any(ex.mid is not None for ex in exchanges) else None)


def _gather_exchange(parts):
    n = len(parts)
    kinds = [kd for _, kd in parts]
    sizes = [a.shape[1] if kd == "col" else a.shape[0] for a, kd in parts]

    def plan(ins, outs, sems):
        send_sems, recv_sems, local_sems = sems
        x, y, c = _coords()
        me, sibling = (x, y, c), (x, y, 1 - c)
        chips = [(1 - x, y), (x, 1 - y), (1 - x, 1 - y)]

        def win(p, dev):
            return _window(outs[p], kinds[p], 4 * dev[0] + 2 * dev[1] + dev[2], sizes[p])

        def copy(p, k, block, to, src=None):
            return pltpu.make_async_remote_copy(
                src_ref=win(p, block) if src is None else src, dst_ref=win(p, block),
                send_sem=send_sems.at[p * 7 + k], recv_sem=recv_sems.at[p * 7 + k], device_id=to, device_id_type=MESH_ID)

        mine = [pltpu.make_async_copy(ins[p], win(p, me), local_sems.at[p]) for p in range(n)]
        first, arrived, passed, rest = [], [], [], []
        for p in range(n):
            first.append(copy(p, 0, me, sibling, src=ins[p]))
            first += [copy(p, 1 + q, me, (*chip, c), src=ins[p]) for q, chip in enumerate(chips)]
            rest.append(copy(p, 0, sibling, me))
            rest += [copy(p, 4 + q, (*chip, 1 - c), me) for q, chip in enumerate(chips)]
        for q, chip in enumerate(chips):
            for p in range(n):
                arrived.append(copy(p, 1 + q, (*chip, c), me))
                passed.append(copy(p, 4 + q, (*chip, c), sibling))
        return mine, first, arrived, passed, rest

    def start(ins, outs, sems):
        mine, first, _, _, _ = plan(ins, outs, sems)
        for cp in mine + first:
            cp.start()

    def mid(ins, outs, sems):
        _, _, arrived, passed, _ = plan(ins, outs, sems)
        for got, fwd in zip(arrived, passed):
            got.wait_recv()
            fwd.start()

    def finish(ins, outs, sems):
        mine, first, _, passed, rest = plan(ins, outs, sems)
        for cp in rest:
            cp.wait_recv()
        for cp in first + passed:
            cp.wait_send()
        for cp in mine:
            cp.wait()

    out_shape = [jax.ShapeDtypeStruct((a.shape[0], N_DEV * a.shape[1]) if kd == "col" else (N_DEV * a.shape[0], a.shape[1]),
                                      a.dtype) for a, kd in parts]
    scratch = [pltpu.SemaphoreType.DMA((7 * n,)), pltpu.SemaphoreType.DMA((7 * n,)), pltpu.SemaphoreType.DMA((n,))]
    return _Exchange([a for a, _ in parts], out_shape, scratch, start, finish, mid)


def _all_gather(parts, name):
    return _run_exchange(_gather_exchange(parts), name)


def _shard_shape(a, kd):
    return (a.shape[0], a.shape[1] // N_DEV) if kd == "col" else (a.shape[0] // N_DEV, a.shape[1])


def _symmetric_exchange(inputs, out_shape, n_copies, plan):
    def start(ins, outs, sems):
        for cp in plan(ins, outs, sems):
            cp.start()

    def finish(ins, outs, sems):
        copies = plan(ins, outs, sems)
        for cp in copies:
            cp.wait_recv()
        for cp in copies:
            cp.wait_send()

    scratch = [pltpu.SemaphoreType.DMA((n_copies,)), pltpu.SemaphoreType.DMA((n_copies,))]
    return _Exchange(inputs, out_shape, scratch, start, finish)


def _rs_pair_exchange(grads):
    n = len(grads)
    kinds = [kd for _, kd in grads]
    shapes = [_shard_shape(a, kd) for a, kd in grads]

    def plan(ins, outs, sems):
        send_sems, recv_sems = sems
        x, y, c = _coords()
        copies = []
        for p in range(n):
            size = shapes[p][1] if kinds[p] == "col" else shapes[p][0]
            for s in range(4):
                src = _window(ins[p], kinds[p], 2 * s + (1 - c), size)
                copies.append(pltpu.make_async_remote_copy(
                    src_ref=src, dst_ref=outs[p].at[s], send_sem=send_sems.at[4 * p + s], recv_sem=recv_sems.at[4 * p + s],
                    device_id=(x, y, 1 - c), device_id_type=MESH_ID))
        return copies

    return _symmetric_exchange([a for a, _ in grads], [jax.ShapeDtypeStruct((4,) + shapes[p], BF16) for p in range(n)],
                               4 * n, plan)


def _rs_chips_exchange(sums):
    n = len(sums)

    def plan(ins, outs, sems):
        send_sems, recv_sems = sems
        x, y, c = _coords()
        chips = [(1 - x, y), (x, 1 - y), (1 - x, 1 - y)]
        copies = []
        for p in range(n):
            for q, (cx, cy) in enumerate(chips):
                copies.append(pltpu.make_async_remote_copy(
                    src_ref=ins[p].at[2 * cx + cy], dst_ref=outs[p].at[q],
                    send_sem=send_sems.at[3 * p + q], recv_sem=recv_sems.at[3 * p + q],
                    device_id=(cx, cy, c), device_id_type=MESH_ID))
        return copies

    return _symmetric_exchange(list(sums), [jax.ShapeDtypeStruct((3,) + a.shape[1:], BF16) for a in sums], 3 * n, plan)


def _pair_sum(grad, kd, recv, core, name):
    _, r, cw = recv.shape
    tr = min(r, TM)

    def body(core_ref, g_ref, r_ref, o_ref):
        del core_ref
        o_ref[0] = (g_ref[...].astype(F32) + r_ref[0].astype(F32)).astype(BF16)

    if kd == "col":
        g_spec = pl.BlockSpec((tr, cw), lambda s, i, core_ref: (i, 2 * s + core_ref[0]))
    else:
        g_spec = pl.BlockSpec((tr, cw), lambda s, i, core_ref: ((2 * s + core_ref[0]) * (r // tr) + i, 0))
    grid_spec = pltpu.PrefetchScalarGridSpec(
        num_scalar_prefetch=1, grid=(4, r // tr),
        in_specs=[g_spec, pl.BlockSpec((1, tr, cw), lambda s, i, core_ref: (s, i, 0))],
        out_specs=pl.BlockSpec((1, tr, cw), lambda s, i, core_ref: (s, i, 0)))
    return pl.pallas_call(
        body, name=name, grid_spec=grid_spec, out_shape=jax.ShapeDtypeStruct(recv.shape, BF16),
        compiler_params=_cparams(2),
    )(core, grad, recv)


def _adam_math(w, g, m, v):
    m2 = B1 * m + (1.0 - B1) * g
    v2 = B2 * v + (1.0 - B2) * (g * g)
    m_hat = m2 / (1.0 - B1 ** STEP)
    v_hat = v2 / (1.0 - B2 ** STEP)
    delta = -LR * (m_hat / (jnp.sqrt(v_hat) + ADAM_EPS) + WD * w)
    return delta, m2, v2


def _chip_sum_adam(items, chip, tr, name, ride=None):
    r = items[0][1].shape[0]
    steps = r // tr
    n_parts = [len(parts) for parts, _, _, _ in items]
    r_in = 0 if ride is None else len(ride.inputs)
    r_out = 0 if ride is None else len(ride.out_shape)
    n_in = sum(2 * k + 3 for k in n_parts)
    n_out = 4 * len(items)

    def body(chip_ref, *refs):
        del chip_ref
        ins, refs = refs[:n_in], refs[n_in:]
        r_ins, refs = refs[:r_in], refs[r_in:]
        outs, refs = refs[:n_out], refs[n_out:]
        r_outs, sems = refs[:r_out], refs[r_out:]
        i = pl.program_id(0)
        if ride is not None:
            @pl.when(i == 0)
            def _():
                ride.start(r_ins, r_outs, sems)

        pos = 0
        for q, (k, (_, w, _, _)) in enumerate(zip(n_parts, items)):
            cols = []
            for _ in range(k):
                p_ref, c_ref = ins[pos], ins[pos + 1]
                pos += 2
                cols.append(p_ref[0].astype(F32) + c_ref[0].astype(F32) + c_ref[1].astype(F32) + c_ref[2].astype(F32))
            g = (cols[0] if k == 1 else jnp.concatenate(cols, axis=1))[:, :w.shape[1]]
            w_ref, m_ref, v_ref = ins[pos:pos + 3]
            pos += 3
            delta, m2, v2 = _adam_math(w_ref[...], g, m_ref[...], v_ref[...])
            outs[4 * q][...] = g
            outs[4 * q + 1][...] = delta
            outs[4 * q + 2][...] = m2
            outs[4 * q + 3][...] = v2

        if ride is not None:
            @pl.when(i == steps - 1)
            def _():
                ride.finish(r_ins, r_outs, sems)

    in_specs, args, out_specs, out_shape = [], [], [], []
    for parts, w, m, v in items:
        for psum, recv in parts:
            pc = psum.shape[2]
            in_specs += [pl.BlockSpec((1, tr, pc), lambda i, chip_ref: (chip_ref[0], i, 0)),
                         pl.BlockSpec((3, tr, pc), lambda i, chip_ref: (0, i, 0))]
            args += [psum, recv]
        loc = pl.BlockSpec((tr, w.shape[1]), lambda i, chip_ref: (i, 0))
        in_specs += [loc] * 3
        args += [w, m, v]
        out_specs += [loc] * 4
        out_shape += [jax.ShapeDtypeStruct(w.shape, F32)] * 4
    grid_spec = pltpu.PrefetchScalarGridSpec(
        num_scalar_prefetch=1, grid=(steps,), in_specs=in_specs + [ANY] * r_in, out_specs=out_specs + [ANY] * r_out,
        scratch_shapes=[] if ride is None else ride.scratch)
    res = pl.pallas_call(
        body, name=name, grid_spec=grid_spec, out_shape=out_shape + ([] if ride is None else ride.out_shape),
        compiler_params=_cparams(1),
    )(chip, *args, *([] if ride is None else ride.inputs))
    return res if ride is None else (res[:n_out], res[n_out:])


def _small_allreduce_adam(partials, params, moms, vels, plain, name, ride=None):
    n, n_plain = len(partials), len(plain)
    summed = list(partials) + list(plain)
    row0 = []
    rows = 0
    for a in summed:
        if a.shape[0] >= 8:
            rows = _pad_to(rows, 8)
        row0.append(rows)
        rows += a.shape[0]
    rows = _pad_to(rows, 8)
    width = max(a.shape[1] for a in summed)
    r_in = 0 if ride is None else len(ride.inputs)
    r_out = 0 if ride is None else len(ride.out_shape)
    n_out = 4 * n + n_plain

    def body(*refs):
        w_in, m_in, v_in = refs[0:n], refs[n:2 * n], refs[2 * n:3 * n]
        g_in, refs = refs[3 * n:4 * n + n_plain], refs[4 * n + n_plain:]
        r_ins, refs = refs[:r_in], refs[r_in:]
        outs, refs = refs[:n_out], refs[n_out:]
        r_outs, refs = refs[:r_out], refs[r_out:]
        pair, chips, send_sems, recv_sems = refs[:4]
        if ride is not None:
            ride.start(r_ins, r_outs, refs[4:])
        x, y, c = _coords()
        chip = 2 * x + y
        pair[c] = jnp.zeros((rows, width), F32)
        for p, a in enumerate(summed):
            r, cw = a.shape
            pair[c, row0[p]:row0[p] + r, 0:cw] = g_in[p][...]
        swap = pltpu.make_async_remote_copy(src_ref=pair.at[c], dst_ref=pair.at[c], send_sem=send_sems.at[0],
                                            recv_sem=recv_sems.at[0], device_id=(x, y, 1 - c), device_id_type=MESH_ID)
        swap.start()
        swap.wait_recv()
        swap.wait_send()
        chips[chip] = pair[0] + pair[1]
        copies = [pltpu.make_async_remote_copy(
            src_ref=chips.at[chip], dst_ref=chips.at[chip], send_sem=send_sems.at[1 + q], recv_sem=recv_sems.at[1 + q],
            device_id=(cx, cy, c), device_id_type=MESH_ID) for q, (cx, cy) in enumerate([(1 - x, y), (x, 1 - y), (1 - x, 1 - y)])]
        for cp in copies:
            cp.start()
        for cp in copies:
            cp.wait_recv()
        for cp in copies:
            cp.wait_send()
        for p, a in enumerate(summed):
            r, cw = a.shape
            g = chips[0, row0[p]:row0[p] + r, 0:cw]
            for q in range(1, 4):
                g = g + chips[q, row0[p]:row0[p] + r, 0:cw]
            if p >= n:
                outs[4 * n + p - n][...] = g
                continue
            delta, m2, v2 = _adam_math(w_in[p][...], g, m_in[p][...], v_in[p][...])
            outs[4 * p][...] = g
            outs[4 * p + 1][...] = delta
            outs[4 * p + 2][...] = m2
            outs[4 * p + 3][...] = v2
        if ride is not None:
            ride.finish(r_ins, r_outs, refs[4:])

    out_shape = []
    for a in partials:
        out_shape += [jax.ShapeDtypeStruct(a.shape, F32)] * 4
    out_shape += [jax.ShapeDtypeStruct(a.shape, F32) for a in plain]
    res = pl.pallas_call(
        body, name=name, in_specs=[VMEM_SPEC] * (4 * n + n_plain) + [ANY] * r_in,
        out_specs=[VMEM_SPEC] * n_out + [ANY] * r_out, out_shape=out_shape + ([] if ride is None else ride.out_shape),
        scratch_shapes=[pltpu.VMEM((2, rows, width), F32), pltpu.VMEM((4, rows, width), F32),
                        pltpu.SemaphoreType.DMA((4,)), pltpu.SemaphoreType.DMA((4,))] + ([] if ride is None else ride.scratch),
    )(*params, *moms, *vels, *partials, *plain, *([] if ride is None else ride.inputs))
    return res if ride is None else (res[:n_out], res[n_out:])


def _local_step(xf, tgt, nseq, seq, cols1_all, later, small_w, core, small_step):
    d = xf.shape[1]
    n1, n2, gain, pool_w, pool_scale, n3, nf = small_w
    tf = 2 * cols1_all.shape[1] // N_DEV
    consts = _retention_constants()
    cos_t, sin_t = _rotary_tables(seq)
    wp_b = pool_w.astype(BF16)

    def pair_sums(grads, recv, names):
        return [_pair_sum(g, kd, r, core, "pair_sum_" + nm) for (g, kd), r, nm in zip(grads, recv, names)]

    def riding(host):
        return _gather_exchange(later[host])

    both = lambda first, second: _join([_rs_chips_exchange(first), _rs_pair_exchange([second])])

    (h1, b1, sil1, dsil1, s1), (d1_all, win_all, wout_all, gate2_all, up2_all) = _ffn_act(
        xf, n1, (cols1_all, 0), (cols1_all, 1), "ffn1_act", ride=riding("ffn1_act"))
    x1, (d2_all,) = _ffn_down(s1, xf, d1_all, "ffn1_down", ride=riding("ffn1_down"))
    h2, qs, kr, vv, gg, uu = _mix_in(x1, n2, win_all, cos_t, sin_t, seq, "mix_in")
    x2, mix, oo, pooled, states = _mix_core_fwd(qs, kr, vv, gg, uu, x1, consts, gain, wp_b, pool_scale, wout_all,
                                                 nseq, seq, "mix_core_fwd")
    (x3, h3, b3, sil3, dsil3), (cols2_t, d2_t, win_t, wout_t) = _ffn_fwd(
        x2, n3, (gate2_all, 0), (up2_all, 0), d2_all, "ffn2_fwd", ride=riding("ffn2_fwd"))
    dx3, dnf, loss_part, dx3b = _loss_head(x3, nf, tgt, "loss_head")
    out = {}

    (da3, db3, g_wd2), (cols1_t, d1_t) = _ffn_bwd_act(dx3b, b3, sil3, dsil3, d2_t, "ffn2_bwd_act",
                                                      ride=riding("ffn2_bwd_act"))
    names2 = ["ffn2_gate", "ffn2_up", "ffn2_down"]
    grads2 = [(_wgrad(da3, h3, 1.0, tf, d, "wgrad_gate2"), "row"), (_wgrad(db3, h3, 1.0, tf, d, "wgrad_up2"), "row"),
              (g_wd2, "row")]
    (dx2, dn3), recv2 = _ffn_bwd_in(da3, db3, dx3, x2, n3, cols2_t, 0, "ffn2_bwd_in", ride=_rs_pair_exchange(grads2))
    sums2 = pair_sums(grads2, recv2, names2)
    (dp, dx2b, dgain, dscale, dwp), crecv2 = _mix_core_bwd(
        dx2, qs, kr, vv, gg, oo, pooled, states, consts, gain, wp_b, pool_scale, wout_t, cos_t, sin_t, nseq, seq,
        "mix_core_bwd", ride=_rs_chips_exchange(sums2))
    out.update({nm: [(s, r)] for nm, s, r in zip(names2, sums2, crecv2)})

    names_m = ["w_in", "w_out"]
    grads_m = [(_wgrad(h2, dp, 1.0, d, d, "wgrad_in"), "col"), (_wgrad(mix, dx2b, 1.0, d, d, "wgrad_out"), "row")]
    (dx1, dn2, dx1b), recv_m = _mix_in_bwd(dp, dx2, x1, n2, win_t, "mix_in_bwd", ride=_rs_pair_exchange(grads_m))
    sums_m = pair_sums(grads_m, recv_m, names_m)
    (da1, db1, g_wd1), crecv_m = _ffn_bwd_act(dx1b, b1, sil1, dsil1, d1_t, "ffn1_bwd_act", ride=_rs_chips_exchange(sums_m))
    out.update({nm: [(s, r)] for nm, s, r in zip(names_m, sums_m, crecv_m)})

    dx0, dn1 = _ffn_bwd_in(da1, db1, dx1, xf, n1, cols1_t, 0, "ffn1_bwd_in")
    g_down = (g_wd1, "row")
    g_gate, recv_d = _wgrad(da1, h1, 1.0, tf, d, "wgrad_gate1", ride=_rs_pair_exchange([g_down]))
    g_gate = (g_gate, "row")
    sum_d = pair_sums([g_down], recv_d, ["ffn1_down"])
    g_lo, (crecv_d, recv_g) = _wgrad(db1, h1, 1.0, tf, d // 2, "wgrad_up1_lo", ride=both(sum_d, g_gate), b_cols=(0, 1))
    g_lo = (g_lo, "row")
    sum_g = pair_sums([g_gate], [recv_g], ["ffn1_gate"])
    g_hi, (crecv_g, recv_lo) = _wgrad(db1, h1, 1.0, tf, d // 2, "wgrad_up1_hi", ride=both(sum_g, g_lo), b_cols=(1, 1))
    g_hi = (g_hi, "row")
    sum_lo = pair_sums([g_lo], [recv_lo], ["ffn1_up_lo"])
    small_out, (crecv_lo, recv_hi) = small_step((dn1, dn2, dgain, dwp, dscale, dn3, dnf), loss_part, both(sum_lo, g_hi))
    sum_hi = pair_sums([g_hi], [recv_hi], ["ffn1_up_hi"])
    out.update({"ffn1_gate": [(sum_g[0], crecv_g)], "ffn1_down": [(sum_d[0], crecv_d)],
                "ffn1_up": [(sum_lo[0], crecv_lo), (sum_hi[0], None)]})
    return small_out[-1], dx0, out, small_out[:-1], _rs_chips_exchange(sum_hi)


def kernel(x, norm_ffn1, ffn1_gate, ffn1_up, ffn1_down, norm_mix, w_in, ret_gn_gain, pool_w, pool_scale, w_out, norm_ffn2, ffn2_gate, ffn2_up, ffn2_down, norm_final, loss_target, m_norm_ffn1, m_ffn1_gate, m_ffn1_up, m_ffn1_down, m_norm_mix, m_w_in, m_ret_gn_gain, m_pool_w, m_pool_scale, m_w_out, m_norm_ffn2, m_ffn2_gate, m_ffn2_up, m_ffn2_down, m_norm_final, v_norm_ffn1, v_ffn1_gate, v_ffn1_up, v_ffn1_down, v_norm_mix, v_w_in, v_ret_gn_gain, v_pool_w, v_pool_scale, v_w_out, v_norm_ffn2, v_ffn2_gate, v_ffn2_up, v_ffn2_down, v_norm_final):
    nseq, seq, d = x.shape
    t = nseq * seq
    f_loc = ffn1_gate.shape[2]
    f_pad = _pad_to(f_loc, LANE)
    xf = x.reshape(t, d)
    tgt = loss_target.reshape(t, d)
    core = lax.axis_index("c").astype(jnp.int32).reshape(1)
    chip = (2 * lax.axis_index("x") + lax.axis_index("y")).astype(jnp.int32).reshape(1)

    colp = lambda w: jnp.pad(w[0].astype(BF16), ((0, 0), (0, f_pad - f_loc)))
    rowp = lambda w: jnp.pad(w[0].astype(BF16), ((0, f_pad - f_loc), (0, 0)))
    gate2, up2 = colp(ffn2_gate), colp(ffn2_up)
    cols1 = jnp.concatenate([colp(ffn1_gate), colp(ffn1_up)], axis=0)
    cols2_t = jnp.concatenate([gate2.T, up2.T], axis=1)
    (cols1_all,) = _all_gather([(cols1, "col")], "all_gather_ffn1")
    d1_loc, d2_loc, win_loc, wout_loc = rowp(ffn1_down), rowp(ffn2_down), w_in[0].astype(BF16), w_out[0].astype(BF16)
    later = {"ffn1_act": [(d1_loc, "row"), (win_loc, "col"), (wout_loc, "row"), (gate2, "col"), (up2, "col")],
             "ffn1_down": [(d2_loc, "row")],
             "ffn2_fwd": [(cols2_t, "row"), (d2_loc.T, "col"), (win_loc.T, "row"), (wout_loc.T, "col")],
             "ffn2_bwd_act": [(cols1.T, "row"), (d1_loc.T, "col")]}

    flat = lambda a: a.reshape(pool_w.size // d, d)
    params = [norm_ffn1, norm_mix, ret_gn_gain, flat(pool_w), pool_scale, norm_ffn2, norm_final.reshape(1, d)]
    moms = [m_norm_ffn1, m_norm_mix, m_ret_gn_gain, flat(m_pool_w), m_pool_scale, m_norm_ffn2, m_norm_final.reshape(1, d)]
    vels = [v_norm_ffn1, v_norm_mix, v_ret_gn_gain, flat(v_pool_w), v_pool_scale, v_norm_ffn2, v_norm_final.reshape(1, d)]

    def small_step(parts, loss_part, ride):
        dn1, dn2, dgain, dwp, dscale, dn3, dnf = parts
        return _small_allreduce_adam([dn1, dn2, dgain, flat(dwp), dscale, dn3, dnf], params, moms, vels, [loss_part],
                                     "small_allreduce_adam", ride)

    small_w = (norm_ffn1, norm_mix, ret_gn_gain, pool_w[0], pool_scale, norm_ffn2, norm_final.reshape(1, d))
    loss_sum, dx0, reduced, small_out, pending = _local_step(xf, tgt, nseq, seq, cols1_all, later, small_w, core,
                                                             small_step)

    local = {"ffn1_gate": (ffn1_gate, m_ffn1_gate, v_ffn1_gate), "ffn1_up": (ffn1_up, m_ffn1_up, v_ffn1_up),
             "ffn1_down": (ffn1_down, m_ffn1_down, v_ffn1_down), "w_in": (w_in, m_w_in, v_w_in),
             "w_out": (w_out, m_w_out, v_w_out), "ffn2_gate": (ffn2_gate, m_ffn2_gate, v_ffn2_gate),
             "ffn2_up": (ffn2_up, m_ffn2_up, v_ffn2_up), "ffn2_down": (ffn2_down, m_ffn2_down, v_ffn2_down)}
    flip = lambda nm: nm.endswith("gate") or nm.endswith("up")

    def item(nm):
        view = (lambda a: a[0].T) if flip(nm) else (lambda a: a[0])
        w, m, v = local[nm]
        return reduced[nm], view(w), view(m), view(v)

    big = {}

    def keep(names, res):
        for q, nm in enumerate(names):
            big[nm] = tuple((a.T if flip(nm) else a)[None] for a in res[4 * q:4 * q + 4])

    second = ["ffn2_gate", "ffn2_up", "ffn2_down"]
    res, (last_recv,) = _chip_sum_adam([item(nm) for nm in second], chip, item(second[0])[1].shape[0] // 2, "adam_ffn2",
                                       ride=pending)
    keep(second, res)
    reduced["ffn1_up"][-1] = (reduced["ffn1_up"][-1][0], last_recv)
    first = ["ffn1_gate", "ffn1_up", "ffn1_down"]
    keep(first, _chip_sum_adam([item(nm) for nm in first], chip, item(first[0])[1].shape[0] // 2, "adam_ffn1"))
    for nm in ["w_in", "w_out"]:
        keep([nm], _chip_sum_adam([item(nm)], chip, min(item(nm)[1].shape[0], TM), "adam_" + nm))

    small_names = ["norm_ffn1", "norm_mix", "ret_gn_gain", "pool_w", "pool_scale", "norm_ffn2", "norm_final"]
    shapes = [norm_ffn1.shape, norm_mix.shape, ret_gn_gain.shape, pool_w.shape, pool_scale.shape, norm_ffn2.shape,
              norm_final.shape]
    small = {nm: tuple(small_out[4 * p + q].reshape(shapes[p]) for q in range(4)) for p, nm in enumerate(small_names)}

    loss = loss_sum[0, 0]
    order = ["norm_ffn1", "ffn1_gate", "ffn1_up", "ffn1_down", "norm_mix", "w_in", "ret_gn_gain", "pool_w", "pool_scale",
             "w_out", "norm_ffn2", "ffn2_gate", "ffn2_up", "ffn2_down", "norm_final"]
    both = {**big, **small}
    outs = [loss, dx0.reshape(nseq, seq, d)]
    for q in range(4):
        outs += [both[nm][q] for nm in order]
    return tuple(outs)
```

```python
import numpy as np
import jax
import jax.numpy as jnp
from jax import lax
from jax.experimental import pallas as pl
from jax.experimental.pallas import tpu as pltpu

F32, BF16 = jnp.float32, jnp.bfloat16
MESH_ID = pl.DeviceIdType.MESH
ANY = pl.BlockSpec(memory_space=pl.ANY)
VMEM_SPEC = pl.BlockSpec(memory_space=pltpu.VMEM)

N_DEV = 8
RMS_EPS = 1e-6
GN_EPS = 1e-5
HEADS, DK, DV = 4, 64, 128
QK_W, V_W, POOL_W = HEADS * DK, HEADS * DV, 512
WINDOWS = (2, 4, 8, 16)
GC = POOL_W // len(WINDOWS)
CHUNK = 64
BLK = 4 * CHUNK
HALO = 16
ROPE_BASE = 10000.0
LR, B1, B2, ADAM_EPS, WD, STEP = 0.001, 0.9, 0.999, 1e-08, 0.01, 10
LANE = 128
TM = 512
FFN_TM = 1024
FFN_FWD_TF = 768
WGRAD_TT = 4096
VMEM_LIMIT = 56 * 1024 * 1024


def _cparams(n_axes):
    return pltpu.CompilerParams(dimension_semantics=("arbitrary",) * n_axes, vmem_limit_bytes=VMEM_LIMIT)


class _Exchange:
    def __init__(self, inputs, out_shape, scratch, start, finish, mid=None):
        self.inputs, self.out_shape, self.scratch = list(inputs), list(out_shape), list(scratch)
        self.start, self.finish, self.mid = start, finish, mid


def _pallas(body, name, grid, in_specs, out_specs, out_shape, scratch_shapes, args, ride=None):
    n_axes = len(grid)
    if ride is None:
        return pl.pallas_call(body, name=name, grid=grid, in_specs=in_specs, out_specs=out_specs, out_shape=out_shape,
                              scratch_shapes=scratch_shapes, compiler_params=_cparams(n_axes))(*args)
    n_in, n_out, n_scr = len(in_specs), len(out_specs), len(scratch_shapes)
    r_in, r_out = len(ride.inputs), len(ride.out_shape)

    def hosted(*refs):
        ins, refs = refs[:n_in], refs[n_in:]
        r_ins, refs = refs[:r_in], refs[r_in:]
        outs, refs = refs[:n_out], refs[n_out:]
        r_outs, refs = refs[:r_out], refs[r_out:]
        scr, sems = refs[:n_scr], refs[n_scr:]
        ids = [pl.program_id(a) for a in range(n_axes)]
        first, last, inner0 = ids[0] == 0, ids[0] == grid[0] - 1, None
        for a in range(1, n_axes):
            first = first & (ids[a] == 0)
            last = last & (ids[a] == grid[a] - 1)
            inner0 = (ids[a] == 0) if inner0 is None else inner0 & (ids[a] == 0)

        @pl.when(first)
        def _():
            ride.start(r_ins, r_outs, sems)

        if ride.mid is not None:
            at_mid = ids[0] == grid[0] - 1
            if inner0 is not None:
                at_mid = at_mid & inner0

            @pl.when(at_mid)
            def _():
                ride.mid(r_ins, r_outs, sems)

        body(*ins, *outs, *scr)

        @pl.when(last)
        def _():
            ride.finish(r_ins, r_outs, sems)

    res = pl.pallas_call(
        hosted, name=name, grid=grid, in_specs=list(in_specs) + [ANY] * r_in, out_specs=list(out_specs) + [ANY] * r_out,
        out_shape=list(out_shape) + ride.out_shape, scratch_shapes=list(scratch_shapes) + ride.scratch,
        compiler_params=_cparams(n_axes))(*args, *ride.inputs)
    return res[:n_out], res[n_out:]


def _dot(a, b):
    return jnp.dot(a, b, preferred_element_type=F32)


def _dot_nt(a, b):
    return lax.dot_general(a, b, (((1,), (1,)), ((), ())), preferred_element_type=F32)


def _dot_tn(a, b):
    return lax.dot_general(a, b, (((0,), (0,)), ((), ())), preferred_element_type=F32)


def _sigmoid(x):
    return 0.5 * jnp.tanh(0.5 * x) + 0.5


def _pad_to(n, m):
    return (n + m - 1) // m * m


def _retention_constants():
    gamma = (1.0 - 2.0 ** (-5.0 - np.arange(HEADS, dtype=np.float32))).astype(np.float32)
    log_g = np.log(gamma).astype(np.float32)
    i = np.arange(BLK)
    diff = (i[:, None] - i[None, :]).astype(np.float32)
    same = (i[:, None] // CHUNK) == (i[None, :] // CHUNK)
    earlier = (i[None, :] // CHUNK) < (i[:, None] // CHUNK)
    mask = np.zeros((HEADS, BLK, BLK), np.float32)
    for h in range(HEADS):
        dec_abs = np.exp(log_g[h] * np.abs(diff)).astype(np.float32)
        dec = np.exp(log_g[h] * diff * earlier).astype(np.float32)
        mask[h] = np.where(same, dec_abs, np.where(earlier, dec, 0.0))
    dq = np.zeros((BLK, V_W), np.float32)
    dk = np.zeros((BLK, QK_W), np.float32)
    gbd = np.zeros((QK_W, V_W), np.float32)
    for h in range(HEADS):
        dq[:, h * DV:(h + 1) * DV] = np.exp(log_g[h] * (i + 1.0)).astype(np.float32)[:, None]
        dk[:, h * DK:(h + 1) * DK] = np.exp(log_g[h] * (BLK - 1.0 - i)).astype(np.float32)[:, None]
        gbd[h * DK:(h + 1) * DK, h * DV:(h + 1) * DV] = np.exp(log_g[h] * np.float32(BLK))
    bd = (gbd > 0).astype(np.float32)
    return jnp.asarray(mask), jnp.asarray(dq), jnp.asarray(dk), jnp.asarray(gbd), jnp.asarray(bd)


def _rotary_tables(seq):
    half = DK // 2
    freqs = ROPE_BASE ** (-jnp.arange(half, dtype=F32) * 2.0 / DK)
    ang = jnp.arange(seq, dtype=F32)[:, None] * freqs[None, :]
    cos, sin = jnp.cos(ang), jnp.sin(ang)
    cos_t = jnp.tile(jnp.concatenate([cos, cos], axis=1), (1, HEADS))
    sin_t = jnp.tile(jnp.concatenate([-sin, sin], axis=1), (1, HEADS))
    return cos_t, sin_t


def _swap_halves(x):
    lane = lax.broadcasted_iota(jnp.int32, (1, QK_W), 1)
    first = (lane & (DK - 1)) < DK // 2
    return jnp.where(first, pltpu.roll(x, QK_W - DK // 2, 1), pltpu.roll(x, DK // 2, 1))


def _head_mask(h):
    lane = lax.broadcasted_iota(jnp.int32, (1, QK_W), 1)
    return (lane >= h * DK) & (lane < (h + 1) * DK)


def _ffn_fwd(x, n, gate, up, wd, name, ride=None):
    t, d = x.shape
    (wg, gq), (wu, uq) = gate, up
    fp = wg.shape[1]
    tm = min(t, FFN_TM)
    tf = FFN_FWD_TF
    nj = fp // tf

    def body(x_ref, n_ref, wg_ref, wu_ref, wd_ref, xo_ref, h_ref, b_ref, sil_ref, dsil_ref, acc_ref):
        j = pl.program_id(1)

        @pl.when(j == 0)
        def _():
            xv = x_ref[...]
            r = lax.rsqrt(jnp.mean(xv * xv, axis=-1, keepdims=True) + RMS_EPS)
            h_ref[...] = (xv * r * n_ref[...]).astype(BF16)
            acc_ref[...] = jnp.zeros_like(acc_ref)

        h = h_ref[...]
        a = _dot(h, wg_ref[...])
        b = _dot(h, wu_ref[...])
        sg = _sigmoid(a)
        sil = a * sg
        b_ref[...] = b.astype(BF16)
        sil_ref[...] = sil.astype(BF16)
        dsil_ref[...] = (sg + sil * (1.0 - sg)).astype(BF16)
        acc_ref[...] += _dot((sil * b).astype(BF16), wd_ref[...])

        @pl.when(j == nj - 1)
        def _():
            xo_ref[...] = x_ref[...] + 0.5 * acc_ref[...]

    act = pl.BlockSpec((tm, tf), lambda i, j: (i, j))
    return _pallas(
        body, name, (t // tm, nj),
        [pl.BlockSpec((tm, d), lambda i, j: (i, 0)), pl.BlockSpec((1, d), lambda i, j: (0, 0)),
         pl.BlockSpec((d, tf), lambda i, j: (gq, j)), pl.BlockSpec((d, tf), lambda i, j: (uq, j)),
         pl.BlockSpec((tf, d), lambda i, j: (j, 0))],
        [pl.BlockSpec((tm, d), lambda i, j: (i, 0)), pl.BlockSpec((tm, d), lambda i, j: (i, 0)), act, act, act],
        [jax.ShapeDtypeStruct((t, d), F32), jax.ShapeDtypeStruct((t, d), BF16)] + [jax.ShapeDtypeStruct((t, fp), BF16)] * 3,
        [pltpu.VMEM((tm, d), F32)], (x, n, wg, wu, wd), ride)


def _ffn_act(x, n, gate, up, name, ride=None):
    t, d = x.shape
    (wg, gq), (wu, uq) = gate, up
    fp = wg.shape[1]
    tm = min(t, FFN_TM)
    tf = 2 * fp // N_DEV
    nj = fp // tf

    def body(x_ref, n_ref, wg_ref, wu_ref, h_ref, b_ref, sil_ref, dsil_ref, s_ref):
        @pl.when(pl.program_id(1) == 0)
        def _():
            xv = x_ref[...]
            r = lax.rsqrt(jnp.mean(xv * xv, axis=-1, keepdims=True) + RMS_EPS)
            h_ref[...] = (xv * r * n_ref[...]).astype(BF16)

        h = h_ref[...]
        a = _dot(h, wg_ref[...])
        b = _dot(h, wu_ref[...])
        sg = _sigmoid(a)
        sil = a * sg
        b_ref[...] = b.astype(BF16)
        sil_ref[...] = sil.astype(BF16)
        dsil_ref[...] = (sg + sil * (1.0 - sg)).astype(BF16)
        s_ref[...] = (sil * b).astype(BF16)

    act = pl.BlockSpec((tm, tf), lambda i, j: (i, j))
    return _pallas(
        body, name, (t // tm, nj),
        [pl.BlockSpec((tm, d), lambda i, j: (i, 0)), pl.BlockSpec((1, d), lambda i, j: (0, 0)),
         pl.BlockSpec((d, tf), lambda i, j: (gq, j)), pl.BlockSpec((d, tf), lambda i, j: (uq, j))],
        [pl.BlockSpec((tm, d), lambda i, j: (i, 0)), act, act, act, act],
        [jax.ShapeDtypeStruct((t, d), BF16)] + [jax.ShapeDtypeStruct((t, fp), BF16)] * 4,
        [], (x, n, wg, wu), ride)


def _ffn_down(s, x, wd, name, ride=None):
    t, d = x.shape
    fp = wd.shape[0]
    tm = min(t, FFN_TM)

    def body(s_ref, x_ref, wd_ref, xo_ref):
        xo_ref[...] = x_ref[...] + 0.5 * _dot(s_ref[...], wd_ref[...])

    row = pl.BlockSpec((tm, d), lambda i: (i, 0))
    res = _pallas(body, name, (t // tm,), [pl.BlockSpec((tm, fp), lambda i: (i, 0)), row, pl.BlockSpec((fp, d), lambda i: (0, 0))],
                  [row], [jax.ShapeDtypeStruct((t, d), F32)], [], (s, x, wd), ride)
    return res[0] if ride is None else (res[0][0], res[1])


def _ffn_bwd_act(dxob, b, sil, dsil, wd_t, name, ride=None):
    t, d = dxob.shape
    fp = wd_t.shape[1]
    tm = min(t, FFN_TM)
    tf = 2 * fp // N_DEV
    ni = t // tm

    def body(dx_ref, b_ref, sil_ref, dsil_ref, wd_ref, da_ref, db_ref, gd_ref, acc_ref):
        i = pl.program_id(1)

        @pl.when(i == 0)
        def _():
            acc_ref[...] = jnp.zeros_like(acc_ref)

        dxv = dx_ref[...]
        bv, sv = b_ref[...].astype(F32), sil_ref[...].astype(F32)
        ds = _dot(dxv, wd_ref[...])
        da_ref[...] = (ds * bv * dsil_ref[...].astype(F32)).astype(BF16)
        db_ref[...] = (ds * sv).astype(BF16)
        acc_ref[...] += _dot_tn((sv * bv).astype(BF16), dxv)

        @pl.when(i == ni - 1)
        def _():
            gd_ref[...] = acc_ref[...].astype(BF16)

    act = pl.BlockSpec((tm, tf), lambda c, i: (i, c))
    return _pallas(
        body, name, (fp // tf, ni),
        [pl.BlockSpec((tm, d), lambda c, i: (i, 0)), act, act, act, pl.BlockSpec((d, tf), lambda c, i: (0, c))],
        [act, act, pl.BlockSpec((tf, d), lambda c, i: (c, 0))],
        [jax.ShapeDtypeStruct((t, fp), BF16), jax.ShapeDtypeStruct((t, fp), BF16), jax.ShapeDtypeStruct((fp, d), BF16)],
        [pltpu.VMEM((tf, d), F32)], (dxob, b, sil, dsil, wd_t), ride)


def _ffn_bwd_in(da, db, dxo, x, n, cols_t, gq, name, ride=None):
    t, d = x.shape
    fp = cols_t.shape[0]
    tm = min(t, FFN_TM)
    tf = 2 * fp // N_DEV
    nj = fp // tf

    def body(da_ref, db_ref, dxo_ref, x_ref, n_ref, wg_ref, wu_ref, dx_ref, dn_ref, acc_ref):
        i, j = pl.program_id(0), pl.program_id(1)

        @pl.when((i == 0) & (j == 0))
        def _():
            dn_ref[...] = jnp.zeros_like(dn_ref)

        @pl.when(j == 0)
        def _():
            acc_ref[...] = jnp.zeros_like(acc_ref)

        acc_ref[...] += _dot(da_ref[...], wg_ref[...]) + _dot(db_ref[...], wu_ref[...])

        @pl.when(j == nj - 1)
        def _():
            xv = x_ref[...]
            r = lax.rsqrt(jnp.mean(xv * xv, axis=-1, keepdims=True) + RMS_EPS)
            xh = xv * r
            dh = acc_ref[...]
            dn_ref[...] += jnp.sum(dh * xh, axis=0, keepdims=True)
            dhn = dh * n_ref[...]
            dx_ref[...] = dxo_ref[...] + r * (dhn - xh * jnp.mean(dhn * xh, axis=-1, keepdims=True))

    act = pl.BlockSpec((tm, tf), lambda i, j: (i, j))
    row = pl.BlockSpec((tm, d), lambda i, j: (i, 0))
    return _pallas(
        body, name, (t // tm, nj),
        [act, act, row, row, pl.BlockSpec((1, d), lambda i, j: (0, 0)),
         pl.BlockSpec((tf, d), lambda i, j: (j, gq)), pl.BlockSpec((tf, d), lambda i, j: (j, gq + 1))],
        [row, pl.BlockSpec((1, d), lambda i, j: (0, 0))],
        [jax.ShapeDtypeStruct((t, d), F32), jax.ShapeDtypeStruct((1, d), F32)],
        [pltpu.VMEM((tm, d), F32)], (da, db, dxo, x, n, cols_t, cols_t), ride)


def _wgrad(a, b, scale, tk, tn, name, ride=None, b_cols=None):
    t, k = a.shape
    q0, nq = (0, b.shape[1] // tn) if b_cols is None else b_cols
    n = nq * tn
    tt = min(t, WGRAD_TT)
    nt = t // tt

    def body(a_ref, b_ref, o_ref, acc_ref):
        s = pl.program_id(2)

        @pl.when(s == 0)
        def _():
            acc_ref[...] = jnp.zeros_like(acc_ref)

        acc_ref[...] += _dot_tn(a_ref[...], b_ref[...])

        @pl.when(s == nt - 1)
        def _():
            o_ref[...] = (scale * acc_ref[...]).astype(BF16)

    res = _pallas(
        body, name, (k // tk, n // tn, nt),
        [pl.BlockSpec((tt, tk), lambda p, q, s: (s, p)), pl.BlockSpec((tt, tn), lambda p, q, s: (s, q + q0))],
        [pl.BlockSpec((tk, tn), lambda p, q, s: (p, q))], [jax.ShapeDtypeStruct((k, n), BF16)],
        [pltpu.VMEM((tk, tn), F32)], (a, b), ride)
    return res[0] if ride is None else (res[0][0], res[1])


def _mix_in(x, n, w_in, cos_t, sin_t, seq, name):
    t, d = x.shape
    per_seq = seq // TM

    def body(x_ref, n_ref, w_ref, c_ref, s_ref, h_ref, q_ref, k_ref, v_ref, g_ref, u_ref):
        xv = x_ref[...]
        r = lax.rsqrt(jnp.mean(xv * xv, axis=-1, keepdims=True) + RMS_EPS)
        h = (xv * r * n_ref[...]).astype(BF16)
        h_ref[...] = h
        p = _dot(h, w_ref[...])
        c, s = c_ref[...], s_ref[...]
        q = p[:, :QK_W]
        k = p[:, QK_W:2 * QK_W]
        q_ref[...] = ((q * c + _swap_halves(q) * s) * (DK ** -0.5)).astype(BF16)
        k_ref[...] = (k * c + _swap_halves(k) * s).astype(BF16)
        v_ref[...] = p[:, 2 * QK_W:2 * QK_W + V_W].astype(BF16)
        g_ref[...] = p[:, 2 * QK_W + V_W:2 * QK_W + 2 * V_W]
        u_ref[...] = p[:, 2 * QK_W + 2 * V_W:]

    tile = lambda w: pl.BlockSpec((TM, w), lambda i: (i, 0))
    return pl.pallas_call(
        body, name=name, grid=(t // TM,),
        in_specs=[tile(d), pl.BlockSpec((1, d), lambda i: (0, 0)), pl.BlockSpec(w_in.shape, lambda i: (0, 0)),
                  pl.BlockSpec((TM, QK_W), lambda i: (i % per_seq, 0)), pl.BlockSpec((TM, QK_W), lambda i: (i % per_seq, 0))],
        out_specs=[tile(d), tile(QK_W), tile(QK_W), tile(V_W), tile(V_W), tile(POOL_W)],
        out_shape=[jax.ShapeDtypeStruct((t, d), BF16), jax.ShapeDtypeStruct((t, QK_W), BF16),
                   jax.ShapeDtypeStruct((t, QK_W), BF16), jax.ShapeDtypeStruct((t, V_W), BF16),
                   jax.ShapeDtypeStruct((t, V_W), F32), jax.ShapeDtypeStruct((t, POOL_W), F32)],
        compiler_params=_cparams(1),
    )(x, n, w_in, cos_t, sin_t)


def _mix_in_bwd(dp, dx2, x1, n, w_in_t, name, ride=None):
    t, d = x1.shape

    def body(dp_ref, dx2_ref, x_ref, n_ref, w_ref, dx_ref, dn_ref, dxb_ref):
        @pl.when(pl.program_id(0) == 0)
        def _():
            dn_ref[...] = jnp.zeros_like(dn_ref)

        dh = _dot(dp_ref[...], w_ref[...])
        xv = x_ref[...]
        r = lax.rsqrt(jnp.mean(xv * xv, axis=-1, keepdims=True) + RMS_EPS)
        xh = xv * r
        dn_ref[...] += jnp.sum(dh * xh, axis=0, keepdims=True)
        dhn = dh * n_ref[...]
        dx = dx2_ref[...] + r * (dhn - xh * jnp.mean(dhn * xh, axis=-1, keepdims=True))
        dx_ref[...] = dx
        dxb_ref[...] = (0.5 * dx).astype(BF16)

    tile = lambda w: pl.BlockSpec((TM, w), lambda i: (i, 0))
    return _pallas(
        body, name, (t // TM,),
        [tile(dp.shape[1]), tile(d), tile(d), pl.BlockSpec((1, d), lambda i: (0, 0)),
         pl.BlockSpec(w_in_t.shape, lambda i: (0, 0))],
        [tile(d), pl.BlockSpec((1, d), lambda i: (0, 0)), tile(d)],
        [jax.ShapeDtypeStruct((t, d), F32), jax.ShapeDtypeStruct((1, d), F32), jax.ShapeDtypeStruct((t, d), BF16)],
        [], (dp, dx2, x1, n, w_in_t), ride)


def _group_norm(o):
    parts, rstds = [], []
    for h in range(HEADS):
        oh = o[:, h * DV:(h + 1) * DV]
        dlt = oh - jnp.mean(oh, axis=-1, keepdims=True)
        rstd = lax.rsqrt(jnp.mean(dlt * dlt, axis=-1, keepdims=True) + GN_EPS)
        parts.append(dlt * rstd)
        rstds.append(rstd)
    return jnp.concatenate(parts, axis=1), rstds


def _mix_core_fwd(qs, k, v, g, u, x1, consts, gain, wp, scale, w_out, nseq, seq, name):
    t, d = x1.shape
    nblk = seq // BLK
    mask, dq, dk, gbd, bd = consts

    def body(q_ref, k_ref, v_ref, g_ref, u_ref, x1_ref, m_ref, dq_ref, dk_ref, gbd_ref, bd_ref, gain_ref, wp_ref,
             sc_ref, wo_ref, x2_ref, mix_ref, o_ref, pooled_ref, st_ref, state, halo):
        j = pl.program_id(1)

        @pl.when(j == 0)
        def _():
            state[...] = jnp.zeros_like(state)
            halo[...] = jnp.zeros_like(halo)

        qv, kv, vv = q_ref[...], k_ref[...], v_ref[...]
        st = state[...]
        st_ref[0] = st
        cross = _dot(qv, st.astype(BF16)) * dq_ref[...]
        outs = []
        for h in range(HEADS):
            qh = jnp.where(_head_mask(h), qv, jnp.zeros_like(qv))
            am = (_dot_nt(qh, kv) * m_ref[h]).astype(BF16)
            outs.append(_dot(am, vv[:, h * DV:(h + 1) * DV]))
        o = jnp.concatenate(outs, axis=1) + cross
        o_ref[...] = o
        kd = (kv.astype(F32) * dk_ref[...]).astype(BF16)
        state[...] = gbd_ref[...] * st + _dot_tn(kd, vv) * bd_ref[...]

        gv = g_ref[...]
        nrm, _ = _group_norm(o)
        ret = (gv * _sigmoid(gv)) * (nrm * gain_ref[...])

        uv = u_ref[...]
        c = jnp.concatenate([halo[...], uv], axis=0)
        halo[...] = uv[BLK - HALO:, :]
        pos = j * BLK + lax.broadcasted_iota(jnp.int32, (BLK, 1), 0)
        parts = []
        for gi, w in enumerate(WINDOWS):
            c = c + pltpu.roll(c, w // 2, 0)
            cnt = jnp.minimum(pos + 1, w).astype(F32)
            parts.append(c[HALO:, :GC] / cnt)
            if gi + 1 < len(WINDOWS):
                c = c[:, GC:]
        pooled = (jnp.concatenate(parts, axis=1) - uv).astype(BF16)
        pooled_ref[...] = pooled
        z = jnp.concatenate([_dot(pooled[:, gi * GC:(gi + 1) * GC], wp_ref[gi]) for gi in range(len(WINDOWS))], axis=1)
        mix = jnp.concatenate([ret, z * sc_ref[...]], axis=1).astype(BF16)
        mix_ref[...] = mix
        x2_ref[...] = x1_ref[...] + _dot(mix, wo_ref[...])

    blk = lambda w: pl.BlockSpec((BLK, w), lambda i, j: (i * nblk + j, 0))
    full = lambda a: pl.BlockSpec(a.shape, lambda i, j: (0,) * a.ndim)
    return _pallas(
        body, name, (nseq, nblk),
        [blk(QK_W), blk(QK_W), blk(V_W), blk(V_W), blk(POOL_W), blk(d),
         full(mask), full(dq), full(dk), full(gbd), full(bd), full(gain), full(wp), full(scale), full(w_out)],
        [blk(d), blk(d), blk(V_W), blk(POOL_W), pl.BlockSpec((1, QK_W, V_W), lambda i, j: (i * nblk + j, 0, 0))],
        [jax.ShapeDtypeStruct((t, d), F32), jax.ShapeDtypeStruct((t, d), BF16),
         jax.ShapeDtypeStruct((t, V_W), F32), jax.ShapeDtypeStruct((t, POOL_W), BF16),
         jax.ShapeDtypeStruct((nseq * nblk, QK_W, V_W), F32)],
        [pltpu.VMEM((QK_W, V_W), F32), pltpu.VMEM((HALO, POOL_W), F32)],
        (qs, k, v, g, u, x1, mask, dq, dk, gbd, bd, gain, wp, scale, w_out))


def _mix_core_bwd(dx2, qs, k, v, g, o, pooled, st, consts, gain, wp, scale, w_out, cos_t, sin_t, nseq, seq, name,
                  ride=None):
    t, d = dx2.shape
    nblk = seq // BLK
    mask, dq, dk, gbd, bd = consts
    n_win = len(WINDOWS)

    def body(dx2_ref, q_ref, k_ref, v_ref, g_ref, o_ref, pooled_ref, st_ref, m_ref, dq_ref, dk_ref, gbd_ref, bd_ref,
             gain_ref, wp_ref, sc_ref, wo_ref, c_ref, s_ref,
             dp_ref, dx2b_ref, dgain_ref, dscale_ref, dwp_ref, rstate, carry):
        i, j = pl.program_id(0), pl.program_id(1)

        @pl.when((i == 0) & (j == 0))
        def _():
            dgain_ref[...] = jnp.zeros_like(dgain_ref)
            dscale_ref[...] = jnp.zeros_like(dscale_ref)
            dwp_ref[...] = jnp.zeros_like(dwp_ref)

        @pl.when(j == 0)
        def _():
            rstate[...] = jnp.zeros_like(rstate)
            carry[...] = jnp.zeros_like(carry)

        dx2b = dx2_ref[...].astype(BF16)
        dx2b_ref[...] = dx2b
        dmix = _dot(dx2b, wo_ref[...])
        dret, dpool = dmix[:, :V_W], dmix[:, V_W:]

        gv, ov, gain_v = g_ref[...], o_ref[...], gain_ref[...]
        sg = _sigmoid(gv)
        sil = gv * sg
        nrm, rstds = _group_norm(ov)
        dg = dret * (nrm * gain_v) * (sg * (1.0 + gv * (1.0 - sg)))
        dgn = dret * sil
        dgain_ref[...] += jnp.sum(dgn * nrm, axis=0, keepdims=True)
        dnrm = dgn * gain_v
        do_parts = []
        for h in range(HEADS):
            dn_h = dnrm[:, h * DV:(h + 1) * DV]
            n_h = nrm[:, h * DV:(h + 1) * DV]
            do_parts.append(rstds[h] * (dn_h - jnp.mean(dn_h, axis=-1, keepdims=True)
                                        - n_h * jnp.mean(dn_h * n_h, axis=-1, keepdims=True)))
        do = jnp.concatenate(do_parts, axis=1)
        dob = do.astype(BF16)

        qv, kv, vv = q_ref[...], k_ref[...], v_ref[...]
        stb = st_ref[0].astype(BF16)
        rs = rstate[...]
        rsb = rs.astype(BF16)
        dod = (do * dq_ref[...]).astype(BF16)
        dqs = _dot_nt(dod, stb)
        dst = _dot_tn(qv, dod) * bd_ref[...]
        dkf = dk_ref[...]
        kd = (kv.astype(F32) * dkf).astype(BF16)
        dks = _dot_nt(vv, rsb) * dkf
        dvs = _dot(kd, rsb)
        dv_parts = []
        for h in range(HEADS):
            hm = _head_mask(h)
            qh = jnp.where(hm, qv, jnp.zeros_like(qv))
            mh = m_ref[h]
            am = (_dot_nt(qh, kv) * mh).astype(BF16)
            dpm = (_dot_nt(dob[:, h * DV:(h + 1) * DV], vv[:, h * DV:(h + 1) * DV]) * mh).astype(BF16)
            dqs = dqs + jnp.where(hm, _dot(dpm, kv), 0.0)
            dks = dks + jnp.where(hm, _dot_tn(dpm, qv), 0.0)
            dv_parts.append(_dot_tn(am, dob[:, h * DV:(h + 1) * DV]))
        dvs = dvs + jnp.concatenate(dv_parts, axis=1)
        rstate[...] = dst + gbd_ref[...] * rs

        cv, sv = c_ref[...], s_ref[...]
        dqr = dqs * (DK ** -0.5)
        dq_pre = dqr * cv + _swap_halves(dqr * sv)
        dk_pre = dks * cv + _swap_halves(dks * sv)

        pv = pooled_ref[...]
        sc = sc_ref[...]
        dzb = (dpool * sc).astype(BF16)
        z_parts, dpo_parts = [], []
        for gi in range(n_win):
            p_g = pv[:, gi * GC:(gi + 1) * GC]
            dz_g = dzb[:, gi * GC:(gi + 1) * GC]
            z_parts.append(_dot(p_g, wp_ref[gi]))
            dwp_ref[gi] += _dot_tn(p_g, dz_g)
            dpo_parts.append(_dot_nt(dz_g, wp_ref[gi]))
        dscale_ref[...] += jnp.sum(dpool * jnp.concatenate(z_parts, axis=1), axis=0, keepdims=True)
        dpo = jnp.concatenate(dpo_parts, axis=1)
        pos = (nblk - 1 - j) * BLK + lax.broadcasted_iota(jnp.int32, (BLK, 1), 0)
        e = jnp.concatenate(
            [dpo[:, gi * GC:(gi + 1) * GC] / jnp.minimum(pos + 1, w).astype(F32) for gi, w in enumerate(WINDOWS)], axis=1)
        c = jnp.concatenate([e, carry[...]], axis=0)
        carry[...] = e[:HALO, :]
        rows = BLK + HALO
        lead = []
        for gi, w in enumerate(WINDOWS):
            c = c + pltpu.roll(c, rows - w // 2, 0)
            lead.append(c[:BLK, :GC])
            if gi + 1 < n_win:
                c = c[:, GC:]
        du = jnp.concatenate(lead, axis=1) - dpo

        dp_ref[:, 0:QK_W] = dq_pre.astype(BF16)
        dp_ref[:, QK_W:2 * QK_W] = dk_pre.astype(BF16)
        dp_ref[:, 2 * QK_W:2 * QK_W + V_W] = dvs.astype(BF16)
        dp_ref[:, 2 * QK_W + V_W:2 * QK_W + 2 * V_W] = dg.astype(BF16)
        dp_ref[:, 2 * QK_W + 2 * V_W:] = du.astype(BF16)

    rev = lambda i, j: i * nblk + (nblk - 1 - j)
    blk = lambda w: pl.BlockSpec((BLK, w), lambda i, j: (rev(i, j), 0))
    full = lambda a: pl.BlockSpec(a.shape, lambda i, j: (0,) * a.ndim)
    in_w = 2 * QK_W + 2 * V_W + POOL_W
    return _pallas(
        body, name, (nseq, nblk),
        [blk(d), blk(QK_W), blk(QK_W), blk(V_W), blk(V_W), blk(V_W), blk(POOL_W),
         pl.BlockSpec((1, QK_W, V_W), lambda i, j: (rev(i, j), 0, 0)),
         full(mask), full(dq), full(dk), full(gbd), full(bd), full(gain), full(wp), full(scale), full(w_out),
         pl.BlockSpec((BLK, QK_W), lambda i, j: (nblk - 1 - j, 0)),
         pl.BlockSpec((BLK, QK_W), lambda i, j: (nblk - 1 - j, 0))],
        [blk(in_w), blk(d), pl.BlockSpec((1, V_W), lambda i, j: (0, 0)),
         pl.BlockSpec((1, POOL_W), lambda i, j: (0, 0)), pl.BlockSpec((n_win, GC, GC), lambda i, j: (0, 0, 0))],
        [jax.ShapeDtypeStruct((t, in_w), BF16), jax.ShapeDtypeStruct((t, d), BF16),
         jax.ShapeDtypeStruct((1, V_W), F32), jax.ShapeDtypeStruct((1, POOL_W), F32),
         jax.ShapeDtypeStruct((n_win, GC, GC), F32)],
        [pltpu.VMEM((QK_W, V_W), F32), pltpu.VMEM((HALO, POOL_W), F32)],
        (dx2, qs, k, v, g, o, pooled, st, mask, dq, dk, gbd, bd, gain, wp, scale, w_out, cos_t, sin_t), ride)


def _loss_head(x3, nf, tgt, name):
    t, d = x3.shape

    def body(x_ref, n_ref, t_ref, dx_ref, dn_ref, loss_ref, dxb_ref):
        @pl.when(pl.program_id(0) == 0)
        def _():
            dn_ref[...] = jnp.zeros_like(dn_ref)
            loss_ref[...] = jnp.zeros_like(loss_ref)

        xv = x_ref[...]
        nv = n_ref[...]
        r = lax.rsqrt(jnp.mean(xv * xv, axis=-1, keepdims=True) + RMS_EPS)
        xh = xv * r
        err = xh * nv - t_ref[...]
        row = jnp.mean(err * err, axis=-1, keepdims=True)
        loss_ref[...] += 0.5 * jnp.sum(row, axis=0, keepdims=True)
        dy = err * (1.0 / d)
        dn_ref[...] += jnp.sum(dy * xh, axis=0, keepdims=True)
        dxh = dy * nv
        dx = r * (dxh - xh * jnp.mean(dxh * xh, axis=-1, keepdims=True))
        dx_ref[...] = dx
        dxb_ref[...] = (0.5 * dx).astype(BF16)

    tile = pl.BlockSpec((TM, d), lambda i: (i, 0))
    return pl.pallas_call(
        body, name=name, grid=(t // TM,),
        in_specs=[tile, pl.BlockSpec((1, d), lambda i: (0, 0)), tile],
        out_specs=[tile, pl.BlockSpec((1, d), lambda i: (0, 0)), pl.BlockSpec((1, 1), lambda i: (0, 0)), tile],
        out_shape=[jax.ShapeDtypeStruct((t, d), F32), jax.ShapeDtypeStruct((1, d), F32), jax.ShapeDtypeStruct((1, 1), F32),
                   jax.ShapeDtypeStruct((t, d), BF16)],
        compiler_params=_cparams(1),
    )(x3, nf, tgt)


def _coords():
    return lax.axis_index("x"), lax.axis_index("y"), lax.axis_index("c")


def _window(ref, kind, idx, size):
    if kind == "col":
        return ref.at[:, pl.ds(pl.multiple_of(idx * size, LANE), size)]
    return ref.at[pl.ds(pl.multiple_of(idx * size, 8), size), :]


def _run_exchange(ex, name):
    n_in = len(ex.inputs)

    def body(*refs):
        ins, outs, sems = refs[:n_in], refs[n_in:n_in + len(ex.out_shape)], refs[n_in + len(ex.out_shape):]
        ex.start(ins, outs, sems)
        if ex.mid is not None:
            ex.mid(ins, outs, sems)
        ex.finish(ins, outs, sems)

    return pl.pallas_call(body, name=name, in_specs=[ANY] * n_in, out_specs=[ANY] * len(ex.out_shape),
                          out_shape=ex.out_shape, scratch_shapes=ex.scratch)(*ex.inputs)


def _join(exchanges):
    bounds = []
    i0 = o0 = s0 = 0
    for ex in exchanges:
        bounds.append((i0, o0, s0))
        i0, o0, s0 = i0 + len(ex.inputs), o0 + len(ex.out_shape), s0 + len(ex.scratch)

    def phase(which):
        def run(ins, outs, sems):
            for ex, (i, o, s) in zip(exchanges, bounds):
                fn = getattr(ex, which)
                if fn is not None:
                    fn(ins[i:i + len(ex.inputs)], outs[o:o + len(ex.out_shape)], sems[s:s + len(ex.scratch)])
        return run

    return _Exchange(sum((ex.inputs for ex in exchanges), []), sum((ex.out_shape for ex in exchanges), []),
                     sum((ex.scratch for ex in exchanges), []), phase("start"), phase("finish"),
                     phase("mid") if any(ex.mid is not None for ex in exchanges) else None)


def _gather_exchange(parts):
    n = len(parts)
    kinds = [kd for _, kd in parts]
    sizes = [a.shape[1] if kd == "col" else a.shape[0] for a, kd in parts]

    def plan(ins, outs, sems):
        send_sems, recv_sems, local_sems = sems
        x, y, c = _coords()
        me, sibling = (x, y, c), (x, y, 1 - c)
        chips = [(1 - x, y), (x, 1 - y), (1 - x, 1 - y)]

        def win(p, dev):
            return _window(outs[p], kinds[p], 4 * dev[0] + 2 * dev[1] + dev[2], sizes[p])

        def copy(p, k, block, to, src=None):
            return pltpu.make_async_remote_copy(
                src_ref=win(p, block) if src is None else src, dst_ref=win(p, block),
                send_sem=send_sems.at[p * 7 + k], recv_sem=recv_sems.at[p * 7 + k], device_id=to, device_id_type=MESH_ID)

        mine = [pltpu.make_async_copy(ins[p], win(p, me), local_sems.at[p]) for p in range(n)]
        first, arrived, passed, rest = [], [], [], []
        for p in range(n):
            first.append(copy(p, 0, me, sibling, src=ins[p]))
            first += [copy(p, 1 + q, me, (*chip, c), src=ins[p]) for q, chip in enumerate(chips)]
            rest.append(copy(p, 0, sibling, me))
            rest += [copy(p, 4 + q, (*chip, 1 - c), me) for q, chip in enumerate(chips)]
        for q, chip in enumerate(chips):
            for p in range(n):
                arrived.append(copy(p, 1 + q, (*chip, c), me))
                passed.append(copy(p, 4 + q, (*chip, c), sibling))
        return mine, first, arrived, passed, rest

    def start(ins, outs, sems):
        mine, first, _, _, _ = plan(ins, outs, sems)
        for cp in mine + first:
            cp.start()

    def mid(ins, outs, sems):
        _, _, arrived, passed, _ = plan(ins, outs, sems)
        for got, fwd in zip(arrived, passed):
            got.wait_recv()
            fwd.start()

    def finish(ins, outs, sems):
        mine, first, _, passed, rest = plan(ins, outs, sems)
        for cp in rest:
            cp.wait_recv()
        for cp in first + passed:
            cp.wait_send()
        for cp in mine:
            cp.wait()

    out_shape = [jax.ShapeDtypeStruct((a.shape[0], N_DEV * a.shape[1]) if kd == "col" else (N_DEV * a.shape[0], a.shape[1]),
                                      a.dtype) for a, kd in parts]
    scratch = [pltpu.SemaphoreType.DMA((7 * n,)), pltpu.SemaphoreType.DMA((7 * n,)), pltpu.SemaphoreType.DMA((n,))]
    return _Exchange([a for a, _ in parts], out_shape, scratch, start, finish, mid)


def _all_gather(parts, name):
    return _run_exchange(_gather_exchange(parts), name)


def _shard_shape(a, kd):
    return (a.shape[0], a.shape[1] // N_DEV) if kd == "col" else (a.shape[0] // N_DEV, a.shape[1])


def _symmetric_exchange(inputs, out_shape, n_copies, plan):
    def start(ins, outs, sems):
        for cp in plan(ins, outs, sems):
            cp.start()

    def finish(ins, outs, sems):
        copies = plan(ins, outs, sems)
        for cp in copies:
            cp.wait_recv()
        for cp in copies:
            cp.wait_send()

    scratch = [pltpu.SemaphoreType.DMA((n_copies,)), pltpu.SemaphoreType.DMA((n_copies,))]
    return _Exchange(inputs, out_shape, scratch, start, finish)


def _rs_pair_exchange(grads):
    n = len(grads)
    kinds = [kd for _, kd in grads]
    shapes = [_shard_shape(a, kd) for a, kd in grads]

    def plan(ins, outs, sems):
        send_sems, recv_sems = sems
        x, y, c = _coords()
        copies = []
        for p in range(n):
            size = shapes[p][1] if kinds[p] == "col" else shapes[p][0]
            for s in range(4):
                src = _window(ins[p], kinds[p], 2 * s + (1 - c), size)
                copies.append(pltpu.make_async_remote_copy(
                    src_ref=src, dst_ref=outs[p].at[s], send_sem=send_sems.at[4 * p + s], recv_sem=recv_sems.at[4 * p + s],
                    device_id=(x, y, 1 - c), device_id_type=MESH_ID))
        return copies

    return _symmetric_exchange([a for a, _ in grads], [jax.ShapeDtypeStruct((4,) + shapes[p], BF16) for p in range(n)],
                               4 * n, plan)


def _rs_chips_exchange(sums):
    n = len(sums)

    def plan(ins, outs, sems):
        send_sems, recv_sems = sems
        x, y, c = _coords()
        chips = [(1 - x, y), (x, 1 - y), (1 - x, 1 - y)]
        copies = []
        for p in range(n):
            for q, (cx, cy) in enumerate(chips):
                copies.append(pltpu.make_async_remote_copy(
                    src_ref=ins[p].at[2 * cx + cy], dst_ref=outs[p].at[q],
                    send_sem=send_sems.at[3 * p + q], recv_sem=recv_sems.at[3 * p + q],
                    device_id=(cx, cy, c), device_id_type=MESH_ID))
        return copies

    return _symmetric_exchange(list(sums), [jax.ShapeDtypeStruct((3,) + a.shape[1:], BF16) for a in sums], 3 * n, plan)


def _pair_sum(grad, kd, recv, core, name):
    _, r, cw = recv.shape
    tr = min(r, TM)

    def body(core_ref, g_ref, r_ref, o_ref):
        del core_ref
        o_ref[0] = (g_ref[...].astype(F32) + r_ref[0].astype(F32)).astype(BF16)

    if kd == "col":
        g_spec = pl.BlockSpec((tr, cw), lambda s, i, core_ref: (i, 2 * s + core_ref[0]))
    else:
        g_spec = pl.BlockSpec((tr, cw), lambda s, i, core_ref: ((2 * s + core_ref[0]) * (r // tr) + i, 0))
    grid_spec = pltpu.PrefetchScalarGridSpec(
        num_scalar_prefetch=1, grid=(4, r // tr),
        in_specs=[g_spec, pl.BlockSpec((1, tr, cw), lambda s, i, core_ref: (s, i, 0))],
        out_specs=pl.BlockSpec((1, tr, cw), lambda s, i, core_ref: (s, i, 0)))
    return pl.pallas_call(
        body, name=name, grid_spec=grid_spec, out_shape=jax.ShapeDtypeStruct(recv.shape, BF16),
        compiler_params=_cparams(2),
    )(core, grad, recv)


def _adam_math(w, g, m, v):
    m2 = B1 * m + (1.0 - B1) * g
    v2 = B2 * v + (1.0 - B2) * (g * g)
    m_hat = m2 / (1.0 - B1 ** STEP)
    v_hat = v2 / (1.0 - B2 ** STEP)
    delta = -LR * (m_hat / (jnp.sqrt(v_hat) + ADAM_EPS) + WD * w)
    return delta, m2, v2


def _chip_sum_adam(items, chip, tr, name, ride=None):
    r = items[0][1].shape[0]
    steps = r // tr
    n_parts = [len(parts) for parts, _, _, _ in items]
    r_in = 0 if ride is None else len(ride.inputs)
    r_out = 0 if ride is None else len(ride.out_shape)
    n_in = sum(2 * k + 3 for k in n_parts)
    n_out = 4 * len(items)

    def body(chip_ref, *refs):
        del chip_ref
        ins, refs = refs[:n_in], refs[n_in:]
        r_ins, refs = refs[:r_in], refs[r_in:]
        outs, refs = refs[:n_out], refs[n_out:]
        r_outs, sems = refs[:r_out], refs[r_out:]
        i = pl.program_id(0)
        if ride is not None:
            @pl.when(i == 0)
            def _():
                ride.start(r_ins, r_outs, sems)

        pos = 0
        for q, (k, (_, w, _, _)) in enumerate(zip(n_parts, items)):
            cols = []
            for _ in range(k):
                p_ref, c_ref = ins[pos], ins[pos + 1]
                pos += 2
                cols.append(p_ref[0].astype(F32) + c_ref[0].astype(F32) + c_ref[1].astype(F32) + c_ref[2].astype(F32))
            g = (cols[0] if k == 1 else jnp.concatenate(cols, axis=1))[:, :w.shape[1]]
            w_ref, m_ref, v_ref = ins[pos:pos + 3]
            pos += 3
            delta, m2, v2 = _adam_math(w_ref[...], g, m_ref[...], v_ref[...])
            outs[4 * q][...] = g
            outs[4 * q + 1][...] = delta
            outs[4 * q + 2][...] = m2
            outs[4 * q + 3][...] = v2

        if ride is not None:
            @pl.when(i == steps - 1)
            def _():
                ride.finish(r_ins, r_outs, sems)

    in_specs, args, out_specs, out_shape = [], [], [], []
    for parts, w, m, v in items:
        for psum, recv in parts:
            pc = psum.shape[2]
            in_specs += [pl.BlockSpec((1, tr, pc), lambda i, chip_ref: (chip_ref[0], i, 0)),
                         pl.BlockSpec((3, tr, pc), lambda i, chip_ref: (0, i, 0))]
            args += [psum, recv]
        loc = pl.BlockSpec((tr, w.shape[1]), lambda i, chip_ref: (i, 0))
        in_specs += [loc] * 3
        args += [w, m, v]
        out_specs += [loc] * 4
        out_shape += [jax.ShapeDtypeStruct(w.shape, F32)] * 4
    grid_spec = pltpu.PrefetchScalarGridSpec(
        num_scalar_prefetch=1, grid=(steps,), in_specs=in_specs + [ANY] * r_in, out_specs=out_specs + [ANY] * r_out,
        scratch_shapes=[] if ride is None else ride.scratch)
    res = pl.pallas_call(
        body, name=name, grid_spec=grid_spec, out_shape=out_shape + ([] if ride is None else ride.out_shape),
        compiler_params=_cparams(1),
    )(chip, *args, *([] if ride is None else ride.inputs))
    return res if ride is None else (res[:n_out], res[n_out:])


def _small_allreduce_adam(partials, params, moms, vels, plain, name, ride=None):
    n, n_plain = len(partials), len(plain)
    summed = list(partials) + list(plain)
    row0 = []
    rows = 0
    for a in summed:
        if a.shape[0] >= 8:
            rows = _pad_to(rows, 8)
        row0.append(rows)
        rows += a.shape[0]
    rows = _pad_to(rows, 8)
    width = max(a.shape[1] for a in summed)
    r_in = 0 if ride is None else len(ride.inputs)
    r_out = 0 if ride is None else len(ride.out_shape)
    n_out = 4 * n + n_plain

    def body(*refs):
        w_in, m_in, v_in = refs[0:n], refs[n:2 * n], refs[2 * n:3 * n]
        g_in, refs = refs[3 * n:4 * n + n_plain], refs[4 * n + n_plain:]
        r_ins, refs = refs[:r_in], refs[r_in:]
        outs, refs = refs[:n_out], refs[n_out:]
        r_outs, refs = refs[:r_out], refs[r_out:]
        pair, chips, send_sems, recv_sems = refs[:4]
        if ride is not None:
            ride.start(r_ins, r_outs, refs[4:])
        x, y, c = _coords()
        chip = 2 * x + y
        pair[c] = jnp.zeros((rows, width), F32)
        for p, a in enumerate(summed):
            r, cw = a.shape
            pair[c, row0[p]:row0[p] + r, 0:cw] = g_in[p][...]
        swap = pltpu.make_async_remote_copy(src_ref=pair.at[c], dst_ref=pair.at[c], send_sem=send_sems.at[0],
                                            recv_sem=recv_sems.at[0], device_id=(x, y, 1 - c), device_id_type=MESH_ID)
        swap.start()
        swap.wait_recv()
        swap.wait_send()
        chips[chip] = pair[0] + pair[1]
        copies = [pltpu.make_async_remote_copy(
            src_ref=chips.at[chip], dst_ref=chips.at[chip], send_sem=send_sems.at[1 + q], recv_sem=recv_sems.at[1 + q],
            device_id=(cx, cy, c), device_id_type=MESH_ID) for q, (cx, cy) in enumerate([(1 - x, y), (x, 1 - y), (1 - x, 1 - y)])]
        for cp in copies:
            cp.start()
        for cp in copies:
            cp.wait_recv()
        for cp in copies:
            cp.wait_send()
        for p, a in enumerate(summed):
            r, cw = a.shape
            g = chips[0, row0[p]:row0[p] + r, 0:cw]
            for q in range(1, 4):
                g = g + chips[q, row0[p]:row0[p] + r, 0:cw]
            if p >= n:
                outs[4 * n + p - n][...] = g
                continue
            delta, m2, v2 = _adam_math(w_in[p][...], g, m_in[p][...], v_in[p][...])
            outs[4 * p][...] = g
            outs[4 * p + 1][...] = delta
            outs[4 * p + 2][...] = m2
            outs[4 * p + 3][...] = v2
        if ride is not None:
            ride.finish(r_ins, r_outs, refs[4:])

    out_shape = []
    for a in partials:
        out_shape += [jax.ShapeDtypeStruct(a.shape, F32)] * 4
    out_shape += [jax.ShapeDtypeStruct(a.shape, F32) for a in plain]
    res = pl.pallas_call(
        body, name=name, in_specs=[VMEM_SPEC] * (4 * n + n_plain) + [ANY] * r_in,
        out_specs=[VMEM_SPEC] * n_out + [ANY] * r_out, out_shape=out_shape + ([] if ride is None else ride.out_shape),
        scratch_shapes=[pltpu.VMEM((2, rows, width), F32), pltpu.VMEM((4, rows, width), F32),
                        pltpu.SemaphoreType.DMA((4,)), pltpu.SemaphoreType.DMA((4,))] + ([] if ride is None else ride.scratch),
    )(*params, *moms, *vels, *partials, *plain, *([] if ride is None else ride.inputs))
    return res if ride is None else (res[:n_out], res[n_out:])


def _local_step(xf, tgt, nseq, seq, cols1_all, later, small_w, core, small_step):
    d = xf.shape[1]
    n1, n2, gain, pool_w, pool_scale, n3, nf = small_w
    tf = 2 * cols1_all.shape[1] // N_DEV
    consts = _retention_constants()
    cos_t, sin_t = _rotary_tables(seq)
    wp_b = pool_w.astype(BF16)

    def pair_sums(grads, recv, names):
        return [_pair_sum(g, kd, r, core, "pair_sum_" + nm) for (g, kd), r, nm in zip(grads, recv, names)]

    def riding(host):
        return _gather_exchange(later[host])

    both = lambda first, second: _join([_rs_chips_exchange(first), _rs_pair_exchange([second])])

    (h1, b1, sil1, dsil1, s1), (d1_all, win_all, wout_all, gate2_all, up2_all) = _ffn_act(
        xf, n1, (cols1_all, 0), (cols1_all, 1), "ffn1_act", ride=riding("ffn1_act"))
    x1, (d2_all,) = _ffn_down(s1, xf, d1_all, "ffn1_down", ride=riding("ffn1_down"))
    h2, qs, kr, vv, gg, uu = _mix_in(x1, n2, win_all, cos_t, sin_t, seq, "mix_in")
    x2, mix, oo, pooled, states = _mix_core_fwd(qs, kr, vv, gg, uu, x1, consts, gain, wp_b, pool_scale, wout_all,
                                                 nseq, seq, "mix_core_fwd")
    (x3, h3, b3, sil3, dsil3), (cols2_t, d2_t, win_t, wout_t) = _ffn_fwd(
        x2, n3, (gate2_all, 0), (up2_all, 0), d2_all, "ffn2_fwd", ride=riding("ffn2_fwd"))
    dx3, dnf, loss_part, dx3b = _loss_head(x3, nf, tgt, "loss_head")
    out = {}

    (da3, db3, g_wd2), (cols1_t, d1_t) = _ffn_bwd_act(dx3b, b3, sil3, dsil3, d2_t, "ffn2_bwd_act",
                                                      ride=riding("ffn2_bwd_act"))
    names2 = ["ffn2_gate", "ffn2_up", "ffn2_down"]
    grads2 = [(_wgrad(da3, h3, 1.0, tf, d, "wgrad_gate2"), "row"), (_wgrad(db3, h3, 1.0, tf, d, "wgrad_up2"), "row"),
              (g_wd2, "row")]
    (dx2, dn3), recv2 = _ffn_bwd_in(da3, db3, dx3, x2, n3, cols2_t, 0, "ffn2_bwd_in", ride=_rs_pair_exchange(grads2))
    sums2 = pair_sums(grads2, recv2, names2)
    (dp, dx2b, dgain, dscale, dwp), crecv2 = _mix_core_bwd(
        dx2, qs, kr, vv, gg, oo, pooled, states, consts, gain, wp_b, pool_scale, wout_t, cos_t, sin_t, nseq, seq,
        "mix_core_bwd", ride=_rs_chips_exchange(sums2))
    out.update({nm: [(s, r)] for nm, s, r in zip(names2, sums2, crecv2)})

    names_m = ["w_in", "w_out"]
    grads_m = [(_wgrad(h2, dp, 1.0, d, d, "wgrad_in"), "col"), (_wgrad(mix, dx2b, 1.0, d, d, "wgrad_out"), "row")]
    (dx1, dn2, dx1b), recv_m = _mix_in_bwd(dp, dx2, x1, n2, win_t, "mix_in_bwd", ride=_rs_pair_exchange(grads_m))
    sums_m = pair_sums(grads_m, recv_m, names_m)
    (da1, db1, g_wd1), crecv_m = _ffn_bwd_act(dx1b, b1, sil1, dsil1, d1_t, "ffn1_bwd_act", ride=_rs_chips_exchange(sums_m))
    out.update({nm: [(s, r)] for nm, s, r in zip(names_m, sums_m, crecv_m)})

    dx0, dn1 = _ffn_bwd_in(da1, db1, dx1, xf, n1, cols1_t, 0, "ffn1_bwd_in")
    g_down = (g_wd1, "row")
    g_gate, recv_d = _wgrad(da1, h1, 1.0, tf, d, "wgrad_gate1", ride=_rs_pair_exchange([g_down]))
    g_gate = (g_gate, "row")
    sum_d = pair_sums([g_down], recv_d, ["ffn1_down"])
    g_lo, (crecv_d, recv_g) = _wgrad(db1, h1, 1.0, tf, d // 2, "wgrad_up1_lo", ride=both(sum_d, g_gate), b_cols=(0, 1))
    g_lo = (g_lo, "row")
    sum_g = pair_sums([g_gate], [recv_g], ["ffn1_gate"])
    g_hi, (crecv_g, recv_lo) = _wgrad(db1, h1, 1.0, tf, d // 2, "wgrad_up1_hi", ride=both(sum_g, g_lo), b_cols=(1, 1))
    g_hi = (g_hi, "row")
    sum_lo = pair_sums([g_lo], [recv_lo], ["ffn1_up_lo"])
    small_out, (crecv_lo, recv_hi) = small_step((dn1, dn2, dgain, dwp, dscale, dn3, dnf), loss_part, both(sum_lo, g_hi))
    sum_hi = pair_sums([g_hi], [recv_hi], ["ffn1_up_hi"])
    out.update({"ffn1_gate": [(sum_g[0], crecv_g)], "ffn1_down": [(sum_d[0], crecv_d)],
                "ffn1_up": [(sum_lo[0], crecv_lo), (sum_hi[0], None)]})
    return small_out[-1], dx0, out, small_out[:-1], _rs_chips_exchange(sum_hi)


def kernel(x, norm_ffn1, ffn1_gate, ffn1_up, ffn1_down, norm_mix, w_in, ret_gn_gain, pool_w, pool_scale, w_out, norm_ffn2, ffn2_gate, ffn2_up, ffn2_down, norm_final, loss_target, m_norm_ffn1, m_ffn1_gate, m_ffn1_up, m_ffn1_down, m_norm_mix, m_w_in, m_ret_gn_gain, m_pool_w, m_pool_scale, m_w_out, m_norm_ffn2, m_ffn2_gate, m_ffn2_up, m_ffn2_down, m_norm_final, v_norm_ffn1, v_ffn1_gate, v_ffn1_up, v_ffn1_down, v_norm_mix, v_w_in, v_ret_gn_gain, v_pool_w, v_pool_scale, v_w_out, v_norm_ffn2, v_ffn2_gate, v_ffn2_up, v_ffn2_down, v_norm_final):
    nseq, seq, d = x.shape
    t = nseq * seq
    f_loc = ffn1_gate.shape[2]
    f_pad = _pad_to(f_loc, LANE)
    xf = x.reshape(t, d)
    tgt = loss_target.reshape(t, d)
    core = lax.axis_index("c").astype(jnp.int32).reshape(1)
    chip = (2 * lax.axis_index("x") + lax.axis_index("y")).astype(jnp.int32).reshape(1)

    colp = lambda w: jnp.pad(w[0].astype(BF16), ((0, 0), (0, f_pad - f_loc)))
    rowp = lambda w: jnp.pad(w[0].astype(BF16), ((0, f_pad - f_loc), (0, 0)))
    gate2, up2 = colp(ffn2_gate), colp(ffn2_up)
    cols1 = jnp.concatenate([colp(ffn1_gate), colp(ffn1_up)], axis=0)
    cols2_t = jnp.concatenate([gate2.T, up2.T], axis=1)
    (cols1_all,) = _all_gather([(cols1, "col")], "all_gather_ffn1")
    d1_loc, d2_loc, win_loc, wout_loc = rowp(ffn1_down), rowp(ffn2_down), w_in[0].astype(BF16), w_out[0].astype(BF16)
    later = {"ffn1_act": [(d1_loc, "row"), (win_loc, "col"), (wout_loc, "row"), (gate2, "col"), (up2, "col")],
             "ffn1_down": [(d2_loc, "row")],
             "ffn2_fwd": [(cols2_t, "row"), (d2_loc.T, "col"), (win_loc.T, "row"), (wout_loc.T, "col")],
             "ffn2_bwd_act": [(cols1.T, "row"), (d1_loc.T, "col")]}

    flat = lambda a: a.reshape(pool_w.size // d, d)
    params = [norm_ffn1, norm_mix, ret_gn_gain, flat(pool_w), pool_scale, norm_ffn2, norm_final.reshape(1, d)]
    moms = [m_norm_ffn1, m_norm_mix, m_ret_gn_gain, flat(m_pool_w), m_pool_scale, m_norm_ffn2, m_norm_final.reshape(1, d)]
    vels = [v_norm_ffn1, v_norm_mix, v_ret_gn_gain, flat(v_pool_w), v_pool_scale, v_norm_ffn2, v_norm_final.reshape(1, d)]

    def small_step(parts, loss_part, ride):
        dn1, dn2, dgain, dwp, dscale, dn3, dnf = parts
        return _small_allreduce_adam([dn1, dn2, dgain, flat(dwp), dscale, dn3, dnf], params, moms, vels, [loss_part],
                                     "small_allreduce_adam", ride)

    small_w = (norm_ffn1, norm_mix, ret_gn_gain, pool_w[0], pool_scale, norm_ffn2, norm_final.reshape(1, d))
    loss_sum, dx0, reduced, small_out, pending = _local_step(xf, tgt, nseq, seq, cols1_all, later, small_w, core,
                                                             small_step)

    local = {"ffn1_gate": (ffn1_gate, m_ffn1_gate, v_ffn1_gate), "ffn1_up": (ffn1_up, m_ffn1_up, v_ffn1_up),
             "ffn1_down": (ffn1_down, m_ffn1_down, v_ffn1_down), "w_in": (w_in, m_w_in, v_w_in),
             "w_out": (w_out, m_w_out, v_w_out), "ffn2_gate": (ffn2_gate, m_ffn2_gate, v_ffn2_gate),
             "ffn2_up": (ffn2_up, m_ffn2_up, v_ffn2_up), "ffn2_down": (ffn2_down, m_ffn2_down, v_ffn2_down)}
    flip = lambda nm: nm.endswith("gate") or nm.endswith("up")

    def item(nm):
        view = (lambda a: a[0].T) if flip(nm) else (lambda a: a[0])
        w, m, v = local[nm]
        return reduced[nm], view(w), view(m), view(v)

    big = {}

    def keep(names, res):
        for q, nm in enumerate(names):
            big[nm] = tuple((a.T if flip(nm) else a)[None] for a in res[4 * q:4 * q + 4])

    second = ["ffn2_gate", "ffn2_up", "ffn2_down"]
    res, (last_recv,) = _chip_sum_adam([item(nm) for nm in second], chip, item(second[0])[1].shape[0] // 2, "adam_ffn2",
                                       ride=pending)
    keep(second, res)
    reduced["ffn1_up"][-1] = (reduced["ffn1_up"][-1][0], last_recv)
    first = ["ffn1_gate", "ffn1_up", "ffn1_down"]
    keep(first, _chip_sum_adam([item(nm) for nm in first], chip, item(first[0])[1].shape[0] // 2, "adam_ffn1"))
    for nm in ["w_in", "w_out"]:
        keep([nm], _chip_sum_adam([item(nm)], chip, min(item(nm)[1].shape[0], TM), "adam_" + nm))

    small_names = ["norm_ffn1", "norm_mix", "ret_gn_gain", "pool_w", "pool_scale", "norm_ffn2", "norm_final"]
    shapes = [norm_ffn1.shape, norm_mix.shape, ret_gn_gain.shape, pool_w.shape, pool_scale.shape, norm_ffn2.shape,
              norm_final.shape]
    small = {nm: tuple(small_out[4 * p + q].reshape(shapes[p]) for q in range(4)) for p, nm in enumerate(small_names)}

    loss = loss_sum[0, 0]
    order = ["norm_ffn1", "ffn1_gate", "ffn1_up", "ffn1_down", "norm_mix", "w_in", "ret_gn_gain", "pool_w", "pool_scale",
             "w_out", "norm_ffn2", "ffn2_gate", "ffn2_up", "ffn2_down", "norm_final"]
    both = {**big, **small}
    outs = [loss, dx0.reshape(nseq, seq, d)]
    for q in range(4):
        outs += [both[nm][q] for nm in order]
    return tuple(outs)
```

```python
import numpy as np
import jax
import jax.numpy as jnp
from jax import lax
from jax.experimental import pallas as pl
from jax.experimental.pallas import tpu as pltpu

F32, BF16 = jnp.float32, jnp.bfloat16
MESH_ID = pl.DeviceIdType.MESH
ANY = pl.BlockSpec(memory_space=pl.ANY)
VMEM_SPEC = pl.BlockSpec(memory_space=pltpu.VMEM)

N_DEV = 8
RMS_EPS = 1e-6
GN_EPS = 1e-5
HEADS, DK, DV = 4, 64, 128
QK_W, V_W, POOL_W = HEADS * DK, HEADS * DV, 512
WINDOWS = (2, 4, 8, 16)
GC = POOL_W // len(WINDOWS)
CHUNK = 64
BLK = 4 * CHUNK
HALO = 16
ROPE_BASE = 10000.0
LR, B1, B2, ADAM_EPS, WD, STEP = 0.001, 0.9, 0.999, 1e-08, 0.01, 10
LANE = 128
TM = 512
FFN_TM = 1024
FFN_FWD_TF = 768
WGRAD_TT = 4096
VMEM_LIMIT = 56 * 1024 * 1024


def _cparams(n_axes):
    return pltpu.CompilerParams(dimension_semantics=("arbitrary",) * n_axes, vmem_limit_bytes=VMEM_LIMIT)


class _Exchange:
    def __init__(self, inputs, out_shape, scratch, start, finish, mid=None):
        self.inputs, self.out_shape, self.scratch = list(inputs), list(out_shape), list(scratch)
        self.start, self.finish, self.mid = start, finish, mid


def _pallas(body, name, grid, in_specs, out_specs, out_shape, scratch_shapes, args, ride=None):
    n_axes = len(grid)
    if ride is None:
        return pl.pallas_call(body, name=name, grid=grid, in_specs=in_specs, out_specs=out_specs, out_shape=out_shape,
                              scratch_shapes=scratch_shapes, compiler_params=_cparams(n_axes))(*args)
    n_in, n_out, n_scr = len(in_specs), len(out_specs), len(scratch_shapes)
    r_in, r_out = len(ride.inputs), len(ride.out_shape)

    def hosted(*refs):
        ins, refs = refs[:n_in], refs[n_in:]
        r_ins, refs = refs[:r_in], refs[r_in:]
        outs, refs = refs[:n_out], refs[n_out:]
        r_outs, refs = refs[:r_out], refs[r_out:]
        scr, sems = refs[:n_scr], refs[n_scr:]
        ids = [pl.program_id(a) for a in range(n_axes)]
        first, last, inner0 = ids[0] == 0, ids[0] == grid[0] - 1, None
        for a in range(1, n_axes):
            first = first & (ids[a] == 0)
            last = last & (ids[a] == grid[a] - 1)
            inner0 = (ids[a] == 0) if inner0 is None else inner0 & (ids[a] == 0)

        @pl.when(first)
        def _():
            ride.start(r_ins, r_outs, sems)

        if ride.mid is not None:
            at_mid = ids[0] == grid[0] - 1
            if inner0 is not None:
                at_mid = at_mid & inner0

            @pl.when(at_mid)
            def _():
                ride.mid(r_ins, r_outs, sems)

        body(*ins, *outs, *scr)

        @pl.when(last)
        def _():
            ride.finish(r_ins, r_outs, sems)

    res = pl.pallas_call(
        hosted, name=name, grid=grid, in_specs=list(in_specs) + [ANY] * r_in, out_specs=list(out_specs) + [ANY] * r_out,
        out_shape=list(out_shape) + ride.out_shape, scratch_shapes=list(scratch_shapes) + ride.scratch,
        compiler_params=_cparams(n_axes))(*args, *ride.inputs)
    return res[:n_out], res[n_out:]


def _dot(a, b):
    return jnp.dot(a, b, preferred_element_type=F32)


def _dot_nt(a, b):
    return lax.dot_general(a, b, (((1,), (1,)), ((), ())), preferred_element_type=F32)


def _dot_tn(a, b):
    return lax.dot_general(a, b, (((0,), (0,)), ((), ())), preferred_element_type=F32)


def _sigmoid(x):
    return 0.5 * jnp.tanh(0.5 * x) + 0.5


def _pad_to(n, m):
    return (n + m - 1) // m * m


def _retention_constants():
    gamma = (1.0 - 2.0 ** (-5.0 - np.arange(HEADS, dtype=np.float32))).astype(np.float32)
    log_g = np.log(gamma).astype(np.float32)
    i = np.arange(BLK)
    diff = (i[:, None] - i[None, :]).astype(np.float32)
    same = (i[:, None] // CHUNK) == (i[None, :] // CHUNK)
    earlier = (i[None, :] // CHUNK) < (i[:, None] // CHUNK)
    mask = np.zeros((HEADS, BLK, BLK), np.float32)
    for h in range(HEADS):
        dec_abs = np.exp(log_g[h] * np.abs(diff)).astype(np.float32)
        dec = np.exp(log_g[h] * diff * earlier).astype(np.float32)
        mask[h] = np.where(same, dec_abs, np.where(earlier, dec, 0.0))
    dq = np.zeros((BLK, V_W), np.float32)
    dk = np.zeros((BLK, QK_W), np.float32)
    gbd = np.zeros((QK_W, V_W), np.float32)
    for h in range(HEADS):
        dq[:, h * DV:(h + 1) * DV] = np.exp(log_g[h] * (i + 1.0)).astype(np.float32)[:, None]
        dk[:, h * DK:(h + 1) * DK] = np.exp(log_g[h] * (BLK - 1.0 - i)).astype(np.float32)[:, None]
        gbd[h * DK:(h + 1) * DK, h * DV:(h + 1) * DV] = np.exp(log_g[h] * np.float32(BLK))
    bd = (gbd > 0).astype(np.float32)
    return jnp.asarray(mask), jnp.asarray(dq), jnp.asarray(dk), jnp.asarray(gbd), jnp.asarray(bd)


def _rotary_tables(seq):
    half = DK // 2
    freqs = ROPE_BASE ** (-jnp.arange(half, dtype=F32) * 2.0 / DK)
    ang = jnp.arange(seq, dtype=F32)[:, None] * freqs[None, :]
    cos, sin = jnp.cos(ang), jnp.sin(ang)
    cos_t = jnp.tile(jnp.concatenate([cos, cos], axis=1), (1, HEADS))
    sin_t = jnp.tile(jnp.concatenate([-sin, sin], axis=1), (1, HEADS))
    return cos_t, sin_t


def _swap_halves(x):
    lane = lax.broadcasted_iota(jnp.int32, (1, QK_W), 1)
    first = (lane & (DK - 1)) < DK // 2
    return jnp.where(first, pltpu.roll(x, QK_W - DK // 2, 1), pltpu.roll(x, DK // 2, 1))


def _head_mask(h):
    lane = lax.broadcasted_iota(jnp.int32, (1, QK_W), 1)
    return (lane >= h * DK) & (lane < (h + 1) * DK)


def _ffn_fwd(x, n, gate, up, wd, name, ride=None):
    t, d = x.shape
    (wg, gq), (wu, uq) = gate, up
    fp = wg.shape[1]
    tm = min(t, FFN_TM)
    tf = FFN_FWD_TF
    nj = fp // tf

    def body(x_ref, n_ref, wg_ref, wu_ref, wd_ref, xo_ref, h_ref, b_ref, sil_ref, dsil_ref, acc_ref):
        j = pl.program_id(1)

        @pl.when(j == 0)
        def _():
            xv = x_ref[...]
            r = lax.rsqrt(jnp.mean(xv * xv, axis=-1, keepdims=True) + RMS_EPS)
            h_ref[...] = (xv * r * n_ref[...]).astype(BF16)
            acc_ref[...] = jnp.zeros_like(acc_ref)

        h = h_ref[...]
        a = _dot(h, wg_ref[...])
        b = _dot(h, wu_ref[...])
        sg = _sigmoid(a)
        sil = a * sg
        b_ref[...] = b.astype(BF16)
        sil_ref[...] = sil.astype(BF16)
        dsil_ref[...] = (sg + sil * (1.0 - sg)).astype(BF16)
        acc_ref[...] += _dot((sil * b).astype(BF16), wd_ref[...])

        @pl.when(j == nj - 1)
        def _():
            xo_ref[...] = x_ref[...] + 0.5 * acc_ref[...]

    act = pl.BlockSpec((tm, tf), lambda i, j: (i, j))
    return _pallas(
        body, name, (t // tm, nj),
        [pl.BlockSpec((tm, d), lambda i, j: (i, 0)), pl.BlockSpec((1, d), lambda i, j: (0, 0)),
         pl.BlockSpec((d, tf), lambda i, j: (gq, j)), pl.BlockSpec((d, tf), lambda i, j: (uq, j)),
         pl.BlockSpec((tf, d), lambda i, j: (j, 0))],
        [pl.BlockSpec((tm, d), lambda i, j: (i, 0)), pl.BlockSpec((tm, d), lambda i, j: (i, 0)), act, act, act],
        [jax.ShapeDtypeStruct((t, d), F32), jax.ShapeDtypeStruct((t, d), BF16)] + [jax.ShapeDtypeStruct((t, fp), BF16)] * 3,
        [pltpu.VMEM((tm, d), F32)], (x, n, wg, wu, wd), ride)


def _ffn_act(x, n, gate, up, name, ride=None):
    t, d = x.shape
    (wg, gq), (wu, uq) = gate, up
    fp = wg.shape[1]
    tm = min(t, FFN_TM)
    tf = 2 * fp // N_DEV
    nj = fp // tf

    def body(x_ref, n_ref, wg_ref, wu_ref, h_ref, b_ref, sil_ref, dsil_ref, s_ref):
        @pl.when(pl.program_id(1) == 0)
        def _():
            xv = x_ref[...]
            r = lax.rsqrt(jnp.mean(xv * xv, axis=-1, keepdims=True) + RMS_EPS)
            h_ref[...] = (xv * r * n_ref[...]).astype(BF16)

        h = h_ref[...]
        a = _dot(h, wg_ref[...])
        b = _dot(h, wu_ref[...])
        sg = _sigmoid(a)
        sil = a * sg
        b_ref[...] = b.astype(BF16)
        sil_ref[...] = sil.astype(BF16)
        dsil_ref[...] = (sg + sil * (1.0 - sg)).astype(BF16)
        s_ref[...] = (sil * b).astype(BF16)

    act = pl.BlockSpec((tm, tf), lambda i, j: (i, j))
    return _pallas(
        body, name, (t // tm, nj),
        [pl.BlockSpec((tm, d), lambda i, j: (i, 0)), pl.BlockSpec((1, d), lambda i, j: (0, 0)),
         pl.BlockSpec((d, tf), lambda i, j: (gq, j)), pl.BlockSpec((d, tf), lambda i, j: (uq, j))],
        [pl.BlockSpec((tm, d), lambda i, j: (i, 0)), act, act, act, act],
        [jax.ShapeDtypeStruct((t, d), BF16)] + [jax.ShapeDtypeStruct((t, fp), BF16)] * 4,
        [], (x, n, wg, wu), ride)


def _ffn_down(s, x, wd, name, ride=None):
    t, d = x.shape
    fp = wd.shape[0]
    tm = min(t, FFN_TM)

    def body(s_ref, x_ref, wd_ref, xo_ref):
        xo_ref[...] = x_ref[...] + 0.5 * _dot(s_ref[...], wd_ref[...])

    row = pl.BlockSpec((tm, d), lambda i: (i, 0))
    res = _pallas(body, name, (t // tm,), [pl.BlockSpec((tm, fp), lambda i: (i, 0)), row, pl.BlockSpec((fp, d), lambda i: (0, 0))],
                  [row], [jax.ShapeDtypeStruct((t, d), F32)], [], (s, x, wd), ride)
    return res[0] if ride is None else (res[0][0], res[1])


def _ffn_bwd_act(dxob, b, sil, dsil, wd_t, name, ride=None):
    t, d = dxob.shape
    fp = wd_t.shape[1]
    tm = min(t, FFN_TM)
    tf = 2 * fp // N_DEV
    ni = t // tm

    def body(dx_ref, b_ref, sil_ref, dsil_ref, wd_ref, da_ref, db_ref, gd_ref, acc_ref):
        i = pl.program_id(1)

        @pl.when(i == 0)
        def _():
            acc_ref[...] = jnp.zeros_like(acc_ref)

        dxv = dx_ref[...]
        bv, sv = b_ref[...].astype(F32), sil_ref[...].astype(F32)
        ds = _dot(dxv, wd_ref[...])
        da_ref[...] = (ds * bv * dsil_ref[...].astype(F32)).astype(BF16)
        db_ref[...] = (ds * sv).astype(BF16)
        acc_ref[...] += _dot_tn((sv * bv).astype(BF16), dxv)

        @pl.when(i == ni - 1)
        def _():
            gd_ref[...] = acc_ref[...].astype(BF16)

    act = pl.BlockSpec((tm, tf), lambda c, i: (i, c))
    return _pallas(
        body, name, (fp // tf, ni),
        [pl.BlockSpec((tm, d), lambda c, i: (i, 0)), act, act, act, pl.BlockSpec((d, tf), lambda c, i: (0, c))],
        [act, act, pl.BlockSpec((tf, d), lambda c, i: (c, 0))],
        [jax.ShapeDtypeStruct((t, fp), BF16), jax.ShapeDtypeStruct((t, fp), BF16), jax.ShapeDtypeStruct((fp, d), BF16)],
        [pltpu.VMEM((tf, d), F32)], (dxob, b, sil, dsil, wd_t), ride)


def _ffn_bwd_in(da, db, dxo, x, n, cols_t, gq, name, ride=None):
    t, d = x.shape
    fp = cols_t.shape[0]
    tm = min(t, FFN_TM)
    tf = fp // 3
    nj = fp // tf

    def body(da_ref, db_ref, dxo_ref, x_ref, n_ref, wg_ref, wu_ref, dx_ref, dn_ref, acc_ref):
        i, j = pl.program_id(0), pl.program_id(1)

        @pl.when((i == 0) & (j == 0))
        def _():
            dn_ref[...] = jnp.zeros_like(dn_ref)

        @pl.when(j == 0)
        def _():
            acc_ref[...] = jnp.zeros_like(acc_ref)

        acc_ref[...] += _dot(da_ref[...], wg_ref[...]) + _dot(db_ref[...], wu_ref[...])

        @pl.when(j == nj - 1)
        def _():
            xv = x_ref[...]
            r = lax.rsqrt(jnp.mean(xv * xv, axis=-1, keepdims=True) + RMS_EPS)
            xh = xv * r
            dh = acc_ref[...]
            dn_ref[...] += jnp.sum(dh * xh, axis=0, keepdims=True)
            dhn = dh * n_ref[...]
            dx_ref[...] = dxo_ref[...] + r * (dhn - xh * jnp.mean(dhn * xh, axis=-1, keepdims=True))

    act = pl.BlockSpec((tm, tf), lambda i, j: (i, j))
    row = pl.BlockSpec((tm, d), lambda i, j: (i, 0))
    return _pallas(
        body, name, (t // tm, nj),
        [act, act, row, row, pl.BlockSpec((1, d), lambda i, j: (0, 0)),
         pl.BlockSpec((tf, d), lambda i, j: (j, gq)), pl.BlockSpec((tf, d), lambda i, j: (j, gq + 1))],
        [row, pl.BlockSpec((1, d), lambda i, j: (0, 0))],
        [jax.ShapeDtypeStruct((t, d), F32), jax.ShapeDtypeStruct((1, d), F32)],
        [pltpu.VMEM((tm, d), F32)], (da, db, dxo, x, n, cols_t, cols_t), ride)


def _wgrad(a, b, scale, tk, tn, name, ride=None, b_cols=None):
    t, k = a.shape
    q0, nq = (0, b.shape[1] // tn) if b_cols is None else b_cols
    n = nq * tn
    tt = min(t, WGRAD_TT)
    nt = t // tt

    def body(a_ref, b_ref, o_ref, acc_ref):
        s = pl.program_id(2)

        @pl.when(s == 0)
        def _():
            acc_ref[...] = jnp.zeros_like(acc_ref)

        acc_ref[...] += _dot_tn(a_ref[...], b_ref[...])

        @pl.when(s == nt - 1)
        def _():
            o_ref[...] = (scale * acc_ref[...]).astype(BF16)

    res = _pallas(
        body, name, (k // tk, n // tn, nt),
        [pl.BlockSpec((tt, tk), lambda p, q, s: (s, p)), pl.BlockSpec((tt, tn), lambda p, q, s: (s, q + q0))],
        [pl.BlockSpec((tk, tn), lambda p, q, s: (p, q))], [jax.ShapeDtypeStruct((k, n), BF16)],
        [pltpu.VMEM((tk, tn), F32)], (a, b), ride)
    return res[0] if ride is None else (res[0][0], res[1])


def _mix_in(x, n, w_in, cos_t, sin_t, seq, name):
    t, d = x.shape
    per_seq = seq // TM

    def body(x_ref, n_ref, w_ref, c_ref, s_ref, h_ref, q_ref, k_ref, v_ref, g_ref, u_ref):
        xv = x_ref[...]
        r = lax.rsqrt(jnp.mean(xv * xv, axis=-1, keepdims=True) + RMS_EPS)
        h = (xv * r * n_ref[...]).astype(BF16)
        h_ref[...] = h
        p = _dot(h, w_ref[...])
        c, s = c_ref[...], s_ref[...]
        q = p[:, :QK_W]
        k = p[:, QK_W:2 * QK_W]
        q_ref[...] = ((q * c + _swap_halves(q) * s) * (DK ** -0.5)).astype(BF16)
        k_ref[...] = (k * c + _swap_halves(k) * s).astype(BF16)
        v_ref[...] = p[:, 2 * QK_W:2 * QK_W + V_W].astype(BF16)
        g_ref[...] = p[:, 2 * QK_W + V_W:2 * QK_W + 2 * V_W]
        u_ref[...] = p[:, 2 * QK_W + 2 * V_W:]

    tile = lambda w: pl.BlockSpec((TM, w), lambda i: (i, 0))
    return pl.pallas_call(
        body, name=name, grid=(t // TM,),
        in_specs=[tile(d), pl.BlockSpec((1, d), lambda i: (0, 0)), pl.BlockSpec(w_in.shape, lambda i: (0, 0)),
                  pl.BlockSpec((TM, QK_W), lambda i: (i % per_seq, 0)), pl.BlockSpec((TM, QK_W), lambda i: (i % per_seq, 0))],
        out_specs=[tile(d), tile(QK_W), tile(QK_W), tile(V_W), tile(V_W), tile(POOL_W)],
        out_shape=[jax.ShapeDtypeStruct((t, d), BF16), jax.ShapeDtypeStruct((t, QK_W), BF16),
                   jax.ShapeDtypeStruct((t, QK_W), BF16), jax.ShapeDtypeStruct((t, V_W), BF16),
                   jax.ShapeDtypeStruct((t, V_W), F32), jax.ShapeDtypeStruct((t, POOL_W), F32)],
        compiler_params=_cparams(1),
    )(x, n, w_in, cos_t, sin_t)


def _mix_in_bwd(dp, dx2, x1, n, w_in_t, name, ride=None):
    t, d = x1.shape

    def body(dp_ref, dx2_ref, x_ref, n_ref, w_ref, dx_ref, dn_ref, dxb_ref):
        @pl.when(pl.program_id(0) == 0)
        def _():
            dn_ref[...] = jnp.zeros_like(dn_ref)

        dh = _dot(dp_ref[...], w_ref[...])
        xv = x_ref[...]
        r = lax.rsqrt(jnp.mean(xv * xv, axis=-1, keepdims=True) + RMS_EPS)
        xh = xv * r
        dn_ref[...] += jnp.sum(dh * xh, axis=0, keepdims=True)
        dhn = dh * n_ref[...]
        dx = dx2_ref[...] + r * (dhn - xh * jnp.mean(dhn * xh, axis=-1, keepdims=True))
        dx_ref[...] = dx
        dxb_ref[...] = (0.5 * dx).astype(BF16)

    tile = lambda w: pl.BlockSpec((TM, w), lambda i: (i, 0))
    return _pallas(
        body, name, (t // TM,),
        [tile(dp.shape[1]), tile(d), tile(d), pl.BlockSpec((1, d), lambda i: (0, 0)),
         pl.BlockSpec(w_in_t.shape, lambda i: (0, 0))],
        [tile(d), pl.BlockSpec((1, d), lambda i: (0, 0)), tile(d)],
        [jax.ShapeDtypeStruct((t, d), F32), jax.ShapeDtypeStruct((1, d), F32), jax.ShapeDtypeStruct((t, d), BF16)],
        [], (dp, dx2, x1, n, w_in_t), ride)


def _group_norm(o):
    parts, rstds = [], []
    for h in range(HEADS):
        oh = o[:, h * DV:(h + 1) * DV]
        dlt = oh - jnp.mean(oh, axis=-1, keepdims=True)
        rstd = lax.rsqrt(jnp.mean(dlt * dlt, axis=-1, keepdims=True) + GN_EPS)
        parts.append(dlt * rstd)
        rstds.append(rstd)
    return jnp.concatenate(parts, axis=1), rstds


def _mix_core_fwd(qs, k, v, g, u, x1, consts, gain, wp, scale, w_out, nseq, seq, name):
    t, d = x1.shape
    nblk = seq // BLK
    mask, dq, dk, gbd, bd = consts

    def body(q_ref, k_ref, v_ref, g_ref, u_ref, x1_ref, m_ref, dq_ref, dk_ref, gbd_ref, bd_ref, gain_ref, wp_ref,
             sc_ref, wo_ref, x2_ref, mix_ref, o_ref, pooled_ref, st_ref, state, halo):
        j = pl.program_id(1)

        @pl.when(j == 0)
        def _():
            state[...] = jnp.zeros_like(state)
            halo[...] = jnp.zeros_like(halo)

        qv, kv, vv = q_ref[...], k_ref[...], v_ref[...]
        st = state[...]
        st_ref[0] = st
        cross = _dot(qv, st.astype(BF16)) * dq_ref[...]
        outs = []
        for h in range(HEADS):
            qh = jnp.where(_head_mask(h), qv, jnp.zeros_like(qv))
            am = (_dot_nt(qh, kv) * m_ref[h]).astype(BF16)
            outs.append(_dot(am, vv[:, h * DV:(h + 1) * DV]))
        o = jnp.concatenate(outs, axis=1) + cross
        o_ref[...] = o
        kd = (kv.astype(F32) * dk_ref[...]).astype(BF16)
        state[...] = gbd_ref[...] * st + _dot_tn(kd, vv) * bd_ref[...]

        gv = g_ref[...]
        nrm, _ = _group_norm(o)
        ret = (gv * _sigmoid(gv)) * (nrm * gain_ref[...])

        uv = u_ref[...]
        c = jnp.concatenate([halo[...], uv], axis=0)
        halo[...] = uv[BLK - HALO:, :]
        pos = j * BLK + lax.broadcasted_iota(jnp.int32, (BLK, 1), 0)
        parts = []
        for gi, w in enumerate(WINDOWS):
            c = c + pltpu.roll(c, w // 2, 0)
            cnt = jnp.minimum(pos + 1, w).astype(F32)
            parts.append(c[HALO:, :GC] / cnt)
            if gi + 1 < len(WINDOWS):
                c = c[:, GC:]
        pooled = (jnp.concatenate(parts, axis=1) - uv).astype(BF16)
        pooled_ref[...] = pooled
        z = jnp.concatenate([_dot(pooled[:, gi * GC:(gi + 1) * GC], wp_ref[gi]) for gi in range(len(WINDOWS))], axis=1)
        mix = jnp.concatenate([ret, z * sc_ref[...]], axis=1).astype(BF16)
        mix_ref[...] = mix
        x2_ref[...] = x1_ref[...] + _dot(mix, wo_ref[...])

    blk = lambda w: pl.BlockSpec((BLK, w), lambda i, j: (i * nblk + j, 0))
    full = lambda a: pl.BlockSpec(a.shape, lambda i, j: (0,) * a.ndim)
    return _pallas(
        body, name, (nseq, nblk),
        [blk(QK_W), blk(QK_W), blk(V_W), blk(V_W), blk(POOL_W), blk(d),
         full(mask), full(dq), full(dk), full(gbd), full(bd), full(gain), full(wp), full(scale), full(w_out)],
        [blk(d), blk(d), blk(V_W), blk(POOL_W), pl.BlockSpec((1, QK_W, V_W), lambda i, j: (i * nblk + j, 0, 0))],
        [jax.ShapeDtypeStruct((t, d), F32), jax.ShapeDtypeStruct((t, d), BF16),
         jax.ShapeDtypeStruct((t, V_W), F32), jax.ShapeDtypeStruct((t, POOL_W), BF16),
         jax.ShapeDtypeStruct((nseq * nblk, QK_W, V_W), F32)],
        [pltpu.VMEM((QK_W, V_W), F32), pltpu.VMEM((HALO, POOL_W), F32)],
        (qs, k, v, g, u, x1, mask, dq, dk, gbd, bd, gain, wp, scale, w_out))


def _mix_core_bwd(dx2, qs, k, v, g, o, pooled, st, consts, gain, wp, scale, w_out, cos_t, sin_t, nseq, seq, name,
                  ride=None):
    t, d = dx2.shape
    nblk = seq // BLK
    mask, dq, dk, gbd, bd = consts
    n_win = len(WINDOWS)

    def body(dx2_ref, q_ref, k_ref, v_ref, g_ref, o_ref, pooled_ref, st_ref, m_ref, dq_ref, dk_ref, gbd_ref, bd_ref,
             gain_ref, wp_ref, sc_ref, wo_ref, c_ref, s_ref,
             dp_ref, dx2b_ref, dgain_ref, dscale_ref, dwp_ref, rstate, carry):
        i, j = pl.program_id(0), pl.program_id(1)

        @pl.when((i == 0) & (j == 0))
        def _():
            dgain_ref[...] = jnp.zeros_like(dgain_ref)
            dscale_ref[...] = jnp.zeros_like(dscale_ref)
            dwp_ref[...] = jnp.zeros_like(dwp_ref)

        @pl.when(j == 0)
        def _():
            rstate[...] = jnp.zeros_like(rstate)
            carry[...] = jnp.zeros_like(carry)

        dx2b = dx2_ref[...].astype(BF16)
        dx2b_ref[...] = dx2b
        dmix = _dot(dx2b, wo_ref[...])
        dret, dpool = dmix[:, :V_W], dmix[:, V_W:]

        gv, ov, gain_v = g_ref[...], o_ref[...], gain_ref[...]
        sg = _sigmoid(gv)
        sil = gv * sg
        nrm, rstds = _group_norm(ov)
        dg = dret * (nrm * gain_v) * (sg * (1.0 + gv * (1.0 - sg)))
        dgn = dret * sil
        dgain_ref[...] += jnp.sum(dgn * nrm, axis=0, keepdims=True)
        dnrm = dgn * gain_v
        do_parts = []
        for h in range(HEADS):
            dn_h = dnrm[:, h * DV:(h + 1) * DV]
            n_h = nrm[:, h * DV:(h + 1) * DV]
            do_parts.append(rstds[h] * (dn_h - jnp.mean(dn_h, axis=-1, keepdims=True)
                                        - n_h * jnp.mean(dn_h * n_h, axis=-1, keepdims=True)))
        do = jnp.concatenate(do_parts, axis=1)
        dob = do.astype(BF16)

        qv, kv, vv = q_ref[...], k_ref[...], v_ref[...]
        stb = st_ref[0].astype(BF16)
        rs = rstate[...]
        rsb = rs.astype(BF16)
        dod = (do * dq_ref[...]).astype(BF16)
        dqs = _dot_nt(dod, stb)
        dst = _dot_tn(qv, dod) * bd_ref[...]
        dkf = dk_ref[...]
        kd = (kv.astype(F32) * dkf).astype(BF16)
        dks = _dot_nt(vv, rsb) * dkf
        dvs = _dot(kd, rsb)
        dv_parts = []
        for h in range(HEADS):
            hm = _head_mask(h)
            qh = jnp.where(hm, qv, jnp.zeros_like(qv))
            mh = m_ref[h]
            am = (_dot_nt(qh, kv) * mh).astype(BF16)
            dpm = (_dot_nt(dob[:, h * DV:(h + 1) * DV], vv[:, h * DV:(h + 1) * DV]) * mh).astype(BF16)
            dqs = dqs + jnp.where(hm, _dot(dpm, kv), 0.0)
            dks = dks + jnp.where(hm, _dot_tn(dpm, qv), 0.0)
            dv_parts.append(_dot_tn(am, dob[:, h * DV:(h + 1) * DV]))
        dvs = dvs + jnp.concatenate(dv_parts, axis=1)
        rstate[...] = dst + gbd_ref[...] * rs

        cv, sv = c_ref[...], s_ref[...]
        dqr = dqs * (DK ** -0.5)
        dq_pre = dqr * cv + _swap_halves(dqr * sv)
        dk_pre = dks * cv + _swap_halves(dks * sv)

        pv = pooled_ref[...]
        sc = sc_ref[...]
        dzb = (dpool * sc).astype(BF16)
        z_parts, dpo_parts = [], []
        for gi in range(n_win):
            p_g = pv[:, gi * GC:(gi + 1) * GC]
            dz_g = dzb[:, gi * GC:(gi + 1) * GC]
            z_parts.append(_dot(p_g, wp_ref[gi]))
            dwp_ref[gi] += _dot_tn(p_g, dz_g)
            dpo_parts.append(_dot_nt(dz_g, wp_ref[gi]))
        dscale_ref[...] += jnp.sum(dpool * jnp.concatenate(z_parts, axis=1), axis=0, keepdims=True)
        dpo = jnp.concatenate(dpo_parts, axis=1)
        pos = (nblk - 1 - j) * BLK + lax.broadcasted_iota(jnp.int32, (BLK, 1), 0)
        e = jnp.concatenate(
            [dpo[:, gi * GC:(gi + 1) * GC] / jnp.minimum(pos + 1, w).astype(F32) for gi, w in enumerate(WINDOWS)], axis=1)
        c = jnp.concatenate([e, carry[...]], axis=0)
        carry[...] = e[:HALO, :]
        rows = BLK + HALO
        lead = []
        for gi, w in enumerate(WINDOWS):
            c = c + pltpu.roll(c, rows - w // 2, 0)
            lead.append(c[:BLK, :GC])
            if gi + 1 < n_win:
                c = c[:, GC:]
        du = jnp.concatenate(lead, axis=1) - dpo

        dp_ref[:, 0:QK_W] = dq_pre.astype(BF16)
        dp_ref[:, QK_W:2 * QK_W] = dk_pre.astype(BF16)
        dp_ref[:, 2 * QK_W:2 * QK_W + V_W] = dvs.astype(BF16)
        dp_ref[:, 2 * QK_W + V_W:2 * QK_W + 2 * V_W] = dg.astype(BF16)
        dp_ref[:, 2 * QK_W + 2 * V_W:] = du.astype(BF16)

    rev = lambda i, j: i * nblk + (nblk - 1 - j)
    blk = lambda w: pl.BlockSpec((BLK, w), lambda i, j: (rev(i, j), 0))
    full = lambda a: pl.BlockSpec(a.shape, lambda i, j: (0,) * a.ndim)
    in_w = 2 * QK_W + 2 * V_W + POOL_W
    return _pallas(
        body, name, (nseq, nblk),
        [blk(d), blk(QK_W), blk(QK_W), blk(V_W), blk(V_W), blk(V_W), blk(POOL_W),
         pl.BlockSpec((1, QK_W, V_W), lambda i, j: (rev(i, j), 0, 0)),
         full(mask), full(dq), full(dk), full(gbd), full(bd), full(gain), full(wp), full(scale), full(w_out),
         pl.BlockSpec((BLK, QK_W), lambda i, j: (nblk - 1 - j, 0)),
         pl.BlockSpec((BLK, QK_W), lambda i, j: (nblk - 1 - j, 0))],
        [blk(in_w), blk(d), pl.BlockSpec((1, V_W), lambda i, j: (0, 0)),
         pl.BlockSpec((1, POOL_W), lambda i, j: (0, 0)), pl.BlockSpec((n_win, GC, GC), lambda i, j: (0, 0, 0))],
        [jax.ShapeDtypeStruct((t, in_w), BF16), jax.ShapeDtypeStruct((t, d), BF16),
         jax.ShapeDtypeStruct((1, V_W), F32), jax.ShapeDtypeStruct((1, POOL_W), F32),
         jax.ShapeDtypeStruct((n_win, GC, GC), F32)],
        [pltpu.VMEM((QK_W, V_W), F32), pltpu.VMEM((HALO, POOL_W), F32)],
        (dx2, qs, k, v, g, o, pooled, st, mask, dq, dk, gbd, bd, gain, wp, scale, w_out, cos_t, sin_t), ride)


def _loss_head(x3, nf, tgt, name):
    t, d = x3.shape

    def body(x_ref, n_ref, t_ref, dx_ref, dn_ref, loss_ref, dxb_ref):
        @pl.when(pl.program_id(0) == 0)
        def _():
            dn_ref[...] = jnp.zeros_like(dn_ref)
            loss_ref[...] = jnp.zeros_like(loss_ref)

        xv = x_ref[...]
        nv = n_ref[...]
        r = lax.rsqrt(jnp.mean(xv * xv, axis=-1, keepdims=True) + RMS_EPS)
        xh = xv * r
        err = xh * nv - t_ref[...]
        row = jnp.mean(err * err, axis=-1, keepdims=True)
        loss_ref[...] += 0.5 * jnp.sum(row, axis=0, keepdims=True)
        dy = err * (1.0 / d)
        dn_ref[...] += jnp.sum(dy * xh, axis=0, keepdims=True)
        dxh = dy * nv
        dx = r * (dxh - xh * jnp.mean(dxh * xh, axis=-1, keepdims=True))
        dx_ref[...] = dx
        dxb_ref[...] = (0.5 * dx).astype(BF16)

    tile = pl.BlockSpec((TM, d), lambda i: (i, 0))
    return pl.pallas_call(
        body, name=name, grid=(t // TM,),
        in_specs=[tile, pl.BlockSpec((1, d), lambda i: (0, 0)), tile],
        out_specs=[tile, pl.BlockSpec((1, d), lambda i: (0, 0)), pl.BlockSpec((1, 1), lambda i: (0, 0)), tile],
        out_shape=[jax.ShapeDtypeStruct((t, d), F32), jax.ShapeDtypeStruct((1, d), F32), jax.ShapeDtypeStruct((1, 1), F32),
                   jax.ShapeDtypeStruct((t, d), BF16)],
        compiler_params=_cparams(1),
    )(x3, nf, tgt)


def _coords():
    return lax.axis_index("x"), lax.axis_index("y"), lax.axis_index("c")


def _window(ref, kind, idx, size):
    if kind == "col":
        return ref.at[:, pl.ds(pl.multiple_of(idx * size, LANE), size)]
    return ref.at[pl.ds(pl.multiple_of(idx * size, 8), size), :]


def _run_exchange(ex, name):
    n_in = len(ex.inputs)

    def body(*refs):
        ins, outs, sems = refs[:n_in], refs[n_in:n_in + len(ex.out_shape)], refs[n_in + len(ex.out_shape):]
        ex.start(ins, outs, sems)
        if ex.mid is not None:
            ex.mid(ins, outs, sems)
        ex.finish(ins, outs, sems)

    return pl.pallas_call(body, name=name, in_specs=[ANY] * n_in, out_specs=[ANY] * len(ex.out_shape),
                          out_shape=ex.out_shape, scratch_shapes=ex.scratch)(*ex.inputs)


def _join(exchanges):
    bounds = []
    i0 = o0 = s0 = 0
    for ex in exchanges:
        bounds.append((i0, o0, s0))
        i0, o0, s0 = i0 + len(ex.inputs), o0 + len(ex.out_shape), s0 + len(ex.scratch)

    def phase(which):
        def run(ins, outs, sems):
            for ex, (i, o, s) in zip(exchanges, bounds):
                fn = getattr(ex, which)
                if fn is not None:
                    fn(ins[i:i + len(ex.inputs)], outs[o:o + len(ex.out_shape)], sems[s:s + len(ex.scratch)])
        return run

    return _Exchange(sum((ex.inputs for ex in exchanges), []), sum((ex.out_shape for ex in exchanges), []),
                     sum((ex.scratch for ex in exchanges), []), phase("start"), phase("finish"),
                     phase("mid") if any(ex.mid is not None for ex in exchanges) else None)


def _gather_exchange(parts):
    n = len(parts)
    kinds = [kd for _, kd in parts]
    sizes = [a.shape[1] if kd == "col" else a.shape[0] for a, kd in parts]

    def plan(ins, outs, sems):
        send_sems, recv_sems, local_sems = sems
        x, y, c = _coords()
        me, sibling = (x, y, c), (x, y, 1 - c)
        chips = [(1 - x, y), (x, 1 - y), (1 - x, 1 - y)]

        def win(p, dev):
            return _window(outs[p], kinds[p], 4 * dev[0] + 2 * dev[1] + dev[2], sizes[p])

        def copy(p, k, block, to, src=None):
            return pltpu.make_async_remote_copy(
                src_ref=win(p, block) if src is None else src, dst_ref=win(p, block),
                send_sem=send_sems.at[p * 7 + k], recv_sem=recv_sems.at[p * 7 + k], device_id=to, device_id_type=MESH_ID)

        mine = [pltpu.make_async_copy(ins[p], win(p, me), local_sems.at[p]) for p in range(n)]
        first, arrived, passed, rest = [], [], [], []
        for p in range(n):
            first.append(copy(p, 0, me, sibling, src=ins[p]))
            first += [copy(p, 1 + q, me, (*chip, c), src=ins[p]) for q, chip in enumerate(chips)]
            rest.append(copy(p, 0, sibling, me))
            rest += [copy(p, 4 + q, (*chip, 1 - c), me) for q, chip in enumerate(chips)]
        for q, chip in enumerate(chips):
            for p in range(n):
                arrived.append(copy(p, 1 + q, (*chip, c), me))
                passed.append(copy(p, 4 + q, (*chip, c), sibling))
        return mine, first, arrived, passed, rest

    def start(ins, outs, sems):
        mine, first, _, _, _ = plan(ins, outs, sems)
        for cp in mine + first:
            cp.start()

    def mid(ins, outs, sems):
        _, _, arrived, passed, _ = plan(ins, outs, sems)
        for got, fwd in zip(arrived, passed):
            got.wait_recv()
            fwd.start()

    def finish(ins, outs, sems):
        mine, first, _, passed, rest = plan(ins, outs, sems)
        for cp in rest:
            cp.wait_recv()
        for cp in first + passed:
            cp.wait_send()
        for cp in mine:
            cp.wait()

    out_shape = [jax.ShapeDtypeStruct((a.shape[0], N_DEV * a.shape[1]) if kd == "col" else (N_DEV * a.shape[0], a.shape[1]),
                                      a.dtype) for a, kd in parts]
    scratch = [pltpu.SemaphoreType.DMA((7 * n,)), pltpu.SemaphoreType.DMA((7 * n,)), pltpu.SemaphoreType.DMA((n,))]
    return _Exchange([a for a, _ in parts], out_shape, scratch, start, finish, mid)


def _all_gather(parts, name):
    return _run_exchange(_gather_exchange(parts), name)


def _shard_shape(a, kd):
    return (a.shape[0], a.shape[1] // N_DEV) if kd == "col" else (a.shape[0] // N_DEV, a.shape[1])


def _symmetric_exchange(inputs, out_shape, n_copies, plan):
    def start(ins, outs, sems):
        for cp in plan(ins, outs, sems):
            cp.start()

    def finish(ins, outs, sems):
        copies = plan(ins, outs, sems)
        for cp in copies:
            cp.wait_recv()
        for cp in copies:
            cp.wait_send()

    scratch = [pltpu.SemaphoreType.DMA((n_copies,)), pltpu.SemaphoreType.DMA((n_copies,))]
    return _Exchange(inputs, out_shape, scratch, start, finish)


def _rs_pair_exchange(grads):
    n = len(grads)
    kinds = [kd for _, kd in grads]
    shapes = [_shard_shape(a, kd) for a, kd in grads]

    def plan(ins, outs, sems):
        send_sems, recv_sems = sems
        x, y, c = _coords()
        copies = []
        for p in range(n):
            size = shapes[p][1] if kinds[p] == "col" else shapes[p][0]
            for s in range(4):
                src = _window(ins[p], kinds[p], 2 * s + (1 - c), size)
                copies.append(pltpu.make_async_remote_copy(
                    src_ref=src, dst_ref=outs[p].at[s], send_sem=send_sems.at[4 * p + s], recv_sem=recv_sems.at[4 * p + s],
                    device_id=(x, y, 1 - c), device_id_type=MESH_ID))
        return copies

    return _symmetric_exchange([a for a, _ in grads], [jax.ShapeDtypeStruct((4,) + shapes[p], BF16) for p in range(n)],
                               4 * n, plan)


def _rs_chips_exchange(sums):
    n = len(sums)

    def plan(ins, outs, sems):
        send_sems, recv_sems = sems
        x, y, c = _coords()
        chips = [(1 - x, y), (x, 1 - y), (1 - x, 1 - y)]
        copies = []
        for p in range(n):
            for q, (cx, cy) in enumerate(chips):
                copies.append(pltpu.make_async_remote_copy(
                    src_ref=ins[p].at[2 * cx + cy], dst_ref=outs[p].at[q],
                    send_sem=send_sems.at[3 * p + q], recv_sem=recv_sems.at[3 * p + q],
                    device_id=(cx, cy, c), device_id_type=MESH_ID))
        return copies

    return _symmetric_exchange(list(sums), [jax.ShapeDtypeStruct((3,) + a.shape[1:], BF16) for a in sums], 3 * n, plan)


def _pair_sum(grad, kd, recv, core, name):
    _, r, cw = recv.shape
    tr = min(r, TM)

    def body(core_ref, g_ref, r_ref, o_ref):
        del core_ref
        o_ref[0] = (g_ref[...].astype(F32) + r_ref[0].astype(F32)).astype(BF16)

    if kd == "col":
        g_spec = pl.BlockSpec((tr, cw), lambda s, i, core_ref: (i, 2 * s + core_ref[0]))
    else:
        g_spec = pl.BlockSpec((tr, cw), lambda s, i, core_ref: ((2 * s + core_ref[0]) * (r // tr) + i, 0))
    grid_spec = pltpu.PrefetchScalarGridSpec(
        num_scalar_prefetch=1, grid=(4, r // tr),
        in_specs=[g_spec, pl.BlockSpec((1, tr, cw), lambda s, i, core_ref: (s, i, 0))],
        out_specs=pl.BlockSpec((1, tr, cw), lambda s, i, core_ref: (s, i, 0)))
    return pl.pallas_call(
        body, name=name, grid_spec=grid_spec, out_shape=jax.ShapeDtypeStruct(recv.shape, BF16),
        compiler_params=_cparams(2),
    )(core, grad, recv)


def _adam_math(w, g, m, v):
    m2 = B1 * m + (1.0 - B1) * g
    v2 = B2 * v + (1.0 - B2) * (g * g)
    m_hat = m2 / (1.0 - B1 ** STEP)
    v_hat = v2 / (1.0 - B2 ** STEP)
    delta = -LR * (m_hat / (jnp.sqrt(v_hat) + ADAM_EPS) + WD * w)
    return delta, m2, v2


def _chip_sum_adam(items, chip, tr, name, ride=None):
    r = items[0][1].shape[0]
    steps = r // tr
    n_parts = [len(parts) for parts, _, _, _ in items]
    r_in = 0 if ride is None else len(ride.inputs)
    r_out = 0 if ride is None else len(ride.out_shape)
    n_in = sum(2 * k + 3 for k in n_parts)
    n_out = 4 * len(items)

    def body(chip_ref, *refs):
        del chip_ref
        ins, refs = refs[:n_in], refs[n_in:]
        r_ins, refs = refs[:r_in], refs[r_in:]
        outs, refs = refs[:n_out], refs[n_out:]
        r_outs, sems = refs[:r_out], refs[r_out:]
        i = pl.program_id(0)
        if ride is not None:
            @pl.when(i == 0)
            def _():
                ride.start(r_ins, r_outs, sems)

        pos = 0
        for q, (k, (_, w, _, _)) in enumerate(zip(n_parts, items)):
            cols = []
            for _ in range(k):
                p_ref, c_ref = ins[pos], ins[pos + 1]
                pos += 2
                cols.append(p_ref[0].astype(F32) + c_ref[0].astype(F32) + c_ref[1].astype(F32) + c_ref[2].astype(F32))
            g = (cols[0] if k == 1 else jnp.concatenate(cols, axis=1))[:, :w.shape[1]]
            w_ref, m_ref, v_ref = ins[pos:pos + 3]
            pos += 3
            delta, m2, v2 = _adam_math(w_ref[...], g, m_ref[...], v_ref[...])
            outs[4 * q][...] = g
            outs[4 * q + 1][...] = delta
            outs[4 * q + 2][...] = m2
            outs[4 * q + 3][...] = v2

        if ride is not None:
            @pl.when(i == steps - 1)
            def _():
                ride.finish(r_ins, r_outs, sems)

    in_specs, args, out_specs, out_shape = [], [], [], []
    for parts, w, m, v in items:
        for psum, recv in parts:
            pc = psum.shape[2]
            in_specs += [pl.BlockSpec((1, tr, pc), lambda i, chip_ref: (chip_ref[0], i, 0)),
                         pl.BlockSpec((3, tr, pc), lambda i, chip_ref: (0, i, 0))]
            args += [psum, recv]
        loc = pl.BlockSpec((tr, w.shape[1]), lambda i, chip_ref: (i, 0))
        in_specs += [loc] * 3
        args += [w, m, v]
        out_specs += [loc] * 4
        out_shape += [jax.ShapeDtypeStruct(w.shape, F32)] * 4
    grid_spec = pltpu.PrefetchScalarGridSpec(
        num_scalar_prefetch=1, grid=(steps,), in_specs=in_specs + [ANY] * r_in, out_specs=out_specs + [ANY] * r_out,
        scratch_shapes=[] if ride is None else ride.scratch)
    res = pl.pallas_call(
        body, name=name, grid_spec=grid_spec, out_shape=out_shape + ([] if ride is None else ride.out_shape),
        compiler_params=_cparams(1),
    )(chip, *args, *([] if ride is None else ride.inputs))
    return res if ride is None else (res[:n_out], res[n_out:])


def _small_allreduce_adam(partials, params, moms, vels, plain, name, ride=None):
    n, n_plain = len(partials), len(plain)
    summed = list(partials) + list(plain)
    row0 = []
    rows = 0
    for a in summed:
        if a.shape[0] >= 8:
            rows = _pad_to(rows, 8)
        row0.append(rows)
        rows += a.shape[0]
    rows = _pad_to(rows, 8)
    width = max(a.shape[1] for a in summed)
    r_in = 0 if ride is None else len(ride.inputs)
    r_out = 0 if ride is None else len(ride.out_shape)
    n_out = 4 * n + n_plain

    def body(*refs):
        w_in, m_in, v_in = refs[0:n], refs[n:2 * n], refs[2 * n:3 * n]
        g_in, refs = refs[3 * n:4 * n + n_plain], refs[4 * n + n_plain:]
        r_ins, refs = refs[:r_in], refs[r_in:]
        outs, refs = refs[:n_out], refs[n_out:]
        r_outs, refs = refs[:r_out], refs[r_out:]
        pair, chips, send_sems, recv_sems = refs[:4]
        if ride is not None:
            ride.start(r_ins, r_outs, refs[4:])
        x, y, c = _coords()
        chip = 2 * x + y
        pair[c] = jnp.zeros((rows, width), F32)
        for p, a in enumerate(summed):
            r, cw = a.shape
            pair[c, row0[p]:row0[p] + r, 0:cw] = g_in[p][...]
        swap = pltpu.make_async_remote_copy(src_ref=pair.at[c], dst_ref=pair.at[c], send_sem=send_sems.at[0],
                                            recv_sem=recv_sems.at[0], device_id=(x, y, 1 - c), device_id_type=MESH_ID)
        swap.start()
        swap.wait_recv()
        swap.wait_send()
        chips[chip] = pair[0] + pair[1]
        copies = [pltpu.make_async_remote_copy(
            src_ref=chips.at[chip], dst_ref=chips.at[chip], send_sem=send_sems.at[1 + q], recv_sem=recv_sems.at[1 + q],
            device_id=(cx, cy, c), device_id_type=MESH_ID) for q, (cx, cy) in enumerate([(1 - x, y), (x, 1 - y), (1 - x, 1 - y)])]
        for cp in copies:
            cp.start()
        for cp in copies:
            cp.wait_recv()
        for cp in copies:
            cp.wait_send()
        for p, a in enumerate(summed):
            r, cw = a.shape
            g = chips[0, row0[p]:row0[p] + r, 0:cw]
            for q in range(1, 4):
                g = g + chips[q, row0[p]:row0[p] + r, 0:cw]
            if p >= n:
                outs[4 * n + p - n][...] = g
                continue
            delta, m2, v2 = _adam_math(w_in[p][...], g, m_in[p][...], v_in[p][...])
            outs[4 * p][...] = g
            outs[4 * p + 1][...] = delta
            outs[4 * p + 2][...] = m2
            outs[4 * p + 3][...] = v2
        if ride is not None:
            ride.finish(r_ins, r_outs, refs[4:])

    out_shape = []
    for a in partials:
        out_shape += [jax.ShapeDtypeStruct(a.shape, F32)] * 4
    out_shape += [jax.ShapeDtypeStruct(a.shape, F32) for a in plain]
    res = pl.pallas_call(
        body, name=name, in_specs=[VMEM_SPEC] * (4 * n + n_plain) + [ANY] * r_in,
        out_specs=[VMEM_SPEC] * n_out + [ANY] * r_out, out_shape=out_shape + ([] if ride is None else ride.out_shape),
        scratch_shapes=[pltpu.VMEM((2, rows, width), F32), pltpu.VMEM((4, rows, width), F32),
                        pltpu.SemaphoreType.DMA((4,)), pltpu.SemaphoreType.DMA((4,))] + ([] if ride is None else ride.scratch),
    )(*params, *moms, *vels, *partials, *plain, *([] if ride is None else ride.inputs))
    return res if ride is None else (res[:n_out], res[n_out:])


def _local_step(xf, tgt, nseq, seq, cols1_all, later, small_w, core, small_step):
    d = xf.shape[1]
    n1, n2, gain, pool_w, pool_scale, n3, nf = small_w
    tf = 2 * cols1_all.shape[1] // N_DEV
    consts = _retention_constants()
    cos_t, sin_t = _rotary_tables(seq)
    wp_b = pool_w.astype(BF16)

    def pair_sums(grads, recv, names):
        return [_pair_sum(g, kd, r, core, "pair_sum_" + nm) for (g, kd), r, nm in zip(grads, recv, names)]

    def riding(host):
        return _gather_exchange(later[host])

    both = lambda first, second: _join([_rs_chips_exchange(first), _rs_pair_exchange([second])])

    (h1, b1, sil1, dsil1, s1), (d1_all, win_all, wout_all, gate2_all, up2_all) = _ffn_act(
        xf, n1, (cols1_all, 0), (cols1_all, 1), "ffn1_act", ride=riding("ffn1_act"))
    x1, (d2_all,) = _ffn_down(s1, xf, d1_all, "ffn1_down", ride=riding("ffn1_down"))
    h2, qs, kr, vv, gg, uu = _mix_in(x1, n2, win_all, cos_t, sin_t, seq, "mix_in")
    x2, mix, oo, pooled, states = _mix_core_fwd(qs, kr, vv, gg, uu, x1, consts, gain, wp_b, pool_scale, wout_all,
                                                 nseq, seq, "mix_core_fwd")
    (x3, h3, b3, sil3, dsil3), (cols2_t, d2_t, win_t, wout_t) = _ffn_fwd(
        x2, n3, (gate2_all, 0), (up2_all, 0), d2_all, "ffn2_fwd", ride=riding("ffn2_fwd"))
    dx3, dnf, loss_part, dx3b = _loss_head(x3, nf, tgt, "loss_head")
    out = {}

    (da3, db3, g_wd2), (cols1_t, d1_t) = _ffn_bwd_act(dx3b, b3, sil3, dsil3, d2_t, "ffn2_bwd_act",
                                                      ride=riding("ffn2_bwd_act"))
    names2 = ["ffn2_gate", "ffn2_up", "ffn2_down"]
    grads2 = [(_wgrad(da3, h3, 1.0, tf, d, "wgrad_gate2"), "row"), (_wgrad(db3, h3, 1.0, tf, d, "wgrad_up2"), "row"),
              (g_wd2, "row")]
    (dx2, dn3), recv2 = _ffn_bwd_in(da3, db3, dx3, x2, n3, cols2_t, 0, "ffn2_bwd_in", ride=_rs_pair_exchange(grads2))
    sums2 = pair_sums(grads2, recv2, names2)
    (dp, dx2b, dgain, dscale, dwp), crecv2 = _mix_core_bwd(
        dx2, qs, kr, vv, gg, oo, pooled, states, consts, gain, wp_b, pool_scale, wout_t, cos_t, sin_t, nseq, seq,
        "mix_core_bwd", ride=_rs_chips_exchange(sums2))
    out.update({nm: [(s, r)] for nm, s, r in zip(names2, sums2, crecv2)})

    names_m = ["w_in", "w_out"]
    grads_m = [(_wgrad(h2, dp, 1.0, d, d, "wgrad_in"), "col"), (_wgrad(mix, dx2b, 1.0, d, d, "wgrad_out"), "row")]
    (dx1, dn2, dx1b), recv_m = _mix_in_bwd(dp, dx2, x1, n2, win_t, "mix_in_bwd", ride=_rs_pair_exchange(grads_m))
    sums_m = pair_sums(grads_m, recv_m, names_m)
    (da1, db1, g_wd1), crecv_m = _ffn_bwd_act(dx1b, b1, sil1, dsil1, d1_t, "ffn1_bwd_act", ride=_rs_chips_exchange(sums_m))
    out.update({nm: [(s, r)] for nm, s, r in zip(names_m, sums_m, crecv_m)})

    dx0, dn1 = _ffn_bwd_in(da1, db1, dx1, xf, n1, cols1_t, 0, "ffn1_bwd_in")
    g_down = (g_wd1, "row")
    g_gate, recv_d = _wgrad(da1, h1, 1.0, tf, d, "wgrad_gate1", ride=_rs_pair_exchange([g_down]))
    g_gate = (g_gate, "row")
    sum_d = pair_sums([g_down], recv_d, ["ffn1_down"])
    g_lo, (crecv_d, recv_g) = _wgrad(db1, h1, 1.0, tf, d // 2, "wgrad_up1_lo", ride=both(sum_d, g_gate), b_cols=(0, 1))
    g_lo = (g_lo, "row")
    sum_g = pair_sums([g_gate], [recv_g], ["ffn1_gate"])
    g_hi, (crecv_g, recv_lo) = _wgrad(db1, h1, 1.0, tf, d // 2, "wgrad_up1_hi", ride=both(sum_g, g_lo), b_cols=(1, 1))
    g_hi = (g_hi, "row")
    sum_lo = pair_sums([g_lo], [recv_lo], ["ffn1_up_lo"])
    small_out, (crecv_lo, recv_hi) = small_step((dn1, dn2, dgain, dwp, dscale, dn3, dnf), loss_part, both(sum_lo, g_hi))
    sum_hi = pair_sums([g_hi], [recv_hi], ["ffn1_up_hi"])
    out.update({"ffn1_gate": [(sum_g[0], crecv_g)], "ffn1_down": [(sum_d[0], crecv_d)],
                "ffn1_up": [(sum_lo[0], crecv_lo), (sum_hi[0], None)]})
    return small_out[-1], dx0, out, small_out[:-1], _rs_chips_exchange(sum_hi)


def kernel(x, norm_ffn1, ffn1_gate, ffn1_up, ffn1_down, norm_mix, w_in, ret_gn_gain, pool_w, pool_scale, w_out, norm_ffn2, ffn2_gate, ffn2_up, ffn2_down, norm_final, loss_target, m_norm_ffn1, m_ffn1_gate, m_ffn1_up, m_ffn1_down, m_norm_mix, m_w_in, m_ret_gn_gain, m_pool_w, m_pool_scale, m_w_out, m_norm_ffn2, m_ffn2_gate, m_ffn2_up, m_ffn2_down, m_norm_final, v_norm_ffn1, v_ffn1_gate, v_ffn1_up, v_ffn1_down, v_norm_mix, v_w_in, v_ret_gn_gain, v_pool_w, v_pool_scale, v_w_out, v_norm_ffn2, v_ffn2_gate, v_ffn2_up, v_ffn2_down, v_norm_final):
    nseq, seq, d = x.shape
    t = nseq * seq
    f_loc = ffn1_gate.shape[2]
    f_pad = _pad_to(f_loc, LANE)
    xf = x.reshape(t, d)
    tgt = loss_target.reshape(t, d)
    core = lax.axis_index("c").astype(jnp.int32).reshape(1)
    chip = (2 * lax.axis_index("x") + lax.axis_index("y")).astype(jnp.int32).reshape(1)

    colp = lambda w: jnp.pad(w[0].astype(BF16), ((0, 0), (0, f_pad - f_loc)))
    rowp = lambda w: jnp.pad(w[0].astype(BF16), ((0, f_pad - f_loc), (0, 0)))
    gate2, up2 = colp(ffn2_gate), colp(ffn2_up)
    cols1 = jnp.concatenate([colp(ffn1_gate), colp(ffn1_up)], axis=0)
    cols2_t = jnp.concatenate([gate2.T, up2.T], axis=1)
    (cols1_all,) = _all_gather([(cols1, "col")], "all_gather_ffn1")
    d1_loc, d2_loc, win_loc, wout_loc = rowp(ffn1_down), rowp(ffn2_down), w_in[0].astype(BF16), w_out[0].astype(BF16)
    later = {"ffn1_act": [(d1_loc, "row"), (win_loc, "col"), (wout_loc, "row"), (gate2, "col"), (up2, "col")],
             "ffn1_down": [(d2_loc, "row")],
             "ffn2_fwd": [(cols2_t, "row"), (d2_loc.T, "col"), (win_loc.T, "row"), (wout_loc.T, "col")],
             "ffn2_bwd_act": [(cols1.T, "row"), (d1_loc.T, "col")]}

    flat = lambda a: a.reshape(pool_w.size // d, d)
    params = [norm_ffn1, norm_mix, ret_gn_gain, flat(pool_w), pool_scale, norm_ffn2, norm_final.reshape(1, d)]
    moms = [m_norm_ffn1, m_norm_mix, m_ret_gn_gain, flat(m_pool_w), m_pool_scale, m_norm_ffn2, m_norm_final.reshape(1, d)]
    vels = [v_norm_ffn1, v_norm_mix, v_ret_gn_gain, flat(v_pool_w), v_pool_scale, v_norm_ffn2, v_norm_final.reshape(1, d)]

    def small_step(parts, loss_part, ride):
        dn1, dn2, dgain, dwp, dscale, dn3, dnf = parts
        return _small_allreduce_adam([dn1, dn2, dgain, flat(dwp), dscale, dn3, dnf], params, moms, vels, [loss_part],
                                     "small_allreduce_adam", ride)

    small_w = (norm_ffn1, norm_mix, ret_gn_gain, pool_w[0], pool_scale, norm_ffn2, norm_final.reshape(1, d))
    loss_sum, dx0, reduced, small_out, pending = _local_step(xf, tgt, nseq, seq, cols1_all, later, small_w, core,
                                                             small_step)

    local = {"ffn1_gate": (ffn1_gate, m_ffn1_gate, v_ffn1_gate), "ffn1_up": (ffn1_up, m_ffn1_up, v_ffn1_up),
             "ffn1_down": (ffn1_down, m_ffn1_down, v_ffn1_down), "w_in": (w_in, m_w_in, v_w_in),
             "w_out": (w_out, m_w_out, v_w_out), "ffn2_gate": (ffn2_gate, m_ffn2_gate, v_ffn2_gate),
             "ffn2_up": (ffn2_up, m_ffn2_up, v_ffn2_up), "ffn2_down": (ffn2_down, m_ffn2_down, v_ffn2_down)}
    flip = lambda nm: nm.endswith("gate") or nm.endswith("up")

    def item(nm):
        view = (lambda a: a[0].T) if flip(nm) else (lambda a: a[0])
        w, m, v = local[nm]
        return reduced[nm], view(w), view(m), view(v)

    big = {}

    def keep(names, res):
        for q, nm in enumerate(names):
            big[nm] = tuple((a.T if flip(nm) else a)[None] for a in res[4 * q:4 * q + 4])

    second = ["ffn2_gate", "ffn2_up", "ffn2_down"]
    res, (last_recv,) = _chip_sum_adam([item(nm) for nm in second], chip, item(second[0])[1].shape[0] // 2, "adam_ffn2",
                                       ride=pending)
    keep(second, res)
    reduced["ffn1_up"][-1] = (reduced["ffn1_up"][-1][0], last_recv)
    first = ["ffn1_gate", "ffn1_up", "ffn1_down"]
    keep(first, _chip_sum_adam([item(nm) for nm in first], chip, item(first[0])[1].shape[0] // 2, "adam_ffn1"))
    for nm in ["w_in", "w_out"]:
        keep([nm], _chip_sum_adam([item(nm)], chip, min(item(nm)[1].shape[0], TM), "adam_" + nm))

    small_names = ["norm_ffn1", "norm_mix", "ret_gn_gain", "pool_w", "pool_scale", "norm_ffn2", "norm_final"]
    shapes = [norm_ffn1.shape, norm_mix.shape, ret_gn_gain.shape, pool_w.shape, pool_scale.shape, norm_ffn2.shape,
              norm_final.shape]
    small = {nm: tuple(small_out[4 * p + q].reshape(shapes[p]) for q in range(4)) for p, nm in enumerate(small_names)}

    loss = loss_sum[0, 0]
    order = ["norm_ffn1", "ffn1_gate", "ffn1_up", "ffn1_down", "norm_mix", "w_in", "ret_gn_gain", "pool_w", "pool_scale",
             "w_out", "norm_ffn2", "ffn2_gate", "ffn2_up", "ffn2_down", "norm_final"]
    both = {**big, **small}
    outs = [loss, dx0.reshape(nseq, seq, d)]
    for q in range(4):
        outs += [both[nm][q] for nm in order]
    return tuple(outs)
```

```python
import numpy as np
import jax
import jax.numpy as jnp
from jax import lax
from jax.experimental import pallas as pl
from jax.experimental.pallas import tpu as pltpu

F32, BF16 = jnp.float32, jnp.bfloat16
MESH_ID = pl.DeviceIdType.MESH
ANY = pl.BlockSpec(memory_space=pl.ANY)
VMEM_SPEC = pl.BlockSpec(memory_space=pltpu.VMEM)

N_DEV = 8
RMS_EPS = 1e-6
GN_EPS = 1e-5
HEADS, DK, DV = 4, 64, 128
QK_W, V_W, POOL_W = HEADS * DK, HEADS * DV, 512
WINDOWS = (2, 4, 8, 16)
GC = POOL_W // len(WINDOWS)
CHUNK = 64
BLK = 4 * CHUNK
HALO = 16
ROPE_BASE = 10000.0
LR, B1, B2, ADAM_EPS, WD, STEP = 0.001, 0.9, 0.999, 1e-08, 0.01, 10
LANE = 128
TM = 512
FFN_TM = 1024
FFN_FWD_TF = 768
WGRAD_TT = 4096
VMEM_LIMIT = 56 * 1024 * 1024


def _cparams(n_axes):
    return pltpu.CompilerParams(dimension_semantics=("arbitrary",) * n_axes, vmem_limit_bytes=VMEM_LIMIT)


class _Exchange:
    def __init__(self, inputs, out_shape, scratch, start, finish, mid=None):
        self.inputs, self.out_shape, self.scratch = list(inputs), list(out_shape), list(scratch)
        self.start, self.finish, self.mid = start, finish, mid


def _pallas(body, name, grid, in_specs, out_specs, out_shape, scratch_shapes, args, ride=None):
    n_axes = len(grid)
    if ride is None:
        return pl.pallas_call(body, name=name, grid=grid, in_specs=in_specs, out_specs=out_specs, out_shape=out_shape,
                              scratch_shapes=scratch_shapes, compiler_params=_cparams(n_axes))(*args)
    n_in, n_out, n_scr = len(in_specs), len(out_specs), len(scratch_shapes)
    r_in, r_out = len(ride.inputs), len(ride.out_shape)

    def hosted(*refs):
        ins, refs = refs[:n_in], refs[n_in:]
        r_ins, refs = refs[:r_in], refs[r_in:]
        outs, refs = refs[:n_out], refs[n_out:]
        r_outs, refs = refs[:r_out], refs[r_out:]
        scr, sems = refs[:n_scr], refs[n_scr:]
        ids = [pl.program_id(a) for a in range(n_axes)]
        first, last, inner0 = ids[0] == 0, ids[0] == grid[0] - 1, None
        for a in range(1, n_axes):
            first = first & (ids[a] == 0)
            last = last & (ids[a] == grid[a] - 1)
            inner0 = (ids[a] == 0) if inner0 is None else inner0 & (ids[a] == 0)

        @pl.when(first)
        def _():
            ride.start(r_ins, r_outs, sems)

        if ride.mid is not None:
            at_mid = ids[0] == grid[0] - 1
            if inner0 is not None:
                at_mid = at_mid & inner0

            @pl.when(at_mid)
            def _():
                ride.mid(r_ins, r_outs, sems)

        body(*ins, *outs, *scr)

        @pl.when(last)
        def _():
            ride.finish(r_ins, r_outs, sems)

    res = pl.pallas_call(
        hosted, name=name, grid=grid, in_specs=list(in_specs) + [ANY] * r_in, out_specs=list(out_specs) + [ANY] * r_out,
        out_shape=list(out_shape) + ride.out_shape, scratch_shapes=list(scratch_shapes) + ride.scratch,
        compiler_params=_cparams(n_axes))(*args, *ride.inputs)
    return res[:n_out], res[n_out:]


def _dot(a, b):
    return jnp.dot(a, b, preferred_element_type=F32)


def _dot_nt(a, b):
    return lax.dot_general(a, b, (((1,), (1,)), ((), ())), preferred_element_type=F32)


def _dot_tn(a, b):
    return lax.dot_general(a, b, (((0,), (0,)), ((), ())), preferred_element_type=F32)


def _sigmoid(x):
    return 0.5 * jnp.tanh(0.5 * x) + 0.5


def _pad_to(n, m):
    return (n + m - 1) // m * m


def _retention_constants():
    gamma = (1.0 - 2.0 ** (-5.0 - np.arange(HEADS, dtype=np.float32))).astype(np.float32)
    log_g = np.log(gamma).astype(np.float32)
    i = np.arange(BLK)
    diff = (i[:, None] - i[None, :]).astype(np.float32)
    same = (i[:, None] // CHUNK) == (i[None, :] // CHUNK)
    earlier = (i[None, :] // CHUNK) < (i[:, None] // CHUNK)
    mask = np.zeros((HEADS, BLK, BLK), np.float32)
    for h in range(HEADS):
        dec_abs = np.exp(log_g[h] * np.abs(diff)).astype(np.float32)
        dec = np.exp(log_g[h] * diff * earlier).astype(np.float32)
        mask[h] = np.where(same, dec_abs, np.where(earlier, dec, 0.0))
    dq = np.zeros((BLK, V_W), np.float32)
    dk = np.zeros((BLK, QK_W), np.float32)
    gbd = np.zeros((QK_W, V_W), np.float32)
    for h in range(HEADS):
        dq[:, h * DV:(h + 1) * DV] = np.exp(log_g[h] * (i + 1.0)).astype(np.float32)[:, None]
        dk[:, h * DK:(h + 1) * DK] = np.exp(log_g[h] * (BLK - 1.0 - i)).astype(np.float32)[:, None]
        gbd[h * DK:(h + 1) * DK, h * DV:(h + 1) * DV] = np.exp(log_g[h] * np.float32(BLK))
    bd = (gbd > 0).astype(np.float32)
    return jnp.asarray(mask), jnp.asarray(dq), jnp.asarray(dk), jnp.asarray(gbd), jnp.asarray(bd)


def _rotary_tables(seq):
    half = DK // 2
    freqs = ROPE_BASE ** (-jnp.arange(half, dtype=F32) * 2.0 / DK)
    ang = jnp.arange(seq, dtype=F32)[:, None] * freqs[None, :]
    cos, sin = jnp.cos(ang), jnp.sin(ang)
    cos_t = jnp.tile(jnp.concatenate([cos, cos], axis=1), (1, HEADS))
    sin_t = jnp.tile(jnp.concatenate([-sin, sin], axis=1), (1, HEADS))
    return cos_t, sin_t


def _swap_halves(x):
    lane = lax.broadcasted_iota(jnp.int32, (1, QK_W), 1)
    first = (lane & (DK - 1)) < DK // 2
    return jnp.where(first, pltpu.roll(x, QK_W - DK // 2, 1), pltpu.roll(x, DK // 2, 1))


def _head_mask(h):
    lane = lax.broadcasted_iota(jnp.int32, (1, QK_W), 1)
    return (lane >= h * DK) & (lane < (h + 1) * DK)


def _ffn_fwd(x, n, gate, up, wd, name, ride=None):
    t, d = x.shape
    (wg, gq), (wu, uq) = gate, up
    fp = wg.shape[1]
    tm = min(t, FFN_TM)
    tf = FFN_FWD_TF
    nj = fp // tf

    def body(x_ref, n_ref, wg_ref, wu_ref, wd_ref, xo_ref, h_ref, b_ref, sil_ref, dsil_ref, acc_ref):
        j = pl.program_id(1)

        @pl.when(j == 0)
        def _():
            xv = x_ref[...]
            r = lax.rsqrt(jnp.mean(xv * xv, axis=-1, keepdims=True) + RMS_EPS)
            h_ref[...] = (xv * r * n_ref[...]).astype(BF16)
            acc_ref[...] = jnp.zeros_like(acc_ref)

        h = h_ref[...]
        a = _dot(h, wg_ref[...])
        b = _dot(h, wu_ref[...])
        sg = _sigmoid(a)
        sil = a * sg
        b_ref[...] = b.astype(BF16)
        sil_ref[...] = sil.astype(BF16)
        dsil_ref[...] = (sg + sil * (1.0 - sg)).astype(BF16)
        acc_ref[...] += _dot((sil * b).astype(BF16), wd_ref[...])

        @pl.when(j == nj - 1)
        def _():
            xo_ref[...] = x_ref[...] + 0.5 * acc_ref[...]

    act = pl.BlockSpec((tm, tf), lambda i, j: (i, j))
    return _pallas(
        body, name, (t // tm, nj),
        [pl.BlockSpec((tm, d), lambda i, j: (i, 0)), pl.BlockSpec((1, d), lambda i, j: (0, 0)),
         pl.BlockSpec((d, tf), lambda i, j: (gq, j)), pl.BlockSpec((d, tf), lambda i, j: (uq, j)),
         pl.BlockSpec((tf, d), lambda i, j: (j, 0))],
        [pl.BlockSpec((tm, d), lambda i, j: (i, 0)), pl.BlockSpec((tm, d), lambda i, j: (i, 0)), act, act, act],
        [jax.ShapeDtypeStruct((t, d), F32), jax.ShapeDtypeStruct((t, d), BF16)] + [jax.ShapeDtypeStruct((t, fp), BF16)] * 3,
        [pltpu.VMEM((tm, d), F32)], (x, n, wg, wu, wd), ride)


def _ffn_act(x, n, gate, up, name, ride=None):
    t, d = x.shape
    (wg, gq), (wu, uq) = gate, up
    fp = wg.shape[1]
    tm = min(t, FFN_TM)
    tf = 2 * fp // N_DEV
    nj = fp // tf

    def body(x_ref, n_ref, wg_ref, wu_ref, h_ref, b_ref, sil_ref, dsil_ref, s_ref):
        @pl.when(pl.program_id(1) == 0)
        def _():
            xv = x_ref[...]
            r = lax.rsqrt(jnp.mean(xv * xv, axis=-1, keepdims=True) + RMS_EPS)
            h_ref[...] = (xv * r * n_ref[...]).astype(BF16)

        h = h_ref[...]
        a = _dot(h, wg_ref[...])
        b = _dot(h, wu_ref[...])
        sg = _sigmoid(a)
        sil = a * sg
        b_ref[...] = b.astype(BF16)
        sil_ref[...] = sil.astype(BF16)
        dsil_ref[...] = (sg + sil * (1.0 - sg)).astype(BF16)
        s_ref[...] = (sil * b).astype(BF16)

    act = pl.BlockSpec((tm, tf), lambda i, j: (i, j))
    return _pallas(
        body, name, (t // tm, nj),
        [pl.BlockSpec((tm, d), lambda i, j: (i, 0)), pl.BlockSpec((1, d), lambda i, j: (0, 0)),
         pl.BlockSpec((d, tf), lambda i, j: (gq, j)), pl.BlockSpec((d, tf), lambda i, j: (uq, j))],
        [pl.BlockSpec((tm, d), lambda i, j: (i, 0)), act, act, act, act],
        [jax.ShapeDtypeStruct((t, d), BF16)] + [jax.ShapeDtypeStruct((t, fp), BF16)] * 4,
        [], (x, n, wg, wu), ride)


def _ffn_down(s, x, wd, name, ride=None):
    t, d = x.shape
    fp = wd.shape[0]
    tm = min(t, FFN_TM)

    def body(s_ref, x_ref, wd_ref, xo_ref):
        xo_ref[...] = x_ref[...] + 0.5 * _dot(s_ref[...], wd_ref[...])

    row = pl.BlockSpec((tm, d), lambda i: (i, 0))
    res = _pallas(body, name, (t // tm,), [pl.BlockSpec((tm, fp), lambda i: (i, 0)), row, pl.BlockSpec((fp, d), lambda i: (0, 0))],
                  [row], [jax.ShapeDtypeStruct((t, d), F32)], [], (s, x, wd), ride)
    return res[0] if ride is None else (res[0][0], res[1])


def _ffn_bwd_act(dxob, b, sil, dsil, wd_t, name, ride=None):
    t, d = dxob.shape
    fp = wd_t.shape[1]
    tm = min(t, FFN_TM)
    tf = 2 * fp // N_DEV
    ni = t // tm

    def body(dx_ref, b_ref, sil_ref, dsil_ref, wd_ref, da_ref, db_ref, gd_ref, acc_ref):
        i = pl.program_id(1)

        @pl.when(i == 0)
        def _():
            acc_ref[...] = jnp.zeros_like(acc_ref)

        dxv = dx_ref[...]
        bv, sv = b_ref[...].astype(F32), sil_ref[...].astype(F32)
        ds = _dot(dxv, wd_ref[...])
        da_ref[...] = (ds * bv * dsil_ref[...].astype(F32)).astype(BF16)
        db_ref[...] = (ds * sv).astype(BF16)
        acc_ref[...] += _dot_tn((sv * bv).astype(BF16), dxv)

        @pl.when(i == ni - 1)
        def _():
            gd_ref[...] = acc_ref[...].astype(BF16)

    act = pl.BlockSpec((tm, tf), lambda c, i: (i, c))
    return _pallas(
        body, name, (fp // tf, ni),
        [pl.BlockSpec((tm, d), lambda c, i: (i, 0)), act, act, act, pl.BlockSpec((d, tf), lambda c, i: (0, c))],
        [act, act, pl.BlockSpec((tf, d), lambda c, i: (c, 0))],
        [jax.ShapeDtypeStruct((t, fp), BF16), jax.ShapeDtypeStruct((t, fp), BF16), jax.ShapeDtypeStruct((fp, d), BF16)],
        [pltpu.VMEM((tf, d), F32)], (dxob, b, sil, dsil, wd_t), ride)


def _ffn_bwd_in(da, db, dxo, x, n, cols_t, gq, name, ride=None):
    t, d = x.shape
    fp = cols_t.shape[0]
    tm = min(t, FFN_TM)
    tf = fp // 3
    nj = fp // tf

    def body(da_ref, db_ref, dxo_ref, x_ref, n_ref, wg_ref, wu_ref, dx_ref, dn_ref, acc_ref):
        i, j = pl.program_id(0), pl.program_id(1)

        @pl.when((i == 0) & (j == 0))
        def _():
            dn_ref[...] = jnp.zeros_like(dn_ref)

        @pl.when(j == 0)
        def _():
            acc_ref[...] = jnp.zeros_like(acc_ref)

        acc_ref[...] += _dot(da_ref[...], wg_ref[...]) + _dot(db_ref[...], wu_ref[...])

        @pl.when(j == nj - 1)
        def _():
            xv = x_ref[...]
            r = lax.rsqrt(jnp.mean(xv * xv, axis=-1, keepdims=True) + RMS_EPS)
            xh = xv * r
            dh = acc_ref[...]
            dn_ref[...] += jnp.sum(dh * xh, axis=0, keepdims=True)
            dhn = dh * n_ref[...]
            dx_ref[...] = dxo_ref[...] + r * (dhn - xh * jnp.mean(dhn * xh, axis=-1, keepdims=True))

    act = pl.BlockSpec((tm, tf), lambda i, j: (i, j))
    row = pl.BlockSpec((tm, d), lambda i, j: (i, 0))
    return _pallas(
        body, name, (t // tm, nj),
        [act, act, row, row, pl.BlockSpec((1, d), lambda i, j: (0, 0)),
         pl.BlockSpec((tf, d), lambda i, j: (j, gq)), pl.BlockSpec((tf, d), lambda i, j: (j, gq + 1))],
        [row, pl.BlockSpec((1, d), lambda i, j: (0, 0))],
        [jax.ShapeDtypeStruct((t, d), F32), jax.ShapeDtypeStruct((1, d), F32)],
        [pltpu.VMEM((tm, d), F32)], (da, db, dxo, x, n, cols_t, cols_t), ride)


def _wgrad(a, b, scale, tk, tn, name, ride=None, b_cols=None):
    t, k = a.shape
    q0, nq = (0, b.shape[1] // tn) if b_cols is None else b_cols
    n = nq * tn
    tt = min(t, WGRAD_TT)
    nt = t // tt

    def body(a_ref, b_ref, o_ref, acc_ref):
        s = pl.program_id(2)

        @pl.when(s == 0)
        def _():
            acc_ref[...] = jnp.zeros_like(acc_ref)

        acc_ref[...] += _dot_tn(a_ref[...], b_ref[...])

        @pl.when(s == nt - 1)
        def _():
            o_ref[...] = (scale * acc_ref[...]).astype(BF16)

    res = _pallas(
        body, name, (k // tk, n // tn, nt),
        [pl.BlockSpec((tt, tk), lambda p, q, s: (s, p)), pl.BlockSpec((tt, tn), lambda p, q, s: (s, q + q0))],
        [pl.BlockSpec((tk, tn), lambda p, q, s: (p, q))], [jax.ShapeDtypeStruct((k, n), BF16)],
        [pltpu.VMEM((tk, tn), F32)], (a, b), ride)
    return res[0] if ride is None else (res[0][0], res[1])


def _mix_in(x, n, w_in, cos_t, sin_t, seq, name):
    t, d = x.shape
    per_seq = seq // TM

    def body(x_ref, n_ref, w_ref, c_ref, s_ref, h_ref, q_ref, k_ref, v_ref, g_ref, u_ref):
        xv = x_ref[...]
        r = lax.rsqrt(jnp.mean(xv * xv, axis=-1, keepdims=True) + RMS_EPS)
        h = (xv * r * n_ref[...]).astype(BF16)
        h_ref[...] = h
        p = _dot(h, w_ref[...])
        c, s = c_ref[...], s_ref[...]
        q = p[:, :QK_W]
        k = p[:, QK_W:2 * QK_W]
        q_ref[...] = ((q * c + _swap_halves(q) * s) * (DK ** -0.5)).astype(BF16)
        k_ref[...] = (k * c + _swap_halves(k) * s).astype(BF16)
        v_ref[...] = p[:, 2 * QK_W:2 * QK_W + V_W].astype(BF16)
        g_ref[...] = p[:, 2 * QK_W + V_W:2 * QK_W + 2 * V_W]
        u_ref[...] = p[:, 2 * QK_W + 2 * V_W:]

    tile = lambda w: pl.BlockSpec((TM, w), lambda i: (i, 0))
    return pl.pallas_call(
        body, name=name, grid=(t // TM,),
        in_specs=[tile(d), pl.BlockSpec((1, d), lambda i: (0, 0)), pl.BlockSpec(w_in.shape, lambda i: (0, 0)),
                  pl.BlockSpec((TM, QK_W), lambda i: (i % per_seq, 0)), pl.BlockSpec((TM, QK_W), lambda i: (i % per_seq, 0))],
        out_specs=[tile(d), tile(QK_W), tile(QK_W), tile(V_W), tile(V_W), tile(POOL_W)],
        out_shape=[jax.ShapeDtypeStruct((t, d), BF16), jax.ShapeDtypeStruct((t, QK_W), BF16),
                   jax.ShapeDtypeStruct((t, QK_W), BF16), jax.ShapeDtypeStruct((t, V_W), BF16),
                   jax.ShapeDtypeStruct((t, V_W), F32), jax.ShapeDtypeStruct((t, POOL_W), F32)],
        compiler_params=_cparams(1),
    )(x, n, w_in, cos_t, sin_t)


def _mix_in_bwd(dp, dx2, x1, n, w_in_t, name, ride=None):
    t, d = x1.shape

    def body(dp_ref, dx2_ref, x_ref, n_ref, w_ref, dx_ref, dn_ref, dxb_ref):
        @pl.when(pl.program_id(0) == 0)
        def _():
            dn_ref[...] = jnp.zeros_like(dn_ref)

        dh = _dot(dp_ref[...], w_ref[...])
        xv = x_ref[...]
        r = lax.rsqrt(jnp.mean(xv * xv, axis=-1, keepdims=True) + RMS_EPS)
        xh = xv * r
        dn_ref[...] += jnp.sum(dh * xh, axis=0, keepdims=True)
        dhn = dh * n_ref[...]
        dx = dx2_ref[...] + r * (dhn - xh * jnp.mean(dhn * xh, axis=-1, keepdims=True))
        dx_ref[...] = dx
        dxb_ref[...] = (0.5 * dx).astype(BF16)

    tile = lambda w: pl.BlockSpec((TM, w), lambda i: (i, 0))
    return _pallas(
        body, name, (t // TM,),
        [tile(dp.shape[1]), tile(d), tile(d), pl.BlockSpec((1, d), lambda i: (0, 0)),
         pl.BlockSpec(w_in_t.shape, lambda i: (0, 0))],
        [tile(d), pl.BlockSpec((1, d), lambda i: (0, 0)), tile(d)],
        [jax.ShapeDtypeStruct((t, d), F32), jax.ShapeDtypeStruct((1, d), F32), jax.ShapeDtypeStruct((t, d), BF16)],
        [], (dp, dx2, x1, n, w_in_t), ride)


def _group_norm(o):
    parts, rstds = [], []
    for h in range(HEADS):
        oh = o[:, h * DV:(h + 1) * DV]
        dlt = oh - jnp.mean(oh, axis=-1, keepdims=True)
        rstd = lax.rsqrt(jnp.mean(dlt * dlt, axis=-1, keepdims=True) + GN_EPS)
        parts.append(dlt * rstd)
        rstds.append(rstd)
    return jnp.concatenate(parts, axis=1), rstds


def _mix_core_fwd(qs, k, v, g, u, x1, consts, gain, wp, scale, w_out, nseq, seq, name):
    t, d = x1.shape
    nblk = seq // BLK
    mask, dq, dk, gbd, bd = consts

    def body(q_ref, k_ref, v_ref, g_ref, u_ref, x1_ref, m_ref, dq_ref, dk_ref, gbd_ref, bd_ref, gain_ref, wp_ref,
             sc_ref, wo_ref, x2_ref, mix_ref, o_ref, pooled_ref, st_ref, state, halo):
        j = pl.program_id(1)

        @pl.when(j == 0)
        def _():
            state[...] = jnp.zeros_like(state)
            halo[...] = jnp.zeros_like(halo)

        qv, kv, vv = q_ref[...], k_ref[...], v_ref[...]
        st = state[...]
        st_ref[0] = st
        cross = _dot(qv, st.astype(BF16)) * dq_ref[...]
        outs = []
        for h in range(HEADS):
            qh = jnp.where(_head_mask(h), qv, jnp.zeros_like(qv))
            am = (_dot_nt(qh, kv) * m_ref[h]).astype(BF16)
            outs.append(_dot(am, vv[:, h * DV:(h + 1) * DV]))
        o = jnp.concatenate(outs, axis=1) + cross
        o_ref[...] = o
        kd = (kv.astype(F32) * dk_ref[...]).astype(BF16)
        state[...] = gbd_ref[...] * st + _dot_tn(kd, vv) * bd_ref[...]

        gv = g_ref[...]
        nrm, _ = _group_norm(o)
        ret = (gv * _sigmoid(gv)) * (nrm * gain_ref[...])

        uv = u_ref[...]
        c = jnp.concatenate([halo[...], uv], axis=0)
        halo[...] = uv[BLK - HALO:, :]
        pos = j * BLK + lax.broadcasted_iota(jnp.int32, (BLK, 1), 0)
        parts = []
        for gi, w in enumerate(WINDOWS):
            c = c + pltpu.roll(c, w // 2, 0)
            cnt = jnp.minimum(pos + 1, w).astype(F32)
            parts.append(c[HALO:, :GC] / cnt)
            if gi + 1 < len(WINDOWS):
                c = c[:, GC:]
        pooled = (jnp.concatenate(parts, axis=1) - uv).astype(BF16)
        pooled_ref[...] = pooled
        z = jnp.concatenate([_dot(pooled[:, gi * GC:(gi + 1) * GC], wp_ref[gi]) for gi in range(len(WINDOWS))], axis=1)
        mix = jnp.concatenate([ret, z * sc_ref[...]], axis=1).astype(BF16)
        mix_ref[...] = mix
        x2_ref[...] = x1_ref[...] + _dot(mix, wo_ref[...])

    blk = lambda w: pl.BlockSpec((BLK, w), lambda i, j: (i * nblk + j, 0))
    full = lambda a: pl.BlockSpec(a.shape, lambda i, j: (0,) * a.ndim)
    return _pallas(
        body, name, (nseq, nblk),
        [blk(QK_W), blk(QK_W), blk(V_W), blk(V_W), blk(POOL_W), blk(d),
         full(mask), full(dq), full(dk), full(gbd), full(bd), full(gain), full(wp), full(scale), full(w_out)],
        [blk(d), blk(d), blk(V_W), blk(POOL_W), pl.BlockSpec((1, QK_W, V_W), lambda i, j: (i * nblk + j, 0, 0))],
        [jax.ShapeDtypeStruct((t, d), F32), jax.ShapeDtypeStruct((t, d), BF16),
         jax.ShapeDtypeStruct((t, V_W), F32), jax.ShapeDtypeStruct((t, POOL_W), BF16),
         jax.ShapeDtypeStruct((nseq * nblk, QK_W, V_W), F32)],
        [pltpu.VMEM((QK_W, V_W), F32), pltpu.VMEM((HALO, POOL_W), F32)],
        (qs, k, v, g, u, x1, mask, dq, dk, gbd, bd, gain, wp, scale, w_out))


def _mix_core_bwd(dx2, qs, k, v, g, o, pooled, st, consts, gain, wp, scale, w_out, cos_t, sin_t, nseq, seq, name,
                  ride=None):
    t, d = dx2.shape
    nblk = seq // BLK
    mask, dq, dk, gbd, bd = consts
    n_win = len(WINDOWS)

    def body(dx2_ref, q_ref, k_ref, v_ref, g_ref, o_ref, pooled_ref, st_ref, m_ref, dq_ref, dk_ref, gbd_ref, bd_ref,
             gain_ref, wp_ref, sc_ref, wo_ref, c_ref, s_ref,
             dp_ref, dx2b_ref, dgain_ref, dscale_ref, dwp_ref, rstate, carry):
        i, j = pl.program_id(0), pl.program_id(1)

        @pl.when((i == 0) & (j == 0))
        def _():
            dgain_ref[...] = jnp.zeros_like(dgain_ref)
            dscale_ref[...] = jnp.zeros_like(dscale_ref)
            dwp_ref[...] = jnp.zeros_like(dwp_ref)

        @pl.when(j == 0)
        def _():
            rstate[...] = jnp.zeros_like(rstate)
            carry[...] = jnp.zeros_like(carry)

        dx2b = dx2_ref[...].astype(BF16)
        dx2b_ref[...] = dx2b
        dmix = _dot(dx2b, wo_ref[...])
        dret, dpool = dmix[:, :V_W], dmix[:, V_W:]

        gv, ov, gain_v = g_ref[...], o_ref[...], gain_ref[...]
        sg = _sigmoid(gv)
        sil = gv * sg
        nrm, rstds = _group_norm(ov)
        dg = dret * (nrm * gain_v) * (sg * (1.0 + gv * (1.0 - sg)))
        dgn = dret * sil
        dgain_ref[...] += jnp.sum(dgn * nrm, axis=0, keepdims=True)
        dnrm = dgn * gain_v
        do_parts = []
        for h in range(HEADS):
            dn_h = dnrm[:, h * DV:(h + 1) * DV]
            n_h = nrm[:, h * DV:(h + 1) * DV]
            do_parts.append(rstds[h] * (dn_h - jnp.mean(dn_h, axis=-1, keepdims=True)
                                        - n_h * jnp.mean(dn_h * n_h, axis=-1, keepdims=True)))
        do = jnp.concatenate(do_parts, axis=1)
        dob = do.astype(BF16)

        qv, kv, vv = q_ref[...], k_ref[...], v_ref[...]
        stb = st_ref[0].astype(BF16)
        rs = rstate[...]
        rsb = rs.astype(BF16)
        dod = (do * dq_ref[...]).astype(BF16)
        dqs = _dot_nt(dod, stb)
        dst = _dot_tn(qv, dod) * bd_ref[...]
        dkf = dk_ref[...]
        kd = (kv.astype(F32) * dkf).astype(BF16)
        dks = _dot_nt(vv, rsb) * dkf
        dvs = _dot(kd, rsb)
        dv_parts = []
        for h in range(HEADS):
            hm = _head_mask(h)
            qh = jnp.where(hm, qv, jnp.zeros_like(qv))
            kh = jnp.where(hm, kv, jnp.zeros_like(kv))
            mh = m_ref[h]
            am = (_dot_nt(qh, kv) * mh).astype(BF16)
            dpm = (_dot_nt(dob[:, h * DV:(h + 1) * DV], vv[:, h * DV:(h + 1) * DV]) * mh).astype(BF16)
            dqs = dqs + _dot(dpm, kh)
            dks = dks + _dot_tn(dpm, qh)
            dv_parts.append(_dot_tn(am, dob[:, h * DV:(h + 1) * DV]))
        dvs = dvs + jnp.concatenate(dv_parts, axis=1)
        rstate[...] = dst + gbd_ref[...] * rs

        cv, sv = c_ref[...], s_ref[...]
        dqr = dqs * (DK ** -0.5)
        dq_pre = dqr * cv + _swap_halves(dqr * sv)
        dk_pre = dks * cv + _swap_halves(dks * sv)

        pv = pooled_ref[...]
        sc = sc_ref[...]
        dzb = (dpool * sc).astype(BF16)
        z_parts, dpo_parts = [], []
        for gi in range(n_win):
            p_g = pv[:, gi * GC:(gi + 1) * GC]
            dz_g = dzb[:, gi * GC:(gi + 1) * GC]
            z_parts.append(_dot(p_g, wp_ref[gi]))
            dwp_ref[gi] += _dot_tn(p_g, dz_g)
            dpo_parts.append(_dot_nt(dz_g, wp_ref[gi]))
        dscale_ref[...] += jnp.sum(dpool * jnp.concatenate(z_parts, axis=1), axis=0, keepdims=True)
        dpo = jnp.concatenate(dpo_parts, axis=1)
        pos = (nblk - 1 - j) * BLK + lax.broadcasted_iota(jnp.int32, (BLK, 1), 0)
        e = jnp.concatenate(
            [dpo[:, gi * GC:(gi + 1) * GC] / jnp.minimum(pos + 1, w).astype(F32) for gi, w in enumerate(WINDOWS)], axis=1)
        c = jnp.concatenate([e, carry[...]], axis=0)
        carry[...] = e[:HALO, :]
        rows = BLK + HALO
        lead = []
        for gi, w in enumerate(WINDOWS):
            c = c + pltpu.roll(c, rows - w // 2, 0)
            lead.append(c[:BLK, :GC])
            if gi + 1 < n_win:
                c = c[:, GC:]
        du = jnp.concatenate(lead, axis=1) - dpo

        dp_ref[:, 0:QK_W] = dq_pre.astype(BF16)
        dp_ref[:, QK_W:2 * QK_W] = dk_pre.astype(BF16)
        dp_ref[:, 2 * QK_W:2 * QK_W + V_W] = dvs.astype(BF16)
        dp_ref[:, 2 * QK_W + V_W:2 * QK_W + 2 * V_W] = dg.astype(BF16)
        dp_ref[:, 2 * QK_W + 2 * V_W:] = du.astype(BF16)

    rev = lambda i, j: i * nblk + (nblk - 1 - j)
    blk = lambda w: pl.BlockSpec((BLK, w), lambda i, j: (rev(i, j), 0))
    full = lambda a: pl.BlockSpec(a.shape, lambda i, j: (0,) * a.ndim)
    in_w = 2 * QK_W + 2 * V_W + POOL_W
    return _pallas(
        body, name, (nseq, nblk),
        [blk(d), blk(QK_W), blk(QK_W), blk(V_W), blk(V_W), blk(V_W), blk(POOL_W),
         pl.BlockSpec((1, QK_W, V_W), lambda i, j: (rev(i, j), 0, 0)),
         full(mask), full(dq), full(dk), full(gbd), full(bd), full(gain), full(wp), full(scale), full(w_out),
         pl.BlockSpec((BLK, QK_W), lambda i, j: (nblk - 1 - j, 0)),
         pl.BlockSpec((BLK, QK_W), lambda i, j: (nblk - 1 - j, 0))],
        [blk(in_w), blk(d), pl.BlockSpec((1, V_W), lambda i, j: (0, 0)),
         pl.BlockSpec((1, POOL_W), lambda i, j: (0, 0)), pl.BlockSpec((n_win, GC, GC), lambda i, j: (0, 0, 0))],
        [jax.ShapeDtypeStruct((t, in_w), BF16), jax.ShapeDtypeStruct((t, d), BF16),
         jax.ShapeDtypeStruct((1, V_W), F32), jax.ShapeDtypeStruct((1, POOL_W), F32),
         jax.ShapeDtypeStruct((n_win, GC, GC), F32)],
        [pltpu.VMEM((QK_W, V_W), F32), pltpu.VMEM((HALO, POOL_W), F32)],
        (dx2, qs, k, v, g, o, pooled, st, mask, dq, dk, gbd, bd, gain, wp, scale, w_out, cos_t, sin_t), ride)


def _loss_head(x3, nf, tgt, name):
    t, d = x3.shape

    def body(x_ref, n_ref, t_ref, dx_ref, dn_ref, loss_ref, dxb_ref):
        @pl.when(pl.program_id(0) == 0)
        def _():
            dn_ref[...] = jnp.zeros_like(dn_ref)
            loss_ref[...] = jnp.zeros_like(loss_ref)

        xv = x_ref[...]
        nv = n_ref[...]
        r = lax.rsqrt(jnp.mean(xv * xv, axis=-1, keepdims=True) + RMS_EPS)
        xh = xv * r
        err = xh * nv - t_ref[...]
        row = jnp.mean(err * err, axis=-1, keepdims=True)
        loss_ref[...] += 0.5 * jnp.sum(row, axis=0, keepdims=True)
        dy = err * (1.0 / d)
        dn_ref[...] += jnp.sum(dy * xh, axis=0, keepdims=True)
        dxh = dy * nv
        dx = r * (dxh - xh * jnp.mean(dxh * xh, axis=-1, keepdims=True))
        dx_ref[...] = dx
        dxb_ref[...] = (0.5 * dx).astype(BF16)

    tile = pl.BlockSpec((TM, d), lambda i: (i, 0))
    return pl.pallas_call(
        body, name=name, grid=(t // TM,),
        in_specs=[tile, pl.BlockSpec((1, d), lambda i: (0, 0)), tile],
        out_specs=[tile, pl.BlockSpec((1, d), lambda i: (0, 0)), pl.BlockSpec((1, 1), lambda i: (0, 0)), tile],
        out_shape=[jax.ShapeDtypeStruct((t, d), F32), jax.ShapeDtypeStruct((1, d), F32), jax.ShapeDtypeStruct((1, 1), F32),
                   jax.ShapeDtypeStruct((t, d), BF16)],
        compiler_params=_cparams(1),
    )(x3, nf, tgt)


def _coords():
    return lax.axis_index("x"), lax.axis_index("y"), lax.axis_index("c")


def _window(ref, kind, idx, size):
    if kind == "col":
        return ref.at[:, pl.ds(pl.multiple_of(idx * size, LANE), size)]
    return ref.at[pl.ds(pl.multiple_of(idx * size, 8), size), :]


def _run_exchange(ex, name):
    n_in = len(ex.inputs)

    def body(*refs):
        ins, outs, sems = refs[:n_in], refs[n_in:n_in + len(ex.out_shape)], refs[n_in + len(ex.out_shape):]
        ex.start(ins, outs, sems)
        if ex.mid is not None:
            ex.mid(ins, outs, sems)
        ex.finish(ins, outs, sems)

    return pl.pallas_call(body, name=name, in_specs=[ANY] * n_in, out_specs=[ANY] * len(ex.out_shape),
                          out_shape=ex.out_shape, scratch_shapes=ex.scratch)(*ex.inputs)


def _join(exchanges):
    bounds = []
    i0 = o0 = s0 = 0
    for ex in exchanges:
        bounds.append((i0, o0, s0))
        i0, o0, s0 = i0 + len(ex.inputs), o0 + len(ex.out_shape), s0 + len(ex.scratch)

    def phase(which):
        def run(ins, outs, sems):
            for ex, (i, o, s) in zip(exchanges, bounds):
                fn = getattr(ex, which)
                if fn is not None:
                    fn(ins[i:i + len(ex.inputs)], outs[o:o + len(ex.out_shape)], sems[s:s + len(ex.scratch)])
        return run

    return _Exchange(sum((ex.inputs for ex in exchanges), []), sum((ex.out_shape for ex in exchanges), []),
                     sum((ex.scratch for ex in exchanges), []), phase("start"), phase("finish"),
                     phase("mid") if any(ex.mid is not None for ex in exchanges) else None)


def _gather_exchange(parts):
    n = len(parts)
    kinds = [kd for _, kd in parts]
    sizes = [a.shape[1] if kd == "col" else a.shape[0] for a, kd in parts]

    def plan(ins, outs, sems):
        send_sems, recv_sems, local_sems = sems
        x, y, c = _coords()
        me, sibling = (x, y, c), (x, y, 1 - c)
        chips = [(1 - x, y), (x, 1 - y), (1 - x, 1 - y)]

        def win(p, dev):
            return _window(outs[p], kinds[p], 4 * dev[0] + 2 * dev[1] + dev[2], sizes[p])

        def copy(p, k, block, to, src=None):
            return pltpu.make_async_remote_copy(
                src_ref=win(p, block) if src is None else src, dst_ref=win(p, block),
                send_sem=send_sems.at[p * 7 + k], recv_sem=recv_sems.at[p * 7 + k], device_id=to, device_id_type=MESH_ID)

        mine = [pltpu.make_async_copy(ins[p], win(p, me), local_sems.at[p]) for p in range(n)]
        first, arrived, passed, rest = [], [], [], []
        for p in range(n):
            first.append(copy(p, 0, me, sibling, src=ins[p]))
            first += [copy(p, 1 + q, me, (*chip, c), src=ins[p]) for q, chip in enumerate(chips)]
            rest.append(copy(p, 0, sibling, me))
            rest += [copy(p, 4 + q, (*chip, 1 - c), me) for q, chip in enumerate(chips)]
        for q, chip in enumerate(chips):
            for p in range(n):
                arrived.append(copy(p, 1 + q, (*chip, c), me))
                passed.append(copy(p, 4 + q, (*chip, c), sibling))
        return mine, first, arrived, passed, rest

    def start(ins, outs, sems):
        mine, first, _, _, _ = plan(ins, outs, sems)
        for cp in mine + first:
            cp.start()

    def mid(ins, outs, sems):
        _, _, arrived, passed, _ = plan(ins, outs, sems)
        for got, fwd in zip(arrived, passed):
            got.wait_recv()
            fwd.start()

    def finish(ins, outs, sems):
        mine, first, _, passed, rest = plan(ins, outs, sems)
        for cp in rest:
            cp.wait_recv()
        for cp in first + passed:
            cp.wait_send()
        for cp in mine:
            cp.wait()

    out_shape = [jax.ShapeDtypeStruct((a.shape[0], N_DEV * a.shape[1]) if kd == "col" else (N_DEV * a.shape[0], a.shape[1]),
                                      a.dtype) for a, kd in parts]
    scratch = [pltpu.SemaphoreType.DMA((7 * n,)), pltpu.SemaphoreType.DMA((7 * n,)), pltpu.SemaphoreType.DMA((n,))]
    return _Exchange([a for a, _ in parts], out_shape, scratch, start, finish, mid)


def _all_gather(parts, name):
    return _run_exchange(_gather_exchange(parts), name)


def _shard_shape(a, kd):
    return (a.shape[0], a.shape[1] // N_DEV) if kd == "col" else (a.shape[0] // N_DEV, a.shape[1])


def _symmetric_exchange(inputs, out_shape, n_copies, plan):
    def start(ins, outs, sems):
        for cp in plan(ins, outs, sems):
            cp.start()

    def finish(ins, outs, sems):
        copies = plan(ins, outs, sems)
        for cp in copies:
            cp.wait_recv()
        for cp in copies:
            cp.wait_send()

    scratch = [pltpu.SemaphoreType.DMA((n_copies,)), pltpu.SemaphoreType.DMA((n_copies,))]
    return _Exchange(inputs, out_shape, scratch, start, finish)


def _rs_pair_exchange(grads):
    n = len(grads)
    kinds = [kd for _, kd in grads]
    shapes = [_shard_shape(a, kd) for a, kd in grads]

    def plan(ins, outs, sems):
        send_sems, recv_sems = sems
        x, y, c = _coords()
        copies = []
        for p in range(n):
            size = shapes[p][1] if kinds[p] == "col" else shapes[p][0]
            for s in range(4):
                src = _window(ins[p], kinds[p], 2 * s + (1 - c), size)
                copies.append(pltpu.make_async_remote_copy(
                    src_ref=src, dst_ref=outs[p].at[s], send_sem=send_sems.at[4 * p + s], recv_sem=recv_sems.at[4 * p + s],
                    device_id=(x, y, 1 - c), device_id_type=MESH_ID))
        return copies

    return _symmetric_exchange([a for a, _ in grads], [jax.ShapeDtypeStruct((4,) + shapes[p], BF16) for p in range(n)],
                               4 * n, plan)


def _rs_chips_exchange(sums):
    n = len(sums)

    def plan(ins, outs, sems):
        send_sems, recv_sems = sems
        x, y, c = _coords()
        chips = [(1 - x, y), (x, 1 - y), (1 - x, 1 - y)]
        copies = []
        for p in range(n):
            for q, (cx, cy) in enumerate(chips):
                copies.append(pltpu.make_async_remote_copy(
                    src_ref=ins[p].at[2 * cx + cy], dst_ref=outs[p].at[q],
                    send_sem=send_sems.at[3 * p + q], recv_sem=recv_sems.at[3 * p + q],
                    device_id=(cx, cy, c), device_id_type=MESH_ID))
        return copies

    return _symmetric_exchange(list(sums), [jax.ShapeDtypeStruct((3,) + a.shape[1:], BF16) for a in sums], 3 * n, plan)


def _pair_sum(grad, kd, recv, core, name):
    _, r, cw = recv.shape
    tr = min(r, TM)

    def body(core_ref, g_ref, r_ref, o_ref):
        del core_ref
        o_ref[0] = (g_ref[...].astype(F32) + r_ref[0].astype(F32)).astype(BF16)

    if kd == "col":
        g_spec = pl.BlockSpec((tr, cw), lambda s, i, core_ref: (i, 2 * s + core_ref[0]))
    else:
        g_spec = pl.BlockSpec((tr, cw), lambda s, i, core_ref: ((2 * s + core_ref[0]) * (r // tr) + i, 0))
    grid_spec = pltpu.PrefetchScalarGridSpec(
        num_scalar_prefetch=1, grid=(4, r // tr),
        in_specs=[g_spec, pl.BlockSpec((1, tr, cw), lambda s, i, core_ref: (s, i, 0))],
        out_specs=pl.BlockSpec((1, tr, cw), lambda s, i, core_ref: (s, i, 0)))
    return pl.pallas_call(
        body, name=name, grid_spec=grid_spec, out_shape=jax.ShapeDtypeStruct(recv.shape, BF16),
        compiler_params=_cparams(2),
    )(core, grad, recv)


def _adam_math(w, g, m, v):
    m2 = B1 * m + (1.0 - B1) * g
    v2 = B2 * v + (1.0 - B2) * (g * g)
    m_hat = m2 / (1.0 - B1 ** STEP)
    v_hat = v2 / (1.0 - B2 ** STEP)
    delta = -LR * (m_hat / (jnp.sqrt(v_hat) + ADAM_EPS) + WD * w)
    return delta, m2, v2


def _chip_sum_adam(items, chip, tr, name, ride=None):
    r = items[0][1].shape[0]
    steps = r // tr
    n_parts = [len(parts) for parts, _, _, _ in items]
    r_in = 0 if ride is None else len(ride.inputs)
    r_out = 0 if ride is None else len(ride.out_shape)
    n_in = sum(2 * k + 3 for k in n_parts)
    n_out = 4 * len(items)

    def body(chip_ref, *refs):
        del chip_ref
        ins, refs = refs[:n_in], refs[n_in:]
        r_ins, refs = refs[:r_in], refs[r_in:]
        outs, refs = refs[:n_out], refs[n_out:]
        r_outs, sems = refs[:r_out], refs[r_out:]
        i = pl.program_id(0)
        if ride is not None:
            @pl.when(i == 0)
            def _():
                ride.start(r_ins, r_outs, sems)

        pos = 0
        for q, (k, (_, w, _, _)) in enumerate(zip(n_parts, items)):
            cols = []
            for _ in range(k):
                p_ref, c_ref = ins[pos], ins[pos + 1]
                pos += 2
                cols.append(p_ref[0].astype(F32) + c_ref[0].astype(F32) + c_ref[1].astype(F32) + c_ref[2].astype(F32))
            g = (cols[0] if k == 1 else jnp.concatenate(cols, axis=1))[:, :w.shape[1]]
            w_ref, m_ref, v_ref = ins[pos:pos + 3]
            pos += 3
            delta, m2, v2 = _adam_math(w_ref[...], g, m_ref[...], v_ref[...])
            outs[4 * q][...] = g
            outs[4 * q + 1][...] = delta
            outs[4 * q + 2][...] = m2
            outs[4 * q + 3][...] = v2

        if ride is not None:
            @pl.when(i == steps - 1)
            def _():
                ride.finish(r_ins, r_outs, sems)

    in_specs, args, out_specs, out_shape = [], [], [], []
    for parts, w, m, v in items:
        for psum, recv in parts:
            pc = psum.shape[2]
            in_specs += [pl.BlockSpec((1, tr, pc), lambda i, chip_ref: (chip_ref[0], i, 0)),
                         pl.BlockSpec((3, tr, pc), lambda i, chip_ref: (0, i, 0))]
            args += [psum, recv]
        loc = pl.BlockSpec((tr, w.shape[1]), lambda i, chip_ref: (i, 0))
        in_specs += [loc] * 3
        args += [w, m, v]
        out_specs += [loc] * 4
        out_shape += [jax.ShapeDtypeStruct(w.shape, F32)] * 4
    grid_spec = pltpu.PrefetchScalarGridSpec(
        num_scalar_prefetch=1, grid=(steps,), in_specs=in_specs + [ANY] * r_in, out_specs=out_specs + [ANY] * r_out,
        scratch_shapes=[] if ride is None else ride.scratch)
    res = pl.pallas_call(
        body, name=name, grid_spec=grid_spec, out_shape=out_shape + ([] if ride is None else ride.out_shape),
        compiler_params=_cparams(1),
    )(chip, *args, *([] if ride is None else ride.inputs))
    return res if ride is None else (res[:n_out], res[n_out:])


def _small_allreduce_adam(partials, params, moms, vels, plain, name, ride=None):
    n, n_plain = len(partials), len(plain)
    summed = list(partials) + list(plain)
    row0 = []
    rows = 0
    for a in summed:
        if a.shape[0] >= 8:
            rows = _pad_to(rows, 8)
        row0.append(rows)
        rows += a.shape[0]
    rows = _pad_to(rows, 8)
    width = max(a.shape[1] for a in summed)
    r_in = 0 if ride is None else len(ride.inputs)
    r_out = 0 if ride is None else len(ride.out_shape)
    n_out = 4 * n + n_plain

    def body(*refs):
        w_in, m_in, v_in = refs[0:n], refs[n:2 * n], refs[2 * n:3 * n]
        g_in, refs = refs[3 * n:4 * n + n_plain], refs[4 * n + n_plain:]
        r_ins, refs = refs[:r_in], refs[r_in:]
        outs, refs = refs[:n_out], refs[n_out:]
        r_outs, refs = refs[:r_out], refs[r_out:]
        pair, chips, send_sems, recv_sems = refs[:4]
        if ride is not None:
            ride.start(r_ins, r_outs, refs[4:])
        x, y, c = _coords()
        chip = 2 * x + y
        pair[c] = jnp.zeros((rows, width), F32)
        for p, a in enumerate(summed):
            r, cw = a.shape
            pair[c, row0[p]:row0[p] + r, 0:cw] = g_in[p][...]
        swap = pltpu.make_async_remote_copy(src_ref=pair.at[c], dst_ref=pair.at[c], send_sem=send_sems.at[0],
                                            recv_sem=recv_sems.at[0], device_id=(x, y, 1 - c), device_id_type=MESH_ID)
        swap.start()
        swap.wait_recv()
        swap.wait_send()
        chips[chip] = pair[0] + pair[1]
        copies = [pltpu.make_async_remote_copy(
            src_ref=chips.at[chip], dst_ref=chips.at[chip], send_sem=send_sems.at[1 + q], recv_sem=recv_sems.at[1 + q],
            device_id=(cx, cy, c), device_id_type=MESH_ID) for q, (cx, cy) in enumerate([(1 - x, y), (x, 1 - y), (1 - x, 1 - y)])]
        for cp in copies:
            cp.start()
        for cp in copies:
            cp.wait_recv()
        for cp in copies:
            cp.wait_send()
        for p, a in enumerate(summed):
            r, cw = a.shape
            g = chips[0, row0[p]:row0[p] + r, 0:cw]
            for q in range(1, 4):
                g = g + chips[q, row0[p]:row0[p] + r, 0:cw]
            if p >= n:
                outs[4 * n + p - n][...] = g
                continue
            delta, m2, v2 = _adam_math(w_in[p][...], g, m_in[p][...], v_in[p][...])
            outs[4 * p][...] = g
            outs[4 * p + 1][...] = delta
            outs[4 * p + 2][...] = m2
            outs[4 * p + 3][...] = v2
        if ride is not None:
            ride.finish(r_ins, r_outs, refs[4:])

    out_shape = []
    for a in partials:
        out_shape += [jax.ShapeDtypeStruct(a.shape, F32)] * 4
    out_shape += [jax.ShapeDtypeStruct(a.shape, F32) for a in plain]
    res = pl.pallas_call(
        body, name=name, in_specs=[VMEM_SPEC] * (4 * n + n_plain) + [ANY] * r_in,
        out_specs=[VMEM_SPEC] * n_out + [ANY] * r_out, out_shape=out_shape + ([] if ride is None else ride.out_shape),
        scratch_shapes=[pltpu.VMEM((2, rows, width), F32), pltpu.VMEM((4, rows, width), F32),
                        pltpu.SemaphoreType.DMA((4,)), pltpu.SemaphoreType.DMA((4,))] + ([] if ride is None else ride.scratch),
    )(*params, *moms, *vels, *partials, *plain, *([] if ride is None else ride.inputs))
    return res if ride is None else (res[:n_out], res[n_out:])


def _local_step(xf, tgt, nseq, seq, cols1_all, later, small_w, core, small_step):
    d = xf.shape[1]
    n1, n2, gain, pool_w, pool_scale, n3, nf = small_w
    tf = 2 * cols1_all.shape[1] // N_DEV
    consts = _retention_constants()
    cos_t, sin_t = _rotary_tables(seq)
    wp_b = pool_w.astype(BF16)

    def pair_sums(grads, recv, names):
        return [_pair_sum(g, kd, r, core, "pair_sum_" + nm) for (g, kd), r, nm in zip(grads, recv, names)]

    def riding(host):
        return _gather_exchange(later[host])

    both = lambda first, second: _join([_rs_chips_exchange(first), _rs_pair_exchange([second])])

    (h1, b1, sil1, dsil1, s1), (d1_all, win_all, wout_all, gate2_all, up2_all) = _ffn_act(
        xf, n1, (cols1_all, 0), (cols1_all, 1), "ffn1_act", ride=riding("ffn1_act"))
    x1, (d2_all,) = _ffn_down(s1, xf, d1_all, "ffn1_down", ride=riding("ffn1_down"))
    h2, qs, kr, vv, gg, uu = _mix_in(x1, n2, win_all, cos_t, sin_t, seq, "mix_in")
    x2, mix, oo, pooled, states = _mix_core_fwd(qs, kr, vv, gg, uu, x1, consts, gain, wp_b, pool_scale, wout_all,
                                                 nseq, seq, "mix_core_fwd")
    (x3, h3, b3, sil3, dsil3), (cols2_t, d2_t, win_t, wout_t) = _ffn_fwd(
        x2, n3, (gate2_all, 0), (up2_all, 0), d2_all, "ffn2_fwd", ride=riding("ffn2_fwd"))
    dx3, dnf, loss_part, dx3b = _loss_head(x3, nf, tgt, "loss_head")
    out = {}

    (da3, db3, g_wd2), (cols1_t, d1_t) = _ffn_bwd_act(dx3b, b3, sil3, dsil3, d2_t, "ffn2_bwd_act",
                                                      ride=riding("ffn2_bwd_act"))
    names2 = ["ffn2_gate", "ffn2_up", "ffn2_down"]
    grads2 = [(_wgrad(da3, h3, 1.0, tf, d, "wgrad_gate2"), "row"), (_wgrad(db3, h3, 1.0, tf, d, "wgrad_up2"), "row"),
              (g_wd2, "row")]
    (dx2, dn3), recv2 = _ffn_bwd_in(da3, db3, dx3, x2, n3, cols2_t, 0, "ffn2_bwd_in", ride=_rs_pair_exchange(grads2))
    sums2 = pair_sums(grads2, recv2, names2)
    (dp, dx2b, dgain, dscale, dwp), crecv2 = _mix_core_bwd(
        dx2, qs, kr, vv, gg, oo, pooled, states, consts, gain, wp_b, pool_scale, wout_t, cos_t, sin_t, nseq, seq,
        "mix_core_bwd", ride=_rs_chips_exchange(sums2))
    out.update({nm: [(s, r)] for nm, s, r in zip(names2, sums2, crecv2)})

    names_m = ["w_in", "w_out"]
    grads_m = [(_wgrad(h2, dp, 1.0, d, d, "wgrad_in"), "col"), (_wgrad(mix, dx2b, 1.0, d, d, "wgrad_out"), "row")]
    (dx1, dn2, dx1b), recv_m = _mix_in_bwd(dp, dx2, x1, n2, win_t, "mix_in_bwd", ride=_rs_pair_exchange(grads_m))
    sums_m = pair_sums(grads_m, recv_m, names_m)
    (da1, db1, g_wd1), crecv_m = _ffn_bwd_act(dx1b, b1, sil1, dsil1, d1_t, "ffn1_bwd_act", ride=_rs_chips_exchange(sums_m))
    out.update({nm: [(s, r)] for nm, s, r in zip(names_m, sums_m, crecv_m)})

    dx0, dn1 = _ffn_bwd_in(da1, db1, dx1, xf, n1, cols1_t, 0, "ffn1_bwd_in")
    g_down = (g_wd1, "row")
    g_gate, recv_d = _wgrad(da1, h1, 1.0, tf, d, "wgrad_gate1", ride=_rs_pair_exchange([g_down]))
    g_gate = (g_gate, "row")
    sum_d = pair_sums([g_down], recv_d, ["ffn1_down"])
    g_lo, (crecv_d, recv_g) = _wgrad(db1, h1, 1.0, tf, d // 2, "wgrad_up1_lo", ride=both(sum_d, g_gate), b_cols=(0, 1))
    g_lo = (g_lo, "row")
    sum_g = pair_sums([g_gate], [recv_g], ["ffn1_gate"])
    g_hi, (crecv_g, recv_lo) = _wgrad(db1, h1, 1.0, tf, d // 2, "wgrad_up1_hi", ride=both(sum_g, g_lo), b_cols=(1, 1))
    g_hi = (g_hi, "row")
    sum_lo = pair_sums([g_lo], [recv_lo], ["ffn1_up_lo"])
    small_out, (crecv_lo, recv_hi) = small_step((dn1, dn2, dgain, dwp, dscale, dn3, dnf), loss_part, both(sum_lo, g_hi))
    sum_hi = pair_sums([g_hi], [recv_hi], ["ffn1_up_hi"])
    out.update({"ffn1_gate": [(sum_g[0], crecv_g)], "ffn1_down": [(sum_d[0], crecv_d)],
                "ffn1_up": [(sum_lo[0], crecv_lo), (sum_hi[0], None)]})
    return small_out[-1], dx0, out, small_out[:-1], _rs_chips_exchange(sum_hi)


def kernel(x, norm_ffn1, ffn1_gate, ffn1_up, ffn1_down, norm_mix, w_in, ret_gn_gain, pool_w, pool_scale, w_out, norm_ffn2, ffn2_gate, ffn2_up, ffn2_down, norm_final, loss_target, m_norm_ffn1, m_ffn1_gate, m_ffn1_up, m_ffn1_down, m_norm_mix, m_w_in, m_ret_gn_gain, m_pool_w, m_pool_scale, m_w_out, m_norm_ffn2, m_ffn2_gate, m_ffn2_up, m_ffn2_down, m_norm_final, v_norm_ffn1, v_ffn1_gate, v_ffn1_up, v_ffn1_down, v_norm_mix, v_w_in, v_ret_gn_gain, v_pool_w, v_pool_scale, v_w_out, v_norm_ffn2, v_ffn2_gate, v_ffn2_up, v_ffn2_down, v_norm_final):
    nseq, seq, d = x.shape
    t = nseq * seq
    f_loc = ffn1_gate.shape[2]
    f_pad = _pad_to(f_loc, LANE)
    xf = x.reshape(t, d)
    tgt = loss_target.reshape(t, d)
    core = lax.axis_index("c").astype(jnp.int32).reshape(1)
    chip = (2 * lax.axis_index("x") + lax.axis_index("y")).astype(jnp.int32).reshape(1)

    colp = lambda w: jnp.pad(w[0].astype(BF16), ((0, 0), (0, f_pad - f_loc)))
    rowp = lambda w: jnp.pad(w[0].astype(BF16), ((0, f_pad - f_loc), (0, 0)))
    gate2, up2 = colp(ffn2_gate), colp(ffn2_up)
    cols1 = jnp.concatenate([colp(ffn1_gate), colp(ffn1_up)], axis=0)
    cols2_t = jnp.concatenate([gate2.T, up2.T], axis=1)
    (cols1_all,) = _all_gather([(cols1, "col")], "all_gather_ffn1")
    d1_loc, d2_loc, win_loc, wout_loc = rowp(ffn1_down), rowp(ffn2_down), w_in[0].astype(BF16), w_out[0].astype(BF16)
    later = {"ffn1_act": [(d1_loc, "row"), (win_loc, "col"), (wout_loc, "row"), (gate2, "col"), (up2, "col")],
             "ffn1_down": [(d2_loc, "row")],
             "ffn2_fwd": [(cols2_t, "row"), (d2_loc.T, "col"), (win_loc.T, "row"), (wout_loc.T, "col")],
             "ffn2_bwd_act": [(cols1.T, "row"), (d1_loc.T, "col")]}

    flat = lambda a: a.reshape(pool_w.size // d, d)
    params = [norm_ffn1, norm_mix, ret_gn_gain, flat(pool_w), pool_scale, norm_ffn2, norm_final.reshape(1, d)]
    moms = [m_norm_ffn1, m_norm_mix, m_ret_gn_gain, flat(m_pool_w), m_pool_scale, m_norm_ffn2, m_norm_final.reshape(1, d)]
    vels = [v_norm_ffn1, v_norm_mix, v_ret_gn_gain, flat(v_pool_w), v_pool_scale, v_norm_ffn2, v_norm_final.reshape(1, d)]

    def small_step(parts, loss_part, ride):
        dn1, dn2, dgain, dwp, dscale, dn3, dnf = parts
        return _small_allreduce_adam([dn1, dn2, dgain, flat(dwp), dscale, dn3, dnf], params, moms, vels, [loss_part],
                                     "small_allreduce_adam", ride)

    small_w = (norm_ffn1, norm_mix, ret_gn_gain, pool_w[0], pool_scale, norm_ffn2, norm_final.reshape(1, d))
    loss_sum, dx0, reduced, small_out, pending = _local_step(xf, tgt, nseq, seq, cols1_all, later, small_w, core,
                                                             small_step)

    local = {"ffn1_gate": (ffn1_gate, m_ffn1_gate, v_ffn1_gate), "ffn1_up": (ffn1_up, m_ffn1_up, v_ffn1_up),
             "ffn1_down": (ffn1_down, m_ffn1_down, v_ffn1_down), "w_in": (w_in, m_w_in, v_w_in),
             "w_out": (w_out, m_w_out, v_w_out), "ffn2_gate": (ffn2_gate, m_ffn2_gate, v_ffn2_gate),
             "ffn2_up": (ffn2_up, m_ffn2_up, v_ffn2_up), "ffn2_down": (ffn2_down, m_ffn2_down, v_ffn2_down)}
    flip = lambda nm: nm.endswith("gate") or nm.endswith("up")

    def item(nm):
        view = (lambda a: a[0].T) if flip(nm) else (lambda a: a[0])
        w, m, v = local[nm]
        return reduced[nm], view(w), view(m), view(v)

    big = {}

    def keep(names, res):
        for q, nm in enumerate(names):
            big[nm] = tuple((a.T if flip(nm) else a)[None] for a in res[4 * q:4 * q + 4])

    second = ["ffn2_gate", "ffn2_up", "ffn2_down"]
    res, (last_recv,) = _chip_sum_adam([item(nm) for nm in second], chip, item(second[0])[1].shape[0] // 2, "adam_ffn2",
                                       ride=pending)
    keep(second, res)
    reduced["ffn1_up"][-1] = (reduced["ffn1_up"][-1][0], last_recv)
    first = ["ffn1_gate", "ffn1_up", "ffn1_down"]
    keep(first, _chip_sum_adam([item(nm) for nm in first], chip, item(first[0])[1].shape[0] // 2, "adam_ffn1"))
    for nm in ["w_in", "w_out"]:
        keep([nm], _chip_sum_adam([item(nm)], chip, min(item(nm)[1].shape[0], TM), "adam_" + nm))

    small_names = ["norm_ffn1", "norm_mix", "ret_gn_gain", "pool_w", "pool_scale", "norm_ffn2", "norm_final"]
    shapes = [norm_ffn1.shape, norm_mix.shape, ret_gn_gain.shape, pool_w.shape, pool_scale.shape, norm_ffn2.shape,
              norm_final.shape]
    small = {nm: tuple(small_out[4 * p + q].reshape(shapes[p]) for q in range(4)) for p, nm in enumerate(small_names)}

    loss = loss_sum[0, 0]
    order = ["norm_ffn1", "ffn1_gate", "ffn1_up", "ffn1_down", "norm_mix", "w_in", "ret_gn_gain", "pool_w", "pool_scale",
             "w_out", "norm_ffn2", "ffn2_gate", "ffn2_up", "ffn2_down", "norm_final"]
    both = {**big, **small}
    outs = [loss, dx0.reshape(nseq, seq, d)]
    for q in range(4):
        outs += [both[nm][q] for nm in order]
    return tuple(outs)
```

```python
import numpy as np
import jax
import jax.numpy as jnp
from jax import lax
from jax.experimental import pallas as pl
from jax.experimental.pallas import tpu as pltpu

F32, BF16 = jnp.float32, jnp.bfloat16
MESH_ID = pl.DeviceIdType.MESH
ANY = pl.BlockSpec(memory_space=pl.ANY)
VMEM_SPEC = pl.BlockSpec(memory_space=pltpu.VMEM)

N_DEV = 8
RMS_EPS = 1e-6
GN_EPS = 1e-5
HEADS, DK, DV = 4, 64, 128
QK_W, V_W, POOL_W = HEADS * DK, HEADS * DV, 512
WINDOWS = (2, 4, 8, 16)
GC = POOL_W // len(WINDOWS)
CHUNK = 64
BLK = 4 * CHUNK
MIX_BLOCKS = 2
HALO = 16
ROPE_BASE = 10000.0
LR, B1, B2, ADAM_EPS, WD, STEP = 0.001, 0.9, 0.999, 1e-08, 0.01, 10
LANE = 128
TM = 512
FFN_TM = 1024
FFN_FWD_TF = 768
WGRAD_TT = 4096
VMEM_LIMIT = 56 * 1024 * 1024


def _cparams(n_axes):
    return pltpu.CompilerParams(dimension_semantics=("arbitrary",) * n_axes, vmem_limit_bytes=VMEM_LIMIT)


class _Exchange:
    def __init__(self, inputs, out_shape, scratch, start, finish, mid=None):
        self.inputs, self.out_shape, self.scratch = list(inputs), list(out_shape), list(scratch)
        self.start, self.finish, self.mid = start, finish, mid


def _pallas(body, name, grid, in_specs, out_specs, out_shape, scratch_shapes, args, ride=None):
    n_axes = len(grid)
    if ride is None:
        return pl.pallas_call(body, name=name, grid=grid, in_specs=in_specs, out_specs=out_specs, out_shape=out_shape,
                              scratch_shapes=scratch_shapes, compiler_params=_cparams(n_axes))(*args)
    n_in, n_out, n_scr = len(in_specs), len(out_specs), len(scratch_shapes)
    r_in, r_out = len(ride.inputs), len(ride.out_shape)

    def hosted(*refs):
        ins, refs = refs[:n_in], refs[n_in:]
        r_ins, refs = refs[:r_in], refs[r_in:]
        outs, refs = refs[:n_out], refs[n_out:]
        r_outs, refs = refs[:r_out], refs[r_out:]
        scr, sems = refs[:n_scr], refs[n_scr:]
        ids = [pl.program_id(a) for a in range(n_axes)]
        first, last, inner0 = ids[0] == 0, ids[0] == grid[0] - 1, None
        for a in range(1, n_axes):
            first = first & (ids[a] == 0)
            last = last & (ids[a] == grid[a] - 1)
            inner0 = (ids[a] == 0) if inner0 is None else inner0 & (ids[a] == 0)

        @pl.when(first)
        def _():
            ride.start(r_ins, r_outs, sems)

        if ride.mid is not None:
            at_mid = ids[0] == grid[0] - 1
            if inner0 is not None:
                at_mid = at_mid & inner0

            @pl.when(at_mid)
            def _():
                ride.mid(r_ins, r_outs, sems)

        body(*ins, *outs, *scr)

        @pl.when(last)
        def _():
            ride.finish(r_ins, r_outs, sems)

    res = pl.pallas_call(
        hosted, name=name, grid=grid, in_specs=list(in_specs) + [ANY] * r_in, out_specs=list(out_specs) + [ANY] * r_out,
        out_shape=list(out_shape) + ride.out_shape, scratch_shapes=list(scratch_shapes) + ride.scratch,
        compiler_params=_cparams(n_axes))(*args, *ride.inputs)
    return res[:n_out], res[n_out:]


def _dot(a, b):
    return jnp.dot(a, b, preferred_element_type=F32)


def _dot_nt(a, b):
    return lax.dot_general(a, b, (((1,), (1,)), ((), ())), preferred_element_type=F32)


def _dot_tn(a, b):
    return lax.dot_general(a, b, (((0,), (0,)), ((), ())), preferred_element_type=F32)


def _sigmoid(x):
    return 0.5 * jnp.tanh(0.5 * x) + 0.5


def _pad_to(n, m):
    return (n + m - 1) // m * m


def _retention_constants():
    gamma = (1.0 - 2.0 ** (-5.0 - np.arange(HEADS, dtype=np.float32))).astype(np.float32)
    log_g = np.log(gamma).astype(np.float32)
    i = np.arange(BLK)
    diff = (i[:, None] - i[None, :]).astype(np.float32)
    same = (i[:, None] // CHUNK) == (i[None, :] // CHUNK)
    earlier = (i[None, :] // CHUNK) < (i[:, None] // CHUNK)
    mask = np.zeros((HEADS, BLK, BLK), np.float32)
    for h in range(HEADS):
        dec_abs = np.exp(log_g[h] * np.abs(diff)).astype(np.float32)
        dec = np.exp(log_g[h] * diff * earlier).astype(np.float32)
        mask[h] = np.where(same, dec_abs, np.where(earlier, dec, 0.0))
    dq = np.zeros((BLK, V_W), np.float32)
    dk = np.zeros((BLK, QK_W), np.float32)
    gbd = np.zeros((QK_W, V_W), np.float32)
    for h in range(HEADS):
        dq[:, h * DV:(h + 1) * DV] = np.exp(log_g[h] * (i + 1.0)).astype(np.float32)[:, None]
        dk[:, h * DK:(h + 1) * DK] = np.exp(log_g[h] * (BLK - 1.0 - i)).astype(np.float32)[:, None]
        gbd[h * DK:(h + 1) * DK, h * DV:(h + 1) * DV] = np.exp(log_g[h] * np.float32(BLK))
    bd = (gbd > 0).astype(np.float32)
    return jnp.asarray(mask), jnp.asarray(dq), jnp.asarray(dk), jnp.asarray(gbd), jnp.asarray(bd)


def _rotary_tables(seq):
    half = DK // 2
    freqs = ROPE_BASE ** (-jnp.arange(half, dtype=F32) * 2.0 / DK)
    ang = jnp.arange(seq, dtype=F32)[:, None] * freqs[None, :]
    cos, sin = jnp.cos(ang), jnp.sin(ang)
    cos_t = jnp.tile(jnp.concatenate([cos, cos], axis=1), (1, HEADS))
    sin_t = jnp.tile(jnp.concatenate([-sin, sin], axis=1), (1, HEADS))
    return cos_t, sin_t


def _swap_halves(x):
    lane = lax.broadcasted_iota(jnp.int32, (1, QK_W), 1)
    first = (lane & (DK - 1)) < DK // 2
    return jnp.where(first, pltpu.roll(x, QK_W - DK // 2, 1), pltpu.roll(x, DK // 2, 1))


def _head_mask(h):
    lane = lax.broadcasted_iota(jnp.int32, (1, QK_W), 1)
    return (lane >= h * DK) & (lane < (h + 1) * DK)


def _ffn_fwd(x, n, gate, up, wd, name, ride=None):
    t, d = x.shape
    (wg, gq), (wu, uq) = gate, up
    fp = wg.shape[1]
    tm = min(t, FFN_TM)
    tf = FFN_FWD_TF
    nj = fp // tf

    def body(x_ref, n_ref, wg_ref, wu_ref, wd_ref, xo_ref, h_ref, b_ref, sil_ref, dsil_ref, acc_ref):
        j = pl.program_id(1)

        @pl.when(j == 0)
        def _():
            xv = x_ref[...]
            r = lax.rsqrt(jnp.mean(xv * xv, axis=-1, keepdims=True) + RMS_EPS)
            h_ref[...] = (xv * r * n_ref[...]).astype(BF16)
            acc_ref[...] = jnp.zeros_like(acc_ref)

        h = h_ref[...]
        a = _dot(h, wg_ref[...])
        b = _dot(h, wu_ref[...])
        sg = _sigmoid(a)
        sil = a * sg
        b_ref[...] = b.astype(BF16)
        sil_ref[...] = sil.astype(BF16)
        dsil_ref[...] = (sg + sil * (1.0 - sg)).astype(BF16)
        acc_ref[...] += _dot((sil * b).astype(BF16), wd_ref[...])

        @pl.when(j == nj - 1)
        def _():
            xo_ref[...] = x_ref[...] + 0.5 * acc_ref[...]

    act = pl.BlockSpec((tm, tf), lambda i, j: (i, j))
    return _pallas(
        body, name, (t // tm, nj),
        [pl.BlockSpec((tm, d), lambda i, j: (i, 0)), pl.BlockSpec((1, d), lambda i, j: (0, 0)),
         pl.BlockSpec((d, tf), lambda i, j: (gq, j)), pl.BlockSpec((d, tf), lambda i, j: (uq, j)),
         pl.BlockSpec((tf, d), lambda i, j: (j, 0))],
        [pl.BlockSpec((tm, d), lambda i, j: (i, 0)), pl.BlockSpec((tm, d), lambda i, j: (i, 0)), act, act, act],
        [jax.ShapeDtypeStruct((t, d), F32), jax.ShapeDtypeStruct((t, d), BF16)] + [jax.ShapeDtypeStruct((t, fp), BF16)] * 3,
        [pltpu.VMEM((tm, d), F32)], (x, n, wg, wu, wd), ride)


def _ffn_act(x, n, gate, up, name, ride=None):
    t, d = x.shape
    (wg, gq), (wu, uq) = gate, up
    fp = wg.shape[1]
    tm = min(t, FFN_TM)
    tf = 2 * fp // N_DEV
    nj = fp // tf

    def body(x_ref, n_ref, wg_ref, wu_ref, h_ref, b_ref, sil_ref, dsil_ref, s_ref):
        @pl.when(pl.program_id(1) == 0)
        def _():
            xv = x_ref[...]
            r = lax.rsqrt(jnp.mean(xv * xv, axis=-1, keepdims=True) + RMS_EPS)
            h_ref[...] = (xv * r * n_ref[...]).astype(BF16)

        h = h_ref[...]
        a = _dot(h, wg_ref[...])
        b = _dot(h, wu_ref[...])
        sg = _sigmoid(a)
        sil = a * sg
        b_ref[...] = b.astype(BF16)
        sil_ref[...] = sil.astype(BF16)
        dsil_ref[...] = (sg + sil * (1.0 - sg)).astype(BF16)
        s_ref[...] = (sil * b).astype(BF16)

    act = pl.BlockSpec((tm, tf), lambda i, j: (i, j))
    return _pallas(
        body, name, (t // tm, nj),
        [pl.BlockSpec((tm, d), lambda i, j: (i, 0)), pl.BlockSpec((1, d), lambda i, j: (0, 0)),
         pl.BlockSpec((d, tf), lambda i, j: (gq, j)), pl.BlockSpec((d, tf), lambda i, j: (uq, j))],
        [pl.BlockSpec((tm, d), lambda i, j: (i, 0)), act, act, act, act],
        [jax.ShapeDtypeStruct((t, d), BF16)] + [jax.ShapeDtypeStruct((t, fp), BF16)] * 4,
        [], (x, n, wg, wu), ride)


def _ffn_down(s, x, wd, name, ride=None):
    t, d = x.shape
    fp = wd.shape[0]
    tm = min(t, FFN_TM)

    def body(s_ref, x_ref, wd_ref, xo_ref):
        xo_ref[...] = x_ref[...] + 0.5 * _dot(s_ref[...], wd_ref[...])

    row = pl.BlockSpec((tm, d), lambda i: (i, 0))
    res = _pallas(body, name, (t // tm,), [pl.BlockSpec((tm, fp), lambda i: (i, 0)), row, pl.BlockSpec((fp, d), lambda i: (0, 0))],
                  [row], [jax.ShapeDtypeStruct((t, d), F32)], [], (s, x, wd), ride)
    return res[0] if ride is None else (res[0][0], res[1])


def _ffn_bwd_act(dxob, b, sil, dsil, wd_t, name, ride=None):
    t, d = dxob.shape
    fp = wd_t.shape[1]
    tm = min(t, FFN_TM)
    tf = 2 * fp // N_DEV
    ni = t // tm

    def body(dx_ref, b_ref, sil_ref, dsil_ref, wd_ref, da_ref, db_ref, gd_ref, acc_ref):
        i = pl.program_id(1)

        @pl.when(i == 0)
        def _():
            acc_ref[...] = jnp.zeros_like(acc_ref)

        dxv = dx_ref[...]
        bv, sv = b_ref[...].astype(F32), sil_ref[...].astype(F32)
        ds = _dot(dxv, wd_ref[...])
        da_ref[...] = (ds * bv * dsil_ref[...].astype(F32)).astype(BF16)
        db_ref[...] = (ds * sv).astype(BF16)
        acc_ref[...] += _dot_tn((sv * bv).astype(BF16), dxv)

        @pl.when(i == ni - 1)
        def _():
            gd_ref[...] = acc_ref[...].astype(BF16)

    act = pl.BlockSpec((tm, tf), lambda c, i: (i, c))
    return _pallas(
        body, name, (fp // tf, ni),
        [pl.BlockSpec((tm, d), lambda c, i: (i, 0)), act, act, act, pl.BlockSpec((d, tf), lambda c, i: (0, c))],
        [act, act, pl.BlockSpec((tf, d), lambda c, i: (c, 0))],
        [jax.ShapeDtypeStruct((t, fp), BF16), jax.ShapeDtypeStruct((t, fp), BF16), jax.ShapeDtypeStruct((fp, d), BF16)],
        [pltpu.VMEM((tf, d), F32)], (dxob, b, sil, dsil, wd_t), ride)


def _ffn_bwd_in(da, db, dxo, x, n, cols_t, gq, name, ride=None):
    t, d = x.shape
    fp = cols_t.shape[0]
    tm = min(t, FFN_TM)
    tf = fp // 3
    nj = fp // tf

    def body(da_ref, db_ref, dxo_ref, x_ref, n_ref, wg_ref, wu_ref, dx_ref, dn_ref, acc_ref):
        i, j = pl.program_id(0), pl.program_id(1)

        @pl.when((i == 0) & (j == 0))
        def _():
            dn_ref[...] = jnp.zeros_like(dn_ref)

        @pl.when(j == 0)
        def _():
            acc_ref[...] = jnp.zeros_like(acc_ref)

        acc_ref[...] += _dot(da_ref[...], wg_ref[...]) + _dot(db_ref[...], wu_ref[...])

        @pl.when(j == nj - 1)
        def _():
            xv = x_ref[...]
            r = lax.rsqrt(jnp.mean(xv * xv, axis=-1, keepdims=True) + RMS_EPS)
            xh = xv * r
            dh = acc_ref[...]
            dn_ref[...] += jnp.sum(dh * xh, axis=0, keepdims=True)
            dhn = dh * n_ref[...]
            dx_ref[...] = dxo_ref[...] + r * (dhn - xh * jnp.mean(dhn * xh, axis=-1, keepdims=True))

    act = pl.BlockSpec((tm, tf), lambda i, j: (i, j))
    row = pl.BlockSpec((tm, d), lambda i, j: (i, 0))
    return _pallas(
        body, name, (t // tm, nj),
        [act, act, row, row, pl.BlockSpec((1, d), lambda i, j: (0, 0)),
         pl.BlockSpec((tf, d), lambda i, j: (j, gq)), pl.BlockSpec((tf, d), lambda i, j: (j, gq + 1))],
        [row, pl.BlockSpec((1, d), lambda i, j: (0, 0))],
        [jax.ShapeDtypeStruct((t, d), F32), jax.ShapeDtypeStruct((1, d), F32)],
        [pltpu.VMEM((tm, d), F32)], (da, db, dxo, x, n, cols_t, cols_t), ride)


def _wgrad(a, b, scale, tk, tn, name, ride=None, b_cols=None):
    t, k = a.shape
    q0, nq = (0, b.shape[1] // tn) if b_cols is None else b_cols
    n = nq * tn
    tt = min(t, WGRAD_TT)
    nt = t // tt

    def body(a_ref, b_ref, o_ref, acc_ref):
        s = pl.program_id(2)

        @pl.when(s == 0)
        def _():
            acc_ref[...] = jnp.zeros_like(acc_ref)

        acc_ref[...] += _dot_tn(a_ref[...], b_ref[...])

        @pl.when(s == nt - 1)
        def _():
            o_ref[...] = (scale * acc_ref[...]).astype(BF16)

    res = _pallas(
        body, name, (k // tk, n // tn, nt),
        [pl.BlockSpec((tt, tk), lambda p, q, s: (s, p)), pl.BlockSpec((tt, tn), lambda p, q, s: (s, q + q0))],
        [pl.BlockSpec((tk, tn), lambda p, q, s: (p, q))], [jax.ShapeDtypeStruct((k, n), BF16)],
        [pltpu.VMEM((tk, tn), F32)], (a, b), ride)
    return res[0] if ride is None else (res[0][0], res[1])


def _mix_in(x, n, w_in, cos_t, sin_t, seq, name):
    t, d = x.shape
    per_seq = seq // TM

    def body(x_ref, n_ref, w_ref, c_ref, s_ref, h_ref, q_ref, k_ref, v_ref, g_ref, u_ref):
        xv = x_ref[...]
        r = lax.rsqrt(jnp.mean(xv * xv, axis=-1, keepdims=True) + RMS_EPS)
        h = (xv * r * n_ref[...]).astype(BF16)
        h_ref[...] = h
        p = _dot(h, w_ref[...])
        c, s = c_ref[...], s_ref[...]
        q = p[:, :QK_W]
        k = p[:, QK_W:2 * QK_W]
        q_ref[...] = ((q * c + _swap_halves(q) * s) * (DK ** -0.5)).astype(BF16)
        k_ref[...] = (k * c + _swap_halves(k) * s).astype(BF16)
        v_ref[...] = p[:, 2 * QK_W:2 * QK_W + V_W].astype(BF16)
        g_ref[...] = p[:, 2 * QK_W + V_W:2 * QK_W + 2 * V_W]
        u_ref[...] = p[:, 2 * QK_W + 2 * V_W:]

    tile = lambda w: pl.BlockSpec((TM, w), lambda i: (i, 0))
    return pl.pallas_call(
        body, name=name, grid=(t // TM,),
        in_specs=[tile(d), pl.BlockSpec((1, d), lambda i: (0, 0)), pl.BlockSpec(w_in.shape, lambda i: (0, 0)),
                  pl.BlockSpec((TM, QK_W), lambda i: (i % per_seq, 0)), pl.BlockSpec((TM, QK_W), lambda i: (i % per_seq, 0))],
        out_specs=[tile(d), tile(QK_W), tile(QK_W), tile(V_W), tile(V_W), tile(POOL_W)],
        out_shape=[jax.ShapeDtypeStruct((t, d), BF16), jax.ShapeDtypeStruct((t, QK_W), BF16),
                   jax.ShapeDtypeStruct((t, QK_W), BF16), jax.ShapeDtypeStruct((t, V_W), BF16),
                   jax.ShapeDtypeStruct((t, V_W), F32), jax.ShapeDtypeStruct((t, POOL_W), F32)],
        compiler_params=_cparams(1),
    )(x, n, w_in, cos_t, sin_t)


def _mix_in_bwd(dp, dx2, x1, n, w_in_t, name, ride=None):
    t, d = x1.shape

    def body(dp_ref, dx2_ref, x_ref, n_ref, w_ref, dx_ref, dn_ref, dxb_ref):
        @pl.when(pl.program_id(0) == 0)
        def _():
            dn_ref[...] = jnp.zeros_like(dn_ref)

        dh = _dot(dp_ref[...], w_ref[...])
        xv = x_ref[...]
        r = lax.rsqrt(jnp.mean(xv * xv, axis=-1, keepdims=True) + RMS_EPS)
        xh = xv * r
        dn_ref[...] += jnp.sum(dh * xh, axis=0, keepdims=True)
        dhn = dh * n_ref[...]
        dx = dx2_ref[...] + r * (dhn - xh * jnp.mean(dhn * xh, axis=-1, keepdims=True))
        dx_ref[...] = dx
        dxb_ref[...] = (0.5 * dx).astype(BF16)

    tile = lambda w: pl.BlockSpec((TM, w), lambda i: (i, 0))
    return _pallas(
        body, name, (t // TM,),
        [tile(dp.shape[1]), tile(d), tile(d), pl.BlockSpec((1, d), lambda i: (0, 0)),
         pl.BlockSpec(w_in_t.shape, lambda i: (0, 0))],
        [tile(d), pl.BlockSpec((1, d), lambda i: (0, 0)), tile(d)],
        [jax.ShapeDtypeStruct((t, d), F32), jax.ShapeDtypeStruct((1, d), F32), jax.ShapeDtypeStruct((t, d), BF16)],
        [], (dp, dx2, x1, n, w_in_t), ride)


def _group_norm(o):
    parts, rstds = [], []
    for h in range(HEADS):
        oh = o[:, h * DV:(h + 1) * DV]
        dlt = oh - jnp.mean(oh, axis=-1, keepdims=True)
        rstd = lax.rsqrt(jnp.mean(dlt * dlt, axis=-1, keepdims=True) + GN_EPS)
        parts.append(dlt * rstd)
        rstds.append(rstd)
    return jnp.concatenate(parts, axis=1), rstds


def _mix_core_fwd(qs, k, v, g, u, x1, consts, gain, wp, scale, w_out, nseq, seq, name):
    t, d = x1.shape
    nb = min(MIX_BLOCKS, seq // BLK)
    nstep = seq // (nb * BLK)
    mask, dq, dk, gbd, bd = consts

    def body(q_ref, k_ref, v_ref, g_ref, u_ref, x1_ref, m_ref, dq_ref, dk_ref, gbd_ref, bd_ref, gain_ref, wp_ref,
             sc_ref, wo_ref, x2_ref, mix_ref, o_ref, pooled_ref, st_ref, state, halo):
        j = pl.program_id(1)

        @pl.when(j == 0)
        def _():
            state[...] = jnp.zeros_like(state)
            halo[...] = jnp.zeros_like(halo)

        for s in range(nb):
            rows = pl.ds(s * BLK, BLK)
            qv, kv, vv = q_ref[rows, :], k_ref[rows, :], v_ref[rows, :]
            st = state[...]
            st_ref[s] = st
            cross = _dot(qv, st.astype(BF16)) * dq_ref[...]
            outs = []
            for h in range(HEADS):
                qh = jnp.where(_head_mask(h), qv, jnp.zeros_like(qv))
                am = (_dot_nt(qh, kv) * m_ref[h]).astype(BF16)
                outs.append(_dot(am, vv[:, h * DV:(h + 1) * DV]))
            o = jnp.concatenate(outs, axis=1) + cross
            o_ref[rows, :] = o
            kd = (kv.astype(F32) * dk_ref[...]).astype(BF16)
            state[...] = gbd_ref[...] * st + _dot_tn(kd, vv) * bd_ref[...]

            gv = g_ref[rows, :]
            nrm, _ = _group_norm(o)
            ret = (gv * _sigmoid(gv)) * (nrm * gain_ref[...])

            uv = u_ref[rows, :]
            c = jnp.concatenate([halo[...], uv], axis=0)
            halo[...] = uv[BLK - HALO:, :]
            pos = (j * nb + s) * BLK + lax.broadcasted_iota(jnp.int32, (BLK, 1), 0)
            parts = []
            for gi, w in enumerate(WINDOWS):
                c = c + pltpu.roll(c, w // 2, 0)
                cnt = jnp.minimum(pos + 1, w).astype(F32)
                parts.append(c[HALO:, :GC] / cnt)
                if gi + 1 < len(WINDOWS):
                    c = c[:, GC:]
            pooled = (jnp.concatenate(parts, axis=1) - uv).astype(BF16)
            pooled_ref[rows, :] = pooled
            z = jnp.concatenate([_dot(pooled[:, gi * GC:(gi + 1) * GC], wp_ref[gi]) for gi in range(len(WINDOWS))],
                                axis=1)
            mix = jnp.concatenate([ret, z * sc_ref[...]], axis=1).astype(BF16)
            mix_ref[rows, :] = mix
            x2_ref[rows, :] = x1_ref[rows, :] + _dot(mix, wo_ref[...])

    blk = lambda w: pl.BlockSpec((nb * BLK, w), lambda i, j: (i * nstep + j, 0))
    full = lambda a: pl.BlockSpec(a.shape, lambda i, j: (0,) * a.ndim)
    return _pallas(
        body, name, (nseq, nstep),
        [blk(QK_W), blk(QK_W), blk(V_W), blk(V_W), blk(POOL_W), blk(d),
         full(mask), full(dq), full(dk), full(gbd), full(bd), full(gain), full(wp), full(scale), full(w_out)],
        [blk(d), blk(d), blk(V_W), blk(POOL_W), pl.BlockSpec((nb, QK_W, V_W), lambda i, j: (i * nstep + j, 0, 0))],
        [jax.ShapeDtypeStruct((t, d), F32), jax.ShapeDtypeStruct((t, d), BF16),
         jax.ShapeDtypeStruct((t, V_W), F32), jax.ShapeDtypeStruct((t, POOL_W), BF16),
         jax.ShapeDtypeStruct((t // BLK, QK_W, V_W), F32)],
        [pltpu.VMEM((QK_W, V_W), F32), pltpu.VMEM((HALO, POOL_W), F32)],
        (qs, k, v, g, u, x1, mask, dq, dk, gbd, bd, gain, wp, scale, w_out))


def _mix_core_bwd(dx2, qs, k, v, g, o, pooled, st, consts, gain, wp, scale, w_out, cos_t, sin_t, nseq, seq, name,
                  ride=None):
    t, d = dx2.shape
    nb = min(MIX_BLOCKS, seq // BLK)
    nstep = seq // (nb * BLK)
    mask, dq, dk, gbd, bd = consts
    n_win = len(WINDOWS)

    def body(dx2_ref, q_ref, k_ref, v_ref, g_ref, o_ref, pooled_ref, st_ref, m_ref, dq_ref, dk_ref, gbd_ref, bd_ref,
             gain_ref, wp_ref, sc_ref, wo_ref, c_ref, s_ref,
             dp_ref, dx2b_ref, dgain_ref, dscale_ref, dwp_ref, rstate, carry):
        i, j = pl.program_id(0), pl.program_id(1)

        @pl.when((i == 0) & (j == 0))
        def _():
            dgain_ref[...] = jnp.zeros_like(dgain_ref)
            dscale_ref[...] = jnp.zeros_like(dscale_ref)
            dwp_ref[...] = jnp.zeros_like(dwp_ref)

        @pl.when(j == 0)
        def _():
            rstate[...] = jnp.zeros_like(rstate)
            carry[...] = jnp.zeros_like(carry)

        for s in reversed(range(nb)):
            rows = pl.ds(s * BLK, BLK)
            dx2b = dx2_ref[rows, :].astype(BF16)
            dx2b_ref[rows, :] = dx2b
            dmix = _dot(dx2b, wo_ref[...])
            dret, dpool = dmix[:, :V_W], dmix[:, V_W:]

            gv, ov, gain_v = g_ref[rows, :], o_ref[rows, :], gain_ref[...]
            sg = _sigmoid(gv)
            sil = gv * sg
            nrm, rstds = _group_norm(ov)
            dg = dret * (nrm * gain_v) * (sg * (1.0 + gv * (1.0 - sg)))
            dgn = dret * sil
            dgain_ref[...] += jnp.sum(dgn * nrm, axis=0, keepdims=True)
            dnrm = dgn * gain_v
            do_parts = []
            for h in range(HEADS):
                dn_h = dnrm[:, h * DV:(h + 1) * DV]
                n_h = nrm[:, h * DV:(h + 1) * DV]
                do_parts.append(rstds[h] * (dn_h - jnp.mean(dn_h, axis=-1, keepdims=True)
                                            - n_h * jnp.mean(dn_h * n_h, axis=-1, keepdims=True)))
            do = jnp.concatenate(do_parts, axis=1)
            dob = do.astype(BF16)

            qv, kv, vv = q_ref[rows, :], k_ref[rows, :], v_ref[rows, :]
            stb = st_ref[s].astype(BF16)
            rs = rstate[...]
            rsb = rs.astype(BF16)
            dod = (do * dq_ref[...]).astype(BF16)
            dqs = _dot_nt(dod, stb)
            dst = _dot_tn(qv, dod) * bd_ref[...]
            dkf = dk_ref[...]
            kd = (kv.astype(F32) * dkf).astype(BF16)
            dks = _dot_nt(vv, rsb) * dkf
            dvs = _dot(kd, rsb)
            dv_parts = []
            for h in range(HEADS):
                hm = _head_mask(h)
                qh = jnp.where(hm, qv, jnp.zeros_like(qv))
                kh = jnp.where(hm, kv, jnp.zeros_like(kv))
                mh = m_ref[h]
                am = (_dot_nt(qh, kv) * mh).astype(BF16)
                dpm = (_dot_nt(dob[:, h * DV:(h + 1) * DV], vv[:, h * DV:(h + 1) * DV]) * mh).astype(BF16)
                dqs = dqs + _dot(dpm, kh)
                dks = dks + _dot_tn(dpm, qh)
                dv_parts.append(_dot_tn(am, dob[:, h * DV:(h + 1) * DV]))
            dvs = dvs + jnp.concatenate(dv_parts, axis=1)
            rstate[...] = dst + gbd_ref[...] * rs

            cv, sv = c_ref[rows, :], s_ref[rows, :]
            dqr = dqs * (DK ** -0.5)
            dq_pre = dqr * cv + _swap_halves(dqr * sv)
            dk_pre = dks * cv + _swap_halves(dks * sv)

            pv = pooled_ref[rows, :]
            sc = sc_ref[...]
            dzb = (dpool * sc).astype(BF16)
            z_parts, dpo_parts = [], []
            for gi in range(n_win):
                p_g = pv[:, gi * GC:(gi + 1) * GC]
                dz_g = dzb[:, gi * GC:(gi + 1) * GC]
                z_parts.append(_dot(p_g, wp_ref[gi]))
                dwp_ref[gi] += _dot_tn(p_g, dz_g)
                dpo_parts.append(_dot_nt(dz_g, wp_ref[gi]))
            dscale_ref[...] += jnp.sum(dpool * jnp.concatenate(z_parts, axis=1), axis=0, keepdims=True)
            dpo = jnp.concatenate(dpo_parts, axis=1)
            pos = ((nstep - 1 - j) * nb + s) * BLK + lax.broadcasted_iota(jnp.int32, (BLK, 1), 0)
            e = jnp.concatenate(
                [dpo[:, gi * GC:(gi + 1) * GC] / jnp.minimum(pos + 1, w).astype(F32) for gi, w in enumerate(WINDOWS)],
                axis=1)
            c = jnp.concatenate([e, carry[...]], axis=0)
            carry[...] = e[:HALO, :]
            span = BLK + HALO
            lead = []
            for gi, w in enumerate(WINDOWS):
                c = c + pltpu.roll(c, span - w // 2, 0)
                lead.append(c[:BLK, :GC])
                if gi + 1 < n_win:
                    c = c[:, GC:]
            du = jnp.concatenate(lead, axis=1) - dpo

            dp_ref[rows, 0:QK_W] = dq_pre.astype(BF16)
            dp_ref[rows, QK_W:2 * QK_W] = dk_pre.astype(BF16)
            dp_ref[rows, 2 * QK_W:2 * QK_W + V_W] = dvs.astype(BF16)
            dp_ref[rows, 2 * QK_W + V_W:2 * QK_W + 2 * V_W] = dg.astype(BF16)
            dp_ref[rows, 2 * QK_W + 2 * V_W:] = du.astype(BF16)

    rev = lambda i, j: i * nstep + (nstep - 1 - j)
    blk = lambda w: pl.BlockSpec((nb * BLK, w), lambda i, j: (rev(i, j), 0))
    full = lambda a: pl.BlockSpec(a.shape, lambda i, j: (0,) * a.ndim)
    in_w = 2 * QK_W + 2 * V_W + POOL_W
    return _pallas(
        body, name, (nseq, nstep),
        [blk(d), blk(QK_W), blk(QK_W), blk(V_W), blk(V_W), blk(V_W), blk(POOL_W),
         pl.BlockSpec((nb, QK_W, V_W), lambda i, j: (rev(i, j), 0, 0)),
         full(mask), full(dq), full(dk), full(gbd), full(bd), full(gain), full(wp), full(scale), full(w_out),
         pl.BlockSpec((nb * BLK, QK_W), lambda i, j: (nstep - 1 - j, 0)),
         pl.BlockSpec((nb * BLK, QK_W), lambda i, j: (nstep - 1 - j, 0))],
        [blk(in_w), blk(d), pl.BlockSpec((1, V_W), lambda i, j: (0, 0)),
         pl.BlockSpec((1, POOL_W), lambda i, j: (0, 0)), pl.BlockSpec((n_win, GC, GC), lambda i, j: (0, 0, 0))],
        [jax.ShapeDtypeStruct((t, in_w), BF16), jax.ShapeDtypeStruct((t, d), BF16),
         jax.ShapeDtypeStruct((1, V_W), F32), jax.ShapeDtypeStruct((1, POOL_W), F32),
         jax.ShapeDtypeStruct((n_win, GC, GC), F32)],
        [pltpu.VMEM((QK_W, V_W), F32), pltpu.VMEM((HALO, POOL_W), F32)],
        (dx2, qs, k, v, g, o, pooled, st, mask, dq, dk, gbd, bd, gain, wp, scale, w_out, cos_t, sin_t), ride)


def _loss_head(x3, nf, tgt, name):
    t, d = x3.shape

    def body(x_ref, n_ref, t_ref, dx_ref, dn_ref, loss_ref, dxb_ref):
        @pl.when(pl.program_id(0) == 0)
        def _():
            dn_ref[...] = jnp.zeros_like(dn_ref)
            loss_ref[...] = jnp.zeros_like(loss_ref)

        xv = x_ref[...]
        nv = n_ref[...]
        r = lax.rsqrt(jnp.mean(xv * xv, axis=-1, keepdims=True) + RMS_EPS)
        xh = xv * r
        err = xh * nv - t_ref[...]
        row = jnp.mean(err * err, axis=-1, keepdims=True)
        loss_ref[...] += 0.5 * jnp.sum(row, axis=0, keepdims=True)
        dy = err * (1.0 / d)
        dn_ref[...] += jnp.sum(dy * xh, axis=0, keepdims=True)
        dxh = dy * nv
        dx = r * (dxh - xh * jnp.mean(dxh * xh, axis=-1, keepdims=True))
        dx_ref[...] = dx
        dxb_ref[...] = (0.5 * dx).astype(BF16)

    tile = pl.BlockSpec((TM, d), lambda i: (i, 0))
    return pl.pallas_call(
        body, name=name, grid=(t // TM,),
        in_specs=[tile, pl.BlockSpec((1, d), lambda i: (0, 0)), tile],
        out_specs=[tile, pl.BlockSpec((1, d), lambda i: (0, 0)), pl.BlockSpec((1, 1), lambda i: (0, 0)), tile],
        out_shape=[jax.ShapeDtypeStruct((t, d), F32), jax.ShapeDtypeStruct((1, d), F32), jax.ShapeDtypeStruct((1, 1), F32),
                   jax.ShapeDtypeStruct((t, d), BF16)],
        compiler_params=_cparams(1),
    )(x3, nf, tgt)


def _coords():
    return lax.axis_index("x"), lax.axis_index("y"), lax.axis_index("c")


def _window(ref, kind, idx, size):
    if kind == "col":
        return ref.at[:, pl.ds(pl.multiple_of(idx * size, LANE), size)]
    return ref.at[pl.ds(pl.multiple_of(idx * size, 8), size), :]


def _run_exchange(ex, name):
    n_in = len(ex.inputs)

    def body(*refs):
        ins, outs, sems = refs[:n_in], refs[n_in:n_in + len(ex.out_shape)], refs[n_in + len(ex.out_shape):]
        ex.start(ins, outs, sems)
        if ex.mid is not None:
            ex.mid(ins, outs, sems)
        ex.finish(ins, outs, sems)

    return pl.pallas_call(body, name=name, in_specs=[ANY] * n_in, out_specs=[ANY] * len(ex.out_shape),
                          out_shape=ex.out_shape, scratch_shapes=ex.scratch)(*ex.inputs)


def _join(exchanges):
    bounds = []
    i0 = o0 = s0 = 0
    for ex in exchanges:
        bounds.append((i0, o0, s0))
        i0, o0, s0 = i0 + len(ex.inputs), o0 + len(ex.out_shape), s0 + len(ex.scratch)

    def phase(which):
        def run(ins, outs, sems):
            for ex, (i, o, s) in zip(exchanges, bounds):
                fn = getattr(ex, which)
                if fn is not None:
                    fn(ins[i:i + len(ex.inputs)], outs[o:o + len(ex.out_shape)], sems[s:s + len(ex.scratch)])
        return run

    return _Exchange(sum((ex.inputs for ex in exchanges), []), sum((ex.out_shape for ex in exchanges), []),
                     sum((ex.scratch for ex in exchanges), []), phase("start"), phase("finish"),
                     phase("mid") if any(ex.mid is not None for ex in exchanges) else None)


def _gather_exchange(parts):
    n = len(parts)
    kinds = [kd for _, kd in parts]
    sizes = [a.shape[1] if kd == "col" else a.shape[0] for a, kd in parts]

    def plan(ins, outs, sems):
        send_sems, recv_sems, local_sems = sems
        x, y, c = _coords()
        me, sibling = (x, y, c), (x, y, 1 - c)
        chips = [(1 - x, y), (x, 1 - y), (1 - x, 1 - y)]

        def win(p, dev):
            return _window(outs[p], kinds[p], 4 * dev[0] + 2 * dev[1] + dev[2], sizes[p])

        def copy(p, k, block, to, src=None):
            return pltpu.make_async_remote_copy(
                src_ref=win(p, block) if src is None else src, dst_ref=win(p, block),
                send_sem=send_sems.at[p * 7 + k], recv_sem=recv_sems.at[p * 7 + k], device_id=to, device_id_type=MESH_ID)

        mine = [pltpu.make_async_copy(ins[p], win(p, me), local_sems.at[p]) for p in range(n)]
        first, arrived, passed, rest = [], [], [], []
        for p in range(n):
            first.append(copy(p, 0, me, sibling, src=ins[p]))
            first += [copy(p, 1 + q, me, (*chip, c), src=ins[p]) for q, chip in enumerate(chips)]
            rest.append(copy(p, 0, sibling, me))
            rest += [copy(p, 4 + q, (*chip, 1 - c), me) for q, chip in enumerate(chips)]
        for q, chip in enumerate(chips):
            for p in range(n):
                arrived.append(copy(p, 1 + q, (*chip, c), me))
                passed.append(copy(p, 4 + q, (*chip, c), sibling))
        return mine, first, arrived, passed, rest

    def start(ins, outs, sems):
        mine, first, _, _, _ = plan(ins, outs, sems)
        for cp in mine + first:
            cp.start()

    def mid(ins, outs, sems):
        _, _, arrived, passed, _ = plan(ins, outs, sems)
        for got, fwd in zip(arrived, passed):
            got.wait_recv()
            fwd.start()

    def finish(ins, outs, sems):
        mine, first, _, passed, rest = plan(ins, outs, sems)
        for cp in rest:
            cp.wait_recv()
        for cp in first + passed:
            cp.wait_send()
        for cp in mine:
            cp.wait()

    out_shape = [jax.ShapeDtypeStruct((a.shape[0], N_DEV * a.shape[1]) if kd == "col" else (N_DEV * a.shape[0], a.shape[1]),
                                      a.dtype) for a, kd in parts]
    scratch = [pltpu.SemaphoreType.DMA((7 * n,)), pltpu.SemaphoreType.DMA((7 * n,)), pltpu.SemaphoreType.DMA((n,))]
    return _Exchange([a for a, _ in parts], out_shape, scratch, start, finish, mid)


def _all_gather(parts, name):
    return _run_exchange(_gather_exchange(parts), name)


def _shard_shape(a, kd):
    return (a.shape[0], a.shape[1] // N_DEV) if kd == "col" else (a.shape[0] // N_DEV, a.shape[1])


def _symmetric_exchange(inputs, out_shape, n_copies, plan):
    def start(ins, outs, sems):
        for cp in plan(ins, outs, sems):
            cp.start()

    def finish(ins, outs, sems):
        copies = plan(ins, outs, sems)
        for cp in copies:
            cp.wait_recv()
        for cp in copies:
            cp.wait_send()

    scratch = [pltpu.SemaphoreType.DMA((n_copies,)), pltpu.SemaphoreType.DMA((n_copies,))]
    return _Exchange(inputs, out_shape, scratch, start, finish)


def _rs_pair_exchange(grads):
    n = len(grads)
    kinds = [kd for _, kd in grads]
    shapes = [_shard_shape(a, kd) for a, kd in grads]

    def plan(ins, outs, sems):
        send_sems, recv_sems = sems
        x, y, c = _coords()
        copies = []
        for p in range(n):
            size = shapes[p][1] if kinds[p] == "col" else shapes[p][0]
            for s in range(4):
                src = _window(ins[p], kinds[p], 2 * s + (1 - c), size)
                copies.append(pltpu.make_async_remote_copy(
                    src_ref=src, dst_ref=outs[p].at[s], send_sem=send_sems.at[4 * p + s], recv_sem=recv_sems.at[4 * p + s],
                    device_id=(x, y, 1 - c), device_id_type=MESH_ID))
        return copies

    return _symmetric_exchange([a for a, _ in grads], [jax.ShapeDtypeStruct((4,) + shapes[p], BF16) for p in range(n)],
                               4 * n, plan)


def _rs_chips_exchange(sums):
    n = len(sums)

    def plan(ins, outs, sems):
        send_sems, recv_sems = sems
        x, y, c = _coords()
        chips = [(1 - x, y), (x, 1 - y), (1 - x, 1 - y)]
        copies = []
        for p in range(n):
            for q, (cx, cy) in enumerate(chips):
                copies.append(pltpu.make_async_remote_copy(
                    src_ref=ins[p].at[2 * cx + cy], dst_ref=outs[p].at[q],
                    send_sem=send_sems.at[3 * p + q], recv_sem=recv_sems.at[3 * p + q],
                    device_id=(cx, cy, c), device_id_type=MESH_ID))
        return copies

    return _symmetric_exchange(list(sums), [jax.ShapeDtypeStruct((3,) + a.shape[1:], BF16) for a in sums], 3 * n, plan)


def _pair_sum(grad, kd, recv, core, name):
    _, r, cw = recv.shape
    tr = min(r, TM)

    def body(core_ref, g_ref, r_ref, o_ref):
        del core_ref
        o_ref[0] = (g_ref[...].astype(F32) + r_ref[0].astype(F32)).astype(BF16)

    if kd == "col":
        g_spec = pl.BlockSpec((tr, cw), lambda s, i, core_ref: (i, 2 * s + core_ref[0]))
    else:
        g_spec = pl.BlockSpec((tr, cw), lambda s, i, core_ref: ((2 * s + core_ref[0]) * (r // tr) + i, 0))
    grid_spec = pltpu.PrefetchScalarGridSpec(
        num_scalar_prefetch=1, grid=(4, r // tr),
        in_specs=[g_spec, pl.BlockSpec((1, tr, cw), lambda s, i, core_ref: (s, i, 0))],
        out_specs=pl.BlockSpec((1, tr, cw), lambda s, i, core_ref: (s, i, 0)))
    return pl.pallas_call(
        body, name=name, grid_spec=grid_spec, out_shape=jax.ShapeDtypeStruct(recv.shape, BF16),
        compiler_params=_cparams(2),
    )(core, grad, recv)


def _adam_math(w, g, m, v):
    m2 = B1 * m + (1.0 - B1) * g
    v2 = B2 * v + (1.0 - B2) * (g * g)
    m_hat = m2 / (1.0 - B1 ** STEP)
    v_hat = v2 / (1.0 - B2 ** STEP)
    delta = -LR * (m_hat / (jnp.sqrt(v_hat) + ADAM_EPS) + WD * w)
    return delta, m2, v2


def _chip_sum_adam(items, chip, tr, name, ride=None):
    r = items[0][1].shape[0]
    steps = r // tr
    n_parts = [len(parts) for parts, _, _, _ in items]
    r_in = 0 if ride is None else len(ride.inputs)
    r_out = 0 if ride is None else len(ride.out_shape)
    n_in = sum(2 * k + 3 for k in n_parts)
    n_out = 4 * len(items)

    def body(chip_ref, *refs):
        del chip_ref
        ins, refs = refs[:n_in], refs[n_in:]
        r_ins, refs = refs[:r_in], refs[r_in:]
        outs, refs = refs[:n_out], refs[n_out:]
        r_outs, sems = refs[:r_out], refs[r_out:]
        i = pl.program_id(0)
        if ride is not None:
            @pl.when(i == 0)
            def _():
                ride.start(r_ins, r_outs, sems)

        pos = 0
        for q, (k, (_, w, _, _)) in enumerate(zip(n_parts, items)):
            cols = []
            for _ in range(k):
                p_ref, c_ref = ins[pos], ins[pos + 1]
                pos += 2
                cols.append(p_ref[0].astype(F32) + c_ref[0].astype(F32) + c_ref[1].astype(F32) + c_ref[2].astype(F32))
            g = (cols[0] if k == 1 else jnp.concatenate(cols, axis=1))[:, :w.shape[1]]
            w_ref, m_ref, v_ref = ins[pos:pos + 3]
            pos += 3
            delta, m2, v2 = _adam_math(w_ref[...], g, m_ref[...], v_ref[...])
            outs[4 * q][...] = g
            outs[4 * q + 1][...] = delta
            outs[4 * q + 2][...] = m2
            outs[4 * q + 3][...] = v2

        if ride is not None:
            @pl.when(i == steps - 1)
            def _():
                ride.finish(r_ins, r_outs, sems)

    in_specs, args, out_specs, out_shape = [], [], [], []
    for parts, w, m, v in items:
        for psum, recv in parts:
            pc = psum.shape[2]
            in_specs += [pl.BlockSpec((1, tr, pc), lambda i, chip_ref: (chip_ref[0], i, 0)),
                         pl.BlockSpec((3, tr, pc), lambda i, chip_ref: (0, i, 0))]
            args += [psum, recv]
        loc = pl.BlockSpec((tr, w.shape[1]), lambda i, chip_ref: (i, 0))
        in_specs += [loc] * 3
        args += [w, m, v]
        out_specs += [loc] * 4
        out_shape += [jax.ShapeDtypeStruct(w.shape, F32)] * 4
    grid_spec = pltpu.PrefetchScalarGridSpec(
        num_scalar_prefetch=1, grid=(steps,), in_specs=in_specs + [ANY] * r_in, out_specs=out_specs + [ANY] * r_out,
        scratch_shapes=[] if ride is None else ride.scratch)
    res = pl.pallas_call(
        body, name=name, grid_spec=grid_spec, out_shape=out_shape + ([] if ride is None else ride.out_shape),
        compiler_params=_cparams(1),
    )(chip, *args, *([] if ride is None else ride.inputs))
    return res if ride is None else (res[:n_out], res[n_out:])


def _small_allreduce_adam(partials, params, moms, vels, plain, name, ride=None):
    n, n_plain = len(partials), len(plain)
    summed = list(partials) + list(plain)
    row0 = []
    rows = 0
    for a in summed:
        if a.shape[0] >= 8:
            rows = _pad_to(rows, 8)
        row0.append(rows)
        rows += a.shape[0]
    rows = _pad_to(rows, 8)
    width = max(a.shape[1] for a in summed)
    r_in = 0 if ride is None else len(ride.inputs)
    r_out = 0 if ride is None else len(ride.out_shape)
    n_out = 4 * n + n_plain

    def body(*refs):
        w_in, m_in, v_in = refs[0:n], refs[n:2 * n], refs[2 * n:3 * n]
        g_in, refs = refs[3 * n:4 * n + n_plain], refs[4 * n + n_plain:]
        r_ins, refs = refs[:r_in], refs[r_in:]
        outs, refs = refs[:n_out], refs[n_out:]
        r_outs, refs = refs[:r_out], refs[r_out:]
        pair, chips, send_sems, recv_sems = refs[:4]
        if ride is not None:
            ride.start(r_ins, r_outs, refs[4:])
        x, y, c = _coords()
        chip = 2 * x + y
        pair[c] = jnp.zeros((rows, width), F32)
        for p, a in enumerate(summed):
            r, cw = a.shape
            pair[c, row0[p]:row0[p] + r, 0:cw] = g_in[p][...]
        swap = pltpu.make_async_remote_copy(src_ref=pair.at[c], dst_ref=pair.at[c], send_sem=send_sems.at[0],
                                            recv_sem=recv_sems.at[0], device_id=(x, y, 1 - c), device_id_type=MESH_ID)
        swap.start()
        swap.wait_recv()
        swap.wait_send()
        chips[chip] = pair[0] + pair[1]
        copies = [pltpu.make_async_remote_copy(
            src_ref=chips.at[chip], dst_ref=chips.at[chip], send_sem=send_sems.at[1 + q], recv_sem=recv_sems.at[1 + q],
            device_id=(cx, cy, c), device_id_type=MESH_ID) for q, (cx, cy) in enumerate([(1 - x, y), (x, 1 - y), (1 - x, 1 - y)])]
        for cp in copies:
            cp.start()
        for cp in copies:
            cp.wait_recv()
        for cp in copies:
            cp.wait_send()
        for p, a in enumerate(summed):
            r, cw = a.shape
            g = chips[0, row0[p]:row0[p] + r, 0:cw]
            for q in range(1, 4):
                g = g + chips[q, row0[p]:row0[p] + r, 0:cw]
            if p >= n:
                outs[4 * n + p - n][...] = g
                continue
            delta, m2, v2 = _adam_math(w_in[p][...], g, m_in[p][...], v_in[p][...])
            outs[4 * p][...] = g
            outs[4 * p + 1][...] = delta
            outs[4 * p + 2][...] = m2
            outs[4 * p + 3][...] = v2
        if ride is not None:
            ride.finish(r_ins, r_outs, refs[4:])

    out_shape = []
    for a in partials:
        out_shape += [jax.ShapeDtypeStruct(a.shape, F32)] * 4
    out_shape += [jax.ShapeDtypeStruct(a.shape, F32) for a in plain]
    res = pl.pallas_call(
        body, name=name, in_specs=[VMEM_SPEC] * (4 * n + n_plain) + [ANY] * r_in,
        out_specs=[VMEM_SPEC] * n_out + [ANY] * r_out, out_shape=out_shape + ([] if ride is None else ride.out_shape),
        scratch_shapes=[pltpu.VMEM((2, rows, width), F32), pltpu.VMEM((4, rows, width), F32),
                        pltpu.SemaphoreType.DMA((4,)), pltpu.SemaphoreType.DMA((4,))] + ([] if ride is None else ride.scratch),
    )(*params, *moms, *vels, *partials, *plain, *([] if ride is None else ride.inputs))
    return res if ride is None else (res[:n_out], res[n_out:])


def _local_step(xf, tgt, nseq, seq, cols1_all, later, small_w, core, small_step):
    d = xf.shape[1]
    n1, n2, gain, pool_w, pool_scale, n3, nf = small_w
    tf = 2 * cols1_all.shape[1] // N_DEV
    consts = _retention_constants()
    cos_t, sin_t = _rotary_tables(seq)
    wp_b = pool_w.astype(BF16)

    def pair_sums(grads, recv, names):
        return [_pair_sum(g, kd, r, core, "pair_sum_" + nm) for (g, kd), r, nm in zip(grads, recv, names)]

    def riding(host):
        return _gather_exchange(later[host])

    both = lambda first, second: _join([_rs_chips_exchange(first), _rs_pair_exchange([second])])

    (h1, b1, sil1, dsil1, s1), (d1_all, win_all, wout_all, gate2_all, up2_all) = _ffn_act(
        xf, n1, (cols1_all, 0), (cols1_all, 1), "ffn1_act", ride=riding("ffn1_act"))
    x1, (d2_all,) = _ffn_down(s1, xf, d1_all, "ffn1_down", ride=riding("ffn1_down"))
    h2, qs, kr, vv, gg, uu = _mix_in(x1, n2, win_all, cos_t, sin_t, seq, "mix_in")
    x2, mix, oo, pooled, states = _mix_core_fwd(qs, kr, vv, gg, uu, x1, consts, gain, wp_b, pool_scale, wout_all,
                                                 nseq, seq, "mix_core_fwd")
    (x3, h3, b3, sil3, dsil3), (cols2_t, d2_t, win_t, wout_t) = _ffn_fwd(
        x2, n3, (gate2_all, 0), (up2_all, 0), d2_all, "ffn2_fwd", ride=riding("ffn2_fwd"))
    dx3, dnf, loss_part, dx3b = _loss_head(x3, nf, tgt, "loss_head")
    out = {}

    (da3, db3, g_wd2), (cols1_t, d1_t) = _ffn_bwd_act(dx3b, b3, sil3, dsil3, d2_t, "ffn2_bwd_act",
                                                      ride=riding("ffn2_bwd_act"))
    names2 = ["ffn2_gate", "ffn2_up", "ffn2_down"]
    grads2 = [(_wgrad(da3, h3, 1.0, tf, d, "wgrad_gate2"), "row"), (_wgrad(db3, h3, 1.0, tf, d, "wgrad_up2"), "row"),
              (g_wd2, "row")]
    (dx2, dn3), recv2 = _ffn_bwd_in(da3, db3, dx3, x2, n3, cols2_t, 0, "ffn2_bwd_in", ride=_rs_pair_exchange(grads2))
    sums2 = pair_sums(grads2, recv2, names2)
    (dp, dx2b, dgain, dscale, dwp), crecv2 = _mix_core_bwd(
        dx2, qs, kr, vv, gg, oo, pooled, states, consts, gain, wp_b, pool_scale, wout_t, cos_t, sin_t, nseq, seq,
        "mix_core_bwd", ride=_rs_chips_exchange(sums2))
    out.update({nm: [(s, r)] for nm, s, r in zip(names2, sums2, crecv2)})

    names_m = ["w_in", "w_out"]
    grads_m = [(_wgrad(h2, dp, 1.0, d, d, "wgrad_in"), "col"), (_wgrad(mix, dx2b, 1.0, d, d, "wgrad_out"), "row")]
    (dx1, dn2, dx1b), recv_m = _mix_in_bwd(dp, dx2, x1, n2, win_t, "mix_in_bwd", ride=_rs_pair_exchange(grads_m))
    sums_m = pair_sums(grads_m, recv_m, names_m)
    (da1, db1, g_wd1), crecv_m = _ffn_bwd_act(dx1b, b1, sil1, dsil1, d1_t, "ffn1_bwd_act", ride=_rs_chips_exchange(sums_m))
    out.update({nm: [(s, r)] for nm, s, r in zip(names_m, sums_m, crecv_m)})

    dx0, dn1 = _ffn_bwd_in(da1, db1, dx1, xf, n1, cols1_t, 0, "ffn1_bwd_in")
    g_down = (g_wd1, "row")
    g_gate, recv_d = _wgrad(da1, h1, 1.0, tf, d, "wgrad_gate1", ride=_rs_pair_exchange([g_down]))
    g_gate = (g_gate, "row")
    sum_d = pair_sums([g_down], recv_d, ["ffn1_down"])
    g_lo, (crecv_d, recv_g) = _wgrad(db1, h1, 1.0, tf, d // 2, "wgrad_up1_lo", ride=both(sum_d, g_gate), b_cols=(0, 1))
    g_lo = (g_lo, "row")
    sum_g = pair_sums([g_gate], [recv_g], ["ffn1_gate"])
    g_hi, (crecv_g, recv_lo) = _wgrad(db1, h1, 1.0, tf, d // 2, "wgrad_up1_hi", ride=both(sum_g, g_lo), b_cols=(1, 1))
    g_hi = (g_hi, "row")
    sum_lo = pair_sums([g_lo], [recv_lo], ["ffn1_up_lo"])
    small_out, (crecv_lo, recv_hi) = small_step((dn1, dn2, dgain, dwp, dscale, dn3, dnf), loss_part, both(sum_lo, g_hi))
    sum_hi = pair_sums([g_hi], [recv_hi], ["ffn1_up_hi"])
    out.update({"ffn1_gate": [(sum_g[0], crecv_g)], "ffn1_down": [(sum_d[0], crecv_d)],
                "ffn1_up": [(sum_lo[0], crecv_lo), (sum_hi[0], None)]})
    return small_out[-1], dx0, out, small_out[:-1], _rs_chips_exchange(sum_hi)


def kernel(x, norm_ffn1, ffn1_gate, ffn1_up, ffn1_down, norm_mix, w_in, ret_gn_gain, pool_w, pool_scale, w_out, norm_ffn2, ffn2_gate, ffn2_up, ffn2_down, norm_final, loss_target, m_norm_ffn1, m_ffn1_gate, m_ffn1_up, m_ffn1_down, m_norm_mix, m_w_in, m_ret_gn_gain, m_pool_w, m_pool_scale, m_w_out, m_norm_ffn2, m_ffn2_gate, m_ffn2_up, m_ffn2_down, m_norm_final, v_norm_ffn1, v_ffn1_gate, v_ffn1_up, v_ffn1_down, v_norm_mix, v_w_in, v_ret_gn_gain, v_pool_w, v_pool_scale, v_w_out, v_norm_ffn2, v_ffn2_gate, v_ffn2_up, v_ffn2_down, v_norm_final):
    nseq, seq, d = x.shape
    t = nseq * seq
    f_loc = ffn1_gate.shape[2]
    f_pad = _pad_to(f_loc, LANE)
    xf = x.reshape(t, d)
    tgt = loss_target.reshape(t, d)
    core = lax.axis_index("c").astype(jnp.int32).reshape(1)
    chip = (2 * lax.axis_index("x") + lax.axis_index("y")).astype(jnp.int32).reshape(1)

    colp = lambda w: jnp.pad(w[0].astype(BF16), ((0, 0), (0, f_pad - f_loc)))
    rowp = lambda w: jnp.pad(w[0].astype(BF16), ((0, f_pad - f_loc), (0, 0)))
    gate2, up2 = colp(ffn2_gate), colp(ffn2_up)
    cols1 = jnp.concatenate([colp(ffn1_gate), colp(ffn1_up)], axis=0)
    cols2_t = jnp.concatenate([gate2.T, up2.T], axis=1)
    (cols1_all,) = _all_gather([(cols1, "col")], "all_gather_ffn1")
    d1_loc, d2_loc, win_loc, wout_loc = rowp(ffn1_down), rowp(ffn2_down), w_in[0].astype(BF16), w_out[0].astype(BF16)
    later = {"ffn1_act": [(d1_loc, "row"), (win_loc, "col"), (wout_loc, "row"), (gate2, "col"), (up2, "col")],
             "ffn1_down": [(d2_loc, "row")],
             "ffn2_fwd": [(cols2_t, "row"), (d2_loc.T, "col"), (win_loc.T, "row"), (wout_loc.T, "col")],
             "ffn2_bwd_act": [(cols1.T, "row"), (d1_loc.T, "col")]}

    flat = lambda a: a.reshape(pool_w.size // d, d)
    params = [norm_ffn1, norm_mix, ret_gn_gain, flat(pool_w), pool_scale, norm_ffn2, norm_final.reshape(1, d)]
    moms = [m_norm_ffn1, m_norm_mix, m_ret_gn_gain, flat(m_pool_w), m_pool_scale, m_norm_ffn2, m_norm_final.reshape(1, d)]
    vels = [v_norm_ffn1, v_norm_mix, v_ret_gn_gain, flat(v_pool_w), v_pool_scale, v_norm_ffn2, v_norm_final.reshape(1, d)]

    def small_step(parts, loss_part, ride):
        dn1, dn2, dgain, dwp, dscale, dn3, dnf = parts
        return _small_allreduce_adam([dn1, dn2, dgain, flat(dwp), dscale, dn3, dnf], params, moms, vels, [loss_part],
                                     "small_allreduce_adam", ride)

    small_w = (norm_ffn1, norm_mix, ret_gn_gain, pool_w[0], pool_scale, norm_ffn2, norm_final.reshape(1, d))
    loss_sum, dx0, reduced, small_out, pending = _local_step(xf, tgt, nseq, seq, cols1_all, later, small_w, core,
                                                             small_step)

    local = {"ffn1_gate": (ffn1_gate, m_ffn1_gate, v_ffn1_gate), "ffn1_up": (ffn1_up, m_ffn1_up, v_ffn1_up),
             "ffn1_down": (ffn1_down, m_ffn1_down, v_ffn1_down), "w_in": (w_in, m_w_in, v_w_in),
             "w_out": (w_out, m_w_out, v_w_out), "ffn2_gate": (ffn2_gate, m_ffn2_gate, v_ffn2_gate),
             "ffn2_up": (ffn2_up, m_ffn2_up, v_ffn2_up), "ffn2_down": (ffn2_down, m_ffn2_down, v_ffn2_down)}
    flip = lambda nm: nm.endswith("gate") or nm.endswith("up")

    def item(nm):
        view = (lambda a: a[0].T) if flip(nm) else (lambda a: a[0])
        w, m, v = local[nm]
        return reduced[nm], view(w), view(m), view(v)

    big = {}

    def keep(names, res):
        for q, nm in enumerate(names):
            big[nm] = tuple((a.T if flip(nm) else a)[None] for a in res[4 * q:4 * q + 4])

    second = ["ffn2_gate", "ffn2_up", "ffn2_down"]
    res, (last_recv,) = _chip_sum_adam([item(nm) for nm in second], chip, item(second[0])[1].shape[0] // 2, "adam_ffn2",
                                       ride=pending)
    keep(second, res)
    reduced["ffn1_up"][-1] = (reduced["ffn1_up"][-1][0], last_recv)
    first = ["ffn1_gate", "ffn1_up", "ffn1_down"]
    keep(first, _chip_sum_adam([item(nm) for nm in first], chip, item(first[0])[1].shape[0] // 2, "adam_ffn1"))
    for nm in ["w_in", "w_out"]:
        keep([nm], _chip_sum_adam([item(nm)], chip, min(item(nm)[1].shape[0], TM), "adam_" + nm))

    small_names = ["norm_ffn1", "norm_mix", "ret_gn_gain", "pool_w", "pool_scale", "norm_ffn2", "norm_final"]
    shapes = [norm_ffn1.shape, norm_mix.shape, ret_gn_gain.shape, pool_w.shape, pool_scale.shape, norm_ffn2.shape,
              norm_final.shape]
    small = {nm: tuple(small_out[4 * p + q].reshape(shapes[p]) for q in range(4)) for p, nm in enumerate(small_names)}

    loss = loss_sum[0, 0]
    order = ["norm_ffn1", "ffn1_gate", "ffn1_up", "ffn1_down", "norm_mix", "w_in", "ret_gn_gain", "pool_w", "pool_scale",
             "w_out", "norm_ffn2", "ffn2_gate", "ffn2_up", "ffn2_down", "norm_final"]
    both = {**big, **small}
    outs = [loss, dx0.reshape(nseq, seq, d)]
    for q in range(4):
        outs += [both[nm][q] for nm in order]
    return tuple(outs)
```

```python
import numpy as np
import jax
import jax.numpy as jnp
from jax import lax
from jax.experimental import pallas as pl
from jax.experimental.pallas import tpu as pltpu

F32, BF16 = jnp.float32, jnp.bfloat16
MESH_ID = pl.DeviceIdType.MESH
ANY = pl.BlockSpec(memory_space=pl.ANY)
VMEM_SPEC = pl.BlockSpec(memory_space=pltpu.VMEM)

N_DEV = 8
RMS_EPS = 1e-6
GN_EPS = 1e-5
HEADS, DK, DV = 4, 64, 128
QK_W, V_W, POOL_W = HEADS * DK, HEADS * DV, 512
WINDOWS = (2, 4, 8, 16)
GC = POOL_W // len(WINDOWS)
CHUNK = 64
BLK = 4 * CHUNK
MIX_BLOCKS = 4
HALO = 16
ROPE_BASE = 10000.0
LR, B1, B2, ADAM_EPS, WD, STEP = 0.001, 0.9, 0.999, 1e-08, 0.01, 10
LANE = 128
TM = 512
FFN_TM = 1024
FFN_FWD_TF = 768
WGRAD_TT = 4096
VMEM_LIMIT = 56 * 1024 * 1024


def _cparams(n_axes):
    return pltpu.CompilerParams(dimension_semantics=("arbitrary",) * n_axes, vmem_limit_bytes=VMEM_LIMIT)


class _Exchange:
    def __init__(self, inputs, out_shape, scratch, start, finish, mid=None):
        self.inputs, self.out_shape, self.scratch = list(inputs), list(out_shape), list(scratch)
        self.start, self.finish, self.mid = start, finish, mid


def _pallas(body, name, grid, in_specs, out_specs, out_shape, scratch_shapes, args, ride=None):
    n_axes = len(grid)
    if ride is None:
        return pl.pallas_call(body, name=name, grid=grid, in_specs=in_specs, out_specs=out_specs, out_shape=out_shape,
                              scratch_shapes=scratch_shapes, compiler_params=_cparams(n_axes))(*args)
    n_in, n_out, n_scr = len(in_specs), len(out_specs), len(scratch_shapes)
    r_in, r_out = len(ride.inputs), len(ride.out_shape)

    def hosted(*refs):
        ins, refs = refs[:n_in], refs[n_in:]
        r_ins, refs = refs[:r_in], refs[r_in:]
        outs, refs = refs[:n_out], refs[n_out:]
        r_outs, refs = refs[:r_out], refs[r_out:]
        scr, sems = refs[:n_scr], refs[n_scr:]
        ids = [pl.program_id(a) for a in range(n_axes)]
        first, last, inner0 = ids[0] == 0, ids[0] == grid[0] - 1, None
        for a in range(1, n_axes):
            first = first & (ids[a] == 0)
            last = last & (ids[a] == grid[a] - 1)
            inner0 = (ids[a] == 0) if inner0 is None else inner0 & (ids[a] == 0)

        @pl.when(first)
        def _():
            ride.start(r_ins, r_outs, sems)

        if ride.mid is not None:
            at_mid = ids[0] == grid[0] - 1
            if inner0 is not None:
                at_mid = at_mid & inner0

            @pl.when(at_mid)
            def _():
                ride.mid(r_ins, r_outs, sems)

        body(*ins, *outs, *scr)

        @pl.when(last)
        def _():
            ride.finish(r_ins, r_outs, sems)

    res = pl.pallas_call(
        hosted, name=name, grid=grid, in_specs=list(in_specs) + [ANY] * r_in, out_specs=list(out_specs) + [ANY] * r_out,
        out_shape=list(out_shape) + ride.out_shape, scratch_shapes=list(scratch_shapes) + ride.scratch,
        compiler_params=_cparams(n_axes))(*args, *ride.inputs)
    return res[:n_out], res[n_out:]


def _dot(a, b):
    return jnp.dot(a, b, preferred_element_type=F32)


def _dot_nt(a, b):
    return lax.dot_general(a, b, (((1,), (1,)), ((), ())), preferred_element_type=F32)


def _dot_tn(a, b):
    return lax.dot_general(a, b, (((0,), (0,)), ((), ())), preferred_element_type=F32)


def _sigmoid(x):
    return 0.5 * jnp.tanh(0.5 * x) + 0.5


def _pad_to(n, m):
    return (n + m - 1) // m * m


def _retention_constants():
    gamma = (1.0 - 2.0 ** (-5.0 - np.arange(HEADS, dtype=np.float32))).astype(np.float32)
    log_g = np.log(gamma).astype(np.float32)
    i = np.arange(BLK)
    diff = (i[:, None] - i[None, :]).astype(np.float32)
    same = (i[:, None] // CHUNK) == (i[None, :] // CHUNK)
    earlier = (i[None, :] // CHUNK) < (i[:, None] // CHUNK)
    mask = np.zeros((HEADS, BLK, BLK), np.float32)
    for h in range(HEADS):
        dec_abs = np.exp(log_g[h] * np.abs(diff)).astype(np.float32)
        dec = np.exp(log_g[h] * diff * earlier).astype(np.float32)
        mask[h] = np.where(same, dec_abs, np.where(earlier, dec, 0.0))
    dq = np.zeros((BLK, V_W), np.float32)
    dk = np.zeros((BLK, QK_W), np.float32)
    gbd = np.zeros((QK_W, V_W), np.float32)
    for h in range(HEADS):
        dq[:, h * DV:(h + 1) * DV] = np.exp(log_g[h] * (i + 1.0)).astype(np.float32)[:, None]
        dk[:, h * DK:(h + 1) * DK] = np.exp(log_g[h] * (BLK - 1.0 - i)).astype(np.float32)[:, None]
        gbd[h * DK:(h + 1) * DK, h * DV:(h + 1) * DV] = np.exp(log_g[h] * np.float32(BLK))
    bd = (gbd > 0).astype(np.float32)
    return jnp.asarray(mask), jnp.asarray(dq), jnp.asarray(dk), jnp.asarray(gbd), jnp.asarray(bd)


def _rotary_tables(seq):
    half = DK // 2
    freqs = ROPE_BASE ** (-jnp.arange(half, dtype=F32) * 2.0 / DK)
    ang = jnp.arange(seq, dtype=F32)[:, None] * freqs[None, :]
    cos, sin = jnp.cos(ang), jnp.sin(ang)
    cos_t = jnp.tile(jnp.concatenate([cos, cos], axis=1), (1, HEADS))
    sin_t = jnp.tile(jnp.concatenate([-sin, sin], axis=1), (1, HEADS))
    return cos_t, sin_t


def _swap_halves(x):
    lane = lax.broadcasted_iota(jnp.int32, (1, QK_W), 1)
    first = (lane & (DK - 1)) < DK // 2
    return jnp.where(first, pltpu.roll(x, QK_W - DK // 2, 1), pltpu.roll(x, DK // 2, 1))


def _head_mask(h):
    lane = lax.broadcasted_iota(jnp.int32, (1, QK_W), 1)
    return (lane >= h * DK) & (lane < (h + 1) * DK)


def _ffn_fwd(x, n, gate, up, wd, name, ride=None):
    t, d = x.shape
    (wg, gq), (wu, uq) = gate, up
    fp = wg.shape[1]
    tm = min(t, FFN_TM)
    tf = FFN_FWD_TF
    nj = fp // tf

    def body(x_ref, n_ref, wg_ref, wu_ref, wd_ref, xo_ref, h_ref, b_ref, sil_ref, dsil_ref, acc_ref):
        j = pl.program_id(1)

        @pl.when(j == 0)
        def _():
            xv = x_ref[...]
            r = lax.rsqrt(jnp.mean(xv * xv, axis=-1, keepdims=True) + RMS_EPS)
            h_ref[...] = (xv * r * n_ref[...]).astype(BF16)
            acc_ref[...] = jnp.zeros_like(acc_ref)

        h = h_ref[...]
        a = _dot(h, wg_ref[...])
        b = _dot(h, wu_ref[...])
        sg = _sigmoid(a)
        sil = a * sg
        b_ref[...] = b.astype(BF16)
        sil_ref[...] = sil.astype(BF16)
        dsil_ref[...] = (sg + sil * (1.0 - sg)).astype(BF16)
        acc_ref[...] += _dot((sil * b).astype(BF16), wd_ref[...])

        @pl.when(j == nj - 1)
        def _():
            xo_ref[...] = x_ref[...] + 0.5 * acc_ref[...]

    act = pl.BlockSpec((tm, tf), lambda i, j: (i, j))
    return _pallas(
        body, name, (t // tm, nj),
        [pl.BlockSpec((tm, d), lambda i, j: (i, 0)), pl.BlockSpec((1, d), lambda i, j: (0, 0)),
         pl.BlockSpec((d, tf), lambda i, j: (gq, j)), pl.BlockSpec((d, tf), lambda i, j: (uq, j)),
         pl.BlockSpec((tf, d), lambda i, j: (j, 0))],
        [pl.BlockSpec((tm, d), lambda i, j: (i, 0)), pl.BlockSpec((tm, d), lambda i, j: (i, 0)), act, act, act],
        [jax.ShapeDtypeStruct((t, d), F32), jax.ShapeDtypeStruct((t, d), BF16)] + [jax.ShapeDtypeStruct((t, fp), BF16)] * 3,
        [pltpu.VMEM((tm, d), F32)], (x, n, wg, wu, wd), ride)


def _ffn_act(x, n, gate, up, name, ride=None):
    t, d = x.shape
    (wg, gq), (wu, uq) = gate, up
    fp = wg.shape[1]
    tm = min(t, FFN_TM)
    tf = 2 * fp // N_DEV
    nj = fp // tf

    def body(x_ref, n_ref, wg_ref, wu_ref, h_ref, b_ref, sil_ref, dsil_ref, s_ref):
        @pl.when(pl.program_id(1) == 0)
        def _():
            xv = x_ref[...]
            r = lax.rsqrt(jnp.mean(xv * xv, axis=-1, keepdims=True) + RMS_EPS)
            h_ref[...] = (xv * r * n_ref[...]).astype(BF16)

        h = h_ref[...]
        a = _dot(h, wg_ref[...])
        b = _dot(h, wu_ref[...])
        sg = _sigmoid(a)
        sil = a * sg
        b_ref[...] = b.astype(BF16)
        sil_ref[...] = sil.astype(BF16)
        dsil_ref[...] = (sg + sil * (1.0 - sg)).astype(BF16)
        s_ref[...] = (sil * b).astype(BF16)

    act = pl.BlockSpec((tm, tf), lambda i, j: (i, j))
    return _pallas(
        body, name, (t // tm, nj),
        [pl.BlockSpec((tm, d), lambda i, j: (i, 0)), pl.BlockSpec((1, d), lambda i, j: (0, 0)),
         pl.BlockSpec((d, tf), lambda i, j: (gq, j)), pl.BlockSpec((d, tf), lambda i, j: (uq, j))],
        [pl.BlockSpec((tm, d), lambda i, j: (i, 0)), act, act, act, act],
        [jax.ShapeDtypeStruct((t, d), BF16)] + [jax.ShapeDtypeStruct((t, fp), BF16)] * 4,
        [], (x, n, wg, wu), ride)


def _ffn_down(s, x, wd, name, ride=None):
    t, d = x.shape
    fp = wd.shape[0]
    tm = min(t, FFN_TM)

    def body(s_ref, x_ref, wd_ref, xo_ref):
        xo_ref[...] = x_ref[...] + 0.5 * _dot(s_ref[...], wd_ref[...])

    row = pl.BlockSpec((tm, d), lambda i: (i, 0))
    res = _pallas(body, name, (t // tm,), [pl.BlockSpec((tm, fp), lambda i: (i, 0)), row, pl.BlockSpec((fp, d), lambda i: (0, 0))],
                  [row], [jax.ShapeDtypeStruct((t, d), F32)], [], (s, x, wd), ride)
    return res[0] if ride is None else (res[0][0], res[1])


def _ffn_bwd_act(dxob, b, sil, dsil, wd_t, name, ride=None):
    t, d = dxob.shape
    fp = wd_t.shape[1]
    tm = min(t, FFN_TM)
    tf = 2 * fp // N_DEV
    ni = t // tm

    def body(dx_ref, b_ref, sil_ref, dsil_ref, wd_ref, da_ref, db_ref, gd_ref, acc_ref):
        i = pl.program_id(1)

        @pl.when(i == 0)
        def _():
            acc_ref[...] = jnp.zeros_like(acc_ref)

        dxv = dx_ref[...]
        bv, sv = b_ref[...].astype(F32), sil_ref[...].astype(F32)
        ds = _dot(dxv, wd_ref[...])
        da_ref[...] = (ds * bv * dsil_ref[...].astype(F32)).astype(BF16)
        db_ref[...] = (ds * sv).astype(BF16)
        acc_ref[...] += _dot_tn((sv * bv).astype(BF16), dxv)

        @pl.when(i == ni - 1)
        def _():
            gd_ref[...] = acc_ref[...].astype(BF16)

    act = pl.BlockSpec((tm, tf), lambda c, i: (i, c))
    return _pallas(
        body, name, (fp // tf, ni),
        [pl.BlockSpec((tm, d), lambda c, i: (i, 0)), act, act, act, pl.BlockSpec((d, tf), lambda c, i: (0, c))],
        [act, act, pl.BlockSpec((tf, d), lambda c, i: (c, 0))],
        [jax.ShapeDtypeStruct((t, fp), BF16), jax.ShapeDtypeStruct((t, fp), BF16), jax.ShapeDtypeStruct((fp, d), BF16)],
        [pltpu.VMEM((tf, d), F32)], (dxob, b, sil, dsil, wd_t), ride)


def _ffn_bwd_in(da, db, dxo, x, n, cols_t, gq, name, ride=None):
    t, d = x.shape
    fp = cols_t.shape[0]
    tm = min(t, FFN_TM)
    tf = fp // 3
    nj = fp // tf

    def body(da_ref, db_ref, dxo_ref, x_ref, n_ref, wg_ref, wu_ref, dx_ref, dn_ref, acc_ref):
        i, j = pl.program_id(0), pl.program_id(1)

        @pl.when((i == 0) & (j == 0))
        def _():
            dn_ref[...] = jnp.zeros_like(dn_ref)

        @pl.when(j == 0)
        def _():
            acc_ref[...] = jnp.zeros_like(acc_ref)

        acc_ref[...] += _dot(da_ref[...], wg_ref[...]) + _dot(db_ref[...], wu_ref[...])

        @pl.when(j == nj - 1)
        def _():
            xv = x_ref[...]
            r = lax.rsqrt(jnp.mean(xv * xv, axis=-1, keepdims=True) + RMS_EPS)
            xh = xv * r
            dh = acc_ref[...]
            dn_ref[...] += jnp.sum(dh * xh, axis=0, keepdims=True)
            dhn = dh * n_ref[...]
            dx_ref[...] = dxo_ref[...] + r * (dhn - xh * jnp.mean(dhn * xh, axis=-1, keepdims=True))

    act = pl.BlockSpec((tm, tf), lambda i, j: (i, j))
    row = pl.BlockSpec((tm, d), lambda i, j: (i, 0))
    return _pallas(
        body, name, (t // tm, nj),
        [act, act, row, row, pl.BlockSpec((1, d), lambda i, j: (0, 0)),
         pl.BlockSpec((tf, d), lambda i, j: (j, gq)), pl.BlockSpec((tf, d), lambda i, j: (j, gq + 1))],
        [row, pl.BlockSpec((1, d), lambda i, j: (0, 0))],
        [jax.ShapeDtypeStruct((t, d), F32), jax.ShapeDtypeStruct((1, d), F32)],
        [pltpu.VMEM((tm, d), F32)], (da, db, dxo, x, n, cols_t, cols_t), ride)


def _wgrad(a, b, scale, tk, tn, name, ride=None, b_cols=None):
    t, k = a.shape
    q0, nq = (0, b.shape[1] // tn) if b_cols is None else b_cols
    n = nq * tn
    tt = min(t, WGRAD_TT)
    nt = t // tt

    def body(a_ref, b_ref, o_ref, acc_ref):
        s = pl.program_id(2)

        @pl.when(s == 0)
        def _():
            acc_ref[...] = jnp.zeros_like(acc_ref)

        acc_ref[...] += _dot_tn(a_ref[...], b_ref[...])

        @pl.when(s == nt - 1)
        def _():
            o_ref[...] = (scale * acc_ref[...]).astype(BF16)

    res = _pallas(
        body, name, (k // tk, n // tn, nt),
        [pl.BlockSpec((tt, tk), lambda p, q, s: (s, p)), pl.BlockSpec((tt, tn), lambda p, q, s: (s, q + q0))],
        [pl.BlockSpec((tk, tn), lambda p, q, s: (p, q))], [jax.ShapeDtypeStruct((k, n), BF16)],
        [pltpu.VMEM((tk, tn), F32)], (a, b), ride)
    return res[0] if ride is None else (res[0][0], res[1])


def _mix_in(x, n, w_in, cos_t, sin_t, seq, name):
    t, d = x.shape
    per_seq = seq // TM

    def body(x_ref, n_ref, w_ref, c_ref, s_ref, h_ref, q_ref, k_ref, v_ref, g_ref, u_ref):
        xv = x_ref[...]
        r = lax.rsqrt(jnp.mean(xv * xv, axis=-1, keepdims=True) + RMS_EPS)
        h = (xv * r * n_ref[...]).astype(BF16)
        h_ref[...] = h
        p = _dot(h, w_ref[...])
        c, s = c_ref[...], s_ref[...]
        q = p[:, :QK_W]
        k = p[:, QK_W:2 * QK_W]
        q_ref[...] = ((q * c + _swap_halves(q) * s) * (DK ** -0.5)).astype(BF16)
        k_ref[...] = (k * c + _swap_halves(k) * s).astype(BF16)
        v_ref[...] = p[:, 2 * QK_W:2 * QK_W + V_W].astype(BF16)
        g_ref[...] = p[:, 2 * QK_W + V_W:2 * QK_W + 2 * V_W]
        u_ref[...] = p[:, 2 * QK_W + 2 * V_W:]

    tile = lambda w: pl.BlockSpec((TM, w), lambda i: (i, 0))
    return pl.pallas_call(
        body, name=name, grid=(t // TM,),
        in_specs=[tile(d), pl.BlockSpec((1, d), lambda i: (0, 0)), pl.BlockSpec(w_in.shape, lambda i: (0, 0)),
                  pl.BlockSpec((TM, QK_W), lambda i: (i % per_seq, 0)), pl.BlockSpec((TM, QK_W), lambda i: (i % per_seq, 0))],
        out_specs=[tile(d), tile(QK_W), tile(QK_W), tile(V_W), tile(V_W), tile(POOL_W)],
        out_shape=[jax.ShapeDtypeStruct((t, d), BF16), jax.ShapeDtypeStruct((t, QK_W), BF16),
                   jax.ShapeDtypeStruct((t, QK_W), BF16), jax.ShapeDtypeStruct((t, V_W), BF16),
                   jax.ShapeDtypeStruct((t, V_W), F32), jax.ShapeDtypeStruct((t, POOL_W), F32)],
        compiler_params=_cparams(1),
    )(x, n, w_in, cos_t, sin_t)


def _mix_in_bwd(dp, dx2, x1, n, w_in_t, name, ride=None):
    t, d = x1.shape

    def body(dp_ref, dx2_ref, x_ref, n_ref, w_ref, dx_ref, dn_ref, dxb_ref):
        @pl.when(pl.program_id(0) == 0)
        def _():
            dn_ref[...] = jnp.zeros_like(dn_ref)

        dh = _dot(dp_ref[...], w_ref[...])
        xv = x_ref[...]
        r = lax.rsqrt(jnp.mean(xv * xv, axis=-1, keepdims=True) + RMS_EPS)
        xh = xv * r
        dn_ref[...] += jnp.sum(dh * xh, axis=0, keepdims=True)
        dhn = dh * n_ref[...]
        dx = dx2_ref[...] + r * (dhn - xh * jnp.mean(dhn * xh, axis=-1, keepdims=True))
        dx_ref[...] = dx
        dxb_ref[...] = (0.5 * dx).astype(BF16)

    tile = lambda w: pl.BlockSpec((TM, w), lambda i: (i, 0))
    return _pallas(
        body, name, (t // TM,),
        [tile(dp.shape[1]), tile(d), tile(d), pl.BlockSpec((1, d), lambda i: (0, 0)),
         pl.BlockSpec(w_in_t.shape, lambda i: (0, 0))],
        [tile(d), pl.BlockSpec((1, d), lambda i: (0, 0)), tile(d)],
        [jax.ShapeDtypeStruct((t, d), F32), jax.ShapeDtypeStruct((1, d), F32), jax.ShapeDtypeStruct((t, d), BF16)],
        [], (dp, dx2, x1, n, w_in_t), ride)


def _group_norm(o):
    parts, rstds = [], []
    for h in range(HEADS):
        oh = o[:, h * DV:(h + 1) * DV]
        dlt = oh - jnp.mean(oh, axis=-1, keepdims=True)
        rstd = lax.rsqrt(jnp.mean(dlt * dlt, axis=-1, keepdims=True) + GN_EPS)
        parts.append(dlt * rstd)
        rstds.append(rstd)
    return jnp.concatenate(parts, axis=1), rstds


def _mix_core_fwd(qs, k, v, g, u, x1, consts, gain, wp, scale, w_out, nseq, seq, name):
    t, d = x1.shape
    nb = min(MIX_BLOCKS, seq // BLK)
    nstep = seq // (nb * BLK)
    mask, dq, dk, gbd, bd = consts

    def body(q_ref, k_ref, v_ref, g_ref, u_ref, x1_ref, m_ref, dq_ref, dk_ref, gbd_ref, bd_ref, gain_ref, wp_ref,
             sc_ref, wo_ref, x2_ref, mix_ref, o_ref, pooled_ref, st_ref, state, halo):
        j = pl.program_id(1)

        @pl.when(j == 0)
        def _():
            state[...] = jnp.zeros_like(state)
            halo[...] = jnp.zeros_like(halo)

        for s in range(nb):
            rows = pl.ds(s * BLK, BLK)
            qv, kv, vv = q_ref[rows, :], k_ref[rows, :], v_ref[rows, :]
            st = state[...]
            st_ref[s] = st
            cross = _dot(qv, st.astype(BF16)) * dq_ref[...]
            outs = []
            for h in range(HEADS):
                qh = jnp.where(_head_mask(h), qv, jnp.zeros_like(qv))
                am = (_dot_nt(qh, kv) * m_ref[h]).astype(BF16)
                outs.append(_dot(am, vv[:, h * DV:(h + 1) * DV]))
            o = jnp.concatenate(outs, axis=1) + cross
            o_ref[rows, :] = o
            kd = (kv.astype(F32) * dk_ref[...]).astype(BF16)
            state[...] = gbd_ref[...] * st + _dot_tn(kd, vv) * bd_ref[...]

            gv = g_ref[rows, :]
            nrm, _ = _group_norm(o)
            ret = (gv * _sigmoid(gv)) * (nrm * gain_ref[...])

            uv = u_ref[rows, :]
            c = jnp.concatenate([halo[...], uv], axis=0)
            halo[...] = uv[BLK - HALO:, :]
            pos = (j * nb + s) * BLK + lax.broadcasted_iota(jnp.int32, (BLK, 1), 0)
            parts = []
            for gi, w in enumerate(WINDOWS):
                c = c + pltpu.roll(c, w // 2, 0)
                cnt = jnp.minimum(pos + 1, w).astype(F32)
                parts.append(c[HALO:, :GC] / cnt)
                if gi + 1 < len(WINDOWS):
                    c = c[:, GC:]
            pooled = (jnp.concatenate(parts, axis=1) - uv).astype(BF16)
            pooled_ref[rows, :] = pooled
            z = jnp.concatenate([_dot(pooled[:, gi * GC:(gi + 1) * GC], wp_ref[gi]) for gi in range(len(WINDOWS))],
                                axis=1)
            mix = jnp.concatenate([ret, z * sc_ref[...]], axis=1).astype(BF16)
            mix_ref[rows, :] = mix
            x2_ref[rows, :] = x1_ref[rows, :] + _dot(mix, wo_ref[...])

    blk = lambda w: pl.BlockSpec((nb * BLK, w), lambda i, j: (i * nstep + j, 0))
    full = lambda a: pl.BlockSpec(a.shape, lambda i, j: (0,) * a.ndim)
    return _pallas(
        body, name, (nseq, nstep),
        [blk(QK_W), blk(QK_W), blk(V_W), blk(V_W), blk(POOL_W), blk(d),
         full(mask), full(dq), full(dk), full(gbd), full(bd), full(gain), full(wp), full(scale), full(w_out)],
        [blk(d), blk(d), blk(V_W), blk(POOL_W), pl.BlockSpec((nb, QK_W, V_W), lambda i, j: (i * nstep + j, 0, 0))],
        [jax.ShapeDtypeStruct((t, d), F32), jax.ShapeDtypeStruct((t, d), BF16),
         jax.ShapeDtypeStruct((t, V_W), F32), jax.ShapeDtypeStruct((t, POOL_W), BF16),
         jax.ShapeDtypeStruct((t // BLK, QK_W, V_W), F32)],
        [pltpu.VMEM((QK_W, V_W), F32), pltpu.VMEM((HALO, POOL_W), F32)],
        (qs, k, v, g, u, x1, mask, dq, dk, gbd, bd, gain, wp, scale, w_out))


def _mix_core_bwd(dx2, qs, k, v, g, o, pooled, st, consts, gain, wp, scale, w_out, cos_t, sin_t, nseq, seq, name,
                  ride=None):
    t, d = dx2.shape
    nb = min(MIX_BLOCKS, seq // BLK)
    nstep = seq // (nb * BLK)
    mask, dq, dk, gbd, bd = consts
    n_win = len(WINDOWS)

    def body(dx2_ref, q_ref, k_ref, v_ref, g_ref, o_ref, pooled_ref, st_ref, m_ref, dq_ref, dk_ref, gbd_ref, bd_ref,
             gain_ref, wp_ref, sc_ref, wo_ref, c_ref, s_ref,
             dp_ref, dx2b_ref, dgain_ref, dscale_ref, dwp_ref, rstate, carry):
        i, j = pl.program_id(0), pl.program_id(1)

        @pl.when((i == 0) & (j == 0))
        def _():
            dgain_ref[...] = jnp.zeros_like(dgain_ref)
            dscale_ref[...] = jnp.zeros_like(dscale_ref)
            dwp_ref[...] = jnp.zeros_like(dwp_ref)

        @pl.when(j == 0)
        def _():
            rstate[...] = jnp.zeros_like(rstate)
            carry[...] = jnp.zeros_like(carry)

        for s in reversed(range(nb)):
            rows = pl.ds(s * BLK, BLK)
            dx2b = dx2_ref[rows, :].astype(BF16)
            dx2b_ref[rows, :] = dx2b
            dmix = _dot(dx2b, wo_ref[...])
            dret, dpool = dmix[:, :V_W], dmix[:, V_W:]

            gv, ov, gain_v = g_ref[rows, :], o_ref[rows, :], gain_ref[...]
            sg = _sigmoid(gv)
            sil = gv * sg
            nrm, rstds = _group_norm(ov)
            dg = dret * (nrm * gain_v) * (sg * (1.0 + gv * (1.0 - sg)))
            dgn = dret * sil
            dgain_ref[...] += jnp.sum(dgn * nrm, axis=0, keepdims=True)
            dnrm = dgn * gain_v
            do_parts = []
            for h in range(HEADS):
                dn_h = dnrm[:, h * DV:(h + 1) * DV]
                n_h = nrm[:, h * DV:(h + 1) * DV]
                do_parts.append(rstds[h] * (dn_h - jnp.mean(dn_h, axis=-1, keepdims=True)
                                            - n_h * jnp.mean(dn_h * n_h, axis=-1, keepdims=True)))
            do = jnp.concatenate(do_parts, axis=1)
            dob = do.astype(BF16)

            qv, kv, vv = q_ref[rows, :], k_ref[rows, :], v_ref[rows, :]
            stb = st_ref[s].astype(BF16)
            rs = rstate[...]
            rsb = rs.astype(BF16)
            dod = (do * dq_ref[...]).astype(BF16)
            dqs = _dot_nt(dod, stb)
            dst = _dot_tn(qv, dod) * bd_ref[...]
            dkf = dk_ref[...]
            kd = (kv.astype(F32) * dkf).astype(BF16)
            dks = _dot_nt(vv, rsb) * dkf
            dvs = _dot(kd, rsb)
            dv_parts = []
            for h in range(HEADS):
                hm = _head_mask(h)
                qh = jnp.where(hm, qv, jnp.zeros_like(qv))
                kh = jnp.where(hm, kv, jnp.zeros_like(kv))
                mh = m_ref[h]
                am = (_dot_nt(qh, kv) * mh).astype(BF16)
                dpm = (_dot_nt(dob[:, h * DV:(h + 1) * DV], vv[:, h * DV:(h + 1) * DV]) * mh).astype(BF16)
                dqs = dqs + _dot(dpm, kh)
                dks = dks + _dot_tn(dpm, qh)
                dv_parts.append(_dot_tn(am, dob[:, h * DV:(h + 1) * DV]))
            dvs = dvs + jnp.concatenate(dv_parts, axis=1)
            rstate[...] = dst + gbd_ref[...] * rs

            cv, sv = c_ref[rows, :], s_ref[rows, :]
            dqr = dqs * (DK ** -0.5)
            dq_pre = dqr * cv + _swap_halves(dqr * sv)
            dk_pre = dks * cv + _swap_halves(dks * sv)

            pv = pooled_ref[rows, :]
            sc = sc_ref[...]
            dzb = (dpool * sc).astype(BF16)
            z_parts, dpo_parts = [], []
            for gi in range(n_win):
                p_g = pv[:, gi * GC:(gi + 1) * GC]
                dz_g = dzb[:, gi * GC:(gi + 1) * GC]
                z_parts.append(_dot(p_g, wp_ref[gi]))
                dwp_ref[gi] += _dot_tn(p_g, dz_g)
                dpo_parts.append(_dot_nt(dz_g, wp_ref[gi]))
            dscale_ref[...] += jnp.sum(dpool * jnp.concatenate(z_parts, axis=1), axis=0, keepdims=True)
            dpo = jnp.concatenate(dpo_parts, axis=1)
            pos = ((nstep - 1 - j) * nb + s) * BLK + lax.broadcasted_iota(jnp.int32, (BLK, 1), 0)
            e = jnp.concatenate(
                [dpo[:, gi * GC:(gi + 1) * GC] / jnp.minimum(pos + 1, w).astype(F32) for gi, w in enumerate(WINDOWS)],
                axis=1)
            c = jnp.concatenate([e, carry[...]], axis=0)
            carry[...] = e[:HALO, :]
            span = BLK + HALO
            lead = []
            for gi, w in enumerate(WINDOWS):
                c = c + pltpu.roll(c, span - w // 2, 0)
                lead.append(c[:BLK, :GC])
                if gi + 1 < n_win:
                    c = c[:, GC:]
            du = jnp.concatenate(lead, axis=1) - dpo

            dp_ref[rows, 0:QK_W] = dq_pre.astype(BF16)
            dp_ref[rows, QK_W:2 * QK_W] = dk_pre.astype(BF16)
            dp_ref[rows, 2 * QK_W:2 * QK_W + V_W] = dvs.astype(BF16)
            dp_ref[rows, 2 * QK_W + V_W:2 * QK_W + 2 * V_W] = dg.astype(BF16)
            dp_ref[rows, 2 * QK_W + 2 * V_W:] = du.astype(BF16)

    rev = lambda i, j: i * nstep + (nstep - 1 - j)
    blk = lambda w: pl.BlockSpec((nb * BLK, w), lambda i, j: (rev(i, j), 0))
    full = lambda a: pl.BlockSpec(a.shape, lambda i, j: (0,) * a.ndim)
    in_w = 2 * QK_W + 2 * V_W + POOL_W
    return _pallas(
        body, name, (nseq, nstep),
        [blk(d), blk(QK_W), blk(QK_W), blk(V_W), blk(V_W), blk(V_W), blk(POOL_W),
         pl.BlockSpec((nb, QK_W, V_W), lambda i, j: (rev(i, j), 0, 0)),
         full(mask), full(dq), full(dk), full(gbd), full(bd), full(gain), full(wp), full(scale), full(w_out),
         pl.BlockSpec((nb * BLK, QK_W), lambda i, j: (nstep - 1 - j, 0)),
         pl.BlockSpec((nb * BLK, QK_W), lambda i, j: (nstep - 1 - j, 0))],
        [blk(in_w), blk(d), pl.BlockSpec((1, V_W), lambda i, j: (0, 0)),
         pl.BlockSpec((1, POOL_W), lambda i, j: (0, 0)), pl.BlockSpec((n_win, GC, GC), lambda i, j: (0, 0, 0))],
        [jax.ShapeDtypeStruct((t, in_w), BF16), jax.ShapeDtypeStruct((t, d), BF16),
         jax.ShapeDtypeStruct((1, V_W), F32), jax.ShapeDtypeStruct((1, POOL_W), F32),
         jax.ShapeDtypeStruct((n_win, GC, GC), F32)],
        [pltpu.VMEM((QK_W, V_W), F32), pltpu.VMEM((HALO, POOL_W), F32)],
        (dx2, qs, k, v, g, o, pooled, st, mask, dq, dk, gbd, bd, gain, wp, scale, w_out, cos_t, sin_t), ride)


def _loss_head(x3, nf, tgt, name):
    t, d = x3.shape

    def body(x_ref, n_ref, t_ref, dx_ref, dn_ref, loss_ref, dxb_ref):
        @pl.when(pl.program_id(0) == 0)
        def _():
            dn_ref[...] = jnp.zeros_like(dn_ref)
            loss_ref[...] = jnp.zeros_like(loss_ref)

        xv = x_ref[...]
        nv = n_ref[...]
        r = lax.rsqrt(jnp.mean(xv * xv, axis=-1, keepdims=True) + RMS_EPS)
        xh = xv * r
        err = xh * nv - t_ref[...]
        row = jnp.mean(err * err, axis=-1, keepdims=True)
        loss_ref[...] += 0.5 * jnp.sum(row, axis=0, keepdims=True)
        dy = err * (1.0 / d)
        dn_ref[...] += jnp.sum(dy * xh, axis=0, keepdims=True)
        dxh = dy * nv
        dx = r * (dxh - xh * jnp.mean(dxh * xh, axis=-1, keepdims=True))
        dx_ref[...] = dx
        dxb_ref[...] = (0.5 * dx).astype(BF16)

    tile = pl.BlockSpec((TM, d), lambda i: (i, 0))
    return pl.pallas_call(
        body, name=name, grid=(t // TM,),
        in_specs=[tile, pl.BlockSpec((1, d), lambda i: (0, 0)), tile],
        out_specs=[tile, pl.BlockSpec((1, d), lambda i: (0, 0)), pl.BlockSpec((1, 1), lambda i: (0, 0)), tile],
        out_shape=[jax.ShapeDtypeStruct((t, d), F32), jax.ShapeDtypeStruct((1, d), F32), jax.ShapeDtypeStruct((1, 1), F32),
                   jax.ShapeDtypeStruct((t, d), BF16)],
        compiler_params=_cparams(1),
    )(x3, nf, tgt)


def _coords():
    return lax.axis_index("x"), lax.axis_index("y"), lax.axis_index("c")


def _window(ref, kind, idx, size):
    if kind == "col":
        return ref.at[:, pl.ds(pl.multiple_of(idx * size, LANE), size)]
    return ref.at[pl.ds(pl.multiple_of(idx * size, 8), size), :]


def _run_exchange(ex, name):
    n_in = len(ex.inputs)

    def body(*refs):
        ins, outs, sems = refs[:n_in], refs[n_in:n_in + len(ex.out_shape)], refs[n_in + len(ex.out_shape):]
        ex.start(ins, outs, sems)
        if ex.mid is not None:
            ex.mid(ins, outs, sems)
        ex.finish(ins, outs, sems)

    return pl.pallas_call(body, name=name, in_specs=[ANY] * n_in, out_specs=[ANY] * len(ex.out_shape),
                          out_shape=ex.out_shape, scratch_shapes=ex.scratch)(*ex.inputs)


def _join(exchanges):
    bounds = []
    i0 = o0 = s0 = 0
    for ex in exchanges:
        bounds.append((i0, o0, s0))
        i0, o0, s0 = i0 + len(ex.inputs), o0 + len(ex.out_shape), s0 + len(ex.scratch)

    def phase(which):
        def run(ins, outs, sems):
            for ex, (i, o, s) in zip(exchanges, bounds):
                fn = getattr(ex, which)
                if fn is not None:
                    fn(ins[i:i + len(ex.inputs)], outs[o:o + len(ex.out_shape)], sems[s:s + len(ex.scratch)])
        return run

    return _Exchange(sum((ex.inputs for ex in exchanges), []), sum((ex.out_shape for ex in exchanges), []),
                     sum((ex.scratch for ex in exchanges), []), phase("start"), phase("finish"),
                     phase("mid") if any(ex.mid is not None for ex in exchanges) else None)


def _gather_exchange(parts):
    n = len(parts)
    kinds = [kd for _, kd in parts]
    sizes = [a.shape[1] if kd == "col" else a.shape[0] for a, kd in parts]

    def plan(ins, outs, sems):
        send_sems, recv_sems, local_sems = sems
        x, y, c = _coords()
        me, sibling = (x, y, c), (x, y, 1 - c)
        chips = [(1 - x, y), (x, 1 - y), (1 - x, 1 - y)]

        def win(p, dev):
            return _window(outs[p], kinds[p], 4 * dev[0] + 2 * dev[1] + dev[2], sizes[p])

        def copy(p, k, block, to, src=None):
            return pltpu.make_async_remote_copy(
                src_ref=win(p, block) if src is None else src, dst_ref=win(p, block),
                send_sem=send_sems.at[p * 7 + k], recv_sem=recv_sems.at[p * 7 + k], device_id=to, device_id_type=MESH_ID)

        mine = [pltpu.make_async_copy(ins[p], win(p, me), local_sems.at[p]) for p in range(n)]
        first, arrived, passed, rest = [], [], [], []
        for p in range(n):
            first.append(copy(p, 0, me, sibling, src=ins[p]))
            first += [copy(p, 1 + q, me, (*chip, c), src=ins[p]) for q, chip in enumerate(chips)]
            rest.append(copy(p, 0, sibling, me))
            rest += [copy(p, 4 + q, (*chip, 1 - c), me) for q, chip in enumerate(chips)]
        for q, chip in enumerate(chips):
            for p in range(n):
                arrived.append(copy(p, 1 + q, (*chip, c), me))
                passed.append(copy(p, 4 + q, (*chip, c), sibling))
        return mine, first, arrived, passed, rest

    def start(ins, outs, sems):
        mine, first, _, _, _ = plan(ins, outs, sems)
        for cp in mine + first:
            cp.start()

    def mid(ins, outs, sems):
        _, _, arrived, passed, _ = plan(ins, outs, sems)
        for got, fwd in zip(arrived, passed):
            got.wait_recv()
            fwd.start()

    def finish(ins, outs, sems):
        mine, first, _, passed, rest = plan(ins, outs, sems)
        for cp in rest:
            cp.wait_recv()
        for cp in first + passed:
            cp.wait_send()
        for cp in mine:
            cp.wait()

    out_shape = [jax.ShapeDtypeStruct((a.shape[0], N_DEV * a.shape[1]) if kd == "col" else (N_DEV * a.shape[0], a.shape[1]),
                                      a.dtype) for a, kd in parts]
    scratch = [pltpu.SemaphoreType.DMA((7 * n,)), pltpu.SemaphoreType.DMA((7 * n,)), pltpu.SemaphoreType.DMA((n,))]
    return _Exchange([a for a, _ in parts], out_shape, scratch, start, finish, mid)


def _all_gather(parts, name):
    return _run_exchange(_gather_exchange(parts), name)


def _shard_shape(a, kd):
    return (a.shape[0], a.shape[1] // N_DEV) if kd == "col" else (a.shape[0] // N_DEV, a.shape[1])


def _symmetric_exchange(inputs, out_shape, n_copies, plan):
    def start(ins, outs, sems):
        for cp in plan(ins, outs, sems):
            cp.start()

    def finish(ins, outs, sems):
        copies = plan(ins, outs, sems)
        for cp in copies:
            cp.wait_recv()
        for cp in copies:
            cp.wait_send()

    scratch = [pltpu.SemaphoreType.DMA((n_copies,)), pltpu.SemaphoreType.DMA((n_copies,))]
    return _Exchange(inputs, out_shape, scratch, start, finish)


def _rs_pair_exchange(grads):
    n = len(grads)
    kinds = [kd for _, kd in grads]
    shapes = [_shard_shape(a, kd) for a, kd in grads]

    def plan(ins, outs, sems):
        send_sems, recv_sems = sems
        x, y, c = _coords()
        copies = []
        for p in range(n):
            size = shapes[p][1] if kinds[p] == "col" else shapes[p][0]
            for s in range(4):
                src = _window(ins[p], kinds[p], 2 * s + (1 - c), size)
                copies.append(pltpu.make_async_remote_copy(
                    src_ref=src, dst_ref=outs[p].at[s], send_sem=send_sems.at[4 * p + s], recv_sem=recv_sems.at[4 * p + s],
                    device_id=(x, y, 1 - c), device_id_type=MESH_ID))
        return copies

    return _symmetric_exchange([a for a, _ in grads], [jax.ShapeDtypeStruct((4,) + shapes[p], BF16) for p in range(n)],
                               4 * n, plan)


def _rs_chips_exchange(sums):
    n = len(sums)

    def plan(ins, outs, sems):
        send_sems, recv_sems = sems
        x, y, c = _coords()
        chips = [(1 - x, y), (x, 1 - y), (1 - x, 1 - y)]
        copies = []
        for p in range(n):
            for q, (cx, cy) in enumerate(chips):
                copies.append(pltpu.make_async_remote_copy(
                    src_ref=ins[p].at[2 * cx + cy], dst_ref=outs[p].at[q],
                    send_sem=send_sems.at[3 * p + q], recv_sem=recv_sems.at[3 * p + q],
                    device_id=(cx, cy, c), device_id_type=MESH_ID))
        return copies

    return _symmetric_exchange(list(sums), [jax.ShapeDtypeStruct((3,) + a.shape[1:], BF16) for a in sums], 3 * n, plan)


def _pair_sum(grad, kd, recv, core, name):
    _, r, cw = recv.shape
    tr = min(r, TM)

    def body(core_ref, g_ref, r_ref, o_ref):
        del core_ref
        o_ref[0] = (g_ref[...].astype(F32) + r_ref[0].astype(F32)).astype(BF16)

    if kd == "col":
        g_spec = pl.BlockSpec((tr, cw), lambda s, i, core_ref: (i, 2 * s + core_ref[0]))
    else:
        g_spec = pl.BlockSpec((tr, cw), lambda s, i, core_ref: ((2 * s + core_ref[0]) * (r // tr) + i, 0))
    grid_spec = pltpu.PrefetchScalarGridSpec(
        num_scalar_prefetch=1, grid=(4, r // tr),
        in_specs=[g_spec, pl.BlockSpec((1, tr, cw), lambda s, i, core_ref: (s, i, 0))],
        out_specs=pl.BlockSpec((1, tr, cw), lambda s, i, core_ref: (s, i, 0)))
    return pl.pallas_call(
        body, name=name, grid_spec=grid_spec, out_shape=jax.ShapeDtypeStruct(recv.shape, BF16),
        compiler_params=_cparams(2),
    )(core, grad, recv)


def _adam_math(w, g, m, v):
    m2 = B1 * m + (1.0 - B1) * g
    v2 = B2 * v + (1.0 - B2) * (g * g)
    m_hat = m2 / (1.0 - B1 ** STEP)
    v_hat = v2 / (1.0 - B2 ** STEP)
    delta = -LR * (m_hat / (jnp.sqrt(v_hat) + ADAM_EPS) + WD * w)
    return delta, m2, v2


def _chip_sum_adam(items, chip, tr, name, ride=None):
    r = items[0][1].shape[0]
    steps = r // tr
    n_parts = [len(parts) for parts, _, _, _ in items]
    r_in = 0 if ride is None else len(ride.inputs)
    r_out = 0 if ride is None else len(ride.out_shape)
    n_in = sum(2 * k + 3 for k in n_parts)
    n_out = 4 * len(items)

    def body(chip_ref, *refs):
        del chip_ref
        ins, refs = refs[:n_in], refs[n_in:]
        r_ins, refs = refs[:r_in], refs[r_in:]
        outs, refs = refs[:n_out], refs[n_out:]
        r_outs, sems = refs[:r_out], refs[r_out:]
        i = pl.program_id(0)
        if ride is not None:
            @pl.when(i == 0)
            def _():
                ride.start(r_ins, r_outs, sems)

        pos = 0
        for q, (k, (_, w, _, _)) in enumerate(zip(n_parts, items)):
            cols = []
            for _ in range(k):
                p_ref, c_ref = ins[pos], ins[pos + 1]
                pos += 2
                cols.append(p_ref[0].astype(F32) + c_ref[0].astype(F32) + c_ref[1].astype(F32) + c_ref[2].astype(F32))
            g = (cols[0] if k == 1 else jnp.concatenate(cols, axis=1))[:, :w.shape[1]]
            w_ref, m_ref, v_ref = ins[pos:pos + 3]
            pos += 3
            delta, m2, v2 = _adam_math(w_ref[...], g, m_ref[...], v_ref[...])
            outs[4 * q][...] = g
            outs[4 * q + 1][...] = delta
            outs[4 * q + 2][...] = m2
            outs[4 * q + 3][...] = v2

        if ride is not None:
            @pl.when(i == steps - 1)
            def _():
                ride.finish(r_ins, r_outs, sems)

    in_specs, args, out_specs, out_shape = [], [], [], []
    for parts, w, m, v in items:
        for psum, recv in parts:
            pc = psum.shape[2]
            in_specs += [pl.BlockSpec((1, tr, pc), lambda i, chip_ref: (chip_ref[0], i, 0)),
                         pl.BlockSpec((3, tr, pc), lambda i, chip_ref: (0, i, 0))]
            args += [psum, recv]
        loc = pl.BlockSpec((tr, w.shape[1]), lambda i, chip_ref: (i, 0))
        in_specs += [loc] * 3
        args += [w, m, v]
        out_specs += [loc] * 4
        out_shape += [jax.ShapeDtypeStruct(w.shape, F32)] * 4
    grid_spec = pltpu.PrefetchScalarGridSpec(
        num_scalar_prefetch=1, grid=(steps,), in_specs=in_specs + [ANY] * r_in, out_specs=out_specs + [ANY] * r_out,
        scratch_shapes=[] if ride is None else ride.scratch)
    res = pl.pallas_call(
        body, name=name, grid_spec=grid_spec, out_shape=out_shape + ([] if ride is None else ride.out_shape),
        compiler_params=_cparams(1),
    )(chip, *args, *([] if ride is None else ride.inputs))
    return res if ride is None else (res[:n_out], res[n_out:])


def _small_allreduce_adam(partials, params, moms, vels, plain, name, ride=None):
    n, n_plain = len(partials), len(plain)
    summed = list(partials) + list(plain)
    row0 = []
    rows = 0
    for a in summed:
        if a.shape[0] >= 8:
            rows = _pad_to(rows, 8)
        row0.append(rows)
        rows += a.shape[0]
    rows = _pad_to(rows, 8)
    width = max(a.shape[1] for a in summed)
    r_in = 0 if ride is None else len(ride.inputs)
    r_out = 0 if ride is None else len(ride.out_shape)
    n_out = 4 * n + n_plain

    def body(*refs):
        w_in, m_in, v_in = refs[0:n], refs[n:2 * n], refs[2 * n:3 * n]
        g_in, refs = refs[3 * n:4 * n + n_plain], refs[4 * n + n_plain:]
        r_ins, refs = refs[:r_in], refs[r_in:]
        outs, refs = refs[:n_out], refs[n_out:]
        r_outs, refs = refs[:r_out], refs[r_out:]
        pair, chips, send_sems, recv_sems = refs[:4]
        if ride is not None:
            ride.start(r_ins, r_outs, refs[4:])
        x, y, c = _coords()
        chip = 2 * x + y
        pair[c] = jnp.zeros((rows, width), F32)
        for p, a in enumerate(summed):
            r, cw = a.shape
            pair[c, row0[p]:row0[p] + r, 0:cw] = g_in[p][...]
        swap = pltpu.make_async_remote_copy(src_ref=pair.at[c], dst_ref=pair.at[c], send_sem=send_sems.at[0],
                                            recv_sem=recv_sems.at[0], device_id=(x, y, 1 - c), device_id_type=MESH_ID)
        swap.start()
        swap.wait_recv()
        swap.wait_send()
        chips[chip] = pair[0] + pair[1]
        copies = [pltpu.make_async_remote_copy(
            src_ref=chips.at[chip], dst_ref=chips.at[chip], send_sem=send_sems.at[1 + q], recv_sem=recv_sems.at[1 + q],
            device_id=(cx, cy, c), device_id_type=MESH_ID) for q, (cx, cy) in enumerate([(1 - x, y), (x, 1 - y), (1 - x, 1 - y)])]
        for cp in copies:
            cp.start()
        for cp in copies:
            cp.wait_recv()
        for cp in copies:
            cp.wait_send()
        for p, a in enumerate(summed):
            r, cw = a.shape
            g = chips[0, row0[p]:row0[p] + r, 0:cw]
            for q in range(1, 4):
                g = g + chips[q, row0[p]:row0[p] + r, 0:cw]
            if p >= n:
                outs[4 * n + p - n][...] = g
                continue
            delta, m2, v2 = _adam_math(w_in[p][...], g, m_in[p][...], v_in[p][...])
            outs[4 * p][...] = g
            outs[4 * p + 1][...] = delta
            outs[4 * p + 2][...] = m2
            outs[4 * p + 3][...] = v2
        if ride is not None:
            ride.finish(r_ins, r_outs, refs[4:])

    out_shape = []
    for a in partials:
        out_shape += [jax.ShapeDtypeStruct(a.shape, F32)] * 4
    out_shape += [jax.ShapeDtypeStruct(a.shape, F32) for a in plain]
    res = pl.pallas_call(
        body, name=name, in_specs=[VMEM_SPEC] * (4 * n + n_plain) + [ANY] * r_in,
        out_specs=[VMEM_SPEC] * n_out + [ANY] * r_out, out_shape=out_shape + ([] if ride is None else ride.out_shape),
        scratch_shapes=[pltpu.VMEM((2, rows, width), F32), pltpu.VMEM((4, rows, width), F32),
                        pltpu.SemaphoreType.DMA((4,)), pltpu.SemaphoreType.DMA((4,))] + ([] if ride is None else ride.scratch),
    )(*params, *moms, *vels, *partials, *plain, *([] if ride is None else ride.inputs))
    return res if ride is None else (res[:n_out], res[n_out:])


def _local_step(xf, tgt, nseq, seq, cols1_all, later, small_w, core, small_step):
    d = xf.shape[1]
    n1, n2, gain, pool_w, pool_scale, n3, nf = small_w
    tf = 2 * cols1_all.shape[1] // N_DEV
    consts = _retention_constants()
    cos_t, sin_t = _rotary_tables(seq)
    wp_b = pool_w.astype(BF16)

    def pair_sums(grads, recv, names):
        return [_pair_sum(g, kd, r, core, "pair_sum_" + nm) for (g, kd), r, nm in zip(grads, recv, names)]

    def riding(host):
        return _gather_exchange(later[host])

    both = lambda first, second: _join([_rs_chips_exchange(first), _rs_pair_exchange([second])])

    (h1, b1, sil1, dsil1, s1), (d1_all, win_all, wout_all, gate2_all, up2_all) = _ffn_act(
        xf, n1, (cols1_all, 0), (cols1_all, 1), "ffn1_act", ride=riding("ffn1_act"))
    x1, (d2_all,) = _ffn_down(s1, xf, d1_all, "ffn1_down", ride=riding("ffn1_down"))
    h2, qs, kr, vv, gg, uu = _mix_in(x1, n2, win_all, cos_t, sin_t, seq, "mix_in")
    x2, mix, oo, pooled, states = _mix_core_fwd(qs, kr, vv, gg, uu, x1, consts, gain, wp_b, pool_scale, wout_all,
                                                 nseq, seq, "mix_core_fwd")
    (x3, h3, b3, sil3, dsil3), (cols2_t, d2_t, win_t, wout_t) = _ffn_fwd(
        x2, n3, (gate2_all, 0), (up2_all, 0), d2_all, "ffn2_fwd", ride=riding("ffn2_fwd"))
    dx3, dnf, loss_part, dx3b = _loss_head(x3, nf, tgt, "loss_head")
    out = {}

    (da3, db3, g_wd2), (cols1_t, d1_t) = _ffn_bwd_act(dx3b, b3, sil3, dsil3, d2_t, "ffn2_bwd_act",
                                                      ride=riding("ffn2_bwd_act"))
    names2 = ["ffn2_gate", "ffn2_up", "ffn2_down"]
    grads2 = [(_wgrad(da3, h3, 1.0, tf, d, "wgrad_gate2"), "row"), (_wgrad(db3, h3, 1.0, tf, d, "wgrad_up2"), "row"),
              (g_wd2, "row")]
    (dx2, dn3), recv2 = _ffn_bwd_in(da3, db3, dx3, x2, n3, cols2_t, 0, "ffn2_bwd_in", ride=_rs_pair_exchange(grads2))
    sums2 = pair_sums(grads2, recv2, names2)
    (dp, dx2b, dgain, dscale, dwp), crecv2 = _mix_core_bwd(
        dx2, qs, kr, vv, gg, oo, pooled, states, consts, gain, wp_b, pool_scale, wout_t, cos_t, sin_t, nseq, seq,
        "mix_core_bwd", ride=_rs_chips_exchange(sums2))
    out.update({nm: [(s, r)] for nm, s, r in zip(names2, sums2, crecv2)})

    names_m = ["w_in", "w_out"]
    grads_m = [(_wgrad(h2, dp, 1.0, d, d, "wgrad_in"), "col"), (_wgrad(mix, dx2b, 1.0, d, d, "wgrad_out"), "row")]
    (dx1, dn2, dx1b), recv_m = _mix_in_bwd(dp, dx2, x1, n2, win_t, "mix_in_bwd", ride=_rs_pair_exchange(grads_m))
    sums_m = pair_sums(grads_m, recv_m, names_m)
    (da1, db1, g_wd1), crecv_m = _ffn_bwd_act(dx1b, b1, sil1, dsil1, d1_t, "ffn1_bwd_act", ride=_rs_chips_exchange(sums_m))
    out.update({nm: [(s, r)] for nm, s, r in zip(names_m, sums_m, crecv_m)})

    dx0, dn1 = _ffn_bwd_in(da1, db1, dx1, xf, n1, cols1_t, 0, "ffn1_bwd_in")
    g_down = (g_wd1, "row")
    g_gate, recv_d = _wgrad(da1, h1, 1.0, tf, d, "wgrad_gate1", ride=_rs_pair_exchange([g_down]))
    g_gate = (g_gate, "row")
    sum_d = pair_sums([g_down], recv_d, ["ffn1_down"])
    g_lo, (crecv_d, recv_g) = _wgrad(db1, h1, 1.0, tf, d // 2, "wgrad_up1_lo", ride=both(sum_d, g_gate), b_cols=(0, 1))
    g_lo = (g_lo, "row")
    sum_g = pair_sums([g_gate], [recv_g], ["ffn1_gate"])
    g_hi, (crecv_g, recv_lo) = _wgrad(db1, h1, 1.0, tf, d // 2, "wgrad_up1_hi", ride=both(sum_g, g_lo), b_cols=(1, 1))
    g_hi = (g_hi, "row")
    sum_lo = pair_sums([g_lo], [recv_lo], ["ffn1_up_lo"])
    small_out, (crecv_lo, recv_hi) = small_step((dn1, dn2, dgain, dwp, dscale, dn3, dnf), loss_part, both(sum_lo, g_hi))
    sum_hi = pair_sums([g_hi], [recv_hi], ["ffn1_up_hi"])
    out.update({"ffn1_gate": [(sum_g[0], crecv_g)], "ffn1_down": [(sum_d[0], crecv_d)],
                "ffn1_up": [(sum_lo[0], crecv_lo), (sum_hi[0], None)]})
    return small_out[-1], dx0, out, small_out[:-1], _rs_chips_exchange(sum_hi)


def kernel(x, norm_ffn1, ffn1_gate, ffn1_up, ffn1_down, norm_mix, w_in, ret_gn_gain, pool_w, pool_scale, w_out, norm_ffn2, ffn2_gate, ffn2_up, ffn2_down, norm_final, loss_target, m_norm_ffn1, m_ffn1_gate, m_ffn1_up, m_ffn1_down, m_norm_mix, m_w_in, m_ret_gn_gain, m_pool_w, m_pool_scale, m_w_out, m_norm_ffn2, m_ffn2_gate, m_ffn2_up, m_ffn2_down, m_norm_final, v_norm_ffn1, v_ffn1_gate, v_ffn1_up, v_ffn1_down, v_norm_mix, v_w_in, v_ret_gn_gain, v_pool_w, v_pool_scale, v_w_out, v_norm_ffn2, v_ffn2_gate, v_ffn2_up, v_ffn2_down, v_norm_final):
    nseq, seq, d = x.shape
    t = nseq * seq
    f_loc = ffn1_gate.shape[2]
    f_pad = _pad_to(f_loc, LANE)
    xf = x.reshape(t, d)
    tgt = loss_target.reshape(t, d)
    core = lax.axis_index("c").astype(jnp.int32).reshape(1)
    chip = (2 * lax.axis_index("x") + lax.axis_index("y")).astype(jnp.int32).reshape(1)

    colp = lambda w: jnp.pad(w[0].astype(BF16), ((0, 0), (0, f_pad - f_loc)))
    rowp = lambda w: jnp.pad(w[0].astype(BF16), ((0, f_pad - f_loc), (0, 0)))
    gate2, up2 = colp(ffn2_gate), colp(ffn2_up)
    cols1 = jnp.concatenate([colp(ffn1_gate), colp(ffn1_up)], axis=0)
    cols2_t = jnp.concatenate([gate2.T, up2.T], axis=1)
    (cols1_all,) = _all_gather([(cols1, "col")], "all_gather_ffn1")
    d1_loc, d2_loc, win_loc, wout_loc = rowp(ffn1_down), rowp(ffn2_down), w_in[0].astype(BF16), w_out[0].astype(BF16)
    later = {"ffn1_act": [(d1_loc, "row"), (win_loc, "col"), (wout_loc, "row"), (gate2, "col"), (up2, "col")],
             "ffn1_down": [(d2_loc, "row")],
             "ffn2_fwd": [(cols2_t, "row"), (d2_loc.T, "col"), (win_loc.T, "row"), (wout_loc.T, "col")],
             "ffn2_bwd_act": [(cols1.T, "row"), (d1_loc.T, "col")]}

    flat = lambda a: a.reshape(pool_w.size // d, d)
    params = [norm_ffn1, norm_mix, ret_gn_gain, flat(pool_w), pool_scale, norm_ffn2, norm_final.reshape(1, d)]
    moms = [m_norm_ffn1, m_norm_mix, m_ret_gn_gain, flat(m_pool_w), m_pool_scale, m_norm_ffn2, m_norm_final.reshape(1, d)]
    vels = [v_norm_ffn1, v_norm_mix, v_ret_gn_gain, flat(v_pool_w), v_pool_scale, v_norm_ffn2, v_norm_final.reshape(1, d)]

    def small_step(parts, loss_part, ride):
        dn1, dn2, dgain, dwp, dscale, dn3, dnf = parts
        return _small_allreduce_adam([dn1, dn2, dgain, flat(dwp), dscale, dn3, dnf], params, moms, vels, [loss_part],
                                     "small_allreduce_adam", ride)

    small_w = (norm_ffn1, norm_mix, ret_gn_gain, pool_w[0], pool_scale, norm_ffn2, norm_final.reshape(1, d))
    loss_sum, dx0, reduced, small_out, pending = _local_step(xf, tgt, nseq, seq, cols1_all, later, small_w, core,
                                                             small_step)

    local = {"ffn1_gate": (ffn1_gate, m_ffn1_gate, v_ffn1_gate), "ffn1_up": (ffn1_up, m_ffn1_up, v_ffn1_up),
             "ffn1_down": (ffn1_down, m_ffn1_down, v_ffn1_down), "w_in": (w_in, m_w_in, v_w_in),
             "w_out": (w_out, m_w_out, v_w_out), "ffn2_gate": (ffn2_gate, m_ffn2_gate, v_ffn2_gate),
             "ffn2_up": (ffn2_up, m_ffn2_up, v_ffn2_up), "ffn2_down": (ffn2_down, m_ffn2_down, v_ffn2_down)}
    flip = lambda nm: nm.endswith("gate") or nm.endswith("up")

    def item(nm):
        view = (lambda a: a[0].T) if flip(nm) else (lambda a: a[0])
        w, m, v = local[nm]
        return reduced[nm], view(w), view(m), view(v)

    big = {}

    def keep(names, res):
        for q, nm in enumerate(names):
            big[nm] = tuple((a.T if flip(nm) else a)[None] for a in res[4 * q:4 * q + 4])

    second = ["ffn2_gate", "ffn2_up", "ffn2_down"]
    res, (last_recv,) = _chip_sum_adam([item(nm) for nm in second], chip, item(second[0])[1].shape[0] // 2, "adam_ffn2",
                                       ride=pending)
    keep(second, res)
    reduced["ffn1_up"][-1] = (reduced["ffn1_up"][-1][0], last_recv)
    first = ["ffn1_gate", "ffn1_up", "ffn1_down"]
    keep(first, _chip_sum_adam([item(nm) for nm in first], chip, item(first[0])[1].shape[0] // 2, "adam_ffn1"))
    for nm in ["w_in", "w_out"]:
        keep([nm], _chip_sum_adam([item(nm)], chip, min(item(nm)[1].shape[0], TM), "adam_" + nm))

    small_names = ["norm_ffn1", "norm_mix", "ret_gn_gain", "pool_w", "pool_scale", "norm_ffn2", "norm_final"]
    shapes = [norm_ffn1.shape, norm_mix.shape, ret_gn_gain.shape, pool_w.shape, pool_scale.shape, norm_ffn2.shape,
              norm_final.shape]
    small = {nm: tuple(small_out[4 * p + q].reshape(shapes[p]) for q in range(4)) for p, nm in enumerate(small_names)}

    loss = loss_sum[0, 0]
    order = ["norm_ffn1", "ffn1_gate", "ffn1_up", "ffn1_down", "norm_mix", "w_in", "ret_gn_gain", "pool_w", "pool_scale",
             "w_out", "norm_ffn2", "ffn2_gate", "ffn2_up", "ffn2_down", "norm_final"]
    both = {**big, **small}
    outs = [loss, dx0.reshape(nseq, seq, d)]
    for q in range(4):
        outs += [both[nm][q] for nm in order]
    return tuple(outs)
```

```python
import numpy as np
import jax
import jax.numpy as jnp
from jax import lax
from jax.experimental import pallas as pl
from jax.experimental.pallas import tpu as pltpu

F32, BF16 = jnp.float32, jnp.bfloat16
MESH_ID = pl.DeviceIdType.MESH
ANY = pl.BlockSpec(memory_space=pl.ANY)
VMEM_SPEC = pl.BlockSpec(memory_space=pltpu.VMEM)

N_DEV = 8
RMS_EPS = 1e-6
GN_EPS = 1e-5
HEADS, DK, DV = 4, 64, 128
QK_W, V_W, POOL_W = HEADS * DK, HEADS * DV, 512
WINDOWS = (2, 4, 8, 16)
GC = POOL_W // len(WINDOWS)
CHUNK = 64
BLK = 4 * CHUNK
MIX_BLOCKS = 4
HALO = 16
ROPE_BASE = 10000.0
LR, B1, B2, ADAM_EPS, WD, STEP = 0.001, 0.9, 0.999, 1e-08, 0.01, 10
LANE = 128
TM = 512
FFN_TM = 1024
FFN_FWD_TF = 768
WGRAD_TT = 4096
VMEM_LIMIT = 56 * 1024 * 1024


def _cparams(n_axes):
    return pltpu.CompilerParams(dimension_semantics=("arbitrary",) * n_axes, vmem_limit_bytes=VMEM_LIMIT)


class _Exchange:
    def __init__(self, inputs, out_shape, scratch, start, finish, mid=None):
        self.inputs, self.out_shape, self.scratch = list(inputs), list(out_shape), list(scratch)
        self.start, self.finish, self.mid = start, finish, mid


def _pallas(body, name, grid, in_specs, out_specs, out_shape, scratch_shapes, args, ride=None):
    n_axes = len(grid)
    if ride is None:
        return pl.pallas_call(body, name=name, grid=grid, in_specs=in_specs, out_specs=out_specs, out_shape=out_shape,
                              scratch_shapes=scratch_shapes, compiler_params=_cparams(n_axes))(*args)
    n_in, n_out, n_scr = len(in_specs), len(out_specs), len(scratch_shapes)
    r_in, r_out = len(ride.inputs), len(ride.out_shape)

    def hosted(*refs):
        ins, refs = refs[:n_in], refs[n_in:]
        r_ins, refs = refs[:r_in], refs[r_in:]
        outs, refs = refs[:n_out], refs[n_out:]
        r_outs, refs = refs[:r_out], refs[r_out:]
        scr, sems = refs[:n_scr], refs[n_scr:]
        ids = [pl.program_id(a) for a in range(n_axes)]
        first, last, inner0 = ids[0] == 0, ids[0] == grid[0] - 1, None
        for a in range(1, n_axes):
            first = first & (ids[a] == 0)
            last = last & (ids[a] == grid[a] - 1)
            inner0 = (ids[a] == 0) if inner0 is None else inner0 & (ids[a] == 0)

        @pl.when(first)
        def _():
            ride.start(r_ins, r_outs, sems)

        if ride.mid is not None:
            at_mid = ids[0] == grid[0] - 1
            if inner0 is not None:
                at_mid = at_mid & inner0

            @pl.when(at_mid)
            def _():
                ride.mid(r_ins, r_outs, sems)

        body(*ins, *outs, *scr)

        @pl.when(last)
        def _():
            ride.finish(r_ins, r_outs, sems)

    res = pl.pallas_call(
        hosted, name=name, grid=grid, in_specs=list(in_specs) + [ANY] * r_in, out_specs=list(out_specs) + [ANY] * r_out,
        out_shape=list(out_shape) + ride.out_shape, scratch_shapes=list(scratch_shapes) + ride.scratch,
        compiler_params=_cparams(n_axes))(*args, *ride.inputs)
    return res[:n_out], res[n_out:]


def _dot(a, b):
    return jnp.dot(a, b, preferred_element_type=F32)


def _dot_nt(a, b):
    return lax.dot_general(a, b, (((1,), (1,)), ((), ())), preferred_element_type=F32)


def _dot_tn(a, b):
    return lax.dot_general(a, b, (((0,), (0,)), ((), ())), preferred_element_type=F32)


def _sigmoid(x):
    return 0.5 * jnp.tanh(0.5 * x) + 0.5


def _pad_to(n, m):
    return (n + m - 1) // m * m


def _retention_constants():
    gamma = (1.0 - 2.0 ** (-5.0 - np.arange(HEADS, dtype=np.float32))).astype(np.float32)
    log_g = np.log(gamma).astype(np.float32)
    i = np.arange(BLK)
    diff = (i[:, None] - i[None, :]).astype(np.float32)
    same = (i[:, None] // CHUNK) == (i[None, :] // CHUNK)
    earlier = (i[None, :] // CHUNK) < (i[:, None] // CHUNK)
    mask = np.zeros((HEADS, BLK, BLK), np.float32)
    for h in range(HEADS):
        dec_abs = np.exp(log_g[h] * np.abs(diff)).astype(np.float32)
        dec = np.exp(log_g[h] * diff * earlier).astype(np.float32)
        mask[h] = np.where(same, dec_abs, np.where(earlier, dec, 0.0))
    dq = np.zeros((BLK, V_W), np.float32)
    dk = np.zeros((BLK, QK_W), np.float32)
    gbd = np.zeros((QK_W, V_W), np.float32)
    for h in range(HEADS):
        dq[:, h * DV:(h + 1) * DV] = np.exp(log_g[h] * (i + 1.0)).astype(np.float32)[:, None]
        dk[:, h * DK:(h + 1) * DK] = np.exp(log_g[h] * (BLK - 1.0 - i)).astype(np.float32)[:, None]
        gbd[h * DK:(h + 1) * DK, h * DV:(h + 1) * DV] = np.exp(log_g[h] * np.float32(BLK))
    bd = (gbd > 0).astype(np.float32)
    return jnp.asarray(mask), jnp.asarray(dq), jnp.asarray(dk), jnp.asarray(gbd), jnp.asarray(bd)


def _rotary_tables(seq):
    half = DK // 2
    freqs = ROPE_BASE ** (-jnp.arange(half, dtype=F32) * 2.0 / DK)
    ang = jnp.arange(seq, dtype=F32)[:, None] * freqs[None, :]
    cos, sin = jnp.cos(ang), jnp.sin(ang)
    cos_t = jnp.tile(jnp.concatenate([cos, cos], axis=1), (1, HEADS))
    sin_t = jnp.tile(jnp.concatenate([-sin, sin], axis=1), (1, HEADS))
    return cos_t, sin_t


def _swap_halves(x):
    lane = lax.broadcasted_iota(jnp.int32, (1, QK_W), 1)
    first = (lane & (DK - 1)) < DK // 2
    return jnp.where(first, pltpu.roll(x, QK_W - DK // 2, 1), pltpu.roll(x, DK // 2, 1))


def _head_mask(h):
    lane = lax.broadcasted_iota(jnp.int32, (1, QK_W), 1)
    return (lane >= h * DK) & (lane < (h + 1) * DK)


def _ffn_fwd(x, n, gate, up, wd, name, ride=None):
    t, d = x.shape
    (wg, gq), (wu, uq) = gate, up
    fp = wg.shape[1]
    tm = min(t, FFN_TM)
    tf = FFN_FWD_TF
    nj = fp // tf

    def body(x_ref, n_ref, wg_ref, wu_ref, wd_ref, xo_ref, h_ref, b_ref, sil_ref, dsil_ref, acc_ref):
        j = pl.program_id(1)

        @pl.when(j == 0)
        def _():
            xv = x_ref[...]
            r = lax.rsqrt(jnp.mean(xv * xv, axis=-1, keepdims=True) + RMS_EPS)
            h_ref[...] = (xv * r * n_ref[...]).astype(BF16)
            acc_ref[...] = jnp.zeros_like(acc_ref)

        h = h_ref[...]
        a = _dot(h, wg_ref[...])
        b = _dot(h, wu_ref[...])
        sg = _sigmoid(a)
        sil = a * sg
        b_ref[...] = b.astype(BF16)
        sil_ref[...] = sil.astype(BF16)
        dsil_ref[...] = (sg + sil * (1.0 - sg)).astype(BF16)
        acc_ref[...] += _dot((sil * b).astype(BF16), wd_ref[...])

        @pl.when(j == nj - 1)
        def _():
            xo_ref[...] = x_ref[...] + 0.5 * acc_ref[...]

    act = pl.BlockSpec((tm, tf), lambda i, j: (i, j))
    return _pallas(
        body, name, (t // tm, nj),
        [pl.BlockSpec((tm, d), lambda i, j: (i, 0)), pl.BlockSpec((1, d), lambda i, j: (0, 0)),
         pl.BlockSpec((d, tf), lambda i, j: (gq, j)), pl.BlockSpec((d, tf), lambda i, j: (uq, j)),
         pl.BlockSpec((tf, d), lambda i, j: (j, 0))],
        [pl.BlockSpec((tm, d), lambda i, j: (i, 0)), pl.BlockSpec((tm, d), lambda i, j: (i, 0)), act, act, act],
        [jax.ShapeDtypeStruct((t, d), F32), jax.ShapeDtypeStruct((t, d), BF16)] + [jax.ShapeDtypeStruct((t, fp), BF16)] * 3,
        [pltpu.VMEM((tm, d), F32)], (x, n, wg, wu, wd), ride)


def _norm(x, n, name, ride=None):
    t, d = x.shape

    def body(x_ref, n_ref, h_ref):
        xv = x_ref[...]
        r = lax.rsqrt(jnp.mean(xv * xv, axis=-1, keepdims=True) + RMS_EPS)
        h_ref[...] = (xv * r * n_ref[...]).astype(BF16)

    tile = pl.BlockSpec((TM, d), lambda i: (i, 0))
    return _pallas(body, name, (t // TM,), [tile, pl.BlockSpec((1, d), lambda i: (0, 0))], [tile],
                   [jax.ShapeDtypeStruct((t, d), BF16)], [], (x, n), ride)


def _ffn_act(h, gate, up, name, ride=None):
    t, d = h.shape
    (wg, gq), (wu, uq) = gate, up
    fp = wg.shape[1]
    tm = min(t, FFN_TM)
    tf = 2 * fp // N_DEV
    nj = fp // tf

    def body(h_ref, wg_ref, wu_ref, b_ref, sil_ref, dsil_ref, s_ref):
        h = h_ref[...]
        a = _dot(h, wg_ref[...])
        b = _dot(h, wu_ref[...])
        sg = _sigmoid(a)
        sil = a * sg
        b_ref[...] = b.astype(BF16)
        sil_ref[...] = sil.astype(BF16)
        dsil_ref[...] = (sg + sil * (1.0 - sg)).astype(BF16)
        s_ref[...] = (sil * b).astype(BF16)

    act = pl.BlockSpec((tm, tf), lambda i, j: (i, j))
    return _pallas(
        body, name, (t // tm, nj),
        [pl.BlockSpec((tm, d), lambda i, j: (i, 0)),
         pl.BlockSpec((d, tf), lambda i, j: (gq, j)), pl.BlockSpec((d, tf), lambda i, j: (uq, j))],
        [act, act, act, act], [jax.ShapeDtypeStruct((t, fp), BF16)] * 4, [], (h, wg, wu), ride)


def _ffn_down(s, x, wd, name, ride=None):
    t, d = x.shape
    fp = wd.shape[0]
    tm = min(t, FFN_TM)

    def body(s_ref, x_ref, wd_ref, xo_ref):
        xo_ref[...] = x_ref[...] + 0.5 * _dot(s_ref[...], wd_ref[...])

    row = pl.BlockSpec((tm, d), lambda i: (i, 0))
    res = _pallas(body, name, (t // tm,), [pl.BlockSpec((tm, fp), lambda i: (i, 0)), row, pl.BlockSpec((fp, d), lambda i: (0, 0))],
                  [row], [jax.ShapeDtypeStruct((t, d), F32)], [], (s, x, wd), ride)
    return res[0] if ride is None else (res[0][0], res[1])


def _ffn_bwd_act(dxob, b, sil, dsil, wd_t, name, ride=None):
    t, d = dxob.shape
    fp = wd_t.shape[1]
    tm = min(t, FFN_TM)
    tf = 2 * fp // N_DEV
    ni = t // tm

    def body(dx_ref, b_ref, sil_ref, dsil_ref, wd_ref, da_ref, db_ref, gd_ref, acc_ref):
        i = pl.program_id(1)

        @pl.when(i == 0)
        def _():
            acc_ref[...] = jnp.zeros_like(acc_ref)

        dxv = dx_ref[...]
        bv, sv = b_ref[...].astype(F32), sil_ref[...].astype(F32)
        ds = _dot(dxv, wd_ref[...])
        da_ref[...] = (ds * bv * dsil_ref[...].astype(F32)).astype(BF16)
        db_ref[...] = (ds * sv).astype(BF16)
        acc_ref[...] += _dot_tn((sv * bv).astype(BF16), dxv)

        @pl.when(i == ni - 1)
        def _():
            gd_ref[...] = acc_ref[...].astype(BF16)

    act = pl.BlockSpec((tm, tf), lambda c, i: (i, c))
    return _pallas(
        body, name, (fp // tf, ni),
        [pl.BlockSpec((tm, d), lambda c, i: (i, 0)), act, act, act, pl.BlockSpec((d, tf), lambda c, i: (0, c))],
        [act, act, pl.BlockSpec((tf, d), lambda c, i: (c, 0))],
        [jax.ShapeDtypeStruct((t, fp), BF16), jax.ShapeDtypeStruct((t, fp), BF16), jax.ShapeDtypeStruct((fp, d), BF16)],
        [pltpu.VMEM((tf, d), F32)], (dxob, b, sil, dsil, wd_t), ride)


def _ffn_bwd_in(da, db, dxo, x, n, cols_t, gq, name, ride=None):
    t, d = x.shape
    fp = cols_t.shape[0]
    tm = min(t, FFN_TM)
    tf = fp // 3
    nj = fp // tf

    def body(da_ref, db_ref, dxo_ref, x_ref, n_ref, wg_ref, wu_ref, dx_ref, dn_ref, acc_ref):
        i, j = pl.program_id(0), pl.program_id(1)

        @pl.when((i == 0) & (j == 0))
        def _():
            dn_ref[...] = jnp.zeros_like(dn_ref)

        @pl.when(j == 0)
        def _():
            acc_ref[...] = jnp.zeros_like(acc_ref)

        acc_ref[...] += _dot(da_ref[...], wg_ref[...]) + _dot(db_ref[...], wu_ref[...])

        @pl.when(j == nj - 1)
        def _():
            xv = x_ref[...]
            r = lax.rsqrt(jnp.mean(xv * xv, axis=-1, keepdims=True) + RMS_EPS)
            xh = xv * r
            dh = acc_ref[...]
            dn_ref[...] += jnp.sum(dh * xh, axis=0, keepdims=True)
            dhn = dh * n_ref[...]
            dx_ref[...] = dxo_ref[...] + r * (dhn - xh * jnp.mean(dhn * xh, axis=-1, keepdims=True))

    act = pl.BlockSpec((tm, tf), lambda i, j: (i, j))
    row = pl.BlockSpec((tm, d), lambda i, j: (i, 0))
    return _pallas(
        body, name, (t // tm, nj),
        [act, act, row, row, pl.BlockSpec((1, d), lambda i, j: (0, 0)),
         pl.BlockSpec((tf, d), lambda i, j: (j, gq)), pl.BlockSpec((tf, d), lambda i, j: (j, gq + 1))],
        [row, pl.BlockSpec((1, d), lambda i, j: (0, 0))],
        [jax.ShapeDtypeStruct((t, d), F32), jax.ShapeDtypeStruct((1, d), F32)],
        [pltpu.VMEM((tm, d), F32)], (da, db, dxo, x, n, cols_t, cols_t), ride)


def _wgrad(a, b, scale, tk, tn, name, ride=None, b_cols=None):
    t, k = a.shape
    q0, nq = (0, b.shape[1] // tn) if b_cols is None else b_cols
    n = nq * tn
    tt = min(t, WGRAD_TT)
    nt = t // tt

    def body(a_ref, b_ref, o_ref, acc_ref):
        s = pl.program_id(2)

        @pl.when(s == 0)
        def _():
            acc_ref[...] = jnp.zeros_like(acc_ref)

        acc_ref[...] += _dot_tn(a_ref[...], b_ref[...])

        @pl.when(s == nt - 1)
        def _():
            o_ref[...] = (scale * acc_ref[...]).astype(BF16)

    res = _pallas(
        body, name, (k // tk, n // tn, nt),
        [pl.BlockSpec((tt, tk), lambda p, q, s: (s, p)), pl.BlockSpec((tt, tn), lambda p, q, s: (s, q + q0))],
        [pl.BlockSpec((tk, tn), lambda p, q, s: (p, q))], [jax.ShapeDtypeStruct((k, n), BF16)],
        [pltpu.VMEM((tk, tn), F32)], (a, b), ride)
    return res[0] if ride is None else (res[0][0], res[1])


def _mix_in(x, n, w_in, cos_t, sin_t, seq, name):
    t, d = x.shape
    per_seq = seq // TM

    def body(x_ref, n_ref, w_ref, c_ref, s_ref, h_ref, q_ref, k_ref, v_ref, g_ref, u_ref):
        xv = x_ref[...]
        r = lax.rsqrt(jnp.mean(xv * xv, axis=-1, keepdims=True) + RMS_EPS)
        h = (xv * r * n_ref[...]).astype(BF16)
        h_ref[...] = h
        p = _dot(h, w_ref[...])
        c, s = c_ref[...], s_ref[...]
        q = p[:, :QK_W]
        k = p[:, QK_W:2 * QK_W]
        q_ref[...] = ((q * c + _swap_halves(q) * s) * (DK ** -0.5)).astype(BF16)
        k_ref[...] = (k * c + _swap_halves(k) * s).astype(BF16)
        v_ref[...] = p[:, 2 * QK_W:2 * QK_W + V_W].astype(BF16)
        g_ref[...] = p[:, 2 * QK_W + V_W:2 * QK_W + 2 * V_W]
        u_ref[...] = p[:, 2 * QK_W + 2 * V_W:]

    tile = lambda w: pl.BlockSpec((TM, w), lambda i: (i, 0))
    return pl.pallas_call(
        body, name=name, grid=(t // TM,),
        in_specs=[tile(d), pl.BlockSpec((1, d), lambda i: (0, 0)), pl.BlockSpec(w_in.shape, lambda i: (0, 0)),
                  pl.BlockSpec((TM, QK_W), lambda i: (i % per_seq, 0)), pl.BlockSpec((TM, QK_W), lambda i: (i % per_seq, 0))],
        out_specs=[tile(d), tile(QK_W), tile(QK_W), tile(V_W), tile(V_W), tile(POOL_W)],
        out_shape=[jax.ShapeDtypeStruct((t, d), BF16), jax.ShapeDtypeStruct((t, QK_W), BF16),
                   jax.ShapeDtypeStruct((t, QK_W), BF16), jax.ShapeDtypeStruct((t, V_W), BF16),
                   jax.ShapeDtypeStruct((t, V_W), F32), jax.ShapeDtypeStruct((t, POOL_W), F32)],
        compiler_params=_cparams(1),
    )(x, n, w_in, cos_t, sin_t)


def _mix_in_bwd(dp, dx2, x1, n, w_in_t, name, ride=None):
    t, d = x1.shape

    def body(dp_ref, dx2_ref, x_ref, n_ref, w_ref, dx_ref, dn_ref, dxb_ref):
        @pl.when(pl.program_id(0) == 0)
        def _():
            dn_ref[...] = jnp.zeros_like(dn_ref)

        dh = _dot(dp_ref[...], w_ref[...])
        xv = x_ref[...]
        r = lax.rsqrt(jnp.mean(xv * xv, axis=-1, keepdims=True) + RMS_EPS)
        xh = xv * r
        dn_ref[...] += jnp.sum(dh * xh, axis=0, keepdims=True)
        dhn = dh * n_ref[...]
        dx = dx2_ref[...] + r * (dhn - xh * jnp.mean(dhn * xh, axis=-1, keepdims=True))
        dx_ref[...] = dx
        dxb_ref[...] = (0.5 * dx).astype(BF16)

    tile = lambda w: pl.BlockSpec((TM, w), lambda i: (i, 0))
    return _pallas(
        body, name, (t // TM,),
        [tile(dp.shape[1]), tile(d), tile(d), pl.BlockSpec((1, d), lambda i: (0, 0)),
         pl.BlockSpec(w_in_t.shape, lambda i: (0, 0))],
        [tile(d), pl.BlockSpec((1, d), lambda i: (0, 0)), tile(d)],
        [jax.ShapeDtypeStruct((t, d), F32), jax.ShapeDtypeStruct((1, d), F32), jax.ShapeDtypeStruct((t, d), BF16)],
        [], (dp, dx2, x1, n, w_in_t), ride)


def _group_norm(o):
    parts, rstds = [], []
    for h in range(HEADS):
        oh = o[:, h * DV:(h + 1) * DV]
        dlt = oh - jnp.mean(oh, axis=-1, keepdims=True)
        rstd = lax.rsqrt(jnp.mean(dlt * dlt, axis=-1, keepdims=True) + GN_EPS)
        parts.append(dlt * rstd)
        rstds.append(rstd)
    return jnp.concatenate(parts, axis=1), rstds


def _mix_core_fwd(qs, k, v, g, u, x1, consts, gain, wp, scale, w_out, nseq, seq, name):
    t, d = x1.shape
    nb = min(MIX_BLOCKS, seq // BLK)
    nstep = seq // (nb * BLK)
    mask, dq, dk, gbd, bd = consts

    def body(q_ref, k_ref, v_ref, g_ref, u_ref, x1_ref, m_ref, dq_ref, dk_ref, gbd_ref, bd_ref, gain_ref, wp_ref,
             sc_ref, wo_ref, x2_ref, mix_ref, o_ref, pooled_ref, st_ref, state, halo):
        j = pl.program_id(1)

        @pl.when(j == 0)
        def _():
            state[...] = jnp.zeros_like(state)
            halo[...] = jnp.zeros_like(halo)

        for s in range(nb):
            rows = pl.ds(s * BLK, BLK)
            qv, kv, vv = q_ref[rows, :], k_ref[rows, :], v_ref[rows, :]
            st = state[...]
            st_ref[s] = st
            cross = _dot(qv, st.astype(BF16)) * dq_ref[...]
            outs = []
            for h in range(HEADS):
                qh = jnp.where(_head_mask(h), qv, jnp.zeros_like(qv))
                am = (_dot_nt(qh, kv) * m_ref[h]).astype(BF16)
                outs.append(_dot(am, vv[:, h * DV:(h + 1) * DV]))
            o = jnp.concatenate(outs, axis=1) + cross
            o_ref[rows, :] = o
            kd = (kv.astype(F32) * dk_ref[...]).astype(BF16)
            state[...] = gbd_ref[...] * st + _dot_tn(kd, vv) * bd_ref[...]

            gv = g_ref[rows, :]
            nrm, _ = _group_norm(o)
            ret = (gv * _sigmoid(gv)) * (nrm * gain_ref[...])

            uv = u_ref[rows, :]
            c = jnp.concatenate([halo[...], uv], axis=0)
            halo[...] = uv[BLK - HALO:, :]
            pos = (j * nb + s) * BLK + lax.broadcasted_iota(jnp.int32, (BLK, 1), 0)
            parts = []
            for gi, w in enumerate(WINDOWS):
                c = c + pltpu.roll(c, w // 2, 0)
                cnt = jnp.minimum(pos + 1, w).astype(F32)
                parts.append(c[HALO:, :GC] / cnt)
                if gi + 1 < len(WINDOWS):
                    c = c[:, GC:]
            pooled = (jnp.concatenate(parts, axis=1) - uv).astype(BF16)
            pooled_ref[rows, :] = pooled
            z = jnp.concatenate([_dot(pooled[:, gi * GC:(gi + 1) * GC], wp_ref[gi]) for gi in range(len(WINDOWS))],
                                axis=1)
            mix = jnp.concatenate([ret, z * sc_ref[...]], axis=1).astype(BF16)
            mix_ref[rows, :] = mix
            x2_ref[rows, :] = x1_ref[rows, :] + _dot(mix, wo_ref[...])

    blk = lambda w: pl.BlockSpec((nb * BLK, w), lambda i, j: (i * nstep + j, 0))
    full = lambda a: pl.BlockSpec(a.shape, lambda i, j: (0,) * a.ndim)
    return _pallas(
        body, name, (nseq, nstep),
        [blk(QK_W), blk(QK_W), blk(V_W), blk(V_W), blk(POOL_W), blk(d),
         full(mask), full(dq), full(dk), full(gbd), full(bd), full(gain), full(wp), full(scale), full(w_out)],
        [blk(d), blk(d), blk(V_W), blk(POOL_W), pl.BlockSpec((nb, QK_W, V_W), lambda i, j: (i * nstep + j, 0, 0))],
        [jax.ShapeDtypeStruct((t, d), F32), jax.ShapeDtypeStruct((t, d), BF16),
         jax.ShapeDtypeStruct((t, V_W), F32), jax.ShapeDtypeStruct((t, POOL_W), BF16),
         jax.ShapeDtypeStruct((t // BLK, QK_W, V_W), F32)],
        [pltpu.VMEM((QK_W, V_W), F32), pltpu.VMEM((HALO, POOL_W), F32)],
        (qs, k, v, g, u, x1, mask, dq, dk, gbd, bd, gain, wp, scale, w_out))


def _mix_core_bwd(dx2, qs, k, v, g, o, pooled, st, consts, gain, wp, scale, w_out, cos_t, sin_t, nseq, seq, name,
                  ride=None):
    t, d = dx2.shape
    nb = min(MIX_BLOCKS, seq // BLK)
    nstep = seq // (nb * BLK)
    mask, dq, dk, gbd, bd = consts
    n_win = len(WINDOWS)

    def body(dx2_ref, q_ref, k_ref, v_ref, g_ref, o_ref, pooled_ref, st_ref, m_ref, dq_ref, dk_ref, gbd_ref, bd_ref,
             gain_ref, wp_ref, sc_ref, wo_ref, c_ref, s_ref,
             dp_ref, dx2b_ref, dgain_ref, dscale_ref, dwp_ref, rstate, carry):
        i, j = pl.program_id(0), pl.program_id(1)

        @pl.when((i == 0) & (j == 0))
        def _():
            dgain_ref[...] = jnp.zeros_like(dgain_ref)
            dscale_ref[...] = jnp.zeros_like(dscale_ref)
            dwp_ref[...] = jnp.zeros_like(dwp_ref)

        @pl.when(j == 0)
        def _():
            rstate[...] = jnp.zeros_like(rstate)
            carry[...] = jnp.zeros_like(carry)

        for s in reversed(range(nb)):
            rows = pl.ds(s * BLK, BLK)
            dx2b = dx2_ref[rows, :].astype(BF16)
            dx2b_ref[rows, :] = dx2b
            dmix = _dot(dx2b, wo_ref[...])
            dret, dpool = dmix[:, :V_W], dmix[:, V_W:]

            gv, ov, gain_v = g_ref[rows, :], o_ref[rows, :], gain_ref[...]
            sg = _sigmoid(gv)
            sil = gv * sg
            nrm, rstds = _group_norm(ov)
            dg = dret * (nrm * gain_v) * (sg * (1.0 + gv * (1.0 - sg)))
            dgn = dret * sil
            dgain_ref[...] += jnp.sum(dgn * nrm, axis=0, keepdims=True)
            dnrm = dgn * gain_v
            do_parts = []
            for h in range(HEADS):
                dn_h = dnrm[:, h * DV:(h + 1) * DV]
                n_h = nrm[:, h * DV:(h + 1) * DV]
                do_parts.append(rstds[h] * (dn_h - jnp.mean(dn_h, axis=-1, keepdims=True)
                                            - n_h * jnp.mean(dn_h * n_h, axis=-1, keepdims=True)))
            do = jnp.concatenate(do_parts, axis=1)
            dob = do.astype(BF16)

            qv, kv, vv = q_ref[rows, :], k_ref[rows, :], v_ref[rows, :]
            stb = st_ref[s].astype(BF16)
            rs = rstate[...]
            rsb = rs.astype(BF16)
            dod = (do * dq_ref[...]).astype(BF16)
            dqs = _dot_nt(dod, stb)
            dst = _dot_tn(qv, dod) * bd_ref[...]
            dkf = dk_ref[...]
            kd = (kv.astype(F32) * dkf).astype(BF16)
            dks = _dot_nt(vv, rsb) * dkf
            dvs = _dot(kd, rsb)
            dv_parts = []
            for h in range(HEADS):
                hm = _head_mask(h)
                qh = jnp.where(hm, qv, jnp.zeros_like(qv))
                kh = jnp.where(hm, kv, jnp.zeros_like(kv))
                mh = m_ref[h]
                am = (_dot_nt(qh, kv) * mh).astype(BF16)
                dpm = (_dot_nt(dob[:, h * DV:(h + 1) * DV], vv[:, h * DV:(h + 1) * DV]) * mh).astype(BF16)
                dqs = dqs + _dot(dpm, kh)
                dks = dks + _dot_tn(dpm, qh)
                dv_parts.append(_dot_tn(am, dob[:, h * DV:(h + 1) * DV]))
            dvs = dvs + jnp.concatenate(dv_parts, axis=1)
            rstate[...] = dst + gbd_ref[...] * rs

            cv, sv = c_ref[rows, :], s_ref[rows, :]
            dqr = dqs * (DK ** -0.5)
            dq_pre = dqr * cv + _swap_halves(dqr * sv)
            dk_pre = dks * cv + _swap_halves(dks * sv)

            pv = pooled_ref[rows, :]
            sc = sc_ref[...]
            dzb = (dpool * sc).astype(BF16)
            z_parts, dpo_parts = [], []
            for gi in range(n_win):
                p_g = pv[:, gi * GC:(gi + 1) * GC]
                dz_g = dzb[:, gi * GC:(gi + 1) * GC]
                z_parts.append(_dot(p_g, wp_ref[gi]))
                dwp_ref[gi] += _dot_tn(p_g, dz_g)
                dpo_parts.append(_dot_nt(dz_g, wp_ref[gi]))
            dscale_ref[...] += jnp.sum(dpool * jnp.concatenate(z_parts, axis=1), axis=0, keepdims=True)
            dpo = jnp.concatenate(dpo_parts, axis=1)
            pos = ((nstep - 1 - j) * nb + s) * BLK + lax.broadcasted_iota(jnp.int32, (BLK, 1), 0)
            e = jnp.concatenate(
                [dpo[:, gi * GC:(gi + 1) * GC] / jnp.minimum(pos + 1, w).astype(F32) for gi, w in enumerate(WINDOWS)],
                axis=1)
            c = jnp.concatenate([e, carry[...]], axis=0)
            carry[...] = e[:HALO, :]
            span = BLK + HALO
            lead = []
            for gi, w in enumerate(WINDOWS):
                c = c + pltpu.roll(c, span - w // 2, 0)
                lead.append(c[:BLK, :GC])
                if gi + 1 < n_win:
                    c = c[:, GC:]
            du = jnp.concatenate(lead, axis=1) - dpo

            dp_ref[rows, 0:QK_W] = dq_pre.astype(BF16)
            dp_ref[rows, QK_W:2 * QK_W] = dk_pre.astype(BF16)
            dp_ref[rows, 2 * QK_W:2 * QK_W + V_W] = dvs.astype(BF16)
            dp_ref[rows, 2 * QK_W + V_W:2 * QK_W + 2 * V_W] = dg.astype(BF16)
            dp_ref[rows, 2 * QK_W + 2 * V_W:] = du.astype(BF16)

    rev = lambda i, j: i * nstep + (nstep - 1 - j)
    blk = lambda w: pl.BlockSpec((nb * BLK, w), lambda i, j: (rev(i, j), 0))
    full = lambda a: pl.BlockSpec(a.shape, lambda i, j: (0,) * a.ndim)
    in_w = 2 * QK_W + 2 * V_W + POOL_W
    return _pallas(
        body, name, (nseq, nstep),
        [blk(d), blk(QK_W), blk(QK_W), blk(V_W), blk(V_W), blk(V_W), blk(POOL_W),
         pl.BlockSpec((nb, QK_W, V_W), lambda i, j: (rev(i, j), 0, 0)),
         full(mask), full(dq), full(dk), full(gbd), full(bd), full(gain), full(wp), full(scale), full(w_out),
         pl.BlockSpec((nb * BLK, QK_W), lambda i, j: (nstep - 1 - j, 0)),
         pl.BlockSpec((nb * BLK, QK_W), lambda i, j: (nstep - 1 - j, 0))],
        [blk(in_w), blk(d), pl.BlockSpec((1, V_W), lambda i, j: (0, 0)),
         pl.BlockSpec((1, POOL_W), lambda i, j: (0, 0)), pl.BlockSpec((n_win, GC, GC), lambda i, j: (0, 0, 0))],
        [jax.ShapeDtypeStruct((t, in_w), BF16), jax.ShapeDtypeStruct((t, d), BF16),
         jax.ShapeDtypeStruct((1, V_W), F32), jax.ShapeDtypeStruct((1, POOL_W), F32),
         jax.ShapeDtypeStruct((n_win, GC, GC), F32)],
        [pltpu.VMEM((QK_W, V_W), F32), pltpu.VMEM((HALO, POOL_W), F32)],
        (dx2, qs, k, v, g, o, pooled, st, mask, dq, dk, gbd, bd, gain, wp, scale, w_out, cos_t, sin_t), ride)


def _loss_head(x3, nf, tgt, name):
    t, d = x3.shape

    def body(x_ref, n_ref, t_ref, dx_ref, dn_ref, loss_ref, dxb_ref):
        @pl.when(pl.program_id(0) == 0)
        def _():
            dn_ref[...] = jnp.zeros_like(dn_ref)
            loss_ref[...] = jnp.zeros_like(loss_ref)

        xv = x_ref[...]
        nv = n_ref[...]
        r = lax.rsqrt(jnp.mean(xv * xv, axis=-1, keepdims=True) + RMS_EPS)
        xh = xv * r
        err = xh * nv - t_ref[...]
        row = jnp.mean(err * err, axis=-1, keepdims=True)
        loss_ref[...] += 0.5 * jnp.sum(row, axis=0, keepdims=True)
        dy = err * (1.0 / d)
        dn_ref[...] += jnp.sum(dy * xh, axis=0, keepdims=True)
        dxh = dy * nv
        dx = r * (dxh - xh * jnp.mean(dxh * xh, axis=-1, keepdims=True))
        dx_ref[...] = dx
        dxb_ref[...] = (0.5 * dx).astype(BF16)

    tile = pl.BlockSpec((TM, d), lambda i: (i, 0))
    return pl.pallas_call(
        body, name=name, grid=(t // TM,),
        in_specs=[tile, pl.BlockSpec((1, d), lambda i: (0, 0)), tile],
        out_specs=[tile, pl.BlockSpec((1, d), lambda i: (0, 0)), pl.BlockSpec((1, 1), lambda i: (0, 0)), tile],
        out_shape=[jax.ShapeDtypeStruct((t, d), F32), jax.ShapeDtypeStruct((1, d), F32), jax.ShapeDtypeStruct((1, 1), F32),
                   jax.ShapeDtypeStruct((t, d), BF16)],
        compiler_params=_cparams(1),
    )(x3, nf, tgt)


def _coords():
    return lax.axis_index("x"), lax.axis_index("y"), lax.axis_index("c")


def _window(ref, kind, idx, size):
    if kind == "col":
        return ref.at[:, pl.ds(pl.multiple_of(idx * size, LANE), size)]
    return ref.at[pl.ds(pl.multiple_of(idx * size, 8), size), :]


def _join(exchanges):
    bounds = []
    i0 = o0 = s0 = 0
    for ex in exchanges:
        bounds.append((i0, o0, s0))
        i0, o0, s0 = i0 + len(ex.inputs), o0 + len(ex.out_shape), s0 + len(ex.scratch)

    def phase(which):
        def run(ins, outs, sems):
            for ex, (i, o, s) in zip(exchanges, bounds):
                fn = getattr(ex, which)
                if fn is not None:
                    fn(ins[i:i + len(ex.inputs)], outs[o:o + len(ex.out_shape)], sems[s:s + len(ex.scratch)])
        return run

    return _Exchange(sum((ex.inputs for ex in exchanges), []), sum((ex.out_shape for ex in exchanges), []),
                     sum((ex.scratch for ex in exchanges), []), phase("start"), phase("finish"),
                     phase("mid") if any(ex.mid is not None for ex in exchanges) else None)


def _gather_exchange(parts):
    n = len(parts)
    kinds = [kd for _, kd in parts]
    sizes = [a.shape[1] if kd == "col" else a.shape[0] for a, kd in parts]

    def plan(ins, outs, sems):
        send_sems, recv_sems, local_sems = sems
        x, y, c = _coords()
        me, sibling = (x, y, c), (x, y, 1 - c)
        chips = [(1 - x, y), (x, 1 - y), (1 - x, 1 - y)]

        def win(p, dev):
            return _window(outs[p], kinds[p], 4 * dev[0] + 2 * dev[1] + dev[2], sizes[p])

        def copy(p, k, block, to, src=None):
            return pltpu.make_async_remote_copy(
                src_ref=win(p, block) if src is None else src, dst_ref=win(p, block),
                send_sem=send_sems.at[p * 7 + k], recv_sem=recv_sems.at[p * 7 + k], device_id=to, device_id_type=MESH_ID)

        mine = [pltpu.make_async_copy(ins[p], win(p, me), local_sems.at[p]) for p in range(n)]
        first, arrived, passed, rest = [], [], [], []
        for p in range(n):
            first.append(copy(p, 0, me, sibling, src=ins[p]))
            first += [copy(p, 1 + q, me, (*chip, c), src=ins[p]) for q, chip in enumerate(chips)]
            rest.append(copy(p, 0, sibling, me))
            rest += [copy(p, 4 + q, (*chip, 1 - c), me) for q, chip in enumerate(chips)]
        for q, chip in enumerate(chips):
            for p in range(n):
                arrived.append(copy(p, 1 + q, (*chip, c), me))
                passed.append(copy(p, 4 + q, (*chip, c), sibling))
        return mine, first, arrived, passed, rest

    def start(ins, outs, sems):
        mine, first, _, _, _ = plan(ins, outs, sems)
        for cp in mine + first:
            cp.start()

    def mid(ins, outs, sems):
        _, _, arrived, passed, _ = plan(ins, outs, sems)
        for got, fwd in zip(arrived, passed):
            got.wait_recv()
            fwd.start()

    def finish(ins, outs, sems):
        mine, first, _, passed, rest = plan(ins, outs, sems)
        for cp in rest:
            cp.wait_recv()
        for cp in first + passed:
            cp.wait_send()
        for cp in mine:
            cp.wait()

    out_shape = [jax.ShapeDtypeStruct((a.shape[0], N_DEV * a.shape[1]) if kd == "col" else (N_DEV * a.shape[0], a.shape[1]),
                                      a.dtype) for a, kd in parts]
    scratch = [pltpu.SemaphoreType.DMA((7 * n,)), pltpu.SemaphoreType.DMA((7 * n,)), pltpu.SemaphoreType.DMA((n,))]
    return _Exchange([a for a, _ in parts], out_shape, scratch, start, finish, mid)


def _shard_shape(a, kd):
    return (a.shape[0], a.shape[1] // N_DEV) if kd == "col" else (a.shape[0] // N_DEV, a.shape[1])


def _symmetric_exchange(inputs, out_shape, n_copies, plan):
    def start(ins, outs, sems):
        for cp in plan(ins, outs, sems):
            cp.start()

    def finish(ins, outs, sems):
        copies = plan(ins, outs, sems)
        for cp in copies:
            cp.wait_recv()
        for cp in copies:
            cp.wait_send()

    scratch = [pltpu.SemaphoreType.DMA((n_copies,)), pltpu.SemaphoreType.DMA((n_copies,))]
    return _Exchange(inputs, out_shape, scratch, start, finish)


def _rs_pair_exchange(grads):
    n = len(grads)
    kinds = [kd for _, kd in grads]
    shapes = [_shard_shape(a, kd) for a, kd in grads]

    def plan(ins, outs, sems):
        send_sems, recv_sems = sems
        x, y, c = _coords()
        copies = []
        for p in range(n):
            size = shapes[p][1] if kinds[p] == "col" else shapes[p][0]
            for s in range(4):
                src = _window(ins[p], kinds[p], 2 * s + (1 - c), size)
                copies.append(pltpu.make_async_remote_copy(
                    src_ref=src, dst_ref=outs[p].at[s], send_sem=send_sems.at[4 * p + s], recv_sem=recv_sems.at[4 * p + s],
                    device_id=(x, y, 1 - c), device_id_type=MESH_ID))
        return copies

    return _symmetric_exchange([a for a, _ in grads], [jax.ShapeDtypeStruct((4,) + shapes[p], BF16) for p in range(n)],
                               4 * n, plan)


def _rs_chips_exchange(sums):
    n = len(sums)

    def plan(ins, outs, sems):
        send_sems, recv_sems = sems
        x, y, c = _coords()
        chips = [(1 - x, y), (x, 1 - y), (1 - x, 1 - y)]
        copies = []
        for p in range(n):
            for q, (cx, cy) in enumerate(chips):
                copies.append(pltpu.make_async_remote_copy(
                    src_ref=ins[p].at[2 * cx + cy], dst_ref=outs[p].at[q],
                    send_sem=send_sems.at[3 * p + q], recv_sem=recv_sems.at[3 * p + q],
                    device_id=(cx, cy, c), device_id_type=MESH_ID))
        return copies

    return _symmetric_exchange(list(sums), [jax.ShapeDtypeStruct((3,) + a.shape[1:], BF16) for a in sums], 3 * n, plan)


def _pair_sum(grad, kd, recv, core, name):
    _, r, cw = recv.shape
    tr = min(r, TM)

    def body(core_ref, g_ref, r_ref, o_ref):
        del core_ref
        o_ref[0] = (g_ref[...].astype(F32) + r_ref[0].astype(F32)).astype(BF16)

    if kd == "col":
        g_spec = pl.BlockSpec((tr, cw), lambda s, i, core_ref: (i, 2 * s + core_ref[0]))
    else:
        g_spec = pl.BlockSpec((tr, cw), lambda s, i, core_ref: ((2 * s + core_ref[0]) * (r // tr) + i, 0))
    grid_spec = pltpu.PrefetchScalarGridSpec(
        num_scalar_prefetch=1, grid=(4, r // tr),
        in_specs=[g_spec, pl.BlockSpec((1, tr, cw), lambda s, i, core_ref: (s, i, 0))],
        out_specs=pl.BlockSpec((1, tr, cw), lambda s, i, core_ref: (s, i, 0)))
    return pl.pallas_call(
        body, name=name, grid_spec=grid_spec, out_shape=jax.ShapeDtypeStruct(recv.shape, BF16),
        compiler_params=_cparams(2),
    )(core, grad, recv)


def _adam_math(w, g, m, v):
    m2 = B1 * m + (1.0 - B1) * g
    v2 = B2 * v + (1.0 - B2) * (g * g)
    m_hat = m2 / (1.0 - B1 ** STEP)
    v_hat = v2 / (1.0 - B2 ** STEP)
    delta = -LR * (m_hat / (jnp.sqrt(v_hat) + ADAM_EPS) + WD * w)
    return delta, m2, v2


def _chip_sum_adam(items, chip, tr, name, ride=None):
    r = items[0][1].shape[0]
    steps = r // tr
    n_parts = [len(parts) for parts, _, _, _ in items]
    r_in = 0 if ride is None else len(ride.inputs)
    r_out = 0 if ride is None else len(ride.out_shape)
    n_in = sum(2 * k + 3 for k in n_parts)
    n_out = 4 * len(items)

    def body(chip_ref, *refs):
        del chip_ref
        ins, refs = refs[:n_in], refs[n_in:]
        r_ins, refs = refs[:r_in], refs[r_in:]
        outs, refs = refs[:n_out], refs[n_out:]
        r_outs, sems = refs[:r_out], refs[r_out:]
        i = pl.program_id(0)
        if ride is not None:
            @pl.when(i == 0)
            def _():
                ride.start(r_ins, r_outs, sems)

        pos = 0
        for q, (k, (_, w, _, _)) in enumerate(zip(n_parts, items)):
            cols = []
            for _ in range(k):
                p_ref, c_ref = ins[pos], ins[pos + 1]
                pos += 2
                cols.append(p_ref[0].astype(F32) + c_ref[0].astype(F32) + c_ref[1].astype(F32) + c_ref[2].astype(F32))
            g = (cols[0] if k == 1 else jnp.concatenate(cols, axis=1))[:, :w.shape[1]]
            w_ref, m_ref, v_ref = ins[pos:pos + 3]
            pos += 3
            delta, m2, v2 = _adam_math(w_ref[...], g, m_ref[...], v_ref[...])
            outs[4 * q][...] = g
            outs[4 * q + 1][...] = delta
            outs[4 * q + 2][...] = m2
            outs[4 * q + 3][...] = v2

        if ride is not None:
            @pl.when(i == steps - 1)
            def _():
                ride.finish(r_ins, r_outs, sems)

    in_specs, args, out_specs, out_shape = [], [], [], []
    for parts, w, m, v in items:
        for psum, recv in parts:
            pc = psum.shape[2]
            in_specs += [pl.BlockSpec((1, tr, pc), lambda i, chip_ref: (chip_ref[0], i, 0)),
                         pl.BlockSpec((3, tr, pc), lambda i, chip_ref: (0, i, 0))]
            args += [psum, recv]
        loc = pl.BlockSpec((tr, w.shape[1]), lambda i, chip_ref: (i, 0))
        in_specs += [loc] * 3
        args += [w, m, v]
        out_specs += [loc] * 4
        out_shape += [jax.ShapeDtypeStruct(w.shape, F32)] * 4
    grid_spec = pltpu.PrefetchScalarGridSpec(
        num_scalar_prefetch=1, grid=(steps,), in_specs=in_specs + [ANY] * r_in, out_specs=out_specs + [ANY] * r_out,
        scratch_shapes=[] if ride is None else ride.scratch)
    res = pl.pallas_call(
        body, name=name, grid_spec=grid_spec, out_shape=out_shape + ([] if ride is None else ride.out_shape),
        compiler_params=_cparams(1),
    )(chip, *args, *([] if ride is None else ride.inputs))
    return res if ride is None else (res[:n_out], res[n_out:])


def _small_allreduce_adam(partials, params, moms, vels, plain, name, ride=None):
    n, n_plain = len(partials), len(plain)
    summed = list(partials) + list(plain)
    row0 = []
    rows = 0
    for a in summed:
        if a.shape[0] >= 8:
            rows = _pad_to(rows, 8)
        row0.append(rows)
        rows += a.shape[0]
    rows = _pad_to(rows, 8)
    width = max(a.shape[1] for a in summed)
    r_in = 0 if ride is None else len(ride.inputs)
    r_out = 0 if ride is None else len(ride.out_shape)
    n_out = 4 * n + n_plain

    def body(*refs):
        w_in, m_in, v_in = refs[0:n], refs[n:2 * n], refs[2 * n:3 * n]
        g_in, refs = refs[3 * n:4 * n + n_plain], refs[4 * n + n_plain:]
        r_ins, refs = refs[:r_in], refs[r_in:]
        outs, refs = refs[:n_out], refs[n_out:]
        r_outs, refs = refs[:r_out], refs[r_out:]
        pair, chips, send_sems, recv_sems = refs[:4]
        if ride is not None:
            ride.start(r_ins, r_outs, refs[4:])
        x, y, c = _coords()
        chip = 2 * x + y
        pair[c] = jnp.zeros((rows, width), F32)
        for p, a in enumerate(summed):
            r, cw = a.shape
            pair[c, row0[p]:row0[p] + r, 0:cw] = g_in[p][...]
        swap = pltpu.make_async_remote_copy(src_ref=pair.at[c], dst_ref=pair.at[c], send_sem=send_sems.at[0],
                                            recv_sem=recv_sems.at[0], device_id=(x, y, 1 - c), device_id_type=MESH_ID)
        swap.start()
        swap.wait_recv()
        swap.wait_send()
        chips[chip] = pair[0] + pair[1]
        copies = [pltpu.make_async_remote_copy(
            src_ref=chips.at[chip], dst_ref=chips.at[chip], send_sem=send_sems.at[1 + q], recv_sem=recv_sems.at[1 + q],
            device_id=(cx, cy, c), device_id_type=MESH_ID) for q, (cx, cy) in enumerate([(1 - x, y), (x, 1 - y), (1 - x, 1 - y)])]
        for cp in copies:
            cp.start()
        for cp in copies:
            cp.wait_recv()
        for cp in copies:
            cp.wait_send()
        for p, a in enumerate(summed):
            r, cw = a.shape
            g = chips[0, row0[p]:row0[p] + r, 0:cw]
            for q in range(1, 4):
                g = g + chips[q, row0[p]:row0[p] + r, 0:cw]
            if p >= n:
                outs[4 * n + p - n][...] = g
                continue
            delta, m2, v2 = _adam_math(w_in[p][...], g, m_in[p][...], v_in[p][...])
            outs[4 * p][...] = g
            outs[4 * p + 1][...] = delta
            outs[4 * p + 2][...] = m2
            outs[4 * p + 3][...] = v2
        if ride is not None:
            ride.finish(r_ins, r_outs, refs[4:])

    out_shape = []
    for a in partials:
        out_shape += [jax.ShapeDtypeStruct(a.shape, F32)] * 4
    out_shape += [jax.ShapeDtypeStruct(a.shape, F32) for a in plain]
    res = pl.pallas_call(
        body, name=name, in_specs=[VMEM_SPEC] * (4 * n + n_plain) + [ANY] * r_in,
        out_specs=[VMEM_SPEC] * n_out + [ANY] * r_out, out_shape=out_shape + ([] if ride is None else ride.out_shape),
        scratch_shapes=[pltpu.VMEM((2, rows, width), F32), pltpu.VMEM((4, rows, width), F32),
                        pltpu.SemaphoreType.DMA((4,)), pltpu.SemaphoreType.DMA((4,))] + ([] if ride is None else ride.scratch),
    )(*params, *moms, *vels, *partials, *plain, *([] if ride is None else ride.inputs))
    return res if ride is None else (res[:n_out], res[n_out:])


def _local_step(xf, tgt, nseq, seq, later, small_w, core, small_step):
    d = xf.shape[1]
    n1, n2, gain, pool_w, pool_scale, n3, nf = small_w
    consts = _retention_constants()
    cos_t, sin_t = _rotary_tables(seq)
    wp_b = pool_w.astype(BF16)

    def pair_sums(grads, recv, names):
        return [_pair_sum(g, kd, r, core, "pair_sum_" + nm) for (g, kd), r, nm in zip(grads, recv, names)]

    def riding(host):
        return _gather_exchange(later[host])

    both = lambda first, second: _join([_rs_chips_exchange(first), _rs_pair_exchange([second])])

    (h1,), (cols1_all,) = _norm(xf, n1, "norm1", ride=riding("norm1"))
    tf = 2 * cols1_all.shape[1] // N_DEV
    (b1, sil1, dsil1, s1), (d1_all, win_all, wout_all, gate2_all, up2_all) = _ffn_act(
        h1, (cols1_all, 0), (cols1_all, 1), "ffn1_act", ride=riding("ffn1_act"))
    x1, (d2_all,) = _ffn_down(s1, xf, d1_all, "ffn1_down", ride=riding("ffn1_down"))
    h2, qs, kr, vv, gg, uu = _mix_in(x1, n2, win_all, cos_t, sin_t, seq, "mix_in")
    x2, mix, oo, pooled, states = _mix_core_fwd(qs, kr, vv, gg, uu, x1, consts, gain, wp_b, pool_scale, wout_all,
                                                 nseq, seq, "mix_core_fwd")
    (x3, h3, b3, sil3, dsil3), (cols2_t, d2_t, win_t, wout_t) = _ffn_fwd(
        x2, n3, (gate2_all, 0), (up2_all, 0), d2_all, "ffn2_fwd", ride=riding("ffn2_fwd"))
    dx3, dnf, loss_part, dx3b = _loss_head(x3, nf, tgt, "loss_head")
    out = {}

    (da3, db3, g_wd2), (cols1_t, d1_t) = _ffn_bwd_act(dx3b, b3, sil3, dsil3, d2_t, "ffn2_bwd_act",
                                                      ride=riding("ffn2_bwd_act"))
    names2 = ["ffn2_gate", "ffn2_up", "ffn2_down"]
    grads2 = [(_wgrad(da3, h3, 1.0, tf, d, "wgrad_gate2"), "row"), (_wgrad(db3, h3, 1.0, tf, d, "wgrad_up2"), "row"),
              (g_wd2, "row")]
    (dx2, dn3), recv2 = _ffn_bwd_in(da3, db3, dx3, x2, n3, cols2_t, 0, "ffn2_bwd_in", ride=_rs_pair_exchange(grads2))
    sums2 = pair_sums(grads2, recv2, names2)
    (dp, dx2b, dgain, dscale, dwp), crecv2 = _mix_core_bwd(
        dx2, qs, kr, vv, gg, oo, pooled, states, consts, gain, wp_b, pool_scale, wout_t, cos_t, sin_t, nseq, seq,
        "mix_core_bwd", ride=_rs_chips_exchange(sums2))
    out.update({nm: [(s, r)] for nm, s, r in zip(names2, sums2, crecv2)})

    names_m = ["w_in", "w_out"]
    grads_m = [(_wgrad(h2, dp, 1.0, d, d, "wgrad_in"), "col"), (_wgrad(mix, dx2b, 1.0, d, d, "wgrad_out"), "row")]
    (dx1, dn2, dx1b), recv_m = _mix_in_bwd(dp, dx2, x1, n2, win_t, "mix_in_bwd", ride=_rs_pair_exchange(grads_m))
    sums_m = pair_sums(grads_m, recv_m, names_m)
    (da1, db1, g_wd1), crecv_m = _ffn_bwd_act(dx1b, b1, sil1, dsil1, d1_t, "ffn1_bwd_act", ride=_rs_chips_exchange(sums_m))
    out.update({nm: [(s, r)] for nm, s, r in zip(names_m, sums_m, crecv_m)})

    dx0, dn1 = _ffn_bwd_in(da1, db1, dx1, xf, n1, cols1_t, 0, "ffn1_bwd_in")
    g_down = (g_wd1, "row")
    g_gate, recv_d = _wgrad(da1, h1, 1.0, tf, d, "wgrad_gate1", ride=_rs_pair_exchange([g_down]))
    g_gate = (g_gate, "row")
    sum_d = pair_sums([g_down], recv_d, ["ffn1_down"])
    g_lo, (crecv_d, recv_g) = _wgrad(db1, h1, 1.0, tf, d // 2, "wgrad_up1_lo", ride=both(sum_d, g_gate), b_cols=(0, 1))
    g_lo = (g_lo, "row")
    sum_g = pair_sums([g_gate], [recv_g], ["ffn1_gate"])
    g_hi, (crecv_g, recv_lo) = _wgrad(db1, h1, 1.0, tf, d // 2, "wgrad_up1_hi", ride=both(sum_g, g_lo), b_cols=(1, 1))
    g_hi = (g_hi, "row")
    sum_lo = pair_sums([g_lo], [recv_lo], ["ffn1_up_lo"])
    small_out, (crecv_lo, recv_hi) = small_step((dn1, dn2, dgain, dwp, dscale, dn3, dnf), loss_part, both(sum_lo, g_hi))
    sum_hi = pair_sums([g_hi], [recv_hi], ["ffn1_up_hi"])
    out.update({"ffn1_gate": [(sum_g[0], crecv_g)], "ffn1_down": [(sum_d[0], crecv_d)],
                "ffn1_up": [(sum_lo[0], crecv_lo), (sum_hi[0], None)]})
    return small_out[-1], dx0, out, small_out[:-1], _rs_chips_exchange(sum_hi)


def kernel(x, norm_ffn1, ffn1_gate, ffn1_up, ffn1_down, norm_mix, w_in, ret_gn_gain, pool_w, pool_scale, w_out, norm_ffn2, ffn2_gate, ffn2_up, ffn2_down, norm_final, loss_target, m_norm_ffn1, m_ffn1_gate, m_ffn1_up, m_ffn1_down, m_norm_mix, m_w_in, m_ret_gn_gain, m_pool_w, m_pool_scale, m_w_out, m_norm_ffn2, m_ffn2_gate, m_ffn2_up, m_ffn2_down, m_norm_final, v_norm_ffn1, v_ffn1_gate, v_ffn1_up, v_ffn1_down, v_norm_mix, v_w_in, v_ret_gn_gain, v_pool_w, v_pool_scale, v_w_out, v_norm_ffn2, v_ffn2_gate, v_ffn2_up, v_ffn2_down, v_norm_final):
    nseq, seq, d = x.shape
    t = nseq * seq
    f_loc = ffn1_gate.shape[2]
    f_pad = _pad_to(f_loc, LANE)
    xf = x.reshape(t, d)
    tgt = loss_target.reshape(t, d)
    core = lax.axis_index("c").astype(jnp.int32).reshape(1)
    chip = (2 * lax.axis_index("x") + lax.axis_index("y")).astype(jnp.int32).reshape(1)

    colp = lambda w: jnp.pad(w[0].astype(BF16), ((0, 0), (0, f_pad - f_loc)))
    rowp = lambda w: jnp.pad(w[0].astype(BF16), ((0, f_pad - f_loc), (0, 0)))
    gate2, up2 = colp(ffn2_gate), colp(ffn2_up)
    cols1 = jnp.concatenate([colp(ffn1_gate), colp(ffn1_up)], axis=0)
    cols2_t = jnp.concatenate([gate2.T, up2.T], axis=1)
    d1_loc, d2_loc, win_loc, wout_loc = rowp(ffn1_down), rowp(ffn2_down), w_in[0].astype(BF16), w_out[0].astype(BF16)
    later = {"norm1": [(cols1, "col")],
             "ffn1_act": [(d1_loc, "row"), (win_loc, "col"), (wout_loc, "row"), (gate2, "col"), (up2, "col")],
             "ffn1_down": [(d2_loc, "row")],
             "ffn2_fwd": [(cols2_t, "row"), (d2_loc.T, "col"), (win_loc.T, "row"), (wout_loc.T, "col")],
             "ffn2_bwd_act": [(cols1.T, "row"), (d1_loc.T, "col")]}

    flat = lambda a: a.reshape(pool_w.size // d, d)
    params = [norm_ffn1, norm_mix, ret_gn_gain, flat(pool_w), pool_scale, norm_ffn2, norm_final.reshape(1, d)]
    moms = [m_norm_ffn1, m_norm_mix, m_ret_gn_gain, flat(m_pool_w), m_pool_scale, m_norm_ffn2, m_norm_final.reshape(1, d)]
    vels = [v_norm_ffn1, v_norm_mix, v_ret_gn_gain, flat(v_pool_w), v_pool_scale, v_norm_ffn2, v_norm_final.reshape(1, d)]

    def small_step(parts, loss_part, ride):
        dn1, dn2, dgain, dwp, dscale, dn3, dnf = parts
        return _small_allreduce_adam([dn1, dn2, dgain, flat(dwp), dscale, dn3, dnf], params, moms, vels, [loss_part],
                                     "small_allreduce_adam", ride)

    small_w = (norm_ffn1, norm_mix, ret_gn_gain, pool_w[0], pool_scale, norm_ffn2, norm_final.reshape(1, d))
    loss_sum, dx0, reduced, small_out, pending = _local_step(xf, tgt, nseq, seq, later, small_w, core, small_step)

    local = {"ffn1_gate": (ffn1_gate, m_ffn1_gate, v_ffn1_gate), "ffn1_up": (ffn1_up, m_ffn1_up, v_ffn1_up),
             "ffn1_down": (ffn1_down, m_ffn1_down, v_ffn1_down), "w_in": (w_in, m_w_in, v_w_in),
             "w_out": (w_out, m_w_out, v_w_out), "ffn2_gate": (ffn2_gate, m_ffn2_gate, v_ffn2_gate),
             "ffn2_up": (ffn2_up, m_ffn2_up, v_ffn2_up), "ffn2_down": (ffn2_down, m_ffn2_down, v_ffn2_down)}
    flip = lambda nm: nm.endswith("gate") or nm.endswith("up")

    def item(nm):
        view = (lambda a: a[0].T) if flip(nm) else (lambda a: a[0])
        w, m, v = local[nm]
        return reduced[nm], view(w), view(m), view(v)

    big = {}

    def keep(names, res):
        for q, nm in enumerate(names):
            big[nm] = tuple((a.T if flip(nm) else a)[None] for a in res[4 * q:4 * q + 4])

    second = ["ffn2_gate", "ffn2_up", "ffn2_down"]
    res, (last_recv,) = _chip_sum_adam([item(nm) for nm in second], chip, item(second[0])[1].shape[0] // 2, "adam_ffn2",
                                       ride=pending)
    keep(second, res)
    reduced["ffn1_up"][-1] = (reduced["ffn1_up"][-1][0], last_recv)
    first = ["ffn1_gate", "ffn1_up", "ffn1_down"]
    keep(first, _chip_sum_adam([item(nm) for nm in first], chip, item(first[0])[1].shape[0] // 2, "adam_ffn1"))
    for nm in ["w_in", "w_out"]:
        keep([nm], _chip_sum_adam([item(nm)], chip, min(item(nm)[1].shape[0], TM), "adam_" + nm))

    small_names = ["norm_ffn1", "norm_mix", "ret_gn_gain", "pool_w", "pool_scale", "norm_ffn2", "norm_final"]
    shapes = [norm_ffn1.shape, norm_mix.shape, ret_gn_gain.shape, pool_w.shape, pool_scale.shape, norm_ffn2.shape,
              norm_final.shape]
    small = {nm: tuple(small_out[4 * p + q].reshape(shapes[p]) for q in range(4)) for p, nm in enumerate(small_names)}

    loss = loss_sum[0, 0]
    order = ["norm_ffn1", "ffn1_gate", "ffn1_up", "ffn1_down", "norm_mix", "w_in", "ret_gn_gain", "pool_w", "pool_scale",
             "w_out", "norm_ffn2", "ffn2_gate", "ffn2_up", "ffn2_down", "norm_final"]
    both = {**big, **small}
    outs = [loss, dx0.reshape(nseq, seq, d)]
    for q in range(4):
        outs += [both[nm][q] for nm in order]
    return tuple(outs)
```
